```python
import math
import jax, jax.numpy as jnp
from jax import lax
import numpy as np

D_MODEL = 2048
BATCH = 4
SEQ = 2048
DEPTH = 1

D_MIX = D_MODEL
NSA_HEADS = 8
NSA_KV_HEADS = 2
NSA_GROUP = NSA_HEADS // NSA_KV_HEADS
NSA_HEAD_DIM = 128
NSA_WIDTH = NSA_HEADS * NSA_HEAD_DIM
NSA_KV_WIDTH = NSA_KV_HEADS * NSA_HEAD_DIM
CMP_BLOCK = 32
CMP_STRIDE = 16
SLC_BLOCK = 64
SLC_TOP_N = 16
WINDOW = 512
SLC_QBLK = 64
WIN_QBLK = 128
RWKV_WIDTH = D_MIX - NSA_WIDTH
RWKV_HEAD_DIM = 64
RWKV_HEADS = RWKV_WIDTH // RWKV_HEAD_DIM
DECAY_LORA = 64
AAA_LORA = 64
SHIFT_COLS = 3 * RWKV_WIDTH + DECAY_LORA + AAA_LORA
NUM_BUCKETS = 32
MAX_DISTANCE = 1024
NORM_EPS = 1e-6
RWKV_GN_EPS = 64e-5
COL_Q = NSA_WIDTH
COL_KV = 6 * NSA_KV_WIDTH
COL_GATE = 3 * NSA_HEADS
COL_ZA = NSA_WIDTH
COL_ZB = RWKV_WIDTH
IN_COLS = COL_Q + COL_KV + COL_GATE + COL_ZA + SHIFT_COLS + COL_ZB

kernel_name = 'hybrid_nsa_rwkv7_block'


def rms_norm(x, g):
    xf = x.astype(jnp.float32)
    y = xf * lax.rsqrt(jnp.mean(xf * xf, axis=-1, keepdims=True) + NORM_EPS)
    return (y * g.astype(jnp.float32)).astype(x.dtype)


def rel_bucket(dist):
    n = jnp.maximum(dist, 0)
    max_exact = NUM_BUCKETS // 2
    nf = jnp.maximum(n, 1).astype(jnp.float32)
    large = max_exact + (jnp.log(nf / max_exact) / math.log(MAX_DISTANCE / max_exact)
                         * (NUM_BUCKETS - max_exact)).astype(jnp.int32)
    large = jnp.minimum(large, NUM_BUCKETS - 1)
    return jnp.where(n < max_exact, n, large)


def masked_softmax(logits, mask, axis):
    logits = jnp.where(mask, logits, -jnp.inf)
    m = jnp.max(logits, axis=axis, keepdims=True)
    m = jnp.where(jnp.isfinite(m), m, 0.0)
    e = jnp.where(mask, jnp.exp(logits - m), 0.0)
    s = jnp.sum(e, axis=axis, keepdims=True)
    return e / jnp.maximum(s, 1e-30)


def compress(kv, pos, w1, w2):
    B, S, G, D = kv.shape
    n_cmp = (S - CMP_BLOCK) // CMP_STRIDE + 1
    idx = jnp.arange(n_cmp)[:, None] * CMP_STRIDE + jnp.arange(CMP_BLOCK)[None, :]
    blocks = kv[:, idx] + pos[None, None, :, None, :]
    flat = blocks.transpose(0, 1, 3, 2, 4).reshape(B, n_cmp, G, CMP_BLOCK * D)
    return jax.nn.silu(flat @ w1) @ w2


def nsa_mixer(q, k_cmp, v_cmp, k_slc, v_slc, k_win, v_win, gates, rel_bias_table,
              cmp_pos_k, cmp_pos_v, cmp_k_w1, cmp_k_w2, cmp_v_w1, cmp_v_w2):
    f32 = jnp.float32
    B, S = q.shape[:2]
    G, HG, D = NSA_KV_HEADS, NSA_GROUP, NSA_HEAD_DIM
    qf = q.astype(f32).reshape(B, S, G, HG, D) * (D ** -0.5)
    t = jnp.arange(S)
    table = rel_bias_table.astype(f32)

    kc = compress(k_cmp.astype(f32).reshape(B, S, G, D), cmp_pos_k.astype(f32),
                  cmp_k_w1.astype(f32), cmp_k_w2.astype(f32))
    vc = compress(v_cmp.astype(f32).reshape(B, S, G, D), cmp_pos_v.astype(f32),
                  cmp_v_w1.astype(f32), cmp_v_w2.astype(f32))
    n_cmp = kc.shape[1]
    blk_end = jnp.arange(n_cmp) * CMP_STRIDE + CMP_BLOCK - 1
    dist_c = t[:, None] - blk_end[None, :]
    mask_c = dist_c >= 0
    bias_c = table[rel_bucket(dist_c)].reshape(S, n_cmp, G, HG).transpose(2, 3, 0, 1)
    logits_c = jnp.einsum('bsghd,bigd->bghsi', qf, kc) + bias_c
    p_c = masked_softmax(logits_c, mask_c, -1)
    o_c = jnp.einsum('bghsi,bigd->bsghd', p_c, vc)

    n_blk = S // SLC_BLOCK
    cmp_start = jnp.arange(n_cmp) * CMP_STRIDE
    slc_start = jnp.arange(n_blk) * SLC_BLOCK
    overlap = ((cmp_start[:, None] < slc_start[None, :] + SLC_BLOCK)
               & (cmp_start[:, None] + CMP_BLOCK > slc_start[None, :])).astype(f32)
    imp = jnp.einsum('bghsi,ij->bgsj', p_c, overlap)
    cur = t // SLC_BLOCK
    j = jnp.arange(n_blk)
    forced = (j[None, :] == 0) | (j[None, :] == cur[:, None]) | (j[None, :] == cur[:, None] - 1)
    causal_blk = j[None, :] <= cur[:, None]
    score = jnp.where(forced, jnp.inf, jnp.where(causal_blk, imp, -jnp.inf))
    n_sel = min(SLC_TOP_N, n_blk)
    _, sel_idx = lax.top_k(score, n_sel)

    kb = k_slc.astype(f32).reshape(B, n_blk, SLC_BLOCK, G, D).transpose(0, 3, 1, 2, 4)
    vb = v_slc.astype(f32).reshape(B, n_blk, SLC_BLOCK, G, D).transpose(0, 3, 1, 2, 4)
    nqb = S // SLC_QBLK
    q_blocks = qf.reshape(B, nqb, SLC_QBLK, G, HG, D).transpose(1, 0, 2, 3, 4, 5)
    idx_blocks = sel_idx.reshape(B, G, nqb, SLC_QBLK, n_sel).transpose(2, 0, 1, 3, 4)
    t_blocks = t.reshape(nqb, SLC_QBLK)
    b_ix = jnp.arange(B)[:, None, None, None]
    g_ix = jnp.arange(G)[None, :, None, None]
    g_ix5 = jnp.arange(G)[None, :, None, None, None]
    tab = table.reshape(NUM_BUCKETS, G, HG)

    def slc_block(args):
        qb, ib, tb = args
        kg = kb[b_ix, g_ix, ib]
        vg = vb[b_ix, g_ix, ib]
        s_pos = ib[..., None] * SLC_BLOCK + jnp.arange(SLC_BLOCK)
        dist = tb[None, None, :, None, None] - s_pos
        valid_blk = ib <= (tb // SLC_BLOCK)[None, None, :, None]
        mask = (dist >= 0) & valid_blk[..., None]
        bias = tab[rel_bucket(dist), g_ix5]
        logits = jnp.einsum('bqghd,bgqnkd->bgqnkh', qb, kg) + bias
        m_tok = n_sel * SLC_BLOCK
        logits = logits.reshape(B, G, SLC_QBLK, m_tok, HG)
        p = masked_softmax(logits, mask.reshape(B, G, SLC_QBLK, m_tok)[..., None], -2)
        return jnp.einsum('bgqmh,bgqmd->bqghd', p, vg.reshape(B, G, SLC_QBLK, m_tok, D))

    o_s = lax.map(slc_block, (q_blocks, idx_blocks, t_blocks))
    o_s = o_s.transpose(1, 0, 2, 3, 4, 5).reshape(B, S, G, HG, D)

    nwb = S // WIN_QBLK
    n_pre = WINDOW // WIN_QBLK
    band = n_pre + 1
    m_win = band * WIN_QBLK
    pad = ((0, 0), (WINDOW, 0), (0, 0), (0, 0))
    kw = jnp.pad(k_win.astype(f32).reshape(B, S, G, D), pad).reshape(B, nwb + n_pre, WIN_QBLK, G, D)
    vw = jnp.pad(v_win.astype(f32).reshape(B, S, G, D), pad).reshape(B, nwb + n_pre, WIN_QBLK, G, D)
    kband = jnp.concatenate([kw[:, i:i + nwb] for i in range(band)], axis=2)
    vband = jnp.concatenate([vw[:, i:i + nwb] for i in range(band)], axis=2)
    qw = qf.reshape(B, nwb, WIN_QBLK, G, HG, D)
    tq = t.reshape(nwb, WIN_QBLK)
    s_pos = (jnp.arange(nwb) * WIN_QBLK - WINDOW)[:, None] + jnp.arange(m_win)[None, :]
    dist = tq[:, :, None] - s_pos[:, None, :]
    mask_w = (dist >= 0) & (dist < WINDOW) & (s_pos[:, None, :] >= 0)
    bias_w = table[rel_bucket(dist)].reshape(nwb, WIN_QBLK, m_win, G, HG).transpose(0, 3, 4, 1, 2)
    logits_w = jnp.einsum('bnqghd,bnmgd->bnghqm', qw, kband) + bias_w[None]
    p_w = masked_softmax(logits_w, mask_w[None, :, None, None], -1)
    o_w = jnp.einsum('bnghqm,bnmgd->bnqghd', p_w, vband).reshape(B, S, G, HG, D)

    gt = jax.nn.sigmoid(gates.astype(f32)).reshape(B, S, 3, G, HG, 1)
    o = gt[:, :, 0] * o_c + gt[:, :, 1] * o_s + gt[:, :, 2] * o_w
    return o.reshape(B, S, NSA_WIDTH)


def rwkv7_mixer(feat, mu, w0, w2, a0, a2, k_k, k_a, r_k, ln_w, ln_b):
    f32 = jnp.float32
    B, S, _ = feat.shape
    H, N, W = RWKV_HEADS, RWKV_HEAD_DIM, RWKV_WIDTH
    pf = feat.astype(f32)
    prev = jnp.pad(pf[:, :-1], ((0, 0), (1, 0), (0, 0)))
    pf = pf + mu.astype(f32) * (prev - pf)
    r, k, v, wd, ad = jnp.split(pf, [W, 2 * W, 3 * W, 3 * W + DECAY_LORA], axis=-1)
    w = -jax.nn.softplus(-(w0.astype(f32) + jnp.tanh(wd) @ w2.astype(f32))) - 0.5
    decay = jnp.exp(-jnp.exp(w))
    a = jax.nn.sigmoid(a0.astype(f32) + ad @ a2.astype(f32))
    kk = (k * k_k.astype(f32)).reshape(B, S, H, N)
    kk = kk / jnp.maximum(jnp.sqrt(jnp.sum(kk * kk, axis=-1, keepdims=True)), 1e-12)
    k = k * (1.0 + (a - 1.0) * k_a.astype(f32))
    r4, w4, k4, v4, a4 = (z.reshape(B, S, H, N) for z in (r, decay, k, v, a))
    aa = -kk
    bb = kk * a4

    def step(state, inp):
        r_t, w_t, k_t, v_t, a_t, b_t = inp
        sa = jnp.einsum('bhij,bhj->bhi', state, a_t)
        state = (state * w_t[:, :, None, :] + sa[..., None] * b_t[:, :, None, :]
                 + v_t[..., None] * k_t[:, :, None, :])
        return state, jnp.einsum('bhij,bhj->bhi', state, r_t)

    xs = tuple(z.transpose(1, 0, 2, 3) for z in (r4, w4, k4, v4, aa, bb))
    state0 = jnp.zeros((B, H, N, N), f32)
    _, y = lax.scan(step, state0, xs)
    y = y.transpose(1, 0, 2, 3)
    mean = jnp.mean(y, axis=-1, keepdims=True)
    var = jnp.mean(jnp.square(y - mean), axis=-1, keepdims=True)
    y = ((y - mean) * lax.rsqrt(var + RWKV_GN_EPS)).reshape(B, S, W) * ln_w.astype(f32) + ln_b.astype(f32)
    bonus = jnp.sum(r4 * k4 * r_k.astype(f32), axis=-1, keepdims=True) * v4
    return y + bonus.reshape(B, S, W)


def setup_inputs(seed: int = 0) -> dict:
    key = jax.random.key(seed)
    ks = jax.random.split(key, 24)
    f32 = jnp.float32
    L = DEPTH
    HD = NSA_HEAD_DIM

    def nrm(k, shape, scale):
        return jax.random.normal(k, shape, f32) * scale

    return {
        'x': nrm(ks[0], (BATCH, SEQ, D_MODEL), 1.0),
        'pre_norm_g': 1.0 + nrm(ks[1], (L, D_MODEL), 0.05),
        'w_in': nrm(ks[2], (L, D_MODEL, IN_COLS), D_MODEL ** -0.5),
        'rel_bias_table': nrm(ks[3], (NUM_BUCKETS, NSA_HEADS), 0.5),
        'cmp_pos_k': nrm(ks[4], (L, CMP_BLOCK, HD), 0.1),
        'cmp_pos_v': nrm(ks[5], (L, CMP_BLOCK, HD), 0.1),
        'cmp_k_w1': nrm(ks[6], (L, CMP_BLOCK * HD, HD), (CMP_BLOCK * HD) ** -0.5),
        'cmp_k_w2': nrm(ks[7], (L, HD, HD), HD ** -0.5),
        'cmp_v_w1': nrm(ks[8], (L, CMP_BLOCK * HD, HD), (CMP_BLOCK * HD) ** -0.5),
        'cmp_v_w2': nrm(ks[9], (L, HD, HD), HD ** -0.5),
        'rwkv_mu': jax.random.uniform(ks[10], (L, SHIFT_COLS), f32),
        'rwkv_w0': jax.random.uniform(ks[11], (L, RWKV_WIDTH), f32, -6.0, -1.0),
        'rwkv_w2': nrm(ks[12], (L, DECAY_LORA, RWKV_WIDTH), 0.1),
        'rwkv_a0': nrm(ks[13], (L, RWKV_WIDTH), 0.1),
        'rwkv_a2': nrm(ks[14], (L, AAA_LORA, RWKV_WIDTH), 0.1),
        'rwkv_k_k': 0.85 + nrm(ks[15], (L, RWKV_WIDTH), 0.05),
        'rwkv_k_a': 1.0 + nrm(ks[16], (L, RWKV_WIDTH), 0.05),
        'rwkv_r_k': nrm(ks[17], (L, RWKV_HEADS, RWKV_HEAD_DIM), 0.1),
        'rwkv_ln_w': 1.0 + nrm(ks[18], (L, RWKV_WIDTH), 0.05),
        'rwkv_ln_b': nrm(ks[19], (L, RWKV_WIDTH), 0.02),
        'w_out': nrm(ks[20], (L, D_MIX, D_MODEL), D_MIX ** -0.5),
        'post_norm_g': 1.0 + nrm(ks[21], (L, D_MODEL), 0.05),
    }


def reference(x, pre_norm_g, w_in, rel_bias_table, cmp_pos_k, cmp_pos_v, cmp_k_w1, cmp_k_w2,
              cmp_v_w1, cmp_v_w2, rwkv_mu, rwkv_w0, rwkv_w2, rwkv_a0, rwkv_a2, rwkv_k_k, rwkv_k_a,
              rwkv_r_k, rwkv_ln_w, rwkv_ln_b, w_out, post_norm_g):
    o1 = COL_Q
    o2 = o1 + COL_KV
    o3 = o2 + COL_GATE
    o4 = o3 + COL_ZA
    o5 = o4 + SHIFT_COLS
    h = x
    for l in range(DEPTH):
        hn = rms_norm(h, pre_norm_g[l])
        proj = hn @ w_in[l]
        q, kv6, gates, z_a, feat_b, z_b = jnp.split(proj, [o1, o2, o3, o4, o5], axis=-1)
        k_cmp, v_cmp, k_slc, v_slc, k_win, v_win = jnp.split(kv6, 6, axis=-1)
        o_a = nsa_mixer(q, k_cmp, v_cmp, k_slc, v_slc, k_win, v_win, gates, rel_bias_table,
                        cmp_pos_k[l], cmp_pos_v[l], cmp_k_w1[l], cmp_k_w2[l], cmp_v_w1[l], cmp_v_w2[l])
        o_b = rwkv7_mixer(feat_b, rwkv_mu[l], rwkv_w0[l], rwkv_w2[l], rwkv_a0[l], rwkv_a2[l],
                          rwkv_k_k[l], rwkv_k_a[l], rwkv_r_k[l], rwkv_ln_w[l], rwkv_ln_b[l])
        mix = jnp.concatenate([o_a * jax.nn.silu(z_a.astype(jnp.float32)),
                               o_b * jax.nn.silu(z_b.astype(jnp.float32))], axis=-1).astype(h.dtype)
        y = mix @ w_out[l]
        h = h + rms_norm(y, post_norm_g[l])
    return h
```

```python
import functools
import math

import numpy as np
import jax
import jax.numpy as jnp
from jax import lax
from jax.experimental import pallas as pl
from jax.experimental.pallas import tpu as pltpu

F32 = jnp.float32
BF16 = jnp.bfloat16
HI = lax.Precision.HIGHEST

D_MODEL = 2048
NSA_HEADS = 8
NSA_KV_HEADS = 2
NSA_GROUP = NSA_HEADS // NSA_KV_HEADS
HEAD_DIM = 128
NSA_WIDTH = NSA_HEADS * HEAD_DIM
CMP_BLOCK = 32
CMP_STRIDE = 16
SLC_BLOCK = 64
SLC_TOP_N = 16
WINDOW = 512
RWKV_WIDTH = 1024
RWKV_HEAD_DIM = 64
RWKV_HEADS = RWKV_WIDTH // RWKV_HEAD_DIM
LORA = 64
NUM_BUCKETS = 32
MAX_DISTANCE = 1024
NORM_EPS = 1e-6
RWKV_GN_EPS = 64e-5

R_Q = 0
R_KV = R_Q + NSA_WIDTH
R_GATE = R_KV + 6 * NSA_KV_HEADS * HEAD_DIM
R_ZA = R_GATE + 3 * NSA_HEADS
R_FEAT = R_ZA + NSA_WIDTH
R_ZB = R_FEAT + 3 * RWKV_WIDTH + 2 * LORA
R_END = R_ZB + RWKV_WIDTH

P_Q = 0
P_KV = 1024
P_ZA = 2560
P_RKV = 3584
P_ZB = 6656
P_WDAD = 7680
P_GATE = 7808
NP = 8192

LANE = 128
TQ = 128
ND = 9
CH = 64
HPG = 4
GW = HPG * RWKV_HEAD_DIM
NEG = -1e30


def _bucket_thresholds():
    out = []
    for k in range(1, NUM_BUCKETS // 2):
        n = 16
        while n ** 8 < (16 ** 8) * (2 ** (3 * k)):
            n += 1
        out.append(n)
    return out


_THR = _bucket_thresholds()


def _mm(a, b, precision=None):
    return jnp.dot(a, b, preferred_element_type=F32, precision=precision)


def _mm_nt(a, b, precision=None):
    return lax.dot_general(a, b, (((1,), (1,)), ((), ())), preferred_element_type=F32,
                           precision=precision)


def _mm_tn(a, b, precision=None):
    return lax.dot_general(a, b, (((0,), (0,)), ((), ())), preferred_element_type=F32,
                           precision=precision)


def _iota(shape, dim):
    return lax.broadcasted_iota(jnp.int32, shape, dim)


def _div_pow2(x, n):
    assert n & (n - 1) == 0
    return x >> (n.bit_length() - 1)


def _mod_pow2(x, n):
    assert n & (n - 1) == 0
    return x & (n - 1)


def _inproj_kernel(x_ref, g_ref, w_ref, o_ref, hn_ref):
    @pl.when(pl.program_id(1) == 0)
    def _():
        x = x_ref[...]
        ms = jnp.mean(x * x, axis=-1, keepdims=True)
        hn_ref[...] = (x * lax.rsqrt(ms + NORM_EPS) * g_ref[...]).astype(BF16)

    o_ref[...] = _mm(hn_ref[...], w_ref[...])


def _inproj(x2, g, w, tm, tn):
    m, d = x2.shape
    n = w.shape[1]
    return pl.pallas_call(
        _inproj_kernel,
        grid=(m // tm, n // tn),
        in_specs=[
            pl.BlockSpec((tm, d), lambda i, j: (i, 0)),
            pl.BlockSpec((1, d), lambda i, j: (0, 0)),
            pl.BlockSpec((d, tn), lambda i, j: (0, j)),
        ],
        out_specs=pl.BlockSpec((tm, tn), lambda i, j: (i, j)),
        out_shape=jax.ShapeDtypeStruct((m, n), F32),
        scratch_shapes=[pltpu.VMEM((tm, d), BF16)],
        compiler_params=pltpu.CompilerParams(
            dimension_semantics=("parallel", "arbitrary"),
            vmem_limit_bytes=56 * 1024 * 1024),
        name="inproj",
    )(x2, g, w)


def _bucket(n):
    n = jnp.maximum(n, 0)
    large = jnp.full(n.shape, NUM_BUCKETS // 2, jnp.int32)
    for thr in _THR:
        large = large + (n >= thr).astype(jnp.int32)
    return jnp.where(n < NUM_BUCKETS // 2, n, large)


def _lookup(bucket, tab_ref, h):
    out = jnp.zeros(bucket.shape, F32)
    for b in range(NUM_BUCKETS):
        out = jnp.where(bucket == b, tab_ref[b * NSA_HEADS + h], out)
    return out


def _bias_kernel(tab_ref, bc_ref, tp_ref):
    h = pl.program_id(0)
    s, nr = bc_ref.shape[1], bc_ref.shape[2]
    dist_c = _iota((s, nr), 0) - (_iota((s, nr), 1) * CMP_STRIDE + (CMP_BLOCK - 1))
    bc_ref[0] = _lookup(_bucket(dist_c), tab_ref, h)
    base = _iota((TQ, TQ), 0) - _iota((TQ, TQ), 1)
    for d in range(ND):
        tp_ref[0, d] = _lookup(_bucket(base + d * TQ), tab_ref, h)


def _bias(table_flat, s):
    nr = s // CMP_STRIDE
    return pl.pallas_call(
        _bias_kernel,
        grid=(NSA_HEADS,),
        in_specs=[pl.BlockSpec(memory_space=pltpu.SMEM)],
        out_specs=[
            pl.BlockSpec((1, s, nr), lambda h: (h, 0, 0)),
            pl.BlockSpec((1, ND, TQ, TQ), lambda h: (h, 0, 0, 0)),
        ],
        out_shape=[
            jax.ShapeDtypeStruct((NSA_HEADS, s, nr), F32),
            jax.ShapeDtypeStruct((NSA_HEADS, ND, TQ, TQ), F32),
        ],
        compiler_params=pltpu.CompilerParams(dimension_semantics=("arbitrary",)),
        name="bias",
    )(table_flat)


def _compress(kv_ref, pos_ref, w1_ref, w2_ref, nr):
    half = CMP_STRIDE * HEAD_DIM
    r = jnp.concatenate(
        [kv_ref[0, pl.ds(m, nr, stride=CMP_STRIDE), :] for m in range(CMP_STRIDE)], axis=1)
    a = _mm((r + pos_ref[0:1, :]).astype(BF16), w1_ref[0:half, :])
    b = _mm((r + pos_ref[1:2, :]).astype(BF16), w1_ref[half:2 * half, :])
    pre = a + pltpu.roll(b, nr - 1, 0)
    h1 = pre * jax.nn.sigmoid(pre)
    return _mm(h1.astype(BF16), w2_ref[...])


def _flash(q4, k_ref, v_ref, toep_ref, j_lo, j_hi, qi, mask_fn):
    rows = q4.shape[0]

    def body(j, carry):
        m, l, acc = carry
        k0 = pl.multiple_of(j * TQ, TQ)
        kt = k_ref[0, pl.ds(k0, TQ), :].astype(BF16)
        vt = v_ref[0, pl.ds(k0, TQ), :].astype(BF16)
        delta = jnp.minimum(qi - j, ND - 1)
        bias = jnp.concatenate([toep_ref[hg, delta] for hg in range(NSA_GROUP)], axis=0)
        msk = mask_fn(j)
        msk4 = jnp.concatenate([msk] * NSA_GROUP, axis=0)
        s = jnp.where(msk4, _mm_nt(q4, kt) + bias, NEG)
        m_new = jnp.maximum(m, jnp.max(s, axis=-1, keepdims=True))
        alpha = jnp.exp(m - m_new)
        p = jnp.where(msk4, jnp.exp(s - m_new), 0.0)
        l = alpha * l + jnp.sum(p, axis=-1, keepdims=True)
        acc = alpha * acc + _mm(p.astype(BF16), vt)
        return m_new, l, acc

    init = (jnp.full((rows, 1), NEG, F32), jnp.zeros((rows, 1), F32),
            jnp.zeros((rows, HEAD_DIM), F32))
    m, l, acc = lax.fori_loop(j_lo, j_hi + 1, body, init)
    return acc / jnp.maximum(l, 1e-30)


def _nsa_kernel(q_ref, kc_ref, vc_ref, ks_ref, vs_ref, kw_ref, vw_ref, gate_ref, za_ref,
                bc_ref, toep_ref, posk_ref, w1k_ref, w2k_ref, posv_ref, w1v_ref, w2v_ref,
                o_ref, kcs_ref, vcs_ref):
    qi = pl.program_id(2)
    s_len = kc_ref.shape[1]
    nr = s_len // CMP_STRIDE
    nb = s_len // SLC_BLOCK
    n_sel = min(SLC_TOP_N, nb)

    @pl.when(qi == 0)
    def _():
        kcs_ref[...] = _compress(kc_ref, posk_ref, w1k_ref, w2k_ref, nr).astype(BF16)
        vcs_ref[...] = _compress(vc_ref, posv_ref, w1v_ref, w2v_ref, nr).astype(BF16)

    q0 = qi * TQ
    q = q_ref[0] * (HEAD_DIM ** -0.5)
    qh = [q[:, h * HEAD_DIM:(h + 1) * HEAD_DIM].astype(BF16) for h in range(NSA_GROUP)]

    t_c = q0 + _iota((TQ, nr), 0)
    i_c = _iota((TQ, nr), 1)
    mask_c = (t_c - (i_c * CMP_STRIDE + (CMP_BLOCK - 1)) >= 0) & (i_c < nr - 1)
    kcs = kcs_ref[...]
    vcs = vcs_ref[...]
    o_c = []
    psum = jnp.zeros((TQ, nr), F32)
    for h in range(NSA_GROUP):
        lg = jnp.where(mask_c, _mm_nt(qh[h], kcs) + bc_ref[h], NEG)
        m = jnp.max(lg, axis=-1, keepdims=True)
        e = jnp.where(mask_c, jnp.exp(lg - m), 0.0)
        p = e / jnp.maximum(jnp.sum(e, axis=-1, keepdims=True), 1e-30)
        o_c.append(_mm(p.astype(BF16), vcs))
        psum = psum + p

    ov_i = _iota((nb, nr), 1) * CMP_STRIDE
    ov_j = _iota((nb, nr), 0) * SLC_BLOCK
    ov_t = ((ov_i < ov_j + SLC_BLOCK) & (ov_i + CMP_BLOCK > ov_j)).astype(F32)
    imp_t = _mm_nt(ov_t, psum, precision=HI)
    jb = _iota((nb, TQ), 0)
    cur = _div_pow2(q0 + _iota((nb, TQ), 1), SLC_BLOCK)
    forced = (jb == 0) | (jb == cur) | (jb == cur - 1)
    causal = jb <= cur
    score = jnp.where(forced, jnp.inf, jnp.where(causal, imp_t, -jnp.inf))
    rank = jnp.zeros((nb, TQ), jnp.int32)
    for jp in range(nb):
        sj = score[jp:jp + 1, :]
        beats = (sj > score) | ((sj == score) & (jb > jp))
        rank = rank + beats.astype(jnp.int32)
    sel_t = ((rank < n_sel) & causal).astype(BF16)

    q4 = jnp.concatenate(qh, axis=0)
    t_q = q0 + _iota((TQ, TQ), 0)

    def mask_slc(j):
        kpos = j * TQ + _iota((TQ, TQ), 1)
        e_j = (_iota((nb, TQ), 0) == _div_pow2(j * TQ + _iota((nb, TQ), 1), SLC_BLOCK)).astype(BF16)
        chosen = _mm_tn(sel_t, e_j)
        return (chosen > 0.5) & (kpos <= t_q)

    o_s = _flash(q4, ks_ref, vs_ref, toep_ref, 0, qi, qi, mask_slc)

    def mask_win(j):
        dist = t_q - (j * TQ + _iota((TQ, TQ), 1))
        return (dist >= 0) & (dist < WINDOW)

    o_w = _flash(q4, kw_ref, vw_ref, toep_ref, jnp.maximum(qi - WINDOW // TQ, 0), qi, qi, mask_win)

    gts = jax.nn.sigmoid(gate_ref[0])
    za = za_ref[0]
    for h in range(NSA_GROUP):
        o = (gts[:, h:h + 1] * o_c[h]
             + gts[:, NSA_GROUP + h:NSA_GROUP + h + 1] * o_s[h * TQ:(h + 1) * TQ]
             + gts[:, 2 * NSA_GROUP + h:2 * NSA_GROUP + h + 1] * o_w[h * TQ:(h + 1) * TQ])
        z = za[:, h * HEAD_DIM:(h + 1) * HEAD_DIM]
        o_ref[0, :, h * HEAD_DIM:(h + 1) * HEAD_DIM] = (o * (z * jax.nn.sigmoid(z))).astype(o_ref.dtype)


def _nsa(proj3, bias_c, toep, posk, w1k, w2k, posv, w1v, w2v):
    b, s, _ = proj3.shape
    nr = s // CMP_STRIDE
    gq = NSA_GROUP * HEAD_DIM

    def kvspec(idx):
        return pl.BlockSpec((1, s, HEAD_DIM), lambda bi, g, qi, idx=idx: (bi, 0, P_KV // HEAD_DIM + 2 * idx + g))

    def whole(a):
        return pl.BlockSpec(a.shape, lambda bi, g, qi, nd=a.ndim: (0,) * nd)

    in_specs = [
        pl.BlockSpec((1, TQ, gq), lambda bi, g, qi: (bi, qi, P_Q // gq + g)),
        kvspec(0), kvspec(1), kvspec(2), kvspec(3), kvspec(4), kvspec(5),
        pl.BlockSpec((1, TQ, LANE), lambda bi, g, qi: (bi, qi, P_GATE // LANE + g)),
        pl.BlockSpec((1, TQ, gq), lambda bi, g, qi: (bi, qi, P_ZA // gq + g)),
        pl.BlockSpec((NSA_GROUP, TQ, nr), lambda bi, g, qi: (g, qi, 0)),
        pl.BlockSpec((NSA_GROUP, ND, TQ, TQ), lambda bi, g, qi: (g, 0, 0, 0)),
        whole(posk), whole(w1k), whole(w2k), whole(posv), whole(w1v), whole(w2v),
    ]
    return pl.pallas_call(
        _nsa_kernel,
        grid=(b, NSA_KV_HEADS, s // TQ),
        in_specs=in_specs,
        out_specs=pl.BlockSpec((1, TQ, gq), lambda bi, g, qi: (bi, qi, g)),
        out_shape=jax.ShapeDtypeStruct((b, s, NSA_WIDTH), BF16),
        scratch_shapes=[pltpu.VMEM((nr, HEAD_DIM), BF16), pltpu.VMEM((nr, HEAD_DIM), BF16)],
        compiler_params=pltpu.CompilerParams(
            dimension_semantics=("parallel", "parallel", "arbitrary"),
            vmem_limit_bytes=48 * 1024 * 1024),
        name="nsa",
    )(proj3, proj3, proj3, proj3, proj3, proj3, proj3, proj3, proj3,
      bias_c, toep, posk, w1k, w2k, posv, w1v, w2v)


def _shift(x, prev_ref):
    out = jnp.where(_iota(x.shape, 0) == 0, prev_ref[...], pltpu.roll(x, 1, 0))
    prev_ref[...] = x[x.shape[0] - 1:]
    return out


def _rwkv_kernel(r_ref, k_ref, v_ref, wa_ref, zb_ref, vec_ref, muwa_ref, w2_ref, a2_ref,
                 o_ref, st_ref, pr_ref, pk_ref, pv_ref, pwa_ref):
    ti = pl.program_id(2)
    first = ti == 0
    tb = r_ref.shape[1]

    @pl.when(first)
    def _():
        for ref in (st_ref, pr_ref, pk_ref, pv_ref, pwa_ref):
            ref[...] = jnp.zeros_like(ref)

    vec = vec_ref[...]
    mu_r, mu_k, mu_v = vec[0:1], vec[1:2], vec[2:3]
    w0, a0, k_k, k_a = vec[3:4], vec[4:5], vec[5:6], vec[6:7]
    ln_w, ln_b, r_k = vec[7:8], vec[8:9], vec[9:10]

    def mix(ref, prev_ref, mu):
        x = ref[0]
        return x + mu * (_shift(x, prev_ref) - x)

    r = mix(r_ref, pr_ref, mu_r)
    k = mix(k_ref, pk_ref, mu_k)
    v = mix(v_ref, pv_ref, mu_v)
    wa = mix(wa_ref, pwa_ref, muwa_ref[...])

    w = -jax.nn.softplus(-(w0 + _mm(jnp.tanh(wa[:, :LORA]), w2_ref[...], precision=HI))) - 0.5
    lw = -jnp.exp(w)
    a_sig = jax.nn.sigmoid(a0 + _mm(wa[:, LORA:], a2_ref[...], precision=HI))

    seg = (_div_pow2(_iota((GW, GW), 0), RWKV_HEAD_DIM) == _div_pow2(_iota((GW, GW), 1), RWKV_HEAD_DIM))
    segf = seg.astype(F32)
    kk = k * k_k
    kk = kk / jnp.maximum(jnp.sqrt(_mm(kk * kk, segf, precision=HI)), 1e-12)
    k = k * (1.0 + (a_sig - 1.0) * k_a)
    a_vec = -kk
    b_vec = kk * a_sig

    assert CH == RWKV_HEAD_DIM
    lane_s = _mod_pow2(_iota((CH, GW), 1), CH)
    row_t = _iota((CH, GW), 0)
    strict = lane_s < row_t
    incl = lane_s <= row_t
    eye = (lane_s == row_t).astype(F32)
    tri = (_iota((CH, CH), 1) <= _iota((CH, CH), 0)).astype(F32)

    def bd(x):
        return jnp.concatenate([x] * HPG, axis=0) * segf

    ys = []
    g_state = st_ref[...]
    for c in range(tb // CH):
        sl = slice(c * CH, (c + 1) * CH)
        rc, kc, vc, ac, bc, lwc = r[sl], k[sl], v[sl], a_vec[sl], b_vec[sl], lw[sl]
        cum = _mm(tri, lwc, precision=HI)
        tot = cum[CH - 1:CH]
        e_in = jnp.exp(cum)
        e_out = jnp.exp(-cum)
        e_end = jnp.exp(tot - cum)
        r_t = rc * e_in
        a_t = ac * jnp.exp(cum - lwc)
        b_t = bc * e_out
        k_t = kc * e_out
        lhs = jnp.concatenate([a_t, r_t], axis=0)
        aa = _mm_nt(lhs, jnp.concatenate([bd(b_t), bd(k_t)], axis=0), precision=HI)
        a_ab = jnp.where(strict, aa[:CH, :GW], 0.0)
        a_ak = jnp.where(strict, aa[:CH, GW:], 0.0)
        a_rb = jnp.where(incl, aa[CH:, :GW], 0.0)
        a_rk = jnp.where(incl, aa[CH:, GW:], 0.0)
        t_inv = eye + a_ab
        mpow = _mm(a_ab, bd(a_ab), precision=HI)
        n_sq = int(math.log2(CH)) - 1
        for it in range(n_sq):
            res = _mm(jnp.concatenate([t_inv, mpow], axis=0), bd(mpow), precision=HI)
            t_inv = t_inv + res[:CH]
            mpow = res[CH:]
        x0 = _mm_nt(lhs, g_state, precision=HI)
        u = _mm(t_inv, bd(x0[:CH] + _mm(a_ak, bd(vc), precision=HI)), precision=HI)
        y = x0[CH:] + _mm(jnp.concatenate([a_rb, a_rk], axis=1),
                          jnp.concatenate([bd(u), bd(vc)], axis=0), precision=HI)
        ys.append(y)
        upd = _mm_tn(jnp.concatenate([u, vc], axis=0),
                     jnp.concatenate([bc * e_end, kc * e_end], axis=0), precision=HI)
        g_state = g_state * jnp.exp(tot) + upd * segf
    st_ref[...] = g_state

    y = jnp.concatenate(ys, axis=0)
    inv_n = 1.0 / RWKV_HEAD_DIM
    mean = _mm(y, segf, precision=HI) * inv_n
    yc = y - mean
    var = _mm(yc * yc, segf, precision=HI) * inv_n
    yn = yc * lax.rsqrt(var + RWKV_GN_EPS) * ln_w + ln_b
    bonus = _mm(r * k * r_k, segf, precision=HI) * v
    zb = zb_ref[0]
    o_ref[0] = ((yn + bonus) * (zb * jax.nn.sigmoid(zb))).astype(o_ref.dtype)


def _rwkv(proj3, vecs, mu_wa, w2, a2, tb):
    b, s, _ = proj3.shape
    ng = RWKV_WIDTH // GW

    def col(off):
        return pl.BlockSpec((1, tb, GW), lambda bi, g, ti, off=off: (bi, ti, off // GW + g))

    in_specs = [
        col(P_RKV), col(P_RKV + RWKV_WIDTH), col(P_RKV + 2 * RWKV_WIDTH),
        pl.BlockSpec((1, tb, LANE), lambda bi, g, ti: (bi, ti, P_WDAD // LANE)),
        col(P_ZB),
        pl.BlockSpec((vecs.shape[0], GW), lambda bi, g, ti: (0, g)),
        pl.BlockSpec((1, LANE), lambda bi, g, ti: (0, 0)),
        pl.BlockSpec((LORA, GW), lambda bi, g, ti: (0, g)),
        pl.BlockSpec((LORA, GW), lambda bi, g, ti: (0, g)),
    ]
    return pl.pallas_call(
        _rwkv_kernel,
        grid=(b, ng, s // tb),
        in_specs=in_specs,
        out_specs=pl.BlockSpec((1, tb, GW), lambda bi, g, ti: (bi, ti, g)),
        out_shape=jax.ShapeDtypeStruct((b, s, RWKV_WIDTH), BF16),
        scratch_shapes=[pltpu.VMEM((GW, GW), F32), pltpu.VMEM((1, GW), F32), pltpu.VMEM((1, GW), F32),
                        pltpu.VMEM((1, GW), F32), pltpu.VMEM((1, LANE), F32)],
        compiler_params=pltpu.CompilerParams(
            dimension_semantics=("parallel", "parallel", "arbitrary")),
        name="rwkv",
    )(proj3, proj3, proj3, proj3, proj3, vecs, mu_wa, w2, a2)


def _outproj_kernel(ma_ref, mb_ref, wa_ref, wb_ref, x_ref, g_ref, o_ref):
    y = _mm(ma_ref[...], wa_ref[...]) + _mm(mb_ref[...], wb_ref[...])
    ms = jnp.mean(y * y, axis=-1, keepdims=True)
    o_ref[...] = x_ref[...] + y * lax.rsqrt(ms + NORM_EPS) * g_ref[...]


def _outproj(mix_a, mix_b, w_a, w_b, x2, g, tm):
    m, d = x2.shape
    ka, kb = mix_a.shape[1], mix_b.shape[1]
    return pl.pallas_call(
        _outproj_kernel,
        grid=(m // tm,),
        in_specs=[
            pl.BlockSpec((tm, ka), lambda i: (i, 0)),
            pl.BlockSpec((tm, kb), lambda i: (i, 0)),
            pl.BlockSpec((ka, d), lambda i: (0, 0)),
            pl.BlockSpec((kb, d), lambda i: (0, 0)),
            pl.BlockSpec((tm, d), lambda i: (i, 0)),
            pl.BlockSpec((1, d), lambda i: (0, 0)),
        ],
        out_specs=pl.BlockSpec((tm, d), lambda i: (i, 0)),
        out_shape=jax.ShapeDtypeStruct((m, d), F32),
        compiler_params=pltpu.CompilerParams(
            dimension_semantics=("parallel",), vmem_limit_bytes=56 * 1024 * 1024),
        name="outproj",
    )(mix_a, mix_b, w_a, w_b, x2, g)


def _permute_w_in(w):
    d = w.shape[0]
    gate = w[:, R_GATE:R_ZA].reshape(d, 3, NSA_KV_HEADS, NSA_GROUP)
    gate_blocks = []
    for g in range(NSA_KV_HEADS):
        gb = gate[:, :, g, :].reshape(d, 3 * NSA_GROUP)
        gate_blocks.append(jnp.pad(gb, ((0, 0), (0, LANE - 3 * NSA_GROUP))))
    parts = [
        w[:, R_Q:R_KV], w[:, R_KV:R_GATE], w[:, R_ZA:R_FEAT],
        w[:, R_FEAT:R_FEAT + 3 * RWKV_WIDTH], w[:, R_ZB:R_END],
        w[:, R_FEAT + 3 * RWKV_WIDTH:R_ZB],
    ] + gate_blocks
    wp = jnp.concatenate(parts, axis=1)
    return jnp.pad(wp, ((0, 0), (0, NP - wp.shape[1]))).astype(BF16)


def _block(x, pre_norm_g, w_in, rel_bias_table, cmp_pos_k, cmp_pos_v, cmp_k_w1, cmp_k_w2, cmp_v_w1,
           cmp_v_w2, rwkv_mu, rwkv_w0, rwkv_w2, rwkv_a0, rwkv_a2, rwkv_k_k, rwkv_k_a, rwkv_r_k,
           rwkv_ln_w, rwkv_ln_b, w_out, post_norm_g):
    b, s, d = x.shape
    x2 = x.reshape(b * s, d)
    tm = min(1024, b * s)
    proj = _inproj(x2, pre_norm_g.reshape(1, d), _permute_w_in(w_in), tm, 1024)
    proj3 = proj.reshape(b, s, NP)

    bias_c, toep = _bias(rel_bias_table.reshape(-1), s)
    half = CMP_STRIDE * HEAD_DIM
    mix_a = _nsa(proj3, bias_c, toep,
                 cmp_pos_k.reshape(2, half), cmp_k_w1.astype(BF16), cmp_k_w2.astype(BF16),
                 cmp_pos_v.reshape(2, half), cmp_v_w1.astype(BF16), cmp_v_w2.astype(BF16))

    w3 = 3 * RWKV_WIDTH
    vec_rows = [rwkv_mu[:RWKV_WIDTH], rwkv_mu[RWKV_WIDTH:2 * RWKV_WIDTH], rwkv_mu[2 * RWKV_WIDTH:w3],
                rwkv_w0, rwkv_a0, rwkv_k_k, rwkv_k_a, rwkv_ln_w, rwkv_ln_b, rwkv_r_k.reshape(-1)]
    vecs = jnp.stack(vec_rows + [jnp.zeros_like(rwkv_w0)] * (16 - len(vec_rows)), axis=0)
    mix_b = _rwkv(proj3, vecs, rwkv_mu[w3:].reshape(1, 2 * LORA), rwkv_w2, rwkv_a2, min(256, s))

    w_o = w_out.astype(BF16)
    out = _outproj(mix_a.reshape(b * s, NSA_WIDTH), mix_b.reshape(b * s, RWKV_WIDTH),
                   w_o[:NSA_WIDTH], w_o[NSA_WIDTH:], x2, post_norm_g.reshape(1, d), min(512, b * s))
    return out.reshape(b, s, d)


def kernel(x, pre_norm_g, w_in, rel_bias_table, cmp_pos_k, cmp_pos_v, cmp_k_w1, cmp_k_w2, cmp_v_w1,
           cmp_v_w2, rwkv_mu, rwkv_w0, rwkv_w2, rwkv_a0, rwkv_a2, rwkv_k_k, rwkv_k_a, rwkv_r_k,
           rwkv_ln_w, rwkv_ln_b, w_out, post_norm_g):
    h = x
    for l in range(pre_norm_g.shape[0]):
        h = _block(h, pre_norm_g[l], w_in[l], rel_bias_table, cmp_pos_k[l], cmp_pos_v[l], cmp_k_w1[l],
                   cmp_k_w2[l], cmp_v_w1[l], cmp_v_w2[l], rwkv_mu[l], rwkv_w0[l], rwkv_w2[l],
                   rwkv_a0[l], rwkv_a2[l], rwkv_k_k[l], rwkv_k_a[l], rwkv_r_k[l], rwkv_ln_w[l],
                   rwkv_ln_b[l], w_out[l], post_norm_g[l])
    return h
```

```python
import functools
import math

import numpy as np
import jax
import jax.numpy as jnp
from jax import lax
from jax.experimental import pallas as pl
from jax.experimental.pallas import tpu as pltpu

F32 = jnp.float32
BF16 = jnp.bfloat16
HI = lax.Precision.HIGHEST

D_MODEL = 2048
NSA_HEADS = 8
NSA_KV_HEADS = 2
NSA_GROUP = NSA_HEADS // NSA_KV_HEADS
HEAD_DIM = 128
NSA_WIDTH = NSA_HEADS * HEAD_DIM
CMP_BLOCK = 32
CMP_STRIDE = 16
SLC_BLOCK = 64
SLC_TOP_N = 16
WINDOW = 512
RWKV_WIDTH = 1024
RWKV_HEAD_DIM = 64
RWKV_HEADS = RWKV_WIDTH // RWKV_HEAD_DIM
LORA = 64
NUM_BUCKETS = 32
MAX_DISTANCE = 1024
NORM_EPS = 1e-6
RWKV_GN_EPS = 64e-5

R_Q = 0
R_KV = R_Q + NSA_WIDTH
R_GATE = R_KV + 6 * NSA_KV_HEADS * HEAD_DIM
R_ZA = R_GATE + 3 * NSA_HEADS
R_FEAT = R_ZA + NSA_WIDTH
R_ZB = R_FEAT + 3 * RWKV_WIDTH + 2 * LORA
R_END = R_ZB + RWKV_WIDTH

P_Q = 0
P_KV = 1024
P_ZA = 2560
P_RKV = 3584
P_ZB = 6656
P_WDAD = 7680
P_GATE = 7808
NP = 8192

LANE = 128
TQ = 128
ND = 9
CH = 64
HPG = 4
GW = HPG * RWKV_HEAD_DIM
NEG = -1e30


def _bucket_thresholds():
    out = []
    for k in range(1, NUM_BUCKETS // 2):
        n = 16
        while n ** 8 < (16 ** 8) * (2 ** (3 * k)):
            n += 1
        out.append(n)
    return out


_THR = _bucket_thresholds()


def _mm(a, b, precision=None):
    return jnp.dot(a, b, preferred_element_type=F32, precision=precision)


def _mm_nt(a, b, precision=None):
    return lax.dot_general(a, b, (((1,), (1,)), ((), ())), preferred_element_type=F32,
                           precision=precision)


def _mm_tn(a, b, precision=None):
    return lax.dot_general(a, b, (((0,), (0,)), ((), ())), preferred_element_type=F32,
                           precision=precision)


def _split3(x):
    x1 = x.astype(BF16)
    r1 = x - x1.astype(F32)
    x2 = r1.astype(BF16)
    x3 = (r1 - x2.astype(F32)).astype(BF16)
    return x1, x2, x3


def _mm_split_lhs(a, b_exact):
    a1, a2, a3 = _split3(a)
    return _mm(a1, b_exact) + _mm(a2, b_exact) + _mm(a3, b_exact)


def _mm_split_rhs(a_exact, b):
    b1, b2, b3 = _split3(b)
    return _mm(a_exact, b1) + _mm(a_exact, b2) + _mm(a_exact, b3)


def _iota(shape, dim):
    return lax.broadcasted_iota(jnp.int32, shape, dim)


def _div_pow2(x, n):
    assert n & (n - 1) == 0
    return x >> (n.bit_length() - 1)


def _mod_pow2(x, n):
    assert n & (n - 1) == 0
    return x & (n - 1)


def _inproj_kernel(x_ref, g_ref, w_ref, o_ref, hn_ref):
    @pl.when(pl.program_id(1) == 0)
    def _():
        x = x_ref[...]
        ms = jnp.mean(x * x, axis=-1, keepdims=True)
        hn_ref[...] = (x * lax.rsqrt(ms + NORM_EPS) * g_ref[...]).astype(BF16)

    o_ref[...] = _mm(hn_ref[...], w_ref[...])


def _inproj(x2, g, w, tm, tn):
    m, d = x2.shape
    n = w.shape[1]
    return pl.pallas_call(
        _inproj_kernel,
        grid=(m // tm, n // tn),
        in_specs=[
            pl.BlockSpec((tm, d), lambda i, j: (i, 0)),
            pl.BlockSpec((1, d), lambda i, j: (0, 0)),
            pl.BlockSpec((d, tn), lambda i, j: (0, j)),
        ],
        out_specs=pl.BlockSpec((tm, tn), lambda i, j: (i, j)),
        out_shape=jax.ShapeDtypeStruct((m, n), F32),
        scratch_shapes=[pltpu.VMEM((tm, d), BF16)],
        compiler_params=pltpu.CompilerParams(
            dimension_semantics=("parallel", "arbitrary"),
            vmem_limit_bytes=56 * 1024 * 1024),
        name="inproj",
    )(x2, g, w)


def _bucket(n):
    n = jnp.maximum(n, 0)
    large = jnp.full(n.shape, NUM_BUCKETS // 2, jnp.int32)
    for thr in _THR:
        large = large + (n >= thr).astype(jnp.int32)
    return jnp.where(n < NUM_BUCKETS // 2, n, large)


def _lookup(bucket, tab_ref, h):
    out = jnp.zeros(bucket.shape, F32)
    for b in range(NUM_BUCKETS):
        out = jnp.where(bucket == b, tab_ref[b * NSA_HEADS + h], out)
    return out


def _bias_kernel(tab_ref, bc_ref, tp_ref):
    h = pl.program_id(0)
    s, nr = bc_ref.shape[1], bc_ref.shape[2]
    dist_c = _iota((s, nr), 0) - (_iota((s, nr), 1) * CMP_STRIDE + (CMP_BLOCK - 1))
    bc_ref[0] = _lookup(_bucket(dist_c), tab_ref, h)
    base = _iota((TQ, TQ), 0) - _iota((TQ, TQ), 1)
    for d in range(ND):
        tp_ref[0, d] = _lookup(_bucket(base + d * TQ), tab_ref, h)


def _bias(table_flat, s):
    nr = s // CMP_STRIDE
    return pl.pallas_call(
        _bias_kernel,
        grid=(NSA_HEADS,),
        in_specs=[pl.BlockSpec(memory_space=pltpu.SMEM)],
        out_specs=[
            pl.BlockSpec((1, s, nr), lambda h: (h, 0, 0)),
            pl.BlockSpec((1, ND, TQ, TQ), lambda h: (h, 0, 0, 0)),
        ],
        out_shape=[
            jax.ShapeDtypeStruct((NSA_HEADS, s, nr), F32),
            jax.ShapeDtypeStruct((NSA_HEADS, ND, TQ, TQ), F32),
        ],
        compiler_params=pltpu.CompilerParams(dimension_semantics=("arbitrary",)),
        name="bias",
    )(table_flat)


def _compress(kv_ref, pos_ref, w1_ref, w2_ref, nr):
    half = CMP_STRIDE * HEAD_DIM
    r = jnp.concatenate(
        [kv_ref[0, pl.ds(m, nr, stride=CMP_STRIDE), :] for m in range(CMP_STRIDE)], axis=1)
    a = _mm((r + pos_ref[0:1, :]).astype(BF16), w1_ref[0:half, :])
    b = _mm((r + pos_ref[1:2, :]).astype(BF16), w1_ref[half:2 * half, :])
    pre = a + pltpu.roll(b, nr - 1, 0)
    h1 = pre * jax.nn.sigmoid(pre)
    return _mm(h1.astype(BF16), w2_ref[...])


def _flash(q4, k_ref, v_ref, toep_ref, j_lo, j_hi, qi, mask_fn):
    rows = q4.shape[0]

    def body(j, carry):
        m, l, acc = carry
        k0 = pl.multiple_of(j * TQ, TQ)
        kt = k_ref[0, pl.ds(k0, TQ), :].astype(BF16)
        vt = v_ref[0, pl.ds(k0, TQ), :].astype(BF16)
        delta = jnp.minimum(qi - j, ND - 1)
        bias = jnp.concatenate([toep_ref[hg, delta] for hg in range(NSA_GROUP)], axis=0)
        msk = mask_fn(j)
        msk4 = jnp.concatenate([msk] * NSA_GROUP, axis=0)
        s = jnp.where(msk4, _mm_nt(q4, kt) + bias, NEG)
        m_new = jnp.maximum(m, jnp.max(s, axis=-1, keepdims=True))
        alpha = jnp.exp(m - m_new)
        p = jnp.where(msk4, jnp.exp(s - m_new), 0.0)
        l = alpha * l + jnp.sum(p, axis=-1, keepdims=True)
        acc = alpha * acc + _mm(p.astype(BF16), vt)
        return m_new, l, acc

    init = (jnp.full((rows, 1), NEG, F32), jnp.zeros((rows, 1), F32),
            jnp.zeros((rows, HEAD_DIM), F32))
    m, l, acc = lax.fori_loop(j_lo, j_hi + 1, body, init)
    return acc / jnp.maximum(l, 1e-30)


def _nsa_kernel(q_ref, kc_ref, vc_ref, ks_ref, vs_ref, kw_ref, vw_ref, gate_ref, za_ref,
                bc_ref, toep_ref, posk_ref, w1k_ref, w2k_ref, posv_ref, w1v_ref, w2v_ref,
                o_ref, kcs_ref, vcs_ref):
    qi = pl.program_id(2)
    s_len = kc_ref.shape[1]
    nr = s_len // CMP_STRIDE
    nb = s_len // SLC_BLOCK
    n_sel = min(SLC_TOP_N, nb)

    @pl.when(qi == 0)
    def _():
        kcs_ref[...] = _compress(kc_ref, posk_ref, w1k_ref, w2k_ref, nr).astype(BF16)
        vcs_ref[...] = _compress(vc_ref, posv_ref, w1v_ref, w2v_ref, nr).astype(BF16)

    q0 = qi * TQ
    q = q_ref[0] * (HEAD_DIM ** -0.5)
    qh = [q[:, h * HEAD_DIM:(h + 1) * HEAD_DIM].astype(BF16) for h in range(NSA_GROUP)]

    t_c = q0 + _iota((TQ, nr), 0)
    i_c = _iota((TQ, nr), 1)
    mask_c = (t_c - (i_c * CMP_STRIDE + (CMP_BLOCK - 1)) >= 0) & (i_c < nr - 1)
    kcs = kcs_ref[...]
    vcs = vcs_ref[...]
    o_c = []
    psum = jnp.zeros((TQ, nr), F32)
    for h in range(NSA_GROUP):
        lg = jnp.where(mask_c, _mm_nt(qh[h], kcs) + bc_ref[h], NEG)
        m = jnp.max(lg, axis=-1, keepdims=True)
        e = jnp.where(mask_c, jnp.exp(lg - m), 0.0)
        p = e / jnp.maximum(jnp.sum(e, axis=-1, keepdims=True), 1e-30)
        o_c.append(_mm(p.astype(BF16), vcs))
        psum = psum + p

    ov_i = _iota((nb, nr), 1) * CMP_STRIDE
    ov_j = _iota((nb, nr), 0) * SLC_BLOCK
    ov_t = ((ov_i < ov_j + SLC_BLOCK) & (ov_i + CMP_BLOCK > ov_j)).astype(F32)
    imp_t = _mm_nt(ov_t, psum, precision=HI)
    jb = _iota((nb, TQ), 0)
    cur = _div_pow2(q0 + _iota((nb, TQ), 1), SLC_BLOCK)
    forced = (jb == 0) | (jb == cur) | (jb == cur - 1)
    causal = jb <= cur
    score = jnp.where(forced, jnp.inf, jnp.where(causal, imp_t, -jnp.inf))
    rank = jnp.zeros((nb, TQ), jnp.int32)
    for jp in range(nb):
        sj = score[jp:jp + 1, :]
        beats = (sj > score) | ((sj == score) & (jb > jp))
        rank = rank + beats.astype(jnp.int32)
    sel_t = ((rank < n_sel) & causal).astype(BF16)

    q4 = jnp.concatenate(qh, axis=0)
    t_q = q0 + _iota((TQ, TQ), 0)

    def mask_slc(j):
        kpos = j * TQ + _iota((TQ, TQ), 1)
        e_j = (_iota((nb, TQ), 0) == _div_pow2(j * TQ + _iota((nb, TQ), 1), SLC_BLOCK)).astype(BF16)
        chosen = _mm_tn(sel_t, e_j)
        return (chosen > 0.5) & (kpos <= t_q)

    o_s = _flash(q4, ks_ref, vs_ref, toep_ref, 0, qi, qi, mask_slc)

    def mask_win(j):
        dist = t_q - (j * TQ + _iota((TQ, TQ), 1))
        return (dist >= 0) & (dist < WINDOW)

    o_w = _flash(q4, kw_ref, vw_ref, toep_ref, jnp.maximum(qi - WINDOW // TQ, 0), qi, qi, mask_win)

    gts = jax.nn.sigmoid(gate_ref[0])
    za = za_ref[0]
    for h in range(NSA_GROUP):
        o = (gts[:, h:h + 1] * o_c[h]
             + gts[:, NSA_GROUP + h:NSA_GROUP + h + 1] * o_s[h * TQ:(h + 1) * TQ]
             + gts[:, 2 * NSA_GROUP + h:2 * NSA_GROUP + h + 1] * o_w[h * TQ:(h + 1) * TQ])
        z = za[:, h * HEAD_DIM:(h + 1) * HEAD_DIM]
        o_ref[0, :, h * HEAD_DIM:(h + 1) * HEAD_DIM] = (o * (z * jax.nn.sigmoid(z))).astype(o_ref.dtype)


def _nsa(proj3, bias_c, toep, posk, w1k, w2k, posv, w1v, w2v):
    b, s, _ = proj3.shape
    nr = s // CMP_STRIDE
    gq = NSA_GROUP * HEAD_DIM

    def kvspec(idx):
        return pl.BlockSpec((1, s, HEAD_DIM), lambda bi, g, qi, idx=idx: (bi, 0, P_KV // HEAD_DIM + 2 * idx + g))

    def whole(a):
        return pl.BlockSpec(a.shape, lambda bi, g, qi, nd=a.ndim: (0,) * nd)

    in_specs = [
        pl.BlockSpec((1, TQ, gq), lambda bi, g, qi: (bi, qi, P_Q // gq + g)),
        kvspec(0), kvspec(1), kvspec(2), kvspec(3), kvspec(4), kvspec(5),
        pl.BlockSpec((1, TQ, LANE), lambda bi, g, qi: (bi, qi, P_GATE // LANE + g)),
        pl.BlockSpec((1, TQ, gq), lambda bi, g, qi: (bi, qi, P_ZA // gq + g)),
        pl.BlockSpec((NSA_GROUP, TQ, nr), lambda bi, g, qi: (g, qi, 0)),
        pl.BlockSpec((NSA_GROUP, ND, TQ, TQ), lambda bi, g, qi: (g, 0, 0, 0)),
        whole(posk), whole(w1k), whole(w2k), whole(posv), whole(w1v), whole(w2v),
    ]
    return pl.pallas_call(
        _nsa_kernel,
        grid=(b, NSA_KV_HEADS, s // TQ),
        in_specs=in_specs,
        out_specs=pl.BlockSpec((1, TQ, gq), lambda bi, g, qi: (bi, qi, g)),
        out_shape=jax.ShapeDtypeStruct((b, s, NSA_WIDTH), BF16),
        scratch_shapes=[pltpu.VMEM((nr, HEAD_DIM), BF16), pltpu.VMEM((nr, HEAD_DIM), BF16)],
        compiler_params=pltpu.CompilerParams(
            dimension_semantics=("parallel", "parallel", "arbitrary"),
            vmem_limit_bytes=48 * 1024 * 1024),
        name="nsa",
    )(proj3, proj3, proj3, proj3, proj3, proj3, proj3, proj3, proj3,
      bias_c, toep, posk, w1k, w2k, posv, w1v, w2v)


def _shift(x, prev_ref):
    out = jnp.where(_iota(x.shape, 0) == 0, prev_ref[...], pltpu.roll(x, 1, 0))
    prev_ref[...] = x[x.shape[0] - 1:]
    return out


def _rwkv_kernel(r_ref, k_ref, v_ref, wa_ref, zb_ref, vec_ref, muwa_ref, w2_ref, a2_ref,
                 o_ref, st_ref, pr_ref, pk_ref, pv_ref, pwa_ref):
    ti = pl.program_id(2)
    first = ti == 0
    tb = r_ref.shape[1]

    @pl.when(first)
    def _():
        for ref in (st_ref, pr_ref, pk_ref, pv_ref, pwa_ref):
            ref[...] = jnp.zeros_like(ref)

    vec = vec_ref[...]
    mu_r, mu_k, mu_v = vec[0:1], vec[1:2], vec[2:3]
    w0, a0, k_k, k_a = vec[3:4], vec[4:5], vec[5:6], vec[6:7]
    ln_w, ln_b, r_k = vec[7:8], vec[8:9], vec[9:10]

    def mix(ref, prev_ref, mu):
        x = ref[0]
        return x + mu * (_shift(x, prev_ref) - x)

    r = mix(r_ref, pr_ref, mu_r)
    k = mix(k_ref, pk_ref, mu_k)
    v = mix(v_ref, pv_ref, mu_v)
    wa = mix(wa_ref, pwa_ref, muwa_ref[...])

    w = -jax.nn.softplus(-(w0 + _mm(jnp.tanh(wa[:, :LORA]).astype(BF16), w2_ref[...]))) - 0.5
    lw = -jnp.exp(w)
    a_sig = jax.nn.sigmoid(a0 + _mm(wa[:, LORA:].astype(BF16), a2_ref[...]))

    seg = (_div_pow2(_iota((GW, GW), 0), RWKV_HEAD_DIM) == _div_pow2(_iota((GW, GW), 1), RWKV_HEAD_DIM))
    segf = seg.astype(F32)
    segb = seg.astype(BF16)
    kk = k * k_k
    kk = kk / jnp.maximum(jnp.sqrt(_mm_split_lhs(kk * kk, segb)), 1e-12)
    k = k * (1.0 + (a_sig - 1.0) * k_a)
    a_vec = -kk
    b_vec = kk * a_sig

    assert CH == RWKV_HEAD_DIM
    lane_s = _mod_pow2(_iota((CH, GW), 1), CH)
    row_t = _iota((CH, GW), 0)
    strict = lane_s < row_t
    incl = lane_s <= row_t
    eye = (lane_s == row_t).astype(F32)
    ti_r, ti_c = _iota((tb, tb), 0), _iota((tb, tb), 1)
    trib = ((ti_c <= ti_r) & (_div_pow2(ti_c, CH) == _div_pow2(ti_r, CH))).astype(BF16)
    cum_all = _mm_split_rhs(trib, lw)

    def bd(x):
        xb = x.astype(BF16)
        return jnp.where(seg, jnp.concatenate([xb] * HPG, axis=0), jnp.zeros((), BF16))

    n_chunks = tb // CH
    loc = []
    for c in range(n_chunks):
        sl = slice(c * CH, (c + 1) * CH)
        rc, kc, vc, ac, bc, lwc, cum = r[sl], k[sl], v[sl], a_vec[sl], b_vec[sl], lw[sl], cum_all[sl]
        tot = cum[CH - 1:CH]
        e_out = jnp.exp(-cum)
        e_end = jnp.exp(tot - cum)
        r_t = rc * jnp.exp(cum)
        a_t = ac * jnp.exp(cum - lwc)
        b_t = bc * e_out
        k_t = kc * e_out
        lhs = jnp.concatenate([a_t, r_t], axis=0).astype(BF16)
        aa = _mm_nt(lhs, jnp.concatenate([bd(b_t), bd(k_t)], axis=0))
        a_ab = jnp.where(strict, aa[:CH, :GW], 0.0)
        a_ak = jnp.where(strict, aa[:CH, GW:], 0.0)
        a_rb = jnp.where(incl, aa[CH:, :GW], 0.0)
        a_rk = jnp.where(incl, aa[CH:, GW:], 0.0)
        t_inv = eye + a_ab
        mpow = _mm(a_ab.astype(BF16), bd(a_ab))
        for _ in range(int(math.log2(CH)) - 1):
            res = _mm(jnp.concatenate([t_inv, mpow], axis=0).astype(BF16), bd(mpow))
            t_inv = t_inv + res[:CH]
            mpow = res[CH:]
        av = _mm(a_ak.astype(BF16), bd(vc))
        wu = _mm(t_inv.astype(BF16), jnp.concatenate([bd(a_t), bd(av)], axis=1))
        loc.append(dict(
            lhs=jnp.concatenate([wu[:, :GW], r_t], axis=0).astype(BF16), u_loc=wu[:, GW:],
            a_r=jnp.concatenate([a_rb, a_rk], axis=1).astype(BF16), bdv=bd(vc), vc=vc,
            bk_end=jnp.concatenate([bc * e_end, kc * e_end], axis=0).astype(BF16), dec=jnp.exp(tot)))

    ys = []
    g_state = st_ref[...]
    for c in range(n_chunks):
        lc = loc[c]
        x0 = _mm_nt(lc["lhs"], g_state.astype(BF16))
        u = x0[:CH] + lc["u_loc"]
        ys.append(x0[CH:] + _mm(lc["a_r"], jnp.concatenate([bd(u), lc["bdv"]], axis=0)))
        upd = _mm_tn(jnp.concatenate([u, lc["vc"]], axis=0).astype(BF16), lc["bk_end"])
        g_state = g_state * lc["dec"] + upd * segf
    st_ref[...] = g_state

    y = jnp.concatenate(ys, axis=0)
    inv_n = 1.0 / RWKV_HEAD_DIM
    mean = _mm_split_lhs(y, segb) * inv_n
    yc = y - mean
    var = _mm_split_lhs(yc * yc, segb) * inv_n
    yn = yc * lax.rsqrt(var + RWKV_GN_EPS) * ln_w + ln_b
    bonus = _mm_split_lhs(r * k * r_k, segb) * v
    zb = zb_ref[0]
    o_ref[0] = ((yn + bonus) * (zb * jax.nn.sigmoid(zb))).astype(o_ref.dtype)


def _rwkv(proj3, vecs, mu_wa, w2, a2, tb):
    b, s, _ = proj3.shape
    ng = RWKV_WIDTH // GW

    def col(off):
        return pl.BlockSpec((1, tb, GW), lambda bi, g, ti, off=off: (bi, ti, off // GW + g))

    in_specs = [
        col(P_RKV), col(P_RKV + RWKV_WIDTH), col(P_RKV + 2 * RWKV_WIDTH),
        pl.BlockSpec((1, tb, LANE), lambda bi, g, ti: (bi, ti, P_WDAD // LANE)),
        col(P_ZB),
        pl.BlockSpec((vecs.shape[0], GW), lambda bi, g, ti: (0, g)),
        pl.BlockSpec((1, LANE), lambda bi, g, ti: (0, 0)),
        pl.BlockSpec((LORA, GW), lambda bi, g, ti: (0, g)),
        pl.BlockSpec((LORA, GW), lambda bi, g, ti: (0, g)),
    ]
    return pl.pallas_call(
        _rwkv_kernel,
        grid=(b, ng, s // tb),
        in_specs=in_specs,
        out_specs=pl.BlockSpec((1, tb, GW), lambda bi, g, ti: (bi, ti, g)),
        out_shape=jax.ShapeDtypeStruct((b, s, RWKV_WIDTH), BF16),
        scratch_shapes=[pltpu.VMEM((GW, GW), F32), pltpu.VMEM((1, GW), F32), pltpu.VMEM((1, GW), F32),
                        pltpu.VMEM((1, GW), F32), pltpu.VMEM((1, LANE), F32)],
        compiler_params=pltpu.CompilerParams(
            dimension_semantics=("parallel", "parallel", "arbitrary")),
        name="rwkv",
    )(proj3, proj3, proj3, proj3, proj3, vecs, mu_wa, w2, a2)


def _outproj_kernel(ma_ref, mb_ref, wa_ref, wb_ref, x_ref, g_ref, o_ref):
    y = _mm(ma_ref[...], wa_ref[...]) + _mm(mb_ref[...], wb_ref[...])
    ms = jnp.mean(y * y, axis=-1, keepdims=True)
    o_ref[...] = x_ref[...] + y * lax.rsqrt(ms + NORM_EPS) * g_ref[...]


def _outproj(mix_a, mix_b, w_a, w_b, x2, g, tm):
    m, d = x2.shape
    ka, kb = mix_a.shape[1], mix_b.shape[1]
    return pl.pallas_call(
        _outproj_kernel,
        grid=(m // tm,),
        in_specs=[
            pl.BlockSpec((tm, ka), lambda i: (i, 0)),
            pl.BlockSpec((tm, kb), lambda i: (i, 0)),
            pl.BlockSpec((ka, d), lambda i: (0, 0)),
            pl.BlockSpec((kb, d), lambda i: (0, 0)),
            pl.BlockSpec((tm, d), lambda i: (i, 0)),
            pl.BlockSpec((1, d), lambda i: (0, 0)),
        ],
        out_specs=pl.BlockSpec((tm, d), lambda i: (i, 0)),
        out_shape=jax.ShapeDtypeStruct((m, d), F32),
        compiler_params=pltpu.CompilerParams(
            dimension_semantics=("parallel",), vmem_limit_bytes=56 * 1024 * 1024),
        name="outproj",
    )(mix_a, mix_b, w_a, w_b, x2, g)


def _permute_w_in(w):
    d = w.shape[0]
    gate = w[:, R_GATE:R_ZA].reshape(d, 3, NSA_KV_HEADS, NSA_GROUP)
    gate_blocks = []
    for g in range(NSA_KV_HEADS):
        gb = gate[:, :, g, :].reshape(d, 3 * NSA_GROUP)
        gate_blocks.append(jnp.pad(gb, ((0, 0), (0, LANE - 3 * NSA_GROUP))))
    parts = [
        w[:, R_Q:R_KV], w[:, R_KV:R_GATE], w[:, R_ZA:R_FEAT],
        w[:, R_FEAT:R_FEAT + 3 * RWKV_WIDTH], w[:, R_ZB:R_END],
        w[:, R_FEAT + 3 * RWKV_WIDTH:R_ZB],
    ] + gate_blocks
    wp = jnp.concatenate(parts, axis=1)
    return jnp.pad(wp, ((0, 0), (0, NP - wp.shape[1]))).astype(BF16)


def _block(x, pre_norm_g, w_in, rel_bias_table, cmp_pos_k, cmp_pos_v, cmp_k_w1, cmp_k_w2, cmp_v_w1,
           cmp_v_w2, rwkv_mu, rwkv_w0, rwkv_w2, rwkv_a0, rwkv_a2, rwkv_k_k, rwkv_k_a, rwkv_r_k,
           rwkv_ln_w, rwkv_ln_b, w_out, post_norm_g):
    b, s, d = x.shape
    x2 = x.reshape(b * s, d)
    tm = min(1024, b * s)
    proj = _inproj(x2, pre_norm_g.reshape(1, d), _permute_w_in(w_in), tm, 1024)
    proj3 = proj.reshape(b, s, NP)

    bias_c, toep = _bias(rel_bias_table.reshape(-1), s)
    half = CMP_STRIDE * HEAD_DIM
    mix_a = _nsa(proj3, bias_c, toep,
                 cmp_pos_k.reshape(2, half), cmp_k_w1.astype(BF16), cmp_k_w2.astype(BF16),
                 cmp_pos_v.reshape(2, half), cmp_v_w1.astype(BF16), cmp_v_w2.astype(BF16))

    w3 = 3 * RWKV_WIDTH
    vec_rows = [rwkv_mu[:RWKV_WIDTH], rwkv_mu[RWKV_WIDTH:2 * RWKV_WIDTH], rwkv_mu[2 * RWKV_WIDTH:w3],
                rwkv_w0, rwkv_a0, rwkv_k_k, rwkv_k_a, rwkv_ln_w, rwkv_ln_b, rwkv_r_k.reshape(-1)]
    vecs = jnp.stack(vec_rows + [jnp.zeros_like(rwkv_w0)] * (16 - len(vec_rows)), axis=0)
    mix_b = _rwkv(proj3, vecs, rwkv_mu[w3:].reshape(1, 2 * LORA), rwkv_w2.astype(BF16),
                  rwkv_a2.astype(BF16), min(256, s))

    w_o = w_out.astype(BF16)
    out = _outproj(mix_a.reshape(b * s, NSA_WIDTH), mix_b.reshape(b * s, RWKV_WIDTH),
                   w_o[:NSA_WIDTH], w_o[NSA_WIDTH:], x2, post_norm_g.reshape(1, d), min(512, b * s))
    return out.reshape(b, s, d)


def kernel(x, pre_norm_g, w_in, rel_bias_table, cmp_pos_k, cmp_pos_v, cmp_k_w1, cmp_k_w2, cmp_v_w1,
           cmp_v_w2, rwkv_mu, rwkv_w0, rwkv_w2, rwkv_a0, rwkv_a2, rwkv_k_k, rwkv_k_a, rwkv_r_k,
           rwkv_ln_w, rwkv_ln_b, w_out, post_norm_g):
    h = x
    for l in range(pre_norm_g.shape[0]):
        h = _block(h, pre_norm_g[l], w_in[l], rel_bias_table, cmp_pos_k[l], cmp_pos_v[l], cmp_k_w1[l],
                   cmp_k_w2[l], cmp_v_w1[l], cmp_v_w2[l], rwkv_mu[l], rwkv_w0[l], rwkv_w2[l],
                   rwkv_a0[l], rwkv_a2[l], rwkv_k_k[l], rwkv_k_a[l], rwkv_r_k[l], rwkv_ln_w[l],
                   rwkv_ln_b[l], w_out[l], post_norm_g[l])
    return h
```

```python
import functools
import math

import numpy as np
import jax
import jax.numpy as jnp
from jax import lax
from jax.experimental import pallas as pl
from jax.experimental.pallas import tpu as pltpu

F32 = jnp.float32
BF16 = jnp.bfloat16
HI = lax.Precision.HIGHEST

D_MODEL = 2048
NSA_HEADS = 8
NSA_KV_HEADS = 2
NSA_GROUP = NSA_HEADS // NSA_KV_HEADS
HEAD_DIM = 128
NSA_WIDTH = NSA_HEADS * HEAD_DIM
CMP_BLOCK = 32
CMP_STRIDE = 16
SLC_BLOCK = 64
SLC_TOP_N = 16
WINDOW = 512
RWKV_WIDTH = 1024
RWKV_HEAD_DIM = 64
RWKV_HEADS = RWKV_WIDTH // RWKV_HEAD_DIM
LORA = 64
NUM_BUCKETS = 32
MAX_DISTANCE = 1024
NORM_EPS = 1e-6
RWKV_GN_EPS = 64e-5

R_Q = 0
R_KV = R_Q + NSA_WIDTH
R_GATE = R_KV + 6 * NSA_KV_HEADS * HEAD_DIM
R_ZA = R_GATE + 3 * NSA_HEADS
R_FEAT = R_ZA + NSA_WIDTH
R_ZB = R_FEAT + 3 * RWKV_WIDTH + 2 * LORA
R_END = R_ZB + RWKV_WIDTH

P_Q = 0
P_KV = 1024
P_ZA = 2560
P_RKV = 3584
P_ZB = 6656
P_WDAD = 7680
P_GATE = 7808
NP = 8192

LANE = 128
TQ = 256
TB = 128
ND = 9
CH = 64
HPG = 4
GW = HPG * RWKV_HEAD_DIM
NEG = -1e30


def _bucket_thresholds():
    out = []
    for k in range(1, NUM_BUCKETS // 2):
        n = 16
        while n ** 8 < (16 ** 8) * (2 ** (3 * k)):
            n += 1
        out.append(n)
    return out


_THR = _bucket_thresholds()


def _mm(a, b, precision=None):
    return jnp.dot(a, b, preferred_element_type=F32, precision=precision)


def _mm_nt(a, b, precision=None):
    return lax.dot_general(a, b, (((1,), (1,)), ((), ())), preferred_element_type=F32,
                           precision=precision)


def _mm_tn(a, b, precision=None):
    return lax.dot_general(a, b, (((0,), (0,)), ((), ())), preferred_element_type=F32,
                           precision=precision)


def _split3(x):
    x1 = x.astype(BF16)
    r1 = x - x1.astype(F32)
    x2 = r1.astype(BF16)
    x3 = (r1 - x2.astype(F32)).astype(BF16)
    return x1, x2, x3


def _mm_split_lhs(a, b_exact):
    a1, a2, a3 = _split3(a)
    return _mm(a1, b_exact) + _mm(a2, b_exact) + _mm(a3, b_exact)


def _mm_split_rhs(a_exact, b):
    b1, b2, b3 = _split3(b)
    return _mm(a_exact, b1) + _mm(a_exact, b2) + _mm(a_exact, b3)


def _iota(shape, dim):
    return lax.broadcasted_iota(jnp.int32, shape, dim)


def _div_pow2(x, n):
    assert n & (n - 1) == 0
    return x >> (n.bit_length() - 1)


def _mod_pow2(x, n):
    assert n & (n - 1) == 0
    return x & (n - 1)


def _inproj_kernel(x_ref, g_ref, w_ref, o_ref, hn_ref):
    @pl.when(pl.program_id(1) == 0)
    def _():
        x = x_ref[...]
        ms = jnp.mean(x * x, axis=-1, keepdims=True)
        hn_ref[...] = (x * lax.rsqrt(ms + NORM_EPS) * g_ref[...]).astype(BF16)

    o_ref[...] = _mm(hn_ref[...], w_ref[...])


def _inproj(x2, g, w, tm, tn):
    m, d = x2.shape
    n = w.shape[1]
    return pl.pallas_call(
        _inproj_kernel,
        grid=(m // tm, n // tn),
        in_specs=[
            pl.BlockSpec((tm, d), lambda i, j: (i, 0)),
            pl.BlockSpec((1, d), lambda i, j: (0, 0)),
            pl.BlockSpec((d, tn), lambda i, j: (0, j)),
        ],
        out_specs=pl.BlockSpec((tm, tn), lambda i, j: (i, j)),
        out_shape=jax.ShapeDtypeStruct((m, n), F32),
        scratch_shapes=[pltpu.VMEM((tm, d), BF16)],
        compiler_params=pltpu.CompilerParams(
            dimension_semantics=("parallel", "arbitrary"),
            vmem_limit_bytes=56 * 1024 * 1024),
        name="inproj",
    )(x2, g, w)


def _bucket(n):
    n = jnp.maximum(n, 0)
    large = jnp.full(n.shape, NUM_BUCKETS // 2, jnp.int32)
    for thr in _THR:
        large = large + (n >= thr).astype(jnp.int32)
    return jnp.where(n < NUM_BUCKETS // 2, n, large)


def _lookup(bucket, tab_ref, h):
    out = jnp.zeros(bucket.shape, F32)
    for b in range(NUM_BUCKETS):
        out = jnp.where(bucket == b, tab_ref[b * NSA_HEADS + h], out)
    return out


def _bias_kernel(tab_ref, bc_ref, tp_ref):
    h = pl.program_id(0)
    s, nr = bc_ref.shape[1], bc_ref.shape[2]
    dist_c = _iota((s, nr), 0) - (_iota((s, nr), 1) * CMP_STRIDE + (CMP_BLOCK - 1))
    bc_ref[0] = _lookup(_bucket(dist_c), tab_ref, h)
    base = _iota((TB, TB), 0) - _iota((TB, TB), 1)
    for d in range(ND):
        tp_ref[0, d] = _lookup(_bucket(base + d * TB), tab_ref, h)


def _bias(table_flat, s):
    nr = s // CMP_STRIDE
    return pl.pallas_call(
        _bias_kernel,
        grid=(NSA_HEADS,),
        in_specs=[pl.BlockSpec(memory_space=pltpu.SMEM)],
        out_specs=[
            pl.BlockSpec((1, s, nr), lambda h: (h, 0, 0)),
            pl.BlockSpec((1, ND, TB, TB), lambda h: (h, 0, 0, 0)),
        ],
        out_shape=[
            jax.ShapeDtypeStruct((NSA_HEADS, s, nr), F32),
            jax.ShapeDtypeStruct((NSA_HEADS, ND, TB, TB), F32),
        ],
        compiler_params=pltpu.CompilerParams(dimension_semantics=("arbitrary",)),
        name="bias",
    )(table_flat)


def _compress(kv_ref, pos_ref, w1_ref, w2_ref, nr):
    half = CMP_STRIDE * HEAD_DIM
    r = jnp.concatenate(
        [kv_ref[0, pl.ds(m, nr, stride=CMP_STRIDE), :] for m in range(CMP_STRIDE)], axis=1)
    a = _mm((r + pos_ref[0:1, :]).astype(BF16), w1_ref[0:half, :])
    b = _mm((r + pos_ref[1:2, :]).astype(BF16), w1_ref[half:2 * half, :])
    pre = a + pltpu.roll(b, nr - 1, 0)
    h1 = pre * jax.nn.sigmoid(pre)
    return _mm(h1.astype(BF16), w2_ref[...])


def _bias_tile(toep_ref, d_tiles):
    sub = TQ // TB
    rows = []
    for h in range(NSA_GROUP):
        for ri in range(sub):
            rows.append(jnp.concatenate(
                [toep_ref[h, jnp.clip(d_tiles + ri - ci, 0, ND - 1)] for ci in range(sub)], axis=1))
    return jnp.concatenate(rows, axis=0)


def _add_shared(s, mask_add):
    n = s.shape[-1]
    return (s.reshape(NSA_GROUP, TQ, n) + mask_add[None]).reshape(NSA_GROUP * TQ, n)


def _fold_lanes(x, op):
    out = x[:, :LANE]
    for c in range(1, x.shape[1] // LANE):
        out = op(out, x[:, c * LANE:(c + 1) * LANE])
    return out


def _nsa_kernel(q_ref, kc_ref, vc_ref, ks_ref, vs_ref, kw_ref, vw_ref, gate_ref, za_ref,
                bc_ref, toep_ref, posk_ref, w1k_ref, w2k_ref, posv_ref, w1v_ref, w2v_ref,
                o_ref, kcs_ref, vcs_ref, s_ref, am_ref):
    qi = pl.program_id(2)
    s_len = kc_ref.shape[1]
    nr = s_len // CMP_STRIDE
    nb = s_len // SLC_BLOCK
    n_sel = min(SLC_TOP_N, nb)

    @pl.when(qi == 0)
    def _():
        kcs_ref[...] = _compress(kc_ref, posk_ref, w1k_ref, w2k_ref, nr).astype(BF16)
        vcs_ref[...] = _compress(vc_ref, posv_ref, w1v_ref, w2v_ref, nr).astype(BF16)

    q0 = qi * TQ
    rows = NSA_GROUP * TQ
    q = q_ref[0] * (HEAD_DIM ** -0.5)
    q4 = jnp.concatenate([q[:, h * HEAD_DIM:(h + 1) * HEAD_DIM] for h in range(NSA_GROUP)],
                         axis=0).astype(BF16)

    t_c = q0 + _iota((TQ, nr), 0)
    i_c = _iota((TQ, nr), 1)
    mask_c = (t_c - (i_c * CMP_STRIDE + (CMP_BLOCK - 1)) >= 0) & (i_c < nr - 1)
    lg = _add_shared(_mm_nt(q4, kcs_ref[...]) + bc_ref[...].reshape(rows, nr),
                     jnp.where(mask_c, 0.0, NEG))
    keep = lg > 0.5 * NEG
    e = jnp.where(keep, jnp.exp(lg - jnp.max(lg, axis=-1, keepdims=True)), 0.0)
    p = e / jnp.maximum(jnp.sum(e, axis=-1, keepdims=True), 1e-30)
    o_c = _mm(p.astype(BF16), vcs_ref[...])
    psum = p[:TQ]
    for h in range(1, NSA_GROUP):
        psum = psum + p[h * TQ:(h + 1) * TQ]

    ov_i = _iota((nb, nr), 1) * CMP_STRIDE
    ov_j = _iota((nb, nr), 0) * SLC_BLOCK
    ov_t = ((ov_i < ov_j + SLC_BLOCK) & (ov_i + CMP_BLOCK > ov_j)).astype(BF16)
    p1, p2, p3 = _split3(psum)
    imp_t = _mm_nt(ov_t, p1) + _mm_nt(ov_t, p2) + _mm_nt(ov_t, p3)
    jb = _iota((nb, TQ), 0)
    cur = _div_pow2(q0 + _iota((nb, TQ), 1), SLC_BLOCK)
    forced = (jb == 0) | (jb == cur) | (jb == cur - 1)
    causal = jb <= cur
    score = jnp.where(forced, jnp.inf, jnp.where(causal, imp_t, -jnp.inf))
    rank = jnp.zeros((nb, TQ), jnp.int32)
    for jp in range(nb):
        sj = score[jp:jp + 1, :]
        beats = (sj > score) | ((sj == score) & (jb > jp))
        rank = rank + beats.astype(jnp.int32)
    sel_t = ((rank < n_sel) & causal).astype(BF16)

    sub = TQ // TB
    t_q = q0 + _iota((TQ, TQ), 0)
    for j in range(s_len // TQ):
        @pl.when(j <= qi)
        def _(j=j):
            kpos = j * TQ + _iota((TQ, TQ), 1)
            e_j = (_iota((nb, TQ), 0) == _div_pow2(j * TQ + _iota((nb, TQ), 1), SLC_BLOCK)).astype(BF16)
            chosen = _mm_tn(sel_t, e_j)
            am_ref[j] = jnp.where((chosen > 0.5) & (kpos <= t_q), 0.0, NEG)

    def slc_logits(j, macc):
        kt = ks_ref[0, pl.ds(pl.multiple_of(j * TQ, TQ), TQ), :].astype(BF16)
        s = _add_shared(_mm_nt(q4, kt) + _bias_tile(toep_ref, (qi - j) * sub), am_ref[j])
        s_ref[j] = s
        return jnp.maximum(macc, _fold_lanes(s, jnp.maximum))

    macc = lax.fori_loop(0, qi + 1, slc_logits, jnp.full((rows, LANE), NEG, F32))
    m_s = jnp.max(macc, axis=-1, keepdims=True)

    def slc_values(j, carry):
        lacc, acc = carry
        p_j = jnp.exp(s_ref[j] - m_s)
        vt = vs_ref[0, pl.ds(pl.multiple_of(j * TQ, TQ), TQ), :].astype(BF16)
        return lacc + _fold_lanes(p_j, jnp.add), acc + _mm(p_j.astype(BF16), vt)

    lacc, acc = lax.fori_loop(0, qi + 1, slc_values,
                              (jnp.zeros((rows, LANE), F32), jnp.zeros((rows, HEAD_DIM), F32)))
    o_s = acc / jnp.maximum(jnp.sum(lacc, axis=-1, keepdims=True), 1e-30)

    n_band = WINDOW // TQ + 1
    s_w, v_w = [], []
    for c in range(n_band):
        j = qi - (n_band - 1) + c
        jc = jnp.maximum(j, 0)
        k0 = pl.multiple_of(jc * TQ, TQ)
        dist = t_q - (jc * TQ + _iota((TQ, TQ), 1))
        ok = (dist >= 0) & (dist < WINDOW) & (j >= 0)
        kt = kw_ref[0, pl.ds(k0, TQ), :].astype(BF16)
        s_w.append(_add_shared(_mm_nt(q4, kt) + _bias_tile(toep_ref, (qi - jc) * sub),
                               jnp.where(ok, 0.0, NEG)))
        v_w.append(vw_ref[0, pl.ds(k0, TQ), :].astype(BF16))
    m_w = s_w[0]
    for c in range(1, n_band):
        m_w = jnp.maximum(m_w, s_w[c])
    m_w = jnp.max(m_w, axis=-1, keepdims=True)
    l_w = jnp.zeros((rows, TQ), F32)
    acc_w = jnp.zeros((rows, HEAD_DIM), F32)
    for c in range(n_band):
        p_c = jnp.exp(s_w[c] - m_w)
        l_w = l_w + p_c
        acc_w = acc_w + _mm(p_c.astype(BF16), v_w[c])
    o_w = acc_w / jnp.maximum(jnp.sum(l_w, axis=-1, keepdims=True), 1e-30)

    gts = jax.nn.sigmoid(gate_ref[0])
    za = za_ref[0]
    for h in range(NSA_GROUP):
        hr = slice(h * TQ, (h + 1) * TQ)
        o = (gts[:, h:h + 1] * o_c[hr]
             + gts[:, NSA_GROUP + h:NSA_GROUP + h + 1] * o_s[hr]
             + gts[:, 2 * NSA_GROUP + h:2 * NSA_GROUP + h + 1] * o_w[hr])
        z = za[:, h * HEAD_DIM:(h + 1) * HEAD_DIM]
        o_ref[0, :, h * HEAD_DIM:(h + 1) * HEAD_DIM] = (o * (z * jax.nn.sigmoid(z))).astype(o_ref.dtype)


def _nsa(proj3, bias_c, toep, posk, w1k, w2k, posv, w1v, w2v):
    b, s, _ = proj3.shape
    nr = s // CMP_STRIDE
    gq = NSA_GROUP * HEAD_DIM

    def kvspec(idx):
        return pl.BlockSpec((1, s, HEAD_DIM), lambda bi, g, qi, idx=idx: (bi, 0, P_KV // HEAD_DIM + 2 * idx + g))

    def whole(a):
        return pl.BlockSpec(a.shape, lambda bi, g, qi, nd=a.ndim: (0,) * nd)

    in_specs = [
        pl.BlockSpec((1, TQ, gq), lambda bi, g, qi: (bi, qi, P_Q // gq + g)),
        kvspec(0), kvspec(1), kvspec(2), kvspec(3), kvspec(4), kvspec(5),
        pl.BlockSpec((1, TQ, LANE), lambda bi, g, qi: (bi, qi, P_GATE // LANE + g)),
        pl.BlockSpec((1, TQ, gq), lambda bi, g, qi: (bi, qi, P_ZA // gq + g)),
        pl.BlockSpec((NSA_GROUP, TQ, nr), lambda bi, g, qi: (g, qi, 0)),
        pl.BlockSpec((NSA_GROUP, ND, TB, TB), lambda bi, g, qi: (g, 0, 0, 0)),
        whole(posk), whole(w1k), whole(w2k), whole(posv), whole(w1v), whole(w2v),
    ]
    return pl.pallas_call(
        _nsa_kernel,
        grid=(b, NSA_KV_HEADS, s // TQ),
        in_specs=in_specs,
        out_specs=pl.BlockSpec((1, TQ, gq), lambda bi, g, qi: (bi, qi, g)),
        out_shape=jax.ShapeDtypeStruct((b, s, NSA_WIDTH), BF16),
        scratch_shapes=[pltpu.VMEM((nr, HEAD_DIM), BF16), pltpu.VMEM((nr, HEAD_DIM), BF16),
                        pltpu.VMEM((s // TQ, NSA_GROUP * TQ, TQ), F32),
                        pltpu.VMEM((s // TQ, TQ, TQ), F32)],
        compiler_params=pltpu.CompilerParams(
            dimension_semantics=("parallel", "parallel", "arbitrary"),
            vmem_limit_bytes=56 * 1024 * 1024),
        name="nsa",
    )(proj3, proj3, proj3, proj3, proj3, proj3, proj3, proj3, proj3,
      bias_c, toep, posk, w1k, w2k, posv, w1v, w2v)


def _shift(x, prev_ref):
    out = jnp.where(_iota(x.shape, 0) == 0, prev_ref[...], pltpu.roll(x, 1, 0))
    prev_ref[...] = x[x.shape[0] - 1:]
    return out


def _rwkv_kernel(r_ref, k_ref, v_ref, wa_ref, zb_ref, vec_ref, muwa_ref, w2_ref, a2_ref,
                 o_ref, st_ref, pr_ref, pk_ref, pv_ref, pwa_ref):
    ti = pl.program_id(2)
    first = ti == 0
    tb = r_ref.shape[1]

    @pl.when(first)
    def _():
        for ref in (st_ref, pr_ref, pk_ref, pv_ref, pwa_ref):
            ref[...] = jnp.zeros_like(ref)

    vec = vec_ref[...]
    mu_r, mu_k, mu_v = vec[0:1], vec[1:2], vec[2:3]
    w0, a0, k_k, k_a = vec[3:4], vec[4:5], vec[5:6], vec[6:7]
    ln_w, ln_b, r_k = vec[7:8], vec[8:9], vec[9:10]

    def mix(ref, prev_ref, mu):
        x = ref[0]
        return x + mu * (_shift(x, prev_ref) - x)

    r = mix(r_ref, pr_ref, mu_r)
    k = mix(k_ref, pk_ref, mu_k)
    v = mix(v_ref, pv_ref, mu_v)
    wa = mix(wa_ref, pwa_ref, muwa_ref[...])

    w = -jax.nn.softplus(-(w0 + _mm(jnp.tanh(wa[:, :LORA]).astype(BF16), w2_ref[...]))) - 0.5
    lw = -jnp.exp(w)
    a_sig = jax.nn.sigmoid(a0 + _mm(wa[:, LORA:].astype(BF16), a2_ref[...]))

    seg = (_div_pow2(_iota((GW, GW), 0), RWKV_HEAD_DIM) == _div_pow2(_iota((GW, GW), 1), RWKV_HEAD_DIM))
    segf = seg.astype(F32)
    segb = seg.astype(BF16)
    kk = k * k_k
    kk = kk / jnp.maximum(jnp.sqrt(_mm_split_lhs(kk * kk, segb)), 1e-12)
    k = k * (1.0 + (a_sig - 1.0) * k_a)
    a_vec = -kk
    b_vec = kk * a_sig

    assert CH == RWKV_HEAD_DIM
    lane_s = _mod_pow2(_iota((CH, GW), 1), CH)
    row_t = _iota((CH, GW), 0)
    strict = lane_s < row_t
    incl = lane_s <= row_t
    eye = (lane_s == row_t).astype(F32)
    ti_r, ti_c = _iota((tb, tb), 0), _iota((tb, tb), 1)
    trib = ((ti_c <= ti_r) & (_div_pow2(ti_c, CH) == _div_pow2(ti_r, CH))).astype(BF16)
    cum_all = _mm_split_rhs(trib, lw)

    def bd(x):
        xb = x.astype(BF16)
        return jnp.where(seg, jnp.concatenate([xb] * HPG, axis=0), jnp.zeros((), BF16))

    n_chunks = tb // CH
    loc = []
    for c in range(n_chunks):
        sl = slice(c * CH, (c + 1) * CH)
        rc, kc, vc, ac, bc, lwc, cum = r[sl], k[sl], v[sl], a_vec[sl], b_vec[sl], lw[sl], cum_all[sl]
        tot = cum[CH - 1:CH]
        e_out = jnp.exp(-cum)
        e_end = jnp.exp(tot - cum)
        r_t = rc * jnp.exp(cum)
        a_t = ac * jnp.exp(cum - lwc)
        b_t = bc * e_out
        k_t = kc * e_out
        lhs = jnp.concatenate([a_t, r_t], axis=0).astype(BF16)
        aa = _mm_nt(lhs, jnp.concatenate([bd(b_t), bd(k_t)], axis=0))
        a_ab = jnp.where(strict, aa[:CH, :GW], 0.0)
        a_ak = jnp.where(strict, aa[:CH, GW:], 0.0)
        a_rb = jnp.where(incl, aa[CH:, :GW], 0.0)
        a_rk = jnp.where(incl, aa[CH:, GW:], 0.0)
        t_inv = eye + a_ab
        mpow = _mm(a_ab.astype(BF16), bd(a_ab))
        for _ in range(int(math.log2(CH)) - 1):
            res = _mm(jnp.concatenate([t_inv, mpow], axis=0).astype(BF16), bd(mpow))
            t_inv = t_inv + res[:CH]
            mpow = res[CH:]
        av = _mm(a_ak.astype(BF16), bd(vc))
        wu = _mm(t_inv.astype(BF16), jnp.concatenate([bd(a_t), bd(av)], axis=1))
        loc.append(dict(
            lhs=jnp.concatenate([wu[:, :GW], r_t], axis=0).astype(BF16), u_loc=wu[:, GW:],
            a_r=jnp.concatenate([a_rb, a_rk], axis=1).astype(BF16), bdv=bd(vc), vc=vc,
            bk_end=jnp.concatenate([bc * e_end, kc * e_end], axis=0).astype(BF16), dec=jnp.exp(tot)))

    ys = []
    g_state = st_ref[...]
    for c in range(n_chunks):
        lc = loc[c]
        x0 = _mm_nt(lc["lhs"], g_state.astype(BF16))
        u = x0[:CH] + lc["u_loc"]
        ys.append(x0[CH:] + _mm(lc["a_r"], jnp.concatenate([bd(u), lc["bdv"]], axis=0)))
        upd = _mm_tn(jnp.concatenate([u, lc["vc"]], axis=0).astype(BF16), lc["bk_end"])
        g_state = g_state * lc["dec"] + upd * segf
    st_ref[...] = g_state

    y = jnp.concatenate(ys, axis=0)
    inv_n = 1.0 / RWKV_HEAD_DIM
    mean = _mm_split_lhs(y, segb) * inv_n
    yc = y - mean
    var = _mm_split_lhs(yc * yc, segb) * inv_n
    yn = yc * lax.rsqrt(var + RWKV_GN_EPS) * ln_w + ln_b
    bonus = _mm_split_lhs(r * k * r_k, segb) * v
    zb = zb_ref[0]
    o_ref[0] = ((yn + bonus) * (zb * jax.nn.sigmoid(zb))).astype(o_ref.dtype)


def _rwkv(proj3, vecs, mu_wa, w2, a2, tb):
    b, s, _ = proj3.shape
    ng = RWKV_WIDTH // GW

    def col(off):
        return pl.BlockSpec((1, tb, GW), lambda bi, g, ti, off=off: (bi, ti, off // GW + g))

    in_specs = [
        col(P_RKV), col(P_RKV + RWKV_WIDTH), col(P_RKV + 2 * RWKV_WIDTH),
        pl.BlockSpec((1, tb, LANE), lambda bi, g, ti: (bi, ti, P_WDAD // LANE)),
        col(P_ZB),
        pl.BlockSpec((vecs.shape[0], GW), lambda bi, g, ti: (0, g)),
        pl.BlockSpec((1, LANE), lambda bi, g, ti: (0, 0)),
        pl.BlockSpec((LORA, GW), lambda bi, g, ti: (0, g)),
        pl.BlockSpec((LORA, GW), lambda bi, g, ti: (0, g)),
    ]
    return pl.pallas_call(
        _rwkv_kernel,
        grid=(b, ng, s // tb),
        in_specs=in_specs,
        out_specs=pl.BlockSpec((1, tb, GW), lambda bi, g, ti: (bi, ti, g)),
        out_shape=jax.ShapeDtypeStruct((b, s, RWKV_WIDTH), BF16),
        scratch_shapes=[pltpu.VMEM((GW, GW), F32), pltpu.VMEM((1, GW), F32), pltpu.VMEM((1, GW), F32),
                        pltpu.VMEM((1, GW), F32), pltpu.VMEM((1, LANE), F32)],
        compiler_params=pltpu.CompilerParams(
            dimension_semantics=("parallel", "parallel", "arbitrary")),
        name="rwkv",
    )(proj3, proj3, proj3, proj3, proj3, vecs, mu_wa, w2, a2)


def _outproj_kernel(ma_ref, mb_ref, wa_ref, wb_ref, x_ref, g_ref, o_ref):
    y = _mm(ma_ref[...], wa_ref[...]) + _mm(mb_ref[...], wb_ref[...])
    ms = jnp.mean(y * y, axis=-1, keepdims=True)
    o_ref[...] = x_ref[...] + y * lax.rsqrt(ms + NORM_EPS) * g_ref[...]


def _outproj(mix_a, mix_b, w_a, w_b, x2, g, tm):
    m, d = x2.shape
    ka, kb = mix_a.shape[1], mix_b.shape[1]
    return pl.pallas_call(
        _outproj_kernel,
        grid=(m // tm,),
        in_specs=[
            pl.BlockSpec((tm, ka), lambda i: (i, 0)),
            pl.BlockSpec((tm, kb), lambda i: (i, 0)),
            pl.BlockSpec((ka, d), lambda i: (0, 0)),
            pl.BlockSpec((kb, d), lambda i: (0, 0)),
            pl.BlockSpec((tm, d), lambda i: (i, 0)),
            pl.BlockSpec((1, d), lambda i: (0, 0)),
        ],
        out_specs=pl.BlockSpec((tm, d), lambda i: (i, 0)),
        out_shape=jax.ShapeDtypeStruct((m, d), F32),
        compiler_params=pltpu.CompilerParams(
            dimension_semantics=("parallel",), vmem_limit_bytes=56 * 1024 * 1024),
        name="outproj",
    )(mix_a, mix_b, w_a, w_b, x2, g)


def _permute_w_in(w):
    d = w.shape[0]
    gate = w[:, R_GATE:R_ZA].reshape(d, 3, NSA_KV_HEADS, NSA_GROUP)
    gate_blocks = []
    for g in range(NSA_KV_HEADS):
        gb = gate[:, :, g, :].reshape(d, 3 * NSA_GROUP)
        gate_blocks.append(jnp.pad(gb, ((0, 0), (0, LANE - 3 * NSA_GROUP))))
    parts = [
        w[:, R_Q:R_KV], w[:, R_KV:R_GATE], w[:, R_ZA:R_FEAT],
        w[:, R_FEAT:R_FEAT + 3 * RWKV_WIDTH], w[:, R_ZB:R_END],
        w[:, R_FEAT + 3 * RWKV_WIDTH:R_ZB],
    ] + gate_blocks
    wp = jnp.concatenate(parts, axis=1)
    return jnp.pad(wp, ((0, 0), (0, NP - wp.shape[1]))).astype(BF16)


def _block(x, pre_norm_g, w_in, rel_bias_table, cmp_pos_k, cmp_pos_v, cmp_k_w1, cmp_k_w2, cmp_v_w1,
           cmp_v_w2, rwkv_mu, rwkv_w0, rwkv_w2, rwkv_a0, rwkv_a2, rwkv_k_k, rwkv_k_a, rwkv_r_k,
           rwkv_ln_w, rwkv_ln_b, w_out, post_norm_g):
    b, s, d = x.shape
    x2 = x.reshape(b * s, d)
    tm = min(1024, b * s)
    proj = _inproj(x2, pre_norm_g.reshape(1, d), _permute_w_in(w_in), tm, 1024)
    proj3 = proj.reshape(b, s, NP)

    bias_c, toep = _bias(rel_bias_table.reshape(-1), s)
    half = CMP_STRIDE * HEAD_DIM
    mix_a = _nsa(proj3, bias_c, toep,
                 cmp_pos_k.reshape(2, half), cmp_k_w1.astype(BF16), cmp_k_w2.astype(BF16),
                 cmp_pos_v.reshape(2, half), cmp_v_w1.astype(BF16), cmp_v_w2.astype(BF16))

    w3 = 3 * RWKV_WIDTH
    vec_rows = [rwkv_mu[:RWKV_WIDTH], rwkv_mu[RWKV_WIDTH:2 * RWKV_WIDTH], rwkv_mu[2 * RWKV_WIDTH:w3],
                rwkv_w0, rwkv_a0, rwkv_k_k, rwkv_k_a, rwkv_ln_w, rwkv_ln_b, rwkv_r_k.reshape(-1)]
    vecs = jnp.stack(vec_rows + [jnp.zeros_like(rwkv_w0)] * (16 - len(vec_rows)), axis=0)
    mix_b = _rwkv(proj3, vecs, rwkv_mu[w3:].reshape(1, 2 * LORA), rwkv_w2.astype(BF16),
                  rwkv_a2.astype(BF16), min(256, s))

    w_o = w_out.astype(BF16)
    out = _outproj(mix_a.reshape(b * s, NSA_WIDTH), mix_b.reshape(b * s, RWKV_WIDTH),
                   w_o[:NSA_WIDTH], w_o[NSA_WIDTH:], x2, post_norm_g.reshape(1, d), min(512, b * s))
    return out.reshape(b, s, d)


def kernel(x, pre_norm_g, w_in, rel_bias_table, cmp_pos_k, cmp_pos_v, cmp_k_w1, cmp_k_w2, cmp_v_w1,
           cmp_v_w2, rwkv_mu, rwkv_w0, rwkv_w2, rwkv_a0, rwkv_a2, rwkv_k_k, rwkv_k_a, rwkv_r_k,
           rwkv_ln_w, rwkv_ln_b, w_out, post_norm_g):
    h = x
    for l in range(pre_norm_g.shape[0]):
        h = _block(h, pre_norm_g[l], w_in[l], rel_bias_table, cmp_pos_k[l], cmp_pos_v[l], cmp_k_w1[l],
                   cmp_k_w2[l], cmp_v_w1[l], cmp_v_w2[l], rwkv_mu[l], rwkv_w0[l], rwkv_w2[l],
                   rwkv_a0[l], rwkv_a2[l], rwkv_k_k[l], rwkv_k_a[l], rwkv_r_k[l], rwkv_ln_w[l],
                   rwkv_ln_b[l], w_out[l], post_norm_g[l])
    return h
```

```python
import functools
import math

import numpy as np
import jax
import jax.numpy as jnp
from jax import lax
from jax.experimental import pallas as pl
from jax.experimental.pallas import tpu as pltpu

F32 = jnp.float32
BF16 = jnp.bfloat16
HI = lax.Precision.HIGHEST

D_MODEL = 2048
NSA_HEADS = 8
NSA_KV_HEADS = 2
NSA_GROUP = NSA_HEADS // NSA_KV_HEADS
HEAD_DIM = 128
NSA_WIDTH = NSA_HEADS * HEAD_DIM
CMP_BLOCK = 32
CMP_STRIDE = 16
SLC_BLOCK = 64
SLC_TOP_N = 16
WINDOW = 512
RWKV_WIDTH = 1024
RWKV_HEAD_DIM = 64
RWKV_HEADS = RWKV_WIDTH // RWKV_HEAD_DIM
LORA = 64
NUM_BUCKETS = 32
MAX_DISTANCE = 1024
NORM_EPS = 1e-6
RWKV_GN_EPS = 64e-5

R_Q = 0
R_KV = R_Q + NSA_WIDTH
R_GATE = R_KV + 6 * NSA_KV_HEADS * HEAD_DIM
R_ZA = R_GATE + 3 * NSA_HEADS
R_FEAT = R_ZA + NSA_WIDTH
R_ZB = R_FEAT + 3 * RWKV_WIDTH + 2 * LORA
R_END = R_ZB + RWKV_WIDTH

P_Q = 0
P_RKV = 1024
P_ZB = 4096
P_ZA = 5120
P_KV = 6144
P_WDAD = 7680
P_GATE = 7808
NP = 8192

LANE = 128
TQ = 256
TB = 128
ND = 9
CH = 64
HPG = 4
GW = HPG * RWKV_HEAD_DIM
RWKV_GROUPS_PER_STEP = 4
NEG = -1e30


def _bucket_thresholds():
    out = []
    for k in range(1, NUM_BUCKETS // 2):
        n = 16
        while n ** 8 < (16 ** 8) * (2 ** (3 * k)):
            n += 1
        out.append(n)
    return out


_THR = _bucket_thresholds()


def _mm(a, b, precision=None):
    return jnp.dot(a, b, preferred_element_type=F32, precision=precision)


def _mm_nt(a, b, precision=None):
    return lax.dot_general(a, b, (((1,), (1,)), ((), ())), preferred_element_type=F32,
                           precision=precision)


def _mm_tn(a, b, precision=None):
    return lax.dot_general(a, b, (((0,), (0,)), ((), ())), preferred_element_type=F32,
                           precision=precision)


def _split3(x):
    x1 = x.astype(BF16)
    r1 = x - x1.astype(F32)
    x2 = r1.astype(BF16)
    x3 = (r1 - x2.astype(F32)).astype(BF16)
    return x1, x2, x3


def _mm_split_lhs(a, b_exact):
    a1, a2, a3 = _split3(a)
    return _mm(a1, b_exact) + _mm(a2, b_exact) + _mm(a3, b_exact)


def _mm_split_rhs(a_exact, b):
    b1, b2, b3 = _split3(b)
    return _mm(a_exact, b1) + _mm(a_exact, b2) + _mm(a_exact, b3)


def _iota(shape, dim):
    return lax.broadcasted_iota(jnp.int32, shape, dim)


def _div_pow2(x, n):
    assert n & (n - 1) == 0
    return x >> (n.bit_length() - 1)


def _mod_pow2(x, n):
    assert n & (n - 1) == 0
    return x & (n - 1)


def _inproj_kernel(x_ref, g_ref, w_ref, o_ref, hn_ref):
    @pl.when(pl.program_id(1) == 0)
    def _():
        x = x_ref[...]
        ms = jnp.mean(x * x, axis=-1, keepdims=True)
        hn_ref[...] = (x * lax.rsqrt(ms + NORM_EPS) * g_ref[...]).astype(BF16)

    o_ref[...] = _mm(hn_ref[...], w_ref[...])


def _inproj(x2, g, w, tm, tn):
    m, d = x2.shape
    n = w.shape[1]
    return pl.pallas_call(
        _inproj_kernel,
        grid=(m // tm, n // tn),
        in_specs=[
            pl.BlockSpec((tm, d), lambda i, j: (i, 0)),
            pl.BlockSpec((1, d), lambda i, j: (0, 0)),
            pl.BlockSpec((d, tn), lambda i, j: (0, j)),
        ],
        out_specs=pl.BlockSpec((tm, tn), lambda i, j: (i, j)),
        out_shape=jax.ShapeDtypeStruct((m, n), F32),
        scratch_shapes=[pltpu.VMEM((tm, d), BF16)],
        compiler_params=pltpu.CompilerParams(
            dimension_semantics=("parallel", "arbitrary"),
            vmem_limit_bytes=56 * 1024 * 1024),
        name="inproj",
    )(x2, g, w)


def _bucket(n):
    n = jnp.maximum(n, 0)
    large = jnp.full(n.shape, NUM_BUCKETS // 2, jnp.int32)
    for thr in _THR:
        large = large + (n >= thr).astype(jnp.int32)
    return jnp.where(n < NUM_BUCKETS // 2, n, large)


def _lookup(bucket, tab_ref, h):
    out = jnp.zeros(bucket.shape, F32)
    for b in range(NUM_BUCKETS):
        out = jnp.where(bucket == b, tab_ref[b * NSA_HEADS + h], out)
    return out


def _bias_kernel(tab_ref, bc_ref, tp_ref):
    h = pl.program_id(0)
    s, nr = bc_ref.shape[1], bc_ref.shape[2]
    dist_c = _iota((s, nr), 0) - (_iota((s, nr), 1) * CMP_STRIDE + (CMP_BLOCK - 1))
    bc_ref[0] = _lookup(_bucket(dist_c), tab_ref, h)
    base = _iota((TB, TB), 0) - _iota((TB, TB), 1)
    for d in range(ND):
        tp_ref[0, d] = _lookup(_bucket(base + d * TB), tab_ref, h)


def _bias(table_flat, s):
    nr = s // CMP_STRIDE
    return pl.pallas_call(
        _bias_kernel,
        grid=(NSA_HEADS,),
        in_specs=[pl.BlockSpec(memory_space=pltpu.SMEM)],
        out_specs=[
            pl.BlockSpec((1, s, nr), lambda h: (h, 0, 0)),
            pl.BlockSpec((1, ND, TB, TB), lambda h: (h, 0, 0, 0)),
        ],
        out_shape=[
            jax.ShapeDtypeStruct((NSA_HEADS, s, nr), F32),
            jax.ShapeDtypeStruct((NSA_HEADS, ND, TB, TB), F32),
        ],
        compiler_params=pltpu.CompilerParams(dimension_semantics=("arbitrary",)),
        name="bias",
    )(table_flat)


def _compress(kv_ref, pos_ref, w1_ref, w2_ref, nr):
    half = CMP_STRIDE * HEAD_DIM
    r = jnp.concatenate(
        [kv_ref[0, pl.ds(m, nr, stride=CMP_STRIDE), :] for m in range(CMP_STRIDE)], axis=1)
    a = _mm((r + pos_ref[0:1, :]).astype(BF16), w1_ref[0:half, :])
    b = _mm((r + pos_ref[1:2, :]).astype(BF16), w1_ref[half:2 * half, :])
    pre = a + pltpu.roll(b, nr - 1, 0)
    h1 = pre * jax.nn.sigmoid(pre)
    return _mm(h1.astype(BF16), w2_ref[...])


def _bias_tile(toep_ref, d_tiles):
    sub = TQ // TB
    rows = []
    for h in range(NSA_GROUP):
        for ri in range(sub):
            rows.append(jnp.concatenate(
                [toep_ref[h, jnp.clip(d_tiles + ri - ci, 0, ND - 1)] for ci in range(sub)], axis=1))
    return jnp.concatenate(rows, axis=0)


def _add_shared(s, mask_add):
    n = s.shape[-1]
    return (s.reshape(NSA_GROUP, TQ, n) + mask_add[None]).reshape(NSA_GROUP * TQ, n)


def _fold_lanes(x, op):
    out = x[:, :LANE]
    for c in range(1, x.shape[1] // LANE):
        out = op(out, x[:, c * LANE:(c + 1) * LANE])
    return out


def _nsa_kernel(q_ref, kc_ref, vc_ref, ks_ref, vs_ref, kw_ref, vw_ref, gate_ref, za_ref,
                bc_ref, toep_ref, posk_ref, w1k_ref, w2k_ref, posv_ref, w1v_ref, w2v_ref,
                o_ref, kcs_ref, vcs_ref, s_ref, am_ref):
    qi = pl.program_id(2)
    s_len = kc_ref.shape[1]
    nr = s_len // CMP_STRIDE
    nb = s_len // SLC_BLOCK
    n_sel = min(SLC_TOP_N, nb)

    @pl.when(qi == 0)
    def _():
        kcs_ref[...] = _compress(kc_ref, posk_ref, w1k_ref, w2k_ref, nr).astype(BF16)
        vcs_ref[...] = _compress(vc_ref, posv_ref, w1v_ref, w2v_ref, nr).astype(BF16)

    q0 = qi * TQ
    rows = NSA_GROUP * TQ
    q = q_ref[0] * (HEAD_DIM ** -0.5)
    q4 = jnp.concatenate([q[:, h * HEAD_DIM:(h + 1) * HEAD_DIM] for h in range(NSA_GROUP)],
                         axis=0).astype(BF16)

    t_c = q0 + _iota((TQ, nr), 0)
    i_c = _iota((TQ, nr), 1)
    mask_c = (t_c - (i_c * CMP_STRIDE + (CMP_BLOCK - 1)) >= 0) & (i_c < nr - 1)
    lg = _add_shared(_mm_nt(q4, kcs_ref[...]) + bc_ref[...].reshape(rows, nr),
                     jnp.where(mask_c, 0.0, NEG))
    keep = lg > 0.5 * NEG
    e = jnp.where(keep, jnp.exp(lg - jnp.max(lg, axis=-1, keepdims=True)), 0.0)
    p = e / jnp.maximum(jnp.sum(e, axis=-1, keepdims=True), 1e-30)
    o_c = _mm(p.astype(BF16), vcs_ref[...])
    psum = p[:TQ]
    for h in range(1, NSA_GROUP):
        psum = psum + p[h * TQ:(h + 1) * TQ]

    ov_i = _iota((nb, nr), 1) * CMP_STRIDE
    ov_j = _iota((nb, nr), 0) * SLC_BLOCK
    ov_t = ((ov_i < ov_j + SLC_BLOCK) & (ov_i + CMP_BLOCK > ov_j)).astype(BF16)
    p1, p2, p3 = _split3(psum)
    imp_t = _mm_nt(ov_t, p1) + _mm_nt(ov_t, p2) + _mm_nt(ov_t, p3)
    jb = _iota((nb, TQ), 0)
    cur = _div_pow2(q0 + _iota((nb, TQ), 1), SLC_BLOCK)
    forced = (jb == 0) | (jb == cur) | (jb == cur - 1)
    causal = jb <= cur
    score = jnp.where(forced, jnp.inf, jnp.where(causal, imp_t, -jnp.inf))
    rank = jnp.zeros((nb, TQ), jnp.int32)
    for jp in range(nb):
        sj = score[jp:jp + 1, :]
        beats = (sj > score) | ((sj == score) & (jb > jp))
        rank = rank + beats.astype(jnp.int32)
    sel_t = ((rank < n_sel) & causal).astype(BF16)

    sub = TQ // TB
    t_q = q0 + _iota((TQ, TQ), 0)
    for j in range(s_len // TQ):
        @pl.when(j <= qi)
        def _(j=j):
            kpos = j * TQ + _iota((TQ, TQ), 1)
            e_j = (_iota((nb, TQ), 0) == _div_pow2(j * TQ + _iota((nb, TQ), 1), SLC_BLOCK)).astype(BF16)
            chosen = _mm_tn(sel_t, e_j)
            am_ref[j] = jnp.where((chosen > 0.5) & (kpos <= t_q), 0.0, NEG)

    def slc_logits(j, macc):
        kt = ks_ref[0, pl.ds(pl.multiple_of(j * TQ, TQ), TQ), :].astype(BF16)
        s = _add_shared(_mm_nt(q4, kt) + _bias_tile(toep_ref, (qi - j) * sub), am_ref[j])
        s_ref[j] = s
        return jnp.maximum(macc, _fold_lanes(s, jnp.maximum))

    macc = lax.fori_loop(0, qi + 1, slc_logits, jnp.full((rows, LANE), NEG, F32))
    m_s = jnp.max(macc, axis=-1, keepdims=True)

    def slc_values(j, carry):
        lacc, acc = carry
        p_j = jnp.exp(s_ref[j] - m_s)
        vt = vs_ref[0, pl.ds(pl.multiple_of(j * TQ, TQ), TQ), :].astype(BF16)
        return lacc + _fold_lanes(p_j, jnp.add), acc + _mm(p_j.astype(BF16), vt)

    lacc, acc = lax.fori_loop(0, qi + 1, slc_values,
                              (jnp.zeros((rows, LANE), F32), jnp.zeros((rows, HEAD_DIM), F32)))
    o_s = acc / jnp.maximum(jnp.sum(lacc, axis=-1, keepdims=True), 1e-30)

    n_band = WINDOW // TQ + 1
    s_w, v_w = [], []
    for c in range(n_band):
        j = qi - (n_band - 1) + c
        jc = jnp.maximum(j, 0)
        k0 = pl.multiple_of(jc * TQ, TQ)
        dist = t_q - (jc * TQ + _iota((TQ, TQ), 1))
        ok = (dist >= 0) & (dist < WINDOW) & (j >= 0)
        kt = kw_ref[0, pl.ds(k0, TQ), :].astype(BF16)
        s_w.append(_add_shared(_mm_nt(q4, kt) + _bias_tile(toep_ref, (qi - jc) * sub),
                               jnp.where(ok, 0.0, NEG)))
        v_w.append(vw_ref[0, pl.ds(k0, TQ), :].astype(BF16))
    m_w = s_w[0]
    for c in range(1, n_band):
        m_w = jnp.maximum(m_w, s_w[c])
    m_w = jnp.max(m_w, axis=-1, keepdims=True)
    l_w = jnp.zeros((rows, TQ), F32)
    acc_w = jnp.zeros((rows, HEAD_DIM), F32)
    for c in range(n_band):
        p_c = jnp.exp(s_w[c] - m_w)
        l_w = l_w + p_c
        acc_w = acc_w + _mm(p_c.astype(BF16), v_w[c])
    o_w = acc_w / jnp.maximum(jnp.sum(l_w, axis=-1, keepdims=True), 1e-30)

    gts = jax.nn.sigmoid(gate_ref[0])
    za = za_ref[0]
    for h in range(NSA_GROUP):
        hr = slice(h * TQ, (h + 1) * TQ)
        o = (gts[:, h:h + 1] * o_c[hr]
             + gts[:, NSA_GROUP + h:NSA_GROUP + h + 1] * o_s[hr]
             + gts[:, 2 * NSA_GROUP + h:2 * NSA_GROUP + h + 1] * o_w[hr])
        z = za[:, h * HEAD_DIM:(h + 1) * HEAD_DIM]
        o_ref[0, :, h * HEAD_DIM:(h + 1) * HEAD_DIM] = (o * (z * jax.nn.sigmoid(z))).astype(o_ref.dtype)


def _nsa(proj3, bias_c, toep, posk, w1k, w2k, posv, w1v, w2v):
    b, s, _ = proj3.shape
    nr = s // CMP_STRIDE
    gq = NSA_GROUP * HEAD_DIM

    def kvspec(idx):
        return pl.BlockSpec((1, s, HEAD_DIM), lambda bi, g, qi, idx=idx: (bi, 0, P_KV // HEAD_DIM + 2 * idx + g))

    def whole(a):
        return pl.BlockSpec(a.shape, lambda bi, g, qi, nd=a.ndim: (0,) * nd)

    in_specs = [
        pl.BlockSpec((1, TQ, gq), lambda bi, g, qi: (bi, qi, P_Q // gq + g)),
        kvspec(0), kvspec(1), kvspec(2), kvspec(3), kvspec(4), kvspec(5),
        pl.BlockSpec((1, TQ, LANE), lambda bi, g, qi: (bi, qi, P_GATE // LANE + g)),
        pl.BlockSpec((1, TQ, gq), lambda bi, g, qi: (bi, qi, P_ZA // gq + g)),
        pl.BlockSpec((NSA_GROUP, TQ, nr), lambda bi, g, qi: (g, qi, 0)),
        pl.BlockSpec((NSA_GROUP, ND, TB, TB), lambda bi, g, qi: (g, 0, 0, 0)),
        whole(posk), whole(w1k), whole(w2k), whole(posv), whole(w1v), whole(w2v),
    ]
    return pl.pallas_call(
        _nsa_kernel,
        grid=(b, NSA_KV_HEADS, s // TQ),
        in_specs=in_specs,
        out_specs=pl.BlockSpec((1, TQ, gq), lambda bi, g, qi: (bi, qi, g)),
        out_shape=jax.ShapeDtypeStruct((b, s, NSA_WIDTH), BF16),
        scratch_shapes=[pltpu.VMEM((nr, HEAD_DIM), BF16), pltpu.VMEM((nr, HEAD_DIM), BF16),
                        pltpu.VMEM((s // TQ, NSA_GROUP * TQ, TQ), F32),
                        pltpu.VMEM((s // TQ, TQ, TQ), F32)],
        compiler_params=pltpu.CompilerParams(
            dimension_semantics=("parallel", "parallel", "arbitrary"),
            vmem_limit_bytes=56 * 1024 * 1024),
        name="nsa",
    )(proj3, proj3, proj3, proj3, proj3, proj3, proj3, proj3, proj3,
      bias_c, toep, posk, w1k, w2k, posv, w1v, w2v)


def _shift_mix(ref, prev_ref, mu, sl):
    x = ref[0, :, sl]
    prev = jnp.where(_iota(x.shape, 0) == 0, prev_ref[:, sl], pltpu.roll(x, 1, 0))
    prev_ref[:, sl] = x[x.shape[0] - 1:]
    return x + mu * (prev - x)


def _interleave(gens):
    results = [None] * len(gens)
    live = list(enumerate(gens))
    while live:
        still = []
        for i, g in live:
            try:
                next(g)
                still.append((i, g))
            except StopIteration as stop:
                results[i] = stop.value
        live = still
    return results


def _rwkv_kernel(r_ref, k_ref, v_ref, wa_ref, zb_ref, vec_ref, muwa_ref, w2_ref, a2_ref,
                 o_ref, st_ref, pr_ref, pk_ref, pv_ref, pwa_ref):
    first = pl.program_id(2) == 0
    tb = r_ref.shape[1]
    n_groups = r_ref.shape[2] // GW
    n_chunks = tb // CH

    @pl.when(first)
    def _():
        for ref in (st_ref, pr_ref, pk_ref, pv_ref, pwa_ref):
            ref[...] = jnp.zeros_like(ref)

    wa = _shift_mix(wa_ref, pwa_ref, muwa_ref[...], slice(0, LANE))
    wd_act = jnp.tanh(wa[:, :LORA]).astype(BF16)
    ad = wa[:, LORA:].astype(BF16)

    seg = (_div_pow2(_iota((GW, GW), 0), RWKV_HEAD_DIM) == _div_pow2(_iota((GW, GW), 1), RWKV_HEAD_DIM))
    segf = seg.astype(F32)
    segb = seg.astype(BF16)
    assert CH == RWKV_HEAD_DIM
    lane_s = _mod_pow2(_iota((CH, GW), 1), CH)
    row_t = _iota((CH, GW), 0)
    strict = lane_s < row_t
    incl = lane_s <= row_t
    eye = (lane_s == row_t).astype(F32)
    ti_r, ti_c = _iota((tb, tb), 0), _iota((tb, tb), 1)
    trib = ((ti_c <= ti_r) & (_div_pow2(ti_c, CH) == _div_pow2(ti_r, CH))).astype(BF16)

    def bd(x):
        xb = x.astype(BF16)
        return jnp.where(seg, jnp.concatenate([xb] * HPG, axis=0), jnp.zeros((), BF16))

    def prep(gi):
        lanes = slice(gi * GW, (gi + 1) * GW)
        vec = vec_ref[:, lanes]
        mu_r, mu_k, mu_v = vec[0:1], vec[1:2], vec[2:3]
        w0, a0, k_k, k_a = vec[3:4], vec[4:5], vec[5:6], vec[6:7]
        r = _shift_mix(r_ref, pr_ref, mu_r, lanes)
        k = _shift_mix(k_ref, pk_ref, mu_k, lanes)
        v = _shift_mix(v_ref, pv_ref, mu_v, lanes)
        w_lora = _mm(wd_act, w2_ref[:, lanes])
        a_lora = _mm(ad, a2_ref[:, lanes])
        kk = k * k_k
        kk_ss = _mm_split_lhs(kk * kk, segb)
        yield
        w = -jax.nn.softplus(-(w0 + w_lora)) - 0.5
        lw = -jnp.exp(w)
        cum = _mm_split_rhs(trib, lw)
        yield
        a_sig = jax.nn.sigmoid(a0 + a_lora)
        kk = kk / jnp.maximum(jnp.sqrt(kk_ss), 1e-12)
        k = k * (1.0 + (a_sig - 1.0) * k_a)
        return dict(r=r, k=k, v=v, a=-kk, b=kk * a_sig, lw=lw, cum=cum, vec=vec, lanes=lanes)

    groups = _interleave([prep(gi) for gi in range(n_groups)])

    def chunk_local(g, c):
        ts = slice(c * CH, (c + 1) * CH)
        rc, kc, vc, ac, bc, lwc, cum = (g[n][ts] for n in ("r", "k", "v", "a", "b", "lw", "cum"))
        tot = cum[CH - 1:CH]
        e_out = jnp.exp(-cum)
        e_end = jnp.exp(tot - cum)
        r_t = rc * jnp.exp(cum)
        a_t = ac * jnp.exp(cum - lwc)
        lhs = jnp.concatenate([a_t, r_t], axis=0).astype(BF16)
        aa = _mm_nt(lhs, jnp.concatenate([bd(bc * e_out), bd(kc * e_out)], axis=0))
        yield
        a_ab = jnp.where(strict, aa[:CH, :GW], 0.0)
        a_ak = jnp.where(strict, aa[:CH, GW:], 0.0)
        a_rb = jnp.where(incl, aa[CH:, :GW], 0.0)
        a_rk = jnp.where(incl, aa[CH:, GW:], 0.0)
        t_inv = eye + a_ab
        mpow = _mm(a_ab.astype(BF16), bd(a_ab))
        av = _mm(a_ak.astype(BF16), bd(vc))
        yield
        for _ in range(int(math.log2(CH)) - 1):
            res = _mm(jnp.concatenate([t_inv, mpow], axis=0).astype(BF16), bd(mpow))
            yield
            t_inv = t_inv + res[:CH]
            mpow = res[CH:]
        wu = _mm(t_inv.astype(BF16), jnp.concatenate([bd(a_t), bd(av)], axis=1))
        yield
        return dict(
            lhs=jnp.concatenate([wu[:, :GW], r_t], axis=0).astype(BF16), u_loc=wu[:, GW:],
            a_r=jnp.concatenate([a_rb, a_rk], axis=1).astype(BF16), bdv=bd(vc), vc=vc,
            bk_end=jnp.concatenate([bc * e_end, kc * e_end], axis=0).astype(BF16), dec=jnp.exp(tot))

    loc = _interleave([chunk_local(g, c) for g in groups for c in range(n_chunks)])

    def chain(gi):
        ys = []
        g_state = st_ref[gi]
        for c in range(n_chunks):
            lc = loc[gi * n_chunks + c]
            x0 = _mm_nt(lc["lhs"], g_state.astype(BF16))
            yield
            u = x0[:CH] + lc["u_loc"]
            y_c = _mm(lc["a_r"], jnp.concatenate([bd(u), lc["bdv"]], axis=0))
            upd = _mm_tn(jnp.concatenate([u, lc["vc"]], axis=0).astype(BF16), lc["bk_end"])
            yield
            ys.append(x0[CH:] + y_c)
            g_state = g_state * lc["dec"] + upd * segf
        st_ref[gi] = g_state
        return jnp.concatenate(ys, axis=0)

    ys = _interleave([chain(gi) for gi in range(n_groups)])

    def finish(g, y):
        vec = g["vec"]
        ln_w, ln_b, r_k = vec[7:8], vec[8:9], vec[9:10]
        inv_n = 1.0 / RWKV_HEAD_DIM
        mean = _mm_split_lhs(y, segb) * inv_n
        bonus = _mm_split_lhs(g["r"] * g["k"] * r_k, segb) * g["v"]
        yield
        yc = y - mean
        var = _mm_split_lhs(yc * yc, segb) * inv_n
        yield
        yn = yc * lax.rsqrt(var + RWKV_GN_EPS) * ln_w + ln_b
        zb = zb_ref[0, :, g["lanes"]]
        o_ref[0, :, g["lanes"]] = ((yn + bonus) * (zb * jax.nn.sigmoid(zb))).astype(o_ref.dtype)

    _interleave([finish(g, y) for g, y in zip(groups, ys)])


def _rwkv(proj3, vecs, mu_wa, w2, a2, tb, gps):
    b, s, _ = proj3.shape
    gw = gps * GW
    ng = RWKV_WIDTH // gw

    def col(off):
        return pl.BlockSpec((1, tb, gw), lambda bi, g, ti, off=off: (bi, ti, off // gw + g))

    in_specs = [
        col(P_RKV), col(P_RKV + RWKV_WIDTH), col(P_RKV + 2 * RWKV_WIDTH),
        pl.BlockSpec((1, tb, LANE), lambda bi, g, ti: (bi, ti, P_WDAD // LANE)),
        col(P_ZB),
        pl.BlockSpec((vecs.shape[0], gw), lambda bi, g, ti: (0, g)),
        pl.BlockSpec((1, LANE), lambda bi, g, ti: (0, 0)),
        pl.BlockSpec((LORA, gw), lambda bi, g, ti: (0, g)),
        pl.BlockSpec((LORA, gw), lambda bi, g, ti: (0, g)),
    ]
    return pl.pallas_call(
        _rwkv_kernel,
        grid=(b, ng, s // tb),
        in_specs=in_specs,
        out_specs=pl.BlockSpec((1, tb, gw), lambda bi, g, ti: (bi, ti, g)),
        out_shape=jax.ShapeDtypeStruct((b, s, RWKV_WIDTH), BF16),
        scratch_shapes=[pltpu.VMEM((gps, GW, GW), F32), pltpu.VMEM((1, gw), F32), pltpu.VMEM((1, gw), F32),
                        pltpu.VMEM((1, gw), F32), pltpu.VMEM((1, LANE), F32)],
        compiler_params=pltpu.CompilerParams(
            dimension_semantics=("parallel", "parallel", "arbitrary")),
        name="rwkv",
    )(proj3, proj3, proj3, proj3, proj3, vecs, mu_wa, w2, a2)


def _outproj_kernel(ma_ref, mb_ref, wa_ref, wb_ref, x_ref, g_ref, o_ref):
    y = _mm(ma_ref[...], wa_ref[...]) + _mm(mb_ref[...], wb_ref[...])
    ms = jnp.mean(y * y, axis=-1, keepdims=True)
    o_ref[...] = x_ref[...] + y * lax.rsqrt(ms + NORM_EPS) * g_ref[...]


def _outproj(mix_a, mix_b, w_a, w_b, x2, g, tm):
    m, d = x2.shape
    ka, kb = mix_a.shape[1], mix_b.shape[1]
    return pl.pallas_call(
        _outproj_kernel,
        grid=(m // tm,),
        in_specs=[
            pl.BlockSpec((tm, ka), lambda i: (i, 0)),
            pl.BlockSpec((tm, kb), lambda i: (i, 0)),
            pl.BlockSpec((ka, d), lambda i: (0, 0)),
            pl.BlockSpec((kb, d), lambda i: (0, 0)),
            pl.BlockSpec((tm, d), lambda i: (i, 0)),
            pl.BlockSpec((1, d), lambda i: (0, 0)),
        ],
        out_specs=pl.BlockSpec((tm, d), lambda i: (i, 0)),
        out_shape=jax.ShapeDtypeStruct((m, d), F32),
        compiler_params=pltpu.CompilerParams(
            dimension_semantics=("parallel",), vmem_limit_bytes=56 * 1024 * 1024),
        name="outproj",
    )(mix_a, mix_b, w_a, w_b, x2, g)


def _permute_w_in(w):
    d = w.shape[0]
    gate = w[:, R_GATE:R_ZA].reshape(d, 3, NSA_KV_HEADS, NSA_GROUP)
    gate_blocks = []
    for g in range(NSA_KV_HEADS):
        gb = gate[:, :, g, :].reshape(d, 3 * NSA_GROUP)
        gate_blocks.append(jnp.pad(gb, ((0, 0), (0, LANE - 3 * NSA_GROUP))))
    parts = [
        w[:, R_Q:R_KV], w[:, R_FEAT:R_FEAT + 3 * RWKV_WIDTH], w[:, R_ZB:R_END], w[:, R_ZA:R_FEAT],
        w[:, R_KV:R_GATE], w[:, R_FEAT + 3 * RWKV_WIDTH:R_ZB],
    ] + gate_blocks
    wp = jnp.concatenate(parts, axis=1)
    return jnp.pad(wp, ((0, 0), (0, NP - wp.shape[1]))).astype(BF16)


def _block(x, pre_norm_g, w_in, rel_bias_table, cmp_pos_k, cmp_pos_v, cmp_k_w1, cmp_k_w2, cmp_v_w1,
           cmp_v_w2, rwkv_mu, rwkv_w0, rwkv_w2, rwkv_a0, rwkv_a2, rwkv_k_k, rwkv_k_a, rwkv_r_k,
           rwkv_ln_w, rwkv_ln_b, w_out, post_norm_g):
    b, s, d = x.shape
    x2 = x.reshape(b * s, d)
    tm = min(1024, b * s)
    proj = _inproj(x2, pre_norm_g.reshape(1, d), _permute_w_in(w_in), tm, 1024)
    proj3 = proj.reshape(b, s, NP)

    bias_c, toep = _bias(rel_bias_table.reshape(-1), s)
    half = CMP_STRIDE * HEAD_DIM
    mix_a = _nsa(proj3, bias_c, toep,
                 cmp_pos_k.reshape(2, half), cmp_k_w1.astype(BF16), cmp_k_w2.astype(BF16),
                 cmp_pos_v.reshape(2, half), cmp_v_w1.astype(BF16), cmp_v_w2.astype(BF16))

    w3 = 3 * RWKV_WIDTH
    vec_rows = [rwkv_mu[:RWKV_WIDTH], rwkv_mu[RWKV_WIDTH:2 * RWKV_WIDTH], rwkv_mu[2 * RWKV_WIDTH:w3],
                rwkv_w0, rwkv_a0, rwkv_k_k, rwkv_k_a, rwkv_ln_w, rwkv_ln_b, rwkv_r_k.reshape(-1)]
    vecs = jnp.stack(vec_rows + [jnp.zeros_like(rwkv_w0)] * (16 - len(vec_rows)), axis=0)
    mix_b = _rwkv(proj3, vecs, rwkv_mu[w3:].reshape(1, 2 * LORA), rwkv_w2.astype(BF16),
                  rwkv_a2.astype(BF16), min(256, s), RWKV_GROUPS_PER_STEP)

    w_o = w_out.astype(BF16)
    out = _outproj(mix_a.reshape(b * s, NSA_WIDTH), mix_b.reshape(b * s, RWKV_WIDTH),
                   w_o[:NSA_WIDTH], w_o[NSA_WIDTH:], x2, post_norm_g.reshape(1, d), min(512, b * s))
    return out.reshape(b, s, d)


def kernel(x, pre_norm_g, w_in, rel_bias_table, cmp_pos_k, cmp_pos_v, cmp_k_w1, cmp_k_w2, cmp_v_w1,
           cmp_v_w2, rwkv_mu, rwkv_w0, rwkv_w2, rwkv_a0, rwkv_a2, rwkv_k_k, rwkv_k_a, rwkv_r_k,
           rwkv_ln_w, rwkv_ln_b, w_out, post_norm_g):
    h = x
    for l in range(pre_norm_g.shape[0]):
        h = _block(h, pre_norm_g[l], w_in[l], rel_bias_table, cmp_pos_k[l], cmp_pos_v[l], cmp_k_w1[l],
                   cmp_k_w2[l], cmp_v_w1[l], cmp_v_w2[l], rwkv_mu[l], rwkv_w0[l], rwkv_w2[l],
                   rwkv_a0[l], rwkv_a2[l], rwkv_k_k[l], rwkv_k_a[l], rwkv_r_k[l], rwkv_ln_w[l],
                   rwkv_ln_b[l], w_out[l], post_norm_g[l])
    return h
```

```python
import functools
import math

import numpy as np
import jax
import jax.numpy as jnp
from jax import lax
from jax.experimental import pallas as pl
from jax.experimental.pallas import tpu as pltpu

F32 = jnp.float32
BF16 = jnp.bfloat16
HI = lax.Precision.HIGHEST

D_MODEL = 2048
NSA_HEADS = 8
NSA_KV_HEADS = 2
NSA_GROUP = NSA_HEADS // NSA_KV_HEADS
HEAD_DIM = 128
NSA_WIDTH = NSA_HEADS * HEAD_DIM
CMP_BLOCK = 32
CMP_STRIDE = 16
SLC_BLOCK = 64
SLC_TOP_N = 16
WINDOW = 512
RWKV_WIDTH = 1024
RWKV_HEAD_DIM = 64
RWKV_HEADS = RWKV_WIDTH // RWKV_HEAD_DIM
LORA = 64
NUM_BUCKETS = 32
MAX_DISTANCE = 1024
NORM_EPS = 1e-6
RWKV_GN_EPS = 64e-5

R_Q = 0
R_KV = R_Q + NSA_WIDTH
R_GATE = R_KV + 6 * NSA_KV_HEADS * HEAD_DIM
R_ZA = R_GATE + 3 * NSA_HEADS
R_FEAT = R_ZA + NSA_WIDTH
R_ZB = R_FEAT + 3 * RWKV_WIDTH + 2 * LORA
R_END = R_ZB + RWKV_WIDTH

P_Q = 0
P_RKV = 1024
P_ZB = 4096
P_ZA = 5120
P_KV = 6144
P_WDAD = 7680
P_GATE = 7808
NP = 8192

LANE = 128
TQ = 256
TB = 128
ND = 9
CH = 64
HPG = 4
GW = HPG * RWKV_HEAD_DIM
RWKV_GROUPS_PER_STEP = 4
NEG = -1e30


def _bucket_thresholds():
    out = []
    for k in range(1, NUM_BUCKETS // 2):
        n = 16
        while n ** 8 < (16 ** 8) * (2 ** (3 * k)):
            n += 1
        out.append(n)
    return out


_THR = _bucket_thresholds()


def _mm(a, b, precision=None):
    return jnp.dot(a, b, preferred_element_type=F32, precision=precision)


def _mm_nt(a, b, precision=None):
    return lax.dot_general(a, b, (((1,), (1,)), ((), ())), preferred_element_type=F32,
                           precision=precision)


def _mm_tn(a, b, precision=None):
    return lax.dot_general(a, b, (((0,), (0,)), ((), ())), preferred_element_type=F32,
                           precision=precision)


def _split3(x):
    x1 = x.astype(BF16)
    r1 = x - x1.astype(F32)
    x2 = r1.astype(BF16)
    x3 = (r1 - x2.astype(F32)).astype(BF16)
    return x1, x2, x3


def _mm_split_lhs(a, b_exact):
    a1, a2, a3 = _split3(a)
    return _mm(a1, b_exact) + _mm(a2, b_exact) + _mm(a3, b_exact)


def _mm_split_rhs(a_exact, b):
    b1, b2, b3 = _split3(b)
    return _mm(a_exact, b1) + _mm(a_exact, b2) + _mm(a_exact, b3)


def _iota(shape, dim):
    return lax.broadcasted_iota(jnp.int32, shape, dim)


def _interleave(gens):
    results = [None] * len(gens)
    live = list(enumerate(gens))
    while live:
        still = []
        for i, g in live:
            try:
                next(g)
                still.append((i, g))
            except StopIteration as stop:
                results[i] = stop.value
        live = still
    return results


def _div_pow2(x, n):
    assert n & (n - 1) == 0
    return x >> (n.bit_length() - 1)


def _mod_pow2(x, n):
    assert n & (n - 1) == 0
    return x & (n - 1)


def _inproj_kernel(x_ref, g_ref, w_ref, o_ref, hn_ref):
    @pl.when(pl.program_id(1) == 0)
    def _():
        x = x_ref[...]
        ms = jnp.mean(x * x, axis=-1, keepdims=True)
        hn_ref[...] = (x * lax.rsqrt(ms + NORM_EPS) * g_ref[...]).astype(BF16)

    o_ref[...] = _mm(hn_ref[...], w_ref[...])


def _inproj(x2, g, w, tm, tn):
    m, d = x2.shape
    n = w.shape[1]
    return pl.pallas_call(
        _inproj_kernel,
        grid=(m // tm, n // tn),
        in_specs=[
            pl.BlockSpec((tm, d), lambda i, j: (i, 0)),
            pl.BlockSpec((1, d), lambda i, j: (0, 0)),
            pl.BlockSpec((d, tn), lambda i, j: (0, j)),
        ],
        out_specs=pl.BlockSpec((tm, tn), lambda i, j: (i, j)),
        out_shape=jax.ShapeDtypeStruct((m, n), F32),
        scratch_shapes=[pltpu.VMEM((tm, d), BF16)],
        compiler_params=pltpu.CompilerParams(
            dimension_semantics=("parallel", "arbitrary"),
            vmem_limit_bytes=56 * 1024 * 1024),
        name="inproj",
    )(x2, g, w)


def _bucket(n):
    n = jnp.maximum(n, 0)
    large = jnp.full(n.shape, NUM_BUCKETS // 2, jnp.int32)
    for thr in _THR:
        large = large + (n >= thr).astype(jnp.int32)
    return jnp.where(n < NUM_BUCKETS // 2, n, large)


def _lookup(bucket, tab_ref, h):
    out = jnp.zeros(bucket.shape, F32)
    for b in range(NUM_BUCKETS):
        out = jnp.where(bucket == b, tab_ref[b * NSA_HEADS + h], out)
    return out


def _bias_kernel(tab_ref, bc_ref, tp_ref):
    h = pl.program_id(0)
    s, nr = bc_ref.shape[1], bc_ref.shape[2]
    dist_c = _iota((s, nr), 0) - (_iota((s, nr), 1) * CMP_STRIDE + (CMP_BLOCK - 1))
    bc_ref[0] = _lookup(_bucket(dist_c), tab_ref, h)
    base = _iota((TB, TB), 0) - _iota((TB, TB), 1)
    for d in range(ND):
        tp_ref[0, d] = _lookup(_bucket(base + d * TB), tab_ref, h)


def _bias(table_flat, s):
    nr = s // CMP_STRIDE
    return pl.pallas_call(
        _bias_kernel,
        grid=(NSA_HEADS,),
        in_specs=[pl.BlockSpec(memory_space=pltpu.SMEM)],
        out_specs=[
            pl.BlockSpec((1, s, nr), lambda h: (h, 0, 0)),
            pl.BlockSpec((1, ND, TB, TB), lambda h: (h, 0, 0, 0)),
        ],
        out_shape=[
            jax.ShapeDtypeStruct((NSA_HEADS, s, nr), F32),
            jax.ShapeDtypeStruct((NSA_HEADS, ND, TB, TB), F32),
        ],
        compiler_params=pltpu.CompilerParams(dimension_semantics=("arbitrary",)),
        name="bias",
    )(table_flat)


def _compress(kv_ref, pos_ref, w1_ref, w2_ref, nr):
    half = CMP_STRIDE * HEAD_DIM
    r = jnp.concatenate(
        [kv_ref[0, pl.ds(m, nr, stride=CMP_STRIDE), :] for m in range(CMP_STRIDE)], axis=1)
    a = _mm((r + pos_ref[0:1, :]).astype(BF16), w1_ref[0:half, :])
    b = _mm((r + pos_ref[1:2, :]).astype(BF16), w1_ref[half:2 * half, :])
    pre = a + pltpu.roll(b, nr - 1, 0)
    h1 = pre * jax.nn.sigmoid(pre)
    return _mm(h1.astype(BF16), w2_ref[...])


def _bias_tile(toep_ref, d_tiles):
    sub = TQ // TB
    rows = []
    for h in range(NSA_GROUP):
        for ri in range(sub):
            rows.append(jnp.concatenate(
                [toep_ref[h, jnp.clip(d_tiles + ri - ci, 0, ND - 1)] for ci in range(sub)], axis=1))
    return jnp.concatenate(rows, axis=0)


def _add_shared(s, mask_add):
    n = s.shape[-1]
    return (s.reshape(NSA_GROUP, TQ, n) + mask_add[None]).reshape(NSA_GROUP * TQ, n)


def _fold_lanes(x, op):
    out = x[:, :LANE]
    for c in range(1, x.shape[1] // LANE):
        out = op(out, x[:, c * LANE:(c + 1) * LANE])
    return out


def _nsa_kernel(q_ref, kc_ref, vc_ref, ks_ref, vs_ref, kw_ref, vw_ref, gate_ref, za_ref,
                bc_ref, toep_ref, posk_ref, w1k_ref, w2k_ref, posv_ref, w1v_ref, w2v_ref,
                o_ref, kcs_ref, vcs_ref, s_ref, am_ref):
    qi = pl.program_id(2)
    s_len = kc_ref.shape[1]
    nr = s_len // CMP_STRIDE
    nb = s_len // SLC_BLOCK
    n_sel = min(SLC_TOP_N, nb)

    @pl.when(qi == 0)
    def _():
        kcs_ref[...] = _compress(kc_ref, posk_ref, w1k_ref, w2k_ref, nr).astype(BF16)
        vcs_ref[...] = _compress(vc_ref, posv_ref, w1v_ref, w2v_ref, nr).astype(BF16)

    q0 = qi * TQ
    rows = NSA_GROUP * TQ
    q = q_ref[0] * (HEAD_DIM ** -0.5)
    q4 = jnp.concatenate([q[:, h * HEAD_DIM:(h + 1) * HEAD_DIM] for h in range(NSA_GROUP)],
                         axis=0).astype(BF16)

    sub = TQ // TB
    n_tiles = s_len // TQ
    t_q = q0 + _iota((TQ, TQ), 0)

    def compressed_and_selection():
        t_c = q0 + _iota((TQ, nr), 0)
        i_c = _iota((TQ, nr), 1)
        mask_c = (t_c - (i_c * CMP_STRIDE + (CMP_BLOCK - 1)) >= 0) & (i_c < nr - 1)
        qk = _mm_nt(q4, kcs_ref[...])
        yield
        lg = _add_shared(qk + bc_ref[...].reshape(rows, nr), jnp.where(mask_c, 0.0, NEG))
        keep = lg > 0.5 * NEG
        e = jnp.where(keep, jnp.exp(lg - jnp.max(lg, axis=-1, keepdims=True)), 0.0)
        p = e / jnp.maximum(jnp.sum(e, axis=-1, keepdims=True), 1e-30)
        o_c = _mm(p.astype(BF16), vcs_ref[...])
        psum = p[:TQ]
        for h in range(1, NSA_GROUP):
            psum = psum + p[h * TQ:(h + 1) * TQ]
        ov_i = _iota((nb, nr), 1) * CMP_STRIDE
        ov_j = _iota((nb, nr), 0) * SLC_BLOCK
        ov_t = ((ov_i < ov_j + SLC_BLOCK) & (ov_i + CMP_BLOCK > ov_j)).astype(BF16)
        p1, p2, p3 = _split3(psum)
        imp_t = _mm_nt(ov_t, p1) + _mm_nt(ov_t, p2) + _mm_nt(ov_t, p3)
        yield
        jb = _iota((nb, TQ), 0)
        cur = _div_pow2(q0 + _iota((nb, TQ), 1), SLC_BLOCK)
        forced = (jb == 0) | (jb == cur) | (jb == cur - 1)
        causal = jb <= cur
        score = jnp.where(forced, jnp.inf, jnp.where(causal, imp_t, -jnp.inf))
        rank = jnp.zeros((nb, TQ), jnp.int32)
        for jp in range(nb):
            sj = score[jp:jp + 1, :]
            beats = (sj > score) | ((sj == score) & (jb > jp))
            rank = rank + beats.astype(jnp.int32)
        sel_t = ((rank < n_sel) & causal).astype(BF16)
        expand = (_iota((nb, s_len), 0) == _div_pow2(_iota((nb, s_len), 1), SLC_BLOCK)).astype(BF16)
        chosen = _mm_tn(sel_t, expand)
        yield
        for j in range(n_tiles):
            kpos = j * TQ + _iota((TQ, TQ), 1)
            am_ref[j] = jnp.where((chosen[:, j * TQ:(j + 1) * TQ] > 0.5) & (kpos <= t_q), 0.0, NEG)
        return o_c

    def window():
        n_band = WINDOW // TQ + 1
        qk, oks, v_w = [], [], []
        for c in range(n_band):
            j = qi - (n_band - 1) + c
            jc = jnp.maximum(j, 0)
            k0 = pl.multiple_of(jc * TQ, TQ)
            dist = t_q - (jc * TQ + _iota((TQ, TQ), 1))
            oks.append(((dist >= 0) & (dist < WINDOW) & (j >= 0), jc))
            qk.append(_mm_nt(q4, kw_ref[0, pl.ds(k0, TQ), :].astype(BF16)))
            v_w.append(vw_ref[0, pl.ds(k0, TQ), :].astype(BF16))
        yield
        s_w = [_add_shared(qk[c] + _bias_tile(toep_ref, (qi - oks[c][1]) * sub),
                           jnp.where(oks[c][0], 0.0, NEG)) for c in range(n_band)]
        m_w = s_w[0]
        for c in range(1, n_band):
            m_w = jnp.maximum(m_w, s_w[c])
        m_w = jnp.max(m_w, axis=-1, keepdims=True)
        l_w = jnp.zeros((rows, TQ), F32)
        acc_w = jnp.zeros((rows, HEAD_DIM), F32)
        for c in range(n_band):
            p_c = jnp.exp(s_w[c] - m_w)
            l_w = l_w + p_c
            acc_w = acc_w + _mm(p_c.astype(BF16), v_w[c])
        yield
        return acc_w / jnp.maximum(jnp.sum(l_w, axis=-1, keepdims=True), 1e-30)

    o_c, o_w = _interleave([compressed_and_selection(), window()])

    assert n_tiles % 2 == 0
    n_pairs = _div_pow2(qi + 2, 2)

    def slc_logits(jp, macc):
        pair = (2 * jp, 2 * jp + 1)
        qk = [_mm_nt(q4, ks_ref[0, pl.ds(pl.multiple_of(j * TQ, TQ), TQ), :].astype(BF16)) for j in pair]
        for j, qk_j in zip(pair, qk):
            s = _add_shared(qk_j + _bias_tile(toep_ref, (qi - j) * sub), am_ref[j])
            s_ref[j] = s
            macc = jnp.maximum(macc, _fold_lanes(s, jnp.maximum))
        return macc

    macc = lax.fori_loop(0, n_pairs, slc_logits, jnp.full((rows, LANE), NEG, F32))
    m_s = jnp.max(macc, axis=-1, keepdims=True)

    def slc_values(jp, carry):
        lacc, acc = carry
        for j in (2 * jp, 2 * jp + 1):
            p_j = jnp.exp(s_ref[j] - m_s)
            vt = vs_ref[0, pl.ds(pl.multiple_of(j * TQ, TQ), TQ), :].astype(BF16)
            lacc = lacc + _fold_lanes(p_j, jnp.add)
            acc = acc + _mm(p_j.astype(BF16), vt)
        return lacc, acc

    lacc, acc = lax.fori_loop(0, n_pairs, slc_values,
                              (jnp.zeros((rows, LANE), F32), jnp.zeros((rows, HEAD_DIM), F32)))
    o_s = acc / jnp.maximum(jnp.sum(lacc, axis=-1, keepdims=True), 1e-30)

    gts = jax.nn.sigmoid(gate_ref[0])
    za = za_ref[0]
    for h in range(NSA_GROUP):
        hr = slice(h * TQ, (h + 1) * TQ)
        o = (gts[:, h:h + 1] * o_c[hr]
             + gts[:, NSA_GROUP + h:NSA_GROUP + h + 1] * o_s[hr]
             + gts[:, 2 * NSA_GROUP + h:2 * NSA_GROUP + h + 1] * o_w[hr])
        z = za[:, h * HEAD_DIM:(h + 1) * HEAD_DIM]
        o_ref[0, :, h * HEAD_DIM:(h + 1) * HEAD_DIM] = (o * (z * jax.nn.sigmoid(z))).astype(o_ref.dtype)


def _nsa(proj3, bias_c, toep, posk, w1k, w2k, posv, w1v, w2v):
    b, s, _ = proj3.shape
    nr = s // CMP_STRIDE
    gq = NSA_GROUP * HEAD_DIM

    def kvspec(idx):
        return pl.BlockSpec((1, s, HEAD_DIM), lambda bi, g, qi, idx=idx: (bi, 0, P_KV // HEAD_DIM + 2 * idx + g))

    def whole(a):
        return pl.BlockSpec(a.shape, lambda bi, g, qi, nd=a.ndim: (0,) * nd)

    in_specs = [
        pl.BlockSpec((1, TQ, gq), lambda bi, g, qi: (bi, qi, P_Q // gq + g)),
        kvspec(0), kvspec(1), kvspec(2), kvspec(3), kvspec(4), kvspec(5),
        pl.BlockSpec((1, TQ, LANE), lambda bi, g, qi: (bi, qi, P_GATE // LANE + g)),
        pl.BlockSpec((1, TQ, gq), lambda bi, g, qi: (bi, qi, P_ZA // gq + g)),
        pl.BlockSpec((NSA_GROUP, TQ, nr), lambda bi, g, qi: (g, qi, 0)),
        pl.BlockSpec((NSA_GROUP, ND, TB, TB), lambda bi, g, qi: (g, 0, 0, 0)),
        whole(posk), whole(w1k), whole(w2k), whole(posv), whole(w1v), whole(w2v),
    ]
    return pl.pallas_call(
        _nsa_kernel,
        grid=(b, NSA_KV_HEADS, s // TQ),
        in_specs=in_specs,
        out_specs=pl.BlockSpec((1, TQ, gq), lambda bi, g, qi: (bi, qi, g)),
        out_shape=jax.ShapeDtypeStruct((b, s, NSA_WIDTH), BF16),
        scratch_shapes=[pltpu.VMEM((nr, HEAD_DIM), BF16), pltpu.VMEM((nr, HEAD_DIM), BF16),
                        pltpu.VMEM((s // TQ, NSA_GROUP * TQ, TQ), F32),
                        pltpu.VMEM((s // TQ, TQ, TQ), F32)],
        compiler_params=pltpu.CompilerParams(
            dimension_semantics=("parallel", "parallel", "arbitrary"),
            vmem_limit_bytes=56 * 1024 * 1024),
        name="nsa",
    )(proj3, proj3, proj3, proj3, proj3, proj3, proj3, proj3, proj3,
      bias_c, toep, posk, w1k, w2k, posv, w1v, w2v)


def _shift_mix(ref, prev_ref, mu, sl):
    x = ref[0, :, sl]
    prev = jnp.where(_iota(x.shape, 0) == 0, prev_ref[:, sl], pltpu.roll(x, 1, 0))
    prev_ref[:, sl] = x[x.shape[0] - 1:]
    return x + mu * (prev - x)


def _rwkv_kernel(r_ref, k_ref, v_ref, wa_ref, zb_ref, vec_ref, muwa_ref, w2_ref, a2_ref,
                 o_ref, st_ref, pr_ref, pk_ref, pv_ref, pwa_ref):
    first = pl.program_id(2) == 0
    tb = r_ref.shape[1]
    n_groups = r_ref.shape[2] // GW
    n_chunks = tb // CH

    @pl.when(first)
    def _():
        for ref in (st_ref, pr_ref, pk_ref, pv_ref, pwa_ref):
            ref[...] = jnp.zeros_like(ref)

    wa = _shift_mix(wa_ref, pwa_ref, muwa_ref[...], slice(0, LANE))
    wd_act = jnp.tanh(wa[:, :LORA]).astype(BF16)
    ad = wa[:, LORA:].astype(BF16)

    seg = (_div_pow2(_iota((GW, GW), 0), RWKV_HEAD_DIM) == _div_pow2(_iota((GW, GW), 1), RWKV_HEAD_DIM))
    segf = seg.astype(F32)
    segb = seg.astype(BF16)
    assert CH == RWKV_HEAD_DIM
    lane_s = _mod_pow2(_iota((CH, GW), 1), CH)
    row_t = _iota((CH, GW), 0)
    strict = lane_s < row_t
    incl = lane_s <= row_t
    eye = (lane_s == row_t).astype(F32)
    ti_r, ti_c = _iota((tb, tb), 0), _iota((tb, tb), 1)
    trib = ((ti_c <= ti_r) & (_div_pow2(ti_c, CH) == _div_pow2(ti_r, CH))).astype(BF16)

    def bd(x):
        xb = x.astype(BF16)
        return jnp.where(seg, jnp.concatenate([xb] * HPG, axis=0), jnp.zeros((), BF16))

    def prep(gi):
        lanes = slice(gi * GW, (gi + 1) * GW)
        vec = vec_ref[:, lanes]
        mu_r, mu_k, mu_v = vec[0:1], vec[1:2], vec[2:3]
        w0, a0, k_k, k_a = vec[3:4], vec[4:5], vec[5:6], vec[6:7]
        r = _shift_mix(r_ref, pr_ref, mu_r, lanes)
        k = _shift_mix(k_ref, pk_ref, mu_k, lanes)
        v = _shift_mix(v_ref, pv_ref, mu_v, lanes)
        w_lora = _mm(wd_act, w2_ref[:, lanes])
        a_lora = _mm(ad, a2_ref[:, lanes])
        kk = k * k_k
        kk_ss = _mm_split_lhs(kk * kk, segb)
        yield
        w = -jax.nn.softplus(-(w0 + w_lora)) - 0.5
        lw = -jnp.exp(w)
        cum = _mm_split_rhs(trib, lw)
        yield
        a_sig = jax.nn.sigmoid(a0 + a_lora)
        kk = kk / jnp.maximum(jnp.sqrt(kk_ss), 1e-12)
        k = k * (1.0 + (a_sig - 1.0) * k_a)
        return dict(r=r, k=k, v=v, a=-kk, b=kk * a_sig, lw=lw, cum=cum, vec=vec, lanes=lanes)

    groups = _interleave([prep(gi) for gi in range(n_groups)])

    def chunk_local(g, c):
        ts = slice(c * CH, (c + 1) * CH)
        rc, kc, vc, ac, bc, lwc, cum = (g[n][ts] for n in ("r", "k", "v", "a", "b", "lw", "cum"))
        tot = cum[CH - 1:CH]
        e_out = jnp.exp(-cum)
        e_end = jnp.exp(tot - cum)
        r_t = rc * jnp.exp(cum)
        a_t = ac * jnp.exp(cum - lwc)
        lhs = jnp.concatenate([a_t, r_t], axis=0).astype(BF16)
        aa = _mm_nt(lhs, jnp.concatenate([bd(bc * e_out), bd(kc * e_out)], axis=0))
        yield
        a_ab = jnp.where(strict, aa[:CH, :GW], 0.0)
        a_ak = jnp.where(strict, aa[:CH, GW:], 0.0)
        a_rb = jnp.where(incl, aa[CH:, :GW], 0.0)
        a_rk = jnp.where(incl, aa[CH:, GW:], 0.0)
        t_inv = eye + a_ab
        mpow = _mm(a_ab.astype(BF16), bd(a_ab))
        av = _mm(a_ak.astype(BF16), bd(vc))
        yield
        for _ in range(int(math.log2(CH)) - 1):
            res = _mm(jnp.concatenate([t_inv, mpow], axis=0).astype(BF16), bd(mpow))
            yield
            t_inv = t_inv + res[:CH]
            mpow = res[CH:]
        wu = _mm(t_inv.astype(BF16), jnp.concatenate([bd(a_t), bd(av)], axis=1))
        yield
        return dict(
            lhs=jnp.concatenate([wu[:, :GW], r_t], axis=0).astype(BF16), u_loc=wu[:, GW:],
            a_r=jnp.concatenate([a_rb, a_rk], axis=1).astype(BF16), bdv=bd(vc), vc=vc,
            bk_end=jnp.concatenate([bc * e_end, kc * e_end], axis=0).astype(BF16), dec=jnp.exp(tot))

    loc = _interleave([chunk_local(g, c) for g in groups for c in range(n_chunks)])

    def chain(gi):
        ys = []
        g_state = st_ref[gi]
        for c in range(n_chunks):
            lc = loc[gi * n_chunks + c]
            x0 = _mm_nt(lc["lhs"], g_state.astype(BF16))
            yield
            u = x0[:CH] + lc["u_loc"]
            y_c = _mm(lc["a_r"], jnp.concatenate([bd(u), lc["bdv"]], axis=0))
            upd = _mm_tn(jnp.concatenate([u, lc["vc"]], axis=0).astype(BF16), lc["bk_end"])
            yield
            ys.append(x0[CH:] + y_c)
            g_state = g_state * lc["dec"] + upd * segf
        st_ref[gi] = g_state
        return jnp.concatenate(ys, axis=0)

    ys = _interleave([chain(gi) for gi in range(n_groups)])

    def finish(g, y):
        vec = g["vec"]
        ln_w, ln_b, r_k = vec[7:8], vec[8:9], vec[9:10]
        inv_n = 1.0 / RWKV_HEAD_DIM
        mean = _mm_split_lhs(y, segb) * inv_n
        bonus = _mm_split_lhs(g["r"] * g["k"] * r_k, segb) * g["v"]
        yield
        yc = y - mean
        var = _mm_split_lhs(yc * yc, segb) * inv_n
        yield
        yn = yc * lax.rsqrt(var + RWKV_GN_EPS) * ln_w + ln_b
        zb = zb_ref[0, :, g["lanes"]]
        o_ref[0, :, g["lanes"]] = ((yn + bonus) * (zb * jax.nn.sigmoid(zb))).astype(o_ref.dtype)

    _interleave([finish(g, y) for g, y in zip(groups, ys)])


def _rwkv(proj3, vecs, mu_wa, w2, a2, tb, gps):
    b, s, _ = proj3.shape
    gw = gps * GW
    ng = RWKV_WIDTH // gw

    def col(off):
        return pl.BlockSpec((1, tb, gw), lambda bi, g, ti, off=off: (bi, ti, off // gw + g))

    in_specs = [
        col(P_RKV), col(P_RKV + RWKV_WIDTH), col(P_RKV + 2 * RWKV_WIDTH),
        pl.BlockSpec((1, tb, LANE), lambda bi, g, ti: (bi, ti, P_WDAD // LANE)),
        col(P_ZB),
        pl.BlockSpec((vecs.shape[0], gw), lambda bi, g, ti: (0, g)),
        pl.BlockSpec((1, LANE), lambda bi, g, ti: (0, 0)),
        pl.BlockSpec((LORA, gw), lambda bi, g, ti: (0, g)),
        pl.BlockSpec((LORA, gw), lambda bi, g, ti: (0, g)),
    ]
    return pl.pallas_call(
        _rwkv_kernel,
        grid=(b, ng, s // tb),
        in_specs=in_specs,
        out_specs=pl.BlockSpec((1, tb, gw), lambda bi, g, ti: (bi, ti, g)),
        out_shape=jax.ShapeDtypeStruct((b, s, RWKV_WIDTH), BF16),
        scratch_shapes=[pltpu.VMEM((gps, GW, GW), F32), pltpu.VMEM((1, gw), F32), pltpu.VMEM((1, gw), F32),
                        pltpu.VMEM((1, gw), F32), pltpu.VMEM((1, LANE), F32)],
        compiler_params=pltpu.CompilerParams(
            dimension_semantics=("parallel", "parallel", "arbitrary")),
        name="rwkv",
    )(proj3, proj3, proj3, proj3, proj3, vecs, mu_wa, w2, a2)


def _outproj_kernel(ma_ref, mb_ref, wa_ref, wb_ref, x_ref, g_ref, o_ref):
    y = _mm(ma_ref[...], wa_ref[...]) + _mm(mb_ref[...], wb_ref[...])
    ms = jnp.mean(y * y, axis=-1, keepdims=True)
    o_ref[...] = x_ref[...] + y * lax.rsqrt(ms + NORM_EPS) * g_ref[...]


def _outproj(mix_a, mix_b, w_a, w_b, x2, g, tm):
    m, d = x2.shape
    ka, kb = mix_a.shape[1], mix_b.shape[1]
    return pl.pallas_call(
        _outproj_kernel,
        grid=(m // tm,),
        in_specs=[
            pl.BlockSpec((tm, ka), lambda i: (i, 0)),
            pl.BlockSpec((tm, kb), lambda i: (i, 0)),
            pl.BlockSpec((ka, d), lambda i: (0, 0)),
            pl.BlockSpec((kb, d), lambda i: (0, 0)),
            pl.BlockSpec((tm, d), lambda i: (i, 0)),
            pl.BlockSpec((1, d), lambda i: (0, 0)),
        ],
        out_specs=pl.BlockSpec((tm, d), lambda i: (i, 0)),
        out_shape=jax.ShapeDtypeStruct((m, d), F32),
        compiler_params=pltpu.CompilerParams(
            dimension_semantics=("parallel",), vmem_limit_bytes=56 * 1024 * 1024),
        name="outproj",
    )(mix_a, mix_b, w_a, w_b, x2, g)


def _permute_w_in(w):
    d = w.shape[0]
    gate = w[:, R_GATE:R_ZA].reshape(d, 3, NSA_KV_HEADS, NSA_GROUP)
    gate_blocks = []
    for g in range(NSA_KV_HEADS):
        gb = gate[:, :, g, :].reshape(d, 3 * NSA_GROUP)
        gate_blocks.append(jnp.pad(gb, ((0, 0), (0, LANE - 3 * NSA_GROUP))))
    parts = [
        w[:, R_Q:R_KV], w[:, R_FEAT:R_FEAT + 3 * RWKV_WIDTH], w[:, R_ZB:R_END], w[:, R_ZA:R_FEAT],
        w[:, R_KV:R_GATE], w[:, R_FEAT + 3 * RWKV_WIDTH:R_ZB],
    ] + gate_blocks
    wp = jnp.concatenate(parts, axis=1)
    return jnp.pad(wp, ((0, 0), (0, NP - wp.shape[1]))).astype(BF16)


def _block(x, pre_norm_g, w_in, rel_bias_table, cmp_pos_k, cmp_pos_v, cmp_k_w1, cmp_k_w2, cmp_v_w1,
           cmp_v_w2, rwkv_mu, rwkv_w0, rwkv_w2, rwkv_a0, rwkv_a2, rwkv_k_k, rwkv_k_a, rwkv_r_k,
           rwkv_ln_w, rwkv_ln_b, w_out, post_norm_g):
    b, s, d = x.shape
    x2 = x.reshape(b * s, d)
    tm = min(1024, b * s)
    proj = _inproj(x2, pre_norm_g.reshape(1, d), _permute_w_in(w_in), tm, 1024)
    proj3 = proj.reshape(b, s, NP)

    bias_c, toep = _bias(rel_bias_table.reshape(-1), s)
    half = CMP_STRIDE * HEAD_DIM
    mix_a = _nsa(proj3, bias_c, toep,
                 cmp_pos_k.reshape(2, half), cmp_k_w1.astype(BF16), cmp_k_w2.astype(BF16),
                 cmp_pos_v.reshape(2, half), cmp_v_w1.astype(BF16), cmp_v_w2.astype(BF16))

    w3 = 3 * RWKV_WIDTH
    vec_rows = [rwkv_mu[:RWKV_WIDTH], rwkv_mu[RWKV_WIDTH:2 * RWKV_WIDTH], rwkv_mu[2 * RWKV_WIDTH:w3],
                rwkv_w0, rwkv_a0, rwkv_k_k, rwkv_k_a, rwkv_ln_w, rwkv_ln_b, rwkv_r_k.reshape(-1)]
    vecs = jnp.stack(vec_rows + [jnp.zeros_like(rwkv_w0)] * (16 - len(vec_rows)), axis=0)
    mix_b = _rwkv(proj3, vecs, rwkv_mu[w3:].reshape(1, 2 * LORA), rwkv_w2.astype(BF16),
                  rwkv_a2.astype(BF16), min(256, s), RWKV_GROUPS_PER_STEP)

    w_o = w_out.astype(BF16)
    out = _outproj(mix_a.reshape(b * s, NSA_WIDTH), mix_b.reshape(b * s, RWKV_WIDTH),
                   w_o[:NSA_WIDTH], w_o[NSA_WIDTH:], x2, post_norm_g.reshape(1, d), min(512, b * s))
    return out.reshape(b, s, d)


def kernel(x, pre_norm_g, w_in, rel_bias_table, cmp_pos_k, cmp_pos_v, cmp_k_w1, cmp_k_w2, cmp_v_w1,
           cmp_v_w2, rwkv_mu, rwkv_w0, rwkv_w2, rwkv_a0, rwkv_a2, rwkv_k_k, rwkv_k_a, rwkv_r_k,
           rwkv_ln_w, rwkv_ln_b, w_out, post_norm_g):
    h = x
    for l in range(pre_norm_g.shape[0]):
        h = _block(h, pre_norm_g[l], w_in[l], rel_bias_table, cmp_pos_k[l], cmp_pos_v[l], cmp_k_w1[l],
                   cmp_k_w2[l], cmp_v_w1[l], cmp_v_w2[l], rwkv_mu[l], rwkv_w0[l], rwkv_w2[l],
                   rwkv_a0[l], rwkv_a2[l], rwkv_k_k[l], rwkv_k_a[l], rwkv_r_k[l], rwkv_ln_w[l],
                   rwkv_ln_b[l], w_out[l], post_norm_g[l])
    return h
```

```python
import functools
import math

import numpy as np
import jax
import jax.numpy as jnp
from jax import lax
from jax.experimental import pallas as pl
from jax.experimental.pallas import tpu as pltpu

F32 = jnp.float32
BF16 = jnp.bfloat16
HI = lax.Precision.HIGHEST

D_MODEL = 2048
NSA_HEADS = 8
NSA_KV_HEADS = 2
NSA_GROUP = NSA_HEADS // NSA_KV_HEADS
HEAD_DIM = 128
NSA_WIDTH = NSA_HEADS * HEAD_DIM
CMP_BLOCK = 32
CMP_STRIDE = 16
SLC_BLOCK = 64
SLC_TOP_N = 16
WINDOW = 512
RWKV_WIDTH = 1024
RWKV_HEAD_DIM = 64
RWKV_HEADS = RWKV_WIDTH // RWKV_HEAD_DIM
LORA = 64
NUM_BUCKETS = 32
MAX_DISTANCE = 1024
NORM_EPS = 1e-6
RWKV_GN_EPS = 64e-5

R_Q = 0
R_KV = R_Q + NSA_WIDTH
R_GATE = R_KV + 6 * NSA_KV_HEADS * HEAD_DIM
R_ZA = R_GATE + 3 * NSA_HEADS
R_FEAT = R_ZA + NSA_WIDTH
R_ZB = R_FEAT + 3 * RWKV_WIDTH + 2 * LORA
R_END = R_ZB + RWKV_WIDTH

P_Q = 0
P_RKV = 1024
P_ZB = 4096
P_ZA = 5120
P_KV = 6144
P_WDAD = 7680
P_GATE = 7808
NP = 8192

LANE = 128
TQ = 256
TB = 128
ND = 9
CH = 64
HPG = 4
GW = HPG * RWKV_HEAD_DIM
RWKV_GROUPS_PER_STEP = 4
NEG = -1e30


def _bucket_thresholds():
    out = []
    for k in range(1, NUM_BUCKETS // 2):
        n = 16
        while n ** 8 < (16 ** 8) * (2 ** (3 * k)):
            n += 1
        out.append(n)
    return out


_THR = _bucket_thresholds()


def _mm(a, b, precision=None):
    return jnp.dot(a, b, preferred_element_type=F32, precision=precision)


def _mm_nt(a, b, precision=None):
    return lax.dot_general(a, b, (((1,), (1,)), ((), ())), preferred_element_type=F32,
                           precision=precision)


def _mm_tn(a, b, precision=None):
    return lax.dot_general(a, b, (((0,), (0,)), ((), ())), preferred_element_type=F32,
                           precision=precision)


def _split3(x):
    x1 = x.astype(BF16)
    r1 = x - x1.astype(F32)
    x2 = r1.astype(BF16)
    x3 = (r1 - x2.astype(F32)).astype(BF16)
    return x1, x2, x3


def _mm_split_rhs(a_exact, b):
    b1 = b.astype(BF16)
    b2 = (b - b1.astype(F32)).astype(BF16)
    return _mm(a_exact, b1) + _mm(a_exact, b2)


def _iota(shape, dim):
    return lax.broadcasted_iota(jnp.int32, shape, dim)


def _interleave(gens):
    results = [None] * len(gens)
    live = list(enumerate(gens))
    while live:
        still = []
        for i, g in live:
            try:
                next(g)
                still.append((i, g))
            except StopIteration as stop:
                results[i] = stop.value
        live = still
    return results


def _div_pow2(x, n):
    assert n & (n - 1) == 0
    return x >> (n.bit_length() - 1)


def _mod_pow2(x, n):
    assert n & (n - 1) == 0
    return x & (n - 1)


def _inproj_kernel(x_ref, g_ref, w_ref, o_ref, hn_ref):
    @pl.when(pl.program_id(1) == 0)
    def _():
        x = x_ref[...]
        ms = jnp.mean(x * x, axis=-1, keepdims=True)
        hn_ref[...] = (x * lax.rsqrt(ms + NORM_EPS) * g_ref[...]).astype(BF16)

    o_ref[...] = _mm(hn_ref[...], w_ref[...])


def _inproj(x2, g, w, tm, tn):
    m, d = x2.shape
    n = w.shape[1]
    return pl.pallas_call(
        _inproj_kernel,
        grid=(m // tm, n // tn),
        in_specs=[
            pl.BlockSpec((tm, d), lambda i, j: (i, 0)),
            pl.BlockSpec((1, d), lambda i, j: (0, 0)),
            pl.BlockSpec((d, tn), lambda i, j: (0, j)),
        ],
        out_specs=pl.BlockSpec((tm, tn), lambda i, j: (i, j)),
        out_shape=jax.ShapeDtypeStruct((m, n), F32),
        scratch_shapes=[pltpu.VMEM((tm, d), BF16)],
        compiler_params=pltpu.CompilerParams(
            dimension_semantics=("parallel", "arbitrary"),
            vmem_limit_bytes=56 * 1024 * 1024),
        name="inproj",
    )(x2, g, w)


def _bucket(n):
    n = jnp.maximum(n, 0)
    large = jnp.full(n.shape, NUM_BUCKETS // 2, jnp.int32)
    for thr in _THR:
        large = large + (n >= thr).astype(jnp.int32)
    return jnp.where(n < NUM_BUCKETS // 2, n, large)


def _lookup(bucket, tab_ref, h):
    out = jnp.zeros(bucket.shape, F32)
    for b in range(NUM_BUCKETS):
        out = jnp.where(bucket == b, tab_ref[b * NSA_HEADS + h], out)
    return out


def _bias_kernel(tab_ref, bc_ref, tp_ref):
    h = pl.program_id(0)
    s, nr = bc_ref.shape[1], bc_ref.shape[2]
    dist_c = _iota((s, nr), 0) - (_iota((s, nr), 1) * CMP_STRIDE + (CMP_BLOCK - 1))
    bc_ref[0] = _lookup(_bucket(dist_c), tab_ref, h)
    base = _iota((TB, TB), 0) - _iota((TB, TB), 1)
    for d in range(ND):
        tp_ref[0, d] = _lookup(_bucket(base + d * TB), tab_ref, h)


def _bias(table_flat, s):
    nr = s // CMP_STRIDE
    return pl.pallas_call(
        _bias_kernel,
        grid=(NSA_HEADS,),
        in_specs=[pl.BlockSpec(memory_space=pltpu.SMEM)],
        out_specs=[
            pl.BlockSpec((1, s, nr), lambda h: (h, 0, 0)),
            pl.BlockSpec((1, ND, TB, TB), lambda h: (h, 0, 0, 0)),
        ],
        out_shape=[
            jax.ShapeDtypeStruct((NSA_HEADS, s, nr), F32),
            jax.ShapeDtypeStruct((NSA_HEADS, ND, TB, TB), F32),
        ],
        compiler_params=pltpu.CompilerParams(dimension_semantics=("arbitrary",)),
        name="bias",
    )(table_flat)


def _compress(kv_ref, pos_ref, w1_ref, w2_ref, nr):
    half = CMP_STRIDE * HEAD_DIM
    r = jnp.concatenate(
        [kv_ref[0, pl.ds(m, nr, stride=CMP_STRIDE), :] for m in range(CMP_STRIDE)], axis=1)
    a = _mm((r + pos_ref[0:1, :]).astype(BF16), w1_ref[0:half, :])
    b = _mm((r + pos_ref[1:2, :]).astype(BF16), w1_ref[half:2 * half, :])
    pre = a + pltpu.roll(b, nr - 1, 0)
    h1 = pre * jax.nn.sigmoid(pre)
    return _mm(h1.astype(BF16), w2_ref[...])


def _bias_tile(toep_ref, d_tiles):
    sub = TQ // TB
    rows = []
    for h in range(NSA_GROUP):
        for ri in range(sub):
            rows.append(jnp.concatenate(
                [toep_ref[h, jnp.clip(d_tiles + ri - ci, 0, ND - 1)] for ci in range(sub)], axis=1))
    return jnp.concatenate(rows, axis=0)


def _add_shared(s, mask_add):
    n = s.shape[-1]
    return (s.reshape(NSA_GROUP, TQ, n) + mask_add[None]).reshape(NSA_GROUP * TQ, n)


def _fold_lanes(x, op):
    out = x[:, :LANE]
    for c in range(1, x.shape[1] // LANE):
        out = op(out, x[:, c * LANE:(c + 1) * LANE])
    return out


def _nsa_kernel(q_ref, kc_ref, vc_ref, ks_ref, vs_ref, kw_ref, vw_ref, gate_ref, za_ref,
                bc_ref, toep_ref, posk_ref, w1k_ref, w2k_ref, posv_ref, w1v_ref, w2v_ref,
                o_ref, kcs_ref, vcs_ref, s_ref, am_ref):
    qi = pl.program_id(2)
    s_len = kc_ref.shape[1]
    nr = s_len // CMP_STRIDE
    nb = s_len // SLC_BLOCK
    n_sel = min(SLC_TOP_N, nb)

    @pl.when(qi == 0)
    def _():
        kcs_ref[...] = _compress(kc_ref, posk_ref, w1k_ref, w2k_ref, nr).astype(BF16)
        vcs_ref[...] = _compress(vc_ref, posv_ref, w1v_ref, w2v_ref, nr).astype(BF16)

    q0 = qi * TQ
    rows = NSA_GROUP * TQ
    q = q_ref[0] * (HEAD_DIM ** -0.5)
    q4 = jnp.concatenate([q[:, h * HEAD_DIM:(h + 1) * HEAD_DIM] for h in range(NSA_GROUP)],
                         axis=0).astype(BF16)

    sub = TQ // TB
    n_tiles = s_len // TQ
    t_q = q0 + _iota((TQ, TQ), 0)

    def compressed_and_selection():
        t_c = q0 + _iota((TQ, nr), 0)
        i_c = _iota((TQ, nr), 1)
        mask_c = (t_c - (i_c * CMP_STRIDE + (CMP_BLOCK - 1)) >= 0) & (i_c < nr - 1)
        qk = _mm_nt(q4, kcs_ref[...])
        yield
        lg = _add_shared(qk + bc_ref[...].reshape(rows, nr), jnp.where(mask_c, 0.0, NEG))
        keep = lg > 0.5 * NEG
        e = jnp.where(keep, jnp.exp(lg - jnp.max(lg, axis=-1, keepdims=True)), 0.0)
        p = e / jnp.maximum(jnp.sum(e, axis=-1, keepdims=True), 1e-30)
        o_c = _mm(p.astype(BF16), vcs_ref[...])
        psum = p[:TQ]
        for h in range(1, NSA_GROUP):
            psum = psum + p[h * TQ:(h + 1) * TQ]
        ov_i = _iota((nb, nr), 1) * CMP_STRIDE
        ov_j = _iota((nb, nr), 0) * SLC_BLOCK
        ov_t = ((ov_i < ov_j + SLC_BLOCK) & (ov_i + CMP_BLOCK > ov_j)).astype(BF16)
        p1, p2, p3 = _split3(psum)
        imp_t = _mm_nt(ov_t, p1) + _mm_nt(ov_t, p2) + _mm_nt(ov_t, p3)
        yield
        jb = _iota((nb, TQ), 0)
        cur = _div_pow2(q0 + _iota((nb, TQ), 1), SLC_BLOCK)
        forced = (jb == 0) | (jb == cur) | (jb == cur - 1)
        causal = jb <= cur
        score = jnp.where(forced, jnp.inf, jnp.where(causal, imp_t, -jnp.inf))
        rank = jnp.zeros((nb, TQ), jnp.int32)
        for jp in range(nb):
            sj = score[jp:jp + 1, :]
            beats = (sj > score) | ((sj == score) & (jb > jp))
            rank = rank + beats.astype(jnp.int32)
        sel_t = ((rank < n_sel) & causal).astype(BF16)
        expand = (_iota((nb, s_len), 0) == _div_pow2(_iota((nb, s_len), 1), SLC_BLOCK)).astype(BF16)
        chosen = _mm_tn(sel_t, expand)
        yield
        for j in range(n_tiles):
            kpos = j * TQ + _iota((TQ, TQ), 1)
            am_ref[j] = jnp.where((chosen[:, j * TQ:(j + 1) * TQ] > 0.5) & (kpos <= t_q), 0.0, NEG)
        return o_c

    def window():
        n_band = WINDOW // TQ + 1
        qk, oks, v_w = [], [], []
        for c in range(n_band):
            j = qi - (n_band - 1) + c
            jc = jnp.maximum(j, 0)
            k0 = pl.multiple_of(jc * TQ, TQ)
            dist = t_q - (jc * TQ + _iota((TQ, TQ), 1))
            oks.append(((dist >= 0) & (dist < WINDOW) & (j >= 0), jc))
            qk.append(_mm_nt(q4, kw_ref[0, pl.ds(k0, TQ), :].astype(BF16)))
            v_w.append(vw_ref[0, pl.ds(k0, TQ), :].astype(BF16))
        yield
        s_w = [_add_shared(qk[c] + _bias_tile(toep_ref, (qi - oks[c][1]) * sub),
                           jnp.where(oks[c][0], 0.0, NEG)) for c in range(n_band)]
        m_w = s_w[0]
        for c in range(1, n_band):
            m_w = jnp.maximum(m_w, s_w[c])
        m_w = jnp.max(m_w, axis=-1, keepdims=True)
        l_w = jnp.zeros((rows, TQ), F32)
        acc_w = jnp.zeros((rows, HEAD_DIM), F32)
        for c in range(n_band):
            p_c = jnp.exp(s_w[c] - m_w)
            l_w = l_w + p_c
            acc_w = acc_w + _mm(p_c.astype(BF16), v_w[c])
        yield
        return acc_w / jnp.maximum(jnp.sum(l_w, axis=-1, keepdims=True), 1e-30)

    o_c, o_w = _interleave([compressed_and_selection(), window()])

    assert n_tiles % 2 == 0
    n_pairs = _div_pow2(qi + 2, 2)

    def slc_logits(jp, macc):
        pair = (2 * jp, 2 * jp + 1)
        qk = [_mm_nt(q4, ks_ref[0, pl.ds(pl.multiple_of(j * TQ, TQ), TQ), :].astype(BF16)) for j in pair]
        for j, qk_j in zip(pair, qk):
            s = _add_shared(qk_j + _bias_tile(toep_ref, (qi - j) * sub), am_ref[j])
            s_ref[j] = s
            macc = jnp.maximum(macc, _fold_lanes(s, jnp.maximum))
        return macc

    macc = lax.fori_loop(0, n_pairs, slc_logits, jnp.full((rows, LANE), NEG, F32))
    m_s = jnp.max(macc, axis=-1, keepdims=True)

    def slc_values(jp, carry):
        lacc, acc = carry
        for j in (2 * jp, 2 * jp + 1):
            p_j = jnp.exp(s_ref[j] - m_s)
            vt = vs_ref[0, pl.ds(pl.multiple_of(j * TQ, TQ), TQ), :].astype(BF16)
            lacc = lacc + _fold_lanes(p_j, jnp.add)
            acc = acc + _mm(p_j.astype(BF16), vt)
        return lacc, acc

    lacc, acc = lax.fori_loop(0, n_pairs, slc_values,
                              (jnp.zeros((rows, LANE), F32), jnp.zeros((rows, HEAD_DIM), F32)))
    o_s = acc / jnp.maximum(jnp.sum(lacc, axis=-1, keepdims=True), 1e-30)

    gts = jax.nn.sigmoid(gate_ref[0])
    za = za_ref[0]
    for h in range(NSA_GROUP):
        hr = slice(h * TQ, (h + 1) * TQ)
        o = (gts[:, h:h + 1] * o_c[hr]
             + gts[:, NSA_GROUP + h:NSA_GROUP + h + 1] * o_s[hr]
             + gts[:, 2 * NSA_GROUP + h:2 * NSA_GROUP + h + 1] * o_w[hr])
        z = za[:, h * HEAD_DIM:(h + 1) * HEAD_DIM]
        o_ref[0, :, h * HEAD_DIM:(h + 1) * HEAD_DIM] = (o * (z * jax.nn.sigmoid(z))).astype(o_ref.dtype)


def _nsa(proj3, bias_c, toep, posk, w1k, w2k, posv, w1v, w2v):
    b, s, _ = proj3.shape
    nr = s // CMP_STRIDE
    gq = NSA_GROUP * HEAD_DIM

    def kvspec(idx):
        return pl.BlockSpec((1, s, HEAD_DIM), lambda bi, g, qi, idx=idx: (bi, 0, P_KV // HEAD_DIM + 2 * idx + g))

    def whole(a):
        return pl.BlockSpec(a.shape, lambda bi, g, qi, nd=a.ndim: (0,) * nd)

    in_specs = [
        pl.BlockSpec((1, TQ, gq), lambda bi, g, qi: (bi, qi, P_Q // gq + g)),
        kvspec(0), kvspec(1), kvspec(2), kvspec(3), kvspec(4), kvspec(5),
        pl.BlockSpec((1, TQ, LANE), lambda bi, g, qi: (bi, qi, P_GATE // LANE + g)),
        pl.BlockSpec((1, TQ, gq), lambda bi, g, qi: (bi, qi, P_ZA // gq + g)),
        pl.BlockSpec((NSA_GROUP, TQ, nr), lambda bi, g, qi: (g, qi, 0)),
        pl.BlockSpec((NSA_GROUP, ND, TB, TB), lambda bi, g, qi: (g, 0, 0, 0)),
        whole(posk), whole(w1k), whole(w2k), whole(posv), whole(w1v), whole(w2v),
    ]
    return pl.pallas_call(
        _nsa_kernel,
        grid=(b, NSA_KV_HEADS, s // TQ),
        in_specs=in_specs,
        out_specs=pl.BlockSpec((1, TQ, gq), lambda bi, g, qi: (bi, qi, g)),
        out_shape=jax.ShapeDtypeStruct((b, s, NSA_WIDTH), BF16),
        scratch_shapes=[pltpu.VMEM((nr, HEAD_DIM), BF16), pltpu.VMEM((nr, HEAD_DIM), BF16),
                        pltpu.VMEM((s // TQ, NSA_GROUP * TQ, TQ), F32),
                        pltpu.VMEM((s // TQ, TQ, TQ), F32)],
        compiler_params=pltpu.CompilerParams(
            dimension_semantics=("parallel", "parallel", "arbitrary"),
            vmem_limit_bytes=56 * 1024 * 1024),
        name="nsa",
    )(proj3, proj3, proj3, proj3, proj3, proj3, proj3, proj3, proj3,
      bias_c, toep, posk, w1k, w2k, posv, w1v, w2v)


def _shift_mix(ref, prev_ref, mu, sl):
    x = ref[0, :, sl]
    prev = jnp.where(_iota(x.shape, 0) == 0, prev_ref[:, sl], pltpu.roll(x, 1, 0))
    prev_ref[:, sl] = x[x.shape[0] - 1:]
    return x + mu * (prev - x)


def _rwkv_kernel(r_ref, k_ref, v_ref, wa_ref, zb_ref, vec_ref, muwa_ref, w2_ref, a2_ref,
                 o_ref, st_ref, pr_ref, pk_ref, pv_ref, pwa_ref):
    first = pl.program_id(2) == 0
    tb = r_ref.shape[1]
    n_groups = r_ref.shape[2] // GW
    n_chunks = tb // CH

    @pl.when(first)
    def _():
        for ref in (st_ref, pr_ref, pk_ref, pv_ref, pwa_ref):
            ref[...] = jnp.zeros_like(ref)

    wa = _shift_mix(wa_ref, pwa_ref, muwa_ref[...], slice(0, LANE))
    wd_act = jnp.tanh(wa[:, :LORA]).astype(BF16)
    ad = wa[:, LORA:].astype(BF16)

    seg = (_div_pow2(_iota((GW, GW), 0), RWKV_HEAD_DIM) == _div_pow2(_iota((GW, GW), 1), RWKV_HEAD_DIM))
    segf = seg.astype(F32)
    segb = seg.astype(BF16)
    assert CH == RWKV_HEAD_DIM
    lane_s = _mod_pow2(_iota((CH, GW), 1), CH)
    row_t = _iota((CH, GW), 0)
    strict = lane_s < row_t
    incl = lane_s <= row_t
    eye = (lane_s == row_t).astype(F32)
    ti_r, ti_c = _iota((tb, tb), 0), _iota((tb, tb), 1)
    trib = ((ti_c <= ti_r) & (_div_pow2(ti_c, CH) == _div_pow2(ti_r, CH))).astype(BF16)

    def bd(x):
        xb = x.astype(BF16)
        return jnp.where(seg, jnp.concatenate([xb] * HPG, axis=0), jnp.zeros((), BF16))

    def prep(gi):
        lanes = slice(gi * GW, (gi + 1) * GW)
        vec = vec_ref[:, lanes]
        mu_r, mu_k, mu_v = vec[0:1], vec[1:2], vec[2:3]
        w0, a0, k_k, k_a = vec[3:4], vec[4:5], vec[5:6], vec[6:7]
        r = _shift_mix(r_ref, pr_ref, mu_r, lanes)
        k = _shift_mix(k_ref, pk_ref, mu_k, lanes)
        v = _shift_mix(v_ref, pv_ref, mu_v, lanes)
        w_lora = _mm(wd_act, w2_ref[:, lanes])
        a_lora = _mm(ad, a2_ref[:, lanes])
        kk = k * k_k
        kk_ss = _mm((kk * kk).astype(BF16), segb)
        yield
        w = -jax.nn.softplus(-(w0 + w_lora)) - 0.5
        lw = -jnp.exp(w)
        cum = _mm_split_rhs(trib, lw)
        yield
        a_sig = jax.nn.sigmoid(a0 + a_lora)
        kk = kk / jnp.maximum(jnp.sqrt(kk_ss), 1e-12)
        k = k * (1.0 + (a_sig - 1.0) * k_a)
        return dict(r=r, k=k, v=v, a=-kk, b=kk * a_sig, lw=lw, cum=cum, vec=vec, lanes=lanes)

    groups = _interleave([prep(gi) for gi in range(n_groups)])

    def chunk_local(g, c):
        ts = slice(c * CH, (c + 1) * CH)
        rc, kc, vc, ac, bc, lwc, cum = (g[n][ts] for n in ("r", "k", "v", "a", "b", "lw", "cum"))
        tot = cum[CH - 1:CH]
        e_out = jnp.exp(-cum)
        e_end = jnp.exp(tot - cum)
        r_t = rc * jnp.exp(cum)
        a_t = ac * jnp.exp(cum - lwc)
        lhs = jnp.concatenate([a_t, r_t], axis=0).astype(BF16)
        aa = _mm_nt(lhs, jnp.concatenate([bd(bc * e_out), bd(kc * e_out)], axis=0))
        yield
        a_ab = jnp.where(strict, aa[:CH, :GW], 0.0)
        a_ak = jnp.where(strict, aa[:CH, GW:], 0.0)
        a_rb = jnp.where(incl, aa[CH:, :GW], 0.0)
        a_rk = jnp.where(incl, aa[CH:, GW:], 0.0)
        t_inv = eye + a_ab
        mpow = _mm(a_ab.astype(BF16), bd(a_ab))
        av = _mm(a_ak.astype(BF16), bd(vc))
        yield
        for _ in range(int(math.log2(CH)) - 1):
            res = _mm(jnp.concatenate([t_inv, mpow], axis=0).astype(BF16), bd(mpow))
            yield
            t_inv = t_inv + res[:CH]
            mpow = res[CH:]
        wu = _mm(t_inv.astype(BF16), jnp.concatenate([bd(a_t), bd(av)], axis=1))
        yield
        return dict(
            lhs=jnp.concatenate([wu[:, :GW], r_t], axis=0).astype(BF16), u_loc=wu[:, GW:],
            a_r=jnp.concatenate([a_rb, a_rk], axis=1).astype(BF16), bdv=bd(vc), vc=vc,
            bk_end=jnp.concatenate([bc * e_end, kc * e_end], axis=0).astype(BF16), dec=jnp.exp(tot))

    loc = _interleave([chunk_local(g, c) for g in groups for c in range(n_chunks)])

    def chain(gi):
        ys = []
        g_state = st_ref[gi]
        for c in range(n_chunks):
            lc = loc[gi * n_chunks + c]
            x0 = _mm_nt(lc["lhs"], g_state.astype(BF16))
            yield
            u = x0[:CH] + lc["u_loc"]
            y_c = _mm(lc["a_r"], jnp.concatenate([bd(u), lc["bdv"]], axis=0))
            upd = _mm_tn(jnp.concatenate([u, lc["vc"]], axis=0).astype(BF16), lc["bk_end"])
            yield
            ys.append(x0[CH:] + y_c)
            g_state = g_state * lc["dec"] + upd * segf
        st_ref[gi] = g_state
        return jnp.concatenate(ys, axis=0)

    ys = _interleave([chain(gi) for gi in range(n_groups)])

    def finish(g, y):
        vec = g["vec"]
        ln_w, ln_b, r_k = vec[7:8], vec[8:9], vec[9:10]
        inv_n = 1.0 / RWKV_HEAD_DIM
        mean = _mm(y.astype(BF16), segb) * inv_n
        bonus = _mm((g["r"] * g["k"] * r_k).astype(BF16), segb) * g["v"]
        yield
        yc = y - mean
        var = _mm((yc * yc).astype(BF16), segb) * inv_n
        yield
        yn = yc * lax.rsqrt(var + RWKV_GN_EPS) * ln_w + ln_b
        zb = zb_ref[0, :, g["lanes"]]
        o_ref[0, :, g["lanes"]] = ((yn + bonus) * (zb * jax.nn.sigmoid(zb))).astype(o_ref.dtype)

    _interleave([finish(g, y) for g, y in zip(groups, ys)])


def _rwkv(proj3, vecs, mu_wa, w2, a2, tb, gps):
    b, s, _ = proj3.shape
    gw = gps * GW
    ng = RWKV_WIDTH // gw

    def col(off):
        return pl.BlockSpec((1, tb, gw), lambda bi, g, ti, off=off: (bi, ti, off // gw + g))

    in_specs = [
        col(P_RKV), col(P_RKV + RWKV_WIDTH), col(P_RKV + 2 * RWKV_WIDTH),
        pl.BlockSpec((1, tb, LANE), lambda bi, g, ti: (bi, ti, P_WDAD // LANE)),
        col(P_ZB),
        pl.BlockSpec((vecs.shape[0], gw), lambda bi, g, ti: (0, g)),
        pl.BlockSpec((1, LANE), lambda bi, g, ti: (0, 0)),
        pl.BlockSpec((LORA, gw), lambda bi, g, ti: (0, g)),
        pl.BlockSpec((LORA, gw), lambda bi, g, ti: (0, g)),
    ]
    return pl.pallas_call(
        _rwkv_kernel,
        grid=(b, ng, s // tb),
        in_specs=in_specs,
        out_specs=pl.BlockSpec((1, tb, gw), lambda bi, g, ti: (bi, ti, g)),
        out_shape=jax.ShapeDtypeStruct((b, s, RWKV_WIDTH), BF16),
        scratch_shapes=[pltpu.VMEM((gps, GW, GW), F32), pltpu.VMEM((1, gw), F32), pltpu.VMEM((1, gw), F32),
                        pltpu.VMEM((1, gw), F32), pltpu.VMEM((1, LANE), F32)],
        compiler_params=pltpu.CompilerParams(
            dimension_semantics=("parallel", "parallel", "arbitrary")),
        name="rwkv",
    )(proj3, proj3, proj3, proj3, proj3, vecs, mu_wa, w2, a2)


def _outproj_kernel(ma_ref, mb_ref, wa_ref, wb_ref, x_ref, g_ref, o_ref):
    y = _mm(ma_ref[...], wa_ref[...]) + _mm(mb_ref[...], wb_ref[...])
    ms = jnp.mean(y * y, axis=-1, keepdims=True)
    o_ref[...] = x_ref[...] + y * lax.rsqrt(ms + NORM_EPS) * g_ref[...]


def _outproj(mix_a, mix_b, w_a, w_b, x2, g, tm):
    m, d = x2.shape
    ka, kb = mix_a.shape[1], mix_b.shape[1]
    return pl.pallas_call(
        _outproj_kernel,
        grid=(m // tm,),
        in_specs=[
            pl.BlockSpec((tm, ka), lambda i: (i, 0)),
            pl.BlockSpec((tm, kb), lambda i: (i, 0)),
            pl.BlockSpec((ka, d), lambda i: (0, 0)),
            pl.BlockSpec((kb, d), lambda i: (0, 0)),
            pl.BlockSpec((tm, d), lambda i: (i, 0)),
            pl.BlockSpec((1, d), lambda i: (0, 0)),
        ],
        out_specs=pl.BlockSpec((tm, d), lambda i: (i, 0)),
        out_shape=jax.ShapeDtypeStruct((m, d), F32),
        compiler_params=pltpu.CompilerParams(
            dimension_semantics=("parallel",), vmem_limit_bytes=56 * 1024 * 1024),
        name="outproj",
    )(mix_a, mix_b, w_a, w_b, x2, g)


def _permute_w_in(w):
    d = w.shape[0]
    gate = w[:, R_GATE:R_ZA].reshape(d, 3, NSA_KV_HEADS, NSA_GROUP)
    gate_blocks = []
    for g in range(NSA_KV_HEADS):
        gb = gate[:, :, g, :].reshape(d, 3 * NSA_GROUP)
        gate_blocks.append(jnp.pad(gb, ((0, 0), (0, LANE - 3 * NSA_GROUP))))
    parts = [
        w[:, R_Q:R_KV], w[:, R_FEAT:R_FEAT + 3 * RWKV_WIDTH], w[:, R_ZB:R_END], w[:, R_ZA:R_FEAT],
        w[:, R_KV:R_GATE], w[:, R_FEAT + 3 * RWKV_WIDTH:R_ZB],
    ] + gate_blocks
    wp = jnp.concatenate(parts, axis=1)
    return jnp.pad(wp, ((0, 0), (0, NP - wp.shape[1]))).astype(BF16)


def _block(x, pre_norm_g, w_in, rel_bias_table, cmp_pos_k, cmp_pos_v, cmp_k_w1, cmp_k_w2, cmp_v_w1,
           cmp_v_w2, rwkv_mu, rwkv_w0, rwkv_w2, rwkv_a0, rwkv_a2, rwkv_k_k, rwkv_k_a, rwkv_r_k,
           rwkv_ln_w, rwkv_ln_b, w_out, post_norm_g):
    b, s, d = x.shape
    x2 = x.reshape(b * s, d)
    tm = min(1024, b * s)
    proj = _inproj(x2, pre_norm_g.reshape(1, d), _permute_w_in(w_in), tm, 1024)
    proj3 = proj.reshape(b, s, NP)

    bias_c, toep = _bias(rel_bias_table.reshape(-1), s)
    half = CMP_STRIDE * HEAD_DIM
    mix_a = _nsa(proj3, bias_c, toep,
                 cmp_pos_k.reshape(2, half), cmp_k_w1.astype(BF16), cmp_k_w2.astype(BF16),
                 cmp_pos_v.reshape(2, half), cmp_v_w1.astype(BF16), cmp_v_w2.astype(BF16))

    w3 = 3 * RWKV_WIDTH
    vec_rows = [rwkv_mu[:RWKV_WIDTH], rwkv_mu[RWKV_WIDTH:2 * RWKV_WIDTH], rwkv_mu[2 * RWKV_WIDTH:w3],
                rwkv_w0, rwkv_a0, rwkv_k_k, rwkv_k_a, rwkv_ln_w, rwkv_ln_b, rwkv_r_k.reshape(-1)]
    vecs = jnp.stack(vec_rows + [jnp.zeros_like(rwkv_w0)] * (16 - len(vec_rows)), axis=0)
    mix_b = _rwkv(proj3, vecs, rwkv_mu[w3:].reshape(1, 2 * LORA), rwkv_w2.astype(BF16),
                  rwkv_a2.astype(BF16), min(256, s), RWKV_GROUPS_PER_STEP)

    w_o = w_out.astype(BF16)
    out = _outproj(mix_a.reshape(b * s, NSA_WIDTH), mix_b.reshape(b * s, RWKV_WIDTH),
                   w_o[:NSA_WIDTH], w_o[NSA_WIDTH:], x2, post_norm_g.reshape(1, d), min(512, b * s))
    return out.reshape(b, s, d)


def kernel(x, pre_norm_g, w_in, rel_bias_table, cmp_pos_k, cmp_pos_v, cmp_k_w1, cmp_k_w2, cmp_v_w1,
           cmp_v_w2, rwkv_mu, rwkv_w0, rwkv_w2, rwkv_a0, rwkv_a2, rwkv_k_k, rwkv_k_a, rwkv_r_k,
           rwkv_ln_w, rwkv_ln_b, w_out, post_norm_g):
    h = x
    for l in range(pre_norm_g.shape[0]):
        h = _block(h, pre_norm_g[l], w_in[l], rel_bias_table, cmp_pos_k[l], cmp_pos_v[l], cmp_k_w1[l],
                   cmp_k_w2[l], cmp_v_w1[l], cmp_v_w2[l], rwkv_mu[l], rwkv_w0[l], rwkv_w2[l],
                   rwkv_a0[l], rwkv_a2[l], rwkv_k_k[l], rwkv_k_a[l], rwkv_r_k[l], rwkv_ln_w[l],
                   rwkv_ln_b[l], w_out[l], post_norm_g[l])
    return h
```

```python
import functools
import math

import numpy as np
import jax
import jax.numpy as jnp
from jax import lax
from jax.experimental import pallas as pl
from jax.experimental.pallas import tpu as pltpu

F32 = jnp.float32
BF16 = jnp.bfloat16
HI = lax.Precision.HIGHEST

D_MODEL = 2048
NSA_HEADS = 8
NSA_KV_HEADS = 2
NSA_GROUP = NSA_HEADS // NSA_KV_HEADS
HEAD_DIM = 128
NSA_WIDTH = NSA_HEADS * HEAD_DIM
CMP_BLOCK = 32
CMP_STRIDE = 16
SLC_BLOCK = 64
SLC_TOP_N = 16
WINDOW = 512
RWKV_WIDTH = 1024
RWKV_HEAD_DIM = 64
RWKV_HEADS = RWKV_WIDTH // RWKV_HEAD_DIM
LORA = 64
NUM_BUCKETS = 32
MAX_DISTANCE = 1024
NORM_EPS = 1e-6
RWKV_GN_EPS = 64e-5

R_Q = 0
R_KV = R_Q + NSA_WIDTH
R_GATE = R_KV + 6 * NSA_KV_HEADS * HEAD_DIM
R_ZA = R_GATE + 3 * NSA_HEADS
R_FEAT = R_ZA + NSA_WIDTH
R_ZB = R_FEAT + 3 * RWKV_WIDTH + 2 * LORA
R_END = R_ZB + RWKV_WIDTH

P_Q = 0
P_RKV = 1024
P_ZB = 4096
P_ZA = 5120
P_KV = 6144
P_WDAD = 7680
P_GATE = 7808
NP = 8192

LANE = 128
TQ = 256
TB = 128
ND = 9
CH = 64
HPG = 4
GW = HPG * RWKV_HEAD_DIM
RWKV_GROUPS_PER_STEP = 4
NEG = -1e30


def _bucket_thresholds():
    out = []
    for k in range(1, NUM_BUCKETS // 2):
        n = 16
        while n ** 8 < (16 ** 8) * (2 ** (3 * k)):
            n += 1
        out.append(n)
    return out


_THR = _bucket_thresholds()


def _mm(a, b, precision=None):
    return jnp.dot(a, b, preferred_element_type=F32, precision=precision)


def _mm_nt(a, b, precision=None):
    return lax.dot_general(a, b, (((1,), (1,)), ((), ())), preferred_element_type=F32,
                           precision=precision)


def _mm_tn(a, b, precision=None):
    return lax.dot_general(a, b, (((0,), (0,)), ((), ())), preferred_element_type=F32,
                           precision=precision)


def _split3(x):
    x1 = x.astype(BF16)
    r1 = x - x1.astype(F32)
    x2 = r1.astype(BF16)
    x3 = (r1 - x2.astype(F32)).astype(BF16)
    return x1, x2, x3


def _mm_split_rhs(a_exact, b):
    b1 = b.astype(BF16)
    b2 = (b - b1.astype(F32)).astype(BF16)
    return _mm(a_exact, b1) + _mm(a_exact, b2)


def _iota(shape, dim):
    return lax.broadcasted_iota(jnp.int32, shape, dim)


def _interleave(gens):
    results = [None] * len(gens)
    live = list(enumerate(gens))
    while live:
        still = []
        for i, g in live:
            try:
                next(g)
                still.append((i, g))
            except StopIteration as stop:
                results[i] = stop.value
        live = still
    return results


def _div_pow2(x, n):
    assert n & (n - 1) == 0
    return x >> (n.bit_length() - 1)


def _mod_pow2(x, n):
    assert n & (n - 1) == 0
    return x & (n - 1)


def _inproj_kernel(x_ref, g_ref, w_ref, o_ref, hn_ref):
    @pl.when(pl.program_id(1) == 0)
    def _():
        x = x_ref[...]
        ms = jnp.mean(x * x, axis=-1, keepdims=True)
        hn_ref[...] = (x * lax.rsqrt(ms + NORM_EPS) * g_ref[...]).astype(BF16)

    o_ref[...] = _mm(hn_ref[...], w_ref[...])


def _inproj(x2, g, w, tm, tn):
    m, d = x2.shape
    n = w.shape[1]
    return pl.pallas_call(
        _inproj_kernel,
        grid=(m // tm, n // tn),
        in_specs=[
            pl.BlockSpec((tm, d), lambda i, j: (i, 0)),
            pl.BlockSpec((1, d), lambda i, j: (0, 0)),
            pl.BlockSpec((d, tn), lambda i, j: (0, j)),
        ],
        out_specs=pl.BlockSpec((tm, tn), lambda i, j: (i, j)),
        out_shape=jax.ShapeDtypeStruct((m, n), F32),
        scratch_shapes=[pltpu.VMEM((tm, d), BF16)],
        compiler_params=pltpu.CompilerParams(
            dimension_semantics=("parallel", "arbitrary"),
            vmem_limit_bytes=56 * 1024 * 1024),
        name="inproj",
    )(x2, g, w)


def _bucket(n):
    n = jnp.maximum(n, 0)
    large = jnp.full(n.shape, NUM_BUCKETS // 2, jnp.int32)
    for thr in _THR:
        large = large + (n >= thr).astype(jnp.int32)
    return jnp.where(n < NUM_BUCKETS // 2, n, large)


def _lookup(bucket, tab_ref, h):
    out = jnp.zeros(bucket.shape, F32)
    for b in range(NUM_BUCKETS):
        out = jnp.where(bucket == b, tab_ref[b * NSA_HEADS + h], out)
    return out


def _bias_kernel(tab_ref, bc_ref, tp_ref):
    h = pl.program_id(0)
    s, nr = bc_ref.shape[1], bc_ref.shape[2]
    dist_c = _iota((s, nr), 0) - (_iota((s, nr), 1) * CMP_STRIDE + (CMP_BLOCK - 1))
    bc_ref[0] = _lookup(_bucket(dist_c), tab_ref, h)
    base = _iota((TB, TB), 0) - _iota((TB, TB), 1)
    for d in range(ND):
        tp_ref[0, d] = _lookup(_bucket(base + d * TB), tab_ref, h)


def _bias(table_flat, s):
    nr = s // CMP_STRIDE
    return pl.pallas_call(
        _bias_kernel,
        grid=(NSA_HEADS,),
        in_specs=[pl.BlockSpec(memory_space=pltpu.SMEM)],
        out_specs=[
            pl.BlockSpec((1, s, nr), lambda h: (h, 0, 0)),
            pl.BlockSpec((1, ND, TB, TB), lambda h: (h, 0, 0, 0)),
        ],
        out_shape=[
            jax.ShapeDtypeStruct((NSA_HEADS, s, nr), F32),
            jax.ShapeDtypeStruct((NSA_HEADS, ND, TB, TB), F32),
        ],
        compiler_params=pltpu.CompilerParams(dimension_semantics=("arbitrary",)),
        name="bias",
    )(table_flat)


def _compress(kv_ref, pos_ref, w1_ref, w2_ref, nr):
    half = CMP_STRIDE * HEAD_DIM
    r = jnp.concatenate(
        [kv_ref[0, pl.ds(m, nr, stride=CMP_STRIDE), :] for m in range(CMP_STRIDE)], axis=1)
    a = _mm((r + pos_ref[0:1, :]).astype(BF16), w1_ref[0:half, :])
    b = _mm((r + pos_ref[1:2, :]).astype(BF16), w1_ref[half:2 * half, :])
    pre = a + pltpu.roll(b, nr - 1, 0)
    h1 = pre * jax.nn.sigmoid(pre)
    return _mm(h1.astype(BF16), w2_ref[...])


def _bias_tile(toep_ref, d_tiles):
    sub = TQ // TB
    rows = []
    for h in range(NSA_GROUP):
        for ri in range(sub):
            rows.append(jnp.concatenate(
                [toep_ref[h, jnp.clip(d_tiles + ri - ci, 0, ND - 1)] for ci in range(sub)], axis=1))
    return jnp.concatenate(rows, axis=0)


def _add_shared(s, mask_add):
    n = s.shape[-1]
    return (s.reshape(NSA_GROUP, TQ, n) + mask_add[None]).reshape(NSA_GROUP * TQ, n)


def _fold_lanes(x, op):
    out = x[:, :LANE]
    for c in range(1, x.shape[1] // LANE):
        out = op(out, x[:, c * LANE:(c + 1) * LANE])
    return out


def _nsa_kernel(q_ref, kc_ref, vc_ref, ks_ref, vs_ref, kw_ref, vw_ref, gate_ref, za_ref,
                bc_ref, toep_ref, posk_ref, w1k_ref, w2k_ref, posv_ref, w1v_ref, w2v_ref,
                o_ref, kcs_ref, vcs_ref, s_ref, am_ref):
    qi = pl.program_id(2)
    s_len = kc_ref.shape[1]
    nr = s_len // CMP_STRIDE
    nb = s_len // SLC_BLOCK
    n_sel = min(SLC_TOP_N, nb)

    @pl.when(qi == 0)
    def _():
        kcs_ref[...] = _compress(kc_ref, posk_ref, w1k_ref, w2k_ref, nr).astype(BF16)
        vcs_ref[...] = _compress(vc_ref, posv_ref, w1v_ref, w2v_ref, nr).astype(BF16)

    q0 = qi * TQ
    rows = NSA_GROUP * TQ
    q = q_ref[0] * (HEAD_DIM ** -0.5)
    q4 = jnp.concatenate([q[:, h * HEAD_DIM:(h + 1) * HEAD_DIM] for h in range(NSA_GROUP)],
                         axis=0).astype(BF16)

    sub = TQ // TB
    n_tiles = s_len // TQ
    t_q = q0 + _iota((TQ, TQ), 0)

    def compressed_and_selection():
        t_c = q0 + _iota((TQ, nr), 0)
        i_c = _iota((TQ, nr), 1)
        mask_c = (t_c - (i_c * CMP_STRIDE + (CMP_BLOCK - 1)) >= 0) & (i_c < nr - 1)
        qk = _mm_nt(q4, kcs_ref[...])
        yield
        lg = _add_shared(qk + bc_ref[...].reshape(rows, nr), jnp.where(mask_c, 0.0, NEG))
        keep = lg > 0.5 * NEG
        e = jnp.where(keep, jnp.exp(lg - jnp.max(lg, axis=-1, keepdims=True)), 0.0)
        p = e / jnp.maximum(jnp.sum(e, axis=-1, keepdims=True), 1e-30)
        o_c = _mm(p.astype(BF16), vcs_ref[...])
        psum = p[:TQ]
        for h in range(1, NSA_GROUP):
            psum = psum + p[h * TQ:(h + 1) * TQ]
        ov_i = _iota((nb, nr), 1) * CMP_STRIDE
        ov_j = _iota((nb, nr), 0) * SLC_BLOCK
        ov_t = ((ov_i < ov_j + SLC_BLOCK) & (ov_i + CMP_BLOCK > ov_j)).astype(BF16)
        p1, p2, p3 = _split3(psum)
        imp_t = _mm_nt(ov_t, p1) + _mm_nt(ov_t, p2) + _mm_nt(ov_t, p3)
        yield
        jb = _iota((nb, TQ), 0)
        cur = _div_pow2(q0 + _iota((nb, TQ), 1), SLC_BLOCK)
        forced = (jb == 0) | (jb == cur) | (jb == cur - 1)
        causal = jb <= cur
        score = jnp.where(forced, jnp.inf, jnp.where(causal, imp_t, -jnp.inf))
        rank = jnp.zeros((nb, TQ), jnp.int32)
        for jp in range(nb):
            sj = score[jp:jp + 1, :]
            beats = (sj > score) | ((sj == score) & (jb > jp))
            rank = rank + beats.astype(jnp.int32)
        sel_t = ((rank < n_sel) & causal).astype(BF16)
        expand = (_iota((nb, s_len), 0) == _div_pow2(_iota((nb, s_len), 1), SLC_BLOCK)).astype(BF16)
        chosen = _mm_tn(sel_t, expand)
        yield
        for j in range(n_tiles):
            kpos = j * TQ + _iota((TQ, TQ), 1)
            am_ref[j] = jnp.where((chosen[:, j * TQ:(j + 1) * TQ] > 0.5) & (kpos <= t_q), 0.0, NEG)
        return o_c

    def window():
        n_band = WINDOW // TQ + 1
        qk, oks, v_w = [], [], []
        for c in range(n_band):
            j = qi - (n_band - 1) + c
            jc = jnp.maximum(j, 0)
            k0 = pl.multiple_of(jc * TQ, TQ)
            dist = t_q - (jc * TQ + _iota((TQ, TQ), 1))
            oks.append(((dist >= 0) & (dist < WINDOW) & (j >= 0), jc))
            qk.append(_mm_nt(q4, kw_ref[0, pl.ds(k0, TQ), :].astype(BF16)))
            v_w.append(vw_ref[0, pl.ds(k0, TQ), :].astype(BF16))
        yield
        s_w = [_add_shared(qk[c] + _bias_tile(toep_ref, (qi - oks[c][1]) * sub),
                           jnp.where(oks[c][0], 0.0, NEG)) for c in range(n_band)]
        m_w = s_w[0]
        for c in range(1, n_band):
            m_w = jnp.maximum(m_w, s_w[c])
        m_w = jnp.max(m_w, axis=-1, keepdims=True)
        l_w = jnp.zeros((rows, TQ), F32)
        acc_w = jnp.zeros((rows, HEAD_DIM), F32)
        for c in range(n_band):
            p_c = jnp.exp(s_w[c] - m_w)
            l_w = l_w + p_c
            acc_w = acc_w + _mm(p_c.astype(BF16), v_w[c])
        yield
        return acc_w / jnp.maximum(jnp.sum(l_w, axis=-1, keepdims=True), 1e-30)

    o_c, o_w = _interleave([compressed_and_selection(), window()])

    assert n_tiles % 2 == 0
    n_pairs = _div_pow2(qi + 2, 2)

    def slc_logits(jp, macc):
        pair = (2 * jp, 2 * jp + 1)
        qk = [_mm_nt(q4, ks_ref[0, pl.ds(pl.multiple_of(j * TQ, TQ), TQ), :].astype(BF16)) for j in pair]
        for j, qk_j in zip(pair, qk):
            s = _add_shared(qk_j + _bias_tile(toep_ref, (qi - j) * sub), am_ref[j])
            s_ref[j] = s
            macc = jnp.maximum(macc, _fold_lanes(s, jnp.maximum))
        return macc

    macc = lax.fori_loop(0, n_pairs, slc_logits, jnp.full((rows, LANE), NEG, F32))
    m_s = jnp.max(macc, axis=-1, keepdims=True)

    def slc_values(jp, carry):
        lacc, acc = carry
        for j in (2 * jp, 2 * jp + 1):
            p_j = jnp.exp(s_ref[j] - m_s)
            vt = vs_ref[0, pl.ds(pl.multiple_of(j * TQ, TQ), TQ), :].astype(BF16)
            lacc = lacc + _fold_lanes(p_j, jnp.add)
            acc = acc + _mm(p_j.astype(BF16), vt)
        return lacc, acc

    lacc, acc = lax.fori_loop(0, n_pairs, slc_values,
                              (jnp.zeros((rows, LANE), F32), jnp.zeros((rows, HEAD_DIM), F32)))
    o_s = acc / jnp.maximum(jnp.sum(lacc, axis=-1, keepdims=True), 1e-30)

    gts = jax.nn.sigmoid(gate_ref[0])
    grp = pl.program_id(1)
    za = za_ref[0]

    def gate(branch, h):
        cols = [gts[:, branch * NSA_HEADS + g * NSA_GROUP + h:branch * NSA_HEADS + g * NSA_GROUP + h + 1]
                for g in range(NSA_KV_HEADS)]
        out = cols[0]
        for g in range(1, NSA_KV_HEADS):
            out = jnp.where(grp == g, cols[g], out)
        return out

    for h in range(NSA_GROUP):
        hr = slice(h * TQ, (h + 1) * TQ)
        o = gate(0, h) * o_c[hr] + gate(1, h) * o_s[hr] + gate(2, h) * o_w[hr]
        z = za[:, h * HEAD_DIM:(h + 1) * HEAD_DIM]
        o_ref[0, :, h * HEAD_DIM:(h + 1) * HEAD_DIM] = (o * (z * jax.nn.sigmoid(z))).astype(o_ref.dtype)


def _nsa(proj3, bias_c, toep, posk, w1k, w2k, posv, w1v, w2v):
    b, s, _ = proj3.shape
    nr = s // CMP_STRIDE
    gq = NSA_GROUP * HEAD_DIM

    def kvspec(idx):
        return pl.BlockSpec((1, s, HEAD_DIM), lambda bi, g, qi, idx=idx: (bi, 0, P_KV // HEAD_DIM + 2 * idx + g))

    def whole(a):
        return pl.BlockSpec(a.shape, lambda bi, g, qi, nd=a.ndim: (0,) * nd)

    in_specs = [
        pl.BlockSpec((1, TQ, gq), lambda bi, g, qi: (bi, qi, P_Q // gq + g)),
        kvspec(0), kvspec(1), kvspec(2), kvspec(3), kvspec(4), kvspec(5),
        pl.BlockSpec((1, TQ, LANE), lambda bi, g, qi: (bi, qi, P_GATE // LANE)),
        pl.BlockSpec((1, TQ, gq), lambda bi, g, qi: (bi, qi, P_ZA // gq + g)),
        pl.BlockSpec((NSA_GROUP, TQ, nr), lambda bi, g, qi: (g, qi, 0)),
        pl.BlockSpec((NSA_GROUP, ND, TB, TB), lambda bi, g, qi: (g, 0, 0, 0)),
        whole(posk), whole(w1k), whole(w2k), whole(posv), whole(w1v), whole(w2v),
    ]
    return pl.pallas_call(
        _nsa_kernel,
        grid=(b, NSA_KV_HEADS, s // TQ),
        in_specs=in_specs,
        out_specs=pl.BlockSpec((1, TQ, gq), lambda bi, g, qi: (bi, qi, g)),
        out_shape=jax.ShapeDtypeStruct((b, s, NSA_WIDTH), BF16),
        scratch_shapes=[pltpu.VMEM((nr, HEAD_DIM), BF16), pltpu.VMEM((nr, HEAD_DIM), BF16),
                        pltpu.VMEM((s // TQ, NSA_GROUP * TQ, TQ), F32),
                        pltpu.VMEM((s // TQ, TQ, TQ), F32)],
        compiler_params=pltpu.CompilerParams(
            dimension_semantics=("parallel", "parallel", "arbitrary"),
            vmem_limit_bytes=56 * 1024 * 1024),
        name="nsa",
    )(proj3, proj3, proj3, proj3, proj3, proj3, proj3, proj3, proj3,
      bias_c, toep, posk, w1k, w2k, posv, w1v, w2v)


def _shift_mix(ref, prev_ref, mu, sl):
    x = ref[0, :, sl]
    prev = jnp.where(_iota(x.shape, 0) == 0, prev_ref[:, sl], pltpu.roll(x, 1, 0))
    prev_ref[:, sl] = x[x.shape[0] - 1:]
    return x + mu * (prev - x)


def _rwkv_kernel(r_ref, k_ref, v_ref, wa_ref, zb_ref, vec_ref, muwa_ref, w2_ref, a2_ref,
                 o_ref, st_ref, pr_ref, pk_ref, pv_ref, pwa_ref):
    first = pl.program_id(2) == 0
    tb = r_ref.shape[1]
    n_groups = r_ref.shape[2] // GW
    n_chunks = tb // CH

    @pl.when(first)
    def _():
        for ref in (st_ref, pr_ref, pk_ref, pv_ref, pwa_ref):
            ref[...] = jnp.zeros_like(ref)

    wa = _shift_mix(wa_ref, pwa_ref, muwa_ref[...], slice(0, LANE))
    wd_act = jnp.tanh(wa[:, :LORA]).astype(BF16)
    ad = wa[:, LORA:].astype(BF16)

    seg = (_div_pow2(_iota((GW, GW), 0), RWKV_HEAD_DIM) == _div_pow2(_iota((GW, GW), 1), RWKV_HEAD_DIM))
    segf = seg.astype(F32)
    segb = seg.astype(BF16)
    assert CH == RWKV_HEAD_DIM
    lane_s = _mod_pow2(_iota((CH, GW), 1), CH)
    row_t = _iota((CH, GW), 0)
    strict = lane_s < row_t
    incl = lane_s <= row_t
    eye = (lane_s == row_t).astype(F32)
    ti_r, ti_c = _iota((tb, tb), 0), _iota((tb, tb), 1)
    trib = ((ti_c <= ti_r) & (_div_pow2(ti_c, CH) == _div_pow2(ti_r, CH))).astype(BF16)

    def bd(x):
        xb = x.astype(BF16)
        return jnp.where(seg, jnp.concatenate([xb] * HPG, axis=0), jnp.zeros((), BF16))

    def prep(gi):
        lanes = slice(gi * GW, (gi + 1) * GW)
        vec = vec_ref[:, lanes]
        mu_r, mu_k, mu_v = vec[0:1], vec[1:2], vec[2:3]
        w0, a0, k_k, k_a = vec[3:4], vec[4:5], vec[5:6], vec[6:7]
        r = _shift_mix(r_ref, pr_ref, mu_r, lanes)
        k = _shift_mix(k_ref, pk_ref, mu_k, lanes)
        v = _shift_mix(v_ref, pv_ref, mu_v, lanes)
        w_lora = _mm(wd_act, w2_ref[:, lanes])
        a_lora = _mm(ad, a2_ref[:, lanes])
        kk = k * k_k
        kk_ss = _mm((kk * kk).astype(BF16), segb)
        yield
        w = -jax.nn.softplus(-(w0 + w_lora)) - 0.5
        lw = -jnp.exp(w)
        cum = _mm_split_rhs(trib, lw)
        yield
        a_sig = jax.nn.sigmoid(a0 + a_lora)
        kk = kk / jnp.maximum(jnp.sqrt(kk_ss), 1e-12)
        k = k * (1.0 + (a_sig - 1.0) * k_a)
        return dict(r=r, k=k, v=v, a=-kk, b=kk * a_sig, lw=lw, cum=cum, vec=vec, lanes=lanes)

    groups = _interleave([prep(gi) for gi in range(n_groups)])

    def chunk_local(g, c):
        ts = slice(c * CH, (c + 1) * CH)
        rc, kc, vc, ac, bc, lwc, cum = (g[n][ts] for n in ("r", "k", "v", "a", "b", "lw", "cum"))
        tot = cum[CH - 1:CH]
        e_out = jnp.exp(-cum)
        e_end = jnp.exp(tot - cum)
        r_t = rc * jnp.exp(cum)
        a_t = ac * jnp.exp(cum - lwc)
        lhs = jnp.concatenate([a_t, r_t], axis=0).astype(BF16)
        aa = _mm_nt(lhs, jnp.concatenate([bd(bc * e_out), bd(kc * e_out)], axis=0))
        yield
        a_ab = jnp.where(strict, aa[:CH, :GW], 0.0)
        a_ak = jnp.where(strict, aa[:CH, GW:], 0.0)
        a_rb = jnp.where(incl, aa[CH:, :GW], 0.0)
        a_rk = jnp.where(incl, aa[CH:, GW:], 0.0)
        t_inv = eye + a_ab
        mpow = _mm(a_ab.astype(BF16), bd(a_ab))
        av = _mm(a_ak.astype(BF16), bd(vc))
        yield
        for _ in range(int(math.log2(CH)) - 1):
            res = _mm(jnp.concatenate([t_inv, mpow], axis=0).astype(BF16), bd(mpow))
            yield
            t_inv = t_inv + res[:CH]
            mpow = res[CH:]
        wu = _mm(t_inv.astype(BF16), jnp.concatenate([bd(a_t), bd(av)], axis=1))
        yield
        return dict(
            lhs=jnp.concatenate([wu[:, :GW], r_t], axis=0).astype(BF16), u_loc=wu[:, GW:],
            a_r=jnp.concatenate([a_rb, a_rk], axis=1).astype(BF16), bdv=bd(vc), vc=vc,
            bk_end=jnp.concatenate([bc * e_end, kc * e_end], axis=0).astype(BF16), dec=jnp.exp(tot))

    loc = _interleave([chunk_local(g, c) for g in groups for c in range(n_chunks)])

    def chain(gi):
        ys = []
        g_state = st_ref[gi]
        for c in range(n_chunks):
            lc = loc[gi * n_chunks + c]
            x0 = _mm_nt(lc["lhs"], g_state.astype(BF16))
            yield
            u = x0[:CH] + lc["u_loc"]
            y_c = _mm(lc["a_r"], jnp.concatenate([bd(u), lc["bdv"]], axis=0))
            upd = _mm_tn(jnp.concatenate([u, lc["vc"]], axis=0).astype(BF16), lc["bk_end"])
            yield
            ys.append(x0[CH:] + y_c)
            g_state = g_state * lc["dec"] + upd * segf
        st_ref[gi] = g_state
        return jnp.concatenate(ys, axis=0)

    ys = _interleave([chain(gi) for gi in range(n_groups)])

    def finish(g, y):
        vec = g["vec"]
        ln_w, ln_b, r_k = vec[7:8], vec[8:9], vec[9:10]
        inv_n = 1.0 / RWKV_HEAD_DIM
        mean = _mm(y.astype(BF16), segb) * inv_n
        bonus = _mm((g["r"] * g["k"] * r_k).astype(BF16), segb) * g["v"]
        yield
        yc = y - mean
        var = _mm((yc * yc).astype(BF16), segb) * inv_n
        yield
        yn = yc * lax.rsqrt(var + RWKV_GN_EPS) * ln_w + ln_b
        zb = zb_ref[0, :, g["lanes"]]
        o_ref[0, :, g["lanes"]] = ((yn + bonus) * (zb * jax.nn.sigmoid(zb))).astype(o_ref.dtype)

    _interleave([finish(g, y) for g, y in zip(groups, ys)])


def _rwkv(proj3, vecs, mu_wa, w2, a2, tb, gps):
    b, s, _ = proj3.shape
    gw = gps * GW
    ng = RWKV_WIDTH // gw

    def col(off):
        return pl.BlockSpec((1, tb, gw), lambda bi, g, ti, off=off: (bi, ti, off // gw + g))

    in_specs = [
        col(P_RKV), col(P_RKV + RWKV_WIDTH), col(P_RKV + 2 * RWKV_WIDTH),
        pl.BlockSpec((1, tb, LANE), lambda bi, g, ti: (bi, ti, P_WDAD // LANE)),
        col(P_ZB),
        pl.BlockSpec((vecs.shape[0], gw), lambda bi, g, ti: (0, g)),
        pl.BlockSpec((1, LANE), lambda bi, g, ti: (0, 0)),
        pl.BlockSpec((LORA, gw), lambda bi, g, ti: (0, g)),
        pl.BlockSpec((LORA, gw), lambda bi, g, ti: (0, g)),
    ]
    return pl.pallas_call(
        _rwkv_kernel,
        grid=(b, ng, s // tb),
        in_specs=in_specs,
        out_specs=pl.BlockSpec((1, tb, gw), lambda bi, g, ti: (bi, ti, g)),
        out_shape=jax.ShapeDtypeStruct((b, s, RWKV_WIDTH), BF16),
        scratch_shapes=[pltpu.VMEM((gps, GW, GW), F32), pltpu.VMEM((1, gw), F32), pltpu.VMEM((1, gw), F32),
                        pltpu.VMEM((1, gw), F32), pltpu.VMEM((1, LANE), F32)],
        compiler_params=pltpu.CompilerParams(
            dimension_semantics=("parallel", "parallel", "arbitrary")),
        name="rwkv",
    )(proj3, proj3, proj3, proj3, proj3, vecs, mu_wa, w2, a2)


def _outproj_kernel(ma_ref, mb_ref, wa_ref, wb_ref, x_ref, g_ref, o_ref):
    y = _mm(ma_ref[...], wa_ref[...]) + _mm(mb_ref[...], wb_ref[...])
    ms = jnp.mean(y * y, axis=-1, keepdims=True)
    o_ref[...] = x_ref[...] + y * lax.rsqrt(ms + NORM_EPS) * g_ref[...]


def _outproj(mix_a, mix_b, w_a, w_b, x2, g, tm):
    m, d = x2.shape
    ka, kb = mix_a.shape[1], mix_b.shape[1]
    return pl.pallas_call(
        _outproj_kernel,
        grid=(m // tm,),
        in_specs=[
            pl.BlockSpec((tm, ka), lambda i: (i, 0)),
            pl.BlockSpec((tm, kb), lambda i: (i, 0)),
            pl.BlockSpec((ka, d), lambda i: (0, 0)),
            pl.BlockSpec((kb, d), lambda i: (0, 0)),
            pl.BlockSpec((tm, d), lambda i: (i, 0)),
            pl.BlockSpec((1, d), lambda i: (0, 0)),
        ],
        out_specs=pl.BlockSpec((tm, d), lambda i: (i, 0)),
        out_shape=jax.ShapeDtypeStruct((m, d), F32),
        compiler_params=pltpu.CompilerParams(
            dimension_semantics=("parallel",), vmem_limit_bytes=56 * 1024 * 1024),
        name="outproj",
    )(mix_a, mix_b, w_a, w_b, x2, g)


_W_SEGMENTS = (
    (P_Q, R_Q, NSA_WIDTH),
    (P_RKV, R_FEAT, 3 * RWKV_WIDTH),
    (P_ZB, R_ZB, RWKV_WIDTH),
    (P_ZA, R_ZA, NSA_WIDTH),
    (P_KV, R_KV, 6 * NSA_KV_HEADS * HEAD_DIM),
    (P_WDAD, R_FEAT + 3 * RWKV_WIDTH, 2 * LORA),
    (P_GATE, R_GATE, LANE),
)


def _relayout_kernel(w_ref, o_ref):
    for dst, src, width in _W_SEGMENTS:
        o_ref[:, dst:dst + width] = w_ref[:, src:src + width].astype(BF16)
    used = P_GATE + LANE
    o_ref[:, used:] = jnp.zeros((o_ref.shape[0], NP - used), BF16)


def _permute_w_in(w, rows):
    d, n = w.shape
    return pl.pallas_call(
        _relayout_kernel,
        grid=(d // rows,),
        in_specs=[pl.BlockSpec((rows, n), lambda i: (i, 0))],
        out_specs=pl.BlockSpec((rows, NP), lambda i: (i, 0)),
        out_shape=jax.ShapeDtypeStruct((d, NP), BF16),
        compiler_params=pltpu.CompilerParams(dimension_semantics=("parallel",)),
        name="relayout",
    )(w)


def _block(x, pre_norm_g, w_in, rel_bias_table, cmp_pos_k, cmp_pos_v, cmp_k_w1, cmp_k_w2, cmp_v_w1,
           cmp_v_w2, rwkv_mu, rwkv_w0, rwkv_w2, rwkv_a0, rwkv_a2, rwkv_k_k, rwkv_k_a, rwkv_r_k,
           rwkv_ln_w, rwkv_ln_b, w_out, post_norm_g):
    b, s, d = x.shape
    x2 = x.reshape(b * s, d)
    tm = min(1024, b * s)
    proj = _inproj(x2, pre_norm_g.reshape(1, d), _permute_w_in(w_in, 256), tm, 1024)
    proj3 = proj.reshape(b, s, NP)

    bias_c, toep = _bias(rel_bias_table.reshape(-1), s)
    half = CMP_STRIDE * HEAD_DIM
    mix_a = _nsa(proj3, bias_c, toep,
                 cmp_pos_k.reshape(2, half), cmp_k_w1.astype(BF16), cmp_k_w2.astype(BF16),
                 cmp_pos_v.reshape(2, half), cmp_v_w1.astype(BF16), cmp_v_w2.astype(BF16))

    w3 = 3 * RWKV_WIDTH
    vec_rows = [rwkv_mu[:RWKV_WIDTH], rwkv_mu[RWKV_WIDTH:2 * RWKV_WIDTH], rwkv_mu[2 * RWKV_WIDTH:w3],
                rwkv_w0, rwkv_a0, rwkv_k_k, rwkv_k_a, rwkv_ln_w, rwkv_ln_b, rwkv_r_k.reshape(-1)]
    vecs = jnp.stack(vec_rows + [jnp.zeros_like(rwkv_w0)] * (16 - len(vec_rows)), axis=0)
    mix_b = _rwkv(proj3, vecs, rwkv_mu[w3:].reshape(1, 2 * LORA), rwkv_w2.astype(BF16),
                  rwkv_a2.astype(BF16), min(256, s), RWKV_GROUPS_PER_STEP)

    w_o = w_out.astype(BF16)
    out = _outproj(mix_a.reshape(b * s, NSA_WIDTH), mix_b.reshape(b * s, RWKV_WIDTH),
                   w_o[:NSA_WIDTH], w_o[NSA_WIDTH:], x2, post_norm_g.reshape(1, d), min(512, b * s))
    return out.reshape(b, s, d)


def kernel(x, pre_norm_g, w_in, rel_bias_table, cmp_pos_k, cmp_pos_v, cmp_k_w1, cmp_k_w2, cmp_v_w1,
           cmp_v_w2, rwkv_mu, rwkv_w0, rwkv_w2, rwkv_a0, rwkv_a2, rwkv_k_k, rwkv_k_a, rwkv_r_k,
           rwkv_ln_w, rwkv_ln_b, w_out, post_norm_g):
    h = x
    for l in range(pre_norm_g.shape[0]):
        h = _block(h, pre_norm_g[l], w_in[l], rel_bias_table, cmp_pos_k[l], cmp_pos_v[l], cmp_k_w1[l],
                   cmp_k_w2[l], cmp_v_w1[l], cmp_v_w2[l], rwkv_mu[l], rwkv_w0[l], rwkv_w2[l],
                   rwkv_a0[l], rwkv_a2[l], rwkv_k_k[l], rwkv_k_a[l], rwkv_r_k[l], rwkv_ln_w[l],
                   rwkv_ln_b[l], w_out[l], post_norm_g[l])
    return h
```

```python
import functools
import math

import numpy as np
import jax
import jax.numpy as jnp
from jax import lax
from jax.experimental import pallas as pl
from jax.experimental.pallas import tpu as pltpu

F32 = jnp.float32
BF16 = jnp.bfloat16
HI = lax.Precision.HIGHEST

D_MODEL = 2048
NSA_HEADS = 8
NSA_KV_HEADS = 2
NSA_GROUP = NSA_HEADS // NSA_KV_HEADS
HEAD_DIM = 128
NSA_WIDTH = NSA_HEADS * HEAD_DIM
CMP_BLOCK = 32
CMP_STRIDE = 16
SLC_BLOCK = 64
SLC_TOP_N = 16
WINDOW = 512
RWKV_WIDTH = 1024
RWKV_HEAD_DIM = 64
RWKV_HEADS = RWKV_WIDTH // RWKV_HEAD_DIM
LORA = 64
NUM_BUCKETS = 32
MAX_DISTANCE = 1024
NORM_EPS = 1e-6
RWKV_GN_EPS = 64e-5

R_Q = 0
R_KV = R_Q + NSA_WIDTH
R_GATE = R_KV + 6 * NSA_KV_HEADS * HEAD_DIM
R_ZA = R_GATE + 3 * NSA_HEADS
R_FEAT = R_ZA + NSA_WIDTH
R_ZB = R_FEAT + 3 * RWKV_WIDTH + 2 * LORA
R_END = R_ZB + RWKV_WIDTH

P_Q = 0
P_RKV = 1024
P_ZB = 4096
P_ZA = 5120
P_KV = 6144
P_WDAD = 7680
P_GATE = 7808
NP = 8192

LANE = 128
TQ = 256
TB = 128
ND = 9
CH = 64
HPG = 4
GW = HPG * RWKV_HEAD_DIM
RWKV_GROUPS_PER_STEP = 4
NEG = -1e30


def _bucket_thresholds():
    out = []
    for k in range(1, NUM_BUCKETS // 2):
        n = 16
        while n ** 8 < (16 ** 8) * (2 ** (3 * k)):
            n += 1
        out.append(n)
    return out


_THR = _bucket_thresholds()


def _mm(a, b, precision=None):
    return jnp.dot(a, b, preferred_element_type=F32, precision=precision)


def _mm_nt(a, b, precision=None):
    return lax.dot_general(a, b, (((1,), (1,)), ((), ())), preferred_element_type=F32,
                           precision=precision)


def _mm_tn(a, b, precision=None):
    return lax.dot_general(a, b, (((0,), (0,)), ((), ())), preferred_element_type=F32,
                           precision=precision)


def _split3(x):
    x1 = x.astype(BF16)
    r1 = x - x1.astype(F32)
    x2 = r1.astype(BF16)
    x3 = (r1 - x2.astype(F32)).astype(BF16)
    return x1, x2, x3


def _mm_split_rhs(a_exact, b):
    b1 = b.astype(BF16)
    b2 = (b - b1.astype(F32)).astype(BF16)
    return _mm(a_exact, b1) + _mm(a_exact, b2)


def _iota(shape, dim):
    return lax.broadcasted_iota(jnp.int32, shape, dim)


def _interleave(gens):
    results = [None] * len(gens)
    live = list(enumerate(gens))
    while live:
        still = []
        for i, g in live:
            try:
                next(g)
                still.append((i, g))
            except StopIteration as stop:
                results[i] = stop.value
        live = still
    return results


def _div_pow2(x, n):
    assert n & (n - 1) == 0
    return x >> (n.bit_length() - 1)


def _mod_pow2(x, n):
    assert n & (n - 1) == 0
    return x & (n - 1)


def _inproj_kernel(x_ref, g_ref, w_ref, o_ref, hn_ref):
    @pl.when(pl.program_id(1) == 0)
    def _():
        x = x_ref[...]
        ms = jnp.mean(x * x, axis=-1, keepdims=True)
        hn_ref[...] = (x * lax.rsqrt(ms + NORM_EPS) * g_ref[...]).astype(BF16)

    o_ref[...] = _mm(hn_ref[...], w_ref[...])


def _inproj(x2, g, w, tm, tn):
    m, d = x2.shape
    n = w.shape[1]
    return pl.pallas_call(
        _inproj_kernel,
        grid=(m // tm, n // tn),
        in_specs=[
            pl.BlockSpec((tm, d), lambda i, j: (i, 0)),
            pl.BlockSpec((1, d), lambda i, j: (0, 0)),
            pl.BlockSpec((d, tn), lambda i, j: (0, j)),
        ],
        out_specs=pl.BlockSpec((tm, tn), lambda i, j: (i, j)),
        out_shape=jax.ShapeDtypeStruct((m, n), F32),
        scratch_shapes=[pltpu.VMEM((tm, d), BF16)],
        compiler_params=pltpu.CompilerParams(
            dimension_semantics=("parallel", "arbitrary"),
            vmem_limit_bytes=56 * 1024 * 1024),
        name="inproj",
    )(x2, g, w)


def _bucket(n):
    n = jnp.maximum(n, 0)
    large = jnp.full(n.shape, NUM_BUCKETS // 2, jnp.int32)
    for thr in _THR:
        large = large + (n >= thr).astype(jnp.int32)
    return jnp.where(n < NUM_BUCKETS // 2, n, large)


def _lookup(bucket, tab_ref, h):
    out = jnp.zeros(bucket.shape, F32)
    for b in range(NUM_BUCKETS):
        out = jnp.where(bucket == b, tab_ref[b * NSA_HEADS + h], out)
    return out


def _bias_kernel(tab_ref, bc_ref, tp_ref):
    h = pl.program_id(0)
    s, nr = bc_ref.shape[1], bc_ref.shape[2]
    dist_c = _iota((s, nr), 0) - (_iota((s, nr), 1) * CMP_STRIDE + (CMP_BLOCK - 1))
    bc_ref[0] = _lookup(_bucket(dist_c), tab_ref, h)
    base = _iota((TB, TB), 0) - _iota((TB, TB), 1)
    for d in range(ND):
        tp_ref[0, d] = _lookup(_bucket(base + d * TB), tab_ref, h)


def _bias(table_flat, s):
    nr = s // CMP_STRIDE
    return pl.pallas_call(
        _bias_kernel,
        grid=(NSA_HEADS,),
        in_specs=[pl.BlockSpec(memory_space=pltpu.SMEM)],
        out_specs=[
            pl.BlockSpec((1, s, nr), lambda h: (h, 0, 0)),
            pl.BlockSpec((1, ND, TB, TB), lambda h: (h, 0, 0, 0)),
        ],
        out_shape=[
            jax.ShapeDtypeStruct((NSA_HEADS, s, nr), F32),
            jax.ShapeDtypeStruct((NSA_HEADS, ND, TB, TB), F32),
        ],
        compiler_params=pltpu.CompilerParams(dimension_semantics=("arbitrary",)),
        name="bias",
    )(table_flat)


def _compress(kv_ref, pos_ref, w1_ref, w2_ref, nr):
    half = CMP_STRIDE * HEAD_DIM
    r = jnp.concatenate(
        [kv_ref[0, pl.ds(m, nr, stride=CMP_STRIDE), :] for m in range(CMP_STRIDE)], axis=1)
    a = _mm((r + pos_ref[0:1, :]).astype(BF16), w1_ref[0:half, :])
    b = _mm((r + pos_ref[1:2, :]).astype(BF16), w1_ref[half:2 * half, :])
    pre = a + pltpu.roll(b, nr - 1, 0)
    h1 = pre * jax.nn.sigmoid(pre)
    return _mm(h1.astype(BF16), w2_ref[...])


def _bias_tile(toep_ref, d_tiles):
    sub = TQ // TB
    rows = []
    for h in range(NSA_GROUP):
        for ri in range(sub):
            rows.append(jnp.concatenate(
                [toep_ref[h, jnp.clip(d_tiles + ri - ci, 0, ND - 1)] for ci in range(sub)], axis=1))
    return jnp.concatenate(rows, axis=0)


def _add_shared(s, mask_add):
    n = s.shape[-1]
    return (s.reshape(NSA_GROUP, TQ, n) + mask_add[None]).reshape(NSA_GROUP * TQ, n)


def _fold_lanes(x, op):
    out = x[:, :LANE]
    for c in range(1, x.shape[1] // LANE):
        out = op(out, x[:, c * LANE:(c + 1) * LANE])
    return out


def _nsa_kernel(q_ref, kc_ref, vc_ref, ks_ref, vs_ref, kw_ref, vw_ref, gate_ref, za_ref,
                bc_ref, toep_ref, posk_ref, w1k_ref, w2k_ref, posv_ref, w1v_ref, w2v_ref,
                o_ref, kcs_ref, vcs_ref, s_ref, am_ref):
    qi = pl.program_id(2)
    s_len = kc_ref.shape[1]
    nr = s_len // CMP_STRIDE
    nb = s_len // SLC_BLOCK
    n_sel = min(SLC_TOP_N, nb)

    @pl.when(qi == 0)
    def _():
        kcs_ref[...] = _compress(kc_ref, posk_ref, w1k_ref, w2k_ref, nr).astype(BF16)
        vcs_ref[...] = _compress(vc_ref, posv_ref, w1v_ref, w2v_ref, nr).astype(BF16)

    q0 = qi * TQ
    rows = NSA_GROUP * TQ
    q = q_ref[0] * (HEAD_DIM ** -0.5)
    q4 = jnp.concatenate([q[:, h * HEAD_DIM:(h + 1) * HEAD_DIM] for h in range(NSA_GROUP)],
                         axis=0).astype(BF16)

    sub = TQ // TB
    n_tiles = s_len // TQ
    t_q = q0 + _iota((TQ, TQ), 0)

    def compressed_and_selection():
        t_c = q0 + _iota((TQ, nr), 0)
        i_c = _iota((TQ, nr), 1)
        mask_c = (t_c - (i_c * CMP_STRIDE + (CMP_BLOCK - 1)) >= 0) & (i_c < nr - 1)
        qk = _mm_nt(q4, kcs_ref[...])
        yield
        lg = _add_shared(qk + bc_ref[...].reshape(rows, nr), jnp.where(mask_c, 0.0, NEG))
        keep = lg > 0.5 * NEG
        e = jnp.where(keep, jnp.exp(lg - jnp.max(lg, axis=-1, keepdims=True)), 0.0)
        p = e / jnp.maximum(jnp.sum(e, axis=-1, keepdims=True), 1e-30)
        o_c = _mm(p.astype(BF16), vcs_ref[...])
        psum = p[:TQ]
        for h in range(1, NSA_GROUP):
            psum = psum + p[h * TQ:(h + 1) * TQ]
        ov_i = _iota((nb, nr), 1) * CMP_STRIDE
        ov_j = _iota((nb, nr), 0) * SLC_BLOCK
        ov_t = ((ov_i < ov_j + SLC_BLOCK) & (ov_i + CMP_BLOCK > ov_j)).astype(BF16)
        p1, p2, p3 = _split3(psum)
        imp_t = _mm_nt(ov_t, p1) + _mm_nt(ov_t, p2) + _mm_nt(ov_t, p3)
        yield
        jb = _iota((nb, TQ), 0)
        cur = _div_pow2(q0 + _iota((nb, TQ), 1), SLC_BLOCK)
        forced = (jb == 0) | (jb == cur) | (jb == cur - 1)
        causal = jb <= cur
        score = jnp.where(forced, jnp.inf, jnp.where(causal, imp_t, -jnp.inf))
        rank = jnp.zeros((nb, TQ), jnp.int32)
        for jp in range(nb):
            sj = score[jp:jp + 1, :]
            beats = (sj > score) | ((sj == score) & (jb > jp))
            rank = rank + beats.astype(jnp.int32)
        sel_t = ((rank < n_sel) & causal).astype(BF16)
        expand = (_iota((nb, s_len), 0) == _div_pow2(_iota((nb, s_len), 1), SLC_BLOCK)).astype(BF16)
        chosen = _mm_tn(sel_t, expand)
        yield
        for j in range(n_tiles):
            kpos = j * TQ + _iota((TQ, TQ), 1)
            am_ref[j] = jnp.where((chosen[:, j * TQ:(j + 1) * TQ] > 0.5) & (kpos <= t_q), 0.0, NEG)
        return o_c

    def window():
        n_band = WINDOW // TQ + 1
        qk, oks, v_w = [], [], []
        for c in range(n_band):
            j = qi - (n_band - 1) + c
            jc = jnp.maximum(j, 0)
            k0 = pl.multiple_of(jc * TQ, TQ)
            dist = t_q - (jc * TQ + _iota((TQ, TQ), 1))
            oks.append(((dist >= 0) & (dist < WINDOW) & (j >= 0), jc))
            qk.append(_mm_nt(q4, kw_ref[0, pl.ds(k0, TQ), :].astype(BF16)))
            v_w.append(vw_ref[0, pl.ds(k0, TQ), :].astype(BF16))
        yield
        s_w = [_add_shared(qk[c] + _bias_tile(toep_ref, (qi - oks[c][1]) * sub),
                           jnp.where(oks[c][0], 0.0, NEG)) for c in range(n_band)]
        m_w = s_w[0]
        for c in range(1, n_band):
            m_w = jnp.maximum(m_w, s_w[c])
        m_w = jnp.max(m_w, axis=-1, keepdims=True)
        l_w = jnp.zeros((rows, TQ), F32)
        acc_w = jnp.zeros((rows, HEAD_DIM), F32)
        for c in range(n_band):
            p_c = jnp.exp(s_w[c] - m_w)
            l_w = l_w + p_c
            acc_w = acc_w + _mm(p_c.astype(BF16), v_w[c])
        yield
        return acc_w / jnp.maximum(jnp.sum(l_w, axis=-1, keepdims=True), 1e-30)

    o_c, o_w = _interleave([compressed_and_selection(), window()])

    assert n_tiles % 2 == 0
    n_pairs = _div_pow2(qi + 2, 2)

    def slc_logits(jp, macc):
        pair = (2 * jp, 2 * jp + 1)
        qk = [_mm_nt(q4, ks_ref[0, pl.ds(pl.multiple_of(j * TQ, TQ), TQ), :].astype(BF16)) for j in pair]
        for j, qk_j in zip(pair, qk):
            s = _add_shared(qk_j + _bias_tile(toep_ref, (qi - j) * sub), am_ref[j])
            s_ref[j] = s
            macc = jnp.maximum(macc, _fold_lanes(s, jnp.maximum))
        return macc

    macc = lax.fori_loop(0, n_pairs, slc_logits, jnp.full((rows, LANE), NEG, F32))
    m_s = jnp.max(macc, axis=-1, keepdims=True)

    def slc_values(jp, carry):
        lacc, acc = carry
        for j in (2 * jp, 2 * jp + 1):
            p_j = jnp.exp(s_ref[j] - m_s)
            vt = vs_ref[0, pl.ds(pl.multiple_of(j * TQ, TQ), TQ), :].astype(BF16)
            lacc = lacc + _fold_lanes(p_j, jnp.add)
            acc = acc + _mm(p_j.astype(BF16), vt)
        return lacc, acc

    lacc, acc = lax.fori_loop(0, n_pairs, slc_values,
                              (jnp.zeros((rows, LANE), F32), jnp.zeros((rows, HEAD_DIM), F32)))
    o_s = acc / jnp.maximum(jnp.sum(lacc, axis=-1, keepdims=True), 1e-30)

    gts = jax.nn.sigmoid(gate_ref[0])
    grp = pl.program_id(1)
    za = za_ref[0]

    def gate(branch, h):
        cols = [gts[:, branch * NSA_HEADS + g * NSA_GROUP + h:branch * NSA_HEADS + g * NSA_GROUP + h + 1]
                for g in range(NSA_KV_HEADS)]
        out = cols[0]
        for g in range(1, NSA_KV_HEADS):
            out = jnp.where(grp == g, cols[g], out)
        return out

    for h in range(NSA_GROUP):
        hr = slice(h * TQ, (h + 1) * TQ)
        o = gate(0, h) * o_c[hr] + gate(1, h) * o_s[hr] + gate(2, h) * o_w[hr]
        z = za[:, h * HEAD_DIM:(h + 1) * HEAD_DIM]
        o_ref[0, :, h * HEAD_DIM:(h + 1) * HEAD_DIM] = (o * (z * jax.nn.sigmoid(z))).astype(o_ref.dtype)


def _nsa(proj3, bias_c, toep, posk, w1k, w2k, posv, w1v, w2v):
    b, s, _ = proj3.shape
    nr = s // CMP_STRIDE
    gq = NSA_GROUP * HEAD_DIM

    def kvspec(idx):
        return pl.BlockSpec((1, s, HEAD_DIM), lambda bi, g, qi, idx=idx: (bi, 0, P_KV // HEAD_DIM + 2 * idx + g))

    def whole(a):
        return pl.BlockSpec(a.shape, lambda bi, g, qi, nd=a.ndim: (0,) * nd)

    in_specs = [
        pl.BlockSpec((1, TQ, gq), lambda bi, g, qi: (bi, qi, P_Q // gq + g)),
        kvspec(0), kvspec(1), kvspec(2), kvspec(3), kvspec(4), kvspec(5),
        pl.BlockSpec((1, TQ, LANE), lambda bi, g, qi: (bi, qi, P_GATE // LANE)),
        pl.BlockSpec((1, TQ, gq), lambda bi, g, qi: (bi, qi, P_ZA // gq + g)),
        pl.BlockSpec((NSA_GROUP, TQ, nr), lambda bi, g, qi: (g, qi, 0)),
        pl.BlockSpec((NSA_GROUP, ND, TB, TB), lambda bi, g, qi: (g, 0, 0, 0)),
        whole(posk), whole(w1k), whole(w2k), whole(posv), whole(w1v), whole(w2v),
    ]
    return pl.pallas_call(
        _nsa_kernel,
        grid=(b, NSA_KV_HEADS, s // TQ),
        in_specs=in_specs,
        out_specs=pl.BlockSpec((1, TQ, gq), lambda bi, g, qi: (bi, qi, g)),
        out_shape=jax.ShapeDtypeStruct((b, s, NSA_WIDTH), BF16),
        scratch_shapes=[pltpu.VMEM((nr, HEAD_DIM), BF16), pltpu.VMEM((nr, HEAD_DIM), BF16),
                        pltpu.VMEM((s // TQ, NSA_GROUP * TQ, TQ), F32),
                        pltpu.VMEM((s // TQ, TQ, TQ), F32)],
        compiler_params=pltpu.CompilerParams(
            dimension_semantics=("parallel", "parallel", "arbitrary"),
            vmem_limit_bytes=56 * 1024 * 1024),
        name="nsa",
    )(proj3, proj3, proj3, proj3, proj3, proj3, proj3, proj3, proj3,
      bias_c, toep, posk, w1k, w2k, posv, w1v, w2v)


def _shift_mix(ref, prev_ref, mu, sl):
    x = ref[0, :, sl]
    prev = jnp.where(_iota(x.shape, 0) == 0, prev_ref[:, sl], pltpu.roll(x, 1, 0))
    prev_ref[:, sl] = x[x.shape[0] - 1:]
    return x + mu * (prev - x)


def _rwkv_kernel(r_ref, k_ref, v_ref, wa_ref, zb_ref, vec_ref, muwa_ref, w2_ref, a2_ref,
                 o_ref, st_ref, pr_ref, pk_ref, pv_ref, pwa_ref):
    first = pl.program_id(2) == 0
    tb = r_ref.shape[1]
    n_groups = r_ref.shape[2] // GW
    n_chunks = tb // CH

    @pl.when(first)
    def _():
        for ref in (st_ref, pr_ref, pk_ref, pv_ref, pwa_ref):
            ref[...] = jnp.zeros_like(ref)

    wa = _shift_mix(wa_ref, pwa_ref, muwa_ref[...], slice(0, LANE))
    wd_act = jnp.tanh(wa[:, :LORA]).astype(BF16)
    ad = wa[:, LORA:].astype(BF16)

    seg = (_div_pow2(_iota((GW, GW), 0), RWKV_HEAD_DIM) == _div_pow2(_iota((GW, GW), 1), RWKV_HEAD_DIM))
    segf = seg.astype(F32)
    segb = seg.astype(BF16)
    assert CH == RWKV_HEAD_DIM
    lane_s = _mod_pow2(_iota((CH, GW), 1), CH)
    row_t = _iota((CH, GW), 0)
    strict = lane_s < row_t
    incl = lane_s <= row_t
    eye = (lane_s == row_t).astype(F32)
    ti_r, ti_c = _iota((tb, tb), 0), _iota((tb, tb), 1)
    trib = ((ti_c <= ti_r) & (_div_pow2(ti_c, CH) == _div_pow2(ti_r, CH))).astype(BF16)

    def bd(x):
        xb = x.astype(BF16)
        return jnp.where(seg, jnp.concatenate([xb] * HPG, axis=0), jnp.zeros((), BF16))

    def prep(gi):
        lanes = slice(gi * GW, (gi + 1) * GW)
        vec = vec_ref[:, lanes]
        mu_r, mu_k, mu_v = vec[0:1], vec[1:2], vec[2:3]
        w0, a0, k_k, k_a = vec[3:4], vec[4:5], vec[5:6], vec[6:7]
        r = _shift_mix(r_ref, pr_ref, mu_r, lanes)
        k = _shift_mix(k_ref, pk_ref, mu_k, lanes)
        v = _shift_mix(v_ref, pv_ref, mu_v, lanes)
        w_lora = _mm(wd_act, w2_ref[:, lanes])
        a_lora = _mm(ad, a2_ref[:, lanes])
        kk = k * k_k
        kk_ss = _mm((kk * kk).astype(BF16), segb)
        yield
        w = -jax.nn.softplus(-(w0 + w_lora)) - 0.5
        lw = -jnp.exp(w)
        cum = _mm_split_rhs(trib, lw)
        yield
        a_sig = jax.nn.sigmoid(a0 + a_lora)
        kk = kk / jnp.maximum(jnp.sqrt(kk_ss), 1e-12)
        k = k * (1.0 + (a_sig - 1.0) * k_a)
        return dict(r=r, k=k, v=v, a=-kk, b=kk * a_sig, lw=lw, cum=cum, vec=vec, lanes=lanes)

    groups = _interleave([prep(gi) for gi in range(n_groups)])

    def chunk_local(g, c):
        ts = slice(c * CH, (c + 1) * CH)
        rc, kc, vc, ac, bc, lwc, cum = (g[n][ts] for n in ("r", "k", "v", "a", "b", "lw", "cum"))
        tot = cum[CH - 1:CH]
        e_out = jnp.exp(-cum)
        e_end = jnp.exp(tot - cum)
        r_t = rc * jnp.exp(cum)
        a_t = ac * jnp.exp(cum - lwc)
        lhs = jnp.concatenate([a_t, r_t], axis=0).astype(BF16)
        aa = _mm_nt(lhs, jnp.concatenate([bd(bc * e_out), bd(kc * e_out)], axis=0))
        yield
        a_ab = jnp.where(strict, aa[:CH, :GW], 0.0)
        a_ak = jnp.where(strict, aa[:CH, GW:], 0.0)
        a_rb = jnp.where(incl, aa[CH:, :GW], 0.0)
        a_rk = jnp.where(incl, aa[CH:, GW:], 0.0)
        t_inv = eye + a_ab
        mpow = _mm(a_ab.astype(BF16), bd(a_ab))
        av = _mm(a_ak.astype(BF16), bd(vc))
        yield
        for _ in range(int(math.log2(CH)) - 1):
            res = _mm(jnp.concatenate([t_inv, mpow], axis=0).astype(BF16), bd(mpow))
            yield
            t_inv = t_inv + res[:CH]
            mpow = res[CH:]
        wu = _mm(t_inv.astype(BF16), jnp.concatenate([bd(a_t), bd(av)], axis=1))
        yield
        return dict(
            lhs=jnp.concatenate([wu[:, :GW], r_t], axis=0).astype(BF16), u_loc=wu[:, GW:],
            a_r=jnp.concatenate([a_rb, a_rk], axis=1).astype(BF16), bdv=bd(vc), vc=vc,
            bk_end=jnp.concatenate([bc * e_end, kc * e_end], axis=0).astype(BF16), dec=jnp.exp(tot))

    loc = _interleave([chunk_local(g, c) for g in groups for c in range(n_chunks)])

    def chain(gi):
        ys = []
        g_state = st_ref[gi]
        for c in range(n_chunks):
            lc = loc[gi * n_chunks + c]
            x0 = _mm_nt(lc["lhs"], g_state.astype(BF16))
            yield
            u = x0[:CH] + lc["u_loc"]
            y_c = _mm(lc["a_r"], jnp.concatenate([bd(u), lc["bdv"]], axis=0))
            upd = _mm_tn(jnp.concatenate([u, lc["vc"]], axis=0).astype(BF16), lc["bk_end"])
            yield
            ys.append(x0[CH:] + y_c)
            g_state = g_state * lc["dec"] + upd * segf
        st_ref[gi] = g_state
        return jnp.concatenate(ys, axis=0)

    ys = _interleave([chain(gi) for gi in range(n_groups)])

    def finish(g, y):
        vec = g["vec"]
        ln_w, ln_b, r_k = vec[7:8], vec[8:9], vec[9:10]
        inv_n = 1.0 / RWKV_HEAD_DIM
        mean = _mm(y.astype(BF16), segb) * inv_n
        bonus = _mm((g["r"] * g["k"] * r_k).astype(BF16), segb) * g["v"]
        yield
        yc = y - mean
        var = _mm((yc * yc).astype(BF16), segb) * inv_n
        yield
        yn = yc * lax.rsqrt(var + RWKV_GN_EPS) * ln_w + ln_b
        zb = zb_ref[0, :, g["lanes"]]
        o_ref[0, :, g["lanes"]] = ((yn + bonus) * (zb * jax.nn.sigmoid(zb))).astype(o_ref.dtype)

    _interleave([finish(g, y) for g, y in zip(groups, ys)])


def _rwkv(proj3, vecs, mu_wa, w2, a2, tb, gps):
    b, s, _ = proj3.shape
    gw = gps * GW
    ng = RWKV_WIDTH // gw

    def col(off):
        return pl.BlockSpec((1, tb, gw), lambda bi, g, ti, off=off: (bi, ti, off // gw + g))

    in_specs = [
        col(P_RKV), col(P_RKV + RWKV_WIDTH), col(P_RKV + 2 * RWKV_WIDTH),
        pl.BlockSpec((1, tb, LANE), lambda bi, g, ti: (bi, ti, P_WDAD // LANE)),
        col(P_ZB),
        pl.BlockSpec((vecs.shape[0], gw), lambda bi, g, ti: (0, g)),
        pl.BlockSpec((1, LANE), lambda bi, g, ti: (0, 0)),
        pl.BlockSpec((LORA, gw), lambda bi, g, ti: (0, g)),
        pl.BlockSpec((LORA, gw), lambda bi, g, ti: (0, g)),
    ]
    return pl.pallas_call(
        _rwkv_kernel,
        grid=(b, ng, s // tb),
        in_specs=in_specs,
        out_specs=pl.BlockSpec((1, tb, gw), lambda bi, g, ti: (bi, ti, g)),
        out_shape=jax.ShapeDtypeStruct((b, s, RWKV_WIDTH), BF16),
        scratch_shapes=[pltpu.VMEM((gps, GW, GW), F32), pltpu.VMEM((1, gw), F32), pltpu.VMEM((1, gw), F32),
                        pltpu.VMEM((1, gw), F32), pltpu.VMEM((1, LANE), F32)],
        compiler_params=pltpu.CompilerParams(
            dimension_semantics=("parallel", "parallel", "arbitrary")),
        name="rwkv",
    )(proj3, proj3, proj3, proj3, proj3, vecs, mu_wa, w2, a2)


def _outproj_kernel(ma_ref, mb_ref, wa_ref, wb_ref, x_ref, g_ref, o_ref):
    y = _mm(ma_ref[...], wa_ref[...]) + _mm(mb_ref[...], wb_ref[...])
    ms = jnp.mean(y * y, axis=-1, keepdims=True)
    o_ref[...] = x_ref[...] + y * lax.rsqrt(ms + NORM_EPS) * g_ref[...]


def _outproj(mix_a, mix_b, w_a, w_b, x2, g, tm):
    m, d = x2.shape
    ka, kb = mix_a.shape[1], mix_b.shape[1]
    return pl.pallas_call(
        _outproj_kernel,
        grid=(m // tm,),
        in_specs=[
            pl.BlockSpec((tm, ka), lambda i: (i, 0)),
            pl.BlockSpec((tm, kb), lambda i: (i, 0)),
            pl.BlockSpec((ka, d), lambda i: (0, 0)),
            pl.BlockSpec((kb, d), lambda i: (0, 0)),
            pl.BlockSpec((tm, d), lambda i: (i, 0)),
            pl.BlockSpec((1, d), lambda i: (0, 0)),
        ],
        out_specs=pl.BlockSpec((tm, d), lambda i: (i, 0)),
        out_shape=jax.ShapeDtypeStruct((m, d), F32),
        compiler_params=pltpu.CompilerParams(
            dimension_semantics=("parallel",), vmem_limit_bytes=56 * 1024 * 1024),
        name="outproj",
    )(mix_a, mix_b, w_a, w_b, x2, g)


_W_SEGMENTS = (
    (P_Q, R_Q, NSA_WIDTH),
    (P_RKV, R_FEAT, 3 * RWKV_WIDTH),
    (P_ZB, R_ZB, RWKV_WIDTH),
    (P_ZA, R_ZA, NSA_WIDTH),
    (P_KV, R_KV, 6 * NSA_KV_HEADS * HEAD_DIM),
    (P_WDAD, R_FEAT + 3 * RWKV_WIDTH, 2 * LORA),
    (P_GATE, R_GATE, LANE),
)


def _relayout_kernel(w_ref, o_ref):
    for dst, src, width in _W_SEGMENTS:
        o_ref[:, dst:dst + width] = w_ref[:, src:src + width].astype(BF16)
    used = P_GATE + LANE
    o_ref[:, used:] = jnp.zeros((o_ref.shape[0], NP - used), BF16)


def _permute_w_in(w, rows):
    d, n = w.shape
    return pl.pallas_call(
        _relayout_kernel,
        grid=(d // rows,),
        in_specs=[pl.BlockSpec((rows, n), lambda i: (i, 0))],
        out_specs=pl.BlockSpec((rows, NP), lambda i: (i, 0)),
        out_shape=jax.ShapeDtypeStruct((d, NP), BF16),
        compiler_params=pltpu.CompilerParams(dimension_semantics=("parallel",)),
        name="relayout",
    )(w)


def _block(x, pre_norm_g, w_in, rel_bias_table, cmp_pos_k, cmp_pos_v, cmp_k_w1, cmp_k_w2, cmp_v_w1,
           cmp_v_w2, rwkv_mu, rwkv_w0, rwkv_w2, rwkv_a0, rwkv_a2, rwkv_k_k, rwkv_k_a, rwkv_r_k,
           rwkv_ln_w, rwkv_ln_b, w_out, post_norm_g):
    b, s, d = x.shape
    x2 = x.reshape(b * s, d)
    tm = min(1024, b * s)
    proj = _inproj(x2, pre_norm_g.reshape(1, d), _permute_w_in(w_in.astype(BF16), 256), tm, 1024)
    proj3 = proj.reshape(b, s, NP)

    bias_c, toep = _bias(rel_bias_table.reshape(-1), s)
    half = CMP_STRIDE * HEAD_DIM
    mix_a = _nsa(proj3, bias_c, toep,
                 cmp_pos_k.reshape(2, half), cmp_k_w1.astype(BF16), cmp_k_w2.astype(BF16),
                 cmp_pos_v.reshape(2, half), cmp_v_w1.astype(BF16), cmp_v_w2.astype(BF16))

    w3 = 3 * RWKV_WIDTH
    vec_rows = [rwkv_mu[:RWKV_WIDTH], rwkv_mu[RWKV_WIDTH:2 * RWKV_WIDTH], rwkv_mu[2 * RWKV_WIDTH:w3],
                rwkv_w0, rwkv_a0, rwkv_k_k, rwkv_k_a, rwkv_ln_w, rwkv_ln_b, rwkv_r_k.reshape(-1)]
    vecs = jnp.stack(vec_rows + [jnp.zeros_like(rwkv_w0)] * (16 - len(vec_rows)), axis=0)
    mix_b = _rwkv(proj3, vecs, rwkv_mu[w3:].reshape(1, 2 * LORA), rwkv_w2.astype(BF16),
                  rwkv_a2.astype(BF16), min(256, s), RWKV_GROUPS_PER_STEP)

    w_o = w_out.astype(BF16)
    out = _outproj(mix_a.reshape(b * s, NSA_WIDTH), mix_b.reshape(b * s, RWKV_WIDTH),
                   w_o[:NSA_WIDTH], w_o[NSA_WIDTH:], x2, post_norm_g.reshape(1, d), min(512, b * s))
    return out.reshape(b, s, d)


def kernel(x, pre_norm_g, w_in, rel_bias_table, cmp_pos_k, cmp_pos_v, cmp_k_w1, cmp_k_w2, cmp_v_w1,
           cmp_v_w2, rwkv_mu, rwkv_w0, rwkv_w2, rwkv_a0, rwkv_a2, rwkv_k_k, rwkv_k_a, rwkv_r_k,
           rwkv_ln_w, rwkv_ln_b, w_out, post_norm_g):
    h = x
    for l in range(pre_norm_g.shape[0]):
        h = _block(h, pre_norm_g[l], w_in[l], rel_bias_table, cmp_pos_k[l], cmp_pos_v[l], cmp_k_w1[l],
                   cmp_k_w2[l], cmp_v_w1[l], cmp_v_w2[l], rwkv_mu[l], rwkv_w0[l], rwkv_w2[l],
                   rwkv_a0[l], rwkv_a2[l], rwkv_k_k[l], rwkv_k_a[l], rwkv_r_k[l], rwkv_ln_w[l],
                   rwkv_ln_b[l], w_out[l], post_norm_g[l])
    return h
```

```python
import functools
import math

import numpy as np
import jax
import jax.numpy as jnp
from jax import lax
from jax.experimental import pallas as pl
from jax.experimental.pallas import tpu as pltpu

F32 = jnp.float32
BF16 = jnp.bfloat16
HI = lax.Precision.HIGHEST

D_MODEL = 2048
NSA_HEADS = 8
NSA_KV_HEADS = 2
NSA_GROUP = NSA_HEADS // NSA_KV_HEADS
HEAD_DIM = 128
NSA_WIDTH = NSA_HEADS * HEAD_DIM
CMP_BLOCK = 32
CMP_STRIDE = 16
SLC_BLOCK = 64
SLC_TOP_N = 16
WINDOW = 512
RWKV_WIDTH = 1024
RWKV_HEAD_DIM = 64
RWKV_HEADS = RWKV_WIDTH // RWKV_HEAD_DIM
LORA = 64
NUM_BUCKETS = 32
MAX_DISTANCE = 1024
NORM_EPS = 1e-6
RWKV_GN_EPS = 64e-5

R_Q = 0
R_KV = R_Q + NSA_WIDTH
R_GATE = R_KV + 6 * NSA_KV_HEADS * HEAD_DIM
R_ZA = R_GATE + 3 * NSA_HEADS
R_FEAT = R_ZA + NSA_WIDTH
R_ZB = R_FEAT + 3 * RWKV_WIDTH + 2 * LORA
R_END = R_ZB + RWKV_WIDTH

P_Q = 0
P_RKV = 1024
P_ZB = 4096
P_ZA = 5120
P_KV = 6144
P_WDAD = 7680
P_GATE = 7808
NP = 8192

LANE = 128
TQ = 256
TB = 128
ND = 9
T_DIAG = ND
T_WEND = ND + 1
T_NONE = ND + 2
NT_ALL = ND + 3
LOG2E = math.log2(math.e)
BIG = 2.0 ** 100
CH = 64
HPG = 4
GW = HPG * RWKV_HEAD_DIM
RWKV_GROUPS_PER_STEP = 4
NEG = -1e30


def _bucket_thresholds():
    out = []
    for k in range(1, NUM_BUCKETS // 2):
        n = 16
        while n ** 8 < (16 ** 8) * (2 ** (3 * k)):
            n += 1
        out.append(n)
    return out


_THR = _bucket_thresholds()


def _mm(a, b, precision=None):
    return jnp.dot(a, b, preferred_element_type=F32, precision=precision)


def _mm_nt(a, b, precision=None):
    return lax.dot_general(a, b, (((1,), (1,)), ((), ())), preferred_element_type=F32,
                           precision=precision)


def _mm_tn(a, b, precision=None):
    return lax.dot_general(a, b, (((0,), (0,)), ((), ())), preferred_element_type=F32,
                           precision=precision)


def _split3(x):
    x1 = x.astype(BF16)
    r1 = x - x1.astype(F32)
    x2 = r1.astype(BF16)
    x3 = (r1 - x2.astype(F32)).astype(BF16)
    return x1, x2, x3


def _mm_split_rhs(a_exact, b):
    b1 = b.astype(BF16)
    b2 = (b - b1.astype(F32)).astype(BF16)
    return _mm(a_exact, b1) + _mm(a_exact, b2)


def _iota(shape, dim):
    return lax.broadcasted_iota(jnp.int32, shape, dim)


def _interleave(gens):
    results = [None] * len(gens)
    live = list(enumerate(gens))
    while live:
        still = []
        for i, g in live:
            try:
                next(g)
                still.append((i, g))
            except StopIteration as stop:
                results[i] = stop.value
        live = still
    return results


def _div_pow2(x, n):
    assert n & (n - 1) == 0
    return x >> (n.bit_length() - 1)


def _mod_pow2(x, n):
    assert n & (n - 1) == 0
    return x & (n - 1)


def _inproj_kernel(x_ref, g_ref, w_ref, o_ref, hn_ref):
    @pl.when(pl.program_id(1) == 0)
    def _():
        x = x_ref[...]
        ms = jnp.mean(x * x, axis=-1, keepdims=True)
        hn_ref[...] = (x * lax.rsqrt(ms + NORM_EPS) * g_ref[...]).astype(BF16)

    o_ref[...] = _mm(hn_ref[...], w_ref[...])


def _inproj(x2, g, w, tm, tn):
    m, d = x2.shape
    n = w.shape[1]
    return pl.pallas_call(
        _inproj_kernel,
        grid=(m // tm, n // tn),
        in_specs=[
            pl.BlockSpec((tm, d), lambda i, j: (i, 0)),
            pl.BlockSpec((1, d), lambda i, j: (0, 0)),
            pl.BlockSpec((d, tn), lambda i, j: (0, j)),
        ],
        out_specs=pl.BlockSpec((tm, tn), lambda i, j: (i, j)),
        out_shape=jax.ShapeDtypeStruct((m, n), F32),
        scratch_shapes=[pltpu.VMEM((tm, d), BF16)],
        compiler_params=pltpu.CompilerParams(
            dimension_semantics=("parallel", "arbitrary"),
            vmem_limit_bytes=56 * 1024 * 1024),
        name="inproj",
    )(x2, g, w)


def _bucket(n):
    n = jnp.maximum(n, 0)
    large = jnp.full(n.shape, NUM_BUCKETS // 2, jnp.int32)
    for thr in _THR:
        large = large + (n >= thr).astype(jnp.int32)
    return jnp.where(n < NUM_BUCKETS // 2, n, large)


def _lookup_all_heads(dist, tab_ref):
    bucket = _bucket(dist)
    hits = [bucket == b for b in range(NUM_BUCKETS)]
    outs = []
    for h in range(NSA_HEADS):
        out = jnp.zeros(dist.shape, F32)
        for b in range(NUM_BUCKETS):
            out = jnp.where(hits[b], tab_ref[b * NSA_HEADS + h] * LOG2E, out)
        outs.append(out)
    return outs


def _bias_kernel(tab_ref, bc_ref, tp_ref):
    i = pl.program_id(0)
    rows, nr = bc_ref.shape[1], bc_ref.shape[2]
    dist_c = (i * rows + _iota((rows, nr), 0)) - (_iota((rows, nr), 1) * CMP_STRIDE + (CMP_BLOCK - 1))
    for h, vals in enumerate(_lookup_all_heads(dist_c, tab_ref)):
        bc_ref[h] = vals

    @pl.when(i == 0)
    def _():
        base = _iota((TB, TB), 0) - _iota((TB, TB), 1)
        neg = jnp.full((TB, TB), NEG, F32)
        for h in range(NSA_HEADS):
            tp_ref[h, T_NONE] = neg
        for d in range(ND):
            for h, vals in enumerate(_lookup_all_heads(base + d * TB, tab_ref)):
                tp_ref[h, d] = vals
                if d == 0:
                    tp_ref[h, T_DIAG] = jnp.where(base >= 0, vals, neg)
                if d == WINDOW // TB:
                    tp_ref[h, T_WEND] = jnp.where(base < 0, vals, neg)


def _bias(table_flat, s):
    nr = s // CMP_STRIDE
    rows = min(256, s)
    return pl.pallas_call(
        _bias_kernel,
        grid=(s // rows,),
        in_specs=[pl.BlockSpec(memory_space=pltpu.SMEM)],
        out_specs=[
            pl.BlockSpec((NSA_HEADS, rows, nr), lambda i: (0, i, 0)),
            pl.BlockSpec((NSA_HEADS, NT_ALL, TB, TB), lambda i: (0, 0, 0, 0)),
        ],
        out_shape=[
            jax.ShapeDtypeStruct((NSA_HEADS, s, nr), F32),
            jax.ShapeDtypeStruct((NSA_HEADS, NT_ALL, TB, TB), F32),
        ],
        compiler_params=pltpu.CompilerParams(dimension_semantics=("arbitrary",)),
        name="bias",
    )(table_flat)


def _compress(kv_ref, pos_ref, w1_ref, w2_ref, nr):
    half = CMP_STRIDE * HEAD_DIM
    r = jnp.concatenate(
        [kv_ref[0, pl.ds(m, nr, stride=CMP_STRIDE), :] for m in range(CMP_STRIDE)], axis=1)
    a = _mm((r + pos_ref[0:1, :]).astype(BF16), w1_ref[0:half, :])
    b = _mm((r + pos_ref[1:2, :]).astype(BF16), w1_ref[half:2 * half, :])
    pre = a + pltpu.roll(b, nr - 1, 0)
    h1 = pre * jax.nn.sigmoid(pre)
    return _mm(h1.astype(BF16), w2_ref[...])


def _bias_tile(toep_ref, d_tiles, window, valid=True):
    sub = TQ // TB
    rows = []
    for h in range(NSA_GROUP):
        for ri in range(sub):
            cols = []
            for ci in range(sub):
                d = d_tiles + ri - ci
                idx = jnp.where(d == 0, T_DIAG, jnp.minimum(d, ND - 1))
                if window:
                    idx = jnp.where(d == WINDOW // TB, T_WEND, jnp.where(d > WINDOW // TB, T_NONE, idx))
                idx = jnp.where((d < 0) | jnp.logical_not(valid), T_NONE, idx)
                cols.append(toep_ref[h, idx])
            rows.append(jnp.concatenate(cols, axis=1))
    return jnp.concatenate(rows, axis=0)


def _add_shared(s, mask_add):
    n = s.shape[-1]
    return (s.reshape(NSA_GROUP, TQ, n) + mask_add[None]).reshape(NSA_GROUP * TQ, n)


def _fold_lanes(x, op):
    out = x[:, :LANE]
    for c in range(1, x.shape[1] // LANE):
        out = op(out, x[:, c * LANE:(c + 1) * LANE])
    return out


def _nsa_kernel(q_ref, kc_ref, vc_ref, ks_ref, vs_ref, kw_ref, vw_ref, gate_ref, za_ref,
                bc_ref, toep_ref, posk_ref, w1k_ref, w2k_ref, posv_ref, w1v_ref, w2v_ref,
                o_ref, kcs_ref, vcs_ref, s_ref):
    qi = pl.program_id(2)
    s_len = kc_ref.shape[1]
    nr = s_len // CMP_STRIDE
    nb = s_len // SLC_BLOCK
    n_sel = min(SLC_TOP_N, nb)

    @pl.when(qi == 0)
    def _():
        kcs_ref[...] = _compress(kc_ref, posk_ref, w1k_ref, w2k_ref, nr).astype(BF16)
        vcs_ref[...] = _compress(vc_ref, posv_ref, w1v_ref, w2v_ref, nr).astype(BF16)

    q0 = qi * TQ
    rows = NSA_GROUP * TQ
    q = q_ref[0] * (HEAD_DIM ** -0.5 * LOG2E)
    q4 = jnp.concatenate([q[:, h * HEAD_DIM:(h + 1) * HEAD_DIM] for h in range(NSA_GROUP)],
                         axis=0).astype(BF16)

    sub = TQ // TB
    n_tiles = s_len // TQ

    def compressed_and_selection():
        t_c = q0 + _iota((TQ, nr), 0)
        i_c = _iota((TQ, nr), 1)
        mask_c = (t_c - (i_c * CMP_STRIDE + (CMP_BLOCK - 1)) >= 0) & (i_c < nr - 1)
        qk = _mm_nt(q4, kcs_ref[...])
        yield
        lg = _add_shared(qk + bc_ref[...].reshape(rows, nr), jnp.where(mask_c, 0.0, NEG))
        keep = lg > 0.5 * NEG
        e = jnp.where(keep, jnp.exp2(lg - jnp.max(lg, axis=-1, keepdims=True)), 0.0)
        p = e / jnp.maximum(jnp.sum(e, axis=-1, keepdims=True), 1e-30)
        o_c = _mm(p.astype(BF16), vcs_ref[...])
        psum = p[:TQ]
        for h in range(1, NSA_GROUP):
            psum = psum + p[h * TQ:(h + 1) * TQ]
        ov_i = _iota((nb, nr), 1) * CMP_STRIDE
        ov_j = _iota((nb, nr), 0) * SLC_BLOCK
        ov_t = ((ov_i < ov_j + SLC_BLOCK) & (ov_i + CMP_BLOCK > ov_j)).astype(BF16)
        p1, p2, p3 = _split3(psum)
        imp_t = _mm_nt(ov_t, p1) + _mm_nt(ov_t, p2) + _mm_nt(ov_t, p3)
        yield
        jb = _iota((nb, TQ), 0)
        cur = _div_pow2(q0 + _iota((nb, TQ), 1), SLC_BLOCK)
        forced = (jb == 0) | (jb == cur) | (jb == cur - 1)
        causal = jb <= cur
        score = jnp.where(forced, jnp.inf, jnp.where(causal, imp_t, -jnp.inf))
        rank = jnp.zeros((nb, TQ), jnp.int32)
        for jp in range(nb):
            sj = score[jp:jp + 1, :]
            beats = (sj > score) | ((sj == score) & (jb > jp))
            rank = rank + beats.astype(jnp.int32)
        sel_t = ((rank < n_sel) & causal).astype(BF16)
        place = (_iota((nb, LANE), 0) == _iota((nb, LANE), 1)).astype(BF16)
        unsel = _mm_tn(sel_t, place) - (_iota((TQ, LANE), 1) < nb).astype(F32)
        yield
        return o_c, unsel.astype(BF16)

    def window():
        n_band = WINDOW // TQ + 1
        qk, tiles, v_w = [], [], []
        for c in range(n_band):
            j = qi - (n_band - 1) + c
            jc = jnp.maximum(j, 0)
            k0 = pl.multiple_of(jc * TQ, TQ)
            tiles.append((j, jc))
            qk.append(_mm_nt(q4, kw_ref[0, pl.ds(k0, TQ), :].astype(BF16)))
            v_w.append(vw_ref[0, pl.ds(k0, TQ), :].astype(BF16))
        yield
        s_w = [qk[c] + _bias_tile(toep_ref, (qi - tiles[c][1]) * sub, True, tiles[c][0] >= 0)
               for c in range(n_band)]
        m_w = s_w[0]
        for c in range(1, n_band):
            m_w = jnp.maximum(m_w, s_w[c])
        m_w = jnp.max(m_w, axis=-1, keepdims=True)
        l_w = jnp.zeros((rows, TQ), F32)
        acc_w = jnp.zeros((rows, HEAD_DIM), F32)
        for c in range(n_band):
            p_c = jnp.exp2(s_w[c] - m_w)
            l_w = l_w + p_c
            acc_w = acc_w + _mm(p_c.astype(BF16), v_w[c])
        yield
        return acc_w / jnp.maximum(jnp.sum(l_w, axis=-1, keepdims=True), 1e-30)

    (o_c, unsel), o_w = _interleave([compressed_and_selection(), window()])

    q_sel = jnp.concatenate([q4, jnp.concatenate([unsel] * NSA_GROUP, axis=0)], axis=1)

    def key_tile(j):
        k0 = pl.multiple_of(j * TQ, TQ)
        blk = _div_pow2(j * TQ + _iota((TQ, LANE), 0), SLC_BLOCK)
        marks = jnp.where(_iota((TQ, LANE), 1) == blk, BIG, 0.0).astype(BF16)
        return jnp.concatenate([ks_ref[0, pl.ds(k0, TQ), :].astype(BF16), marks], axis=1)

    assert n_tiles % 2 == 0
    n_pairs = _div_pow2(qi + 2, 2)

    def slc_logits(jp, macc):
        pair = (2 * jp, 2 * jp + 1)
        qk = [_mm_nt(q_sel, key_tile(j)) for j in pair]
        for j, qk_j in zip(pair, qk):
            s = qk_j + _bias_tile(toep_ref, (qi - j) * sub, False)
            s_ref[j] = s
            macc = jnp.maximum(macc, _fold_lanes(s, jnp.maximum))
        return macc

    macc = lax.fori_loop(0, n_pairs, slc_logits, jnp.full((rows, LANE), NEG, F32))
    m_s = jnp.max(macc, axis=-1, keepdims=True)

    def slc_values(jp, carry):
        lacc, acc = carry
        for j in (2 * jp, 2 * jp + 1):
            p_j = jnp.exp2(s_ref[j] - m_s)
            vt = vs_ref[0, pl.ds(pl.multiple_of(j * TQ, TQ), TQ), :].astype(BF16)
            lacc = lacc + _fold_lanes(p_j, jnp.add)
            acc = acc + _mm(p_j.astype(BF16), vt)
        return lacc, acc

    lacc, acc = lax.fori_loop(0, n_pairs, slc_values,
                              (jnp.zeros((rows, LANE), F32), jnp.zeros((rows, HEAD_DIM), F32)))
    o_s = acc / jnp.maximum(jnp.sum(lacc, axis=-1, keepdims=True), 1e-30)

    gts = jax.nn.sigmoid(gate_ref[0])
    grp = pl.program_id(1)
    za = za_ref[0]

    def gate(branch, h):
        cols = [gts[:, branch * NSA_HEADS + g * NSA_GROUP + h:branch * NSA_HEADS + g * NSA_GROUP + h + 1]
                for g in range(NSA_KV_HEADS)]
        out = cols[0]
        for g in range(1, NSA_KV_HEADS):
            out = jnp.where(grp == g, cols[g], out)
        return out

    for h in range(NSA_GROUP):
        hr = slice(h * TQ, (h + 1) * TQ)
        o = gate(0, h) * o_c[hr] + gate(1, h) * o_s[hr] + gate(2, h) * o_w[hr]
        z = za[:, h * HEAD_DIM:(h + 1) * HEAD_DIM]
        o_ref[0, :, h * HEAD_DIM:(h + 1) * HEAD_DIM] = (o * (z * jax.nn.sigmoid(z))).astype(o_ref.dtype)


def _nsa(proj3, bias_c, toep, posk, w1k, w2k, posv, w1v, w2v):
    b, s, _ = proj3.shape
    nr = s // CMP_STRIDE
    gq = NSA_GROUP * HEAD_DIM

    def kvspec(idx):
        return pl.BlockSpec((1, s, HEAD_DIM), lambda bi, g, qi, idx=idx: (bi, 0, P_KV // HEAD_DIM + 2 * idx + g))

    def whole(a):
        return pl.BlockSpec(a.shape, lambda bi, g, qi, nd=a.ndim: (0,) * nd)

    in_specs = [
        pl.BlockSpec((1, TQ, gq), lambda bi, g, qi: (bi, qi, P_Q // gq + g)),
        kvspec(0), kvspec(1), kvspec(2), kvspec(3), kvspec(4), kvspec(5),
        pl.BlockSpec((1, TQ, LANE), lambda bi, g, qi: (bi, qi, P_GATE // LANE)),
        pl.BlockSpec((1, TQ, gq), lambda bi, g, qi: (bi, qi, P_ZA // gq + g)),
        pl.BlockSpec((NSA_GROUP, TQ, nr), lambda bi, g, qi: (g, qi, 0)),
        pl.BlockSpec((NSA_GROUP, NT_ALL, TB, TB), lambda bi, g, qi: (g, 0, 0, 0)),
        whole(posk), whole(w1k), whole(w2k), whole(posv), whole(w1v), whole(w2v),
    ]
    return pl.pallas_call(
        _nsa_kernel,
        grid=(b, NSA_KV_HEADS, s // TQ),
        in_specs=in_specs,
        out_specs=pl.BlockSpec((1, TQ, gq), lambda bi, g, qi: (bi, qi, g)),
        out_shape=jax.ShapeDtypeStruct((b, s, NSA_WIDTH), BF16),
        scratch_shapes=[pltpu.VMEM((nr, HEAD_DIM), BF16), pltpu.VMEM((nr, HEAD_DIM), BF16),
                        pltpu.VMEM((s // TQ, NSA_GROUP * TQ, TQ), F32)],
        compiler_params=pltpu.CompilerParams(
            dimension_semantics=("parallel", "parallel", "arbitrary"),
            vmem_limit_bytes=56 * 1024 * 1024),
        name="nsa",
    )(proj3, proj3, proj3, proj3, proj3, proj3, proj3, proj3, proj3,
      bias_c, toep, posk, w1k, w2k, posv, w1v, w2v)


def _shift_mix(ref, prev_ref, mu, sl):
    x = ref[0, :, sl]
    prev = jnp.where(_iota(x.shape, 0) == 0, prev_ref[:, sl], pltpu.roll(x, 1, 0))
    prev_ref[:, sl] = x[x.shape[0] - 1:]
    return x + mu * (prev - x)


def _rwkv_kernel(r_ref, k_ref, v_ref, wa_ref, zb_ref, vec_ref, muwa_ref, w2_ref, a2_ref,
                 o_ref, st_ref, pr_ref, pk_ref, pv_ref, pwa_ref):
    first = pl.program_id(2) == 0
    tb = r_ref.shape[1]
    n_groups = r_ref.shape[2] // GW
    n_chunks = tb // CH

    @pl.when(first)
    def _():
        for ref in (st_ref, pr_ref, pk_ref, pv_ref, pwa_ref):
            ref[...] = jnp.zeros_like(ref)

    wa = _shift_mix(wa_ref, pwa_ref, muwa_ref[...], slice(0, LANE))
    wd_act = jnp.tanh(wa[:, :LORA]).astype(BF16)
    ad = wa[:, LORA:].astype(BF16)

    seg = (_div_pow2(_iota((GW, GW), 0), RWKV_HEAD_DIM) == _div_pow2(_iota((GW, GW), 1), RWKV_HEAD_DIM))
    segf = seg.astype(F32)
    segb = seg.astype(BF16)
    assert CH == RWKV_HEAD_DIM
    lane_s = _mod_pow2(_iota((CH, GW), 1), CH)
    row_t = _iota((CH, GW), 0)
    strict = lane_s < row_t
    incl = lane_s <= row_t
    eye = (lane_s == row_t).astype(F32)
    ti_r, ti_c = _iota((tb, tb), 0), _iota((tb, tb), 1)
    trib = ((ti_c <= ti_r) & (_div_pow2(ti_c, CH) == _div_pow2(ti_r, CH))).astype(BF16)

    def bd(x):
        xb = x.astype(BF16)
        return jnp.where(seg, jnp.concatenate([xb] * HPG, axis=0), jnp.zeros((), BF16))

    def prep(gi):
        lanes = slice(gi * GW, (gi + 1) * GW)
        vec = vec_ref[:, lanes]
        mu_r, mu_k, mu_v = vec[0:1], vec[1:2], vec[2:3]
        w0, a0, k_k, k_a = vec[3:4], vec[4:5], vec[5:6], vec[6:7]
        r = _shift_mix(r_ref, pr_ref, mu_r, lanes)
        k = _shift_mix(k_ref, pk_ref, mu_k, lanes)
        v = _shift_mix(v_ref, pv_ref, mu_v, lanes)
        w_lora = _mm(wd_act, w2_ref[:, lanes])
        a_lora = _mm(ad, a2_ref[:, lanes])
        kk = k * k_k
        kk_ss = _mm((kk * kk).astype(BF16), segb)
        yield
        w = -jax.nn.softplus(-(w0 + w_lora)) - 0.5
        lw = -jnp.exp(w)
        cum = _mm_split_rhs(trib, lw)
        yield
        a_sig = jax.nn.sigmoid(a0 + a_lora)
        kk = kk / jnp.maximum(jnp.sqrt(kk_ss), 1e-12)
        k = k * (1.0 + (a_sig - 1.0) * k_a)
        return dict(r=r, k=k, v=v, a=-kk, b=kk * a_sig, lw=lw, cum=cum, vec=vec, lanes=lanes)

    groups = _interleave([prep(gi) for gi in range(n_groups)])

    def chunk_local(g, c):
        ts = slice(c * CH, (c + 1) * CH)
        rc, kc, vc, ac, bc, lwc, cum = (g[n][ts] for n in ("r", "k", "v", "a", "b", "lw", "cum"))
        tot = cum[CH - 1:CH]
        e_out = jnp.exp(-cum)
        e_end = jnp.exp(tot - cum)
        r_t = rc * jnp.exp(cum)
        a_t = ac * jnp.exp(cum - lwc)
        lhs = jnp.concatenate([a_t, r_t], axis=0).astype(BF16)
        aa = _mm_nt(lhs, jnp.concatenate([bd(bc * e_out), bd(kc * e_out)], axis=0))
        yield
        a_ab = jnp.where(strict, aa[:CH, :GW], 0.0)
        a_ak = jnp.where(strict, aa[:CH, GW:], 0.0)
        a_rb = jnp.where(incl, aa[CH:, :GW], 0.0)
        a_rk = jnp.where(incl, aa[CH:, GW:], 0.0)
        t_inv = eye + a_ab
        mpow = _mm(a_ab.astype(BF16), bd(a_ab))
        av = _mm(a_ak.astype(BF16), bd(vc))
        yield
        for _ in range(int(math.log2(CH)) - 1):
            res = _mm(jnp.concatenate([t_inv, mpow], axis=0).astype(BF16), bd(mpow))
            yield
            t_inv = t_inv + res[:CH]
            mpow = res[CH:]
        wu = _mm(t_inv.astype(BF16), jnp.concatenate([bd(a_t), bd(av)], axis=1))
        yield
        return dict(
            lhs=jnp.concatenate([wu[:, :GW], r_t], axis=0).astype(BF16), u_loc=wu[:, GW:],
            a_r=jnp.concatenate([a_rb, a_rk], axis=1).astype(BF16), bdv=bd(vc), vc=vc,
            bk_end=jnp.concatenate([bc * e_end, kc * e_end], axis=0).astype(BF16), dec=jnp.exp(tot))

    loc = _interleave([chunk_local(g, c) for g in groups for c in range(n_chunks)])

    def chain(gi):
        ys = []
        g_state = st_ref[gi]
        for c in range(n_chunks):
            lc = loc[gi * n_chunks + c]
            x0 = _mm_nt(lc["lhs"], g_state.astype(BF16))
            yield
            u = x0[:CH] + lc["u_loc"]
            y_c = _mm(lc["a_r"], jnp.concatenate([bd(u), lc["bdv"]], axis=0))
            upd = _mm_tn(jnp.concatenate([u, lc["vc"]], axis=0).astype(BF16), lc["bk_end"])
            yield
            ys.append(x0[CH:] + y_c)
            g_state = g_state * lc["dec"] + upd * segf
        st_ref[gi] = g_state
        return jnp.concatenate(ys, axis=0)

    ys = _interleave([chain(gi) for gi in range(n_groups)])

    def finish(g, y):
        vec = g["vec"]
        ln_w, ln_b, r_k = vec[7:8], vec[8:9], vec[9:10]
        inv_n = 1.0 / RWKV_HEAD_DIM
        mean = _mm(y.astype(BF16), segb) * inv_n
        bonus = _mm((g["r"] * g["k"] * r_k).astype(BF16), segb) * g["v"]
        yield
        yc = y - mean
        var = _mm((yc * yc).astype(BF16), segb) * inv_n
        yield
        yn = yc * lax.rsqrt(var + RWKV_GN_EPS) * ln_w + ln_b
        zb = zb_ref[0, :, g["lanes"]]
        o_ref[0, :, g["lanes"]] = ((yn + bonus) * (zb * jax.nn.sigmoid(zb))).astype(o_ref.dtype)

    _interleave([finish(g, y) for g, y in zip(groups, ys)])


def _rwkv(proj3, vecs, mu_wa, w2, a2, tb, gps):
    b, s, _ = proj3.shape
    gw = gps * GW
    ng = RWKV_WIDTH // gw

    def col(off):
        return pl.BlockSpec((1, tb, gw), lambda bi, g, ti, off=off: (bi, ti, off // gw + g))

    in_specs = [
        col(P_RKV), col(P_RKV + RWKV_WIDTH), col(P_RKV + 2 * RWKV_WIDTH),
        pl.BlockSpec((1, tb, LANE), lambda bi, g, ti: (bi, ti, P_WDAD // LANE)),
        col(P_ZB),
        pl.BlockSpec((vecs.shape[0], gw), lambda bi, g, ti: (0, g)),
        pl.BlockSpec((1, LANE), lambda bi, g, ti: (0, 0)),
        pl.BlockSpec((LORA, gw), lambda bi, g, ti: (0, g)),
        pl.BlockSpec((LORA, gw), lambda bi, g, ti: (0, g)),
    ]
    return pl.pallas_call(
        _rwkv_kernel,
        grid=(b, ng, s // tb),
        in_specs=in_specs,
        out_specs=pl.BlockSpec((1, tb, gw), lambda bi, g, ti: (bi, ti, g)),
        out_shape=jax.ShapeDtypeStruct((b, s, RWKV_WIDTH), BF16),
        scratch_shapes=[pltpu.VMEM((gps, GW, GW), F32), pltpu.VMEM((1, gw), F32), pltpu.VMEM((1, gw), F32),
                        pltpu.VMEM((1, gw), F32), pltpu.VMEM((1, LANE), F32)],
        compiler_params=pltpu.CompilerParams(
            dimension_semantics=("parallel", "parallel", "arbitrary")),
        name="rwkv",
    )(proj3, proj3, proj3, proj3, proj3, vecs, mu_wa, w2, a2)


def _outproj_kernel(ma_ref, mb_ref, wa_ref, wb_ref, x_ref, g_ref, o_ref):
    y = _mm(ma_ref[...], wa_ref[...]) + _mm(mb_ref[...], wb_ref[...])
    ms = jnp.mean(y * y, axis=-1, keepdims=True)
    o_ref[...] = x_ref[...] + y * lax.rsqrt(ms + NORM_EPS) * g_ref[...]


def _outproj(mix_a, mix_b, w_a, w_b, x2, g, tm):
    m, d = x2.shape
    ka, kb = mix_a.shape[1], mix_b.shape[1]
    return pl.pallas_call(
        _outproj_kernel,
        grid=(m // tm,),
        in_specs=[
            pl.BlockSpec((tm, ka), lambda i: (i, 0)),
            pl.BlockSpec((tm, kb), lambda i: (i, 0)),
            pl.BlockSpec((ka, d), lambda i: (0, 0)),
            pl.BlockSpec((kb, d), lambda i: (0, 0)),
            pl.BlockSpec((tm, d), lambda i: (i, 0)),
            pl.BlockSpec((1, d), lambda i: (0, 0)),
        ],
        out_specs=pl.BlockSpec((tm, d), lambda i: (i, 0)),
        out_shape=jax.ShapeDtypeStruct((m, d), F32),
        compiler_params=pltpu.CompilerParams(
            dimension_semantics=("parallel",), vmem_limit_bytes=56 * 1024 * 1024),
        name="outproj",
    )(mix_a, mix_b, w_a, w_b, x2, g)


_W_SEGMENTS = (
    (P_Q, R_Q, NSA_WIDTH),
    (P_RKV, R_FEAT, 3 * RWKV_WIDTH),
    (P_ZB, R_ZB, RWKV_WIDTH),
    (P_ZA, R_ZA, NSA_WIDTH),
    (P_KV, R_KV, 6 * NSA_KV_HEADS * HEAD_DIM),
    (P_WDAD, R_FEAT + 3 * RWKV_WIDTH, 2 * LORA),
    (P_GATE, R_GATE, LANE),
)


def _relayout_kernel(w_ref, o_ref):
    for dst, src, width in _W_SEGMENTS:
        o_ref[:, dst:dst + width] = w_ref[:, src:src + width].astype(BF16)
    used = P_GATE + LANE
    o_ref[:, used:] = jnp.zeros((o_ref.shape[0], NP - used), BF16)


def _permute_w_in(w, rows):
    d, n = w.shape
    return pl.pallas_call(
        _relayout_kernel,
        grid=(d // rows,),
        in_specs=[pl.BlockSpec((rows, n), lambda i: (i, 0))],
        out_specs=pl.BlockSpec((rows, NP), lambda i: (i, 0)),
        out_shape=jax.ShapeDtypeStruct((d, NP), BF16),
        compiler_params=pltpu.CompilerParams(dimension_semantics=("parallel",)),
        name="relayout",
    )(w)


def _block(x, pre_norm_g, w_in, rel_bias_table, cmp_pos_k, cmp_pos_v, cmp_k_w1, cmp_k_w2, cmp_v_w1,
           cmp_v_w2, rwkv_mu, rwkv_w0, rwkv_w2, rwkv_a0, rwkv_a2, rwkv_k_k, rwkv_k_a, rwkv_r_k,
           rwkv_ln_w, rwkv_ln_b, w_out, post_norm_g):
    b, s, d = x.shape
    x2 = x.reshape(b * s, d)
    tm = min(1024, b * s)
    proj = _inproj(x2, pre_norm_g.reshape(1, d), _permute_w_in(w_in.astype(BF16), 256), tm, 1024)
    proj3 = proj.reshape(b, s, NP)

    bias_c, toep = _bias(rel_bias_table.reshape(-1), s)
    half = CMP_STRIDE * HEAD_DIM
    mix_a = _nsa(proj3, bias_c, toep,
                 cmp_pos_k.reshape(2, half), cmp_k_w1.astype(BF16), cmp_k_w2.astype(BF16),
                 cmp_pos_v.reshape(2, half), cmp_v_w1.astype(BF16), cmp_v_w2.astype(BF16))

    w3 = 3 * RWKV_WIDTH
    vec_rows = [rwkv_mu[:RWKV_WIDTH], rwkv_mu[RWKV_WIDTH:2 * RWKV_WIDTH], rwkv_mu[2 * RWKV_WIDTH:w3],
                rwkv_w0, rwkv_a0, rwkv_k_k, rwkv_k_a, rwkv_ln_w, rwkv_ln_b, rwkv_r_k.reshape(-1)]
    vecs = jnp.stack(vec_rows + [jnp.zeros_like(rwkv_w0)] * (16 - len(vec_rows)), axis=0)
    mix_b = _rwkv(proj3, vecs, rwkv_mu[w3:].reshape(1, 2 * LORA), rwkv_w2.astype(BF16),
                  rwkv_a2.astype(BF16), min(256, s), RWKV_GROUPS_PER_STEP)

    w_o = w_out.astype(BF16)
    out = _outproj(mix_a.reshape(b * s, NSA_WIDTH), mix_b.reshape(b * s, RWKV_WIDTH),
                   w_o[:NSA_WIDTH], w_o[NSA_WIDTH:], x2, post_norm_g.reshape(1, d), min(512, b * s))
    return out.reshape(b, s, d)


def kernel(x, pre_norm_g, w_in, rel_bias_table, cmp_pos_k, cmp_pos_v, cmp_k_w1, cmp_k_w2, cmp_v_w1,
           cmp_v_w2, rwkv_mu, rwkv_w0, rwkv_w2, rwkv_a0, rwkv_a2, rwkv_k_k, rwkv_k_a, rwkv_r_k,
           rwkv_ln_w, rwkv_ln_b, w_out, post_norm_g):
    h = x
    for l in range(pre_norm_g.shape[0]):
        h = _block(h, pre_norm_g[l], w_in[l], rel_bias_table, cmp_pos_k[l], cmp_pos_v[l], cmp_k_w1[l],
                   cmp_k_w2[l], cmp_v_w1[l], cmp_v_w2[l], rwkv_mu[l], rwkv_w0[l], rwkv_w2[l],
                   rwkv_a0[l], rwkv_a2[l], rwkv_k_k[l], rwkv_k_a[l], rwkv_r_k[l], rwkv_ln_w[l],
                   rwkv_ln_b[l], w_out[l], post_norm_g[l])
    return h
```

```python
import functools
import math

import numpy as np
import jax
import jax.numpy as jnp
from jax import lax
from jax.experimental import pallas as pl
from jax.experimental.pallas import tpu as pltpu

F32 = jnp.float32
BF16 = jnp.bfloat16
HI = lax.Precision.HIGHEST

D_MODEL = 2048
NSA_HEADS = 8
NSA_KV_HEADS = 2
NSA_GROUP = NSA_HEADS // NSA_KV_HEADS
HEAD_DIM = 128
NSA_WIDTH = NSA_HEADS * HEAD_DIM
CMP_BLOCK = 32
CMP_STRIDE = 16
SLC_BLOCK = 64
SLC_TOP_N = 16
WINDOW = 512
RWKV_WIDTH = 1024
RWKV_HEAD_DIM = 64
RWKV_HEADS = RWKV_WIDTH // RWKV_HEAD_DIM
LORA = 64
NUM_BUCKETS = 32
MAX_DISTANCE = 1024
NORM_EPS = 1e-6
RWKV_GN_EPS = 64e-5

R_Q = 0
R_KV = R_Q + NSA_WIDTH
R_GATE = R_KV + 6 * NSA_KV_HEADS * HEAD_DIM
R_ZA = R_GATE + 3 * NSA_HEADS
R_FEAT = R_ZA + NSA_WIDTH
R_ZB = R_FEAT + 3 * RWKV_WIDTH + 2 * LORA
R_END = R_ZB + RWKV_WIDTH

P_Q = 0
P_RKV = 1024
P_ZB = 4096
P_ZA = 5120
P_KV = 6144
P_WDAD = 7680
P_GATE = 7808
NP = 8192

LANE = 128
TQ = 256
TB = 128
ND = 9
T_DIAG = ND
T_WEND = ND + 1
T_NONE = ND + 2
NT_ALL = ND + 3
LOG2E = math.log2(math.e)
BIG = 2.0 ** 100
CH = 64
HPG = 4
GW = HPG * RWKV_HEAD_DIM
RWKV_GROUPS_PER_STEP = 4
NEG = -1e30


def _bucket_thresholds():
    out = []
    for k in range(1, NUM_BUCKETS // 2):
        n = 16
        while n ** 8 < (16 ** 8) * (2 ** (3 * k)):
            n += 1
        out.append(n)
    return out


_THR = _bucket_thresholds()


def _mm(a, b, precision=None):
    return jnp.dot(a, b, preferred_element_type=F32, precision=precision)


def _mm_nt(a, b, precision=None):
    return lax.dot_general(a, b, (((1,), (1,)), ((), ())), preferred_element_type=F32,
                           precision=precision)


def _mm_tn(a, b, precision=None):
    return lax.dot_general(a, b, (((0,), (0,)), ((), ())), preferred_element_type=F32,
                           precision=precision)


def _split3(x):
    x1 = x.astype(BF16)
    r1 = x - x1.astype(F32)
    x2 = r1.astype(BF16)
    x3 = (r1 - x2.astype(F32)).astype(BF16)
    return x1, x2, x3


def _mm_split_rhs(a_exact, b):
    b1 = b.astype(BF16)
    b2 = (b - b1.astype(F32)).astype(BF16)
    return _mm(a_exact, b1) + _mm(a_exact, b2)


def _iota(shape, dim):
    return lax.broadcasted_iota(jnp.int32, shape, dim)


def _interleave(gens):
    results = [None] * len(gens)
    live = list(enumerate(gens))
    while live:
        still = []
        for i, g in live:
            try:
                next(g)
                still.append((i, g))
            except StopIteration as stop:
                results[i] = stop.value
        live = still
    return results


def _div_pow2(x, n):
    assert n & (n - 1) == 0
    return x >> (n.bit_length() - 1)


def _mod_pow2(x, n):
    assert n & (n - 1) == 0
    return x & (n - 1)


def _inproj_kernel(x_ref, g_ref, w_ref, o_ref, hn_ref):
    @pl.when(pl.program_id(1) == 0)
    def _():
        x = x_ref[...]
        ms = jnp.mean(x * x, axis=-1, keepdims=True)
        hn_ref[...] = (x * lax.rsqrt(ms + NORM_EPS) * g_ref[...]).astype(BF16)

    o_ref[...] = _mm(hn_ref[...], w_ref[...])


def _inproj(x2, g, w, tm, tn):
    m, d = x2.shape
    n = w.shape[1]
    return pl.pallas_call(
        _inproj_kernel,
        grid=(m // tm, n // tn),
        in_specs=[
            pl.BlockSpec((tm, d), lambda i, j: (i, 0)),
            pl.BlockSpec((1, d), lambda i, j: (0, 0)),
            pl.BlockSpec((d, tn), lambda i, j: (0, j)),
        ],
        out_specs=pl.BlockSpec((tm, tn), lambda i, j: (i, j)),
        out_shape=jax.ShapeDtypeStruct((m, n), F32),
        scratch_shapes=[pltpu.VMEM((tm, d), BF16)],
        compiler_params=pltpu.CompilerParams(
            dimension_semantics=("parallel", "arbitrary"),
            vmem_limit_bytes=56 * 1024 * 1024),
        name="inproj",
    )(x2, g, w)


def _bucket(n):
    n = jnp.maximum(n, 0)
    large = jnp.full(n.shape, NUM_BUCKETS // 2, jnp.int32)
    for thr in _THR:
        large = large + (n >= thr).astype(jnp.int32)
    return jnp.where(n < NUM_BUCKETS // 2, n, large)


def _lookup_all_heads(dist, tab_ref):
    bucket = _bucket(dist)
    hits = [bucket == b for b in range(NUM_BUCKETS)]
    outs = []
    for h in range(NSA_HEADS):
        out = jnp.zeros(dist.shape, F32)
        for b in range(NUM_BUCKETS):
            out = jnp.where(hits[b], tab_ref[b * NSA_HEADS + h] * LOG2E, out)
        outs.append(out)
    return outs


def _bias_kernel(tab_ref, bc_ref, tp_ref):
    i = pl.program_id(0)
    rows, nr = bc_ref.shape[1], bc_ref.shape[2]
    dist_c = (i * rows + _iota((rows, nr), 0)) - (_iota((rows, nr), 1) * CMP_STRIDE + (CMP_BLOCK - 1))
    for h, vals in enumerate(_lookup_all_heads(dist_c, tab_ref)):
        bc_ref[h] = vals

    @pl.when(i == 0)
    def _():
        base = _iota((TB, TB), 0) - _iota((TB, TB), 1)
        neg = jnp.full((TB, TB), NEG, F32)
        for h in range(NSA_HEADS):
            tp_ref[h, T_NONE] = neg
        for d in range(ND):
            for h, vals in enumerate(_lookup_all_heads(base + d * TB, tab_ref)):
                tp_ref[h, d] = vals
                if d == 0:
                    tp_ref[h, T_DIAG] = jnp.where(base >= 0, vals, neg)
                if d == WINDOW // TB:
                    tp_ref[h, T_WEND] = jnp.where(base < 0, vals, neg)


def _bias(table_flat, s):
    nr = s // CMP_STRIDE
    rows = min(256, s)
    return pl.pallas_call(
        _bias_kernel,
        grid=(s // rows,),
        in_specs=[pl.BlockSpec(memory_space=pltpu.SMEM)],
        out_specs=[
            pl.BlockSpec((NSA_HEADS, rows, nr), lambda i: (0, i, 0)),
            pl.BlockSpec((NSA_HEADS, NT_ALL, TB, TB), lambda i: (0, 0, 0, 0)),
        ],
        out_shape=[
            jax.ShapeDtypeStruct((NSA_HEADS, s, nr), F32),
            jax.ShapeDtypeStruct((NSA_HEADS, NT_ALL, TB, TB), F32),
        ],
        compiler_params=pltpu.CompilerParams(dimension_semantics=("arbitrary",)),
        name="bias",
    )(table_flat)


def _compress(kv_ref, pos_ref, w1_ref, w2_ref, nr):
    half = CMP_STRIDE * HEAD_DIM
    r = jnp.concatenate(
        [kv_ref[0, pl.ds(m, nr, stride=CMP_STRIDE), :] for m in range(CMP_STRIDE)], axis=1)
    a = _mm((r + pos_ref[0:1, :]).astype(BF16), w1_ref[0:half, :])
    b = _mm((r + pos_ref[1:2, :]).astype(BF16), w1_ref[half:2 * half, :])
    pre = a + pltpu.roll(b, nr - 1, 0)
    h1 = pre * jax.nn.sigmoid(pre)
    return _mm(h1.astype(BF16), w2_ref[...])


def _bias_tile(toep_ref, d_tiles, window, valid=True):
    sub = TQ // TB
    rows = []
    for h in range(NSA_GROUP):
        for ri in range(sub):
            cols = []
            for ci in range(sub):
                d = d_tiles + ri - ci
                idx = jnp.where(d == 0, T_DIAG, jnp.minimum(d, ND - 1))
                if window:
                    idx = jnp.where(d == WINDOW // TB, T_WEND, jnp.where(d > WINDOW // TB, T_NONE, idx))
                idx = jnp.where((d < 0) | jnp.logical_not(valid), T_NONE, idx)
                cols.append(toep_ref[h, idx])
            rows.append(jnp.concatenate(cols, axis=1))
    return jnp.concatenate(rows, axis=0)


def _add_shared(s, mask_add):
    n = s.shape[-1]
    return (s.reshape(NSA_GROUP, TQ, n) + mask_add[None]).reshape(NSA_GROUP * TQ, n)


def _with_ones(v):
    return jnp.concatenate([v, jnp.ones(v.shape, v.dtype)], axis=1)


def _fold_lanes(x, op):
    out = x[:, :LANE]
    for c in range(1, x.shape[1] // LANE):
        out = op(out, x[:, c * LANE:(c + 1) * LANE])
    return out


def _nsa_kernel(q_ref, kc_ref, vc_ref, ks_ref, vs_ref, kw_ref, vw_ref, gate_ref, za_ref,
                bc_ref, toep_ref, posk_ref, w1k_ref, w2k_ref, posv_ref, w1v_ref, w2v_ref,
                o_ref, kcs_ref, vcs_ref, s_ref):
    qi = pl.program_id(2)
    s_len = kc_ref.shape[1]
    nr = s_len // CMP_STRIDE
    nb = s_len // SLC_BLOCK
    n_sel = min(SLC_TOP_N, nb)

    @pl.when(qi == 0)
    def _():
        kcs_ref[...] = _compress(kc_ref, posk_ref, w1k_ref, w2k_ref, nr).astype(BF16)
        vcs_ref[...] = _compress(vc_ref, posv_ref, w1v_ref, w2v_ref, nr).astype(BF16)

    q0 = qi * TQ
    rows = NSA_GROUP * TQ
    q = q_ref[0] * (HEAD_DIM ** -0.5 * LOG2E)
    q4 = jnp.concatenate([q[:, h * HEAD_DIM:(h + 1) * HEAD_DIM] for h in range(NSA_GROUP)],
                         axis=0).astype(BF16)

    sub = TQ // TB
    n_tiles = s_len // TQ

    def compressed_and_selection():
        t_c = q0 + _iota((TQ, nr), 0)
        i_c = _iota((TQ, nr), 1)
        mask_c = (t_c - (i_c * CMP_STRIDE + (CMP_BLOCK - 1)) >= 0) & (i_c < nr - 1)
        qk = _mm_nt(q4, kcs_ref[...])
        yield
        lg = _add_shared(qk + bc_ref[...].reshape(rows, nr), jnp.where(mask_c, 0.0, NEG))
        keep = lg > 0.5 * NEG
        m_c = jnp.max(lg, axis=-1, keepdims=True)
        yield
        e = jnp.where(keep, jnp.exp2(lg - m_c), 0.0)
        l_c = jnp.sum(e, axis=-1, keepdims=True)
        yield
        p = e / jnp.maximum(l_c, 1e-30)
        o_c = _mm(p.astype(BF16), vcs_ref[...])
        psum = p[:TQ]
        for h in range(1, NSA_GROUP):
            psum = psum + p[h * TQ:(h + 1) * TQ]
        ov_i = _iota((nb, nr), 1) * CMP_STRIDE
        ov_j = _iota((nb, nr), 0) * SLC_BLOCK
        ov_t = ((ov_i < ov_j + SLC_BLOCK) & (ov_i + CMP_BLOCK > ov_j)).astype(BF16)
        p1, p2, p3 = _split3(psum)
        imp_t = _mm_nt(ov_t, p1) + _mm_nt(ov_t, p2) + _mm_nt(ov_t, p3)
        yield
        jb = _iota((nb, TQ), 0)
        cur = _div_pow2(q0 + _iota((nb, TQ), 1), SLC_BLOCK)
        forced = (jb == 0) | (jb == cur) | (jb == cur - 1)
        causal = jb <= cur
        score = jnp.where(forced, jnp.inf, jnp.where(causal, imp_t, -jnp.inf))
        rank = jnp.zeros((nb, TQ), jnp.int32)
        for jp in range(nb):
            sj = score[jp:jp + 1, :]
            beats = (sj > score) | ((sj == score) & (jb > jp))
            rank = rank + beats.astype(jnp.int32)
        sel_t = ((rank < n_sel) & causal).astype(BF16)
        place = (_iota((nb, LANE), 0) == _iota((nb, LANE), 1)).astype(BF16)
        unsel = _mm_tn(sel_t, place) - (_iota((TQ, LANE), 1) < nb).astype(F32)
        yield
        return o_c, unsel.astype(BF16)

    def window():
        n_band = WINDOW // TQ + 1
        qk, tiles, v_w = [], [], []
        for c in range(n_band):
            j = qi - (n_band - 1) + c
            jc = jnp.maximum(j, 0)
            k0 = pl.multiple_of(jc * TQ, TQ)
            tiles.append((j, jc))
            qk.append(_mm_nt(q4, kw_ref[0, pl.ds(k0, TQ), :].astype(BF16)))
            v_w.append(vw_ref[0, pl.ds(k0, TQ), :].astype(BF16))
        yield
        s_w, m_w = [], None
        for c in range(n_band):
            s_w.append(qk[c] + _bias_tile(toep_ref, (qi - tiles[c][1]) * sub, True, tiles[c][0] >= 0))
            m_w = s_w[c] if c == 0 else jnp.maximum(m_w, s_w[c])
            yield
        m_w = jnp.max(m_w, axis=-1, keepdims=True)
        yield
        acc_w = jnp.zeros((rows, 2 * HEAD_DIM), F32)
        for c in range(n_band):
            acc_w = acc_w + _mm(jnp.exp2(s_w[c] - m_w).astype(BF16), _with_ones(v_w[c]))
            yield
        return acc_w[:, :HEAD_DIM] / jnp.maximum(acc_w[:, HEAD_DIM:], 1e-30)

    def gates():
        gts = jax.nn.sigmoid(gate_ref[0])
        n_out = 3 * NSA_GROUP * LANE
        blk = _div_pow2(_iota((LANE, n_out), 1), LANE)
        src = _div_pow2(blk, NSA_GROUP) * NSA_HEADS + pl.program_id(1) * NSA_GROUP + _mod_pow2(blk, NSA_GROUP)
        pick = (_iota((LANE, n_out), 0) == src).astype(BF16)
        g_hi = gts.astype(BF16)
        g_lo = (gts - g_hi.astype(F32)).astype(BF16)
        rep = _mm(g_hi, pick) + _mm(g_lo, pick)
        yield
        return rep

    (o_c, unsel), o_w, gate_rep = _interleave([compressed_and_selection(), window(), gates()])

    q_sel = jnp.concatenate([q4, jnp.concatenate([unsel] * NSA_GROUP, axis=0)], axis=1)

    def key_tile(j):
        k0 = pl.multiple_of(j * TQ, TQ)
        blk = _div_pow2(j * TQ + _iota((TQ, LANE), 0), SLC_BLOCK)
        marks = jnp.where(_iota((TQ, LANE), 1) == blk, BIG, 0.0).astype(BF16)
        return jnp.concatenate([ks_ref[0, pl.ds(k0, TQ), :].astype(BF16), marks], axis=1)

    assert n_tiles % 2 == 0
    n_pairs = _div_pow2(qi + 2, 2)

    def slc_logits(jp, macc):
        pair = (2 * jp, 2 * jp + 1)
        qk = [_mm_nt(q_sel, key_tile(j)) for j in pair]
        for j, qk_j in zip(pair, qk):
            s = qk_j + _bias_tile(toep_ref, (qi - j) * sub, False)
            s_ref[j] = s
            macc = jnp.maximum(macc, _fold_lanes(s, jnp.maximum))
        return macc

    macc = lax.fori_loop(0, n_pairs, slc_logits, jnp.full((rows, LANE), NEG, F32))
    m_s = jnp.max(macc, axis=-1, keepdims=True)

    def slc_values(jp, acc):
        for j in (2 * jp, 2 * jp + 1):
            vt = vs_ref[0, pl.ds(pl.multiple_of(j * TQ, TQ), TQ), :].astype(BF16)
            acc = acc + _mm(jnp.exp2(s_ref[j] - m_s).astype(BF16), _with_ones(vt))
        return acc

    acc = lax.fori_loop(0, n_pairs, slc_values, jnp.zeros((rows, 2 * HEAD_DIM), F32))
    o_s = acc[:, :HEAD_DIM] / jnp.maximum(acc[:, HEAD_DIM:], 1e-30)

    za = za_ref[0]

    def gate(branch, h):
        k = branch * NSA_GROUP + h
        return gate_rep[:, k * LANE:(k + 1) * LANE]

    assert HEAD_DIM == LANE
    for h in range(NSA_GROUP):
        hr = slice(h * TQ, (h + 1) * TQ)
        o = gate(0, h) * o_c[hr] + gate(1, h) * o_s[hr] + gate(2, h) * o_w[hr]
        z = za[:, h * HEAD_DIM:(h + 1) * HEAD_DIM]
        o_ref[0, :, h * HEAD_DIM:(h + 1) * HEAD_DIM] = (o * (z * jax.nn.sigmoid(z))).astype(o_ref.dtype)


def _nsa(proj3, bias_c, toep, posk, w1k, w2k, posv, w1v, w2v):
    b, s, _ = proj3.shape
    nr = s // CMP_STRIDE
    gq = NSA_GROUP * HEAD_DIM

    def kvspec(idx):
        return pl.BlockSpec((1, s, HEAD_DIM), lambda bi, g, qi, idx=idx: (bi, 0, P_KV // HEAD_DIM + 2 * idx + g))

    def whole(a):
        return pl.BlockSpec(a.shape, lambda bi, g, qi, nd=a.ndim: (0,) * nd)

    in_specs = [
        pl.BlockSpec((1, TQ, gq), lambda bi, g, qi: (bi, qi, P_Q // gq + g)),
        kvspec(0), kvspec(1), kvspec(2), kvspec(3), kvspec(4), kvspec(5),
        pl.BlockSpec((1, TQ, LANE), lambda bi, g, qi: (bi, qi, P_GATE // LANE)),
        pl.BlockSpec((1, TQ, gq), lambda bi, g, qi: (bi, qi, P_ZA // gq + g)),
        pl.BlockSpec((NSA_GROUP, TQ, nr), lambda bi, g, qi: (g, qi, 0)),
        pl.BlockSpec((NSA_GROUP, NT_ALL, TB, TB), lambda bi, g, qi: (g, 0, 0, 0)),
        whole(posk), whole(w1k), whole(w2k), whole(posv), whole(w1v), whole(w2v),
    ]
    return pl.pallas_call(
        _nsa_kernel,
        grid=(b, NSA_KV_HEADS, s // TQ),
        in_specs=in_specs,
        out_specs=pl.BlockSpec((1, TQ, gq), lambda bi, g, qi: (bi, qi, g)),
        out_shape=jax.ShapeDtypeStruct((b, s, NSA_WIDTH), BF16),
        scratch_shapes=[pltpu.VMEM((nr, HEAD_DIM), BF16), pltpu.VMEM((nr, HEAD_DIM), BF16),
                        pltpu.VMEM((s // TQ, NSA_GROUP * TQ, TQ), F32)],
        compiler_params=pltpu.CompilerParams(
            dimension_semantics=("parallel", "parallel", "arbitrary"),
            vmem_limit_bytes=56 * 1024 * 1024),
        name="nsa",
    )(proj3, proj3, proj3, proj3, proj3, proj3, proj3, proj3, proj3,
      bias_c, toep, posk, w1k, w2k, posv, w1v, w2v)


def _shift_mix(ref, prev_ref, mu, sl):
    x = ref[0, :, sl]
    prev = jnp.where(_iota(x.shape, 0) == 0, prev_ref[:, sl], pltpu.roll(x, 1, 0))
    prev_ref[:, sl] = x[x.shape[0] - 1:]
    return x + mu * (prev - x)


def _rwkv_kernel(r_ref, k_ref, v_ref, wa_ref, zb_ref, vec_ref, muwa_ref, w2_ref, a2_ref,
                 o_ref, st_ref, pr_ref, pk_ref, pv_ref, pwa_ref):
    first = pl.program_id(2) == 0
    tb = r_ref.shape[1]
    n_groups = r_ref.shape[2] // GW
    n_chunks = tb // CH

    @pl.when(first)
    def _():
        for ref in (st_ref, pr_ref, pk_ref, pv_ref, pwa_ref):
            ref[...] = jnp.zeros_like(ref)

    wa = _shift_mix(wa_ref, pwa_ref, muwa_ref[...], slice(0, LANE))
    wd_act = jnp.tanh(wa[:, :LORA]).astype(BF16)
    ad = wa[:, LORA:].astype(BF16)

    seg = (_div_pow2(_iota((GW, GW), 0), RWKV_HEAD_DIM) == _div_pow2(_iota((GW, GW), 1), RWKV_HEAD_DIM))
    segf = seg.astype(F32)
    segb = seg.astype(BF16)
    assert CH == RWKV_HEAD_DIM
    lane_s = _mod_pow2(_iota((CH, GW), 1), CH)
    row_t = _iota((CH, GW), 0)
    strict = lane_s < row_t
    incl = lane_s <= row_t
    eye = (lane_s == row_t).astype(F32)
    ti_r, ti_c = _iota((tb, tb), 0), _iota((tb, tb), 1)
    trib = ((ti_c <= ti_r) & (_div_pow2(ti_c, CH) == _div_pow2(ti_r, CH))).astype(BF16)

    def bd(x):
        xb = x.astype(BF16)
        return jnp.where(seg, jnp.concatenate([xb] * HPG, axis=0), jnp.zeros((), BF16))

    def prep(gi):
        lanes = slice(gi * GW, (gi + 1) * GW)
        vec = vec_ref[:, lanes]
        mu_r, mu_k, mu_v = vec[0:1], vec[1:2], vec[2:3]
        w0, a0, k_k, k_a = vec[3:4], vec[4:5], vec[5:6], vec[6:7]
        r = _shift_mix(r_ref, pr_ref, mu_r, lanes)
        k = _shift_mix(k_ref, pk_ref, mu_k, lanes)
        v = _shift_mix(v_ref, pv_ref, mu_v, lanes)
        w_lora = _mm(wd_act, w2_ref[:, lanes])
        a_lora = _mm(ad, a2_ref[:, lanes])
        kk = k * k_k
        kk_ss = _mm((kk * kk).astype(BF16), segb)
        yield
        w = -jax.nn.softplus(-(w0 + w_lora)) - 0.5
        lw = -jnp.exp(w)
        cum = _mm_split_rhs(trib, lw)
        yield
        a_sig = jax.nn.sigmoid(a0 + a_lora)
        kk = kk / jnp.maximum(jnp.sqrt(kk_ss), 1e-12)
        k = k * (1.0 + (a_sig - 1.0) * k_a)
        return dict(r=r, k=k, v=v, a=-kk, b=kk * a_sig, lw=lw, cum=cum, vec=vec, lanes=lanes)

    groups = _interleave([prep(gi) for gi in range(n_groups)])

    def chunk_local(g, c):
        ts = slice(c * CH, (c + 1) * CH)
        rc, kc, vc, ac, bc, lwc, cum = (g[n][ts] for n in ("r", "k", "v", "a", "b", "lw", "cum"))
        tot = cum[CH - 1:CH]
        e_out = jnp.exp(-cum)
        e_end = jnp.exp(tot - cum)
        r_t = rc * jnp.exp(cum)
        a_t = ac * jnp.exp(cum - lwc)
        lhs = jnp.concatenate([a_t, r_t], axis=0).astype(BF16)
        aa = _mm_nt(lhs, jnp.concatenate([bd(bc * e_out), bd(kc * e_out)], axis=0))
        yield
        a_ab = jnp.where(strict, aa[:CH, :GW], 0.0)
        a_ak = jnp.where(strict, aa[:CH, GW:], 0.0)
        a_rb = jnp.where(incl, aa[CH:, :GW], 0.0)
        a_rk = jnp.where(incl, aa[CH:, GW:], 0.0)
        t_inv = eye + a_ab
        mpow = _mm(a_ab.astype(BF16), bd(a_ab))
        av = _mm(a_ak.astype(BF16), bd(vc))
        yield
        for _ in range(int(math.log2(CH)) - 1):
            res = _mm(jnp.concatenate([t_inv, mpow], axis=0).astype(BF16), bd(mpow))
            yield
            t_inv = t_inv + res[:CH]
            mpow = res[CH:]
        wu = _mm(t_inv.astype(BF16), jnp.concatenate([bd(a_t), bd(av)], axis=1))
        yield
        return dict(
            lhs=jnp.concatenate([wu[:, :GW], r_t], axis=0).astype(BF16), u_loc=wu[:, GW:],
            a_r=jnp.concatenate([a_rb, a_rk], axis=1).astype(BF16), bdv=bd(vc), vc=vc,
            bk_end=jnp.concatenate([bc * e_end, kc * e_end], axis=0).astype(BF16), dec=jnp.exp(tot))

    loc = _interleave([chunk_local(g, c) for g in groups for c in range(n_chunks)])

    def chain(gi):
        ys = []
        g_state = st_ref[gi]
        for c in range(n_chunks):
            lc = loc[gi * n_chunks + c]
            x0 = _mm_nt(lc["lhs"], g_state.astype(BF16))
            yield
            u = x0[:CH] + lc["u_loc"]
            y_c = _mm(lc["a_r"], jnp.concatenate([bd(u), lc["bdv"]], axis=0))
            upd = _mm_tn(jnp.concatenate([u, lc["vc"]], axis=0).astype(BF16), lc["bk_end"])
            yield
            ys.append(x0[CH:] + y_c)
            g_state = g_state * lc["dec"] + upd * segf
        st_ref[gi] = g_state
        return jnp.concatenate(ys, axis=0)

    ys = _interleave([chain(gi) for gi in range(n_groups)])

    def finish(g, y):
        vec = g["vec"]
        ln_w, ln_b, r_k = vec[7:8], vec[8:9], vec[9:10]
        inv_n = 1.0 / RWKV_HEAD_DIM
        mean = _mm(y.astype(BF16), segb) * inv_n
        bonus = _mm((g["r"] * g["k"] * r_k).astype(BF16), segb) * g["v"]
        yield
        yc = y - mean
        var = _mm((yc * yc).astype(BF16), segb) * inv_n
        yield
        yn = yc * lax.rsqrt(var + RWKV_GN_EPS) * ln_w + ln_b
        zb = zb_ref[0, :, g["lanes"]]
        o_ref[0, :, g["lanes"]] = ((yn + bonus) * (zb * jax.nn.sigmoid(zb))).astype(o_ref.dtype)

    _interleave([finish(g, y) for g, y in zip(groups, ys)])


def _rwkv(proj3, vecs, mu_wa, w2, a2, tb, gps):
    b, s, _ = proj3.shape
    gw = gps * GW
    ng = RWKV_WIDTH // gw

    def col(off):
        return pl.BlockSpec((1, tb, gw), lambda bi, g, ti, off=off: (bi, ti, off // gw + g))

    in_specs = [
        col(P_RKV), col(P_RKV + RWKV_WIDTH), col(P_RKV + 2 * RWKV_WIDTH),
        pl.BlockSpec((1, tb, LANE), lambda bi, g, ti: (bi, ti, P_WDAD // LANE)),
        col(P_ZB),
        pl.BlockSpec((vecs.shape[0], gw), lambda bi, g, ti: (0, g)),
        pl.BlockSpec((1, LANE), lambda bi, g, ti: (0, 0)),
        pl.BlockSpec((LORA, gw), lambda bi, g, ti: (0, g)),
        pl.BlockSpec((LORA, gw), lambda bi, g, ti: (0, g)),
    ]
    return pl.pallas_call(
        _rwkv_kernel,
        grid=(b, ng, s // tb),
        in_specs=in_specs,
        out_specs=pl.BlockSpec((1, tb, gw), lambda bi, g, ti: (bi, ti, g)),
        out_shape=jax.ShapeDtypeStruct((b, s, RWKV_WIDTH), BF16),
        scratch_shapes=[pltpu.VMEM((gps, GW, GW), F32), pltpu.VMEM((1, gw), F32), pltpu.VMEM((1, gw), F32),
                        pltpu.VMEM((1, gw), F32), pltpu.VMEM((1, LANE), F32)],
        compiler_params=pltpu.CompilerParams(
            dimension_semantics=("parallel", "parallel", "arbitrary")),
        name="rwkv",
    )(proj3, proj3, proj3, proj3, proj3, vecs, mu_wa, w2, a2)


def _outproj_kernel(ma_ref, mb_ref, wa_ref, wb_ref, x_ref, g_ref, o_ref):
    y = _mm(ma_ref[...], wa_ref[...]) + _mm(mb_ref[...], wb_ref[...])
    ms = jnp.mean(y * y, axis=-1, keepdims=True)
    o_ref[...] = x_ref[...] + y * lax.rsqrt(ms + NORM_EPS) * g_ref[...]


def _outproj(mix_a, mix_b, w_a, w_b, x2, g, tm):
    m, d = x2.shape
    ka, kb = mix_a.shape[1], mix_b.shape[1]
    return pl.pallas_call(
        _outproj_kernel,
        grid=(m // tm,),
        in_specs=[
            pl.BlockSpec((tm, ka), lambda i: (i, 0)),
            pl.BlockSpec((tm, kb), lambda i: (i, 0)),
            pl.BlockSpec((ka, d), lambda i: (0, 0)),
            pl.BlockSpec((kb, d), lambda i: (0, 0)),
            pl.BlockSpec((tm, d), lambda i: (i, 0)),
            pl.BlockSpec((1, d), lambda i: (0, 0)),
        ],
        out_specs=pl.BlockSpec((tm, d), lambda i: (i, 0)),
        out_shape=jax.ShapeDtypeStruct((m, d), F32),
        compiler_params=pltpu.CompilerParams(
            dimension_semantics=("parallel",), vmem_limit_bytes=56 * 1024 * 1024),
        name="outproj",
    )(mix_a, mix_b, w_a, w_b, x2, g)


_W_SEGMENTS = (
    (P_Q, R_Q, NSA_WIDTH),
    (P_RKV, R_FEAT, 3 * RWKV_WIDTH),
    (P_ZB, R_ZB, RWKV_WIDTH),
    (P_ZA, R_ZA, NSA_WIDTH),
    (P_KV, R_KV, 6 * NSA_KV_HEADS * HEAD_DIM),
    (P_WDAD, R_FEAT + 3 * RWKV_WIDTH, 2 * LORA),
    (P_GATE, R_GATE, LANE),
)


def _relayout_kernel(w_ref, o_ref):
    for dst, src, width in _W_SEGMENTS:
        o_ref[:, dst:dst + width] = w_ref[:, src:src + width].astype(BF16)
    used = P_GATE + LANE
    o_ref[:, used:] = jnp.zeros((o_ref.shape[0], NP - used), BF16)


def _permute_w_in(w, rows):
    d, n = w.shape
    return pl.pallas_call(
        _relayout_kernel,
        grid=(d // rows,),
        in_specs=[pl.BlockSpec((rows, n), lambda i: (i, 0))],
        out_specs=pl.BlockSpec((rows, NP), lambda i: (i, 0)),
        out_shape=jax.ShapeDtypeStruct((d, NP), BF16),
        compiler_params=pltpu.CompilerParams(dimension_semantics=("parallel",)),
        name="relayout",
    )(w)


def _block(x, pre_norm_g, w_in, rel_bias_table, cmp_pos_k, cmp_pos_v, cmp_k_w1, cmp_k_w2, cmp_v_w1,
           cmp_v_w2, rwkv_mu, rwkv_w0, rwkv_w2, rwkv_a0, rwkv_a2, rwkv_k_k, rwkv_k_a, rwkv_r_k,
           rwkv_ln_w, rwkv_ln_b, w_out, post_norm_g):
    b, s, d = x.shape
    x2 = x.reshape(b * s, d)
    tm = min(1024, b * s)
    proj = _inproj(x2, pre_norm_g.reshape(1, d), _permute_w_in(w_in.astype(BF16), 256), tm, 1024)
    proj3 = proj.reshape(b, s, NP)

    bias_c, toep = _bias(rel_bias_table.reshape(-1), s)
    half = CMP_STRIDE * HEAD_DIM
    mix_a = _nsa(proj3, bias_c, toep,
                 cmp_pos_k.reshape(2, half), cmp_k_w1.astype(BF16), cmp_k_w2.astype(BF16),
                 cmp_pos_v.reshape(2, half), cmp_v_w1.astype(BF16), cmp_v_w2.astype(BF16))

    w3 = 3 * RWKV_WIDTH
    vec_rows = [rwkv_mu[:RWKV_WIDTH], rwkv_mu[RWKV_WIDTH:2 * RWKV_WIDTH], rwkv_mu[2 * RWKV_WIDTH:w3],
                rwkv_w0, rwkv_a0, rwkv_k_k, rwkv_k_a, rwkv_ln_w, rwkv_ln_b, rwkv_r_k.reshape(-1)]
    vecs = jnp.stack(vec_rows + [jnp.zeros_like(rwkv_w0)] * (16 - len(vec_rows)), axis=0)
    mix_b = _rwkv(proj3, vecs, rwkv_mu[w3:].reshape(1, 2 * LORA), rwkv_w2.astype(BF16),
                  rwkv_a2.astype(BF16), min(256, s), RWKV_GROUPS_PER_STEP)

    w_o = w_out.astype(BF16)
    out = _outproj(mix_a.reshape(b * s, NSA_WIDTH), mix_b.reshape(b * s, RWKV_WIDTH),
                   w_o[:NSA_WIDTH], w_o[NSA_WIDTH:], x2, post_norm_g.reshape(1, d), min(512, b * s))
    return out.reshape(b, s, d)


def kernel(x, pre_norm_g, w_in, rel_bias_table, cmp_pos_k, cmp_pos_v, cmp_k_w1, cmp_k_w2, cmp_v_w1,
           cmp_v_w2, rwkv_mu, rwkv_w0, rwkv_w2, rwkv_a0, rwkv_a2, rwkv_k_k, rwkv_k_a, rwkv_r_k,
           rwkv_ln_w, rwkv_ln_b, w_out, post_norm_g):
    h = x
    for l in range(pre_norm_g.shape[0]):
        h = _block(h, pre_norm_g[l], w_in[l], rel_bias_table, cmp_pos_k[l], cmp_pos_v[l], cmp_k_w1[l],
                   cmp_k_w2[l], cmp_v_w1[l], cmp_v_w2[l], rwkv_mu[l], rwkv_w0[l], rwkv_w2[l],
                   rwkv_a0[l], rwkv_a2[l], rwkv_k_k[l], rwkv_k_a[l], rwkv_r_k[l], rwkv_ln_w[l],
                   rwkv_ln_b[l], w_out[l], post_norm_g[l])
    return h
```

```python
import functools
import math

import numpy as np
import jax
import jax.numpy as jnp
from jax import lax
from jax.experimental import pallas as pl
from jax.experimental.pallas import tpu as pltpu

F32 = jnp.float32
BF16 = jnp.bfloat16
HI = lax.Precision.HIGHEST

D_MODEL = 2048
NSA_HEADS = 8
NSA_KV_HEADS = 2
NSA_GROUP = NSA_HEADS // NSA_KV_HEADS
HEAD_DIM = 128
NSA_WIDTH = NSA_HEADS * HEAD_DIM
CMP_BLOCK = 32
CMP_STRIDE = 16
SLC_BLOCK = 64
SLC_TOP_N = 16
WINDOW = 512
RWKV_WIDTH = 1024
RWKV_HEAD_DIM = 64
RWKV_HEADS = RWKV_WIDTH // RWKV_HEAD_DIM
LORA = 64
NUM_BUCKETS = 32
MAX_DISTANCE = 1024
NORM_EPS = 1e-6
RWKV_GN_EPS = 64e-5

R_Q = 0
R_KV = R_Q + NSA_WIDTH
R_GATE = R_KV + 6 * NSA_KV_HEADS * HEAD_DIM
R_ZA = R_GATE + 3 * NSA_HEADS
R_FEAT = R_ZA + NSA_WIDTH
R_ZB = R_FEAT + 3 * RWKV_WIDTH + 2 * LORA
R_END = R_ZB + RWKV_WIDTH

P_Q = 0
P_RKV = 1024
P_ZB = 4096
P_ZA = 5120
P_KV = 6144
P_WDAD = 7680
P_GATE = 7808
NP = 8192

LANE = 128
TQ = 256
TB = 128
ND = 9
T_DIAG = ND
T_WEND = ND + 1
T_NONE = ND + 2
NT_ALL = ND + 3
LOG2E = math.log2(math.e)
BIG = 2.0 ** 100
CH = 64
HPG = 4
GW = HPG * RWKV_HEAD_DIM
RWKV_GROUPS_PER_STEP = 4
NEG = -1e30


def _bucket_thresholds():
    out = []
    for k in range(1, NUM_BUCKETS // 2):
        n = 16
        while n ** 8 < (16 ** 8) * (2 ** (3 * k)):
            n += 1
        out.append(n)
    return out


_THR = _bucket_thresholds()


def _mm(a, b, precision=None):
    return jnp.dot(a, b, preferred_element_type=F32, precision=precision)


def _mm_nt(a, b, precision=None):
    return lax.dot_general(a, b, (((1,), (1,)), ((), ())), preferred_element_type=F32,
                           precision=precision)


def _mm_tn(a, b, precision=None):
    return lax.dot_general(a, b, (((0,), (0,)), ((), ())), preferred_element_type=F32,
                           precision=precision)


def _split3(x):
    x1 = x.astype(BF16)
    r1 = x - x1.astype(F32)
    x2 = r1.astype(BF16)
    x3 = (r1 - x2.astype(F32)).astype(BF16)
    return x1, x2, x3


def _mm_split_rhs(a_exact, b):
    b1 = b.astype(BF16)
    b2 = (b - b1.astype(F32)).astype(BF16)
    return _mm(a_exact, b1) + _mm(a_exact, b2)


def _iota(shape, dim):
    return lax.broadcasted_iota(jnp.int32, shape, dim)


def _interleave(gens):
    results = [None] * len(gens)
    live = list(enumerate(gens))
    while live:
        still = []
        for i, g in live:
            try:
                next(g)
                still.append((i, g))
            except StopIteration as stop:
                results[i] = stop.value
        live = still
    return results


def _div_pow2(x, n):
    assert n & (n - 1) == 0
    return x >> (n.bit_length() - 1)


def _mod_pow2(x, n):
    assert n & (n - 1) == 0
    return x & (n - 1)


def _inproj_kernel(x_ref, g_ref, w_ref, o_ref, hn_ref):
    @pl.when(pl.program_id(1) == 0)
    def _():
        x = x_ref[...]
        ms = jnp.mean(x * x, axis=-1, keepdims=True)
        hn_ref[...] = (x * lax.rsqrt(ms + NORM_EPS) * g_ref[...]).astype(BF16)

    o_ref[...] = _mm(hn_ref[...], w_ref[...])


def _inproj(x2, g, w, tm, tn):
    m, d = x2.shape
    n = w.shape[1]
    return pl.pallas_call(
        _inproj_kernel,
        grid=(m // tm, n // tn),
        in_specs=[
            pl.BlockSpec((tm, d), lambda i, j: (i, 0)),
            pl.BlockSpec((1, d), lambda i, j: (0, 0)),
            pl.BlockSpec((d, tn), lambda i, j: (0, j)),
        ],
        out_specs=pl.BlockSpec((tm, tn), lambda i, j: (i, j)),
        out_shape=jax.ShapeDtypeStruct((m, n), F32),
        scratch_shapes=[pltpu.VMEM((tm, d), BF16)],
        compiler_params=pltpu.CompilerParams(
            dimension_semantics=("parallel", "arbitrary"),
            vmem_limit_bytes=56 * 1024 * 1024),
        name="inproj",
    )(x2, g, w)


def _bucket(n):
    n = jnp.maximum(n, 0)
    large = jnp.full(n.shape, NUM_BUCKETS // 2, jnp.int32)
    for thr in _THR:
        large = large + (n >= thr).astype(jnp.int32)
    return jnp.where(n < NUM_BUCKETS // 2, n, large)


def _lookup_all_heads(dist, tab_ref):
    bucket = _bucket(dist)
    hits = [bucket == b for b in range(NUM_BUCKETS)]
    outs = []
    for h in range(NSA_HEADS):
        out = jnp.zeros(dist.shape, F32)
        for b in range(NUM_BUCKETS):
            out = jnp.where(hits[b], tab_ref[b * NSA_HEADS + h] * LOG2E, out)
        outs.append(out)
    return outs


def _bias_kernel(tab_ref, bc_ref, tp_ref):
    i = pl.program_id(0)
    rows, nr = bc_ref.shape[1], bc_ref.shape[2]
    dist_c = (i * rows + _iota((rows, nr), 0)) - (_iota((rows, nr), 1) * CMP_STRIDE + (CMP_BLOCK - 1))
    for h, vals in enumerate(_lookup_all_heads(dist_c, tab_ref)):
        bc_ref[h] = vals

    @pl.when(i == 0)
    def _():
        base = _iota((TB, TB), 0) - _iota((TB, TB), 1)
        neg = jnp.full((TB, TB), NEG, F32)
        for h in range(NSA_HEADS):
            tp_ref[h, T_NONE] = neg
        for d in range(ND):
            for h, vals in enumerate(_lookup_all_heads(base + d * TB, tab_ref)):
                tp_ref[h, d] = vals
                if d == 0:
                    tp_ref[h, T_DIAG] = jnp.where(base >= 0, vals, neg)
                if d == WINDOW // TB:
                    tp_ref[h, T_WEND] = jnp.where(base < 0, vals, neg)


def _bias(table_flat, s):
    nr = s // CMP_STRIDE
    rows = min(256, s)
    return pl.pallas_call(
        _bias_kernel,
        grid=(s // rows,),
        in_specs=[pl.BlockSpec(memory_space=pltpu.SMEM)],
        out_specs=[
            pl.BlockSpec((NSA_HEADS, rows, nr), lambda i: (0, i, 0)),
            pl.BlockSpec((NSA_HEADS, NT_ALL, TB, TB), lambda i: (0, 0, 0, 0)),
        ],
        out_shape=[
            jax.ShapeDtypeStruct((NSA_HEADS, s, nr), F32),
            jax.ShapeDtypeStruct((NSA_HEADS, NT_ALL, TB, TB), F32),
        ],
        compiler_params=pltpu.CompilerParams(dimension_semantics=("arbitrary",)),
        name="bias",
    )(table_flat)


def _compress(kv_ref, pos_ref, w1_ref, w2_ref, nr):
    half = CMP_STRIDE * HEAD_DIM
    r = jnp.concatenate(
        [kv_ref[0, pl.ds(m, nr, stride=CMP_STRIDE), :] for m in range(CMP_STRIDE)], axis=1)
    a = _mm((r + pos_ref[0:1, :]).astype(BF16), w1_ref[0:half, :])
    b = _mm((r + pos_ref[1:2, :]).astype(BF16), w1_ref[half:2 * half, :])
    pre = a + pltpu.roll(b, nr - 1, 0)
    h1 = pre * jax.nn.sigmoid(pre)
    return _mm(h1.astype(BF16), w2_ref[...])


def _bias_tile(toep_ref, d_tiles, window, valid=True):
    sub = TQ // TB
    rows = []
    for h in range(NSA_GROUP):
        for ri in range(sub):
            cols = []
            for ci in range(sub):
                d = d_tiles + ri - ci
                idx = jnp.where(d == 0, T_DIAG, jnp.minimum(d, ND - 1))
                if window:
                    idx = jnp.where(d == WINDOW // TB, T_WEND, jnp.where(d > WINDOW // TB, T_NONE, idx))
                idx = jnp.where((d < 0) | jnp.logical_not(valid), T_NONE, idx)
                cols.append(toep_ref[h, idx])
            rows.append(jnp.concatenate(cols, axis=1))
    return jnp.concatenate(rows, axis=0)


def _add_shared(s, mask_add):
    n = s.shape[-1]
    return (s.reshape(NSA_GROUP, TQ, n) + mask_add[None]).reshape(NSA_GROUP * TQ, n)


def _with_ones(v):
    return jnp.concatenate([v, jnp.ones(v.shape, v.dtype)], axis=1)


def _fold_lanes(x, op):
    out = x[:, :LANE]
    for c in range(1, x.shape[1] // LANE):
        out = op(out, x[:, c * LANE:(c + 1) * LANE])
    return out


def _nsa_kernel(q_ref, kc_ref, vc_ref, ks_ref, vs_ref, kw_ref, vw_ref, gate_ref, za_ref,
                bc_ref, toep_ref, posk_ref, w1k_ref, w2k_ref, posv_ref, w1v_ref, w2v_ref,
                o_ref, kcs_ref, vcs_ref, s_ref):
    qi = pl.program_id(2)
    s_len = kc_ref.shape[1]
    nr = s_len // CMP_STRIDE
    nb = s_len // SLC_BLOCK
    n_sel = min(SLC_TOP_N, nb)

    @pl.when(qi == 0)
    def _():
        kcs_ref[...] = _compress(kc_ref, posk_ref, w1k_ref, w2k_ref, nr).astype(BF16)
        vcs_ref[...] = _compress(vc_ref, posv_ref, w1v_ref, w2v_ref, nr).astype(BF16)

    q0 = qi * TQ
    rows = NSA_GROUP * TQ
    q = q_ref[0] * (HEAD_DIM ** -0.5 * LOG2E)
    q4 = jnp.concatenate([q[:, h * HEAD_DIM:(h + 1) * HEAD_DIM] for h in range(NSA_GROUP)],
                         axis=0).astype(BF16)

    sub = TQ // TB
    n_tiles = s_len // TQ

    def compressed_and_selection():
        t_c = q0 + _iota((TQ, nr), 0)
        i_c = _iota((TQ, nr), 1)
        mask_c = (t_c - (i_c * CMP_STRIDE + (CMP_BLOCK - 1)) >= 0) & (i_c < nr - 1)
        qk = _mm_nt(q4, kcs_ref[...])
        yield
        lg = _add_shared(qk + bc_ref[...].reshape(rows, nr), jnp.where(mask_c, 0.0, NEG))
        keep = lg > 0.5 * NEG
        m_c = jnp.max(lg, axis=-1, keepdims=True)
        yield
        e = jnp.where(keep, jnp.exp2(lg - m_c), 0.0)
        l_c = jnp.sum(e, axis=-1, keepdims=True)
        yield
        p = e / jnp.maximum(l_c, 1e-30)
        o_c = _mm(p.astype(BF16), vcs_ref[...])
        psum = p[:TQ]
        for h in range(1, NSA_GROUP):
            psum = psum + p[h * TQ:(h + 1) * TQ]
        ov_i = _iota((nb, nr), 1) * CMP_STRIDE
        ov_j = _iota((nb, nr), 0) * SLC_BLOCK
        ov_t = ((ov_i < ov_j + SLC_BLOCK) & (ov_i + CMP_BLOCK > ov_j)).astype(BF16)
        p1, p2, p3 = _split3(psum)
        imp_t = _mm_nt(ov_t, p1) + _mm_nt(ov_t, p2) + _mm_nt(ov_t, p3)
        yield
        jb = _iota((nb, TQ), 0)
        cur = _div_pow2(q0 + _iota((nb, TQ), 1), SLC_BLOCK)
        forced = (jb == 0) | (jb == cur) | (jb == cur - 1)
        causal = jb <= cur
        score = jnp.where(forced, jnp.inf, jnp.where(causal, imp_t, -jnp.inf))
        rank = jnp.zeros((nb, TQ), jnp.int32)
        for jp in range(nb):
            sj = score[jp:jp + 1, :]
            beats = (sj > score) | ((sj == score) & (jb > jp))
            rank = rank + beats.astype(jnp.int32)
        sel_t = ((rank < n_sel) & causal).astype(BF16)
        place = (_iota((nb, LANE), 0) == _iota((nb, LANE), 1)).astype(BF16)
        unsel = _mm_tn(sel_t, place) - (_iota((TQ, LANE), 1) < nb).astype(F32)
        yield
        return o_c, unsel.astype(BF16)

    def window():
        n_band = WINDOW // TQ + 1
        qk, tiles, v_w = [], [], []
        for c in range(n_band):
            j = qi - (n_band - 1) + c
            jc = jnp.maximum(j, 0)
            k0 = pl.multiple_of(jc * TQ, TQ)
            tiles.append((j, jc))
            qk.append(_mm_nt(q4, kw_ref[0, pl.ds(k0, TQ), :].astype(BF16)))
            v_w.append(vw_ref[0, pl.ds(k0, TQ), :].astype(BF16))
        yield
        s_w, m_w = [], None
        for c in range(n_band):
            s_w.append(qk[c] + _bias_tile(toep_ref, (qi - tiles[c][1]) * sub, True, tiles[c][0] >= 0))
            m_w = s_w[c] if c == 0 else jnp.maximum(m_w, s_w[c])
            yield
        m_w = jnp.max(m_w, axis=-1, keepdims=True)
        yield
        acc_w = jnp.zeros((rows, 2 * HEAD_DIM), F32)
        for c in range(n_band):
            acc_w = acc_w + _mm(jnp.exp2(s_w[c] - m_w).astype(BF16), _with_ones(v_w[c]))
            yield
        return acc_w[:, :HEAD_DIM] / jnp.maximum(acc_w[:, HEAD_DIM:], 1e-30)

    def gates():
        gts = jax.nn.sigmoid(gate_ref[0])
        n_out = 3 * NSA_GROUP * LANE
        blk = _div_pow2(_iota((LANE, n_out), 1), LANE)
        src = _div_pow2(blk, NSA_GROUP) * NSA_HEADS + pl.program_id(1) * NSA_GROUP + _mod_pow2(blk, NSA_GROUP)
        pick = (_iota((LANE, n_out), 0) == src).astype(BF16)
        g_hi = gts.astype(BF16)
        g_lo = (gts - g_hi.astype(F32)).astype(BF16)
        rep = _mm(g_hi, pick) + _mm(g_lo, pick)
        yield
        return rep

    (o_c, unsel), o_w, gate_rep = _interleave([compressed_and_selection(), window(), gates()])

    q_sel = jnp.concatenate([q4, jnp.concatenate([unsel] * NSA_GROUP, axis=0)], axis=1)

    def key_tile(j):
        k0 = pl.multiple_of(j * TQ, TQ)
        blk = _div_pow2(j * TQ + _iota((TQ, LANE), 0), SLC_BLOCK)
        marks = jnp.where(_iota((TQ, LANE), 1) == blk, BIG, 0.0).astype(BF16)
        return jnp.concatenate([ks_ref[0, pl.ds(k0, TQ), :].astype(BF16), marks], axis=1)

    assert n_tiles % 2 == 0
    n_pairs = _div_pow2(qi + 2, 2)

    def slc_logits(jp, macc):
        pair = (2 * jp, 2 * jp + 1)
        qk = [_mm_nt(q_sel, key_tile(j)) for j in pair]
        for j, qk_j in zip(pair, qk):
            s = qk_j + _bias_tile(toep_ref, (qi - j) * sub, False)
            s_ref[j] = s
            macc = jnp.maximum(macc, _fold_lanes(s, jnp.maximum))
        return macc

    macc = lax.fori_loop(0, n_pairs, slc_logits, jnp.full((rows, LANE), NEG, F32))
    m_s = jnp.max(macc, axis=-1, keepdims=True)

    def slc_values(jp, acc):
        for j in (2 * jp, 2 * jp + 1):
            vt = vs_ref[0, pl.ds(pl.multiple_of(j * TQ, TQ), TQ), :].astype(BF16)
            acc = acc + _mm(jnp.exp2(s_ref[j] - m_s).astype(BF16), _with_ones(vt))
        return acc

    acc = lax.fori_loop(0, n_pairs, slc_values, jnp.zeros((rows, 2 * HEAD_DIM), F32))
    o_s = acc[:, :HEAD_DIM] / jnp.maximum(acc[:, HEAD_DIM:], 1e-30)

    za = za_ref[0]

    def gate(branch, h):
        k = branch * NSA_GROUP + h
        return gate_rep[:, k * LANE:(k + 1) * LANE]

    assert HEAD_DIM == LANE
    for h in range(NSA_GROUP):
        hr = slice(h * TQ, (h + 1) * TQ)
        o = gate(0, h) * o_c[hr] + gate(1, h) * o_s[hr] + gate(2, h) * o_w[hr]
        z = za[:, h * HEAD_DIM:(h + 1) * HEAD_DIM]
        o_ref[0, :, h * HEAD_DIM:(h + 1) * HEAD_DIM] = (o * (z * jax.nn.sigmoid(z))).astype(o_ref.dtype)


def _nsa(proj3, bias_c, toep, posk, w1k, w2k, posv, w1v, w2v):
    b, s, _ = proj3.shape
    nr = s // CMP_STRIDE
    gq = NSA_GROUP * HEAD_DIM

    def kvspec(idx):
        return pl.BlockSpec((1, s, HEAD_DIM), lambda bi, g, qi, idx=idx: (bi, 0, P_KV // HEAD_DIM + 2 * idx + g))

    def whole(a):
        return pl.BlockSpec(a.shape, lambda bi, g, qi, nd=a.ndim: (0,) * nd)

    in_specs = [
        pl.BlockSpec((1, TQ, gq), lambda bi, g, qi: (bi, qi, P_Q // gq + g)),
        kvspec(0), kvspec(1), kvspec(2), kvspec(3), kvspec(4), kvspec(5),
        pl.BlockSpec((1, TQ, LANE), lambda bi, g, qi: (bi, qi, P_GATE // LANE)),
        pl.BlockSpec((1, TQ, gq), lambda bi, g, qi: (bi, qi, P_ZA // gq + g)),
        pl.BlockSpec((NSA_GROUP, TQ, nr), lambda bi, g, qi: (g, qi, 0)),
        pl.BlockSpec((NSA_GROUP, NT_ALL, TB, TB), lambda bi, g, qi: (g, 0, 0, 0)),
        whole(posk), whole(w1k), whole(w2k), whole(posv), whole(w1v), whole(w2v),
    ]
    return pl.pallas_call(
        _nsa_kernel,
        grid=(b, NSA_KV_HEADS, s // TQ),
        in_specs=in_specs,
        out_specs=pl.BlockSpec((1, TQ, gq), lambda bi, g, qi: (bi, qi, g)),
        out_shape=jax.ShapeDtypeStruct((b, s, NSA_WIDTH), BF16),
        scratch_shapes=[pltpu.VMEM((nr, HEAD_DIM), BF16), pltpu.VMEM((nr, HEAD_DIM), BF16),
                        pltpu.VMEM((s // TQ, NSA_GROUP * TQ, TQ), F32)],
        compiler_params=pltpu.CompilerParams(
            dimension_semantics=("parallel", "parallel", "arbitrary"),
            vmem_limit_bytes=56 * 1024 * 1024),
        name="nsa",
    )(proj3, proj3, proj3, proj3, proj3, proj3, proj3, proj3, proj3,
      bias_c, toep, posk, w1k, w2k, posv, w1v, w2v)


def _shift_mix(ref, prev_ref, mu, sl):
    x = ref[0, :, sl]
    prev = jnp.where(_iota(x.shape, 0) == 0, prev_ref[:, sl], pltpu.roll(x, 1, 0))
    prev_ref[:, sl] = x[x.shape[0] - 1:]
    return x + mu * (prev - x)


def _rwkv_kernel(r_ref, k_ref, v_ref, wa_ref, zb_ref, vec_ref, muwa_ref, w2_ref, a2_ref,
                 o_ref, st_ref, pr_ref, pk_ref, pv_ref, pwa_ref):
    first = pl.program_id(2) == 0
    tb = r_ref.shape[1]
    n_groups = r_ref.shape[2] // GW
    n_chunks = tb // CH

    @pl.when(first)
    def _():
        for ref in (st_ref, pr_ref, pk_ref, pv_ref, pwa_ref):
            ref[...] = jnp.zeros_like(ref)

    wa = _shift_mix(wa_ref, pwa_ref, muwa_ref[...], slice(0, LANE))
    wd_act = jnp.tanh(wa[:, :LORA]).astype(BF16)
    ad = wa[:, LORA:].astype(BF16)

    seg = (_div_pow2(_iota((GW, GW), 0), RWKV_HEAD_DIM) == _div_pow2(_iota((GW, GW), 1), RWKV_HEAD_DIM))
    segf = seg.astype(F32)
    segb = seg.astype(BF16)
    assert CH == RWKV_HEAD_DIM
    lane_s = _mod_pow2(_iota((CH, GW), 1), CH)
    row_t = _iota((CH, GW), 0)
    strict = lane_s < row_t
    incl = lane_s <= row_t
    eye = (lane_s == row_t).astype(F32)
    ti_r, ti_c = _iota((tb, tb), 0), _iota((tb, tb), 1)
    trib = ((ti_c <= ti_r) & (_div_pow2(ti_c, CH) == _div_pow2(ti_r, CH))).astype(BF16)

    def bd(x):
        xb = x.astype(BF16)
        return jnp.where(seg, jnp.concatenate([xb] * HPG, axis=0), jnp.zeros((), BF16))

    def prep(gi):
        lanes = slice(gi * GW, (gi + 1) * GW)
        vec = vec_ref[:, lanes]
        mu_r, mu_k, mu_v = vec[0:1], vec[1:2], vec[2:3]
        w0, a0, k_k, k_a = vec[3:4], vec[4:5], vec[5:6], vec[6:7]
        r = _shift_mix(r_ref, pr_ref, mu_r, lanes)
        k = _shift_mix(k_ref, pk_ref, mu_k, lanes)
        v = _shift_mix(v_ref, pv_ref, mu_v, lanes)
        w_lora = _mm(wd_act, w2_ref[:, lanes])
        a_lora = _mm(ad, a2_ref[:, lanes])
        kk = k * k_k
        kk_ss = _mm((kk * kk).astype(BF16), segb)
        yield
        w = -jax.nn.softplus(-(w0 + w_lora)) - 0.5
        lw = -jnp.exp(w)
        cum = _mm_split_rhs(trib, lw)
        yield
        a_sig = jax.nn.sigmoid(a0 + a_lora)
        kk = kk / jnp.maximum(jnp.sqrt(kk_ss), 1e-12)
        k = k * (1.0 + (a_sig - 1.0) * k_a)
        return dict(r=r, k=k, v=v, a=-kk, b=kk * a_sig, lw=lw, cum=cum, vec=vec, lanes=lanes)

    groups = _interleave([prep(gi) for gi in range(n_groups)])

    def chunk_local(g, c):
        ts = slice(c * CH, (c + 1) * CH)
        rc, kc, vc, ac, bc, lwc, cum = (g[n][ts] for n in ("r", "k", "v", "a", "b", "lw", "cum"))
        tot = cum[CH - 1:CH]
        e_out = jnp.exp(-cum)
        e_end = jnp.exp(tot - cum)
        r_t = rc * jnp.exp(cum)
        a_t = ac * jnp.exp(cum - lwc)
        lhs = jnp.concatenate([a_t, r_t], axis=0).astype(BF16)
        aa = _mm_nt(lhs, jnp.concatenate([bd(bc * e_out), bd(kc * e_out)], axis=0))
        yield
        a_ab = jnp.where(strict, aa[:CH, :GW], 0.0)
        a_ak = jnp.where(strict, aa[:CH, GW:], 0.0)
        a_rb = jnp.where(incl, aa[CH:, :GW], 0.0)
        a_rk = jnp.where(incl, aa[CH:, GW:], 0.0)
        t_inv = eye + a_ab
        mpow = _mm(a_ab.astype(BF16), bd(a_ab))
        av = _mm(a_ak.astype(BF16), bd(vc))
        yield
        for _ in range(int(math.log2(CH)) - 1):
            res = _mm(jnp.concatenate([t_inv, mpow], axis=0).astype(BF16), bd(mpow))
            yield
            t_inv = t_inv + res[:CH]
            mpow = res[CH:]
        wu = _mm(t_inv.astype(BF16), jnp.concatenate([bd(a_t), bd(av)], axis=1))
        yield
        return dict(
            lhs=jnp.concatenate([wu[:, :GW], r_t], axis=0).astype(BF16), u_loc=wu[:, GW:],
            a_r=jnp.concatenate([a_rb, a_rk], axis=1).astype(BF16), bdv=bd(vc), vc=vc,
            bk_end=jnp.concatenate([bc * e_end, kc * e_end], axis=0).astype(BF16), dec=jnp.exp(tot))

    loc = _interleave([chunk_local(g, c) for g in groups for c in range(n_chunks)])

    def chain(gi):
        ys = []
        g_state = st_ref[gi]
        for c in range(n_chunks):
            lc = loc[gi * n_chunks + c]
            x0 = _mm_nt(lc["lhs"], g_state.astype(BF16))
            yield
            u = x0[:CH] + lc["u_loc"]
            y_c = _mm(lc["a_r"], jnp.concatenate([bd(u), lc["bdv"]], axis=0))
            upd = _mm_tn(jnp.concatenate([u, lc["vc"]], axis=0).astype(BF16), lc["bk_end"])
            yield
            ys.append(x0[CH:] + y_c)
            g_state = g_state * lc["dec"] + upd * segf
        st_ref[gi] = g_state
        return jnp.concatenate(ys, axis=0)

    ys = _interleave([chain(gi) for gi in range(n_groups)])

    def finish(g, y):
        vec = g["vec"]
        ln_w, ln_b, r_k = vec[7:8], vec[8:9], vec[9:10]
        inv_n = 1.0 / RWKV_HEAD_DIM
        mean = _mm(y.astype(BF16), segb) * inv_n
        bonus = _mm((g["r"] * g["k"] * r_k).astype(BF16), segb) * g["v"]
        yield
        yc = y - mean
        var = _mm((yc * yc).astype(BF16), segb) * inv_n
        yield
        yn = yc * lax.rsqrt(var + RWKV_GN_EPS) * ln_w + ln_b
        zb = zb_ref[0, :, g["lanes"]]
        o_ref[0, :, g["lanes"]] = ((yn + bonus) * (zb * jax.nn.sigmoid(zb))).astype(o_ref.dtype)

    _interleave([finish(g, y) for g, y in zip(groups, ys)])


def _rwkv(proj3, vecs, mu_wa, w2, a2, tb, gps):
    b, s, _ = proj3.shape
    gw = gps * GW
    ng = RWKV_WIDTH // gw

    def col(off):
        return pl.BlockSpec((1, tb, gw), lambda bi, g, ti, off=off: (bi, ti, off // gw + g))

    in_specs = [
        col(P_RKV), col(P_RKV + RWKV_WIDTH), col(P_RKV + 2 * RWKV_WIDTH),
        pl.BlockSpec((1, tb, LANE), lambda bi, g, ti: (bi, ti, P_WDAD // LANE)),
        col(P_ZB),
        pl.BlockSpec((vecs.shape[0], gw), lambda bi, g, ti: (0, g)),
        pl.BlockSpec((1, LANE), lambda bi, g, ti: (0, 0)),
        pl.BlockSpec((LORA, gw), lambda bi, g, ti: (0, g)),
        pl.BlockSpec((LORA, gw), lambda bi, g, ti: (0, g)),
    ]
    return pl.pallas_call(
        _rwkv_kernel,
        grid=(b, ng, s // tb),
        in_specs=in_specs,
        out_specs=pl.BlockSpec((1, tb, gw), lambda bi, g, ti: (bi, ti, g)),
        out_shape=jax.ShapeDtypeStruct((b, s, RWKV_WIDTH), BF16),
        scratch_shapes=[pltpu.VMEM((gps, GW, GW), F32), pltpu.VMEM((1, gw), F32), pltpu.VMEM((1, gw), F32),
                        pltpu.VMEM((1, gw), F32), pltpu.VMEM((1, LANE), F32)],
        compiler_params=pltpu.CompilerParams(
            dimension_semantics=("parallel", "parallel", "arbitrary")),
        name="rwkv",
    )(proj3, proj3, proj3, proj3, proj3, vecs, mu_wa, w2, a2)


def _outproj_kernel(ma_ref, mb_ref, wa_ref, wb_ref, x_ref, g_ref, o_ref):
    y = _mm(ma_ref[...], wa_ref[...]) + _mm(mb_ref[...], wb_ref[...])
    ms = jnp.mean(y * y, axis=-1, keepdims=True)
    o_ref[...] = x_ref[...] + y * lax.rsqrt(ms + NORM_EPS) * g_ref[...]


def _outproj(mix_a, mix_b, w_a, w_b, x2, g, tm):
    m, d = x2.shape
    ka, kb = mix_a.shape[1], mix_b.shape[1]
    return pl.pallas_call(
        _outproj_kernel,
        grid=(m // tm,),
        in_specs=[
            pl.BlockSpec((tm, ka), lambda i: (i, 0)),
            pl.BlockSpec((tm, kb), lambda i: (i, 0)),
            pl.BlockSpec((ka, d), lambda i: (0, 0)),
            pl.BlockSpec((kb, d), lambda i: (0, 0)),
            pl.BlockSpec((tm, d), lambda i: (i, 0)),
            pl.BlockSpec((1, d), lambda i: (0, 0)),
        ],
        out_specs=pl.BlockSpec((tm, d), lambda i: (i, 0)),
        out_shape=jax.ShapeDtypeStruct((m, d), F32),
        compiler_params=pltpu.CompilerParams(
            dimension_semantics=("parallel",), vmem_limit_bytes=56 * 1024 * 1024),
        name="outproj",
    )(mix_a, mix_b, w_a, w_b, x2, g)


_W_SEGMENTS = (
    (P_Q, R_Q, NSA_WIDTH),
    (P_RKV, R_FEAT, 3 * RWKV_WIDTH),
    (P_ZB, R_ZB, RWKV_WIDTH),
    (P_ZA, R_ZA, NSA_WIDTH),
    (P_KV, R_KV, 6 * NSA_KV_HEADS * HEAD_DIM),
    (P_WDAD, R_FEAT + 3 * RWKV_WIDTH, 2 * LORA),
    (P_GATE, R_GATE, LANE),
)


def _relayout_kernel(w_ref, o_ref):
    for dst, src, width in _W_SEGMENTS:
        o_ref[:, dst:dst + width] = w_ref[0, :, src:src + width].astype(BF16)
    used = P_GATE + LANE
    o_ref[:, used:] = jnp.zeros((o_ref.shape[0], NP - used), BF16)


def _permute_w_in(w_all, layer, rows):
    _, d, n = w_all.shape
    return pl.pallas_call(
        _relayout_kernel,
        grid=(d // rows,),
        in_specs=[pl.BlockSpec((1, rows, n), lambda i: (layer, i, 0))],
        out_specs=pl.BlockSpec((rows, NP), lambda i: (i, 0)),
        out_shape=jax.ShapeDtypeStruct((d, NP), BF16),
        compiler_params=pltpu.CompilerParams(dimension_semantics=("parallel",)),
        name="relayout",
    )(w_all)


def _block(layer, x, pre_norm_g, w_in_all, rel_bias_table, cmp_pos_k, cmp_pos_v, cmp_k_w1, cmp_k_w2, cmp_v_w1,
           cmp_v_w2, rwkv_mu, rwkv_w0, rwkv_w2, rwkv_a0, rwkv_a2, rwkv_k_k, rwkv_k_a, rwkv_r_k,
           rwkv_ln_w, rwkv_ln_b, w_out, post_norm_g):
    b, s, d = x.shape
    x2 = x.reshape(b * s, d)
    tm = min(1024, b * s)
    proj = _inproj(x2, pre_norm_g.reshape(1, d), _permute_w_in(w_in_all, layer, 128), tm, 1024)
    proj3 = proj.reshape(b, s, NP)

    bias_c, toep = _bias(rel_bias_table.reshape(-1), s)
    half = CMP_STRIDE * HEAD_DIM
    mix_a = _nsa(proj3, bias_c, toep,
                 cmp_pos_k.reshape(2, half), cmp_k_w1.astype(BF16), cmp_k_w2.astype(BF16),
                 cmp_pos_v.reshape(2, half), cmp_v_w1.astype(BF16), cmp_v_w2.astype(BF16))

    w3 = 3 * RWKV_WIDTH
    vec_rows = [rwkv_mu[:RWKV_WIDTH], rwkv_mu[RWKV_WIDTH:2 * RWKV_WIDTH], rwkv_mu[2 * RWKV_WIDTH:w3],
                rwkv_w0, rwkv_a0, rwkv_k_k, rwkv_k_a, rwkv_ln_w, rwkv_ln_b, rwkv_r_k.reshape(-1)]
    vecs = jnp.stack(vec_rows + [jnp.zeros_like(rwkv_w0)] * (16 - len(vec_rows)), axis=0)
    mix_b = _rwkv(proj3, vecs, rwkv_mu[w3:].reshape(1, 2 * LORA), rwkv_w2.astype(BF16),
                  rwkv_a2.astype(BF16), min(256, s), RWKV_GROUPS_PER_STEP)

    w_o = w_out.astype(BF16)
    out = _outproj(mix_a.reshape(b * s, NSA_WIDTH), mix_b.reshape(b * s, RWKV_WIDTH),
                   w_o[:NSA_WIDTH], w_o[NSA_WIDTH:], x2, post_norm_g.reshape(1, d), min(512, b * s))
    return out.reshape(b, s, d)


def kernel(x, pre_norm_g, w_in, rel_bias_table, cmp_pos_k, cmp_pos_v, cmp_k_w1, cmp_k_w2, cmp_v_w1,
           cmp_v_w2, rwkv_mu, rwkv_w0, rwkv_w2, rwkv_a0, rwkv_a2, rwkv_k_k, rwkv_k_a, rwkv_r_k,
           rwkv_ln_w, rwkv_ln_b, w_out, post_norm_g):
    h = x
    for l in range(pre_norm_g.shape[0]):
        h = _block(l, h, pre_norm_g[l], w_in, rel_bias_table, cmp_pos_k[l], cmp_pos_v[l], cmp_k_w1[l],
                   cmp_k_w2[l], cmp_v_w1[l], cmp_v_w2[l], rwkv_mu[l], rwkv_w0[l], rwkv_w2[l],
                   rwkv_a0[l], rwkv_a2[l], rwkv_k_k[l], rwkv_k_a[l], rwkv_r_k[l], rwkv_ln_w[l],
                   rwkv_ln_b[l], w_out[l], post_norm_g[l])
    return h
```

```python
import functools
import math

import numpy as np
import jax
import jax.numpy as jnp
from jax import lax
from jax.experimental import pallas as pl
from jax.experimental.pallas import tpu as pltpu

F32 = jnp.float32
BF16 = jnp.bfloat16
HI = lax.Precision.HIGHEST

D_MODEL = 2048
NSA_HEADS = 8
NSA_KV_HEADS = 2
NSA_GROUP = NSA_HEADS // NSA_KV_HEADS
HEAD_DIM = 128
NSA_WIDTH = NSA_HEADS * HEAD_DIM
CMP_BLOCK = 32
CMP_STRIDE = 16
SLC_BLOCK = 64
SLC_TOP_N = 16
WINDOW = 512
RWKV_WIDTH = 1024
RWKV_HEAD_DIM = 64
RWKV_HEADS = RWKV_WIDTH // RWKV_HEAD_DIM
LORA = 64
NUM_BUCKETS = 32
MAX_DISTANCE = 1024
NORM_EPS = 1e-6
RWKV_GN_EPS = 64e-5

R_Q = 0
R_KV = R_Q + NSA_WIDTH
R_GATE = R_KV + 6 * NSA_KV_HEADS * HEAD_DIM
R_ZA = R_GATE + 3 * NSA_HEADS
R_FEAT = R_ZA + NSA_WIDTH
R_ZB = R_FEAT + 3 * RWKV_WIDTH + 2 * LORA
R_END = R_ZB + RWKV_WIDTH

P_Q = 0
P_RKV = 1024
P_ZB = 4096
P_ZA = 5120
P_KV = 6144
P_WDAD = 7680
P_GATE = 7808
NP = 8192

LANE = 128
TQ = 256
TB = 128
ND = 9
T_DIAG = ND
T_WEND = ND + 1
T_NONE = ND + 2
NT_ALL = ND + 3
LOG2E = math.log2(math.e)
BIG = 2.0 ** 100
CH = 64
HPG = 4
GW = HPG * RWKV_HEAD_DIM
RWKV_GROUPS_PER_STEP = 4
NEG = -1e30


def _bucket_thresholds():
    out = []
    for k in range(1, NUM_BUCKETS // 2):
        n = 16
        while n ** 8 < (16 ** 8) * (2 ** (3 * k)):
            n += 1
        out.append(n)
    return out


_THR = _bucket_thresholds()


def _mm(a, b, precision=None):
    return jnp.dot(a, b, preferred_element_type=F32, precision=precision)


def _mm_nt(a, b, precision=None):
    return lax.dot_general(a, b, (((1,), (1,)), ((), ())), preferred_element_type=F32,
                           precision=precision)


def _mm_tn(a, b, precision=None):
    return lax.dot_general(a, b, (((0,), (0,)), ((), ())), preferred_element_type=F32,
                           precision=precision)


def _split3(x):
    x1 = x.astype(BF16)
    r1 = x - x1.astype(F32)
    x2 = r1.astype(BF16)
    x3 = (r1 - x2.astype(F32)).astype(BF16)
    return x1, x2, x3


def _mm_split_rhs(a_exact, b):
    b1 = b.astype(BF16)
    b2 = (b - b1.astype(F32)).astype(BF16)
    return _mm(a_exact, b1) + _mm(a_exact, b2)


def _iota(shape, dim):
    return lax.broadcasted_iota(jnp.int32, shape, dim)


def _interleave(gens):
    results = [None] * len(gens)
    live = list(enumerate(gens))
    while live:
        still = []
        for i, g in live:
            try:
                next(g)
                still.append((i, g))
            except StopIteration as stop:
                results[i] = stop.value
        live = still
    return results


def _div_pow2(x, n):
    assert n & (n - 1) == 0
    return x >> (n.bit_length() - 1)


def _mod_pow2(x, n):
    assert n & (n - 1) == 0
    return x & (n - 1)


def _inproj_kernel(x_ref, g_ref, w_ref, o_ref, hn_ref):
    @pl.when(pl.program_id(1) == 0)
    def _():
        x = x_ref[...]
        ms = jnp.mean(x * x, axis=-1, keepdims=True)
        hn_ref[...] = (x * lax.rsqrt(ms + NORM_EPS) * g_ref[...]).astype(BF16)

    o_ref[...] = _mm(hn_ref[...], w_ref[...])


def _inproj(x2, g, w, tm, tn):
    m, d = x2.shape
    n = w.shape[1]
    return pl.pallas_call(
        _inproj_kernel,
        grid=(m // tm, n // tn),
        in_specs=[
            pl.BlockSpec((tm, d), lambda i, j: (i, 0)),
            pl.BlockSpec((1, d), lambda i, j: (0, 0)),
            pl.BlockSpec((d, tn), lambda i, j: (0, j)),
        ],
        out_specs=pl.BlockSpec((tm, tn), lambda i, j: (i, j)),
        out_shape=jax.ShapeDtypeStruct((m, n), F32),
        scratch_shapes=[pltpu.VMEM((tm, d), BF16)],
        compiler_params=pltpu.CompilerParams(
            dimension_semantics=("parallel", "arbitrary"),
            vmem_limit_bytes=56 * 1024 * 1024),
        name="inproj",
    )(x2, g, w)


def _bucket(n):
    n = jnp.maximum(n, 0)
    large = jnp.full(n.shape, NUM_BUCKETS // 2, jnp.int32)
    for thr in _THR:
        large = large + (n >= thr).astype(jnp.int32)
    return jnp.where(n < NUM_BUCKETS // 2, n, large)


def _lookup_all_heads(dist, tab_ref):
    bucket = _bucket(dist)
    hits = [bucket == b for b in range(NUM_BUCKETS)]
    outs = []
    for h in range(NSA_HEADS):
        out = jnp.zeros(dist.shape, F32)
        for b in range(NUM_BUCKETS):
            out = jnp.where(hits[b], tab_ref[b * NSA_HEADS + h] * LOG2E, out)
        outs.append(out)
    return outs


def _bias_kernel(tab_ref, bc_ref, tp_ref):
    i = pl.program_id(0)
    rows, nr = bc_ref.shape[1], bc_ref.shape[2]
    dist_c = (i * rows + _iota((rows, nr), 0)) - (_iota((rows, nr), 1) * CMP_STRIDE + (CMP_BLOCK - 1))
    for h, vals in enumerate(_lookup_all_heads(dist_c, tab_ref)):
        bc_ref[h] = vals

    @pl.when(i == 0)
    def _():
        base = _iota((TB, TB), 0) - _iota((TB, TB), 1)
        neg = jnp.full((TB, TB), NEG, F32)
        for h in range(NSA_HEADS):
            tp_ref[h, T_NONE] = neg
        for d in range(ND):
            for h, vals in enumerate(_lookup_all_heads(base + d * TB, tab_ref)):
                tp_ref[h, d] = vals
                if d == 0:
                    tp_ref[h, T_DIAG] = jnp.where(base >= 0, vals, neg)
                if d == WINDOW // TB:
                    tp_ref[h, T_WEND] = jnp.where(base < 0, vals, neg)


def _bias(table_flat, s):
    nr = s // CMP_STRIDE
    rows = min(256, s)
    return pl.pallas_call(
        _bias_kernel,
        grid=(s // rows,),
        in_specs=[pl.BlockSpec(memory_space=pltpu.SMEM)],
        out_specs=[
            pl.BlockSpec((NSA_HEADS, rows, nr), lambda i: (0, i, 0)),
            pl.BlockSpec((NSA_HEADS, NT_ALL, TB, TB), lambda i: (0, 0, 0, 0)),
        ],
        out_shape=[
            jax.ShapeDtypeStruct((NSA_HEADS, s, nr), F32),
            jax.ShapeDtypeStruct((NSA_HEADS, NT_ALL, TB, TB), F32),
        ],
        compiler_params=pltpu.CompilerParams(dimension_semantics=("arbitrary",)),
        name="bias",
    )(table_flat)


def _compress(kv_ref, pos_ref, w1_ref, w2_ref, nr):
    half = CMP_STRIDE * HEAD_DIM
    r = jnp.concatenate(
        [kv_ref[0, pl.ds(m, nr, stride=CMP_STRIDE), :] for m in range(CMP_STRIDE)], axis=1)
    a = _mm((r + pos_ref[0:1, :]).astype(BF16), w1_ref[0:half, :])
    b = _mm((r + pos_ref[1:2, :]).astype(BF16), w1_ref[half:2 * half, :])
    pre = a + pltpu.roll(b, nr - 1, 0)
    h1 = pre * jax.nn.sigmoid(pre)
    return _mm(h1.astype(BF16), w2_ref[...])


def _bias_head(toep_ref, h, d_tiles, window, valid=True):
    sub = TQ // TB
    rows = []
    for ri in range(sub):
        cols = []
        for ci in range(sub):
            d = d_tiles + ri - ci
            idx = jnp.where(d == 0, T_DIAG, jnp.minimum(d, ND - 1))
            if window:
                idx = jnp.where(d == WINDOW // TB, T_WEND, jnp.where(d > WINDOW // TB, T_NONE, idx))
            idx = jnp.where((d < 0) | jnp.logical_not(valid), T_NONE, idx)
            cols.append(toep_ref[h, idx])
        rows.append(jnp.concatenate(cols, axis=1))
    return jnp.concatenate(rows, axis=0)


def _bias_tile(toep_ref, d_tiles, window, valid=True):
    return jnp.concatenate([_bias_head(toep_ref, h, d_tiles, window, valid) for h in range(NSA_GROUP)], axis=0)


def _add_shared(s, mask_add):
    n = s.shape[-1]
    return (s.reshape(NSA_GROUP, TQ, n) + mask_add[None]).reshape(NSA_GROUP * TQ, n)


def _with_ones(v):
    return jnp.concatenate([v, jnp.ones(v.shape, v.dtype)], axis=1)


def _fold_lanes(x, op):
    out = x[:, :LANE]
    for c in range(1, x.shape[1] // LANE):
        out = op(out, x[:, c * LANE:(c + 1) * LANE])
    return out


def _nsa_kernel(q_ref, kc_ref, vc_ref, ks_ref, vs_ref, kw_ref, vw_ref, gate_ref, za_ref,
                bc_ref, toep_ref, posk_ref, w1k_ref, w2k_ref, posv_ref, w1v_ref, w2v_ref,
                o_ref, kcs_ref, vcs_ref, s_ref):
    qi = pl.program_id(2)
    s_len = kc_ref.shape[1]
    nr = s_len // CMP_STRIDE
    nb = s_len // SLC_BLOCK
    n_sel = min(SLC_TOP_N, nb)

    @pl.when(qi == 0)
    def _():
        kcs_ref[...] = _compress(kc_ref, posk_ref, w1k_ref, w2k_ref, nr).astype(BF16)
        vcs_ref[...] = _compress(vc_ref, posv_ref, w1v_ref, w2v_ref, nr).astype(BF16)

    q0 = qi * TQ
    rows = NSA_GROUP * TQ
    q = q_ref[0] * (HEAD_DIM ** -0.5 * LOG2E)
    q4 = jnp.concatenate([q[:, h * HEAD_DIM:(h + 1) * HEAD_DIM] for h in range(NSA_GROUP)],
                         axis=0).astype(BF16)

    sub = TQ // TB
    n_tiles = s_len // TQ

    def compressed_and_selection():
        t_c = q0 + _iota((TQ, nr), 0)
        i_c = _iota((TQ, nr), 1)
        mask_c = (t_c - (i_c * CMP_STRIDE + (CMP_BLOCK - 1)) >= 0) & (i_c < nr - 1)
        qk = _mm_nt(q4, kcs_ref[...])
        mask_add = jnp.where(mask_c, 0.0, NEG)
        yield
        p_heads = []
        for h in range(NSA_GROUP):
            lg = qk[h * TQ:(h + 1) * TQ] + bc_ref[h] + mask_add
            m_c = jnp.max(lg, axis=-1, keepdims=True)
            yield
            e = jnp.where(lg > 0.5 * NEG, jnp.exp2(lg - m_c), 0.0)
            l_c = jnp.sum(e, axis=-1, keepdims=True)
            yield
            p_heads.append(e / jnp.maximum(l_c, 1e-30))
        p = jnp.concatenate(p_heads, axis=0)
        o_c = _mm(p.astype(BF16), vcs_ref[...])
        psum = p_heads[0]
        for h in range(1, NSA_GROUP):
            psum = psum + p_heads[h]
        ov_i = _iota((nb, nr), 1) * CMP_STRIDE
        ov_j = _iota((nb, nr), 0) * SLC_BLOCK
        ov_t = ((ov_i < ov_j + SLC_BLOCK) & (ov_i + CMP_BLOCK > ov_j)).astype(BF16)
        p1, p2, p3 = _split3(psum)
        imp_t = _mm_nt(ov_t, p1) + _mm_nt(ov_t, p2) + _mm_nt(ov_t, p3)
        yield
        jb = _iota((nb, TQ), 0)
        cur = _div_pow2(q0 + _iota((nb, TQ), 1), SLC_BLOCK)
        forced = (jb == 0) | (jb == cur) | (jb == cur - 1)
        causal = jb <= cur
        score = jnp.where(forced, jnp.inf, jnp.where(causal, imp_t, -jnp.inf))
        rank = jnp.zeros((nb, TQ), jnp.int32)
        for jp in range(nb):
            sj = score[jp:jp + 1, :]
            beats = (sj > score) | ((sj == score) & (jb > jp))
            rank = rank + beats.astype(jnp.int32)
            if jp % 8 == 7:
                yield
        sel_t = ((rank < n_sel) & causal).astype(BF16)
        place = (_iota((nb, LANE), 0) == _iota((nb, LANE), 1)).astype(BF16)
        unsel = _mm_tn(sel_t, place) - (_iota((TQ, LANE), 1) < nb).astype(F32)
        yield
        return o_c, unsel.astype(BF16)

    def window():
        n_band = WINDOW // TQ + 1
        qk, tiles, v_w = [], [], []
        for c in range(n_band):
            j = qi - (n_band - 1) + c
            jc = jnp.maximum(j, 0)
            k0 = pl.multiple_of(jc * TQ, TQ)
            tiles.append((j, jc))
            qk.append(_mm_nt(q4, kw_ref[0, pl.ds(k0, TQ), :].astype(BF16)))
            v_w.append(_with_ones(vw_ref[0, pl.ds(k0, TQ), :].astype(BF16)))
        yield
        heads = range(NSA_GROUP)
        s_w = [[None] * NSA_GROUP for _ in range(n_band)]
        m_w = [None] * NSA_GROUP
        for c in range(n_band):
            for h in heads:
                s = qk[c][h * TQ:(h + 1) * TQ] + _bias_head(toep_ref, h, (qi - tiles[c][1]) * sub, True,
                                                             tiles[c][0] >= 0)
                s_w[c][h] = s
                m_w[h] = s if c == 0 else jnp.maximum(m_w[h], s)
                yield
        for h in heads:
            m_w[h] = jnp.max(m_w[h], axis=-1, keepdims=True)
        yield
        acc_w = [jnp.zeros((TQ, 2 * HEAD_DIM), F32) for _ in heads]
        for c in range(n_band):
            for h in heads:
                acc_w[h] = acc_w[h] + _mm(jnp.exp2(s_w[c][h] - m_w[h]).astype(BF16), v_w[c])
                yield
        return jnp.concatenate([a[:, :HEAD_DIM] / jnp.maximum(a[:, HEAD_DIM:], 1e-30) for a in acc_w], axis=0)

    def gates():
        gts = jax.nn.sigmoid(gate_ref[0])
        n_out = 3 * NSA_GROUP * LANE
        blk = _div_pow2(_iota((LANE, n_out), 1), LANE)
        src = _div_pow2(blk, NSA_GROUP) * NSA_HEADS + pl.program_id(1) * NSA_GROUP + _mod_pow2(blk, NSA_GROUP)
        pick = (_iota((LANE, n_out), 0) == src).astype(BF16)
        g_hi = gts.astype(BF16)
        g_lo = (gts - g_hi.astype(F32)).astype(BF16)
        rep = _mm(g_hi, pick) + _mm(g_lo, pick)
        yield
        return rep

    (o_c, unsel), o_w, gate_rep = _interleave([compressed_and_selection(), window(), gates()])

    q_sel = jnp.concatenate([q4, jnp.concatenate([unsel] * NSA_GROUP, axis=0)], axis=1)

    def key_tile(j):
        k0 = pl.multiple_of(j * TQ, TQ)
        blk = _div_pow2(j * TQ + _iota((TQ, LANE), 0), SLC_BLOCK)
        marks = jnp.where(_iota((TQ, LANE), 1) == blk, BIG, 0.0).astype(BF16)
        return jnp.concatenate([ks_ref[0, pl.ds(k0, TQ), :].astype(BF16), marks], axis=1)

    assert n_tiles % 2 == 0
    n_pairs = _div_pow2(qi + 2, 2)

    def slc_logits(jp, macc):
        pair = (2 * jp, 2 * jp + 1)
        qk = [_mm_nt(q_sel, key_tile(j)) for j in pair]
        for j, qk_j in zip(pair, qk):
            s = qk_j + _bias_tile(toep_ref, (qi - j) * sub, False)
            s_ref[j] = s
            macc = jnp.maximum(macc, _fold_lanes(s, jnp.maximum))
        return macc

    macc = lax.fori_loop(0, n_pairs, slc_logits, jnp.full((rows, LANE), NEG, F32))
    m_s = jnp.max(macc, axis=-1, keepdims=True)

    def slc_values(jp, acc):
        for j in (2 * jp, 2 * jp + 1):
            vt = vs_ref[0, pl.ds(pl.multiple_of(j * TQ, TQ), TQ), :].astype(BF16)
            acc = acc + _mm(jnp.exp2(s_ref[j] - m_s).astype(BF16), _with_ones(vt))
        return acc

    acc = lax.fori_loop(0, n_pairs, slc_values, jnp.zeros((rows, 2 * HEAD_DIM), F32))
    o_s = acc[:, :HEAD_DIM] / jnp.maximum(acc[:, HEAD_DIM:], 1e-30)

    za = za_ref[0]

    def gate(branch, h):
        k = branch * NSA_GROUP + h
        return gate_rep[:, k * LANE:(k + 1) * LANE]

    assert HEAD_DIM == LANE
    for h in range(NSA_GROUP):
        hr = slice(h * TQ, (h + 1) * TQ)
        o = gate(0, h) * o_c[hr] + gate(1, h) * o_s[hr] + gate(2, h) * o_w[hr]
        z = za[:, h * HEAD_DIM:(h + 1) * HEAD_DIM]
        o_ref[0, :, h * HEAD_DIM:(h + 1) * HEAD_DIM] = (o * (z * jax.nn.sigmoid(z))).astype(o_ref.dtype)


def _nsa(proj3, bias_c, toep, posk, w1k, w2k, posv, w1v, w2v):
    b, s, _ = proj3.shape
    nr = s // CMP_STRIDE
    gq = NSA_GROUP * HEAD_DIM

    def kvspec(idx):
        return pl.BlockSpec((1, s, HEAD_DIM), lambda bi, g, qi, idx=idx: (bi, 0, P_KV // HEAD_DIM + 2 * idx + g))

    def whole(a):
        return pl.BlockSpec(a.shape, lambda bi, g, qi, nd=a.ndim: (0,) * nd)

    in_specs = [
        pl.BlockSpec((1, TQ, gq), lambda bi, g, qi: (bi, qi, P_Q // gq + g)),
        kvspec(0), kvspec(1), kvspec(2), kvspec(3), kvspec(4), kvspec(5),
        pl.BlockSpec((1, TQ, LANE), lambda bi, g, qi: (bi, qi, P_GATE // LANE)),
        pl.BlockSpec((1, TQ, gq), lambda bi, g, qi: (bi, qi, P_ZA // gq + g)),
        pl.BlockSpec((NSA_GROUP, TQ, nr), lambda bi, g, qi: (g, qi, 0)),
        pl.BlockSpec((NSA_GROUP, NT_ALL, TB, TB), lambda bi, g, qi: (g, 0, 0, 0)),
        whole(posk), whole(w1k), whole(w2k), whole(posv), whole(w1v), whole(w2v),
    ]
    return pl.pallas_call(
        _nsa_kernel,
        grid=(b, NSA_KV_HEADS, s // TQ),
        in_specs=in_specs,
        out_specs=pl.BlockSpec((1, TQ, gq), lambda bi, g, qi: (bi, qi, g)),
        out_shape=jax.ShapeDtypeStruct((b, s, NSA_WIDTH), BF16),
        scratch_shapes=[pltpu.VMEM((nr, HEAD_DIM), BF16), pltpu.VMEM((nr, HEAD_DIM), BF16),
                        pltpu.VMEM((s // TQ, NSA_GROUP * TQ, TQ), F32)],
        compiler_params=pltpu.CompilerParams(
            dimension_semantics=("parallel", "parallel", "arbitrary"),
            vmem_limit_bytes=56 * 1024 * 1024),
        name="nsa",
    )(proj3, proj3, proj3, proj3, proj3, proj3, proj3, proj3, proj3,
      bias_c, toep, posk, w1k, w2k, posv, w1v, w2v)


def _shift_mix(ref, prev_ref, mu, sl):
    x = ref[0, :, sl]
    prev = jnp.where(_iota(x.shape, 0) == 0, prev_ref[:, sl], pltpu.roll(x, 1, 0))
    prev_ref[:, sl] = x[x.shape[0] - 1:]
    return x + mu * (prev - x)


def _rwkv_kernel(r_ref, k_ref, v_ref, wa_ref, zb_ref, vec_ref, muwa_ref, w2_ref, a2_ref,
                 o_ref, st_ref, pr_ref, pk_ref, pv_ref, pwa_ref):
    first = pl.program_id(2) == 0
    tb = r_ref.shape[1]
    n_groups = r_ref.shape[2] // GW
    n_chunks = tb // CH

    @pl.when(first)
    def _():
        for ref in (st_ref, pr_ref, pk_ref, pv_ref, pwa_ref):
            ref[...] = jnp.zeros_like(ref)

    wa = _shift_mix(wa_ref, pwa_ref, muwa_ref[...], slice(0, LANE))
    wd_act = jnp.tanh(wa[:, :LORA]).astype(BF16)
    ad = wa[:, LORA:].astype(BF16)

    seg = (_div_pow2(_iota((GW, GW), 0), RWKV_HEAD_DIM) == _div_pow2(_iota((GW, GW), 1), RWKV_HEAD_DIM))
    segf = seg.astype(F32)
    segb = seg.astype(BF16)
    assert CH == RWKV_HEAD_DIM
    lane_s = _mod_pow2(_iota((CH, GW), 1), CH)
    row_t = _iota((CH, GW), 0)
    strict = lane_s < row_t
    incl = lane_s <= row_t
    eye = (lane_s == row_t).astype(F32)
    ti_r, ti_c = _iota((tb, tb), 0), _iota((tb, tb), 1)
    trib = ((ti_c <= ti_r) & (_div_pow2(ti_c, CH) == _div_pow2(ti_r, CH))).astype(BF16)

    def bd(x):
        xb = x.astype(BF16)
        return jnp.where(seg, jnp.concatenate([xb] * HPG, axis=0), jnp.zeros((), BF16))

    def prep(gi):
        lanes = slice(gi * GW, (gi + 1) * GW)
        vec = vec_ref[:, lanes]
        mu_r, mu_k, mu_v = vec[0:1], vec[1:2], vec[2:3]
        w0, a0, k_k, k_a = vec[3:4], vec[4:5], vec[5:6], vec[6:7]
        r = _shift_mix(r_ref, pr_ref, mu_r, lanes)
        k = _shift_mix(k_ref, pk_ref, mu_k, lanes)
        v = _shift_mix(v_ref, pv_ref, mu_v, lanes)
        w_lora = _mm(wd_act, w2_ref[:, lanes])
        a_lora = _mm(ad, a2_ref[:, lanes])
        kk = k * k_k
        kk_ss = _mm((kk * kk).astype(BF16), segb)
        yield
        w = -jax.nn.softplus(-(w0 + w_lora)) - 0.5
        lw = -jnp.exp(w)
        cum = _mm_split_rhs(trib, lw)
        yield
        a_sig = jax.nn.sigmoid(a0 + a_lora)
        kk = kk / jnp.maximum(jnp.sqrt(kk_ss), 1e-12)
        k = k * (1.0 + (a_sig - 1.0) * k_a)
        return dict(r=r, k=k, v=v, a=-kk, b=kk * a_sig, lw=lw, cum=cum, vec=vec, lanes=lanes)

    groups = _interleave([prep(gi) for gi in range(n_groups)])

    def chunk_local(g, c):
        ts = slice(c * CH, (c + 1) * CH)
        rc, kc, vc, ac, bc, lwc, cum = (g[n][ts] for n in ("r", "k", "v", "a", "b", "lw", "cum"))
        tot = cum[CH - 1:CH]
        e_out = jnp.exp(-cum)
        e_end = jnp.exp(tot - cum)
        r_t = rc * jnp.exp(cum)
        a_t = ac * jnp.exp(cum - lwc)
        lhs = jnp.concatenate([a_t, r_t], axis=0).astype(BF16)
        aa = _mm_nt(lhs, jnp.concatenate([bd(bc * e_out), bd(kc * e_out)], axis=0))
        yield
        a_ab = jnp.where(strict, aa[:CH, :GW], 0.0)
        a_ak = jnp.where(strict, aa[:CH, GW:], 0.0)
        a_rb = jnp.where(incl, aa[CH:, :GW], 0.0)
        a_rk = jnp.where(incl, aa[CH:, GW:], 0.0)
        t_inv = eye + a_ab
        mpow = _mm(a_ab.astype(BF16), bd(a_ab))
        av = _mm(a_ak.astype(BF16), bd(vc))
        yield
        for _ in range(int(math.log2(CH)) - 1):
            res = _mm(jnp.concatenate([t_inv, mpow], axis=0).astype(BF16), bd(mpow))
            yield
            t_inv = t_inv + res[:CH]
            mpow = res[CH:]
        wu = _mm(t_inv.astype(BF16), jnp.concatenate([bd(a_t), bd(av)], axis=1))
        yield
        return dict(
            lhs=jnp.concatenate([wu[:, :GW], r_t], axis=0).astype(BF16), u_loc=wu[:, GW:],
            a_r=jnp.concatenate([a_rb, a_rk], axis=1).astype(BF16), bdv=bd(vc), vc=vc,
            bk_end=jnp.concatenate([bc * e_end, kc * e_end], axis=0).astype(BF16), dec=jnp.exp(tot))

    loc = _interleave([chunk_local(g, c) for g in groups for c in range(n_chunks)])

    def chain(gi):
        ys = []
        g_state = st_ref[gi]
        for c in range(n_chunks):
            lc = loc[gi * n_chunks + c]
            x0 = _mm_nt(lc["lhs"], g_state.astype(BF16))
            yield
            u = x0[:CH] + lc["u_loc"]
            y_c = _mm(lc["a_r"], jnp.concatenate([bd(u), lc["bdv"]], axis=0))
            upd = _mm_tn(jnp.concatenate([u, lc["vc"]], axis=0).astype(BF16), lc["bk_end"])
            yield
            ys.append(x0[CH:] + y_c)
            g_state = g_state * lc["dec"] + upd * segf
        st_ref[gi] = g_state
        return jnp.concatenate(ys, axis=0)

    ys = _interleave([chain(gi) for gi in range(n_groups)])

    def finish(g, y):
        vec = g["vec"]
        ln_w, ln_b, r_k = vec[7:8], vec[8:9], vec[9:10]
        inv_n = 1.0 / RWKV_HEAD_DIM
        mean = _mm(y.astype(BF16), segb) * inv_n
        bonus = _mm((g["r"] * g["k"] * r_k).astype(BF16), segb) * g["v"]
        yield
        yc = y - mean
        var = _mm((yc * yc).astype(BF16), segb) * inv_n
        yield
        yn = yc * lax.rsqrt(var + RWKV_GN_EPS) * ln_w + ln_b
        zb = zb_ref[0, :, g["lanes"]]
        o_ref[0, :, g["lanes"]] = ((yn + bonus) * (zb * jax.nn.sigmoid(zb))).astype(o_ref.dtype)

    _interleave([finish(g, y) for g, y in zip(groups, ys)])


def _rwkv(proj3, vecs, mu_wa, w2, a2, tb, gps):
    b, s, _ = proj3.shape
    gw = gps * GW
    ng = RWKV_WIDTH // gw

    def col(off):
        return pl.BlockSpec((1, tb, gw), lambda bi, g, ti, off=off: (bi, ti, off // gw + g))

    in_specs = [
        col(P_RKV), col(P_RKV + RWKV_WIDTH), col(P_RKV + 2 * RWKV_WIDTH),
        pl.BlockSpec((1, tb, LANE), lambda bi, g, ti: (bi, ti, P_WDAD // LANE)),
        col(P_ZB),
        pl.BlockSpec((vecs.shape[0], gw), lambda bi, g, ti: (0, g)),
        pl.BlockSpec((1, LANE), lambda bi, g, ti: (0, 0)),
        pl.BlockSpec((LORA, gw), lambda bi, g, ti: (0, g)),
        pl.BlockSpec((LORA, gw), lambda bi, g, ti: (0, g)),
    ]
    return pl.pallas_call(
        _rwkv_kernel,
        grid=(b, ng, s // tb),
        in_specs=in_specs,
        out_specs=pl.BlockSpec((1, tb, gw), lambda bi, g, ti: (bi, ti, g)),
        out_shape=jax.ShapeDtypeStruct((b, s, RWKV_WIDTH), BF16),
        scratch_shapes=[pltpu.VMEM((gps, GW, GW), F32), pltpu.VMEM((1, gw), F32), pltpu.VMEM((1, gw), F32),
                        pltpu.VMEM((1, gw), F32), pltpu.VMEM((1, LANE), F32)],
        compiler_params=pltpu.CompilerParams(
            dimension_semantics=("parallel", "parallel", "arbitrary")),
        name="rwkv",
    )(proj3, proj3, proj3, proj3, proj3, vecs, mu_wa, w2, a2)


def _outproj_kernel(ma_ref, mb_ref, wa_ref, wb_ref, x_ref, g_ref, o_ref):
    y = _mm(ma_ref[...], wa_ref[...]) + _mm(mb_ref[...], wb_ref[...])
    ms = jnp.mean(y * y, axis=-1, keepdims=True)
    o_ref[...] = x_ref[...] + y * lax.rsqrt(ms + NORM_EPS) * g_ref[...]


def _outproj(mix_a, mix_b, w_a, w_b, x2, g, tm):
    m, d = x2.shape
    ka, kb = mix_a.shape[1], mix_b.shape[1]
    return pl.pallas_call(
        _outproj_kernel,
        grid=(m // tm,),
        in_specs=[
            pl.BlockSpec((tm, ka), lambda i: (i, 0)),
            pl.BlockSpec((tm, kb), lambda i: (i, 0)),
            pl.BlockSpec((ka, d), lambda i: (0, 0)),
            pl.BlockSpec((kb, d), lambda i: (0, 0)),
            pl.BlockSpec((tm, d), lambda i: (i, 0)),
            pl.BlockSpec((1, d), lambda i: (0, 0)),
        ],
        out_specs=pl.BlockSpec((tm, d), lambda i: (i, 0)),
        out_shape=jax.ShapeDtypeStruct((m, d), F32),
        compiler_params=pltpu.CompilerParams(
            dimension_semantics=("parallel",), vmem_limit_bytes=56 * 1024 * 1024),
        name="outproj",
    )(mix_a, mix_b, w_a, w_b, x2, g)


_W_SEGMENTS = (
    (P_Q, R_Q, NSA_WIDTH),
    (P_RKV, R_FEAT, 3 * RWKV_WIDTH),
    (P_ZB, R_ZB, RWKV_WIDTH),
    (P_ZA, R_ZA, NSA_WIDTH),
    (P_KV, R_KV, 6 * NSA_KV_HEADS * HEAD_DIM),
    (P_WDAD, R_FEAT + 3 * RWKV_WIDTH, 2 * LORA),
    (P_GATE, R_GATE, LANE),
)


def _relayout_kernel(w_ref, o_ref):
    for dst, src, width in _W_SEGMENTS:
        o_ref[:, dst:dst + width] = w_ref[0, :, src:src + width].astype(BF16)
    used = P_GATE + LANE
    o_ref[:, used:] = jnp.zeros((o_ref.shape[0], NP - used), BF16)


def _permute_w_in(w_all, layer, rows):
    _, d, n = w_all.shape
    return pl.pallas_call(
        _relayout_kernel,
        grid=(d // rows,),
        in_specs=[pl.BlockSpec((1, rows, n), lambda i: (layer, i, 0))],
        out_specs=pl.BlockSpec((rows, NP), lambda i: (i, 0)),
        out_shape=jax.ShapeDtypeStruct((d, NP), BF16),
        compiler_params=pltpu.CompilerParams(dimension_semantics=("parallel",)),
        name="relayout",
    )(w_all)


def _block(layer, x, pre_norm_g, w_in_all, rel_bias_table, cmp_pos_k, cmp_pos_v, cmp_k_w1, cmp_k_w2, cmp_v_w1,
           cmp_v_w2, rwkv_mu, rwkv_w0, rwkv_w2, rwkv_a0, rwkv_a2, rwkv_k_k, rwkv_k_a, rwkv_r_k,
           rwkv_ln_w, rwkv_ln_b, w_out, post_norm_g):
    b, s, d = x.shape
    x2 = x.reshape(b * s, d)
    tm = min(1024, b * s)
    proj = _inproj(x2, pre_norm_g.reshape(1, d), _permute_w_in(w_in_all[layer:layer + 1].astype(BF16), 0, 256),
                   tm, 1024)
    proj3 = proj.reshape(b, s, NP)

    bias_c, toep = _bias(rel_bias_table.reshape(-1), s)
    half = CMP_STRIDE * HEAD_DIM
    mix_a = _nsa(proj3, bias_c, toep,
                 cmp_pos_k.reshape(2, half), cmp_k_w1.astype(BF16), cmp_k_w2.astype(BF16),
                 cmp_pos_v.reshape(2, half), cmp_v_w1.astype(BF16), cmp_v_w2.astype(BF16))

    w3 = 3 * RWKV_WIDTH
    vec_rows = [rwkv_mu[:RWKV_WIDTH], rwkv_mu[RWKV_WIDTH:2 * RWKV_WIDTH], rwkv_mu[2 * RWKV_WIDTH:w3],
                rwkv_w0, rwkv_a0, rwkv_k_k, rwkv_k_a, rwkv_ln_w, rwkv_ln_b, rwkv_r_k.reshape(-1)]
    vecs = jnp.stack(vec_rows + [jnp.zeros_like(rwkv_w0)] * (16 - len(vec_rows)), axis=0)
    mix_b = _rwkv(proj3, vecs, rwkv_mu[w3:].reshape(1, 2 * LORA), rwkv_w2.astype(BF16),
                  rwkv_a2.astype(BF16), min(256, s), RWKV_GROUPS_PER_STEP)

    w_o = w_out.astype(BF16)
    out = _outproj(mix_a.reshape(b * s, NSA_WIDTH), mix_b.reshape(b * s, RWKV_WIDTH),
                   w_o[:NSA_WIDTH], w_o[NSA_WIDTH:], x2, post_norm_g.reshape(1, d), min(512, b * s))
    return out.reshape(b, s, d)


def kernel(x, pre_norm_g, w_in, rel_bias_table, cmp_pos_k, cmp_pos_v, cmp_k_w1, cmp_k_w2, cmp_v_w1,
           cmp_v_w2, rwkv_mu, rwkv_w0, rwkv_w2, rwkv_a0, rwkv_a2, rwkv_k_k, rwkv_k_a, rwkv_r_k,
           rwkv_ln_w, rwkv_ln_b, w_out, post_norm_g):
    h = x
    for l in range(pre_norm_g.shape[0]):
        h = _block(l, h, pre_norm_g[l], w_in, rel_bias_table, cmp_pos_k[l], cmp_pos_v[l], cmp_k_w1[l],
                   cmp_k_w2[l], cmp_v_w1[l], cmp_v_w2[l], rwkv_mu[l], rwkv_w0[l], rwkv_w2[l],
                   rwkv_a0[l], rwkv_a2[l], rwkv_k_k[l], rwkv_k_a[l], rwkv_r_k[l], rwkv_ln_w[l],
                   rwkv_ln_b[l], w_out[l], post_norm_g[l])
    return h
```

```python
import functools
import math

import numpy as np
import jax
import jax.numpy as jnp
from jax import lax
from jax.experimental import pallas as pl
from jax.experimental.pallas import tpu as pltpu

F32 = jnp.float32
BF16 = jnp.bfloat16
HI = lax.Precision.HIGHEST

D_MODEL = 2048
NSA_HEADS = 8
NSA_KV_HEADS = 2
NSA_GROUP = NSA_HEADS // NSA_KV_HEADS
HEAD_DIM = 128
NSA_WIDTH = NSA_HEADS * HEAD_DIM
CMP_BLOCK = 32
CMP_STRIDE = 16
SLC_BLOCK = 64
SLC_TOP_N = 16
WINDOW = 512
RWKV_WIDTH = 1024
RWKV_HEAD_DIM = 64
RWKV_HEADS = RWKV_WIDTH // RWKV_HEAD_DIM
LORA = 64
NUM_BUCKETS = 32
MAX_DISTANCE = 1024
NORM_EPS = 1e-6
RWKV_GN_EPS = 64e-5

R_Q = 0
R_KV = R_Q + NSA_WIDTH
R_GATE = R_KV + 6 * NSA_KV_HEADS * HEAD_DIM
R_ZA = R_GATE + 3 * NSA_HEADS
R_FEAT = R_ZA + NSA_WIDTH
R_ZB = R_FEAT + 3 * RWKV_WIDTH + 2 * LORA
R_END = R_ZB + RWKV_WIDTH

P_Q = 0
P_RKV = 1024
P_ZB = 4096
P_ZA = 5120
P_KV = 6144
P_WDAD = 7680
P_GATE = 7808
NP = 8192

LANE = 128
TQ = 256
TB = 128
ND = 9
T_DIAG = ND
T_WEND = ND + 1
T_NONE = ND + 2
NT_ALL = ND + 3
LOG2E = math.log2(math.e)
BIG = 2.0 ** 100
CH = 64
HPG = 4
GW = HPG * RWKV_HEAD_DIM
RWKV_GROUPS_PER_STEP = 4
NEG = -1e30


def _bucket_thresholds():
    out = []
    for k in range(1, NUM_BUCKETS // 2):
        n = 16
        while n ** 8 < (16 ** 8) * (2 ** (3 * k)):
            n += 1
        out.append(n)
    return out


_THR = _bucket_thresholds()


def _mm(a, b, precision=None):
    return jnp.dot(a, b, preferred_element_type=F32, precision=precision)


def _mm_nt(a, b, precision=None):
    return lax.dot_general(a, b, (((1,), (1,)), ((), ())), preferred_element_type=F32,
                           precision=precision)


def _mm_tn(a, b, precision=None):
    return lax.dot_general(a, b, (((0,), (0,)), ((), ())), preferred_element_type=F32,
                           precision=precision)


def _split3(x):
    x1 = x.astype(BF16)
    r1 = x - x1.astype(F32)
    x2 = r1.astype(BF16)
    x3 = (r1 - x2.astype(F32)).astype(BF16)
    return x1, x2, x3


def _mm_split_rhs(a_exact, b):
    b1 = b.astype(BF16)
    b2 = (b - b1.astype(F32)).astype(BF16)
    return _mm(a_exact, b1) + _mm(a_exact, b2)


def _iota(shape, dim):
    return lax.broadcasted_iota(jnp.int32, shape, dim)


def _interleave(gens):
    results = [None] * len(gens)
    live = list(enumerate(gens))
    while live:
        still = []
        for i, g in live:
            try:
                next(g)
                still.append((i, g))
            except StopIteration as stop:
                results[i] = stop.value
        live = still
    return results


def _div_pow2(x, n):
    assert n & (n - 1) == 0
    return x >> (n.bit_length() - 1)


def _mod_pow2(x, n):
    assert n & (n - 1) == 0
    return x & (n - 1)


def _inproj_kernel(x_ref, g_ref, w_ref, o_ref, hn_ref):
    @pl.when(pl.program_id(1) == 0)
    def _():
        x = x_ref[...]
        ms = jnp.mean(x * x, axis=-1, keepdims=True)
        hn_ref[...] = (x * lax.rsqrt(ms + NORM_EPS) * g_ref[...]).astype(BF16)

    o_ref[...] = _mm(hn_ref[...], w_ref[...])


def _inproj(x2, g, w, tm, tn):
    m, d = x2.shape
    n = w.shape[1]
    return pl.pallas_call(
        _inproj_kernel,
        grid=(m // tm, n // tn),
        in_specs=[
            pl.BlockSpec((tm, d), lambda i, j: (i, 0)),
            pl.BlockSpec((1, d), lambda i, j: (0, 0)),
            pl.BlockSpec((d, tn), lambda i, j: (0, j)),
        ],
        out_specs=pl.BlockSpec((tm, tn), lambda i, j: (i, j)),
        out_shape=jax.ShapeDtypeStruct((m, n), F32),
        scratch_shapes=[pltpu.VMEM((tm, d), BF16)],
        compiler_params=pltpu.CompilerParams(
            dimension_semantics=("parallel", "arbitrary"),
            vmem_limit_bytes=56 * 1024 * 1024),
        name="inproj",
    )(x2, g, w)


def _bucket(n):
    n = jnp.maximum(n, 0)
    large = jnp.full(n.shape, NUM_BUCKETS // 2, jnp.int32)
    for thr in _THR:
        large = large + (n >= thr).astype(jnp.int32)
    return jnp.where(n < NUM_BUCKETS // 2, n, large)


def _lookup_all_heads(dist, tab_ref):
    bucket = _bucket(dist)
    hits = [bucket == b for b in range(NUM_BUCKETS)]
    outs = []
    for h in range(NSA_HEADS):
        out = jnp.zeros(dist.shape, F32)
        for b in range(NUM_BUCKETS):
            out = jnp.where(hits[b], tab_ref[b * NSA_HEADS + h] * LOG2E, out)
        outs.append(out)
    return outs


def _bias_kernel(tab_ref, bc_ref, tp_ref):
    i = pl.program_id(0)
    rows, nr = bc_ref.shape[1], bc_ref.shape[2]
    dist_c = (i * rows + _iota((rows, nr), 0)) - (_iota((rows, nr), 1) * CMP_STRIDE + (CMP_BLOCK - 1))
    for h, vals in enumerate(_lookup_all_heads(dist_c, tab_ref)):
        bc_ref[h] = vals

    @pl.when(i == 0)
    def _():
        base = _iota((TB, TB), 0) - _iota((TB, TB), 1)
        neg = jnp.full((TB, TB), NEG, F32)
        for h in range(NSA_HEADS):
            tp_ref[h, T_NONE] = neg
        for d in range(ND):
            for h, vals in enumerate(_lookup_all_heads(base + d * TB, tab_ref)):
                tp_ref[h, d] = vals
                if d == 0:
                    tp_ref[h, T_DIAG] = jnp.where(base >= 0, vals, neg)
                if d == WINDOW // TB:
                    tp_ref[h, T_WEND] = jnp.where(base < 0, vals, neg)


def _bias(table_flat, s):
    nr = s // CMP_STRIDE
    rows = min(256, s)
    return pl.pallas_call(
        _bias_kernel,
        grid=(s // rows,),
        in_specs=[pl.BlockSpec(memory_space=pltpu.SMEM)],
        out_specs=[
            pl.BlockSpec((NSA_HEADS, rows, nr), lambda i: (0, i, 0)),
            pl.BlockSpec((NSA_HEADS, NT_ALL, TB, TB), lambda i: (0, 0, 0, 0)),
        ],
        out_shape=[
            jax.ShapeDtypeStruct((NSA_HEADS, s, nr), F32),
            jax.ShapeDtypeStruct((NSA_HEADS, NT_ALL, TB, TB), F32),
        ],
        compiler_params=pltpu.CompilerParams(dimension_semantics=("arbitrary",)),
        name="bias",
    )(table_flat)


def _compress(kv_ref, pos_ref, w1_ref, w2_ref, nr):
    half = CMP_STRIDE * HEAD_DIM
    r = jnp.concatenate(
        [kv_ref[0, pl.ds(m, nr, stride=CMP_STRIDE), :] for m in range(CMP_STRIDE)], axis=1)
    a = _mm((r + pos_ref[0:1, :]).astype(BF16), w1_ref[0:half, :])
    b = _mm((r + pos_ref[1:2, :]).astype(BF16), w1_ref[half:2 * half, :])
    pre = a + pltpu.roll(b, nr - 1, 0)
    h1 = pre * jax.nn.sigmoid(pre)
    return _mm(h1.astype(BF16), w2_ref[...])


def _bias_head(toep_ref, h, d_tiles, window, valid=True):
    sub = TQ // TB
    rows = []
    for ri in range(sub):
        cols = []
        for ci in range(sub):
            d = d_tiles + ri - ci
            idx = jnp.where(d == 0, T_DIAG, jnp.minimum(d, ND - 1))
            if window:
                idx = jnp.where(d == WINDOW // TB, T_WEND, jnp.where(d > WINDOW // TB, T_NONE, idx))
            idx = jnp.where((d < 0) | jnp.logical_not(valid), T_NONE, idx)
            cols.append(toep_ref[h, idx])
        rows.append(jnp.concatenate(cols, axis=1))
    return jnp.concatenate(rows, axis=0)


def _bias_tile(toep_ref, d_tiles, window, valid=True):
    return jnp.concatenate([_bias_head(toep_ref, h, d_tiles, window, valid) for h in range(NSA_GROUP)], axis=0)


def _add_shared(s, mask_add):
    n = s.shape[-1]
    return (s.reshape(NSA_GROUP, TQ, n) + mask_add[None]).reshape(NSA_GROUP * TQ, n)


def _with_ones(v):
    return jnp.concatenate([v, jnp.ones(v.shape, v.dtype)], axis=1)


def _fold_lanes(x, op):
    out = x[:, :LANE]
    for c in range(1, x.shape[1] // LANE):
        out = op(out, x[:, c * LANE:(c + 1) * LANE])
    return out


def _nsa_kernel(q_ref, kc_ref, vc_ref, ks_ref, vs_ref, kw_ref, vw_ref, gate_ref, za_ref,
                bc_ref, toep_ref, posk_ref, w1k_ref, w2k_ref, posv_ref, w1v_ref, w2v_ref,
                o_ref, kcs_ref, vcs_ref, s_ref):
    qi = pl.program_id(2)
    s_len = kc_ref.shape[1]
    nr = s_len // CMP_STRIDE
    nb = s_len // SLC_BLOCK
    n_sel = min(SLC_TOP_N, nb)

    @pl.when(qi == 0)
    def _():
        kcs_ref[...] = _compress(kc_ref, posk_ref, w1k_ref, w2k_ref, nr).astype(BF16)
        vcs_ref[...] = _compress(vc_ref, posv_ref, w1v_ref, w2v_ref, nr).astype(BF16)

    q0 = qi * TQ
    rows = NSA_GROUP * TQ
    q = q_ref[0] * (HEAD_DIM ** -0.5 * LOG2E)
    q4 = jnp.concatenate([q[:, h * HEAD_DIM:(h + 1) * HEAD_DIM] for h in range(NSA_GROUP)],
                         axis=0).astype(BF16)

    sub = TQ // TB
    n_tiles = s_len // TQ

    def compressed_and_selection():
        t_c = q0 + _iota((TQ, nr), 0)
        i_c = _iota((TQ, nr), 1)
        mask_c = (t_c - (i_c * CMP_STRIDE + (CMP_BLOCK - 1)) >= 0) & (i_c < nr - 1)
        qk = _mm_nt(q4, kcs_ref[...])
        mask_add = jnp.where(mask_c, 0.0, NEG)
        yield
        p_heads = []
        for h in range(NSA_GROUP):
            lg = qk[h * TQ:(h + 1) * TQ] + bc_ref[h] + mask_add
            m_c = jnp.max(lg, axis=-1, keepdims=True)
            yield
            e = jnp.where(lg > 0.5 * NEG, jnp.exp2(lg - m_c), 0.0)
            l_c = jnp.sum(e, axis=-1, keepdims=True)
            yield
            p_heads.append(e / jnp.maximum(l_c, 1e-30))
        p = jnp.concatenate(p_heads, axis=0)
        o_c = _mm(p.astype(BF16), vcs_ref[...])
        psum = p_heads[0]
        for h in range(1, NSA_GROUP):
            psum = psum + p_heads[h]
        ov_i = _iota((nb, nr), 1) * CMP_STRIDE
        ov_j = _iota((nb, nr), 0) * SLC_BLOCK
        ov_t = ((ov_i < ov_j + SLC_BLOCK) & (ov_i + CMP_BLOCK > ov_j)).astype(BF16)
        p1, p2, p3 = _split3(psum)
        imp_t = _mm_nt(ov_t, p1) + _mm_nt(ov_t, p2) + _mm_nt(ov_t, p3)
        yield
        jb = _iota((nb, TQ), 0)
        cur = _div_pow2(q0 + _iota((nb, TQ), 1), SLC_BLOCK)
        forced = (jb == 0) | (jb == cur) | (jb == cur - 1)
        causal = jb <= cur
        score = jnp.where(forced, jnp.inf, jnp.where(causal, imp_t, -jnp.inf))
        rank = jnp.zeros((nb, TQ), jnp.int32)
        for jp in range(nb):
            sj = score[jp:jp + 1, :]
            beats = (sj > score) | ((sj == score) & (jb > jp))
            rank = rank + beats.astype(jnp.int32)
            if jp % 8 == 7:
                yield
        sel_t = ((rank < n_sel) & causal).astype(BF16)
        place = (_iota((nb, LANE), 0) == _iota((nb, LANE), 1)).astype(BF16)
        unsel = _mm_tn(sel_t, place) - (_iota((TQ, LANE), 1) < nb).astype(F32)
        yield
        return o_c, unsel.astype(BF16)

    def window():
        n_band = WINDOW // TQ + 1
        qk, tiles, v_w = [], [], []
        for c in range(n_band):
            j = qi - (n_band - 1) + c
            jc = jnp.maximum(j, 0)
            k0 = pl.multiple_of(jc * TQ, TQ)
            tiles.append((j, jc))
            qk.append(_mm_nt(q4, kw_ref[0, pl.ds(k0, TQ), :].astype(BF16)))
            v_w.append(_with_ones(vw_ref[0, pl.ds(k0, TQ), :].astype(BF16)))
        yield
        heads = range(NSA_GROUP)
        s_w = [[None] * NSA_GROUP for _ in range(n_band)]
        m_w = [None] * NSA_GROUP
        for c in range(n_band):
            for h in heads:
                s = qk[c][h * TQ:(h + 1) * TQ] + _bias_head(toep_ref, h, (qi - tiles[c][1]) * sub, True,
                                                             tiles[c][0] >= 0)
                s_w[c][h] = s
                m_w[h] = s if c == 0 else jnp.maximum(m_w[h], s)
                yield
        for h in heads:
            m_w[h] = jnp.max(m_w[h], axis=-1, keepdims=True)
        yield
        acc_w = [jnp.zeros((TQ, 2 * HEAD_DIM), F32) for _ in heads]
        for c in range(n_band):
            for h in heads:
                acc_w[h] = acc_w[h] + _mm(jnp.exp2(s_w[c][h] - m_w[h]).astype(BF16), v_w[c])
                yield
        return jnp.concatenate([a[:, :HEAD_DIM] / jnp.maximum(a[:, HEAD_DIM:], 1e-30) for a in acc_w], axis=0)

    def gates():
        gts = jax.nn.sigmoid(gate_ref[0])
        n_out = 3 * NSA_GROUP * LANE
        blk = _div_pow2(_iota((LANE, n_out), 1), LANE)
        src = _div_pow2(blk, NSA_GROUP) * NSA_HEADS + pl.program_id(1) * NSA_GROUP + _mod_pow2(blk, NSA_GROUP)
        pick = (_iota((LANE, n_out), 0) == src).astype(BF16)
        g_hi = gts.astype(BF16)
        g_lo = (gts - g_hi.astype(F32)).astype(BF16)
        rep = _mm(g_hi, pick) + _mm(g_lo, pick)
        yield
        return rep

    (o_c, unsel), o_w, gate_rep = _interleave([compressed_and_selection(), window(), gates()])

    q_sel = jnp.concatenate([q4, jnp.concatenate([unsel] * NSA_GROUP, axis=0)], axis=1)

    def key_tile(j):
        k0 = pl.multiple_of(j * TQ, TQ)
        blk = _div_pow2(j * TQ + _iota((TQ, LANE), 0), SLC_BLOCK)
        marks = jnp.where(_iota((TQ, LANE), 1) == blk, BIG, 0.0).astype(BF16)
        return jnp.concatenate([ks_ref[0, pl.ds(k0, TQ), :].astype(BF16), marks], axis=1)

    assert n_tiles % 2 == 0
    n_pairs = _div_pow2(qi + 2, 2)

    def slc_logits(jp, macc):
        pair = (2 * jp, 2 * jp + 1)
        qk = [_mm_nt(q_sel, key_tile(j)) for j in pair]
        for j, qk_j in zip(pair, qk):
            s = qk_j + _bias_tile(toep_ref, (qi - j) * sub, False)
            s_ref[j] = s
            macc = jnp.maximum(macc, _fold_lanes(s, jnp.maximum))
        return macc

    macc = lax.fori_loop(0, n_pairs, slc_logits, jnp.full((rows, LANE), NEG, F32))
    m_s = jnp.max(macc, axis=-1, keepdims=True)

    def slc_values(jp, acc):
        for j in (2 * jp, 2 * jp + 1):
            vt = vs_ref[0, pl.ds(pl.multiple_of(j * TQ, TQ), TQ), :].astype(BF16)
            acc = acc + _mm(jnp.exp2(s_ref[j] - m_s).astype(BF16), _with_ones(vt))
        return acc

    acc = lax.fori_loop(0, n_pairs, slc_values, jnp.zeros((rows, 2 * HEAD_DIM), F32))
    o_s = acc[:, :HEAD_DIM] / jnp.maximum(acc[:, HEAD_DIM:], 1e-30)

    za = za_ref[0]

    def gate(branch, h):
        k = branch * NSA_GROUP + h
        return gate_rep[:, k * LANE:(k + 1) * LANE]

    assert HEAD_DIM == LANE
    for h in range(NSA_GROUP):
        hr = slice(h * TQ, (h + 1) * TQ)
        o = gate(0, h) * o_c[hr] + gate(1, h) * o_s[hr] + gate(2, h) * o_w[hr]
        z = za[:, h * HEAD_DIM:(h + 1) * HEAD_DIM]
        o_ref[0, :, h * HEAD_DIM:(h + 1) * HEAD_DIM] = (o * (z * jax.nn.sigmoid(z))).astype(o_ref.dtype)


def _nsa(proj3, bias_c, toep, posk, w1k, w2k, posv, w1v, w2v):
    b, s, _ = proj3.shape
    nr = s // CMP_STRIDE
    gq = NSA_GROUP * HEAD_DIM

    def kvspec(idx):
        return pl.BlockSpec((1, s, HEAD_DIM), lambda bi, g, qi, idx=idx: (bi, 0, P_KV // HEAD_DIM + 2 * idx + g))

    def whole(a):
        return pl.BlockSpec(a.shape, lambda bi, g, qi, nd=a.ndim: (0,) * nd)

    in_specs = [
        pl.BlockSpec((1, TQ, gq), lambda bi, g, qi: (bi, qi, P_Q // gq + g)),
        kvspec(0), kvspec(1), kvspec(2), kvspec(3), kvspec(4), kvspec(5),
        pl.BlockSpec((1, TQ, LANE), lambda bi, g, qi: (bi, qi, P_GATE // LANE)),
        pl.BlockSpec((1, TQ, gq), lambda bi, g, qi: (bi, qi, P_ZA // gq + g)),
        pl.BlockSpec((NSA_GROUP, TQ, nr), lambda bi, g, qi: (g, qi, 0)),
        pl.BlockSpec((NSA_GROUP, NT_ALL, TB, TB), lambda bi, g, qi: (g, 0, 0, 0)),
        whole(posk), whole(w1k), whole(w2k), whole(posv), whole(w1v), whole(w2v),
    ]
    return pl.pallas_call(
        _nsa_kernel,
        grid=(b, NSA_KV_HEADS, s // TQ),
        in_specs=in_specs,
        out_specs=pl.BlockSpec((1, TQ, gq), lambda bi, g, qi: (bi, qi, g)),
        out_shape=jax.ShapeDtypeStruct((b, s, NSA_WIDTH), BF16),
        scratch_shapes=[pltpu.VMEM((nr, HEAD_DIM), BF16), pltpu.VMEM((nr, HEAD_DIM), BF16),
                        pltpu.VMEM((s // TQ, NSA_GROUP * TQ, TQ), F32)],
        compiler_params=pltpu.CompilerParams(
            dimension_semantics=("parallel", "parallel", "arbitrary"),
            vmem_limit_bytes=56 * 1024 * 1024),
        name="nsa",
    )(proj3, proj3, proj3, proj3, proj3, proj3, proj3, proj3, proj3,
      bias_c, toep, posk, w1k, w2k, posv, w1v, w2v)


def _shift_mix(ref, prev_ref, mu, sl):
    x = ref[0, :, sl]
    prev = jnp.where(_iota(x.shape, 0) == 0, prev_ref[:, sl], pltpu.roll(x, 1, 0))
    prev_ref[:, sl] = x[x.shape[0] - 1:]
    return x + mu * (prev - x)


def _rwkv_kernel(r_ref, k_ref, v_ref, wa_ref, zb_ref, vec_ref, muwa_ref, w2_ref, a2_ref,
                 o_ref, st_ref, pr_ref, pk_ref, pv_ref, pwa_ref):
    first = pl.program_id(2) == 0
    tb = r_ref.shape[1]
    n_groups = r_ref.shape[2] // GW
    n_chunks = tb // CH

    @pl.when(first)
    def _():
        for ref in (st_ref, pr_ref, pk_ref, pv_ref, pwa_ref):
            ref[...] = jnp.zeros_like(ref)

    wa = _shift_mix(wa_ref, pwa_ref, muwa_ref[...], slice(0, LANE))
    wd_act = jnp.tanh(wa[:, :LORA]).astype(BF16)
    ad = wa[:, LORA:].astype(BF16)

    seg = (_div_pow2(_iota((GW, GW), 0), RWKV_HEAD_DIM) == _div_pow2(_iota((GW, GW), 1), RWKV_HEAD_DIM))
    segf = seg.astype(F32)
    segb = seg.astype(BF16)
    assert CH == RWKV_HEAD_DIM
    lane_s = _mod_pow2(_iota((CH, GW), 1), CH)
    row_t = _iota((CH, GW), 0)
    strict = lane_s < row_t
    incl = lane_s <= row_t
    eye = (lane_s == row_t).astype(F32)
    ti_r, ti_c = _iota((tb, tb), 0), _iota((tb, tb), 1)
    trib = ((ti_c <= ti_r) & (_div_pow2(ti_c, CH) == _div_pow2(ti_r, CH))).astype(BF16)

    def bd(x):
        xb = x.astype(BF16)
        return jnp.where(seg, jnp.concatenate([xb] * HPG, axis=0), jnp.zeros((), BF16))

    def prep(gi):
        lanes = slice(gi * GW, (gi + 1) * GW)
        vec = vec_ref[:, lanes]
        mu_r, mu_k, mu_v = vec[0:1], vec[1:2], vec[2:3]
        w0, a0, k_k, k_a = vec[3:4], vec[4:5], vec[5:6], vec[6:7]
        r = _shift_mix(r_ref, pr_ref, mu_r, lanes)
        k = _shift_mix(k_ref, pk_ref, mu_k, lanes)
        v = _shift_mix(v_ref, pv_ref, mu_v, lanes)
        w_lora = _mm(wd_act, w2_ref[:, lanes])
        a_lora = _mm(ad, a2_ref[:, lanes])
        kk = k * k_k
        kk_ss = _mm((kk * kk).astype(BF16), segb)
        yield
        w = -jax.nn.softplus(-(w0 + w_lora)) - 0.5
        lw = -jnp.exp(w)
        cum = _mm_split_rhs(trib, lw)
        yield
        a_sig = jax.nn.sigmoid(a0 + a_lora)
        kk = kk / jnp.maximum(jnp.sqrt(kk_ss), 1e-12)
        k = k * (1.0 + (a_sig - 1.0) * k_a)
        return dict(r=r, k=k, v=v, a=-kk, b=kk * a_sig, lw=lw, cum=cum, vec=vec, lanes=lanes)

    groups = _interleave([prep(gi) for gi in range(n_groups)])

    def chunk_local(g, c):
        ts = slice(c * CH, (c + 1) * CH)
        rc, kc, vc, ac, bc, lwc, cum = (g[n][ts] for n in ("r", "k", "v", "a", "b", "lw", "cum"))
        tot = cum[CH - 1:CH]
        e_out = jnp.exp(-cum)
        e_end = jnp.exp(tot - cum)
        r_t = rc * jnp.exp(cum)
        a_t = ac * jnp.exp(cum - lwc)
        lhs = jnp.concatenate([a_t, r_t], axis=0).astype(BF16)
        aa = _mm_nt(lhs, jnp.concatenate([bd(bc * e_out), bd(kc * e_out)], axis=0))
        yield
        a_ab = jnp.where(strict, aa[:CH, :GW], 0.0)
        a_ak = jnp.where(strict, aa[:CH, GW:], 0.0)
        a_rb = jnp.where(incl, aa[CH:, :GW], 0.0)
        a_rk = jnp.where(incl, aa[CH:, GW:], 0.0)
        t_inv = eye + a_ab
        mpow = _mm(a_ab.astype(BF16), bd(a_ab))
        av = _mm(a_ak.astype(BF16), bd(vc))
        yield
        for _ in range(int(math.log2(CH)) - 1):
            res = _mm(jnp.concatenate([t_inv, mpow], axis=0).astype(BF16), bd(mpow))
            yield
            t_inv = t_inv + res[:CH]
            mpow = res[CH:]
        wu = _mm(t_inv.astype(BF16), jnp.concatenate([bd(a_t), bd(av)], axis=1))
        yield
        return dict(
            lhs=jnp.concatenate([wu[:, :GW], r_t], axis=0).astype(BF16), u_loc=wu[:, GW:],
            a_r=jnp.concatenate([a_rb, a_rk], axis=1).astype(BF16), bdv=bd(vc), vc=vc,
            bk_end=jnp.concatenate([bc * e_end, kc * e_end], axis=0).astype(BF16), dec=jnp.exp(tot))

    loc = _interleave([chunk_local(g, c) for g in groups for c in range(n_chunks)])

    def chain(gi):
        ys = []
        g_state = st_ref[gi]
        for c in range(n_chunks):
            lc = loc[gi * n_chunks + c]
            x0 = _mm_nt(lc["lhs"], g_state.astype(BF16))
            yield
            u = x0[:CH] + lc["u_loc"]
            y_c = _mm(lc["a_r"], jnp.concatenate([bd(u), lc["bdv"]], axis=0))
            upd = _mm_tn(jnp.concatenate([u, lc["vc"]], axis=0).astype(BF16), lc["bk_end"])
            yield
            ys.append(x0[CH:] + y_c)
            g_state = g_state * lc["dec"] + upd * segf
        st_ref[gi] = g_state
        return jnp.concatenate(ys, axis=0)

    ys = _interleave([chain(gi) for gi in range(n_groups)])

    def finish(g, y):
        vec = g["vec"]
        ln_w, ln_b, r_k = vec[7:8], vec[8:9], vec[9:10]
        inv_n = 1.0 / RWKV_HEAD_DIM
        mean = _mm(y.astype(BF16), segb) * inv_n
        bonus = _mm((g["r"] * g["k"] * r_k).astype(BF16), segb) * g["v"]
        yield
        yc = y - mean
        var = _mm((yc * yc).astype(BF16), segb) * inv_n
        yield
        yn = yc * lax.rsqrt(var + RWKV_GN_EPS) * ln_w + ln_b
        zb = zb_ref[0, :, g["lanes"]]
        o_ref[0, :, g["lanes"]] = ((yn + bonus) * (zb * jax.nn.sigmoid(zb))).astype(o_ref.dtype)

    _interleave([finish(g, y) for g, y in zip(groups, ys)])


def _rwkv(proj3, vecs, mu_wa, w2, a2, tb, gps):
    b, s, _ = proj3.shape
    gw = gps * GW
    ng = RWKV_WIDTH // gw

    def col(off):
        return pl.BlockSpec((1, tb, gw), lambda bi, g, ti, off=off: (bi, ti, off // gw + g))

    in_specs = [
        col(P_RKV), col(P_RKV + RWKV_WIDTH), col(P_RKV + 2 * RWKV_WIDTH),
        pl.BlockSpec((1, tb, LANE), lambda bi, g, ti: (bi, ti, P_WDAD // LANE)),
        col(P_ZB),
        pl.BlockSpec((vecs.shape[0], gw), lambda bi, g, ti: (0, g)),
        pl.BlockSpec((1, LANE), lambda bi, g, ti: (0, 0)),
        pl.BlockSpec((LORA, gw), lambda bi, g, ti: (0, g)),
        pl.BlockSpec((LORA, gw), lambda bi, g, ti: (0, g)),
    ]
    return pl.pallas_call(
        _rwkv_kernel,
        grid=(b, ng, s // tb),
        in_specs=in_specs,
        out_specs=pl.BlockSpec((1, tb, gw), lambda bi, g, ti: (bi, ti, g)),
        out_shape=jax.ShapeDtypeStruct((b, s, RWKV_WIDTH), BF16),
        scratch_shapes=[pltpu.VMEM((gps, GW, GW), F32), pltpu.VMEM((1, gw), F32), pltpu.VMEM((1, gw), F32),
                        pltpu.VMEM((1, gw), F32), pltpu.VMEM((1, LANE), F32)],
        compiler_params=pltpu.CompilerParams(
            dimension_semantics=("parallel", "parallel", "arbitrary")),
        name="rwkv",
    )(proj3, proj3, proj3, proj3, proj3, vecs, mu_wa, w2, a2)


def _outproj_kernel(ma_ref, mb_ref, wa_ref, wb_ref, x_ref, g_ref, o_ref):
    y = _mm(ma_ref[...], wa_ref[...]) + _mm(mb_ref[...], wb_ref[...])
    ms = jnp.mean(y * y, axis=-1, keepdims=True)
    o_ref[...] = x_ref[...] + y * lax.rsqrt(ms + NORM_EPS) * g_ref[...]


def _outproj(mix_a, mix_b, w_a, w_b, x2, g, tm):
    m, d = x2.shape
    ka, kb = mix_a.shape[1], mix_b.shape[1]
    return pl.pallas_call(
        _outproj_kernel,
        grid=(m // tm,),
        in_specs=[
            pl.BlockSpec((tm, ka), lambda i: (i, 0)),
            pl.BlockSpec((tm, kb), lambda i: (i, 0)),
            pl.BlockSpec((ka, d), lambda i: (0, 0)),
            pl.BlockSpec((kb, d), lambda i: (0, 0)),
            pl.BlockSpec((tm, d), lambda i: (i, 0)),
            pl.BlockSpec((1, d), lambda i: (0, 0)),
        ],
        out_specs=pl.BlockSpec((tm, d), lambda i: (i, 0)),
        out_shape=jax.ShapeDtypeStruct((m, d), F32),
        compiler_params=pltpu.CompilerParams(
            dimension_semantics=("parallel",), vmem_limit_bytes=56 * 1024 * 1024),
        name="outproj",
    )(mix_a, mix_b, w_a, w_b, x2, g)


_W_SEGMENTS = (
    (P_Q, R_Q, NSA_WIDTH),
    (P_RKV, R_FEAT, 3 * RWKV_WIDTH),
    (P_ZB, R_ZB, RWKV_WIDTH),
    (P_ZA, R_ZA, NSA_WIDTH),
    (P_KV, R_KV, 6 * NSA_KV_HEADS * HEAD_DIM),
    (P_WDAD, R_FEAT + 3 * RWKV_WIDTH, 2 * LORA),
    (P_GATE, R_GATE, LANE),
)


def _relayout_kernel(w_ref, o_ref):
    for dst, src, width in _W_SEGMENTS:
        o_ref[:, dst:dst + width] = w_ref[:, src:src + width].astype(BF16)
    used = P_GATE + LANE
    o_ref[:, used:] = jnp.zeros((o_ref.shape[0], NP - used), BF16)


def _permute_w_in(w, rows):
    d, n = w.shape
    return pl.pallas_call(
        _relayout_kernel,
        grid=(d // rows,),
        in_specs=[pl.BlockSpec((rows, n), lambda i: (i, 0))],
        out_specs=pl.BlockSpec((rows, NP), lambda i: (i, 0)),
        out_shape=jax.ShapeDtypeStruct((d, NP), BF16),
        compiler_params=pltpu.CompilerParams(dimension_semantics=("parallel",)),
        name="relayout",
    )(w)


def _block(x, pre_norm_g, w_in, rel_bias_table, cmp_pos_k, cmp_pos_v, cmp_k_w1, cmp_k_w2, cmp_v_w1,
           cmp_v_w2, rwkv_mu, rwkv_w0, rwkv_w2, rwkv_a0, rwkv_a2, rwkv_k_k, rwkv_k_a, rwkv_r_k,
           rwkv_ln_w, rwkv_ln_b, w_out, post_norm_g):
    b, s, d = x.shape
    x2 = x.reshape(b * s, d)
    tm = min(1024, b * s)
    proj = _inproj(x2, pre_norm_g.reshape(1, d), _permute_w_in(w_in.astype(BF16), 256), tm, 1024)
    proj3 = proj.reshape(b, s, NP)

    bias_c, toep = _bias(rel_bias_table.reshape(-1), s)
    half = CMP_STRIDE * HEAD_DIM
    mix_a = _nsa(proj3, bias_c, toep,
                 cmp_pos_k.reshape(2, half), cmp_k_w1.astype(BF16), cmp_k_w2.astype(BF16),
                 cmp_pos_v.reshape(2, half), cmp_v_w1.astype(BF16), cmp_v_w2.astype(BF16))

    w3 = 3 * RWKV_WIDTH
    vec_rows = [rwkv_mu[:RWKV_WIDTH], rwkv_mu[RWKV_WIDTH:2 * RWKV_WIDTH], rwkv_mu[2 * RWKV_WIDTH:w3],
                rwkv_w0, rwkv_a0, rwkv_k_k, rwkv_k_a, rwkv_ln_w, rwkv_ln_b, rwkv_r_k.reshape(-1)]
    vecs = jnp.stack(vec_rows + [jnp.zeros_like(rwkv_w0)] * (16 - len(vec_rows)), axis=0)
    mix_b = _rwkv(proj3, vecs, rwkv_mu[w3:].reshape(1, 2 * LORA), rwkv_w2.astype(BF16),
                  rwkv_a2.astype(BF16), min(256, s), RWKV_GROUPS_PER_STEP)

    w_o = w_out.astype(BF16)
    out = _outproj(mix_a.reshape(b * s, NSA_WIDTH), mix_b.reshape(b * s, RWKV_WIDTH),
                   w_o[:NSA_WIDTH], w_o[NSA_WIDTH:], x2, post_norm_g.reshape(1, d), min(512, b * s))
    return out.reshape(b, s, d)


def kernel(x, pre_norm_g, w_in, rel_bias_table, cmp_pos_k, cmp_pos_v, cmp_k_w1, cmp_k_w2, cmp_v_w1,
           cmp_v_w2, rwkv_mu, rwkv_w0, rwkv_w2, rwkv_a0, rwkv_a2, rwkv_k_k, rwkv_k_a, rwkv_r_k,
           rwkv_ln_w, rwkv_ln_b, w_out, post_norm_g):
    h = x
    for l in range(pre_norm_g.shape[0]):
        h = _block(h, pre_norm_g[l], w_in[l], rel_bias_table, cmp_pos_k[l], cmp_pos_v[l], cmp_k_w1[l],
                   cmp_k_w2[l], cmp_v_w1[l], cmp_v_w2[l], rwkv_mu[l], rwkv_w0[l], rwkv_w2[l],
                   rwkv_a0[l], rwkv_a2[l], rwkv_k_k[l], rwkv_k_a[l], rwkv_r_k[l], rwkv_ln_w[l],
                   rwkv_ln_b[l], w_out[l], post_norm_g[l])
    return h
```

```python
import functools
import math

import numpy as np
import jax
import jax.numpy as jnp
from jax import lax
from jax.experimental import pallas as pl
from jax.experimental.pallas import tpu as pltpu

F32 = jnp.float32
BF16 = jnp.bfloat16
HI = lax.Precision.HIGHEST

D_MODEL = 2048
NSA_HEADS = 8
NSA_KV_HEADS = 2
NSA_GROUP = NSA_HEADS // NSA_KV_HEADS
HEAD_DIM = 128
NSA_WIDTH = NSA_HEADS * HEAD_DIM
CMP_BLOCK = 32
CMP_STRIDE = 16
SLC_BLOCK = 64
SLC_TOP_N = 16
WINDOW = 512
RWKV_WIDTH = 1024
RWKV_HEAD_DIM = 64
RWKV_HEADS = RWKV_WIDTH // RWKV_HEAD_DIM
LORA = 64
NUM_BUCKETS = 32
MAX_DISTANCE = 1024
NORM_EPS = 1e-6
RWKV_GN_EPS = 64e-5

R_Q = 0
R_KV = R_Q + NSA_WIDTH
R_GATE = R_KV + 6 * NSA_KV_HEADS * HEAD_DIM
R_ZA = R_GATE + 3 * NSA_HEADS
R_FEAT = R_ZA + NSA_WIDTH
R_ZB = R_FEAT + 3 * RWKV_WIDTH + 2 * LORA
R_END = R_ZB + RWKV_WIDTH

P_Q = 0
P_RKV = 1024
P_ZB = 4096
P_ZA = 5120
P_KV = 6144
P_WDAD = 7680
P_GATE = 7808
NP = 8192

LANE = 128
TQ = 256
TB = 128
ND = 9
T_DIAG = ND
T_WEND = ND + 1
T_NONE = ND + 2
NT_ALL = ND + 3
LOG2E = math.log2(math.e)
BIG = 2.0 ** 100
CH = 64
HPG = 4
GW = HPG * RWKV_HEAD_DIM
RWKV_GROUPS_PER_STEP = 4
NEG = -1e30


def _bucket_thresholds():
    out = []
    for k in range(1, NUM_BUCKETS // 2):
        n = 16
        while n ** 8 < (16 ** 8) * (2 ** (3 * k)):
            n += 1
        out.append(n)
    return out


_THR = _bucket_thresholds()


def _mm(a, b, precision=None):
    return jnp.dot(a, b, preferred_element_type=F32, precision=precision)


def _mm_nt(a, b, precision=None):
    return lax.dot_general(a, b, (((1,), (1,)), ((), ())), preferred_element_type=F32,
                           precision=precision)


def _mm_tn(a, b, precision=None):
    return lax.dot_general(a, b, (((0,), (0,)), ((), ())), preferred_element_type=F32,
                           precision=precision)


def _split3(x):
    x1 = x.astype(BF16)
    r1 = x - x1.astype(F32)
    x2 = r1.astype(BF16)
    x3 = (r1 - x2.astype(F32)).astype(BF16)
    return x1, x2, x3


def _mm_split_rhs(a_exact, b):
    b1 = b.astype(BF16)
    b2 = (b - b1.astype(F32)).astype(BF16)
    return _mm(a_exact, b1) + _mm(a_exact, b2)


def _iota(shape, dim):
    return lax.broadcasted_iota(jnp.int32, shape, dim)


def _interleave(gens):
    results = [None] * len(gens)
    live = list(enumerate(gens))
    while live:
        still = []
        for i, g in live:
            try:
                next(g)
                still.append((i, g))
            except StopIteration as stop:
                results[i] = stop.value
        live = still
    return results


def _div_pow2(x, n):
    assert n & (n - 1) == 0
    return x >> (n.bit_length() - 1)


def _mod_pow2(x, n):
    assert n & (n - 1) == 0
    return x & (n - 1)


def _inproj_kernel(x_ref, g_ref, w_ref, o_ref, hn_ref):
    @pl.when(pl.program_id(1) == 0)
    def _():
        x = x_ref[...]
        ms = jnp.mean(x * x, axis=-1, keepdims=True)
        hn_ref[...] = (x * lax.rsqrt(ms + NORM_EPS) * g_ref[...]).astype(BF16)

    o_ref[...] = _mm(hn_ref[...], w_ref[...])


def _inproj(x2, g, w, tm, tn):
    m, d = x2.shape
    n = w.shape[1]
    return pl.pallas_call(
        _inproj_kernel,
        grid=(m // tm, n // tn),
        in_specs=[
            pl.BlockSpec((tm, d), lambda i, j: (i, 0)),
            pl.BlockSpec((1, d), lambda i, j: (0, 0)),
            pl.BlockSpec((d, tn), lambda i, j: (0, j)),
        ],
        out_specs=pl.BlockSpec((tm, tn), lambda i, j: (i, j)),
        out_shape=jax.ShapeDtypeStruct((m, n), F32),
        scratch_shapes=[pltpu.VMEM((tm, d), BF16)],
        compiler_params=pltpu.CompilerParams(
            dimension_semantics=("parallel", "arbitrary"),
            vmem_limit_bytes=56 * 1024 * 1024),
        name="inproj",
    )(x2, g, w)


def _bucket(n):
    n = jnp.maximum(n, 0)
    large = jnp.full(n.shape, NUM_BUCKETS // 2, jnp.int32)
    for thr in _THR:
        large = large + (n >= thr).astype(jnp.int32)
    return jnp.where(n < NUM_BUCKETS // 2, n, large)


def _lookup_all_heads(dist, tab_ref):
    bucket = _bucket(dist)
    hits = [bucket == b for b in range(NUM_BUCKETS)]
    outs = []
    for h in range(NSA_HEADS):
        out = jnp.zeros(dist.shape, F32)
        for b in range(NUM_BUCKETS):
            out = jnp.where(hits[b], tab_ref[b * NSA_HEADS + h] * LOG2E, out)
        outs.append(out)
    return outs


def _bias_kernel(tab_ref, bc_ref, tp_ref):
    i = pl.program_id(0)
    rows, nr = bc_ref.shape[1], bc_ref.shape[2]
    dist_c = (i * rows + _iota((rows, nr), 0)) - (_iota((rows, nr), 1) * CMP_STRIDE + (CMP_BLOCK - 1))
    for h, vals in enumerate(_lookup_all_heads(dist_c, tab_ref)):
        bc_ref[h] = vals

    @pl.when(i == 0)
    def _():
        base = _iota((TB, TB), 0) - _iota((TB, TB), 1)
        neg = jnp.full((TB, TB), NEG, F32)
        for h in range(NSA_HEADS):
            tp_ref[h, T_NONE] = neg
        for d in range(ND):
            for h, vals in enumerate(_lookup_all_heads(base + d * TB, tab_ref)):
                tp_ref[h, d] = vals
                if d == 0:
                    tp_ref[h, T_DIAG] = jnp.where(base >= 0, vals, neg)
                if d == WINDOW // TB:
                    tp_ref[h, T_WEND] = jnp.where(base < 0, vals, neg)


def _bias(table_flat, s):
    nr = s // CMP_STRIDE
    rows = min(256, s)
    return pl.pallas_call(
        _bias_kernel,
        grid=(s // rows,),
        in_specs=[pl.BlockSpec(memory_space=pltpu.SMEM)],
        out_specs=[
            pl.BlockSpec((NSA_HEADS, rows, nr), lambda i: (0, i, 0)),
            pl.BlockSpec((NSA_HEADS, NT_ALL, TB, TB), lambda i: (0, 0, 0, 0)),
        ],
        out_shape=[
            jax.ShapeDtypeStruct((NSA_HEADS, s, nr), F32),
            jax.ShapeDtypeStruct((NSA_HEADS, NT_ALL, TB, TB), F32),
        ],
        compiler_params=pltpu.CompilerParams(dimension_semantics=("arbitrary",)),
        name="bias",
    )(table_flat)


def _compress(kv_ref, pos_ref, w1_ref, w2_ref, nr):
    half = CMP_STRIDE * HEAD_DIM
    r = jnp.concatenate(
        [kv_ref[0, pl.ds(m, nr, stride=CMP_STRIDE), :] for m in range(CMP_STRIDE)], axis=1)
    a = _mm((r + pos_ref[0:1, :]).astype(BF16), w1_ref[0:half, :])
    b = _mm((r + pos_ref[1:2, :]).astype(BF16), w1_ref[half:2 * half, :])
    pre = a + pltpu.roll(b, nr - 1, 0)
    h1 = pre * jax.nn.sigmoid(pre)
    return _mm(h1.astype(BF16), w2_ref[...])


def _bias_head(toep_ref, h, d_tiles, window, valid=True):
    sub = TQ // TB
    rows = []
    for ri in range(sub):
        cols = []
        for ci in range(sub):
            d = d_tiles + ri - ci
            idx = jnp.where(d == 0, T_DIAG, jnp.minimum(d, ND - 1))
            if window:
                idx = jnp.where(d == WINDOW // TB, T_WEND, jnp.where(d > WINDOW // TB, T_NONE, idx))
            idx = jnp.where((d < 0) | jnp.logical_not(valid), T_NONE, idx)
            cols.append(toep_ref[h, idx])
        rows.append(jnp.concatenate(cols, axis=1))
    return jnp.concatenate(rows, axis=0)


def _bias_tile(toep_ref, d_tiles, window, valid=True):
    return jnp.concatenate([_bias_head(toep_ref, h, d_tiles, window, valid) for h in range(NSA_GROUP)], axis=0)


def _add_shared(s, mask_add):
    n = s.shape[-1]
    return (s.reshape(NSA_GROUP, TQ, n) + mask_add[None]).reshape(NSA_GROUP * TQ, n)


def _with_ones(v):
    return jnp.concatenate([v, jnp.ones(v.shape, v.dtype)], axis=1)


def _fold_lanes(x, op):
    out = x[:, :LANE]
    for c in range(1, x.shape[1] // LANE):
        out = op(out, x[:, c * LANE:(c + 1) * LANE])
    return out


def _nsa_kernel(q_ref, kc_ref, vc_ref, ks_ref, vs_ref, kw_ref, vw_ref, gate_ref, za_ref,
                bc_ref, toep_ref, posk_ref, w1k_ref, w2k_ref, posv_ref, w1v_ref, w2v_ref,
                o_ref, kcs_ref, vcs_ref, s_ref):
    qi = pl.program_id(2)
    s_len = kc_ref.shape[1]
    nr = s_len // CMP_STRIDE
    nb = s_len // SLC_BLOCK
    n_sel = min(SLC_TOP_N, nb)

    @pl.when(qi == 0)
    def _():
        kcs_ref[...] = _compress(kc_ref, posk_ref, w1k_ref, w2k_ref, nr).astype(BF16)
        vcs_ref[...] = _compress(vc_ref, posv_ref, w1v_ref, w2v_ref, nr).astype(BF16)

    q0 = qi * TQ
    rows = NSA_GROUP * TQ
    q = q_ref[0] * (HEAD_DIM ** -0.5 * LOG2E)
    q4 = jnp.concatenate([q[:, h * HEAD_DIM:(h + 1) * HEAD_DIM] for h in range(NSA_GROUP)],
                         axis=0).astype(BF16)

    sub = TQ // TB
    n_tiles = s_len // TQ

    def compressed_and_selection():
        t_c = q0 + _iota((TQ, nr), 0)
        i_c = _iota((TQ, nr), 1)
        mask_c = (t_c - (i_c * CMP_STRIDE + (CMP_BLOCK - 1)) >= 0) & (i_c < nr - 1)
        qk = _mm_nt(q4, kcs_ref[...])
        mask_add = jnp.where(mask_c, 0.0, NEG)
        yield
        p_heads = []
        for h in range(NSA_GROUP):
            lg = qk[h * TQ:(h + 1) * TQ] + bc_ref[h] + mask_add
            m_c = jnp.max(lg, axis=-1, keepdims=True)
            yield
            e = jnp.where(lg > 0.5 * NEG, jnp.exp2(lg - m_c), 0.0)
            l_c = jnp.sum(e, axis=-1, keepdims=True)
            yield
            p_heads.append(e / jnp.maximum(l_c, 1e-30))
        p = jnp.concatenate(p_heads, axis=0)
        o_c = _mm(p.astype(BF16), vcs_ref[...])
        psum = p_heads[0]
        for h in range(1, NSA_GROUP):
            psum = psum + p_heads[h]
        ov_i = _iota((nb, nr), 1) * CMP_STRIDE
        ov_j = _iota((nb, nr), 0) * SLC_BLOCK
        ov_t = ((ov_i < ov_j + SLC_BLOCK) & (ov_i + CMP_BLOCK > ov_j)).astype(BF16)
        p1, p2, p3 = _split3(psum)
        imp_t = _mm_nt(ov_t, p1) + _mm_nt(ov_t, p2) + _mm_nt(ov_t, p3)
        yield
        jb = _iota((nb, TQ), 0)
        cur = _div_pow2(q0 + _iota((nb, TQ), 1), SLC_BLOCK)
        forced = (jb == 0) | (jb == cur) | (jb == cur - 1)
        causal = jb <= cur
        score = jnp.where(forced, jnp.inf, jnp.where(causal, imp_t, -jnp.inf))
        rank = jnp.zeros((nb, TQ), jnp.int32)
        for jp in range(nb):
            sj = score[jp:jp + 1, :]
            beats = (sj > score) | ((sj == score) & (jb > jp))
            rank = rank + beats.astype(jnp.int32)
            if jp % 8 == 7:
                yield
        sel_t = ((rank < n_sel) & causal).astype(BF16)
        place = (_iota((nb, LANE), 0) == _iota((nb, LANE), 1)).astype(BF16)
        unsel = _mm_tn(sel_t, place) - (_iota((TQ, LANE), 1) < nb).astype(F32)
        yield
        return o_c, unsel.astype(BF16)

    def window():
        n_band = WINDOW // TQ + 1
        qk, tiles, v_w = [], [], []
        for c in range(n_band):
            j = qi - (n_band - 1) + c
            jc = jnp.maximum(j, 0)
            k0 = pl.multiple_of(jc * TQ, TQ)
            tiles.append((j, jc))
            qk.append(_mm_nt(q4, kw_ref[0, pl.ds(k0, TQ), :].astype(BF16)))
            v_w.append(_with_ones(vw_ref[0, pl.ds(k0, TQ), :].astype(BF16)))
        yield
        heads = range(NSA_GROUP)
        s_w = [[None] * NSA_GROUP for _ in range(n_band)]
        m_w = [None] * NSA_GROUP
        for c in range(n_band):
            for h in heads:
                s = qk[c][h * TQ:(h + 1) * TQ] + _bias_head(toep_ref, h, (qi - tiles[c][1]) * sub, True,
                                                             tiles[c][0] >= 0)
                s_w[c][h] = s
                m_w[h] = s if c == 0 else jnp.maximum(m_w[h], s)
                yield
        for h in heads:
            m_w[h] = jnp.max(m_w[h], axis=-1, keepdims=True)
        yield
        acc_w = [jnp.zeros((TQ, 2 * HEAD_DIM), F32) for _ in heads]
        for c in range(n_band):
            for h in heads:
                acc_w[h] = acc_w[h] + _mm(jnp.exp2(s_w[c][h] - m_w[h]).astype(BF16), v_w[c])
                yield
        return jnp.concatenate([a[:, :HEAD_DIM] / jnp.maximum(a[:, HEAD_DIM:], 1e-30) for a in acc_w], axis=0)

    def gates():
        gts = jax.nn.sigmoid(gate_ref[0])
        n_out = 3 * NSA_GROUP * LANE
        blk = _div_pow2(_iota((LANE, n_out), 1), LANE)
        src = _div_pow2(blk, NSA_GROUP) * NSA_HEADS + pl.program_id(1) * NSA_GROUP + _mod_pow2(blk, NSA_GROUP)
        pick = (_iota((LANE, n_out), 0) == src).astype(BF16)
        g_hi = gts.astype(BF16)
        g_lo = (gts - g_hi.astype(F32)).astype(BF16)
        rep = _mm(g_hi, pick) + _mm(g_lo, pick)
        yield
        return rep

    (o_c, unsel), o_w, gate_rep = _interleave([compressed_and_selection(), window(), gates()])

    q_sel = jnp.concatenate([q4, jnp.concatenate([unsel] * NSA_GROUP, axis=0)], axis=1)

    def key_tile(j):
        k0 = pl.multiple_of(j * TQ, TQ)
        blk = _div_pow2(j * TQ + _iota((TQ, LANE), 0), SLC_BLOCK)
        marks = jnp.where(_iota((TQ, LANE), 1) == blk, BIG, 0.0).astype(BF16)
        return jnp.concatenate([ks_ref[0, pl.ds(k0, TQ), :].astype(BF16), marks], axis=1)

    assert n_tiles % 2 == 0
    n_pairs = _div_pow2(qi + 2, 2)

    def slc_logits(jp, macc):
        pair = (2 * jp, 2 * jp + 1)
        qk = [_mm_nt(q_sel, key_tile(j)) for j in pair]
        for j, qk_j in zip(pair, qk):
            s = qk_j + _bias_tile(toep_ref, (qi - j) * sub, False)
            s_ref[j] = s
            macc = jnp.maximum(macc, _fold_lanes(s, jnp.maximum))
        return macc

    macc = lax.fori_loop(0, n_pairs, slc_logits, jnp.full((rows, LANE), NEG, F32))
    m_s = jnp.max(macc, axis=-1, keepdims=True)

    def slc_values(jp, acc):
        for j in (2 * jp, 2 * jp + 1):
            vt = vs_ref[0, pl.ds(pl.multiple_of(j * TQ, TQ), TQ), :].astype(BF16)
            acc = acc + _mm(jnp.exp2(s_ref[j] - m_s).astype(BF16), _with_ones(vt))
        return acc

    acc = lax.fori_loop(0, n_pairs, slc_values, jnp.zeros((rows, 2 * HEAD_DIM), F32))
    o_s = acc[:, :HEAD_DIM] / jnp.maximum(acc[:, HEAD_DIM:], 1e-30)

    za = za_ref[0]

    def gate(branch, h):
        k = branch * NSA_GROUP + h
        return gate_rep[:, k * LANE:(k + 1) * LANE]

    assert HEAD_DIM == LANE
    for h in range(NSA_GROUP):
        hr = slice(h * TQ, (h + 1) * TQ)
        o = gate(0, h) * o_c[hr] + gate(1, h) * o_s[hr] + gate(2, h) * o_w[hr]
        z = za[:, h * HEAD_DIM:(h + 1) * HEAD_DIM]
        o_ref[0, :, h * HEAD_DIM:(h + 1) * HEAD_DIM] = (o * (z * jax.nn.sigmoid(z))).astype(o_ref.dtype)


def _nsa(proj3, bias_c, toep, posk, w1k, w2k, posv, w1v, w2v):
    b, s, _ = proj3.shape
    nr = s // CMP_STRIDE
    gq = NSA_GROUP * HEAD_DIM

    def kvspec(idx):
        return pl.BlockSpec((1, s, HEAD_DIM), lambda bi, g, qi, idx=idx: (bi, 0, P_KV // HEAD_DIM + 2 * idx + g))

    def whole(a):
        return pl.BlockSpec(a.shape, lambda bi, g, qi, nd=a.ndim: (0,) * nd)

    in_specs = [
        pl.BlockSpec((1, TQ, gq), lambda bi, g, qi: (bi, qi, P_Q // gq + g)),
        kvspec(0), kvspec(1), kvspec(2), kvspec(3), kvspec(4), kvspec(5),
        pl.BlockSpec((1, TQ, LANE), lambda bi, g, qi: (bi, qi, P_GATE // LANE)),
        pl.BlockSpec((1, TQ, gq), lambda bi, g, qi: (bi, qi, P_ZA // gq + g)),
        pl.BlockSpec((NSA_GROUP, TQ, nr), lambda bi, g, qi: (g, qi, 0)),
        pl.BlockSpec((NSA_GROUP, NT_ALL, TB, TB), lambda bi, g, qi: (g, 0, 0, 0)),
        whole(posk), whole(w1k), whole(w2k), whole(posv), whole(w1v), whole(w2v),
    ]
    return pl.pallas_call(
        _nsa_kernel,
        grid=(b, NSA_KV_HEADS, s // TQ),
        in_specs=in_specs,
        out_specs=pl.BlockSpec((1, TQ, gq), lambda bi, g, qi: (bi, qi, g)),
        out_shape=jax.ShapeDtypeStruct((b, s, NSA_WIDTH), BF16),
        scratch_shapes=[pltpu.VMEM((nr, HEAD_DIM), BF16), pltpu.VMEM((nr, HEAD_DIM), BF16),
                        pltpu.VMEM((s // TQ, NSA_GROUP * TQ, TQ), F32)],
        compiler_params=pltpu.CompilerParams(
            dimension_semantics=("parallel", "parallel", "arbitrary"),
            vmem_limit_bytes=56 * 1024 * 1024),
        name="nsa",
    )(proj3, proj3, proj3, proj3, proj3, proj3, proj3, proj3, proj3,
      bias_c, toep, posk, w1k, w2k, posv, w1v, w2v)


def _shift_mix(ref, prev_ref, mu, sl):
    x = ref[0, :, sl]
    prev = jnp.where(_iota(x.shape, 0) == 0, prev_ref[:, sl], pltpu.roll(x, 1, 0))
    prev_ref[:, sl] = x[x.shape[0] - 1:]
    return x + mu * (prev - x)


def _rwkv_kernel(r_ref, k_ref, v_ref, wa_ref, zb_ref, vec_ref, muwa_ref, w2_ref, a2_ref,
                 o_ref, st_ref, pr_ref, pk_ref, pv_ref, pwa_ref):
    first = pl.program_id(2) == 0
    tb = r_ref.shape[1]
    n_groups = r_ref.shape[2] // GW
    n_chunks = tb // CH

    @pl.when(first)
    def _():
        for ref in (st_ref, pr_ref, pk_ref, pv_ref, pwa_ref):
            ref[...] = jnp.zeros_like(ref)

    wa = _shift_mix(wa_ref, pwa_ref, muwa_ref[...], slice(0, LANE))
    wd_act = jnp.tanh(wa[:, :LORA]).astype(BF16)
    ad = wa[:, LORA:].astype(BF16)

    seg = (_div_pow2(_iota((GW, GW), 0), RWKV_HEAD_DIM) == _div_pow2(_iota((GW, GW), 1), RWKV_HEAD_DIM))
    segf = seg.astype(F32)
    segb = seg.astype(BF16)
    assert CH == RWKV_HEAD_DIM
    lane_s = _mod_pow2(_iota((CH, GW), 1), CH)
    row_t = _iota((CH, GW), 0)
    strict = lane_s < row_t
    incl = lane_s <= row_t
    eye = (lane_s == row_t).astype(F32)
    ti_r, ti_c = _iota((tb, tb), 0), _iota((tb, tb), 1)
    trib = ((ti_c <= ti_r) & (_div_pow2(ti_c, CH) == _div_pow2(ti_r, CH))).astype(BF16)

    def bd(x):
        xb = x.astype(BF16)
        return jnp.concatenate([xb] * HPG, axis=0) * segb

    def prep(gi):
        lanes = slice(gi * GW, (gi + 1) * GW)
        vec = vec_ref[:, lanes]
        mu_r, mu_k, mu_v = vec[0:1], vec[1:2], vec[2:3]
        w0, a0, k_k, k_a = vec[3:4], vec[4:5], vec[5:6], vec[6:7]
        r = _shift_mix(r_ref, pr_ref, mu_r, lanes)
        k = _shift_mix(k_ref, pk_ref, mu_k, lanes)
        v = _shift_mix(v_ref, pv_ref, mu_v, lanes)
        w_lora = _mm(wd_act, w2_ref[:, lanes])
        a_lora = _mm(ad, a2_ref[:, lanes])
        kk = k * k_k
        kk_ss = _mm((kk * kk).astype(BF16), segb)
        yield
        lw = jax.nn.sigmoid(w0 + w_lora) * (-math.exp(-0.5) * LOG2E)
        cum = _mm_split_rhs(trib, lw)
        yield
        a_sig = jax.nn.sigmoid(a0 + a_lora)
        kk = kk * lax.rsqrt(jnp.maximum(kk_ss, 1e-24))
        k = k * (1.0 + (a_sig - 1.0) * k_a)
        return dict(r=r, k=k, v=v, a=-kk, b=kk * a_sig, lw=lw, cum=cum, vec=vec, lanes=lanes)

    groups = _interleave([prep(gi) for gi in range(n_groups)])

    def chunk_local(g, c):
        ts = slice(c * CH, (c + 1) * CH)
        rc, kc, vc, ac, bc, lwc, cum = (g[n][ts] for n in ("r", "k", "v", "a", "b", "lw", "cum"))
        tot = cum[CH - 1:CH]
        e_out = jnp.exp2(-cum)
        e_end = jnp.exp2(tot - cum)
        r_t = rc * jnp.exp2(cum)
        a_t = ac * jnp.exp2(cum - lwc)
        lhs = jnp.concatenate([a_t, r_t], axis=0).astype(BF16)
        aa = _mm_nt(lhs, jnp.concatenate([bd(bc * e_out), bd(kc * e_out)], axis=0))
        yield
        a_ab = jnp.where(strict, aa[:CH, :GW], 0.0)
        a_ak = jnp.where(strict, aa[:CH, GW:], 0.0)
        a_rb = jnp.where(incl, aa[CH:, :GW], 0.0)
        a_rk = jnp.where(incl, aa[CH:, GW:], 0.0)
        t_inv = eye + a_ab
        mpow = _mm(a_ab.astype(BF16), bd(a_ab))
        av = _mm(a_ak.astype(BF16), bd(vc))
        yield
        for _ in range(int(math.log2(CH)) - 1):
            res = _mm(jnp.concatenate([t_inv, mpow], axis=0).astype(BF16), bd(mpow))
            yield
            t_inv = t_inv + res[:CH]
            mpow = res[CH:]
        wu = _mm(t_inv.astype(BF16), jnp.concatenate([bd(a_t), bd(av)], axis=1))
        yield
        return dict(
            lhs=jnp.concatenate([wu[:, :GW], r_t], axis=0).astype(BF16), u_loc=wu[:, GW:],
            a_r=jnp.concatenate([a_rb, a_rk], axis=1).astype(BF16), bdv=bd(vc), vc=vc,
            bk_end=jnp.concatenate([bc * e_end, kc * e_end], axis=0).astype(BF16), dec=jnp.exp2(tot))

    loc = _interleave([chunk_local(g, c) for g in groups for c in range(n_chunks)])

    def chain(gi):
        ys = []
        g_state = st_ref[gi]
        for c in range(n_chunks):
            lc = loc[gi * n_chunks + c]
            x0 = _mm_nt(lc["lhs"], g_state.astype(BF16))
            yield
            u = x0[:CH] + lc["u_loc"]
            y_c = _mm(lc["a_r"], jnp.concatenate([bd(u), lc["bdv"]], axis=0))
            upd = _mm_tn(jnp.concatenate([u, lc["vc"]], axis=0).astype(BF16), lc["bk_end"])
            yield
            ys.append(x0[CH:] + y_c)
            g_state = g_state * lc["dec"] + upd * segf
        st_ref[gi] = g_state
        return jnp.concatenate(ys, axis=0)

    ys = _interleave([chain(gi) for gi in range(n_groups)])

    def finish(g, y):
        vec = g["vec"]
        ln_w, ln_b, r_k = vec[7:8], vec[8:9], vec[9:10]
        inv_n = 1.0 / RWKV_HEAD_DIM
        mean = _mm(y.astype(BF16), segb) * inv_n
        bonus = _mm((g["r"] * g["k"] * r_k).astype(BF16), segb) * g["v"]
        yield
        yc = y - mean
        var = _mm((yc * yc).astype(BF16), segb) * inv_n
        yield
        yn = yc * lax.rsqrt(var + RWKV_GN_EPS) * ln_w + ln_b
        zb = zb_ref[0, :, g["lanes"]]
        o_ref[0, :, g["lanes"]] = ((yn + bonus) * (zb * jax.nn.sigmoid(zb))).astype(o_ref.dtype)

    _interleave([finish(g, y) for g, y in zip(groups, ys)])


def _rwkv(proj3, vecs, mu_wa, w2, a2, tb, gps):
    b, s, _ = proj3.shape
    gw = gps * GW
    ng = RWKV_WIDTH // gw

    def col(off):
        return pl.BlockSpec((1, tb, gw), lambda bi, g, ti, off=off: (bi, ti, off // gw + g))

    in_specs = [
        col(P_RKV), col(P_RKV + RWKV_WIDTH), col(P_RKV + 2 * RWKV_WIDTH),
        pl.BlockSpec((1, tb, LANE), lambda bi, g, ti: (bi, ti, P_WDAD // LANE)),
        col(P_ZB),
        pl.BlockSpec((vecs.shape[0], gw), lambda bi, g, ti: (0, g)),
        pl.BlockSpec((1, LANE), lambda bi, g, ti: (0, 0)),
        pl.BlockSpec((LORA, gw), lambda bi, g, ti: (0, g)),
        pl.BlockSpec((LORA, gw), lambda bi, g, ti: (0, g)),
    ]
    return pl.pallas_call(
        _rwkv_kernel,
        grid=(b, ng, s // tb),
        in_specs=in_specs,
        out_specs=pl.BlockSpec((1, tb, gw), lambda bi, g, ti: (bi, ti, g)),
        out_shape=jax.ShapeDtypeStruct((b, s, RWKV_WIDTH), BF16),
        scratch_shapes=[pltpu.VMEM((gps, GW, GW), F32), pltpu.VMEM((1, gw), F32), pltpu.VMEM((1, gw), F32),
                        pltpu.VMEM((1, gw), F32), pltpu.VMEM((1, LANE), F32)],
        compiler_params=pltpu.CompilerParams(
            dimension_semantics=("parallel", "parallel", "arbitrary")),
        name="rwkv",
    )(proj3, proj3, proj3, proj3, proj3, vecs, mu_wa, w2, a2)


def _outproj_kernel(ma_ref, mb_ref, wa_ref, wb_ref, x_ref, g_ref, o_ref):
    y = _mm(ma_ref[...], wa_ref[...]) + _mm(mb_ref[...], wb_ref[...])
    ms = jnp.mean(y * y, axis=-1, keepdims=True)
    o_ref[...] = x_ref[...] + y * lax.rsqrt(ms + NORM_EPS) * g_ref[...]


def _outproj(mix_a, mix_b, w_a, w_b, x2, g, tm):
    m, d = x2.shape
    ka, kb = mix_a.shape[1], mix_b.shape[1]
    return pl.pallas_call(
        _outproj_kernel,
        grid=(m // tm,),
        in_specs=[
            pl.BlockSpec((tm, ka), lambda i: (i, 0)),
            pl.BlockSpec((tm, kb), lambda i: (i, 0)),
            pl.BlockSpec((ka, d), lambda i: (0, 0)),
            pl.BlockSpec((kb, d), lambda i: (0, 0)),
            pl.BlockSpec((tm, d), lambda i: (i, 0)),
            pl.BlockSpec((1, d), lambda i: (0, 0)),
        ],
        out_specs=pl.BlockSpec((tm, d), lambda i: (i, 0)),
        out_shape=jax.ShapeDtypeStruct((m, d), F32),
        compiler_params=pltpu.CompilerParams(
            dimension_semantics=("parallel",), vmem_limit_bytes=56 * 1024 * 1024),
        name="outproj",
    )(mix_a, mix_b, w_a, w_b, x2, g)


_W_SEGMENTS = (
    (P_Q, R_Q, NSA_WIDTH),
    (P_RKV, R_FEAT, 3 * RWKV_WIDTH),
    (P_ZB, R_ZB, RWKV_WIDTH),
    (P_ZA, R_ZA, NSA_WIDTH),
    (P_KV, R_KV, 6 * NSA_KV_HEADS * HEAD_DIM),
    (P_WDAD, R_FEAT + 3 * RWKV_WIDTH, 2 * LORA),
    (P_GATE, R_GATE, LANE),
)


def _relayout_kernel(w_ref, o_ref):
    for dst, src, width in _W_SEGMENTS:
        o_ref[:, dst:dst + width] = w_ref[:, src:src + width].astype(BF16)
    used = P_GATE + LANE
    o_ref[:, used:] = jnp.zeros((o_ref.shape[0], NP - used), BF16)


def _permute_w_in(w, rows):
    d, n = w.shape
    return pl.pallas_call(
        _relayout_kernel,
        grid=(d // rows,),
        in_specs=[pl.BlockSpec((rows, n), lambda i: (i, 0))],
        out_specs=pl.BlockSpec((rows, NP), lambda i: (i, 0)),
        out_shape=jax.ShapeDtypeStruct((d, NP), BF16),
        compiler_params=pltpu.CompilerParams(dimension_semantics=("parallel",)),
        name="relayout",
    )(w)


def _block(x, pre_norm_g, w_in, rel_bias_table, cmp_pos_k, cmp_pos_v, cmp_k_w1, cmp_k_w2, cmp_v_w1,
           cmp_v_w2, rwkv_mu, rwkv_w0, rwkv_w2, rwkv_a0, rwkv_a2, rwkv_k_k, rwkv_k_a, rwkv_r_k,
           rwkv_ln_w, rwkv_ln_b, w_out, post_norm_g):
    b, s, d = x.shape
    x2 = x.reshape(b * s, d)
    tm = min(1024, b * s)
    proj = _inproj(x2, pre_norm_g.reshape(1, d), _permute_w_in(w_in.astype(BF16), 256), tm, 1024)
    proj3 = proj.reshape(b, s, NP)

    bias_c, toep = _bias(rel_bias_table.reshape(-1), s)
    half = CMP_STRIDE * HEAD_DIM
    mix_a = _nsa(proj3, bias_c, toep,
                 cmp_pos_k.reshape(2, half), cmp_k_w1.astype(BF16), cmp_k_w2.astype(BF16),
                 cmp_pos_v.reshape(2, half), cmp_v_w1.astype(BF16), cmp_v_w2.astype(BF16))

    w3 = 3 * RWKV_WIDTH
    vec_rows = [rwkv_mu[:RWKV_WIDTH], rwkv_mu[RWKV_WIDTH:2 * RWKV_WIDTH], rwkv_mu[2 * RWKV_WIDTH:w3],
                rwkv_w0, rwkv_a0, rwkv_k_k, rwkv_k_a, rwkv_ln_w, rwkv_ln_b, rwkv_r_k.reshape(-1)]
    vecs = jnp.stack(vec_rows + [jnp.zeros_like(rwkv_w0)] * (16 - len(vec_rows)), axis=0)
    mix_b = _rwkv(proj3, vecs, rwkv_mu[w3:].reshape(1, 2 * LORA), rwkv_w2.astype(BF16),
                  rwkv_a2.astype(BF16), min(256, s), RWKV_GROUPS_PER_STEP)

    w_o = w_out.astype(BF16)
    out = _outproj(mix_a.reshape(b * s, NSA_WIDTH), mix_b.reshape(b * s, RWKV_WIDTH),
                   w_o[:NSA_WIDTH], w_o[NSA_WIDTH:], x2, post_norm_g.reshape(1, d), min(512, b * s))
    return out.reshape(b, s, d)


def kernel(x, pre_norm_g, w_in, rel_bias_table, cmp_pos_k, cmp_pos_v, cmp_k_w1, cmp_k_w2, cmp_v_w1,
           cmp_v_w2, rwkv_mu, rwkv_w0, rwkv_w2, rwkv_a0, rwkv_a2, rwkv_k_k, rwkv_k_a, rwkv_r_k,
           rwkv_ln_w, rwkv_ln_b, w_out, post_norm_g):
    h = x
    for l in range(pre_norm_g.shape[0]):
        h = _block(h, pre_norm_g[l], w_in[l], rel_bias_table, cmp_pos_k[l], cmp_pos_v[l], cmp_k_w1[l],
                   cmp_k_w2[l], cmp_v_w1[l], cmp_v_w2[l], rwkv_mu[l], rwkv_w0[l], rwkv_w2[l],
                   rwkv_a0[l], rwkv_a2[l], rwkv_k_k[l], rwkv_k_a[l], rwkv_r_k[l], rwkv_ln_w[l],
                   rwkv_ln_b[l], w_out[l], post_norm_g[l])
    return h
```

```python
import math

import jax
import jax.numpy as jnp
from jax import lax
from jax.experimental import pallas as pl
from jax.experimental.pallas import tpu as pltpu

F32 = jnp.float32
BF16 = jnp.bfloat16

D_MODEL = 2048
NSA_HEADS = 8
NSA_KV_HEADS = 2
NSA_GROUP = NSA_HEADS // NSA_KV_HEADS
HEAD_DIM = 128
NSA_WIDTH = NSA_HEADS * HEAD_DIM
CMP_BLOCK = 32
CMP_STRIDE = 16
SLC_BLOCK = 64
SLC_TOP_N = 16
WINDOW = 512
RWKV_WIDTH = 1024
RWKV_HEAD_DIM = 64
RWKV_HEADS = RWKV_WIDTH // RWKV_HEAD_DIM
LORA = 64
NUM_BUCKETS = 32
MAX_DISTANCE = 1024
NORM_EPS = 1e-6
RWKV_GN_EPS = 64e-5

R_Q = 0
R_KV = R_Q + NSA_WIDTH
R_GATE = R_KV + 6 * NSA_KV_HEADS * HEAD_DIM
R_ZA = R_GATE + 3 * NSA_HEADS
R_FEAT = R_ZA + NSA_WIDTH
R_ZB = R_FEAT + 3 * RWKV_WIDTH + 2 * LORA
R_END = R_ZB + RWKV_WIDTH

P_Q = 0
P_RKV = 1024
P_ZB = 4096
P_ZA = 5120
P_KV = 6144
P_WDAD = 7680
P_GATE = 7808
NP = 8192

LANE = 128
TQ = 256
TB = 128
ND = 9
T_DIAG = ND
T_WEND = ND + 1
T_NONE = ND + 2
NT_ALL = ND + 3
LOG2E = math.log2(math.e)
BIG = 2.0 ** 100
CH = 64
HPG = 4
GW = HPG * RWKV_HEAD_DIM
RWKV_GROUPS_PER_STEP = 4
NEG = -1e30


def _bucket_thresholds():
    out = []
    for k in range(1, NUM_BUCKETS // 2):
        n = 16
        while n ** 8 < (16 ** 8) * (2 ** (3 * k)):
            n += 1
        out.append(n)
    return out


_THR = _bucket_thresholds()


def _mm(a, b):
    return jnp.dot(a, b, preferred_element_type=F32)


def _mm_nt(a, b):
    return lax.dot_general(a, b, (((1,), (1,)), ((), ())), preferred_element_type=F32)


def _mm_tn(a, b):
    return lax.dot_general(a, b, (((0,), (0,)), ((), ())), preferred_element_type=F32)


def _split3(x):
    x1 = x.astype(BF16)
    r1 = x - x1.astype(F32)
    x2 = r1.astype(BF16)
    x3 = (r1 - x2.astype(F32)).astype(BF16)
    return x1, x2, x3


def _mm_split_rhs(a_exact, b):
    b1 = b.astype(BF16)
    b2 = (b - b1.astype(F32)).astype(BF16)
    return _mm(a_exact, b1) + _mm(a_exact, b2)


def _iota(shape, dim):
    return lax.broadcasted_iota(jnp.int32, shape, dim)


def _interleave(gens):
    results = [None] * len(gens)
    live = list(enumerate(gens))
    while live:
        still = []
        for i, g in live:
            try:
                next(g)
                still.append((i, g))
            except StopIteration as stop:
                results[i] = stop.value
        live = still
    return results


def _div_pow2(x, n):
    assert n & (n - 1) == 0
    return x >> (n.bit_length() - 1)


def _mod_pow2(x, n):
    assert n & (n - 1) == 0
    return x & (n - 1)


def _inproj_kernel(x_ref, g_ref, w_ref, o_ref, hn_ref):
    @pl.when(pl.program_id(1) == 0)
    def _():
        x = x_ref[...]
        ms = jnp.mean(x * x, axis=-1, keepdims=True)
        hn_ref[...] = (x * lax.rsqrt(ms + NORM_EPS) * g_ref[...]).astype(BF16)

    o_ref[...] = _mm(hn_ref[...], w_ref[...])


def _inproj(x2, g, w, tm, tn):
    m, d = x2.shape
    n = w.shape[1]
    return pl.pallas_call(
        _inproj_kernel,
        grid=(m // tm, n // tn),
        in_specs=[
            pl.BlockSpec((tm, d), lambda i, j: (i, 0)),
            pl.BlockSpec((1, d), lambda i, j: (0, 0)),
            pl.BlockSpec((d, tn), lambda i, j: (0, j)),
        ],
        out_specs=pl.BlockSpec((tm, tn), lambda i, j: (i, j)),
        out_shape=jax.ShapeDtypeStruct((m, n), F32),
        scratch_shapes=[pltpu.VMEM((tm, d), BF16)],
        compiler_params=pltpu.CompilerParams(
            dimension_semantics=("parallel", "arbitrary"),
            vmem_limit_bytes=56 * 1024 * 1024),
        name="inproj",
    )(x2, g, w)


def _bucket(n):
    n = jnp.maximum(n, 0)
    large = jnp.full(n.shape, NUM_BUCKETS // 2, jnp.int32)
    for thr in _THR:
        large = large + (n >= thr).astype(jnp.int32)
    return jnp.where(n < NUM_BUCKETS // 2, n, large)


def _lookup_all_heads(dist, tab_ref):
    bucket = _bucket(dist)
    hits = [bucket == b for b in range(NUM_BUCKETS)]
    outs = []
    for h in range(NSA_HEADS):
        out = jnp.zeros(dist.shape, F32)
        for b in range(NUM_BUCKETS):
            out = jnp.where(hits[b], tab_ref[b * NSA_HEADS + h] * LOG2E, out)
        outs.append(out)
    return outs


def _bias_kernel(tab_ref, bc_ref, tp_ref):
    i = pl.program_id(0)
    rows, nr = bc_ref.shape[1], bc_ref.shape[2]
    dist_c = (i * rows + _iota((rows, nr), 0)) - (_iota((rows, nr), 1) * CMP_STRIDE + (CMP_BLOCK - 1))
    for h, vals in enumerate(_lookup_all_heads(dist_c, tab_ref)):
        bc_ref[h] = vals

    @pl.when(i == 0)
    def _():
        base = _iota((TB, TB), 0) - _iota((TB, TB), 1)
        neg = jnp.full((TB, TB), NEG, F32)
        for h in range(NSA_HEADS):
            tp_ref[h, T_NONE] = neg
        for d in range(ND):
            for h, vals in enumerate(_lookup_all_heads(base + d * TB, tab_ref)):
                tp_ref[h, d] = vals
                if d == 0:
                    tp_ref[h, T_DIAG] = jnp.where(base >= 0, vals, neg)
                if d == WINDOW // TB:
                    tp_ref[h, T_WEND] = jnp.where(base < 0, vals, neg)


def _bias(table_flat, s):
    nr = s // CMP_STRIDE
    rows = min(256, s)
    return pl.pallas_call(
        _bias_kernel,
        grid=(s // rows,),
        in_specs=[pl.BlockSpec(memory_space=pltpu.SMEM)],
        out_specs=[
            pl.BlockSpec((NSA_HEADS, rows, nr), lambda i: (0, i, 0)),
            pl.BlockSpec((NSA_HEADS, NT_ALL, TB, TB), lambda i: (0, 0, 0, 0)),
        ],
        out_shape=[
            jax.ShapeDtypeStruct((NSA_HEADS, s, nr), F32),
            jax.ShapeDtypeStruct((NSA_HEADS, NT_ALL, TB, TB), F32),
        ],
        compiler_params=pltpu.CompilerParams(dimension_semantics=("arbitrary",)),
        name="bias",
    )(table_flat)


def _compress(kv_ref, pos_ref, w1_ref, w2_ref, nr):
    half = CMP_STRIDE * HEAD_DIM
    r = jnp.concatenate(
        [kv_ref[0, pl.ds(m, nr, stride=CMP_STRIDE), :] for m in range(CMP_STRIDE)], axis=1)
    a = _mm((r + pos_ref[0:1, :]).astype(BF16), w1_ref[0:half, :])
    b = _mm((r + pos_ref[1:2, :]).astype(BF16), w1_ref[half:2 * half, :])
    pre = a + pltpu.roll(b, nr - 1, 0)
    h1 = pre * jax.nn.sigmoid(pre)
    return _mm(h1.astype(BF16), w2_ref[...])


def _bias_head(toep_ref, h, d_tiles, window, valid=True):
    sub = TQ // TB
    rows = []
    for ri in range(sub):
        cols = []
        for ci in range(sub):
            d = d_tiles + ri - ci
            idx = jnp.where(d == 0, T_DIAG, jnp.minimum(d, ND - 1))
            if window:
                idx = jnp.where(d == WINDOW // TB, T_WEND, jnp.where(d > WINDOW // TB, T_NONE, idx))
            idx = jnp.where((d < 0) | jnp.logical_not(valid), T_NONE, idx)
            cols.append(toep_ref[h, idx])
        rows.append(jnp.concatenate(cols, axis=1))
    return jnp.concatenate(rows, axis=0)


def _bias_tile(toep_ref, d_tiles, window, valid=True):
    return jnp.concatenate([_bias_head(toep_ref, h, d_tiles, window, valid) for h in range(NSA_GROUP)], axis=0)


def _add_shared(s, mask_add):
    n = s.shape[-1]
    return (s.reshape(NSA_GROUP, TQ, n) + mask_add[None]).reshape(NSA_GROUP * TQ, n)


def _with_ones(v):
    return jnp.concatenate([v, jnp.ones(v.shape, v.dtype)], axis=1)


def _fold_lanes(x, op):
    out = x[:, :LANE]
    for c in range(1, x.shape[1] // LANE):
        out = op(out, x[:, c * LANE:(c + 1) * LANE])
    return out


def _nsa_kernel(q_ref, kc_ref, vc_ref, ks_ref, vs_ref, kw_ref, vw_ref, gate_ref, za_ref,
                bc_ref, toep_ref, posk_ref, w1k_ref, w2k_ref, posv_ref, w1v_ref, w2v_ref,
                o_ref, kcs_ref, vcs_ref, s_ref):
    qi = pl.program_id(2)
    s_len = kc_ref.shape[1]
    nr = s_len // CMP_STRIDE
    nb = s_len // SLC_BLOCK
    n_sel = min(SLC_TOP_N, nb)

    @pl.when(qi == 0)
    def _():
        kcs_ref[...] = _compress(kc_ref, posk_ref, w1k_ref, w2k_ref, nr).astype(BF16)
        vcs_ref[...] = _compress(vc_ref, posv_ref, w1v_ref, w2v_ref, nr).astype(BF16)

    q0 = qi * TQ
    rows = NSA_GROUP * TQ
    q = q_ref[0] * (HEAD_DIM ** -0.5 * LOG2E)
    q4 = jnp.concatenate([q[:, h * HEAD_DIM:(h + 1) * HEAD_DIM] for h in range(NSA_GROUP)],
                         axis=0).astype(BF16)

    sub = TQ // TB
    n_tiles = s_len // TQ

    def compressed_and_selection():
        t_c = q0 + _iota((TQ, nr), 0)
        i_c = _iota((TQ, nr), 1)
        mask_c = (t_c - (i_c * CMP_STRIDE + (CMP_BLOCK - 1)) >= 0) & (i_c < nr - 1)
        qk = _mm_nt(q4, kcs_ref[...])
        mask_add = jnp.where(mask_c, 0.0, NEG)
        yield
        p_heads = []
        for h in range(NSA_GROUP):
            lg = qk[h * TQ:(h + 1) * TQ] + bc_ref[h] + mask_add
            m_c = jnp.max(lg, axis=-1, keepdims=True)
            yield
            e = jnp.where(lg > 0.5 * NEG, jnp.exp2(lg - m_c), 0.0)
            l_c = jnp.sum(e, axis=-1, keepdims=True)
            yield
            p_heads.append(e / jnp.maximum(l_c, 1e-30))
        p = jnp.concatenate(p_heads, axis=0)
        o_c = _mm(p.astype(BF16), vcs_ref[...])
        psum = p_heads[0]
        for h in range(1, NSA_GROUP):
            psum = psum + p_heads[h]
        ov_i = _iota((nb, nr), 1) * CMP_STRIDE
        ov_j = _iota((nb, nr), 0) * SLC_BLOCK
        ov_t = ((ov_i < ov_j + SLC_BLOCK) & (ov_i + CMP_BLOCK > ov_j)).astype(BF16)
        p1, p2, p3 = _split3(psum)
        imp_t = _mm_nt(ov_t, p1) + _mm_nt(ov_t, p2) + _mm_nt(ov_t, p3)
        yield
        jb = _iota((nb, TQ), 0)
        cur = _div_pow2(q0 + _iota((nb, TQ), 1), SLC_BLOCK)
        forced = (jb == 0) | (jb == cur) | (jb == cur - 1)
        causal = jb <= cur
        score = jnp.where(forced, jnp.inf, jnp.where(causal, imp_t, -jnp.inf))
        rank = jnp.zeros((nb, TQ), jnp.int32)
        for jp in range(nb):
            sj = score[jp:jp + 1, :]
            beats = (sj > score) | ((sj == score) & (jb > jp))
            rank = rank + beats.astype(jnp.int32)
            if jp % 8 == 7:
                yield
        sel_t = ((rank < n_sel) & causal).astype(BF16)
        place = (_iota((nb, LANE), 0) == _iota((nb, LANE), 1)).astype(BF16)
        unsel = _mm_tn(sel_t, place) - (_iota((TQ, LANE), 1) < nb).astype(F32)
        yield
        return o_c, unsel.astype(BF16)

    def window():
        n_band = WINDOW // TQ + 1
        qk, tiles, v_w = [], [], []
        for c in range(n_band):
            j = qi - (n_band - 1) + c
            jc = jnp.maximum(j, 0)
            k0 = pl.multiple_of(jc * TQ, TQ)
            tiles.append((j, jc))
            qk.append(_mm_nt(q4, kw_ref[0, pl.ds(k0, TQ), :].astype(BF16)))
            v_w.append(_with_ones(vw_ref[0, pl.ds(k0, TQ), :].astype(BF16)))
        yield
        heads = range(NSA_GROUP)
        s_w = [[None] * NSA_GROUP for _ in range(n_band)]
        m_w = [None] * NSA_GROUP
        for c in range(n_band):
            for h in heads:
                s = qk[c][h * TQ:(h + 1) * TQ] + _bias_head(toep_ref, h, (qi - tiles[c][1]) * sub, True,
                                                             tiles[c][0] >= 0)
                s_w[c][h] = s
                m_w[h] = s if c == 0 else jnp.maximum(m_w[h], s)
                yield
        for h in heads:
            m_w[h] = jnp.max(m_w[h], axis=-1, keepdims=True)
        yield
        v_band = jnp.concatenate(v_w, axis=0)
        acc_w = []
        for h in heads:
            p_h = []
            for c in range(n_band):
                p_h.append(jnp.exp2(s_w[c][h] - m_w[h]).astype(BF16))
                yield
            acc_w.append(_mm(jnp.concatenate(p_h, axis=1), v_band))
            yield
        return jnp.concatenate([a[:, :HEAD_DIM] / jnp.maximum(a[:, HEAD_DIM:], 1e-30) for a in acc_w], axis=0)

    def gates():
        gts = jax.nn.sigmoid(gate_ref[0])
        n_out = 3 * NSA_GROUP * LANE
        blk = _div_pow2(_iota((LANE, n_out), 1), LANE)
        src = _div_pow2(blk, NSA_GROUP) * NSA_HEADS + pl.program_id(1) * NSA_GROUP + _mod_pow2(blk, NSA_GROUP)
        pick = (_iota((LANE, n_out), 0) == src).astype(BF16)
        g_hi = gts.astype(BF16)
        g_lo = (gts - g_hi.astype(F32)).astype(BF16)
        rep = _mm(g_hi, pick) + _mm(g_lo, pick)
        yield
        return rep

    (o_c, unsel), o_w, gate_rep = _interleave([compressed_and_selection(), window(), gates()])

    q_sel = jnp.concatenate([q4, jnp.concatenate([unsel] * NSA_GROUP, axis=0)], axis=1)

    def key_tile(j):
        k0 = pl.multiple_of(j * TQ, TQ)
        blk = _div_pow2(j * TQ + _iota((TQ, LANE), 0), SLC_BLOCK)
        marks = jnp.where(_iota((TQ, LANE), 1) == blk, BIG, 0.0).astype(BF16)
        return jnp.concatenate([ks_ref[0, pl.ds(k0, TQ), :].astype(BF16), marks], axis=1)

    assert n_tiles % 4 == 0
    n_pairs = _div_pow2(qi + 2, 2)
    n_quads = _div_pow2(n_pairs, 2)
    passes = ((4, 0, n_quads), (2, 4 * n_quads, n_pairs - 2 * n_quads))

    def slc_logits(tpi, base):
        def body(it, macc):
            tiles = [base + tpi * it + c for c in range(tpi)]
            qk = [_mm_nt(q_sel, key_tile(j)) for j in tiles]
            for j, qk_j in zip(tiles, qk):
                s = qk_j + _bias_tile(toep_ref, (qi - j) * sub, False)
                s_ref[j] = s
                macc = jnp.maximum(macc, _fold_lanes(s, jnp.maximum))
            return macc
        return body

    macc = jnp.full((rows, LANE), NEG, F32)
    for tpi, base, trips in passes:
        macc = lax.fori_loop(0, trips, slc_logits(tpi, base), macc)
    m_s = jnp.max(macc, axis=-1, keepdims=True)

    def slc_values(tpi, base):
        def body(it, acc):
            j0 = base + tpi * it
            k0 = pl.multiple_of(j0 * TQ, 2 * TQ)
            p_it = jnp.concatenate([jnp.exp2(s_ref[j0 + c] - m_s).astype(BF16) for c in range(tpi)], axis=1)
            return acc + _mm(p_it, _with_ones(vs_ref[0, pl.ds(k0, tpi * TQ), :].astype(BF16)))
        return body

    acc = jnp.zeros((rows, 2 * HEAD_DIM), F32)
    for tpi, base, trips in passes:
        acc = lax.fori_loop(0, trips, slc_values(tpi, base), acc)
    o_s = acc[:, :HEAD_DIM] / jnp.maximum(acc[:, HEAD_DIM:], 1e-30)

    za = za_ref[0]

    def gate(branch, h):
        k = branch * NSA_GROUP + h
        return gate_rep[:, k * LANE:(k + 1) * LANE]

    assert HEAD_DIM == LANE
    for h in range(NSA_GROUP):
        hr = slice(h * TQ, (h + 1) * TQ)
        o = gate(0, h) * o_c[hr] + gate(1, h) * o_s[hr] + gate(2, h) * o_w[hr]
        z = za[:, h * HEAD_DIM:(h + 1) * HEAD_DIM]
        o_ref[0, :, h * HEAD_DIM:(h + 1) * HEAD_DIM] = (o * (z * jax.nn.sigmoid(z))).astype(o_ref.dtype)


def _nsa(proj3, bias_c, toep, posk, w1k, w2k, posv, w1v, w2v):
    b, s, _ = proj3.shape
    nr = s // CMP_STRIDE
    gq = NSA_GROUP * HEAD_DIM

    def kvspec(idx):
        return pl.BlockSpec((1, s, HEAD_DIM), lambda bi, g, qi, idx=idx: (bi, 0, P_KV // HEAD_DIM + 2 * idx + g))

    def whole(a):
        return pl.BlockSpec(a.shape, lambda bi, g, qi, nd=a.ndim: (0,) * nd)

    in_specs = [
        pl.BlockSpec((1, TQ, gq), lambda bi, g, qi: (bi, qi, P_Q // gq + g)),
        kvspec(0), kvspec(1), kvspec(2), kvspec(3), kvspec(4), kvspec(5),
        pl.BlockSpec((1, TQ, LANE), lambda bi, g, qi: (bi, qi, P_GATE // LANE)),
        pl.BlockSpec((1, TQ, gq), lambda bi, g, qi: (bi, qi, P_ZA // gq + g)),
        pl.BlockSpec((NSA_GROUP, TQ, nr), lambda bi, g, qi: (g, qi, 0)),
        pl.BlockSpec((NSA_GROUP, NT_ALL, TB, TB), lambda bi, g, qi: (g, 0, 0, 0)),
        whole(posk), whole(w1k), whole(w2k), whole(posv), whole(w1v), whole(w2v),
    ]
    return pl.pallas_call(
        _nsa_kernel,
        grid=(b, NSA_KV_HEADS, s // TQ),
        in_specs=in_specs,
        out_specs=pl.BlockSpec((1, TQ, gq), lambda bi, g, qi: (bi, qi, g)),
        out_shape=jax.ShapeDtypeStruct((b, s, NSA_WIDTH), BF16),
        scratch_shapes=[pltpu.VMEM((nr, HEAD_DIM), BF16), pltpu.VMEM((nr, HEAD_DIM), BF16),
                        pltpu.VMEM((s // TQ, NSA_GROUP * TQ, TQ), F32)],
        compiler_params=pltpu.CompilerParams(
            dimension_semantics=("parallel", "parallel", "arbitrary"),
            vmem_limit_bytes=56 * 1024 * 1024),
        name="nsa",
    )(proj3, proj3, proj3, proj3, proj3, proj3, proj3, proj3, proj3,
      bias_c, toep, posk, w1k, w2k, posv, w1v, w2v)


def _shift_mix(ref, prev_ref, mu, sl):
    x = ref[0, :, sl]
    prev = jnp.where(_iota(x.shape, 0) == 0, prev_ref[:, sl], pltpu.roll(x, 1, 0))
    prev_ref[:, sl] = x[x.shape[0] - 1:]
    return x + mu * (prev - x)


def _rwkv_kernel(r_ref, k_ref, v_ref, wa_ref, zb_ref, vec_ref, muwa_ref, w2_ref, a2_ref,
                 o_ref, st_ref, pr_ref, pk_ref, pv_ref, pwa_ref):
    first = pl.program_id(2) == 0
    tb = r_ref.shape[1]
    n_groups = r_ref.shape[2] // GW
    n_chunks = tb // CH

    @pl.when(first)
    def _():
        for ref in (st_ref, pr_ref, pk_ref, pv_ref, pwa_ref):
            ref[...] = jnp.zeros_like(ref)

    wa = _shift_mix(wa_ref, pwa_ref, muwa_ref[...], slice(0, LANE))
    wd_act = jnp.tanh(wa[:, :LORA]).astype(BF16)
    ad = wa[:, LORA:].astype(BF16)

    seg = (_div_pow2(_iota((GW, GW), 0), RWKV_HEAD_DIM) == _div_pow2(_iota((GW, GW), 1), RWKV_HEAD_DIM))
    segf = seg.astype(F32)
    segb = seg.astype(BF16)
    assert CH == RWKV_HEAD_DIM
    lane_s = _mod_pow2(_iota((CH, GW), 1), CH)
    row_t = _iota((CH, GW), 0)
    strict = lane_s < row_t
    incl = lane_s <= row_t
    eye = (lane_s == row_t).astype(F32)
    ti_r, ti_c = _iota((tb, tb), 0), _iota((tb, tb), 1)
    trib = ((ti_c <= ti_r) & (_div_pow2(ti_c, CH) == _div_pow2(ti_r, CH))).astype(BF16)

    def bd(x):
        xb = x.astype(BF16)
        return jnp.concatenate([xb] * HPG, axis=0) * segb

    def prep(gi):
        lanes = slice(gi * GW, (gi + 1) * GW)
        vec = vec_ref[:, lanes]
        mu_r, mu_k, mu_v = vec[0:1], vec[1:2], vec[2:3]
        w0, a0, k_k, k_a = vec[3:4], vec[4:5], vec[5:6], vec[6:7]
        r = _shift_mix(r_ref, pr_ref, mu_r, lanes)
        k = _shift_mix(k_ref, pk_ref, mu_k, lanes)
        v = _shift_mix(v_ref, pv_ref, mu_v, lanes)
        w_lora = _mm(wd_act, w2_ref[:, lanes])
        a_lora = _mm(ad, a2_ref[:, lanes])
        kk = k * k_k
        kk_ss = _mm((kk * kk).astype(BF16), segb)
        yield
        lw = jax.nn.sigmoid(w0 + w_lora) * (-math.exp(-0.5) * LOG2E)
        cum = _mm_split_rhs(trib, lw)
        yield
        a_sig = jax.nn.sigmoid(a0 + a_lora)
        kk = kk * lax.rsqrt(jnp.maximum(kk_ss, 1e-24))
        k = k * (1.0 + (a_sig - 1.0) * k_a)
        return dict(r=r, k=k, v=v, a=-kk, b=kk * a_sig, lw=lw, cum=cum, vec=vec, lanes=lanes)

    groups = _interleave([prep(gi) for gi in range(n_groups)])

    def chunk_local(g, c):
        ts = slice(c * CH, (c + 1) * CH)
        rc, kc, vc, ac, bc, lwc, cum = (g[n][ts] for n in ("r", "k", "v", "a", "b", "lw", "cum"))
        tot = cum[CH - 1:CH]
        e_out = jnp.exp2(-cum)
        e_end = jnp.exp2(tot - cum)
        r_t = rc * jnp.exp2(cum)
        a_t = ac * jnp.exp2(cum - lwc)
        lhs = jnp.concatenate([a_t, r_t], axis=0).astype(BF16)
        aa = _mm_nt(lhs, jnp.concatenate([bd(bc * e_out), bd(kc * e_out)], axis=0))
        yield
        a_ab = jnp.where(strict, aa[:CH, :GW], 0.0)
        a_ak = jnp.where(strict, aa[:CH, GW:], 0.0)
        a_rb = jnp.where(incl, aa[CH:, :GW], 0.0)
        a_rk = jnp.where(incl, aa[CH:, GW:], 0.0)
        t_inv = eye + a_ab
        mpow = _mm(a_ab.astype(BF16), bd(a_ab))
        av = _mm(a_ak.astype(BF16), bd(vc))
        yield
        for _ in range(int(math.log2(CH)) - 1):
            res = _mm(jnp.concatenate([t_inv, mpow], axis=0).astype(BF16), bd(mpow))
            yield
            t_inv = t_inv + res[:CH]
            mpow = res[CH:]
        wu = _mm(t_inv.astype(BF16), jnp.concatenate([bd(a_t), bd(av)], axis=1))
        yield
        return dict(
            lhs=jnp.concatenate([wu[:, :GW], r_t], axis=0).astype(BF16), u_loc=wu[:, GW:],
            a_r=jnp.concatenate([a_rb, a_rk], axis=1).astype(BF16), bdv=bd(vc), vc=vc,
            bk_end=jnp.concatenate([bc * e_end, kc * e_end], axis=0).astype(BF16), dec=jnp.exp2(tot))

    loc = _interleave([chunk_local(g, c) for g in groups for c in range(n_chunks)])

    def chain(gi):
        ys = []
        g_state = st_ref[gi]
        for c in range(n_chunks):
            lc = loc[gi * n_chunks + c]
            x0 = _mm_nt(lc["lhs"], g_state.astype(BF16))
            yield
            u = x0[:CH] + lc["u_loc"]
            y_c = _mm(lc["a_r"], jnp.concatenate([bd(u), lc["bdv"]], axis=0))
            upd = _mm_tn(jnp.concatenate([u, lc["vc"]], axis=0).astype(BF16), lc["bk_end"])
            yield
            ys.append(x0[CH:] + y_c)
            g_state = g_state * lc["dec"] + upd * segf
        st_ref[gi] = g_state
        return jnp.concatenate(ys, axis=0)

    ys = _interleave([chain(gi) for gi in range(n_groups)])

    def finish(g, y):
        vec = g["vec"]
        ln_w, ln_b, r_k = vec[7:8], vec[8:9], vec[9:10]
        inv_n = 1.0 / RWKV_HEAD_DIM
        mean = _mm(y.astype(BF16), segb) * inv_n
        bonus = _mm((g["r"] * g["k"] * r_k).astype(BF16), segb) * g["v"]
        yield
        yc = y - mean
        var = _mm((yc * yc).astype(BF16), segb) * inv_n
        yield
        yn = yc * lax.rsqrt(var + RWKV_GN_EPS) * ln_w + ln_b
        zb = zb_ref[0, :, g["lanes"]]
        o_ref[0, :, g["lanes"]] = ((yn + bonus) * (zb * jax.nn.sigmoid(zb))).astype(o_ref.dtype)

    _interleave([finish(g, y) for g, y in zip(groups, ys)])


def _rwkv(proj3, vecs, mu_wa, w2, a2, tb, gps):
    b, s, _ = proj3.shape
    gw = gps * GW
    ng = RWKV_WIDTH // gw

    def col(off):
        return pl.BlockSpec((1, tb, gw), lambda bi, g, ti, off=off: (bi, ti, off // gw + g))

    in_specs = [
        col(P_RKV), col(P_RKV + RWKV_WIDTH), col(P_RKV + 2 * RWKV_WIDTH),
        pl.BlockSpec((1, tb, LANE), lambda bi, g, ti: (bi, ti, P_WDAD // LANE)),
        col(P_ZB),
        pl.BlockSpec((vecs.shape[0], gw), lambda bi, g, ti: (0, g)),
        pl.BlockSpec((1, LANE), lambda bi, g, ti: (0, 0)),
        pl.BlockSpec((LORA, gw), lambda bi, g, ti: (0, g)),
        pl.BlockSpec((LORA, gw), lambda bi, g, ti: (0, g)),
    ]
    return pl.pallas_call(
        _rwkv_kernel,
        grid=(b, ng, s // tb),
        in_specs=in_specs,
        out_specs=pl.BlockSpec((1, tb, gw), lambda bi, g, ti: (bi, ti, g)),
        out_shape=jax.ShapeDtypeStruct((b, s, RWKV_WIDTH), BF16),
        scratch_shapes=[pltpu.VMEM((gps, GW, GW), F32), pltpu.VMEM((1, gw), F32), pltpu.VMEM((1, gw), F32),
                        pltpu.VMEM((1, gw), F32), pltpu.VMEM((1, LANE), F32)],
        compiler_params=pltpu.CompilerParams(
            dimension_semantics=("parallel", "parallel", "arbitrary")),
        name="rwkv",
    )(proj3, proj3, proj3, proj3, proj3, vecs, mu_wa, w2, a2)


def _outproj_kernel(ma_ref, mb_ref, wa_ref, wb_ref, x_ref, g_ref, o_ref):
    y = _mm(ma_ref[...], wa_ref[...]) + _mm(mb_ref[...], wb_ref[...])
    ms = jnp.mean(y * y, axis=-1, keepdims=True)
    o_ref[...] = x_ref[...] + y * lax.rsqrt(ms + NORM_EPS) * g_ref[...]


def _outproj(mix_a, mix_b, w_a, w_b, x2, g, tm):
    m, d = x2.shape
    ka, kb = mix_a.shape[1], mix_b.shape[1]
    return pl.pallas_call(
        _outproj_kernel,
        grid=(m // tm,),
        in_specs=[
            pl.BlockSpec((tm, ka), lambda i: (i, 0)),
            pl.BlockSpec((tm, kb), lambda i: (i, 0)),
            pl.BlockSpec((ka, d), lambda i: (0, 0)),
            pl.BlockSpec((kb, d), lambda i: (0, 0)),
            pl.BlockSpec((tm, d), lambda i: (i, 0)),
            pl.BlockSpec((1, d), lambda i: (0, 0)),
        ],
        out_specs=pl.BlockSpec((tm, d), lambda i: (i, 0)),
        out_shape=jax.ShapeDtypeStruct((m, d), F32),
        compiler_params=pltpu.CompilerParams(
            dimension_semantics=("parallel",), vmem_limit_bytes=56 * 1024 * 1024),
        name="outproj",
    )(mix_a, mix_b, w_a, w_b, x2, g)


_W_SEGMENTS = (
    (P_Q, R_Q, NSA_WIDTH),
    (P_RKV, R_FEAT, 3 * RWKV_WIDTH),
    (P_ZB, R_ZB, RWKV_WIDTH),
    (P_ZA, R_ZA, NSA_WIDTH),
    (P_KV, R_KV, 6 * NSA_KV_HEADS * HEAD_DIM),
    (P_WDAD, R_FEAT + 3 * RWKV_WIDTH, 2 * LORA),
    (P_GATE, R_GATE, LANE),
)


def _relayout_kernel(w_ref, o_ref):
    for dst, src, width in _W_SEGMENTS:
        o_ref[:, dst:dst + width] = w_ref[:, src:src + width].astype(BF16)
    used = P_GATE + LANE
    o_ref[:, used:] = jnp.zeros((o_ref.shape[0], NP - used), BF16)


def _permute_w_in(w, rows):
    d, n = w.shape
    return pl.pallas_call(
        _relayout_kernel,
        grid=(d // rows,),
        in_specs=[pl.BlockSpec((rows, n), lambda i: (i, 0))],
        out_specs=pl.BlockSpec((rows, NP), lambda i: (i, 0)),
        out_shape=jax.ShapeDtypeStruct((d, NP), BF16),
        compiler_params=pltpu.CompilerParams(dimension_semantics=("parallel",)),
        name="relayout",
    )(w)


def _block(x, pre_norm_g, w_in, rel_bias_table, cmp_pos_k, cmp_pos_v, cmp_k_w1, cmp_k_w2, cmp_v_w1,
           cmp_v_w2, rwkv_mu, rwkv_w0, rwkv_w2, rwkv_a0, rwkv_a2, rwkv_k_k, rwkv_k_a, rwkv_r_k,
           rwkv_ln_w, rwkv_ln_b, w_out, post_norm_g):
    b, s, d = x.shape
    x2 = x.reshape(b * s, d)
    tm = min(1024, b * s)
    proj = _inproj(x2, pre_norm_g.reshape(1, d), _permute_w_in(w_in.astype(BF16), 256), tm, 1024)
    proj3 = proj.reshape(b, s, NP)

    bias_c, toep = _bias(rel_bias_table.reshape(-1), s)
    half = CMP_STRIDE * HEAD_DIM
    mix_a = _nsa(proj3, bias_c, toep,
                 cmp_pos_k.reshape(2, half), cmp_k_w1.astype(BF16), cmp_k_w2.astype(BF16),
                 cmp_pos_v.reshape(2, half), cmp_v_w1.astype(BF16), cmp_v_w2.astype(BF16))

    w3 = 3 * RWKV_WIDTH
    vec_rows = [rwkv_mu[:RWKV_WIDTH], rwkv_mu[RWKV_WIDTH:2 * RWKV_WIDTH], rwkv_mu[2 * RWKV_WIDTH:w3],
                rwkv_w0, rwkv_a0, rwkv_k_k, rwkv_k_a, rwkv_ln_w, rwkv_ln_b, rwkv_r_k.reshape(-1)]
    vecs = jnp.stack(vec_rows + [jnp.zeros_like(rwkv_w0)] * (16 - len(vec_rows)), axis=0)
    mix_b = _rwkv(proj3, vecs, rwkv_mu[w3:].reshape(1, 2 * LORA), rwkv_w2.astype(BF16),
                  rwkv_a2.astype(BF16), min(256, s), RWKV_GROUPS_PER_STEP)

    w_o = w_out.astype(BF16)
    out = _outproj(mix_a.reshape(b * s, NSA_WIDTH), mix_b.reshape(b * s, RWKV_WIDTH),
                   w_o[:NSA_WIDTH], w_o[NSA_WIDTH:], x2, post_norm_g.reshape(1, d), min(512, b * s))
    return out.reshape(b, s, d)


def kernel(x, pre_norm_g, w_in, rel_bias_table, cmp_pos_k, cmp_pos_v, cmp_k_w1, cmp_k_w2, cmp_v_w1,
           cmp_v_w2, rwkv_mu, rwkv_w0, rwkv_w2, rwkv_a0, rwkv_a2, rwkv_k_k, rwkv_k_a, rwkv_r_k,
           rwkv_ln_w, rwkv_ln_b, w_out, post_norm_g):
    h = x
    for l in range(pre_norm_g.shape[0]):
        h = _block(h, pre_norm_g[l], w_in[l], rel_bias_table, cmp_pos_k[l], cmp_pos_v[l], cmp_k_w1[l],
                   cmp_k_w2[l], cmp_v_w1[l], cmp_v_w2[l], rwkv_mu[l], rwkv_w0[l], rwkv_w2[l],
                   rwkv_a0[l], rwkv_a2[l], rwkv_k_k[l], rwkv_k_a[l], rwkv_r_k[l], rwkv_ln_w[l],
                   rwkv_ln_b[l], w_out[l], post_norm_g[l])
    return h
```

```python
import math

import jax
import jax.numpy as jnp
from jax import lax
from jax.experimental import pallas as pl
from jax.experimental.pallas import tpu as pltpu

F32 = jnp.float32
BF16 = jnp.bfloat16

D_MODEL = 2048
NSA_HEADS = 8
NSA_KV_HEADS = 2
NSA_GROUP = NSA_HEADS // NSA_KV_HEADS
HEAD_DIM = 128
NSA_WIDTH = NSA_HEADS * HEAD_DIM
CMP_BLOCK = 32
CMP_STRIDE = 16
SLC_BLOCK = 64
SLC_TOP_N = 16
WINDOW = 512
RWKV_WIDTH = 1024
RWKV_HEAD_DIM = 64
RWKV_HEADS = RWKV_WIDTH // RWKV_HEAD_DIM
LORA = 64
NUM_BUCKETS = 32
MAX_DISTANCE = 1024
NORM_EPS = 1e-6
RWKV_GN_EPS = 64e-5

R_Q = 0
R_KV = R_Q + NSA_WIDTH
R_GATE = R_KV + 6 * NSA_KV_HEADS * HEAD_DIM
R_ZA = R_GATE + 3 * NSA_HEADS
R_FEAT = R_ZA + NSA_WIDTH
R_ZB = R_FEAT + 3 * RWKV_WIDTH + 2 * LORA
R_END = R_ZB + RWKV_WIDTH

P_Q = 0
P_RKV = 1024
P_ZB = 4096
P_ZA = 5120
P_KV = 6144
P_WDAD = 7680
P_GATE = 7808
NP = 8192

LANE = 128
TQ = 256
TB = 128
ND = 9
T_DIAG = ND
T_WEND = ND + 1
T_NONE = ND + 2
NT_ALL = ND + 3
LOG2E = math.log2(math.e)
BIG = 2.0 ** 100
CH = 64
HPG = 4
GW = HPG * RWKV_HEAD_DIM
RWKV_GROUPS_PER_STEP = 4
NEG = -1e30


def _bucket_thresholds():
    out = []
    for k in range(1, NUM_BUCKETS // 2):
        n = 16
        while n ** 8 < (16 ** 8) * (2 ** (3 * k)):
            n += 1
        out.append(n)
    return out


_THR = _bucket_thresholds()


def _mm(a, b):
    return jnp.dot(a, b, preferred_element_type=F32)


def _mm_nt(a, b):
    return lax.dot_general(a, b, (((1,), (1,)), ((), ())), preferred_element_type=F32)


def _mm_tn(a, b):
    return lax.dot_general(a, b, (((0,), (0,)), ((), ())), preferred_element_type=F32)


def _split3(x):
    x1 = x.astype(BF16)
    r1 = x - x1.astype(F32)
    x2 = r1.astype(BF16)
    x3 = (r1 - x2.astype(F32)).astype(BF16)
    return x1, x2, x3


def _mm_split_rhs(a_exact, b):
    b1 = b.astype(BF16)
    b2 = (b - b1.astype(F32)).astype(BF16)
    return _mm(a_exact, b1) + _mm(a_exact, b2)


def _iota(shape, dim):
    return lax.broadcasted_iota(jnp.int32, shape, dim)


def _interleave(gens):
    results = [None] * len(gens)
    live = list(enumerate(gens))
    while live:
        still = []
        for i, g in live:
            try:
                next(g)
                still.append((i, g))
            except StopIteration as stop:
                results[i] = stop.value
        live = still
    return results


def _div_pow2(x, n):
    assert n & (n - 1) == 0
    return x >> (n.bit_length() - 1)


def _mod_pow2(x, n):
    assert n & (n - 1) == 0
    return x & (n - 1)


def _inproj_kernel(x_ref, g_ref, w_ref, o_ref, hn_ref):
    @pl.when(pl.program_id(1) == 0)
    def _():
        x = x_ref[...]
        ms = jnp.mean(x * x, axis=-1, keepdims=True)
        hn_ref[...] = (x * lax.rsqrt(ms + NORM_EPS) * g_ref[...]).astype(BF16)

    o_ref[...] = _mm(hn_ref[...], w_ref[...])


def _inproj(x2, g, w, tm, tn):
    m, d = x2.shape
    n = w.shape[1]
    return pl.pallas_call(
        _inproj_kernel,
        grid=(m // tm, n // tn),
        in_specs=[
            pl.BlockSpec((tm, d), lambda i, j: (i, 0)),
            pl.BlockSpec((1, d), lambda i, j: (0, 0)),
            pl.BlockSpec((d, tn), lambda i, j: (0, j)),
        ],
        out_specs=pl.BlockSpec((tm, tn), lambda i, j: (i, j)),
        out_shape=jax.ShapeDtypeStruct((m, n), F32),
        scratch_shapes=[pltpu.VMEM((tm, d), BF16)],
        compiler_params=pltpu.CompilerParams(
            dimension_semantics=("parallel", "arbitrary"),
            vmem_limit_bytes=56 * 1024 * 1024),
        name="inproj",
    )(x2, g, w)


def _bucket(n):
    n = jnp.maximum(n, 0)
    large = jnp.full(n.shape, NUM_BUCKETS // 2, jnp.int32)
    for thr in _THR:
        large = large + (n >= thr).astype(jnp.int32)
    return jnp.where(n < NUM_BUCKETS // 2, n, large)


def _lookup_all_heads(dist, tab_ref):
    bucket = _bucket(dist)
    hits = [bucket == b for b in range(NUM_BUCKETS)]
    outs = []
    for h in range(NSA_HEADS):
        out = jnp.zeros(dist.shape, F32)
        for b in range(NUM_BUCKETS):
            out = jnp.where(hits[b], tab_ref[b * NSA_HEADS + h] * LOG2E, out)
        outs.append(out)
    return outs


def _bias_kernel(tab_ref, bc_ref, tp_ref):
    i = pl.program_id(0)
    rows, nr = bc_ref.shape[1], bc_ref.shape[2]
    dist_c = (i * rows + _iota((rows, nr), 0)) - (_iota((rows, nr), 1) * CMP_STRIDE + (CMP_BLOCK - 1))
    for h, vals in enumerate(_lookup_all_heads(dist_c, tab_ref)):
        bc_ref[h] = vals

    @pl.when(i == 0)
    def _():
        base = _iota((TB, TB), 0) - _iota((TB, TB), 1)
        neg = jnp.full((TB, TB), NEG, F32)
        for h in range(NSA_HEADS):
            tp_ref[h, T_NONE] = neg
        for d in range(ND):
            for h, vals in enumerate(_lookup_all_heads(base + d * TB, tab_ref)):
                tp_ref[h, d] = vals
                if d == 0:
                    tp_ref[h, T_DIAG] = jnp.where(base >= 0, vals, neg)
                if d == WINDOW // TB:
                    tp_ref[h, T_WEND] = jnp.where(base < 0, vals, neg)


def _bias(table_flat, s):
    nr = s // CMP_STRIDE
    rows = min(256, s)
    return pl.pallas_call(
        _bias_kernel,
        grid=(s // rows,),
        in_specs=[pl.BlockSpec(memory_space=pltpu.SMEM)],
        out_specs=[
            pl.BlockSpec((NSA_HEADS, rows, nr), lambda i: (0, i, 0)),
            pl.BlockSpec((NSA_HEADS, NT_ALL, TB, TB), lambda i: (0, 0, 0, 0)),
        ],
        out_shape=[
            jax.ShapeDtypeStruct((NSA_HEADS, s, nr), F32),
            jax.ShapeDtypeStruct((NSA_HEADS, NT_ALL, TB, TB), F32),
        ],
        compiler_params=pltpu.CompilerParams(dimension_semantics=("arbitrary",)),
        name="bias",
    )(table_flat)


def _compress(kv_ref, pos_ref, w1_ref, w2_ref, nr):
    half = CMP_STRIDE * HEAD_DIM
    r = jnp.concatenate(
        [kv_ref[0, pl.ds(m, nr, stride=CMP_STRIDE), :] for m in range(CMP_STRIDE)], axis=1)
    a = _mm((r + pos_ref[0:1, :]).astype(BF16), w1_ref[0:half, :])
    b = _mm((r + pos_ref[1:2, :]).astype(BF16), w1_ref[half:2 * half, :])
    pre = a + pltpu.roll(b, nr - 1, 0)
    h1 = pre * jax.nn.sigmoid(pre)
    return _mm(h1.astype(BF16), w2_ref[...])


def _bias_head(toep_ref, h, d_tiles, window, valid=True):
    sub = TQ // TB
    rows = []
    for ri in range(sub):
        cols = []
        for ci in range(sub):
            d = d_tiles + ri - ci
            idx = jnp.where(d == 0, T_DIAG, jnp.minimum(d, ND - 1))
            if window:
                idx = jnp.where(d == WINDOW // TB, T_WEND, jnp.where(d > WINDOW // TB, T_NONE, idx))
            idx = jnp.where((d < 0) | jnp.logical_not(valid), T_NONE, idx)
            cols.append(toep_ref[h, idx])
        rows.append(jnp.concatenate(cols, axis=1))
    return jnp.concatenate(rows, axis=0)


def _bias_tile(toep_ref, d_tiles, window, valid=True):
    return jnp.concatenate([_bias_head(toep_ref, h, d_tiles, window, valid) for h in range(NSA_GROUP)], axis=0)


def _add_shared(s, mask_add):
    n = s.shape[-1]
    return (s.reshape(NSA_GROUP, TQ, n) + mask_add[None]).reshape(NSA_GROUP * TQ, n)


def _with_ones(v):
    return jnp.concatenate([v, jnp.ones(v.shape, v.dtype)], axis=1)


def _fold_lanes(x, op):
    out = x[:, :LANE]
    for c in range(1, x.shape[1] // LANE):
        out = op(out, x[:, c * LANE:(c + 1) * LANE])
    return out


def _nsa_kernel(q_ref, kc_ref, vc_ref, ks_ref, vs_ref, kw_ref, vw_ref, gate_ref, za_ref,
                bc_ref, toep_ref, posk_ref, w1k_ref, w2k_ref, posv_ref, w1v_ref, w2v_ref,
                o_ref, kcs_ref, vcs_ref, s_ref, macc_ref, acc_ref):
    qi = pl.program_id(2)
    s_len = kc_ref.shape[1]
    nr = s_len // CMP_STRIDE
    nb = s_len // SLC_BLOCK
    n_sel = min(SLC_TOP_N, nb)

    @pl.when(qi == 0)
    def _():
        kcs_ref[...] = _compress(kc_ref, posk_ref, w1k_ref, w2k_ref, nr).astype(BF16)
        vcs_ref[...] = _compress(vc_ref, posv_ref, w1v_ref, w2v_ref, nr).astype(BF16)

    q0 = qi * TQ
    rows = NSA_GROUP * TQ
    q = q_ref[0] * (HEAD_DIM ** -0.5 * LOG2E)
    q4 = jnp.concatenate([q[:, h * HEAD_DIM:(h + 1) * HEAD_DIM] for h in range(NSA_GROUP)],
                         axis=0).astype(BF16)

    sub = TQ // TB
    n_tiles = s_len // TQ

    def compressed_and_selection():
        t_c = q0 + _iota((TQ, nr), 0)
        i_c = _iota((TQ, nr), 1)
        mask_c = (t_c - (i_c * CMP_STRIDE + (CMP_BLOCK - 1)) >= 0) & (i_c < nr - 1)
        qk = _mm_nt(q4, kcs_ref[...])
        mask_add = jnp.where(mask_c, 0.0, NEG)
        yield
        p_heads = []
        for h in range(NSA_GROUP):
            lg = qk[h * TQ:(h + 1) * TQ] + bc_ref[h] + mask_add
            m_c = jnp.max(lg, axis=-1, keepdims=True)
            yield
            e = jnp.where(lg > 0.5 * NEG, jnp.exp2(lg - m_c), 0.0)
            l_c = jnp.sum(e, axis=-1, keepdims=True)
            yield
            p_heads.append(e / jnp.maximum(l_c, 1e-30))
        p = jnp.concatenate(p_heads, axis=0)
        o_c = _mm(p.astype(BF16), vcs_ref[...])
        psum = p_heads[0]
        for h in range(1, NSA_GROUP):
            psum = psum + p_heads[h]
        ov_i = _iota((nb, nr), 1) * CMP_STRIDE
        ov_j = _iota((nb, nr), 0) * SLC_BLOCK
        ov_t = ((ov_i < ov_j + SLC_BLOCK) & (ov_i + CMP_BLOCK > ov_j)).astype(BF16)
        p1, p2, p3 = _split3(psum)
        imp_t = _mm_nt(ov_t, p1) + _mm_nt(ov_t, p2) + _mm_nt(ov_t, p3)
        yield
        jb = _iota((nb, TQ), 0)
        cur = _div_pow2(q0 + _iota((nb, TQ), 1), SLC_BLOCK)
        forced = (jb == 0) | (jb == cur) | (jb == cur - 1)
        causal = jb <= cur
        score = jnp.where(forced, jnp.inf, jnp.where(causal, imp_t, -jnp.inf))
        rank = jnp.zeros((nb, TQ), jnp.int32)
        for jp in range(nb):
            sj = score[jp:jp + 1, :]
            beats = (sj > score) | ((sj == score) & (jb > jp))
            rank = rank + beats.astype(jnp.int32)
            if jp % 8 == 7:
                yield
        sel_t = ((rank < n_sel) & causal).astype(BF16)
        place = (_iota((nb, LANE), 0) == _iota((nb, LANE), 1)).astype(BF16)
        unsel = _mm_tn(sel_t, place) - (_iota((TQ, LANE), 1) < nb).astype(F32)
        yield
        return o_c, unsel.astype(BF16)

    def window():
        n_band = WINDOW // TQ + 1
        qk, tiles, v_w = [], [], []
        for c in range(n_band):
            j = qi - (n_band - 1) + c
            jc = jnp.maximum(j, 0)
            k0 = pl.multiple_of(jc * TQ, TQ)
            tiles.append((j, jc))
            qk.append(_mm_nt(q4, kw_ref[0, pl.ds(k0, TQ), :].astype(BF16)))
            v_w.append(_with_ones(vw_ref[0, pl.ds(k0, TQ), :].astype(BF16)))
        yield
        heads = range(NSA_GROUP)
        s_w = [[None] * NSA_GROUP for _ in range(n_band)]
        m_w = [None] * NSA_GROUP
        for c in range(n_band):
            for h in heads:
                s = qk[c][h * TQ:(h + 1) * TQ] + _bias_head(toep_ref, h, (qi - tiles[c][1]) * sub, True,
                                                             tiles[c][0] >= 0)
                s_w[c][h] = s
                m_w[h] = s if c == 0 else jnp.maximum(m_w[h], s)
                yield
        for h in heads:
            m_w[h] = jnp.max(m_w[h], axis=-1, keepdims=True)
        yield
        v_band = jnp.concatenate(v_w, axis=0)
        acc_w = []
        for h in heads:
            p_h = []
            for c in range(n_band):
                p_h.append(jnp.exp2(s_w[c][h] - m_w[h]).astype(BF16))
                yield
            acc_w.append(_mm(jnp.concatenate(p_h, axis=1), v_band))
            yield
        return jnp.concatenate([a[:, :HEAD_DIM] / jnp.maximum(a[:, HEAD_DIM:], 1e-30) for a in acc_w], axis=0)

    def gates():
        gts = jax.nn.sigmoid(gate_ref[0])
        n_out = 3 * NSA_GROUP * LANE
        blk = _div_pow2(_iota((LANE, n_out), 1), LANE)
        src = _div_pow2(blk, NSA_GROUP) * NSA_HEADS + pl.program_id(1) * NSA_GROUP + _mod_pow2(blk, NSA_GROUP)
        pick = (_iota((LANE, n_out), 0) == src).astype(BF16)
        g_hi = gts.astype(BF16)
        g_lo = (gts - g_hi.astype(F32)).astype(BF16)
        rep = _mm(g_hi, pick) + _mm(g_lo, pick)
        yield
        return rep

    (o_c, unsel), o_w, gate_rep = _interleave([compressed_and_selection(), window(), gates()])

    q_sel = jnp.concatenate([q4, jnp.concatenate([unsel] * NSA_GROUP, axis=0)], axis=1)

    def key_tile(j):
        k0 = pl.multiple_of(j * TQ, TQ)
        blk = _div_pow2(j * TQ + _iota((TQ, LANE), 0), SLC_BLOCK)
        marks = jnp.where(_iota((TQ, LANE), 1) == blk, BIG, 0.0).astype(BF16)
        return jnp.concatenate([ks_ref[0, pl.ds(k0, TQ), :].astype(BF16), marks], axis=1)

    assert n_tiles % 4 == 0
    n_pairs = _div_pow2(qi + 2, 2)
    passes, first = [], 0
    for tiles_per_iter in (8, 4, 2):
        if tiles_per_iter <= n_tiles:
            trips = _div_pow2(2 * n_pairs - first, tiles_per_iter)
            passes.append((tiles_per_iter, first, trips))
            first = first + tiles_per_iter * trips

    def slc_logits(tpi, base):
        def body(it, carry):
            tiles = [base + tpi * it + c for c in range(tpi)]
            qk = [_mm_nt(q_sel, key_tile(j)) for j in tiles]
            macc = macc_ref[...]
            for j, qk_j in zip(tiles, qk):
                s = qk_j + _bias_tile(toep_ref, (qi - j) * sub, False)
                s_ref[j] = s
                macc = jnp.maximum(macc, _fold_lanes(s, jnp.maximum))
            macc_ref[...] = macc
            return carry
        return body

    macc_ref[...] = jnp.full((rows, LANE), NEG, F32)
    for tpi, base, trips in passes:
        lax.fori_loop(0, trips, slc_logits(tpi, base), 0)
    m_s = jnp.max(macc_ref[...], axis=-1, keepdims=True)

    def slc_values(tpi, base):
        def body(it, carry):
            j0 = base + tpi * it
            k0 = pl.multiple_of(j0 * TQ, 2 * TQ)
            p_it = jnp.concatenate([jnp.exp2(s_ref[j0 + c] - m_s).astype(BF16) for c in range(tpi)], axis=1)
            acc_ref[...] += _mm(p_it, _with_ones(vs_ref[0, pl.ds(k0, tpi * TQ), :].astype(BF16)))
            return carry
        return body

    acc_ref[...] = jnp.zeros((rows, 2 * HEAD_DIM), F32)
    for tpi, base, trips in passes:
        lax.fori_loop(0, trips, slc_values(tpi, base), 0)
    acc = acc_ref[...]
    o_s = acc[:, :HEAD_DIM] / jnp.maximum(acc[:, HEAD_DIM:], 1e-30)

    za = za_ref[0]

    def gate(branch, h):
        k = branch * NSA_GROUP + h
        return gate_rep[:, k * LANE:(k + 1) * LANE]

    assert HEAD_DIM == LANE
    for h in range(NSA_GROUP):
        hr = slice(h * TQ, (h + 1) * TQ)
        o = gate(0, h) * o_c[hr] + gate(1, h) * o_s[hr] + gate(2, h) * o_w[hr]
        z = za[:, h * HEAD_DIM:(h + 1) * HEAD_DIM]
        o_ref[0, :, h * HEAD_DIM:(h + 1) * HEAD_DIM] = (o * (z * jax.nn.sigmoid(z))).astype(o_ref.dtype)


def _nsa(proj3, bias_c, toep, posk, w1k, w2k, posv, w1v, w2v):
    b, s, _ = proj3.shape
    nr = s // CMP_STRIDE
    gq = NSA_GROUP * HEAD_DIM

    def kvspec(idx):
        return pl.BlockSpec((1, s, HEAD_DIM), lambda bi, g, qi, idx=idx: (bi, 0, P_KV // HEAD_DIM + 2 * idx + g))

    def whole(a):
        return pl.BlockSpec(a.shape, lambda bi, g, qi, nd=a.ndim: (0,) * nd)

    in_specs = [
        pl.BlockSpec((1, TQ, gq), lambda bi, g, qi: (bi, qi, P_Q // gq + g)),
        kvspec(0), kvspec(1), kvspec(2), kvspec(3), kvspec(4), kvspec(5),
        pl.BlockSpec((1, TQ, LANE), lambda bi, g, qi: (bi, qi, P_GATE // LANE)),
        pl.BlockSpec((1, TQ, gq), lambda bi, g, qi: (bi, qi, P_ZA // gq + g)),
        pl.BlockSpec((NSA_GROUP, TQ, nr), lambda bi, g, qi: (g, qi, 0)),
        pl.BlockSpec((NSA_GROUP, NT_ALL, TB, TB), lambda bi, g, qi: (g, 0, 0, 0)),
        whole(posk), whole(w1k), whole(w2k), whole(posv), whole(w1v), whole(w2v),
    ]
    return pl.pallas_call(
        _nsa_kernel,
        grid=(b, NSA_KV_HEADS, s // TQ),
        in_specs=in_specs,
        out_specs=pl.BlockSpec((1, TQ, gq), lambda bi, g, qi: (bi, qi, g)),
        out_shape=jax.ShapeDtypeStruct((b, s, NSA_WIDTH), BF16),
        scratch_shapes=[pltpu.VMEM((nr, HEAD_DIM), BF16), pltpu.VMEM((nr, HEAD_DIM), BF16),
                        pltpu.VMEM((s // TQ, NSA_GROUP * TQ, TQ), F32),
                        pltpu.VMEM((NSA_GROUP * TQ, LANE), F32),
                        pltpu.VMEM((NSA_GROUP * TQ, 2 * HEAD_DIM), F32)],
        compiler_params=pltpu.CompilerParams(
            dimension_semantics=("parallel", "parallel", "arbitrary"),
            vmem_limit_bytes=56 * 1024 * 1024),
        name="nsa",
    )(proj3, proj3, proj3, proj3, proj3, proj3, proj3, proj3, proj3,
      bias_c, toep, posk, w1k, w2k, posv, w1v, w2v)


def _shift_mix(ref, prev_ref, mu, sl):
    x = ref[0, :, sl]
    prev = jnp.where(_iota(x.shape, 0) == 0, prev_ref[:, sl], pltpu.roll(x, 1, 0))
    prev_ref[:, sl] = x[x.shape[0] - 1:]
    return x + mu * (prev - x)


def _rwkv_kernel(r_ref, k_ref, v_ref, wa_ref, zb_ref, vec_ref, muwa_ref, w2_ref, a2_ref,
                 o_ref, st_ref, pr_ref, pk_ref, pv_ref, pwa_ref):
    first = pl.program_id(2) == 0
    tb = r_ref.shape[1]
    n_groups = r_ref.shape[2] // GW
    n_chunks = tb // CH

    @pl.when(first)
    def _():
        for ref in (st_ref, pr_ref, pk_ref, pv_ref, pwa_ref):
            ref[...] = jnp.zeros_like(ref)

    wa = _shift_mix(wa_ref, pwa_ref, muwa_ref[...], slice(0, LANE))
    wd_act = jnp.tanh(wa[:, :LORA]).astype(BF16)
    ad = wa[:, LORA:].astype(BF16)

    seg = (_div_pow2(_iota((GW, GW), 0), RWKV_HEAD_DIM) == _div_pow2(_iota((GW, GW), 1), RWKV_HEAD_DIM))
    segf = seg.astype(F32)
    segb = seg.astype(BF16)
    assert CH == RWKV_HEAD_DIM
    lane_s = _mod_pow2(_iota((CH, GW), 1), CH)
    row_t = _iota((CH, GW), 0)
    strict = lane_s < row_t
    incl = lane_s <= row_t
    eye = (lane_s == row_t).astype(F32)
    ti_r, ti_c = _iota((tb, tb), 0), _iota((tb, tb), 1)
    trib = ((ti_c <= ti_r) & (_div_pow2(ti_c, CH) == _div_pow2(ti_r, CH))).astype(BF16)

    def bd(x):
        xb = x.astype(BF16)
        return jnp.concatenate([xb] * HPG, axis=0) * segb

    def prep(gi):
        lanes = slice(gi * GW, (gi + 1) * GW)
        vec = vec_ref[:, lanes]
        mu_r, mu_k, mu_v = vec[0:1], vec[1:2], vec[2:3]
        w0, a0, k_k, k_a = vec[3:4], vec[4:5], vec[5:6], vec[6:7]
        r = _shift_mix(r_ref, pr_ref, mu_r, lanes)
        k = _shift_mix(k_ref, pk_ref, mu_k, lanes)
        v = _shift_mix(v_ref, pv_ref, mu_v, lanes)
        w_lora = _mm(wd_act, w2_ref[:, lanes])
        a_lora = _mm(ad, a2_ref[:, lanes])
        kk = k * k_k
        kk_ss = _mm((kk * kk).astype(BF16), segb)
        yield
        lw = jax.nn.sigmoid(w0 + w_lora) * (-math.exp(-0.5) * LOG2E)
        cum = _mm_split_rhs(trib, lw)
        yield
        a_sig = jax.nn.sigmoid(a0 + a_lora)
        kk = kk * lax.rsqrt(jnp.maximum(kk_ss, 1e-24))
        k = k * (1.0 + (a_sig - 1.0) * k_a)
        return dict(r=r, k=k, v=v, a=-kk, b=kk * a_sig, lw=lw, cum=cum, vec=vec, lanes=lanes)

    groups = _interleave([prep(gi) for gi in range(n_groups)])

    def chunk_local(g, c):
        ts = slice(c * CH, (c + 1) * CH)
        rc, kc, vc, ac, bc, lwc, cum = (g[n][ts] for n in ("r", "k", "v", "a", "b", "lw", "cum"))
        tot = cum[CH - 1:CH]
        e_out = jnp.exp2(-cum)
        e_end = jnp.exp2(tot - cum)
        r_t = rc * jnp.exp2(cum)
        a_t = ac * jnp.exp2(cum - lwc)
        lhs = jnp.concatenate([a_t, r_t], axis=0).astype(BF16)
        aa = _mm_nt(lhs, jnp.concatenate([bd(bc * e_out), bd(kc * e_out)], axis=0))
        yield
        a_ab = jnp.where(strict, aa[:CH, :GW], 0.0)
        a_ak = jnp.where(strict, aa[:CH, GW:], 0.0)
        a_rb = jnp.where(incl, aa[CH:, :GW], 0.0)
        a_rk = jnp.where(incl, aa[CH:, GW:], 0.0)
        t_inv = eye + a_ab
        mpow = _mm(a_ab.astype(BF16), bd(a_ab))
        av = _mm(a_ak.astype(BF16), bd(vc))
        yield
        for _ in range(int(math.log2(CH)) - 1):
            res = _mm(jnp.concatenate([t_inv, mpow], axis=0).astype(BF16), bd(mpow))
            yield
            t_inv = t_inv + res[:CH]
            mpow = res[CH:]
        wu = _mm(t_inv.astype(BF16), jnp.concatenate([bd(a_t), bd(av)], axis=1))
        yield
        return dict(
            lhs=jnp.concatenate([wu[:, :GW], r_t], axis=0).astype(BF16), u_loc=wu[:, GW:],
            a_r=jnp.concatenate([a_rb, a_rk], axis=1).astype(BF16), bdv=bd(vc), vc=vc,
            bk_end=jnp.concatenate([bc * e_end, kc * e_end], axis=0).astype(BF16), dec=jnp.exp2(tot))

    loc = _interleave([chunk_local(g, c) for g in groups for c in range(n_chunks)])

    def chain(gi):
        ys = []
        g_state = st_ref[gi]
        for c in range(n_chunks):
            lc = loc[gi * n_chunks + c]
            x0 = _mm_nt(lc["lhs"], g_state.astype(BF16))
            yield
            u = x0[:CH] + lc["u_loc"]
            y_c = _mm(lc["a_r"], jnp.concatenate([bd(u), lc["bdv"]], axis=0))
            upd = _mm_tn(jnp.concatenate([u, lc["vc"]], axis=0).astype(BF16), lc["bk_end"])
            yield
            ys.append(x0[CH:] + y_c)
            g_state = g_state * lc["dec"] + upd * segf
        st_ref[gi] = g_state
        return jnp.concatenate(ys, axis=0)

    ys = _interleave([chain(gi) for gi in range(n_groups)])

    def finish(g, y):
        vec = g["vec"]
        ln_w, ln_b, r_k = vec[7:8], vec[8:9], vec[9:10]
        inv_n = 1.0 / RWKV_HEAD_DIM
        mean = _mm(y.astype(BF16), segb) * inv_n
        bonus = _mm((g["r"] * g["k"] * r_k).astype(BF16), segb) * g["v"]
        yield
        yc = y - mean
        var = _mm((yc * yc).astype(BF16), segb) * inv_n
        yield
        yn = yc * lax.rsqrt(var + RWKV_GN_EPS) * ln_w + ln_b
        zb = zb_ref[0, :, g["lanes"]]
        o_ref[0, :, g["lanes"]] = ((yn + bonus) * (zb * jax.nn.sigmoid(zb))).astype(o_ref.dtype)

    _interleave([finish(g, y) for g, y in zip(groups, ys)])


def _rwkv(proj3, vecs, mu_wa, w2, a2, tb, gps):
    b, s, _ = proj3.shape
    gw = gps * GW
    ng = RWKV_WIDTH // gw

    def col(off):
        return pl.BlockSpec((1, tb, gw), lambda bi, g, ti, off=off: (bi, ti, off // gw + g))

    in_specs = [
        col(P_RKV), col(P_RKV + RWKV_WIDTH), col(P_RKV + 2 * RWKV_WIDTH),
        pl.BlockSpec((1, tb, LANE), lambda bi, g, ti: (bi, ti, P_WDAD // LANE)),
        col(P_ZB),
        pl.BlockSpec((vecs.shape[0], gw), lambda bi, g, ti: (0, g)),
        pl.BlockSpec((1, LANE), lambda bi, g, ti: (0, 0)),
        pl.BlockSpec((LORA, gw), lambda bi, g, ti: (0, g)),
        pl.BlockSpec((LORA, gw), lambda bi, g, ti: (0, g)),
    ]
    return pl.pallas_call(
        _rwkv_kernel,
        grid=(b, ng, s // tb),
        in_specs=in_specs,
        out_specs=pl.BlockSpec((1, tb, gw), lambda bi, g, ti: (bi, ti, g)),
        out_shape=jax.ShapeDtypeStruct((b, s, RWKV_WIDTH), BF16),
        scratch_shapes=[pltpu.VMEM((gps, GW, GW), F32), pltpu.VMEM((1, gw), F32), pltpu.VMEM((1, gw), F32),
                        pltpu.VMEM((1, gw), F32), pltpu.VMEM((1, LANE), F32)],
        compiler_params=pltpu.CompilerParams(
            dimension_semantics=("parallel", "parallel", "arbitrary")),
        name="rwkv",
    )(proj3, proj3, proj3, proj3, proj3, vecs, mu_wa, w2, a2)


def _outproj_kernel(ma_ref, mb_ref, wa_ref, wb_ref, x_ref, g_ref, o_ref):
    y = _mm(ma_ref[...], wa_ref[...]) + _mm(mb_ref[...], wb_ref[...])
    ms = jnp.mean(y * y, axis=-1, keepdims=True)
    o_ref[...] = x_ref[...] + y * lax.rsqrt(ms + NORM_EPS) * g_ref[...]


def _outproj(mix_a, mix_b, w_a, w_b, x2, g, tm):
    m, d = x2.shape
    ka, kb = mix_a.shape[1], mix_b.shape[1]
    return pl.pallas_call(
        _outproj_kernel,
        grid=(m // tm,),
        in_specs=[
            pl.BlockSpec((tm, ka), lambda i: (i, 0)),
            pl.BlockSpec((tm, kb), lambda i: (i, 0)),
            pl.BlockSpec((ka, d), lambda i: (0, 0)),
            pl.BlockSpec((kb, d), lambda i: (0, 0)),
            pl.BlockSpec((tm, d), lambda i: (i, 0)),
            pl.BlockSpec((1, d), lambda i: (0, 0)),
        ],
        out_specs=pl.BlockSpec((tm, d), lambda i: (i, 0)),
        out_shape=jax.ShapeDtypeStruct((m, d), F32),
        compiler_params=pltpu.CompilerParams(
            dimension_semantics=("parallel",), vmem_limit_bytes=56 * 1024 * 1024),
        name="outproj",
    )(mix_a, mix_b, w_a, w_b, x2, g)


_W_SEGMENTS = (
    (P_Q, R_Q, NSA_WIDTH),
    (P_RKV, R_FEAT, 3 * RWKV_WIDTH),
    (P_ZB, R_ZB, RWKV_WIDTH),
    (P_ZA, R_ZA, NSA_WIDTH),
    (P_KV, R_KV, 6 * NSA_KV_HEADS * HEAD_DIM),
    (P_WDAD, R_FEAT + 3 * RWKV_WIDTH, 2 * LORA),
    (P_GATE, R_GATE, LANE),
)


def _relayout_kernel(w_ref, o_ref):
    for dst, src, width in _W_SEGMENTS:
        o_ref[:, dst:dst + width] = w_ref[:, src:src + width].astype(BF16)
    used = P_GATE + LANE
    o_ref[:, used:] = jnp.zeros((o_ref.shape[0], NP - used), BF16)


def _permute_w_in(w, rows):
    d, n = w.shape
    return pl.pallas_call(
        _relayout_kernel,
        grid=(d // rows,),
        in_specs=[pl.BlockSpec((rows, n), lambda i: (i, 0))],
        out_specs=pl.BlockSpec((rows, NP), lambda i: (i, 0)),
        out_shape=jax.ShapeDtypeStruct((d, NP), BF16),
        compiler_params=pltpu.CompilerParams(dimension_semantics=("parallel",)),
        name="relayout",
    )(w)


def _block(x, pre_norm_g, w_in, rel_bias_table, cmp_pos_k, cmp_pos_v, cmp_k_w1, cmp_k_w2, cmp_v_w1,
           cmp_v_w2, rwkv_mu, rwkv_w0, rwkv_w2, rwkv_a0, rwkv_a2, rwkv_k_k, rwkv_k_a, rwkv_r_k,
           rwkv_ln_w, rwkv_ln_b, w_out, post_norm_g):
    b, s, d = x.shape
    x2 = x.reshape(b * s, d)
    tm = min(1024, b * s)
    proj = _inproj(x2, pre_norm_g.reshape(1, d), _permute_w_in(w_in.astype(BF16), 256), tm, 1024)
    proj3 = proj.reshape(b, s, NP)

    bias_c, toep = _bias(rel_bias_table.reshape(-1), s)
    half = CMP_STRIDE * HEAD_DIM
    mix_a = _nsa(proj3, bias_c, toep,
                 cmp_pos_k.reshape(2, half), cmp_k_w1.astype(BF16), cmp_k_w2.astype(BF16),
                 cmp_pos_v.reshape(2, half), cmp_v_w1.astype(BF16), cmp_v_w2.astype(BF16))

    w3 = 3 * RWKV_WIDTH
    vec_rows = [rwkv_mu[:RWKV_WIDTH], rwkv_mu[RWKV_WIDTH:2 * RWKV_WIDTH], rwkv_mu[2 * RWKV_WIDTH:w3],
                rwkv_w0, rwkv_a0, rwkv_k_k, rwkv_k_a, rwkv_ln_w, rwkv_ln_b, rwkv_r_k.reshape(-1)]
    vecs = jnp.stack(vec_rows + [jnp.zeros_like(rwkv_w0)] * (16 - len(vec_rows)), axis=0)
    mix_b = _rwkv(proj3, vecs, rwkv_mu[w3:].reshape(1, 2 * LORA), rwkv_w2.astype(BF16),
                  rwkv_a2.astype(BF16), min(256, s), RWKV_GROUPS_PER_STEP)

    w_o = w_out.astype(BF16)
    out = _outproj(mix_a.reshape(b * s, NSA_WIDTH), mix_b.reshape(b * s, RWKV_WIDTH),
                   w_o[:NSA_WIDTH], w_o[NSA_WIDTH:], x2, post_norm_g.reshape(1, d), min(512, b * s))
    return out.reshape(b, s, d)


def kernel(x, pre_norm_g, w_in, rel_bias_table, cmp_pos_k, cmp_pos_v, cmp_k_w1, cmp_k_w2, cmp_v_w1,
           cmp_v_w2, rwkv_mu, rwkv_w0, rwkv_w2, rwkv_a0, rwkv_a2, rwkv_k_k, rwkv_k_a, rwkv_r_k,
           rwkv_ln_w, rwkv_ln_b, w_out, post_norm_g):
    h = x
    for l in range(pre_norm_g.shape[0]):
        h = _block(h, pre_norm_g[l], w_in[l], rel_bias_table, cmp_pos_k[l], cmp_pos_v[l], cmp_k_w1[l],
                   cmp_k_w2[l], cmp_v_w1[l], cmp_v_w2[l], rwkv_mu[l], rwkv_w0[l], rwkv_w2[l],
                   rwkv_a0[l], rwkv_a2[l], rwkv_k_k[l], rwkv_k_a[l], rwkv_r_k[l], rwkv_ln_w[l],
                   rwkv_ln_b[l], w_out[l], post_norm_g[l])
    return h
```

```python
import math

import jax
import jax.numpy as jnp
from jax import lax
from jax.experimental import pallas as pl
from jax.experimental.pallas import tpu as pltpu

F32 = jnp.float32
BF16 = jnp.bfloat16

D_MODEL = 2048
NSA_HEADS = 8
NSA_KV_HEADS = 2
NSA_GROUP = NSA_HEADS // NSA_KV_HEADS
HEAD_DIM = 128
NSA_WIDTH = NSA_HEADS * HEAD_DIM
CMP_BLOCK = 32
CMP_STRIDE = 16
SLC_BLOCK = 64
SLC_TOP_N = 16
WINDOW = 512
RWKV_WIDTH = 1024
RWKV_HEAD_DIM = 64
RWKV_HEADS = RWKV_WIDTH // RWKV_HEAD_DIM
LORA = 64
NUM_BUCKETS = 32
MAX_DISTANCE = 1024
NORM_EPS = 1e-6
RWKV_GN_EPS = 64e-5

R_Q = 0
R_KV = R_Q + NSA_WIDTH
R_GATE = R_KV + 6 * NSA_KV_HEADS * HEAD_DIM
R_ZA = R_GATE + 3 * NSA_HEADS
R_FEAT = R_ZA + NSA_WIDTH
R_ZB = R_FEAT + 3 * RWKV_WIDTH + 2 * LORA
R_END = R_ZB + RWKV_WIDTH

P_Q = 0
P_RKV = 1024
P_ZB = 4096
P_ZA = 5120
P_KV = 6144
P_WDAD = 7680
P_GATE = 7808
NP = 8192

LANE = 128
TQ = 256
TB = 128
ND = 9
T_DIAG = ND
T_WEND = ND + 1
T_NONE = ND + 2
NT_ALL = ND + 3
LOG2E = math.log2(math.e)
BIG = 2.0 ** 100
CH = 64
HPG = 4
GW = HPG * RWKV_HEAD_DIM
RWKV_GROUPS_PER_STEP = 4
NEG = -1e30


def _bucket_thresholds():
    out = []
    for k in range(1, NUM_BUCKETS // 2):
        n = 16
        while n ** 8 < (16 ** 8) * (2 ** (3 * k)):
            n += 1
        out.append(n)
    return out


_THR = _bucket_thresholds()


def _mm(a, b):
    return jnp.dot(a, b, preferred_element_type=F32)


def _mm_nt(a, b):
    return lax.dot_general(a, b, (((1,), (1,)), ((), ())), preferred_element_type=F32)


def _mm_tn(a, b):
    return lax.dot_general(a, b, (((0,), (0,)), ((), ())), preferred_element_type=F32)


def _split3(x):
    x1 = x.astype(BF16)
    r1 = x - x1.astype(F32)
    x2 = r1.astype(BF16)
    x3 = (r1 - x2.astype(F32)).astype(BF16)
    return x1, x2, x3


def _mm_split_rhs(a_exact, b):
    b1 = b.astype(BF16)
    b2 = (b - b1.astype(F32)).astype(BF16)
    return _mm(a_exact, b1) + _mm(a_exact, b2)


def _iota(shape, dim):
    return lax.broadcasted_iota(jnp.int32, shape, dim)


def _interleave(gens):
    results = [None] * len(gens)
    live = list(enumerate(gens))
    while live:
        still = []
        for i, g in live:
            try:
                next(g)
                still.append((i, g))
            except StopIteration as stop:
                results[i] = stop.value
        live = still
    return results


def _div_pow2(x, n):
    assert n & (n - 1) == 0
    return x >> (n.bit_length() - 1)


def _mod_pow2(x, n):
    assert n & (n - 1) == 0
    return x & (n - 1)


def _inproj_kernel(x_ref, g_ref, w_ref, o_ref, hn_ref):
    @pl.when(pl.program_id(1) == 0)
    def _():
        x = x_ref[...]
        ms = jnp.mean(x * x, axis=-1, keepdims=True)
        hn_ref[...] = (x * lax.rsqrt(ms + NORM_EPS) * g_ref[...]).astype(BF16)

    o_ref[...] = _mm(hn_ref[...], w_ref[...])


def _inproj(x2, g, w, tm, tn):
    m, d = x2.shape
    n = w.shape[1]
    return pl.pallas_call(
        _inproj_kernel,
        grid=(m // tm, n // tn),
        in_specs=[
            pl.BlockSpec((tm, d), lambda i, j: (i, 0)),
            pl.BlockSpec((1, d), lambda i, j: (0, 0)),
            pl.BlockSpec((d, tn), lambda i, j: (0, j)),
        ],
        out_specs=pl.BlockSpec((tm, tn), lambda i, j: (i, j)),
        out_shape=jax.ShapeDtypeStruct((m, n), F32),
        scratch_shapes=[pltpu.VMEM((tm, d), BF16)],
        compiler_params=pltpu.CompilerParams(
            dimension_semantics=("parallel", "arbitrary"),
            vmem_limit_bytes=56 * 1024 * 1024),
        name="inproj",
    )(x2, g, w)


def _bucket(n):
    n = jnp.maximum(n, 0)
    large = jnp.full(n.shape, NUM_BUCKETS // 2, jnp.int32)
    for thr in _THR:
        large = large + (n >= thr).astype(jnp.int32)
    return jnp.where(n < NUM_BUCKETS // 2, n, large)


def _lookup_all_heads(dist, tab_ref):
    bucket = _bucket(dist)
    hits = [bucket == b for b in range(NUM_BUCKETS)]
    outs = []
    for h in range(NSA_HEADS):
        out = jnp.zeros(dist.shape, F32)
        for b in range(NUM_BUCKETS):
            out = jnp.where(hits[b], tab_ref[b * NSA_HEADS + h] * LOG2E, out)
        outs.append(out)
    return outs


def _bias_kernel(tab_ref, bc_ref, tp_ref):
    i = pl.program_id(0)
    rows, nr = bc_ref.shape[1], bc_ref.shape[2]
    dist_c = (i * rows + _iota((rows, nr), 0)) - (_iota((rows, nr), 1) * CMP_STRIDE + (CMP_BLOCK - 1))
    for h, vals in enumerate(_lookup_all_heads(dist_c, tab_ref)):
        bc_ref[h] = vals

    @pl.when(i == 0)
    def _():
        base = _iota((TB, TB), 0) - _iota((TB, TB), 1)
        neg = jnp.full((TB, TB), NEG, F32)
        for h in range(NSA_HEADS):
            tp_ref[h, T_NONE] = neg
        for d in range(ND):
            for h, vals in enumerate(_lookup_all_heads(base + d * TB, tab_ref)):
                tp_ref[h, d] = vals
                if d == 0:
                    tp_ref[h, T_DIAG] = jnp.where(base >= 0, vals, neg)
                if d == WINDOW // TB:
                    tp_ref[h, T_WEND] = jnp.where(base < 0, vals, neg)


def _bias(table_flat, s):
    nr = s // CMP_STRIDE
    rows = min(256, s)
    return pl.pallas_call(
        _bias_kernel,
        grid=(s // rows,),
        in_specs=[pl.BlockSpec(memory_space=pltpu.SMEM)],
        out_specs=[
            pl.BlockSpec((NSA_HEADS, rows, nr), lambda i: (0, i, 0)),
            pl.BlockSpec((NSA_HEADS, NT_ALL, TB, TB), lambda i: (0, 0, 0, 0)),
        ],
        out_shape=[
            jax.ShapeDtypeStruct((NSA_HEADS, s, nr), F32),
            jax.ShapeDtypeStruct((NSA_HEADS, NT_ALL, TB, TB), F32),
        ],
        compiler_params=pltpu.CompilerParams(dimension_semantics=("arbitrary",)),
        name="bias",
    )(table_flat)


def _compress(kv_ref, pos_ref, w1_ref, w2_ref, nr):
    half = CMP_STRIDE * HEAD_DIM
    r = jnp.concatenate(
        [kv_ref[0, pl.ds(m, nr, stride=CMP_STRIDE), :] for m in range(CMP_STRIDE)], axis=1)
    a = _mm((r + pos_ref[0:1, :]).astype(BF16), w1_ref[0:half, :])
    b = _mm((r + pos_ref[1:2, :]).astype(BF16), w1_ref[half:2 * half, :])
    pre = a + pltpu.roll(b, nr - 1, 0)
    h1 = pre * jax.nn.sigmoid(pre)
    return _mm(h1.astype(BF16), w2_ref[...])


def _bias_head(toep_ref, h, d_tiles, window, valid=True):
    sub = TQ // TB
    rows = []
    for ri in range(sub):
        cols = []
        for ci in range(sub):
            d = d_tiles + ri - ci
            idx = jnp.where(d == 0, T_DIAG, jnp.minimum(d, ND - 1))
            if window:
                idx = jnp.where(d == WINDOW // TB, T_WEND, jnp.where(d > WINDOW // TB, T_NONE, idx))
            idx = jnp.where((d < 0) | jnp.logical_not(valid), T_NONE, idx)
            cols.append(toep_ref[h, idx])
        rows.append(jnp.concatenate(cols, axis=1))
    return jnp.concatenate(rows, axis=0)


def _bias_tile(toep_ref, d_tiles, window, valid=True):
    return jnp.concatenate([_bias_head(toep_ref, h, d_tiles, window, valid) for h in range(NSA_GROUP)], axis=0)


def _add_shared(s, mask_add):
    n = s.shape[-1]
    return (s.reshape(NSA_GROUP, TQ, n) + mask_add[None]).reshape(NSA_GROUP * TQ, n)


def _with_ones(v):
    return jnp.concatenate([v, jnp.ones(v.shape, v.dtype)], axis=1)


def _fold_lanes(x, op):
    out = x[:, :LANE]
    for c in range(1, x.shape[1] // LANE):
        out = op(out, x[:, c * LANE:(c + 1) * LANE])
    return out


def _nsa_kernel(q_ref, kc_ref, vc_ref, ks_ref, vs_ref, kw_ref, vw_ref, gate_ref, za_ref,
                bc_ref, toep_ref, posk_ref, w1k_ref, w2k_ref, posv_ref, w1v_ref, w2v_ref,
                o_ref, kcs_ref, vcs_ref, s_ref, macc_ref, acc_ref):
    qi = pl.program_id(2)
    s_len = kc_ref.shape[1]
    nr = s_len // CMP_STRIDE
    nb = s_len // SLC_BLOCK
    n_sel = min(SLC_TOP_N, nb)

    @pl.when(qi == 0)
    def _():
        kcs_ref[...] = _compress(kc_ref, posk_ref, w1k_ref, w2k_ref, nr).astype(BF16)
        vcs_ref[...] = _compress(vc_ref, posv_ref, w1v_ref, w2v_ref, nr).astype(BF16)

    q0 = qi * TQ
    rows = NSA_GROUP * TQ
    q = q_ref[0] * (HEAD_DIM ** -0.5 * LOG2E)
    q4 = jnp.concatenate([q[:, h * HEAD_DIM:(h + 1) * HEAD_DIM] for h in range(NSA_GROUP)],
                         axis=0).astype(BF16)

    sub = TQ // TB
    n_tiles = s_len // TQ

    def compressed_and_selection():
        t_c = q0 + _iota((TQ, nr), 0)
        i_c = _iota((TQ, nr), 1)
        mask_c = (t_c - (i_c * CMP_STRIDE + (CMP_BLOCK - 1)) >= 0) & (i_c < nr - 1)
        qk = _mm_nt(q4, kcs_ref[...])
        mask_add = jnp.where(mask_c, 0.0, NEG)
        yield
        p_heads = []
        for h in range(NSA_GROUP):
            lg = qk[h * TQ:(h + 1) * TQ] + bc_ref[h] + mask_add
            m_c = jnp.max(lg, axis=-1, keepdims=True)
            yield
            e = jnp.where(lg > 0.5 * NEG, jnp.exp2(lg - m_c), 0.0)
            l_c = jnp.sum(e, axis=-1, keepdims=True)
            yield
            p_heads.append(e / jnp.maximum(l_c, 1e-30))
        p = jnp.concatenate(p_heads, axis=0)
        o_c = _mm(p.astype(BF16), vcs_ref[...])
        psum = p_heads[0]
        for h in range(1, NSA_GROUP):
            psum = psum + p_heads[h]
        ov_i = _iota((nb, nr), 1) * CMP_STRIDE
        ov_j = _iota((nb, nr), 0) * SLC_BLOCK
        ov_t = ((ov_i < ov_j + SLC_BLOCK) & (ov_i + CMP_BLOCK > ov_j)).astype(BF16)
        p1, p2, p3 = _split3(psum)
        imp_t = _mm_nt(ov_t, p1) + _mm_nt(ov_t, p2) + _mm_nt(ov_t, p3)
        yield
        jb = _iota((nb, TQ), 0)
        cur = _div_pow2(q0 + _iota((nb, TQ), 1), SLC_BLOCK)
        forced = (jb == 0) | (jb == cur) | (jb == cur - 1)
        causal = jb <= cur
        score = jnp.where(forced, jnp.inf, jnp.where(causal, imp_t, -jnp.inf))
        rank = jnp.zeros((nb, TQ), jnp.int32)
        for jp in range(nb):
            sj = score[jp:jp + 1, :]
            beats = (sj > score) | ((sj == score) & (jb > jp))
            rank = rank + beats.astype(jnp.int32)
            if jp % 8 == 7:
                yield
        sel_t = ((rank < n_sel) & causal).astype(BF16)
        place = (_iota((nb, LANE), 0) == _iota((nb, LANE), 1)).astype(BF16)
        unsel = _mm_tn(sel_t, place) - (_iota((TQ, LANE), 1) < nb).astype(F32)
        yield
        return o_c, unsel.astype(BF16)

    def window():
        n_band = WINDOW // TQ + 1
        qk, tiles, v_w = [], [], []
        for c in range(n_band):
            j = qi - (n_band - 1) + c
            jc = jnp.maximum(j, 0)
            k0 = pl.multiple_of(jc * TQ, TQ)
            tiles.append((j, jc))
            qk.append(_mm_nt(q4, kw_ref[0, pl.ds(k0, TQ), :].astype(BF16)))
            v_w.append(_with_ones(vw_ref[0, pl.ds(k0, TQ), :].astype(BF16)))
        yield
        heads = range(NSA_GROUP)
        s_w = [[None] * NSA_GROUP for _ in range(n_band)]
        m_w = [None] * NSA_GROUP
        for c in range(n_band):
            for h in heads:
                s = qk[c][h * TQ:(h + 1) * TQ] + _bias_head(toep_ref, h, (qi - tiles[c][1]) * sub, True,
                                                             tiles[c][0] >= 0)
                s_w[c][h] = s
                m_w[h] = s if c == 0 else jnp.maximum(m_w[h], s)
                yield
        for h in heads:
            m_w[h] = jnp.max(m_w[h], axis=-1, keepdims=True)
        yield
        v_band = jnp.concatenate(v_w, axis=0)
        acc_w = []
        for h in heads:
            p_h = []
            for c in range(n_band):
                p_h.append(jnp.exp2(s_w[c][h] - m_w[h]).astype(BF16))
                yield
            acc_w.append(_mm(jnp.concatenate(p_h, axis=1), v_band))
            yield
        return jnp.concatenate([a[:, :HEAD_DIM] / jnp.maximum(a[:, HEAD_DIM:], 1e-30) for a in acc_w], axis=0)

    def gates():
        gts = jax.nn.sigmoid(gate_ref[0])
        grp = pl.program_id(1)
        za = za_ref[0]
        out = [[None] * NSA_GROUP for _ in range(3)]
        for h in range(NSA_GROUP):
            z = za[:, h * HEAD_DIM:(h + 1) * HEAD_DIM]
            zs = z * jax.nn.sigmoid(z)
            for branch in range(3):
                lane = branch * NSA_HEADS + h
                col = gts[:, lane:lane + 1]
                for g in range(1, NSA_KV_HEADS):
                    lg = lane + g * NSA_GROUP
                    col = jnp.where(grp == g, gts[:, lg:lg + 1], col)
                out[branch][h] = zs * col
                yield
        return out

    (o_c, unsel), o_w, gz = _interleave([compressed_and_selection(), window(), gates()])
    o_cw = [gz[0][h] * o_c[h * TQ:(h + 1) * TQ] + gz[2][h] * o_w[h * TQ:(h + 1) * TQ] for h in range(NSA_GROUP)]

    q_sel = jnp.concatenate([q4, jnp.concatenate([unsel] * NSA_GROUP, axis=0)], axis=1)

    def key_tile(j):
        k0 = pl.multiple_of(j * TQ, TQ)
        blk = _div_pow2(j * TQ + _iota((TQ, LANE), 0), SLC_BLOCK)
        marks = jnp.where(_iota((TQ, LANE), 1) == blk, BIG, 0.0).astype(BF16)
        return jnp.concatenate([ks_ref[0, pl.ds(k0, TQ), :].astype(BF16), marks], axis=1)

    assert n_tiles % 4 == 0
    n_pairs = _div_pow2(qi + 2, 2)
    passes, first = [], 0
    for tiles_per_iter in (8, 4, 2):
        if tiles_per_iter <= n_tiles:
            trips = _div_pow2(2 * n_pairs - first, tiles_per_iter)
            passes.append((tiles_per_iter, first, trips))
            first = first + tiles_per_iter * trips

    def slc_logits(tpi, base):
        def body(it, carry):
            tiles = [base + tpi * it + c for c in range(tpi)]
            qk = [_mm_nt(q_sel, key_tile(j)) for j in tiles]
            macc = macc_ref[...]
            for j, qk_j in zip(tiles, qk):
                s = qk_j + _bias_tile(toep_ref, (qi - j) * sub, False)
                s_ref[j] = s
                macc = jnp.maximum(macc, _fold_lanes(s, jnp.maximum))
            macc_ref[...] = macc
            return carry
        return body

    macc_ref[...] = jnp.full((rows, LANE), NEG, F32)
    for tpi, base, trips in passes:
        lax.fori_loop(0, trips, slc_logits(tpi, base), 0)
    m_s = jnp.max(macc_ref[...], axis=-1, keepdims=True)

    def slc_values(tpi, base):
        def body(it, carry):
            j0 = base + tpi * it
            k0 = pl.multiple_of(j0 * TQ, 2 * TQ)
            p_it = jnp.concatenate([jnp.exp2(s_ref[j0 + c] - m_s).astype(BF16) for c in range(tpi)], axis=1)
            acc_ref[...] += _mm(p_it, _with_ones(vs_ref[0, pl.ds(k0, tpi * TQ), :].astype(BF16)))
            return carry
        return body

    acc_ref[...] = jnp.zeros((rows, 2 * HEAD_DIM), F32)
    for tpi, base, trips in passes:
        lax.fori_loop(0, trips, slc_values(tpi, base), 0)
    acc = acc_ref[...]
    o_s = acc[:, :HEAD_DIM] / jnp.maximum(acc[:, HEAD_DIM:], 1e-30)

    for h in range(NSA_GROUP):
        o = o_cw[h] + gz[1][h] * o_s[h * TQ:(h + 1) * TQ]
        o_ref[0, :, h * HEAD_DIM:(h + 1) * HEAD_DIM] = o.astype(o_ref.dtype)


def _nsa(proj3, bias_c, toep, posk, w1k, w2k, posv, w1v, w2v):
    b, s, _ = proj3.shape
    nr = s // CMP_STRIDE
    gq = NSA_GROUP * HEAD_DIM

    def kvspec(idx):
        return pl.BlockSpec((1, s, HEAD_DIM), lambda bi, g, qi, idx=idx: (bi, 0, P_KV // HEAD_DIM + 2 * idx + g))

    def whole(a):
        return pl.BlockSpec(a.shape, lambda bi, g, qi, nd=a.ndim: (0,) * nd)

    in_specs = [
        pl.BlockSpec((1, TQ, gq), lambda bi, g, qi: (bi, qi, P_Q // gq + g)),
        kvspec(0), kvspec(1), kvspec(2), kvspec(3), kvspec(4), kvspec(5),
        pl.BlockSpec((1, TQ, LANE), lambda bi, g, qi: (bi, qi, P_GATE // LANE)),
        pl.BlockSpec((1, TQ, gq), lambda bi, g, qi: (bi, qi, P_ZA // gq + g)),
        pl.BlockSpec((NSA_GROUP, TQ, nr), lambda bi, g, qi: (g, qi, 0)),
        pl.BlockSpec((NSA_GROUP, NT_ALL, TB, TB), lambda bi, g, qi: (g, 0, 0, 0)),
        whole(posk), whole(w1k), whole(w2k), whole(posv), whole(w1v), whole(w2v),
    ]
    return pl.pallas_call(
        _nsa_kernel,
        grid=(b, NSA_KV_HEADS, s // TQ),
        in_specs=in_specs,
        out_specs=pl.BlockSpec((1, TQ, gq), lambda bi, g, qi: (bi, qi, g)),
        out_shape=jax.ShapeDtypeStruct((b, s, NSA_WIDTH), BF16),
        scratch_shapes=[pltpu.VMEM((nr, HEAD_DIM), BF16), pltpu.VMEM((nr, HEAD_DIM), BF16),
                        pltpu.VMEM((s // TQ, NSA_GROUP * TQ, TQ), F32),
                        pltpu.VMEM((NSA_GROUP * TQ, LANE), F32),
                        pltpu.VMEM((NSA_GROUP * TQ, 2 * HEAD_DIM), F32)],
        compiler_params=pltpu.CompilerParams(
            dimension_semantics=("parallel", "parallel", "arbitrary"),
            vmem_limit_bytes=56 * 1024 * 1024),
        name="nsa",
    )(proj3, proj3, proj3, proj3, proj3, proj3, proj3, proj3, proj3,
      bias_c, toep, posk, w1k, w2k, posv, w1v, w2v)


def _shift_mix(ref, prev_ref, mu, sl):
    x = ref[0, :, sl]
    prev = jnp.where(_iota(x.shape, 0) == 0, prev_ref[:, sl], pltpu.roll(x, 1, 0))
    prev_ref[:, sl] = x[x.shape[0] - 1:]
    return x + mu * (prev - x)


def _rwkv_kernel(r_ref, k_ref, v_ref, wa_ref, zb_ref, vec_ref, muwa_ref, w2_ref, a2_ref,
                 o_ref, st_ref, pr_ref, pk_ref, pv_ref, pwa_ref):
    first = pl.program_id(2) == 0
    tb = r_ref.shape[1]
    n_groups = r_ref.shape[2] // GW
    n_chunks = tb // CH

    @pl.when(first)
    def _():
        for ref in (st_ref, pr_ref, pk_ref, pv_ref, pwa_ref):
            ref[...] = jnp.zeros_like(ref)

    wa = _shift_mix(wa_ref, pwa_ref, muwa_ref[...], slice(0, LANE))
    wd_act = jnp.tanh(wa[:, :LORA]).astype(BF16)
    ad = wa[:, LORA:].astype(BF16)

    seg = (_div_pow2(_iota((GW, GW), 0), RWKV_HEAD_DIM) == _div_pow2(_iota((GW, GW), 1), RWKV_HEAD_DIM))
    segf = seg.astype(F32)
    segb = seg.astype(BF16)
    assert CH == RWKV_HEAD_DIM
    lane_s = _mod_pow2(_iota((CH, GW), 1), CH)
    row_t = _iota((CH, GW), 0)
    strict = lane_s < row_t
    incl = lane_s <= row_t
    eye = (lane_s == row_t).astype(F32)
    ti_r, ti_c = _iota((tb, tb), 0), _iota((tb, tb), 1)
    trib = ((ti_c <= ti_r) & (_div_pow2(ti_c, CH) == _div_pow2(ti_r, CH))).astype(BF16)

    def bd(x):
        xb = x.astype(BF16)
        return jnp.concatenate([xb] * HPG, axis=0) * segb

    def prep(gi):
        lanes = slice(gi * GW, (gi + 1) * GW)
        vec = vec_ref[:, lanes]
        mu_r, mu_k, mu_v = vec[0:1], vec[1:2], vec[2:3]
        w0, a0, k_k, k_a = vec[3:4], vec[4:5], vec[5:6], vec[6:7]
        r = _shift_mix(r_ref, pr_ref, mu_r, lanes)
        k = _shift_mix(k_ref, pk_ref, mu_k, lanes)
        v = _shift_mix(v_ref, pv_ref, mu_v, lanes)
        w_lora = _mm(wd_act, w2_ref[:, lanes])
        a_lora = _mm(ad, a2_ref[:, lanes])
        kk = k * k_k
        kk_ss = _mm((kk * kk).astype(BF16), segb)
        yield
        lw = jax.nn.sigmoid(w0 + w_lora) * (-math.exp(-0.5) * LOG2E)
        cum = _mm_split_rhs(trib, lw)
        yield
        a_sig = jax.nn.sigmoid(a0 + a_lora)
        kk = kk * lax.rsqrt(jnp.maximum(kk_ss, 1e-24))
        k = k * (1.0 + (a_sig - 1.0) * k_a)
        return dict(r=r, k=k, v=v, a=-kk, b=kk * a_sig, lw=lw, cum=cum, vec=vec, lanes=lanes)

    groups = _interleave([prep(gi) for gi in range(n_groups)])

    def chunk_local(g, c):
        ts = slice(c * CH, (c + 1) * CH)
        rc, kc, vc, ac, bc, lwc, cum = (g[n][ts] for n in ("r", "k", "v", "a", "b", "lw", "cum"))
        tot = cum[CH - 1:CH]
        e_out = jnp.exp2(-cum)
        e_end = jnp.exp2(tot - cum)
        r_t = rc * jnp.exp2(cum)
        a_t = ac * jnp.exp2(cum - lwc)
        lhs = jnp.concatenate([a_t, r_t], axis=0).astype(BF16)
        aa = _mm_nt(lhs, jnp.concatenate([bd(bc * e_out), bd(kc * e_out)], axis=0))
        yield
        a_ab = jnp.where(strict, aa[:CH, :GW], 0.0)
        a_ak = jnp.where(strict, aa[:CH, GW:], 0.0)
        a_rb = jnp.where(incl, aa[CH:, :GW], 0.0)
        a_rk = jnp.where(incl, aa[CH:, GW:], 0.0)
        t_inv = eye + a_ab
        mpow = _mm(a_ab.astype(BF16), bd(a_ab))
        av = _mm(a_ak.astype(BF16), bd(vc))
        yield
        for _ in range(int(math.log2(CH)) - 1):
            res = _mm(jnp.concatenate([t_inv, mpow], axis=0).astype(BF16), bd(mpow))
            yield
            t_inv = t_inv + res[:CH]
            mpow = res[CH:]
        wu = _mm(t_inv.astype(BF16), jnp.concatenate([bd(a_t), bd(av)], axis=1))
        yield
        return dict(
            lhs=jnp.concatenate([wu[:, :GW], r_t], axis=0).astype(BF16), u_loc=wu[:, GW:],
            a_r=jnp.concatenate([a_rb, a_rk], axis=1).astype(BF16), bdv=bd(vc), vc=vc,
            bk_end=jnp.concatenate([bc * e_end, kc * e_end], axis=0).astype(BF16), dec=jnp.exp2(tot))

    loc = _interleave([chunk_local(g, c) for g in groups for c in range(n_chunks)])

    def chain(gi):
        ys = []
        g_state = st_ref[gi]
        for c in range(n_chunks):
            lc = loc[gi * n_chunks + c]
            x0 = _mm_nt(lc["lhs"], g_state.astype(BF16))
            yield
            u = x0[:CH] + lc["u_loc"]
            y_c = _mm(lc["a_r"], jnp.concatenate([bd(u), lc["bdv"]], axis=0))
            upd = _mm_tn(jnp.concatenate([u, lc["vc"]], axis=0).astype(BF16), lc["bk_end"])
            yield
            ys.append(x0[CH:] + y_c)
            g_state = g_state * lc["dec"] + upd * segf
        st_ref[gi] = g_state
        return jnp.concatenate(ys, axis=0)

    ys = _interleave([chain(gi) for gi in range(n_groups)])

    def finish(g, y):
        vec = g["vec"]
        ln_w, ln_b, r_k = vec[7:8], vec[8:9], vec[9:10]
        inv_n = 1.0 / RWKV_HEAD_DIM
        mean = _mm(y.astype(BF16), segb) * inv_n
        bonus = _mm((g["r"] * g["k"] * r_k).astype(BF16), segb) * g["v"]
        yield
        yc = y - mean
        var = _mm((yc * yc).astype(BF16), segb) * inv_n
        yield
        yn = yc * lax.rsqrt(var + RWKV_GN_EPS) * ln_w + ln_b
        zb = zb_ref[0, :, g["lanes"]]
        o_ref[0, :, g["lanes"]] = ((yn + bonus) * (zb * jax.nn.sigmoid(zb))).astype(o_ref.dtype)

    _interleave([finish(g, y) for g, y in zip(groups, ys)])


def _rwkv(proj3, vecs, mu_wa, w2, a2, tb, gps):
    b, s, _ = proj3.shape
    gw = gps * GW
    ng = RWKV_WIDTH // gw

    def col(off):
        return pl.BlockSpec((1, tb, gw), lambda bi, g, ti, off=off: (bi, ti, off // gw + g))

    in_specs = [
        col(P_RKV), col(P_RKV + RWKV_WIDTH), col(P_RKV + 2 * RWKV_WIDTH),
        pl.BlockSpec((1, tb, LANE), lambda bi, g, ti: (bi, ti, P_WDAD // LANE)),
        col(P_ZB),
        pl.BlockSpec((vecs.shape[0], gw), lambda bi, g, ti: (0, g)),
        pl.BlockSpec((1, LANE), lambda bi, g, ti: (0, 0)),
        pl.BlockSpec((LORA, gw), lambda bi, g, ti: (0, g)),
        pl.BlockSpec((LORA, gw), lambda bi, g, ti: (0, g)),
    ]
    return pl.pallas_call(
        _rwkv_kernel,
        grid=(b, ng, s // tb),
        in_specs=in_specs,
        out_specs=pl.BlockSpec((1, tb, gw), lambda bi, g, ti: (bi, ti, g)),
        out_shape=jax.ShapeDtypeStruct((b, s, RWKV_WIDTH), BF16),
        scratch_shapes=[pltpu.VMEM((gps, GW, GW), F32), pltpu.VMEM((1, gw), F32), pltpu.VMEM((1, gw), F32),
                        pltpu.VMEM((1, gw), F32), pltpu.VMEM((1, LANE), F32)],
        compiler_params=pltpu.CompilerParams(
            dimension_semantics=("parallel", "parallel", "arbitrary")),
        name="rwkv",
    )(proj3, proj3, proj3, proj3, proj3, vecs, mu_wa, w2, a2)


def _outproj_kernel(ma_ref, mb_ref, wa_ref, wb_ref, x_ref, g_ref, o_ref):
    y = _mm(ma_ref[...], wa_ref[...]) + _mm(mb_ref[...], wb_ref[...])
    ms = jnp.mean(y * y, axis=-1, keepdims=True)
    o_ref[...] = x_ref[...] + y * lax.rsqrt(ms + NORM_EPS) * g_ref[...]


def _outproj(mix_a, mix_b, w_a, w_b, x2, g, tm):
    m, d = x2.shape
    ka, kb = mix_a.shape[1], mix_b.shape[1]
    return pl.pallas_call(
        _outproj_kernel,
        grid=(m // tm,),
        in_specs=[
            pl.BlockSpec((tm, ka), lambda i: (i, 0)),
            pl.BlockSpec((tm, kb), lambda i: (i, 0)),
            pl.BlockSpec((ka, d), lambda i: (0, 0)),
            pl.BlockSpec((kb, d), lambda i: (0, 0)),
            pl.BlockSpec((tm, d), lambda i: (i, 0)),
            pl.BlockSpec((1, d), lambda i: (0, 0)),
        ],
        out_specs=pl.BlockSpec((tm, d), lambda i: (i, 0)),
        out_shape=jax.ShapeDtypeStruct((m, d), F32),
        compiler_params=pltpu.CompilerParams(
            dimension_semantics=("parallel",), vmem_limit_bytes=56 * 1024 * 1024),
        name="outproj",
    )(mix_a, mix_b, w_a, w_b, x2, g)


_W_SEGMENTS = (
    (P_Q, R_Q, NSA_WIDTH),
    (P_RKV, R_FEAT, 3 * RWKV_WIDTH),
    (P_ZB, R_ZB, RWKV_WIDTH),
    (P_ZA, R_ZA, NSA_WIDTH),
    (P_KV, R_KV, 6 * NSA_KV_HEADS * HEAD_DIM),
    (P_WDAD, R_FEAT + 3 * RWKV_WIDTH, 2 * LORA),
    (P_GATE, R_GATE, LANE),
)


def _relayout_kernel(w_ref, o_ref):
    for dst, src, width in _W_SEGMENTS:
        o_ref[:, dst:dst + width] = w_ref[:, src:src + width].astype(BF16)
    used = P_GATE + LANE
    o_ref[:, used:] = jnp.zeros((o_ref.shape[0], NP - used), BF16)


def _permute_w_in(w, rows):
    d, n = w.shape
    return pl.pallas_call(
        _relayout_kernel,
        grid=(d // rows,),
        in_specs=[pl.BlockSpec((rows, n), lambda i: (i, 0))],
        out_specs=pl.BlockSpec((rows, NP), lambda i: (i, 0)),
        out_shape=jax.ShapeDtypeStruct((d, NP), BF16),
        compiler_params=pltpu.CompilerParams(dimension_semantics=("parallel",)),
        name="relayout",
    )(w)


def _block(x, pre_norm_g, w_in, rel_bias_table, cmp_pos_k, cmp_pos_v, cmp_k_w1, cmp_k_w2, cmp_v_w1,
           cmp_v_w2, rwkv_mu, rwkv_w0, rwkv_w2, rwkv_a0, rwkv_a2, rwkv_k_k, rwkv_k_a, rwkv_r_k,
           rwkv_ln_w, rwkv_ln_b, w_out, post_norm_g):
    b, s, d = x.shape
    x2 = x.reshape(b * s, d)
    tm = min(1024, b * s)
    proj = _inproj(x2, pre_norm_g.reshape(1, d), _permute_w_in(w_in.astype(BF16), 256), tm, 1024)
    proj3 = proj.reshape(b, s, NP)

    bias_c, toep = _bias(rel_bias_table.reshape(-1), s)
    half = CMP_STRIDE * HEAD_DIM
    mix_a = _nsa(proj3, bias_c, toep,
                 cmp_pos_k.reshape(2, half), cmp_k_w1.astype(BF16), cmp_k_w2.astype(BF16),
                 cmp_pos_v.reshape(2, half), cmp_v_w1.astype(BF16), cmp_v_w2.astype(BF16))

    w3 = 3 * RWKV_WIDTH
    vec_rows = [rwkv_mu[:RWKV_WIDTH], rwkv_mu[RWKV_WIDTH:2 * RWKV_WIDTH], rwkv_mu[2 * RWKV_WIDTH:w3],
                rwkv_w0, rwkv_a0, rwkv_k_k, rwkv_k_a, rwkv_ln_w, rwkv_ln_b, rwkv_r_k.reshape(-1)]
    vecs = jnp.stack(vec_rows + [jnp.zeros_like(rwkv_w0)] * (16 - len(vec_rows)), axis=0)
    mix_b = _rwkv(proj3, vecs, rwkv_mu[w3:].reshape(1, 2 * LORA), rwkv_w2.astype(BF16),
                  rwkv_a2.astype(BF16), min(256, s), RWKV_GROUPS_PER_STEP)

    w_o = w_out.astype(BF16)
    out = _outproj(mix_a.reshape(b * s, NSA_WIDTH), mix_b.reshape(b * s, RWKV_WIDTH),
                   w_o[:NSA_WIDTH], w_o[NSA_WIDTH:], x2, post_norm_g.reshape(1, d), min(512, b * s))
    return out.reshape(b, s, d)


def kernel(x, pre_norm_g, w_in, rel_bias_table, cmp_pos_k, cmp_pos_v, cmp_k_w1, cmp_k_w2, cmp_v_w1,
           cmp_v_w2, rwkv_mu, rwkv_w0, rwkv_w2, rwkv_a0, rwkv_a2, rwkv_k_k, rwkv_k_a, rwkv_r_k,
           rwkv_ln_w, rwkv_ln_b, w_out, post_norm_g):
    h = x
    for l in range(pre_norm_g.shape[0]):
        h = _block(h, pre_norm_g[l], w_in[l], rel_bias_table, cmp_pos_k[l], cmp_pos_v[l], cmp_k_w1[l],
                   cmp_k_w2[l], cmp_v_w1[l], cmp_v_w2[l], rwkv_mu[l], rwkv_w0[l], rwkv_w2[l],
                   rwkv_a0[l], rwkv_a2[l], rwkv_k_k[l], rwkv_k_a[l], rwkv_r_k[l], rwkv_ln_w[l],
                   rwkv_ln_b[l], w_out[l], post_norm_g[l])
    return h
```

```python
import math

import jax
import jax.numpy as jnp
from jax import lax
from jax.experimental import pallas as pl
from jax.experimental.pallas import tpu as pltpu

F32 = jnp.float32
BF16 = jnp.bfloat16

D_MODEL = 2048
NSA_HEADS = 8
NSA_KV_HEADS = 2
NSA_GROUP = NSA_HEADS // NSA_KV_HEADS
HEAD_DIM = 128
NSA_WIDTH = NSA_HEADS * HEAD_DIM
CMP_BLOCK = 32
CMP_STRIDE = 16
SLC_BLOCK = 64
SLC_TOP_N = 16
WINDOW = 512
RWKV_WIDTH = 1024
RWKV_HEAD_DIM = 64
RWKV_HEADS = RWKV_WIDTH // RWKV_HEAD_DIM
LORA = 64
NUM_BUCKETS = 32
MAX_DISTANCE = 1024
NORM_EPS = 1e-6
RWKV_GN_EPS = 64e-5

R_Q = 0
R_KV = R_Q + NSA_WIDTH
R_GATE = R_KV + 6 * NSA_KV_HEADS * HEAD_DIM
R_ZA = R_GATE + 3 * NSA_HEADS
R_FEAT = R_ZA + NSA_WIDTH
R_ZB = R_FEAT + 3 * RWKV_WIDTH + 2 * LORA
R_END = R_ZB + RWKV_WIDTH

P_Q = 0
P_RKV = 1024
P_ZB = 4096
P_ZA = 5120
P_KV = 6144
P_WDAD = 7680
P_GATE = 7808
NP = 8192

LANE = 128
TQ = 256
TB = 128
ND = 9
T_DIAG = ND
T_WEND = ND + 1
T_NONE = ND + 2
NT_ALL = ND + 3
LOG2E = math.log2(math.e)
BIG = 2.0 ** 100
CH = 64
HPG = 4
GW = HPG * RWKV_HEAD_DIM
RWKV_GROUPS_PER_STEP = 4
NEG = -1e30


def _bucket_thresholds():
    out = []
    for k in range(1, NUM_BUCKETS // 2):
        n = 16
        while n ** 8 < (16 ** 8) * (2 ** (3 * k)):
            n += 1
        out.append(n)
    return out


_THR = _bucket_thresholds()


def _mm(a, b):
    return jnp.dot(a, b, preferred_element_type=F32)


def _mm_nt(a, b):
    return lax.dot_general(a, b, (((1,), (1,)), ((), ())), preferred_element_type=F32)


def _mm_tn(a, b):
    return lax.dot_general(a, b, (((0,), (0,)), ((), ())), preferred_element_type=F32)


def _split3(x):
    x1 = x.astype(BF16)
    r1 = x - x1.astype(F32)
    x2 = r1.astype(BF16)
    x3 = (r1 - x2.astype(F32)).astype(BF16)
    return x1, x2, x3


def _mm_split_rhs(a_exact, b):
    b1 = b.astype(BF16)
    b2 = (b - b1.astype(F32)).astype(BF16)
    return _mm(a_exact, b1) + _mm(a_exact, b2)


def _iota(shape, dim):
    return lax.broadcasted_iota(jnp.int32, shape, dim)


def _interleave(gens):
    results = [None] * len(gens)
    live = list(enumerate(gens))
    while live:
        still = []
        for i, g in live:
            try:
                next(g)
                still.append((i, g))
            except StopIteration as stop:
                results[i] = stop.value
        live = still
    return results


def _div_pow2(x, n):
    assert n & (n - 1) == 0
    return x >> (n.bit_length() - 1)


def _mod_pow2(x, n):
    assert n & (n - 1) == 0
    return x & (n - 1)


def _inproj_kernel(x_ref, g_ref, w_ref, o_ref, hn_ref):
    @pl.when(pl.program_id(1) == 0)
    def _():
        x = x_ref[...]
        ms = jnp.mean(x * x, axis=-1, keepdims=True)
        hn_ref[...] = (x * lax.rsqrt(ms + NORM_EPS) * g_ref[...]).astype(BF16)

    o_ref[...] = _mm(hn_ref[...], w_ref[...])


def _inproj(x2, g, w, tm, tn):
    m, d = x2.shape
    n = w.shape[1]
    return pl.pallas_call(
        _inproj_kernel,
        grid=(m // tm, n // tn),
        in_specs=[
            pl.BlockSpec((tm, d), lambda i, j: (i, 0)),
            pl.BlockSpec((1, d), lambda i, j: (0, 0)),
            pl.BlockSpec((d, tn), lambda i, j: (0, j)),
        ],
        out_specs=pl.BlockSpec((tm, tn), lambda i, j: (i, j)),
        out_shape=jax.ShapeDtypeStruct((m, n), F32),
        scratch_shapes=[pltpu.VMEM((tm, d), BF16)],
        compiler_params=pltpu.CompilerParams(
            dimension_semantics=("parallel", "arbitrary"),
            vmem_limit_bytes=56 * 1024 * 1024),
        name="inproj",
    )(x2, g, w)


def _bucket(n):
    n = jnp.maximum(n, 0)
    large = jnp.full(n.shape, NUM_BUCKETS // 2, jnp.int32)
    for thr in _THR:
        large = large + (n >= thr).astype(jnp.int32)
    return jnp.where(n < NUM_BUCKETS // 2, n, large)


def _lookup_all_heads(dist, tab_ref):
    bucket = _bucket(dist)
    hits = [bucket == b for b in range(NUM_BUCKETS)]
    outs = []
    for h in range(NSA_HEADS):
        out = jnp.zeros(dist.shape, F32)
        for b in range(NUM_BUCKETS):
            out = jnp.where(hits[b], tab_ref[b * NSA_HEADS + h] * LOG2E, out)
        outs.append(out)
    return outs


def _bias_kernel(tab_ref, bc_ref, tp_ref):
    i = pl.program_id(0)
    rows, nr = bc_ref.shape[1], bc_ref.shape[2]
    dist_c = (i * rows + _iota((rows, nr), 0)) - (_iota((rows, nr), 1) * CMP_STRIDE + (CMP_BLOCK - 1))
    for h, vals in enumerate(_lookup_all_heads(dist_c, tab_ref)):
        bc_ref[h] = vals

    @pl.when(i == 0)
    def _():
        base = _iota((TB, TB), 0) - _iota((TB, TB), 1)
        neg = jnp.full((TB, TB), NEG, F32)
        for h in range(NSA_HEADS):
            tp_ref[h, T_NONE] = neg
        for d in range(ND):
            for h, vals in enumerate(_lookup_all_heads(base + d * TB, tab_ref)):
                tp_ref[h, d] = vals
                if d == 0:
                    tp_ref[h, T_DIAG] = jnp.where(base >= 0, vals, neg)
                if d == WINDOW // TB:
                    tp_ref[h, T_WEND] = jnp.where(base < 0, vals, neg)


def _bias(table_flat, s):
    nr = s // CMP_STRIDE
    rows = min(256, s)
    return pl.pallas_call(
        _bias_kernel,
        grid=(s // rows,),
        in_specs=[pl.BlockSpec(memory_space=pltpu.SMEM)],
        out_specs=[
            pl.BlockSpec((NSA_HEADS, rows, nr), lambda i: (0, i, 0)),
            pl.BlockSpec((NSA_HEADS, NT_ALL, TB, TB), lambda i: (0, 0, 0, 0)),
        ],
        out_shape=[
            jax.ShapeDtypeStruct((NSA_HEADS, s, nr), F32),
            jax.ShapeDtypeStruct((NSA_HEADS, NT_ALL, TB, TB), F32),
        ],
        compiler_params=pltpu.CompilerParams(dimension_semantics=("arbitrary",)),
        name="bias",
    )(table_flat)


def _compress(kv_ref, pos_ref, w1_ref, w2_ref, nr):
    half = CMP_STRIDE * HEAD_DIM
    r = jnp.concatenate(
        [kv_ref[0, pl.ds(m, nr, stride=CMP_STRIDE), :] for m in range(CMP_STRIDE)], axis=1)
    a = _mm((r + pos_ref[0:1, :]).astype(BF16), w1_ref[0:half, :])
    b = _mm((r + pos_ref[1:2, :]).astype(BF16), w1_ref[half:2 * half, :])
    pre = a + pltpu.roll(b, nr - 1, 0)
    h1 = pre * jax.nn.sigmoid(pre)
    return _mm(h1.astype(BF16), w2_ref[...])


def _bias_head(toep_ref, h, d_tiles, window, valid=True):
    sub = TQ // TB
    rows = []
    for ri in range(sub):
        cols = []
        for ci in range(sub):
            d = d_tiles + ri - ci
            idx = jnp.where(d == 0, T_DIAG, jnp.minimum(d, ND - 1))
            if window:
                idx = jnp.where(d == WINDOW // TB, T_WEND, jnp.where(d > WINDOW // TB, T_NONE, idx))
            idx = jnp.where((d < 0) | jnp.logical_not(valid), T_NONE, idx)
            cols.append(toep_ref[h, idx])
        rows.append(jnp.concatenate(cols, axis=1))
    return jnp.concatenate(rows, axis=0)


def _bias_tile(toep_ref, d_tiles, window, valid=True):
    return jnp.concatenate([_bias_head(toep_ref, h, d_tiles, window, valid) for h in range(NSA_GROUP)], axis=0)


def _add_shared(s, mask_add):
    n = s.shape[-1]
    return (s.reshape(NSA_GROUP, TQ, n) + mask_add[None]).reshape(NSA_GROUP * TQ, n)


def _with_ones(v):
    return jnp.concatenate([v, jnp.ones(v.shape, v.dtype)], axis=1)


def _fold_lanes(x, op):
    out = x[:, :LANE]
    for c in range(1, x.shape[1] // LANE):
        out = op(out, x[:, c * LANE:(c + 1) * LANE])
    return out


def _nsa_kernel(q_ref, kc_ref, vc_ref, ks_ref, vs_ref, kw_ref, vw_ref, gate_ref, za_ref,
                bc_ref, toep_ref, posk_ref, w1k_ref, w2k_ref, posv_ref, w1v_ref, w2v_ref,
                o_ref, kcs_ref, vcs_ref, s_ref, macc_ref, acc_ref):
    qi = pl.program_id(2)
    s_len = kc_ref.shape[1]
    nr = s_len // CMP_STRIDE
    nb = s_len // SLC_BLOCK
    n_sel = min(SLC_TOP_N, nb)

    @pl.when(qi == 0)
    def _():
        kcs_ref[...] = _compress(kc_ref, posk_ref, w1k_ref, w2k_ref, nr).astype(BF16)
        vcs_ref[...] = _compress(vc_ref, posv_ref, w1v_ref, w2v_ref, nr).astype(BF16)

    q0 = qi * TQ
    rows = NSA_GROUP * TQ
    q = q_ref[0] * (HEAD_DIM ** -0.5 * LOG2E)
    q4 = jnp.concatenate([q[:, h * HEAD_DIM:(h + 1) * HEAD_DIM] for h in range(NSA_GROUP)],
                         axis=0).astype(BF16)

    sub = TQ // TB
    n_tiles = s_len // TQ

    def compressed_and_selection():
        t_c = q0 + _iota((TQ, nr), 0)
        i_c = _iota((TQ, nr), 1)
        mask_c = (t_c - (i_c * CMP_STRIDE + (CMP_BLOCK - 1)) >= 0) & (i_c < nr - 1)
        qk = _mm_nt(q4, kcs_ref[...])
        mask_add = jnp.where(mask_c, 0.0, NEG)
        yield
        p_heads = []
        for h in range(NSA_GROUP):
            lg = qk[h * TQ:(h + 1) * TQ] + bc_ref[h] + mask_add
            m_c = jnp.max(lg, axis=-1, keepdims=True)
            yield
            e = jnp.where(lg > 0.5 * NEG, jnp.exp2(lg - m_c), 0.0)
            l_c = jnp.sum(e, axis=-1, keepdims=True)
            yield
            p_heads.append(e / jnp.maximum(l_c, 1e-30))
        p = jnp.concatenate(p_heads, axis=0)
        o_c = _mm(p.astype(BF16), vcs_ref[...])
        psum = p_heads[0]
        for h in range(1, NSA_GROUP):
            psum = psum + p_heads[h]
        ov_i = _iota((nb, nr), 1) * CMP_STRIDE
        ov_j = _iota((nb, nr), 0) * SLC_BLOCK
        ov_t = ((ov_i < ov_j + SLC_BLOCK) & (ov_i + CMP_BLOCK > ov_j)).astype(BF16)
        p1, p2, p3 = _split3(psum)
        imp_t = _mm_nt(ov_t, p1) + _mm_nt(ov_t, p2) + _mm_nt(ov_t, p3)
        yield
        jb = _iota((nb, TQ), 0)
        cur = _div_pow2(q0 + _iota((nb, TQ), 1), SLC_BLOCK)
        forced = (jb == 0) | (jb == cur) | (jb == cur - 1)
        causal = jb <= cur
        score = jnp.where(forced, jnp.inf, jnp.where(causal, imp_t, -jnp.inf))
        rank = jnp.zeros((nb, TQ), jnp.int32)
        for jp in range(nb):
            sj = score[jp:jp + 1, :]
            beats = (sj > score) | ((sj == score) & (jb > jp))
            rank = rank + beats.astype(jnp.int32)
            if jp % 8 == 7:
                yield
        sel_t = ((rank < n_sel) & causal).astype(BF16)
        place = (_iota((nb, LANE), 0) == _iota((nb, LANE), 1)).astype(BF16)
        unsel = _mm_tn(sel_t, place) - (_iota((TQ, LANE), 1) < nb).astype(F32)
        yield
        return o_c, unsel.astype(BF16)

    def window():
        n_band = WINDOW // TQ + 1
        qk, tiles, v_w = [], [], []
        for c in range(n_band):
            j = qi - (n_band - 1) + c
            jc = jnp.maximum(j, 0)
            k0 = pl.multiple_of(jc * TQ, TQ)
            tiles.append((j, jc))
            qk.append(_mm_nt(q4, kw_ref[0, pl.ds(k0, TQ), :].astype(BF16)))
            v_w.append(_with_ones(vw_ref[0, pl.ds(k0, TQ), :].astype(BF16)))
        yield
        heads = range(NSA_GROUP)
        s_w = [[None] * NSA_GROUP for _ in range(n_band)]
        m_w = [None] * NSA_GROUP
        for c in range(n_band):
            for h in heads:
                s = qk[c][h * TQ:(h + 1) * TQ] + _bias_head(toep_ref, h, (qi - tiles[c][1]) * sub, True,
                                                             tiles[c][0] >= 0)
                s_w[c][h] = s
                m_w[h] = s if c == 0 else jnp.maximum(m_w[h], s)
                yield
        for h in heads:
            m_w[h] = jnp.max(m_w[h], axis=-1, keepdims=True)
        yield
        v_band = jnp.concatenate(v_w, axis=0)
        acc_w = []
        for h in heads:
            p_h = []
            for c in range(n_band):
                p_h.append(jnp.exp2(s_w[c][h] - m_w[h]).astype(BF16))
                yield
            acc_w.append(_mm(jnp.concatenate(p_h, axis=1), v_band))
            yield
        return jnp.concatenate([a[:, :HEAD_DIM] / jnp.maximum(a[:, HEAD_DIM:], 1e-30) for a in acc_w], axis=0)

    def gates():
        gts = jax.nn.sigmoid(gate_ref[0])
        grp = pl.program_id(1)
        za = za_ref[0]
        out = [[None] * NSA_GROUP for _ in range(3)]
        for h in range(NSA_GROUP):
            z = za[:, h * HEAD_DIM:(h + 1) * HEAD_DIM]
            zs = z * jax.nn.sigmoid(z)
            for branch in range(3):
                lane = branch * NSA_HEADS + h
                col = gts[:, lane:lane + 1]
                for g in range(1, NSA_KV_HEADS):
                    lg = lane + g * NSA_GROUP
                    col = jnp.where(grp == g, gts[:, lg:lg + 1], col)
                out[branch][h] = zs * col
                yield
        return out

    (o_c, unsel), o_w, gz = _interleave([compressed_and_selection(), window(), gates()])
    o_cw = [gz[0][h] * o_c[h * TQ:(h + 1) * TQ] + gz[2][h] * o_w[h * TQ:(h + 1) * TQ] for h in range(NSA_GROUP)]

    q_sel = jnp.concatenate([q4, jnp.concatenate([unsel] * NSA_GROUP, axis=0)], axis=1)

    def key_tile(j):
        k0 = pl.multiple_of(j * TQ, TQ)
        blk = _div_pow2(j * TQ + _iota((TQ, LANE), 0), SLC_BLOCK)
        marks = jnp.where(_iota((TQ, LANE), 1) == blk, BIG, 0.0).astype(BF16)
        return jnp.concatenate([ks_ref[0, pl.ds(k0, TQ), :].astype(BF16), marks], axis=1)

    assert n_tiles % 4 == 0
    n_pairs = _div_pow2(qi + 2, 2)
    passes, first = [], 0
    for tiles_per_iter in (8, 4, 2):
        if tiles_per_iter <= n_tiles:
            trips = _div_pow2(2 * n_pairs - first, tiles_per_iter)
            passes.append((tiles_per_iter, first, trips))
            first = first + tiles_per_iter * trips

    def slc_logits(tpi, base):
        def body(it, carry):
            tiles = [base + tpi * it + c for c in range(tpi)]
            qk = [_mm_nt(q_sel, key_tile(j)) for j in tiles]
            macc = macc_ref[...]
            for j, qk_j in zip(tiles, qk):
                s = qk_j + _bias_tile(toep_ref, (qi - j) * sub, False)
                s_ref[j] = s
                macc = jnp.maximum(macc, _fold_lanes(s, jnp.maximum))
            macc_ref[...] = macc
            return carry
        return body

    macc_ref[...] = jnp.full((rows, LANE), NEG, F32)
    for tpi, base, trips in passes:
        lax.fori_loop(0, trips, slc_logits(tpi, base), 0)
    m_s = jnp.max(macc_ref[...], axis=-1, keepdims=True)

    def slc_values(tpi, base):
        def body(it, carry):
            j0 = base + tpi * it
            k0 = pl.multiple_of(j0 * TQ, 2 * TQ)
            p_it = jnp.concatenate([jnp.exp2(s_ref[j0 + c] - m_s).astype(BF16) for c in range(tpi)], axis=1)
            acc_ref[...] += _mm(p_it, _with_ones(vs_ref[0, pl.ds(k0, tpi * TQ), :].astype(BF16)))
            return carry
        return body

    acc_ref[...] = jnp.zeros((rows, 2 * HEAD_DIM), F32)
    for tpi, base, trips in passes:
        lax.fori_loop(0, trips, slc_values(tpi, base), 0)
    acc = acc_ref[...]
    o_s = acc[:, :HEAD_DIM] / jnp.maximum(acc[:, HEAD_DIM:], 1e-30)

    for h in range(NSA_GROUP):
        o = o_cw[h] + gz[1][h] * o_s[h * TQ:(h + 1) * TQ]
        o_ref[0, :, h * HEAD_DIM:(h + 1) * HEAD_DIM] = o.astype(o_ref.dtype)


def _nsa(proj3, bias_c, toep, posk, w1k, w2k, posv, w1v, w2v):
    b, s, _ = proj3.shape
    nr = s // CMP_STRIDE
    gq = NSA_GROUP * HEAD_DIM

    def kvspec(idx):
        return pl.BlockSpec((1, s, HEAD_DIM), lambda bi, g, qi, idx=idx: (bi, 0, P_KV // HEAD_DIM + 2 * idx + g))

    def whole(a):
        return pl.BlockSpec(a.shape, lambda bi, g, qi, nd=a.ndim: (0,) * nd)

    in_specs = [
        pl.BlockSpec((1, TQ, gq), lambda bi, g, qi: (bi, qi, P_Q // gq + g)),
        kvspec(0), kvspec(1), kvspec(2), kvspec(3), kvspec(4), kvspec(5),
        pl.BlockSpec((1, TQ, LANE), lambda bi, g, qi: (bi, qi, P_GATE // LANE)),
        pl.BlockSpec((1, TQ, gq), lambda bi, g, qi: (bi, qi, P_ZA // gq + g)),
        pl.BlockSpec((NSA_GROUP, TQ, nr), lambda bi, g, qi: (g, qi, 0)),
        pl.BlockSpec((NSA_GROUP, NT_ALL, TB, TB), lambda bi, g, qi: (g, 0, 0, 0)),
        whole(posk), whole(w1k), whole(w2k), whole(posv), whole(w1v), whole(w2v),
    ]
    return pl.pallas_call(
        _nsa_kernel,
        grid=(b, NSA_KV_HEADS, s // TQ),
        in_specs=in_specs,
        out_specs=pl.BlockSpec((1, TQ, gq), lambda bi, g, qi: (bi, qi, g)),
        out_shape=jax.ShapeDtypeStruct((b, s, NSA_WIDTH), BF16),
        scratch_shapes=[pltpu.VMEM((nr, HEAD_DIM), BF16), pltpu.VMEM((nr, HEAD_DIM), BF16),
                        pltpu.VMEM((s // TQ, NSA_GROUP * TQ, TQ), F32),
                        pltpu.VMEM((NSA_GROUP * TQ, LANE), F32),
                        pltpu.VMEM((NSA_GROUP * TQ, 2 * HEAD_DIM), F32)],
        compiler_params=pltpu.CompilerParams(
            dimension_semantics=("parallel", "parallel", "arbitrary"),
            vmem_limit_bytes=56 * 1024 * 1024),
        name="nsa",
    )(proj3, proj3, proj3, proj3, proj3, proj3, proj3, proj3, proj3,
      bias_c, toep, posk, w1k, w2k, posv, w1v, w2v)


def _shift_mix(ref, prev_ref, mu, sl):
    x = ref[0, :, sl]
    prev = jnp.where(_iota(x.shape, 0) == 0, prev_ref[:, sl], pltpu.roll(x, 1, 0))
    prev_ref[:, sl] = x[x.shape[0] - 1:]
    return x + mu * (prev - x)


def _rwkv_kernel(r_ref, k_ref, v_ref, wa_ref, zb_ref, vec_ref, muwa_ref, w2_ref, a2_ref,
                 o_ref, st_ref, pr_ref, pk_ref, pv_ref, pwa_ref):
    first = pl.program_id(2) == 0
    tb = r_ref.shape[1]
    n_groups = r_ref.shape[2] // GW
    n_chunks = tb // CH

    @pl.when(first)
    def _():
        for ref in (st_ref, pr_ref, pk_ref, pv_ref, pwa_ref):
            ref[...] = jnp.zeros_like(ref)

    wa = _shift_mix(wa_ref, pwa_ref, muwa_ref[...], slice(0, LANE))
    wd_act = jnp.tanh(wa[:, :LORA]).astype(BF16)
    ad = wa[:, LORA:].astype(BF16)

    seg = (_div_pow2(_iota((GW, GW), 0), RWKV_HEAD_DIM) == _div_pow2(_iota((GW, GW), 1), RWKV_HEAD_DIM))
    segf = seg.astype(F32)
    segb = seg.astype(BF16)
    assert CH == RWKV_HEAD_DIM
    lane_s = _mod_pow2(_iota((CH, GW), 1), CH)
    row_t = _iota((CH, GW), 0)
    strict = lane_s < row_t
    incl = lane_s <= row_t
    eye = (lane_s == row_t).astype(F32)
    ti_r, ti_c = _iota((tb, tb), 0), _iota((tb, tb), 1)
    trib = ((ti_c <= ti_r) & (_div_pow2(ti_c, CH) == _div_pow2(ti_r, CH))).astype(BF16)

    def bd(x):
        xb = x.astype(BF16)
        return jnp.concatenate([xb] * HPG, axis=0) * segb

    def prep(gi):
        lanes = slice(gi * GW, (gi + 1) * GW)
        vec = vec_ref[:, lanes]
        mu_r, mu_k, mu_v = vec[0:1], vec[1:2], vec[2:3]
        w0, a0, k_k, k_a = vec[3:4], vec[4:5], vec[5:6], vec[6:7]
        r = _shift_mix(r_ref, pr_ref, mu_r, lanes)
        k = _shift_mix(k_ref, pk_ref, mu_k, lanes)
        v = _shift_mix(v_ref, pv_ref, mu_v, lanes)
        w_lora = _mm(wd_act, w2_ref[:, lanes])
        a_lora = _mm(ad, a2_ref[:, lanes])
        kk = k * k_k
        kk_ss = _mm((kk * kk).astype(BF16), segb)
        yield
        lw = jax.nn.sigmoid(w0 + w_lora) * (-math.exp(-0.5) * LOG2E)
        cum = _mm_split_rhs(trib, lw)
        yield
        a_sig = jax.nn.sigmoid(a0 + a_lora)
        kk = kk * lax.rsqrt(jnp.maximum(kk_ss, 1e-24))
        k = k * (1.0 + (a_sig - 1.0) * k_a)
        return dict(r=r, k=k, v=v, a=-kk, b=kk * a_sig, lw=lw, cum=cum, vec=vec, lanes=lanes)

    groups = _interleave([prep(gi) for gi in range(n_groups)])

    def chunk_local(g, c):
        ts = slice(c * CH, (c + 1) * CH)
        rc, kc, vc, ac, bc, lwc, cum = (g[n][ts] for n in ("r", "k", "v", "a", "b", "lw", "cum"))
        tot = cum[CH - 1:CH]
        e_out = jnp.exp2(-cum)
        e_end = jnp.exp2(tot - cum)
        r_t = rc * jnp.exp2(cum)
        a_t = ac * jnp.exp2(cum - lwc)
        lhs = jnp.concatenate([a_t, r_t], axis=0).astype(BF16)
        aa = _mm_nt(lhs, jnp.concatenate([bd(bc * e_out), bd(kc * e_out)], axis=0))
        yield
        a_ab = jnp.where(strict, aa[:CH, :GW], 0.0)
        a_ak = jnp.where(strict, aa[:CH, GW:], 0.0)
        a_rb = jnp.where(incl, aa[CH:, :GW], 0.0)
        a_rk = jnp.where(incl, aa[CH:, GW:], 0.0)
        t_inv = eye + a_ab
        mpow = _mm(a_ab.astype(BF16), bd(a_ab))
        av = _mm(a_ak.astype(BF16), bd(vc))
        yield
        for _ in range(int(math.log2(CH)) - 1):
            res = _mm(jnp.concatenate([t_inv, mpow], axis=0).astype(BF16), bd(mpow))
            yield
            t_inv = t_inv + res[:CH]
            mpow = res[CH:]
        wu = _mm(t_inv.astype(BF16), jnp.concatenate([bd(a_t), bd(av)], axis=1))
        yield
        return dict(
            lhs=jnp.concatenate([wu[:, :GW], r_t], axis=0).astype(BF16), u_loc=wu[:, GW:],
            a_r=jnp.concatenate([a_rb, a_rk], axis=1).astype(BF16), bdv=bd(vc), vc=vc,
            bk_end=jnp.concatenate([bc * e_end, kc * e_end], axis=0).astype(BF16), dec=jnp.exp2(tot))

    loc = _interleave([chunk_local(g, c) for g in groups for c in range(n_chunks)])

    def chain(gi):
        ys = []
        g_state = st_ref[gi]
        for c in range(n_chunks):
            lc = loc[gi * n_chunks + c]
            x0 = _mm_nt(lc["lhs"], g_state.astype(BF16))
            yield
            u = x0[:CH] + lc["u_loc"]
            y_c = _mm(lc["a_r"], jnp.concatenate([bd(u), lc["bdv"]], axis=0))
            upd = _mm_tn(jnp.concatenate([u, lc["vc"]], axis=0).astype(BF16), lc["bk_end"])
            yield
            ys.append(x0[CH:] + y_c)
            g_state = g_state * lc["dec"] + upd * segf
        st_ref[gi] = g_state
        return jnp.concatenate(ys, axis=0)

    ys = _interleave([chain(gi) for gi in range(n_groups)])

    def finish(g, y):
        vec = g["vec"]
        ln_w, ln_b, r_k = vec[7:8], vec[8:9], vec[9:10]
        inv_n = 1.0 / RWKV_HEAD_DIM
        mean = _mm(y.astype(BF16), segb) * inv_n
        bonus = _mm((g["r"] * g["k"] * r_k).astype(BF16), segb) * g["v"]
        yield
        yc = y - mean
        var = _mm((yc * yc).astype(BF16), segb) * inv_n
        yield
        yn = yc * lax.rsqrt(var + RWKV_GN_EPS) * ln_w + ln_b
        zb = zb_ref[0, :, g["lanes"]]
        o_ref[0, :, g["lanes"]] = ((yn + bonus) * (zb * jax.nn.sigmoid(zb))).astype(o_ref.dtype)

    _interleave([finish(g, y) for g, y in zip(groups, ys)])


def _rwkv(proj3, vecs, mu_wa, w2, a2, tb, gps):
    b, s, _ = proj3.shape
    gw = gps * GW
    ng = RWKV_WIDTH // gw

    def col(off):
        return pl.BlockSpec((1, tb, gw), lambda bi, g, ti, off=off: (bi, ti, off // gw + g))

    in_specs = [
        col(P_RKV), col(P_RKV + RWKV_WIDTH), col(P_RKV + 2 * RWKV_WIDTH),
        pl.BlockSpec((1, tb, LANE), lambda bi, g, ti: (bi, ti, P_WDAD // LANE)),
        col(P_ZB),
        pl.BlockSpec((vecs.shape[0], gw), lambda bi, g, ti: (0, g)),
        pl.BlockSpec((1, LANE), lambda bi, g, ti: (0, 0)),
        pl.BlockSpec((LORA, gw), lambda bi, g, ti: (0, g)),
        pl.BlockSpec((LORA, gw), lambda bi, g, ti: (0, g)),
    ]
    return pl.pallas_call(
        _rwkv_kernel,
        grid=(b, ng, s // tb),
        in_specs=in_specs,
        out_specs=pl.BlockSpec((1, tb, gw), lambda bi, g, ti: (bi, ti, g)),
        out_shape=jax.ShapeDtypeStruct((b, s, RWKV_WIDTH), BF16),
        scratch_shapes=[pltpu.VMEM((gps, GW, GW), F32), pltpu.VMEM((1, gw), F32), pltpu.VMEM((1, gw), F32),
                        pltpu.VMEM((1, gw), F32), pltpu.VMEM((1, LANE), F32)],
        compiler_params=pltpu.CompilerParams(
            dimension_semantics=("parallel", "parallel", "arbitrary")),
        name="rwkv",
    )(proj3, proj3, proj3, proj3, proj3, vecs, mu_wa, w2, a2)


def _outproj_kernel(ma_ref, mb_ref, wa_ref, wb_ref, x_ref, g_ref, o_ref):
    y = _mm(ma_ref[...], wa_ref[...]) + _mm(mb_ref[...], wb_ref[...])
    ms = jnp.mean(y * y, axis=-1, keepdims=True)
    o_ref[...] = x_ref[...] + y * lax.rsqrt(ms + NORM_EPS) * g_ref[...]


def _outproj(mix_a, mix_b, w_a, w_b, x2, g, tm):
    m, d = x2.shape
    ka, kb = mix_a.shape[1], mix_b.shape[1]
    return pl.pallas_call(
        _outproj_kernel,
        grid=(m // tm,),
        in_specs=[
            pl.BlockSpec((tm, ka), lambda i: (i, 0)),
            pl.BlockSpec((tm, kb), lambda i: (i, 0)),
            pl.BlockSpec((ka, d), lambda i: (0, 0)),
            pl.BlockSpec((kb, d), lambda i: (0, 0)),
            pl.BlockSpec((tm, d), lambda i: (i, 0)),
            pl.BlockSpec((1, d), lambda i: (0, 0)),
        ],
        out_specs=pl.BlockSpec((tm, d), lambda i: (i, 0)),
        out_shape=jax.ShapeDtypeStruct((m, d), F32),
        compiler_params=pltpu.CompilerParams(
            dimension_semantics=("parallel",), vmem_limit_bytes=56 * 1024 * 1024),
        name="outproj",
    )(mix_a, mix_b, w_a, w_b, x2, g)


_W_SEGMENTS = (
    (P_Q, R_Q, NSA_WIDTH),
    (P_RKV, R_FEAT, 3 * RWKV_WIDTH),
    (P_ZB, R_ZB, RWKV_WIDTH),
    (P_ZA, R_ZA, NSA_WIDTH),
    (P_KV, R_KV, 6 * NSA_KV_HEADS * HEAD_DIM),
    (P_WDAD, R_FEAT + 3 * RWKV_WIDTH, 2 * LORA),
    (P_GATE, R_GATE, LANE),
)


def _relayout_kernel(w_ref, o_ref):
    for dst, src, width in _W_SEGMENTS:
        o_ref[:, dst:dst + width] = w_ref[:, src:src + width].astype(BF16)
    used = P_GATE + LANE
    o_ref[:, used:] = jnp.zeros((o_ref.shape[0], NP - used), BF16)


def _permute_w_in(w, rows):
    d, n = w.shape
    return pl.pallas_call(
        _relayout_kernel,
        grid=(d // rows,),
        in_specs=[pl.BlockSpec((rows, n), lambda i: (i, 0))],
        out_specs=pl.BlockSpec((rows, NP), lambda i: (i, 0)),
        out_shape=jax.ShapeDtypeStruct((d, NP), BF16),
        compiler_params=pltpu.CompilerParams(dimension_semantics=("parallel",)),
        name="relayout",
    )(w)


def _block(x, pre_norm_g, w_in, rel_bias_table, cmp_pos_k, cmp_pos_v, cmp_k_w1, cmp_k_w2, cmp_v_w1,
           cmp_v_w2, rwkv_mu, rwkv_w0, rwkv_w2, rwkv_a0, rwkv_a2, rwkv_k_k, rwkv_k_a, rwkv_r_k,
           rwkv_ln_w, rwkv_ln_b, w_out, post_norm_g):
    b, s, d = x.shape
    x2 = x.reshape(b * s, d)
    tm = min(1024, b * s)
    proj = _inproj(x2, pre_norm_g.reshape(1, d), _permute_w_in(w_in.astype(BF16), 256), tm, 2048)
    proj3 = proj.reshape(b, s, NP)

    bias_c, toep = _bias(rel_bias_table.reshape(-1), s)
    half = CMP_STRIDE * HEAD_DIM
    mix_a = _nsa(proj3, bias_c, toep,
                 cmp_pos_k.reshape(2, half), cmp_k_w1.astype(BF16), cmp_k_w2.astype(BF16),
                 cmp_pos_v.reshape(2, half), cmp_v_w1.astype(BF16), cmp_v_w2.astype(BF16))

    w3 = 3 * RWKV_WIDTH
    vec_rows = [rwkv_mu[:RWKV_WIDTH], rwkv_mu[RWKV_WIDTH:2 * RWKV_WIDTH], rwkv_mu[2 * RWKV_WIDTH:w3],
                rwkv_w0, rwkv_a0, rwkv_k_k, rwkv_k_a, rwkv_ln_w, rwkv_ln_b, rwkv_r_k.reshape(-1)]
    vecs = jnp.stack(vec_rows + [jnp.zeros_like(rwkv_w0)] * (16 - len(vec_rows)), axis=0)
    mix_b = _rwkv(proj3, vecs, rwkv_mu[w3:].reshape(1, 2 * LORA), rwkv_w2.astype(BF16),
                  rwkv_a2.astype(BF16), min(256, s), RWKV_GROUPS_PER_STEP)

    w_o = w_out.astype(BF16)
    out = _outproj(mix_a.reshape(b * s, NSA_WIDTH), mix_b.reshape(b * s, RWKV_WIDTH),
                   w_o[:NSA_WIDTH], w_o[NSA_WIDTH:], x2, post_norm_g.reshape(1, d), min(512, b * s))
    return out.reshape(b, s, d)


def kernel(x, pre_norm_g, w_in, rel_bias_table, cmp_pos_k, cmp_pos_v, cmp_k_w1, cmp_k_w2, cmp_v_w1,
           cmp_v_w2, rwkv_mu, rwkv_w0, rwkv_w2, rwkv_a0, rwkv_a2, rwkv_k_k, rwkv_k_a, rwkv_r_k,
           rwkv_ln_w, rwkv_ln_b, w_out, post_norm_g):
    h = x
    for l in range(pre_norm_g.shape[0]):
        h = _block(h, pre_norm_g[l], w_in[l], rel_bias_table, cmp_pos_k[l], cmp_pos_v[l], cmp_k_w1[l],
                   cmp_k_w2[l], cmp_v_w1[l], cmp_v_w2[l], rwkv_mu[l], rwkv_w0[l], rwkv_w2[l],
                   rwkv_a0[l], rwkv_a2[l], rwkv_k_k[l], rwkv_k_a[l], rwkv_r_k[l], rwkv_ln_w[l],
                   rwkv_ln_b[l], w_out[l], post_norm_g[l])
    return h
```

```python
import math

import jax
import jax.numpy as jnp
from jax import lax
from jax.experimental import pallas as pl
from jax.experimental.pallas import tpu as pltpu

F32 = jnp.float32
BF16 = jnp.bfloat16

D_MODEL = 2048
NSA_HEADS = 8
NSA_KV_HEADS = 2
NSA_GROUP = NSA_HEADS // NSA_KV_HEADS
HEAD_DIM = 128
NSA_WIDTH = NSA_HEADS * HEAD_DIM
CMP_BLOCK = 32
CMP_STRIDE = 16
SLC_BLOCK = 64
SLC_TOP_N = 16
WINDOW = 512
RWKV_WIDTH = 1024
RWKV_HEAD_DIM = 64
RWKV_HEADS = RWKV_WIDTH // RWKV_HEAD_DIM
LORA = 64
NUM_BUCKETS = 32
MAX_DISTANCE = 1024
NORM_EPS = 1e-6
RWKV_GN_EPS = 64e-5

R_Q = 0
R_KV = R_Q + NSA_WIDTH
R_GATE = R_KV + 6 * NSA_KV_HEADS * HEAD_DIM
R_ZA = R_GATE + 3 * NSA_HEADS
R_FEAT = R_ZA + NSA_WIDTH
R_ZB = R_FEAT + 3 * RWKV_WIDTH + 2 * LORA
R_END = R_ZB + RWKV_WIDTH

P_Q = 0
P_RKV = 1024
P_ZB = 4096
P_ZA = 5120
P_KV = 6144
P_WDAD = 7680
P_GATE = 7808
NP = 8192

LANE = 128
TQ = 256
TB = 128
ND = 9
T_DIAG = ND
T_WEND = ND + 1
T_NONE = ND + 2
NT_ALL = ND + 3
LOG2E = math.log2(math.e)
BIG = 2.0 ** 100
CH = 64
HPG = 4
GW = HPG * RWKV_HEAD_DIM
RWKV_GROUPS_PER_STEP = 4
NEG = -1e30


def _bucket_thresholds():
    out = []
    for k in range(1, NUM_BUCKETS // 2):
        n = 16
        while n ** 8 < (16 ** 8) * (2 ** (3 * k)):
            n += 1
        out.append(n)
    return out


_THR = _bucket_thresholds()


def _mm(a, b):
    return jnp.dot(a, b, preferred_element_type=F32)


def _mm_nt(a, b):
    return lax.dot_general(a, b, (((1,), (1,)), ((), ())), preferred_element_type=F32)


def _mm_tn(a, b):
    return lax.dot_general(a, b, (((0,), (0,)), ((), ())), preferred_element_type=F32)


def _split3(x):
    x1 = x.astype(BF16)
    r1 = x - x1.astype(F32)
    x2 = r1.astype(BF16)
    x3 = (r1 - x2.astype(F32)).astype(BF16)
    return x1, x2, x3


def _mm_split_rhs(a_exact, b):
    b1 = b.astype(BF16)
    b2 = (b - b1.astype(F32)).astype(BF16)
    return _mm(a_exact, b1) + _mm(a_exact, b2)


def _iota(shape, dim):
    return lax.broadcasted_iota(jnp.int32, shape, dim)


def _interleave(gens):
    results = [None] * len(gens)
    live = list(enumerate(gens))
    while live:
        still = []
        for i, g in live:
            try:
                next(g)
                still.append((i, g))
            except StopIteration as stop:
                results[i] = stop.value
        live = still
    return results


def _div_pow2(x, n):
    assert n & (n - 1) == 0
    return x >> (n.bit_length() - 1)


def _mod_pow2(x, n):
    assert n & (n - 1) == 0
    return x & (n - 1)


def _inproj_kernel(x_ref, g_ref, w_ref, o_ref, hn_ref):
    @pl.when(pl.program_id(1) == 0)
    def _():
        x = x_ref[...]
        ms = jnp.mean(x * x, axis=-1, keepdims=True)
        hn_ref[...] = (x * lax.rsqrt(ms + NORM_EPS) * g_ref[...]).astype(BF16)

    o_ref[...] = _mm(hn_ref[...], w_ref[...])


def _inproj(x2, g, w, tm, tn):
    m, d = x2.shape
    n = w.shape[1]
    return pl.pallas_call(
        _inproj_kernel,
        grid=(m // tm, n // tn),
        in_specs=[
            pl.BlockSpec((tm, d), lambda i, j: (i, 0)),
            pl.BlockSpec((1, d), lambda i, j: (0, 0)),
            pl.BlockSpec((d, tn), lambda i, j: (0, j)),
        ],
        out_specs=pl.BlockSpec((tm, tn), lambda i, j: (i, j)),
        out_shape=jax.ShapeDtypeStruct((m, n), F32),
        scratch_shapes=[pltpu.VMEM((tm, d), BF16)],
        compiler_params=pltpu.CompilerParams(
            dimension_semantics=("parallel", "arbitrary"),
            vmem_limit_bytes=56 * 1024 * 1024),
        name="inproj",
    )(x2, g, w)


def _bucket(n):
    n = jnp.maximum(n, 0)
    large = jnp.full(n.shape, NUM_BUCKETS // 2, jnp.int32)
    for thr in _THR:
        large = large + (n >= thr).astype(jnp.int32)
    return jnp.where(n < NUM_BUCKETS // 2, n, large)


def _lookup_all_heads(dist, tab_ref):
    bucket = _bucket(dist)
    hits = [bucket == b for b in range(NUM_BUCKETS)]
    outs = []
    for h in range(NSA_HEADS):
        out = jnp.zeros(dist.shape, F32)
        for b in range(NUM_BUCKETS):
            out = jnp.where(hits[b], tab_ref[b * NSA_HEADS + h] * LOG2E, out)
        outs.append(out)
    return outs


def _bias_kernel(tab_ref, bc_ref, tp_ref):
    i = pl.program_id(0)
    rows, nr = bc_ref.shape[1], bc_ref.shape[2]
    dist_c = (i * rows + _iota((rows, nr), 0)) - (_iota((rows, nr), 1) * CMP_STRIDE + (CMP_BLOCK - 1))
    for h, vals in enumerate(_lookup_all_heads(dist_c, tab_ref)):
        bc_ref[h] = vals

    @pl.when(i == 0)
    def _():
        base = _iota((TB, TB), 0) - _iota((TB, TB), 1)
        neg = jnp.full((TB, TB), NEG, F32)
        for h in range(NSA_HEADS):
            tp_ref[h, T_NONE] = neg
        for d in range(ND):
            for h, vals in enumerate(_lookup_all_heads(base + d * TB, tab_ref)):
                tp_ref[h, d] = vals
                if d == 0:
                    tp_ref[h, T_DIAG] = jnp.where(base >= 0, vals, neg)
                if d == WINDOW // TB:
                    tp_ref[h, T_WEND] = jnp.where(base < 0, vals, neg)


def _bias(table_flat, s):
    nr = s // CMP_STRIDE
    rows = min(512, s)
    return pl.pallas_call(
        _bias_kernel,
        grid=(s // rows,),
        in_specs=[pl.BlockSpec(memory_space=pltpu.SMEM)],
        out_specs=[
            pl.BlockSpec((NSA_HEADS, rows, nr), lambda i: (0, i, 0)),
            pl.BlockSpec((NSA_HEADS, NT_ALL, TB, TB), lambda i: (0, 0, 0, 0)),
        ],
        out_shape=[
            jax.ShapeDtypeStruct((NSA_HEADS, s, nr), F32),
            jax.ShapeDtypeStruct((NSA_HEADS, NT_ALL, TB, TB), F32),
        ],
        compiler_params=pltpu.CompilerParams(dimension_semantics=("arbitrary",)),
        name="bias",
    )(table_flat)


def _compress(kv_ref, pos_ref, w1_ref, w2_ref, nr):
    half = CMP_STRIDE * HEAD_DIM
    r = jnp.concatenate(
        [kv_ref[0, pl.ds(m, nr, stride=CMP_STRIDE), :] for m in range(CMP_STRIDE)], axis=1)
    a = _mm((r + pos_ref[0:1, :]).astype(BF16), w1_ref[0:half, :])
    b = _mm((r + pos_ref[1:2, :]).astype(BF16), w1_ref[half:2 * half, :])
    pre = a + pltpu.roll(b, nr - 1, 0)
    h1 = pre * jax.nn.sigmoid(pre)
    return _mm(h1.astype(BF16), w2_ref[...])


def _bias_head(toep_ref, h, d_tiles, window, valid=True):
    sub = TQ // TB
    rows = []
    for ri in range(sub):
        cols = []
        for ci in range(sub):
            d = d_tiles + ri - ci
            idx = jnp.where(d == 0, T_DIAG, jnp.minimum(d, ND - 1))
            if window:
                idx = jnp.where(d == WINDOW // TB, T_WEND, jnp.where(d > WINDOW // TB, T_NONE, idx))
            idx = jnp.where((d < 0) | jnp.logical_not(valid), T_NONE, idx)
            cols.append(toep_ref[h, idx])
        rows.append(jnp.concatenate(cols, axis=1))
    return jnp.concatenate(rows, axis=0)


def _bias_tile(toep_ref, d_tiles, window, valid=True):
    return jnp.concatenate([_bias_head(toep_ref, h, d_tiles, window, valid) for h in range(NSA_GROUP)], axis=0)


def _add_shared(s, mask_add):
    n = s.shape[-1]
    return (s.reshape(NSA_GROUP, TQ, n) + mask_add[None]).reshape(NSA_GROUP * TQ, n)


def _with_ones(v):
    return jnp.concatenate([v, jnp.ones(v.shape, v.dtype)], axis=1)


def _fold_lanes(x, op):
    out = x[:, :LANE]
    for c in range(1, x.shape[1] // LANE):
        out = op(out, x[:, c * LANE:(c + 1) * LANE])
    return out


def _nsa_kernel(q_ref, kc_ref, vc_ref, ks_ref, vs_ref, kw_ref, vw_ref, gate_ref, za_ref,
                bc_ref, toep_ref, posk_ref, w1k_ref, w2k_ref, posv_ref, w1v_ref, w2v_ref,
                o_ref, kcs_ref, vcs_ref, s_ref, macc_ref, acc_ref):
    qi = pl.program_id(2)
    s_len = kc_ref.shape[1]
    nr = s_len // CMP_STRIDE
    nb = s_len // SLC_BLOCK
    n_sel = min(SLC_TOP_N, nb)

    @pl.when(qi == 0)
    def _():
        kcs_ref[...] = _compress(kc_ref, posk_ref, w1k_ref, w2k_ref, nr).astype(BF16)
        vcs_ref[...] = _compress(vc_ref, posv_ref, w1v_ref, w2v_ref, nr).astype(BF16)

    q0 = qi * TQ
    rows = NSA_GROUP * TQ
    q = q_ref[0] * (HEAD_DIM ** -0.5 * LOG2E)
    q4 = jnp.concatenate([q[:, h * HEAD_DIM:(h + 1) * HEAD_DIM] for h in range(NSA_GROUP)],
                         axis=0).astype(BF16)

    sub = TQ // TB
    n_tiles = s_len // TQ

    def compressed_and_selection():
        t_c = q0 + _iota((TQ, nr), 0)
        i_c = _iota((TQ, nr), 1)
        mask_c = (t_c - (i_c * CMP_STRIDE + (CMP_BLOCK - 1)) >= 0) & (i_c < nr - 1)
        qk = _mm_nt(q4, kcs_ref[...])
        mask_add = jnp.where(mask_c, 0.0, NEG)
        yield
        p_heads = []
        for h in range(NSA_GROUP):
            lg = qk[h * TQ:(h + 1) * TQ] + bc_ref[h] + mask_add
            m_c = jnp.max(lg, axis=-1, keepdims=True)
            yield
            e = jnp.where(lg > 0.5 * NEG, jnp.exp2(lg - m_c), 0.0)
            l_c = jnp.sum(e, axis=-1, keepdims=True)
            yield
            p_heads.append(e / jnp.maximum(l_c, 1e-30))
        p = jnp.concatenate(p_heads, axis=0)
        o_c = _mm(p.astype(BF16), vcs_ref[...])
        psum = p_heads[0]
        for h in range(1, NSA_GROUP):
            psum = psum + p_heads[h]
        ov_i = _iota((nb, nr), 1) * CMP_STRIDE
        ov_j = _iota((nb, nr), 0) * SLC_BLOCK
        ov_t = ((ov_i < ov_j + SLC_BLOCK) & (ov_i + CMP_BLOCK > ov_j)).astype(BF16)
        p1, p2, p3 = _split3(psum)
        imp_t = _mm_nt(ov_t, p1) + _mm_nt(ov_t, p2) + _mm_nt(ov_t, p3)
        yield
        jb = _iota((nb, TQ), 0)
        cur = _div_pow2(q0 + _iota((nb, TQ), 1), SLC_BLOCK)
        forced = (jb == 0) | (jb == cur) | (jb == cur - 1)
        causal = jb <= cur
        score = jnp.where(forced, jnp.inf, jnp.where(causal, imp_t, -jnp.inf))
        rank = jnp.zeros((nb, TQ), jnp.int32)
        for jp in range(nb):
            sj = score[jp:jp + 1, :]
            beats = (sj > score) | ((sj == score) & (jb > jp))
            rank = rank + beats.astype(jnp.int32)
            if jp % 8 == 7:
                yield
        sel_t = ((rank < n_sel) & causal).astype(BF16)
        place = (_iota((nb, LANE), 0) == _iota((nb, LANE), 1)).astype(BF16)
        unsel = _mm_tn(sel_t, place) - (_iota((TQ, LANE), 1) < nb).astype(F32)
        yield
        return o_c, unsel.astype(BF16)

    def window():
        n_band = WINDOW // TQ + 1
        qk, tiles, v_w = [], [], []
        for c in range(n_band):
            j = qi - (n_band - 1) + c
            jc = jnp.maximum(j, 0)
            k0 = pl.multiple_of(jc * TQ, TQ)
            tiles.append((j, jc))
            qk.append(_mm_nt(q4, kw_ref[0, pl.ds(k0, TQ), :].astype(BF16)))
            v_w.append(_with_ones(vw_ref[0, pl.ds(k0, TQ), :].astype(BF16)))
        yield
        heads = range(NSA_GROUP)
        s_w = [[None] * NSA_GROUP for _ in range(n_band)]
        m_w = [None] * NSA_GROUP
        for c in range(n_band):
            for h in heads:
                s = qk[c][h * TQ:(h + 1) * TQ] + _bias_head(toep_ref, h, (qi - tiles[c][1]) * sub, True,
                                                             tiles[c][0] >= 0)
                s_w[c][h] = s
                m_w[h] = s if c == 0 else jnp.maximum(m_w[h], s)
                yield
        for h in heads:
            m_w[h] = jnp.max(m_w[h], axis=-1, keepdims=True)
        yield
        v_band = jnp.concatenate(v_w, axis=0)
        acc_w = []
        for h in heads:
            p_h = []
            for c in range(n_band):
                p_h.append(jnp.exp2(s_w[c][h] - m_w[h]).astype(BF16))
                yield
            acc_w.append(_mm(jnp.concatenate(p_h, axis=1), v_band))
            yield
        return jnp.concatenate([a[:, :HEAD_DIM] / jnp.maximum(a[:, HEAD_DIM:], 1e-30) for a in acc_w], axis=0)

    def gates():
        gts = jax.nn.sigmoid(gate_ref[0])
        grp = pl.program_id(1)
        za = za_ref[0]
        out = [[None] * NSA_GROUP for _ in range(3)]
        for h in range(NSA_GROUP):
            z = za[:, h * HEAD_DIM:(h + 1) * HEAD_DIM]
            zs = z * jax.nn.sigmoid(z)
            for branch in range(3):
                lane = branch * NSA_HEADS + h
                col = gts[:, lane:lane + 1]
                for g in range(1, NSA_KV_HEADS):
                    lg = lane + g * NSA_GROUP
                    col = jnp.where(grp == g, gts[:, lg:lg + 1], col)
                out[branch][h] = zs * col
                yield
        return out

    (o_c, unsel), o_w, gz = _interleave([compressed_and_selection(), window(), gates()])
    o_cw = [gz[0][h] * o_c[h * TQ:(h + 1) * TQ] + gz[2][h] * o_w[h * TQ:(h + 1) * TQ] for h in range(NSA_GROUP)]

    q_sel = jnp.concatenate([q4, jnp.concatenate([unsel] * NSA_GROUP, axis=0)], axis=1)

    def key_tile(j):
        k0 = pl.multiple_of(j * TQ, TQ)
        blk = _div_pow2(j * TQ + _iota((TQ, LANE), 0), SLC_BLOCK)
        marks = jnp.where(_iota((TQ, LANE), 1) == blk, BIG, 0.0).astype(BF16)
        return jnp.concatenate([ks_ref[0, pl.ds(k0, TQ), :].astype(BF16), marks], axis=1)

    assert n_tiles % 4 == 0
    n_pairs = _div_pow2(qi + 2, 2)
    passes, first = [], 0
    for tiles_per_iter in (8, 4, 2):
        if tiles_per_iter <= n_tiles:
            trips = _div_pow2(2 * n_pairs - first, tiles_per_iter)
            passes.append((tiles_per_iter, first, trips))
            first = first + tiles_per_iter * trips

    def slc_logits(tpi, base):
        def body(it, carry):
            tiles = [base + tpi * it + c for c in range(tpi)]
            qk = [_mm_nt(q_sel, key_tile(j)) for j in tiles]
            macc = macc_ref[...]
            for j, qk_j in zip(tiles, qk):
                s = qk_j + _bias_tile(toep_ref, (qi - j) * sub, False)
                s_ref[j] = s
                macc = jnp.maximum(macc, _fold_lanes(s, jnp.maximum))
            macc_ref[...] = macc
            return carry
        return body

    macc_ref[...] = jnp.full((rows, LANE), NEG, F32)
    for tpi, base, trips in passes:
        lax.fori_loop(0, trips, slc_logits(tpi, base), 0)
    m_s = jnp.max(macc_ref[...], axis=-1, keepdims=True)

    def slc_values(tpi, base):
        def body(it, carry):
            j0 = base + tpi * it
            k0 = pl.multiple_of(j0 * TQ, 2 * TQ)
            p_it = jnp.concatenate([jnp.exp2(s_ref[j0 + c] - m_s).astype(BF16) for c in range(tpi)], axis=1)
            acc_ref[...] += _mm(p_it, _with_ones(vs_ref[0, pl.ds(k0, tpi * TQ), :].astype(BF16)))
            return carry
        return body

    acc_ref[...] = jnp.zeros((rows, 2 * HEAD_DIM), F32)
    for tpi, base, trips in passes:
        lax.fori_loop(0, trips, slc_values(tpi, base), 0)
    acc = acc_ref[...]
    o_s = acc[:, :HEAD_DIM] / jnp.maximum(acc[:, HEAD_DIM:], 1e-30)

    for h in range(NSA_GROUP):
        o = o_cw[h] + gz[1][h] * o_s[h * TQ:(h + 1) * TQ]
        o_ref[0, :, h * HEAD_DIM:(h + 1) * HEAD_DIM] = o.astype(o_ref.dtype)


def _nsa(proj3, bias_c, toep, posk, w1k, w2k, posv, w1v, w2v):
    b, s, _ = proj3.shape
    nr = s // CMP_STRIDE
    gq = NSA_GROUP * HEAD_DIM

    def kvspec(idx):
        return pl.BlockSpec((1, s, HEAD_DIM), lambda bi, g, qi, idx=idx: (bi, 0, P_KV // HEAD_DIM + 2 * idx + g))

    def whole(a):
        return pl.BlockSpec(a.shape, lambda bi, g, qi, nd=a.ndim: (0,) * nd)

    in_specs = [
        pl.BlockSpec((1, TQ, gq), lambda bi, g, qi: (bi, qi, P_Q // gq + g)),
        kvspec(0), kvspec(1), kvspec(2), kvspec(3), kvspec(4), kvspec(5),
        pl.BlockSpec((1, TQ, LANE), lambda bi, g, qi: (bi, qi, P_GATE // LANE)),
        pl.BlockSpec((1, TQ, gq), lambda bi, g, qi: (bi, qi, P_ZA // gq + g)),
        pl.BlockSpec((NSA_GROUP, TQ, nr), lambda bi, g, qi: (g, qi, 0)),
        pl.BlockSpec((NSA_GROUP, NT_ALL, TB, TB), lambda bi, g, qi: (g, 0, 0, 0)),
        whole(posk), whole(w1k), whole(w2k), whole(posv), whole(w1v), whole(w2v),
    ]
    return pl.pallas_call(
        _nsa_kernel,
        grid=(b, NSA_KV_HEADS, s // TQ),
        in_specs=in_specs,
        out_specs=pl.BlockSpec((1, TQ, gq), lambda bi, g, qi: (bi, qi, g)),
        out_shape=jax.ShapeDtypeStruct((b, s, NSA_WIDTH), BF16),
        scratch_shapes=[pltpu.VMEM((nr, HEAD_DIM), BF16), pltpu.VMEM((nr, HEAD_DIM), BF16),
                        pltpu.VMEM((s // TQ, NSA_GROUP * TQ, TQ), F32),
                        pltpu.VMEM((NSA_GROUP * TQ, LANE), F32),
                        pltpu.VMEM((NSA_GROUP * TQ, 2 * HEAD_DIM), F32)],
        compiler_params=pltpu.CompilerParams(
            dimension_semantics=("parallel", "parallel", "arbitrary"),
            vmem_limit_bytes=56 * 1024 * 1024),
        name="nsa",
    )(proj3, proj3, proj3, proj3, proj3, proj3, proj3, proj3, proj3,
      bias_c, toep, posk, w1k, w2k, posv, w1v, w2v)


def _shift_mix(ref, prev_ref, mu, sl):
    x = ref[0, :, sl]
    prev = jnp.where(_iota(x.shape, 0) == 0, prev_ref[:, sl], pltpu.roll(x, 1, 0))
    prev_ref[:, sl] = x[x.shape[0] - 1:]
    return x + mu * (prev - x)


def _rwkv_kernel(r_ref, k_ref, v_ref, wa_ref, zb_ref, vec_ref, muwa_ref, w2_ref, a2_ref,
                 o_ref, st_ref, pr_ref, pk_ref, pv_ref, pwa_ref):
    first = pl.program_id(2) == 0
    tb = r_ref.shape[1]
    n_groups = r_ref.shape[2] // GW
    n_chunks = tb // CH

    @pl.when(first)
    def _():
        for ref in (st_ref, pr_ref, pk_ref, pv_ref, pwa_ref):
            ref[...] = jnp.zeros_like(ref)

    wa = _shift_mix(wa_ref, pwa_ref, muwa_ref[...], slice(0, LANE))
    wd_act = jnp.tanh(wa[:, :LORA]).astype(BF16)
    ad = wa[:, LORA:].astype(BF16)

    seg = (_div_pow2(_iota((GW, GW), 0), RWKV_HEAD_DIM) == _div_pow2(_iota((GW, GW), 1), RWKV_HEAD_DIM))
    segf = seg.astype(F32)
    segb = seg.astype(BF16)
    assert CH == RWKV_HEAD_DIM
    lane_s = _mod_pow2(_iota((CH, GW), 1), CH)
    row_t = _iota((CH, GW), 0)
    strict = lane_s < row_t
    incl = lane_s <= row_t
    eye = (lane_s == row_t).astype(F32)
    ti_r, ti_c = _iota((tb, tb), 0), _iota((tb, tb), 1)
    trib = ((ti_c <= ti_r) & (_div_pow2(ti_c, CH) == _div_pow2(ti_r, CH))).astype(BF16)

    def bd(x):
        xb = x.astype(BF16)
        return jnp.concatenate([xb] * HPG, axis=0) * segb

    def prep(gi):
        lanes = slice(gi * GW, (gi + 1) * GW)
        vec = vec_ref[:, lanes]
        mu_r, mu_k, mu_v = vec[0:1], vec[1:2], vec[2:3]
        w0, a0, k_k, k_a = vec[3:4], vec[4:5], vec[5:6], vec[6:7]
        r = _shift_mix(r_ref, pr_ref, mu_r, lanes)
        k = _shift_mix(k_ref, pk_ref, mu_k, lanes)
        v = _shift_mix(v_ref, pv_ref, mu_v, lanes)
        w_lora = _mm(wd_act, w2_ref[:, lanes])
        a_lora = _mm(ad, a2_ref[:, lanes])
        kk = k * k_k
        kk_ss = _mm((kk * kk).astype(BF16), segb)
        yield
        lw = jax.nn.sigmoid(w0 + w_lora) * (-math.exp(-0.5) * LOG2E)
        cum = _mm_split_rhs(trib, lw)
        yield
        a_sig = jax.nn.sigmoid(a0 + a_lora)
        kk = kk * lax.rsqrt(jnp.maximum(kk_ss, 1e-24))
        k = k * (1.0 + (a_sig - 1.0) * k_a)
        return dict(r=r, k=k, v=v, a=-kk, b=kk * a_sig, lw=lw, cum=cum, vec=vec, lanes=lanes)

    groups = _interleave([prep(gi) for gi in range(n_groups)])

    def chunk_local(g, c):
        ts = slice(c * CH, (c + 1) * CH)
        rc, kc, vc, ac, bc, lwc, cum = (g[n][ts] for n in ("r", "k", "v", "a", "b", "lw", "cum"))
        tot = cum[CH - 1:CH]
        e_out = jnp.exp2(-cum)
        e_end = jnp.exp2(tot - cum)
        r_t = rc * jnp.exp2(cum)
        a_t = ac * jnp.exp2(cum - lwc)
        lhs = jnp.concatenate([a_t, r_t], axis=0).astype(BF16)
        aa = _mm_nt(lhs, jnp.concatenate([bd(bc * e_out), bd(kc * e_out)], axis=0))
        yield
        a_ab = jnp.where(strict, aa[:CH, :GW], 0.0)
        a_ak = jnp.where(strict, aa[:CH, GW:], 0.0)
        a_rb = jnp.where(incl, aa[CH:, :GW], 0.0)
        a_rk = jnp.where(incl, aa[CH:, GW:], 0.0)
        t_inv = eye + a_ab
        mpow = _mm(a_ab.astype(BF16), bd(a_ab))
        av = _mm(a_ak.astype(BF16), bd(vc))
        yield
        for _ in range(int(math.log2(CH)) - 1):
            res = _mm(jnp.concatenate([t_inv, mpow], axis=0).astype(BF16), bd(mpow))
            yield
            t_inv = t_inv + res[:CH]
            mpow = res[CH:]
        wu = _mm(t_inv.astype(BF16), jnp.concatenate([bd(a_t), bd(av)], axis=1))
        yield
        return dict(
            lhs=jnp.concatenate([wu[:, :GW], r_t], axis=0).astype(BF16), u_loc=wu[:, GW:],
            a_r=jnp.concatenate([a_rb, a_rk], axis=1).astype(BF16), bdv=bd(vc), vc=vc,
            bk_end=jnp.concatenate([bc * e_end, kc * e_end], axis=0).astype(BF16), dec=jnp.exp2(tot))

    loc = _interleave([chunk_local(g, c) for g in groups for c in range(n_chunks)])

    def chain(gi):
        ys = []
        g_state = st_ref[gi]
        for c in range(n_chunks):
            lc = loc[gi * n_chunks + c]
            x0 = _mm_nt(lc["lhs"], g_state.astype(BF16))
            yield
            u = x0[:CH] + lc["u_loc"]
            y_c = _mm(lc["a_r"], jnp.concatenate([bd(u), lc["bdv"]], axis=0))
            upd = _mm_tn(jnp.concatenate([u, lc["vc"]], axis=0).astype(BF16), lc["bk_end"])
            yield
            ys.append(x0[CH:] + y_c)
            g_state = g_state * lc["dec"] + upd * segf
        st_ref[gi] = g_state
        return jnp.concatenate(ys, axis=0)

    ys = _interleave([chain(gi) for gi in range(n_groups)])

    def finish(g, y):
        vec = g["vec"]
        ln_w, ln_b, r_k = vec[7:8], vec[8:9], vec[9:10]
        inv_n = 1.0 / RWKV_HEAD_DIM
        mean = _mm(y.astype(BF16), segb) * inv_n
        bonus = _mm((g["r"] * g["k"] * r_k).astype(BF16), segb) * g["v"]
        yield
        yc = y - mean
        var = _mm((yc * yc).astype(BF16), segb) * inv_n
        yield
        yn = yc * lax.rsqrt(var + RWKV_GN_EPS) * ln_w + ln_b
        zb = zb_ref[0, :, g["lanes"]]
        o_ref[0, :, g["lanes"]] = ((yn + bonus) * (zb * jax.nn.sigmoid(zb))).astype(o_ref.dtype)

    _interleave([finish(g, y) for g, y in zip(groups, ys)])


def _rwkv(proj3, vecs, mu_wa, w2, a2, tb, gps):
    b, s, _ = proj3.shape
    gw = gps * GW
    ng = RWKV_WIDTH // gw

    def col(off):
        return pl.BlockSpec((1, tb, gw), lambda bi, g, ti, off=off: (bi, ti, off // gw + g))

    in_specs = [
        col(P_RKV), col(P_RKV + RWKV_WIDTH), col(P_RKV + 2 * RWKV_WIDTH),
        pl.BlockSpec((1, tb, LANE), lambda bi, g, ti: (bi, ti, P_WDAD // LANE)),
        col(P_ZB),
        pl.BlockSpec((vecs.shape[0], gw), lambda bi, g, ti: (0, g)),
        pl.BlockSpec((1, LANE), lambda bi, g, ti: (0, 0)),
        pl.BlockSpec((LORA, gw), lambda bi, g, ti: (0, g)),
        pl.BlockSpec((LORA, gw), lambda bi, g, ti: (0, g)),
    ]
    return pl.pallas_call(
        _rwkv_kernel,
        grid=(b, ng, s // tb),
        in_specs=in_specs,
        out_specs=pl.BlockSpec((1, tb, gw), lambda bi, g, ti: (bi, ti, g)),
        out_shape=jax.ShapeDtypeStruct((b, s, RWKV_WIDTH), BF16),
        scratch_shapes=[pltpu.VMEM((gps, GW, GW), F32), pltpu.VMEM((1, gw), F32), pltpu.VMEM((1, gw), F32),
                        pltpu.VMEM((1, gw), F32), pltpu.VMEM((1, LANE), F32)],
        compiler_params=pltpu.CompilerParams(
            dimension_semantics=("parallel", "parallel", "arbitrary")),
        name="rwkv",
    )(proj3, proj3, proj3, proj3, proj3, vecs, mu_wa, w2, a2)


def _outproj_kernel(ma_ref, mb_ref, wa_ref, wb_ref, x_ref, g_ref, o_ref):
    y = _mm(ma_ref[...], wa_ref[...]) + _mm(mb_ref[...], wb_ref[...])
    ms = jnp.mean(y * y, axis=-1, keepdims=True)
    o_ref[...] = x_ref[...] + y * lax.rsqrt(ms + NORM_EPS) * g_ref[...]


def _outproj(mix_a, mix_b, w_a, w_b, x2, g, tm):
    m, d = x2.shape
    ka, kb = mix_a.shape[1], mix_b.shape[1]
    return pl.pallas_call(
        _outproj_kernel,
        grid=(m // tm,),
        in_specs=[
            pl.BlockSpec((tm, ka), lambda i: (i, 0)),
            pl.BlockSpec((tm, kb), lambda i: (i, 0)),
            pl.BlockSpec((ka, d), lambda i: (0, 0)),
            pl.BlockSpec((kb, d), lambda i: (0, 0)),
            pl.BlockSpec((tm, d), lambda i: (i, 0)),
            pl.BlockSpec((1, d), lambda i: (0, 0)),
        ],
        out_specs=pl.BlockSpec((tm, d), lambda i: (i, 0)),
        out_shape=jax.ShapeDtypeStruct((m, d), F32),
        compiler_params=pltpu.CompilerParams(
            dimension_semantics=("parallel",), vmem_limit_bytes=56 * 1024 * 1024),
        name="outproj",
    )(mix_a, mix_b, w_a, w_b, x2, g)


_W_SEGMENTS = (
    (P_Q, R_Q, NSA_WIDTH),
    (P_RKV, R_FEAT, 3 * RWKV_WIDTH),
    (P_ZB, R_ZB, RWKV_WIDTH),
    (P_ZA, R_ZA, NSA_WIDTH),
    (P_KV, R_KV, 6 * NSA_KV_HEADS * HEAD_DIM),
    (P_WDAD, R_FEAT + 3 * RWKV_WIDTH, 2 * LORA),
    (P_GATE, R_GATE, LANE),
)


def _relayout_kernel(w_ref, o_ref):
    for dst, src, width in _W_SEGMENTS:
        o_ref[:, dst:dst + width] = w_ref[:, src:src + width].astype(BF16)
    used = P_GATE + LANE
    o_ref[:, used:] = jnp.zeros((o_ref.shape[0], NP - used), BF16)


def _permute_w_in(w, rows):
    d, n = w.shape
    return pl.pallas_call(
        _relayout_kernel,
        grid=(d // rows,),
        in_specs=[pl.BlockSpec((rows, n), lambda i: (i, 0))],
        out_specs=pl.BlockSpec((rows, NP), lambda i: (i, 0)),
        out_shape=jax.ShapeDtypeStruct((d, NP), BF16),
        compiler_params=pltpu.CompilerParams(dimension_semantics=("parallel",)),
        name="relayout",
    )(w)


def _block(x, pre_norm_g, w_in, rel_bias_table, cmp_pos_k, cmp_pos_v, cmp_k_w1, cmp_k_w2, cmp_v_w1,
           cmp_v_w2, rwkv_mu, rwkv_w0, rwkv_w2, rwkv_a0, rwkv_a2, rwkv_k_k, rwkv_k_a, rwkv_r_k,
           rwkv_ln_w, rwkv_ln_b, w_out, post_norm_g):
    b, s, d = x.shape
    x2 = x.reshape(b * s, d)
    tm = min(1024, b * s)
    proj = _inproj(x2, pre_norm_g.reshape(1, d), _permute_w_in(w_in.astype(BF16), 256), tm, 2048)
    proj3 = proj.reshape(b, s, NP)

    bias_c, toep = _bias(rel_bias_table.reshape(-1), s)
    half = CMP_STRIDE * HEAD_DIM
    mix_a = _nsa(proj3, bias_c, toep,
                 cmp_pos_k.reshape(2, half), cmp_k_w1.astype(BF16), cmp_k_w2.astype(BF16),
                 cmp_pos_v.reshape(2, half), cmp_v_w1.astype(BF16), cmp_v_w2.astype(BF16))

    w3 = 3 * RWKV_WIDTH
    vec_rows = [rwkv_mu[:RWKV_WIDTH], rwkv_mu[RWKV_WIDTH:2 * RWKV_WIDTH], rwkv_mu[2 * RWKV_WIDTH:w3],
                rwkv_w0, rwkv_a0, rwkv_k_k, rwkv_k_a, rwkv_ln_w, rwkv_ln_b, rwkv_r_k.reshape(-1)]
    vecs = jnp.stack(vec_rows + [jnp.zeros_like(rwkv_w0)] * (16 - len(vec_rows)), axis=0)
    mix_b = _rwkv(proj3, vecs, rwkv_mu[w3:].reshape(1, 2 * LORA), rwkv_w2.astype(BF16),
                  rwkv_a2.astype(BF16), min(256, s), RWKV_GROUPS_PER_STEP)

    w_o = w_out.astype(BF16)
    out = _outproj(mix_a.reshape(b * s, NSA_WIDTH), mix_b.reshape(b * s, RWKV_WIDTH),
                   w_o[:NSA_WIDTH], w_o[NSA_WIDTH:], x2, post_norm_g.reshape(1, d), min(512, b * s))
    return out.reshape(b, s, d)


def kernel(x, pre_norm_g, w_in, rel_bias_table, cmp_pos_k, cmp_pos_v, cmp_k_w1, cmp_k_w2, cmp_v_w1,
           cmp_v_w2, rwkv_mu, rwkv_w0, rwkv_w2, rwkv_a0, rwkv_a2, rwkv_k_k, rwkv_k_a, rwkv_r_k,
           rwkv_ln_w, rwkv_ln_b, w_out, post_norm_g):
    h = x
    for l in range(pre_norm_g.shape[0]):
        h = _block(h, pre_norm_g[l], w_in[l], rel_bias_table, cmp_pos_k[l], cmp_pos_v[l], cmp_k_w1[l],
                   cmp_k_w2[l], cmp_v_w1[l], cmp_v_w2[l], rwkv_mu[l], rwkv_w0[l], rwkv_w2[l],
                   rwkv_a0[l], rwkv_a2[l], rwkv_k_k[l], rwkv_k_a[l], rwkv_r_k[l], rwkv_ln_w[l],
                   rwkv_ln_b[l], w_out[l], post_norm_g[l])
    return h
```

```python
import math

import jax
import jax.numpy as jnp
from jax import lax
from jax.experimental import pallas as pl
from jax.experimental.pallas import tpu as pltpu

F32 = jnp.float32
BF16 = jnp.bfloat16

D_MODEL = 2048
NSA_HEADS = 8
NSA_KV_HEADS = 2
NSA_GROUP = NSA_HEADS // NSA_KV_HEADS
HEAD_DIM = 128
NSA_WIDTH = NSA_HEADS * HEAD_DIM
CMP_BLOCK = 32
CMP_STRIDE = 16
SLC_BLOCK = 64
SLC_TOP_N = 16
WINDOW = 512
RWKV_WIDTH = 1024
RWKV_HEAD_DIM = 64
RWKV_HEADS = RWKV_WIDTH // RWKV_HEAD_DIM
LORA = 64
NUM_BUCKETS = 32
MAX_DISTANCE = 1024
NORM_EPS = 1e-6
RWKV_GN_EPS = 64e-5

R_Q = 0
R_KV = R_Q + NSA_WIDTH
R_GATE = R_KV + 6 * NSA_KV_HEADS * HEAD_DIM
R_ZA = R_GATE + 3 * NSA_HEADS
R_FEAT = R_ZA + NSA_WIDTH
R_ZB = R_FEAT + 3 * RWKV_WIDTH + 2 * LORA
R_END = R_ZB + RWKV_WIDTH

P_Q = 0
P_RKV = 1024
P_ZB = 4096
P_ZA = 5120
P_KV = 6144
P_WDAD = 7680
P_GATE = 7808
NP = 8192

LANE = 128
TQ = 256
TB = 128
ND = 9
T_DIAG = ND
T_WEND = ND + 1
T_NONE = ND + 2
NT_ALL = ND + 3
LOG2E = math.log2(math.e)
BIG = 2.0 ** 100
CH = 64
HPG = 4
GW = HPG * RWKV_HEAD_DIM
RWKV_GROUPS_PER_STEP = 4
NEG = -1e30


def _bucket_thresholds():
    out = []
    for k in range(1, NUM_BUCKETS // 2):
        n = 16
        while n ** 8 < (16 ** 8) * (2 ** (3 * k)):
            n += 1
        out.append(n)
    return out


_THR = _bucket_thresholds()


def _mm(a, b):
    return jnp.dot(a, b, preferred_element_type=F32)


def _mm_nt(a, b):
    return lax.dot_general(a, b, (((1,), (1,)), ((), ())), preferred_element_type=F32)


def _mm_tn(a, b):
    return lax.dot_general(a, b, (((0,), (0,)), ((), ())), preferred_element_type=F32)


def _split3(x):
    x1 = x.astype(BF16)
    r1 = x - x1.astype(F32)
    x2 = r1.astype(BF16)
    x3 = (r1 - x2.astype(F32)).astype(BF16)
    return x1, x2, x3


def _mm_split_rhs(a_exact, b):
    b1 = b.astype(BF16)
    b2 = (b - b1.astype(F32)).astype(BF16)
    return _mm(a_exact, b1) + _mm(a_exact, b2)


def _iota(shape, dim):
    return lax.broadcasted_iota(jnp.int32, shape, dim)


def _interleave(gens):
    results = [None] * len(gens)
    live = list(enumerate(gens))
    while live:
        still = []
        for i, g in live:
            try:
                next(g)
                still.append((i, g))
            except StopIteration as stop:
                results[i] = stop.value
        live = still
    return results


def _div_pow2(x, n):
    assert n & (n - 1) == 0
    return x >> (n.bit_length() - 1)


def _mod_pow2(x, n):
    assert n & (n - 1) == 0
    return x & (n - 1)


def _inproj_kernel(x_ref, g_ref, w_ref, o_ref, hn_ref):
    @pl.when(pl.program_id(1) == 0)
    def _():
        x = x_ref[...]
        ms = jnp.mean(x * x, axis=-1, keepdims=True)
        hn_ref[...] = (x * lax.rsqrt(ms + NORM_EPS) * g_ref[...]).astype(BF16)

    o_ref[...] = _mm(hn_ref[...], w_ref[...])


def _inproj(x2, g, w, tm, tn):
    m, d = x2.shape
    n = w.shape[1]
    return pl.pallas_call(
        _inproj_kernel,
        grid=(m // tm, n // tn),
        in_specs=[
            pl.BlockSpec((tm, d), lambda i, j: (i, 0)),
            pl.BlockSpec((1, d), lambda i, j: (0, 0)),
            pl.BlockSpec((d, tn), lambda i, j: (0, j)),
        ],
        out_specs=pl.BlockSpec((tm, tn), lambda i, j: (i, j)),
        out_shape=jax.ShapeDtypeStruct((m, n), F32),
        scratch_shapes=[pltpu.VMEM((tm, d), BF16)],
        compiler_params=pltpu.CompilerParams(
            dimension_semantics=("parallel", "arbitrary"),
            vmem_limit_bytes=56 * 1024 * 1024),
        name="inproj",
    )(x2, g, w)


def _bucket(n):
    n = jnp.maximum(n, 0)
    large = jnp.full(n.shape, NUM_BUCKETS // 2, jnp.int32)
    for thr in _THR:
        large = large + (n >= thr).astype(jnp.int32)
    return jnp.where(n < NUM_BUCKETS // 2, n, large)


def _lookup_all_heads(dist, tab_ref):
    bucket = _bucket(dist)
    hits = [bucket == b for b in range(NUM_BUCKETS)]
    outs = []
    for h in range(NSA_HEADS):
        out = jnp.zeros(dist.shape, F32)
        for b in range(NUM_BUCKETS):
            out = jnp.where(hits[b], tab_ref[b * NSA_HEADS + h] * LOG2E, out)
        outs.append(out)
    return outs


def _bias_kernel(tab_ref, bc_ref, tp_ref):
    i = pl.program_id(0)
    rows, nr = bc_ref.shape[1], bc_ref.shape[2]
    dist_c = (i * rows + _iota((rows, nr), 0)) - (_iota((rows, nr), 1) * CMP_STRIDE + (CMP_BLOCK - 1))
    for h, vals in enumerate(_lookup_all_heads(dist_c, tab_ref)):
        bc_ref[h] = vals

    @pl.when(i == 0)
    def _():
        base = _iota((TB, TB), 0) - _iota((TB, TB), 1)
        neg = jnp.full((TB, TB), NEG, F32)
        for h in range(NSA_HEADS):
            tp_ref[h, T_NONE] = neg
        for d in range(ND):
            for h, vals in enumerate(_lookup_all_heads(base + d * TB, tab_ref)):
                tp_ref[h, d] = vals
                if d == 0:
                    tp_ref[h, T_DIAG] = jnp.where(base >= 0, vals, neg)
                if d == WINDOW // TB:
                    tp_ref[h, T_WEND] = jnp.where(base < 0, vals, neg)


def _bias(table_flat, s):
    nr = s // CMP_STRIDE
    rows = min(512, s)
    return pl.pallas_call(
        _bias_kernel,
        grid=(s // rows,),
        in_specs=[pl.BlockSpec(memory_space=pltpu.SMEM)],
        out_specs=[
            pl.BlockSpec((NSA_HEADS, rows, nr), lambda i: (0, i, 0)),
            pl.BlockSpec((NSA_HEADS, NT_ALL, TB, TB), lambda i: (0, 0, 0, 0)),
        ],
        out_shape=[
            jax.ShapeDtypeStruct((NSA_HEADS, s, nr), F32),
            jax.ShapeDtypeStruct((NSA_HEADS, NT_ALL, TB, TB), F32),
        ],
        compiler_params=pltpu.CompilerParams(dimension_semantics=("arbitrary",)),
        name="bias",
    )(table_flat)


def _compress(kv_ref, pos_ref, w1_ref, w2_ref, nr):
    half = CMP_STRIDE * HEAD_DIM
    r = jnp.concatenate(
        [kv_ref[0, pl.ds(m, nr, stride=CMP_STRIDE), :] for m in range(CMP_STRIDE)], axis=1)
    a = _mm((r + pos_ref[0:1, :]).astype(BF16), w1_ref[0:half, :])
    b = _mm((r + pos_ref[1:2, :]).astype(BF16), w1_ref[half:2 * half, :])
    pre = a + pltpu.roll(b, nr - 1, 0)
    h1 = pre * jax.nn.sigmoid(pre)
    return _mm(h1.astype(BF16), w2_ref[...])


def _bias_head(toep_ref, h, d_tiles, window, valid=True):
    sub = TQ // TB
    rows = []
    for ri in range(sub):
        cols = []
        for ci in range(sub):
            d = d_tiles + ri - ci
            idx = jnp.where(d == 0, T_DIAG, jnp.minimum(d, ND - 1))
            if window:
                idx = jnp.where(d == WINDOW // TB, T_WEND, jnp.where(d > WINDOW // TB, T_NONE, idx))
            idx = jnp.where((d < 0) | jnp.logical_not(valid), T_NONE, idx)
            cols.append(toep_ref[h, idx])
        rows.append(jnp.concatenate(cols, axis=1))
    return jnp.concatenate(rows, axis=0)


def _bias_tile(toep_ref, d_tiles, window, valid=True):
    return jnp.concatenate([_bias_head(toep_ref, h, d_tiles, window, valid) for h in range(NSA_GROUP)], axis=0)


def _add_shared(s, mask_add):
    n = s.shape[-1]
    return (s.reshape(NSA_GROUP, TQ, n) + mask_add[None]).reshape(NSA_GROUP * TQ, n)


def _with_ones(v):
    return jnp.concatenate([v, jnp.ones(v.shape, v.dtype)], axis=1)


def _fold_lanes(x, op):
    out = x[:, :LANE]
    for c in range(1, x.shape[1] // LANE):
        out = op(out, x[:, c * LANE:(c + 1) * LANE])
    return out


def _nsa_kernel(q_ref, kc_ref, vc_ref, ks_ref, vs_ref, kw_ref, vw_ref, gate_ref, za_ref,
                bc_ref, toep_ref, posk_ref, w1k_ref, w2k_ref, posv_ref, w1v_ref, w2v_ref,
                o_ref, kcs_ref, vcs_ref, s_ref, macc_ref, acc_ref):
    qi = pl.program_id(2)
    s_len = kc_ref.shape[1]
    nr = s_len // CMP_STRIDE
    nb = s_len // SLC_BLOCK
    n_sel = min(SLC_TOP_N, nb)

    @pl.when(qi == 0)
    def _():
        kcs_ref[...] = _compress(kc_ref, posk_ref, w1k_ref, w2k_ref, nr).astype(BF16)
        vcs_ref[...] = _compress(vc_ref, posv_ref, w1v_ref, w2v_ref, nr).astype(BF16)

    q0 = qi * TQ
    rows = NSA_GROUP * TQ
    q = q_ref[0] * (HEAD_DIM ** -0.5 * LOG2E)
    q4 = jnp.concatenate([q[:, h * HEAD_DIM:(h + 1) * HEAD_DIM] for h in range(NSA_GROUP)],
                         axis=0).astype(BF16)

    sub = TQ // TB
    n_tiles = s_len // TQ

    def compressed_and_selection():
        t_c = q0 + _iota((TQ, nr), 0)
        i_c = _iota((TQ, nr), 1)
        mask_c = (t_c - (i_c * CMP_STRIDE + (CMP_BLOCK - 1)) >= 0) & (i_c < nr - 1)
        qk = _mm_nt(q4, kcs_ref[...])
        mask_add = jnp.where(mask_c, 0.0, NEG)
        yield
        p_heads = []
        for h in range(NSA_GROUP):
            lg = qk[h * TQ:(h + 1) * TQ] + bc_ref[h] + mask_add
            m_c = jnp.max(lg, axis=-1, keepdims=True)
            yield
            e = jnp.where(lg > 0.5 * NEG, jnp.exp2(lg - m_c), 0.0)
            l_c = jnp.sum(e, axis=-1, keepdims=True)
            yield
            p_heads.append(e / jnp.maximum(l_c, 1e-30))
        p = jnp.concatenate(p_heads, axis=0)
        o_c = _mm(p.astype(BF16), vcs_ref[...])
        psum = p_heads[0]
        for h in range(1, NSA_GROUP):
            psum = psum + p_heads[h]
        ov_i = _iota((nb, nr), 1) * CMP_STRIDE
        ov_j = _iota((nb, nr), 0) * SLC_BLOCK
        ov_t = ((ov_i < ov_j + SLC_BLOCK) & (ov_i + CMP_BLOCK > ov_j)).astype(BF16)
        p1, p2, p3 = _split3(psum)
        imp_t = _mm_nt(ov_t, p1) + _mm_nt(ov_t, p2) + _mm_nt(ov_t, p3)
        yield
        jb = _iota((nb, TQ), 0)
        cur = _div_pow2(q0 + _iota((nb, TQ), 1), SLC_BLOCK)
        forced = (jb == 0) | (jb == cur) | (jb == cur - 1)
        causal = jb <= cur
        score = jnp.where(forced, jnp.inf, jnp.where(causal, imp_t, -jnp.inf))
        rank = jnp.zeros((nb, TQ), jnp.int32)
        for jp in range(nb):
            sj = score[jp:jp + 1, :]
            beats = (sj > score) | ((sj == score) & (jb > jp))
            rank = rank + beats.astype(jnp.int32)
            if jp % 8 == 7:
                yield
        sel_t = ((rank < n_sel) & causal).astype(BF16)
        place = (_iota((nb, LANE), 0) == _iota((nb, LANE), 1)).astype(BF16)
        unsel = _mm_tn(sel_t, place) - (_iota((TQ, LANE), 1) < nb).astype(F32)
        yield
        return o_c, unsel.astype(BF16)

    def window():
        n_band = WINDOW // TQ + 1
        qk, tiles, v_w = [], [], []
        for c in range(n_band):
            j = qi - (n_band - 1) + c
            jc = jnp.maximum(j, 0)
            k0 = pl.multiple_of(jc * TQ, TQ)
            tiles.append((j, jc))
            qk.append(_mm_nt(q4, kw_ref[0, pl.ds(k0, TQ), :].astype(BF16)))
            v_w.append(_with_ones(vw_ref[0, pl.ds(k0, TQ), :].astype(BF16)))
        yield
        heads = range(NSA_GROUP)
        s_w = [[None] * NSA_GROUP for _ in range(n_band)]
        m_w = [None] * NSA_GROUP
        for c in range(n_band):
            for h in heads:
                s = qk[c][h * TQ:(h + 1) * TQ] + _bias_head(toep_ref, h, (qi - tiles[c][1]) * sub, True,
                                                             tiles[c][0] >= 0)
                s_w[c][h] = s
                m_w[h] = s if c == 0 else jnp.maximum(m_w[h], s)
                yield
        for h in heads:
            m_w[h] = jnp.max(m_w[h], axis=-1, keepdims=True)
        yield
        v_band = jnp.concatenate(v_w, axis=0)
        acc_w = []
        for h in heads:
            p_h = []
            for c in range(n_band):
                p_h.append(jnp.exp2(s_w[c][h] - m_w[h]).astype(BF16))
                yield
            acc_w.append(_mm(jnp.concatenate(p_h, axis=1), v_band))
            yield
        return jnp.concatenate([a[:, :HEAD_DIM] / jnp.maximum(a[:, HEAD_DIM:], 1e-30) for a in acc_w], axis=0)

    def gates():
        gts = jax.nn.sigmoid(gate_ref[0])
        grp = pl.program_id(1)
        za = za_ref[0]
        out = [[None] * NSA_GROUP for _ in range(3)]
        for h in range(NSA_GROUP):
            z = za[:, h * HEAD_DIM:(h + 1) * HEAD_DIM]
            zs = z * jax.nn.sigmoid(z)
            for branch in range(3):
                lane = branch * NSA_HEADS + h
                col = gts[:, lane:lane + 1]
                for g in range(1, NSA_KV_HEADS):
                    lg = lane + g * NSA_GROUP
                    col = jnp.where(grp == g, gts[:, lg:lg + 1], col)
                out[branch][h] = zs * col
                yield
        return out

    (o_c, unsel), o_w, gz = _interleave([compressed_and_selection(), window(), gates()])
    o_cw = [gz[0][h] * o_c[h * TQ:(h + 1) * TQ] + gz[2][h] * o_w[h * TQ:(h + 1) * TQ] for h in range(NSA_GROUP)]

    q_sel = jnp.concatenate([q4, jnp.concatenate([unsel] * NSA_GROUP, axis=0)], axis=1)

    def key_tile(j):
        k0 = pl.multiple_of(j * TQ, TQ)
        blk = _div_pow2(j * TQ + _iota((TQ, LANE), 0), SLC_BLOCK)
        marks = jnp.where(_iota((TQ, LANE), 1) == blk, BIG, 0.0).astype(BF16)
        return jnp.concatenate([ks_ref[0, pl.ds(k0, TQ), :].astype(BF16), marks], axis=1)

    assert n_tiles % 4 == 0
    n_pairs = _div_pow2(qi + 2, 2)
    passes, first = [], 0
    for tiles_per_iter in (8, 4, 2):
        if tiles_per_iter <= n_tiles:
            trips = _div_pow2(2 * n_pairs - first, tiles_per_iter)
            passes.append((tiles_per_iter, first, trips))
            first = first + tiles_per_iter * trips

    def slc_logits(tpi, base):
        def body(it, carry):
            tiles = [base + tpi * it + c for c in range(tpi)]
            qk = [_mm_nt(q_sel, key_tile(j)) for j in tiles]
            macc = macc_ref[...]
            for j, qk_j in zip(tiles, qk):
                s = qk_j + _bias_tile(toep_ref, (qi - j) * sub, False)
                s_ref[j] = s
                macc = jnp.maximum(macc, _fold_lanes(s, jnp.maximum))
            macc_ref[...] = macc
            return carry
        return body

    macc_ref[...] = jnp.full((rows, LANE), NEG, F32)
    for tpi, base, trips in passes:
        lax.fori_loop(0, trips, slc_logits(tpi, base), 0)
    m_s = jnp.max(macc_ref[...], axis=-1, keepdims=True)

    def slc_values(tpi, base):
        def body(it, carry):
            j0 = base + tpi * it
            k0 = pl.multiple_of(j0 * TQ, 2 * TQ)
            p_it = jnp.concatenate([jnp.exp2(s_ref[j0 + c] - m_s).astype(BF16) for c in range(tpi)], axis=1)
            acc_ref[...] += _mm(p_it, _with_ones(vs_ref[0, pl.ds(k0, tpi * TQ), :].astype(BF16)))
            return carry
        return body

    acc_ref[...] = jnp.zeros((rows, 2 * HEAD_DIM), F32)
    for tpi, base, trips in passes:
        lax.fori_loop(0, trips, slc_values(tpi, base), 0)
    acc = acc_ref[...]
    o_s = acc[:, :HEAD_DIM] / jnp.maximum(acc[:, HEAD_DIM:], 1e-30)

    for h in range(NSA_GROUP):
        o = o_cw[h] + gz[1][h] * o_s[h * TQ:(h + 1) * TQ]
        o_ref[0, :, h * HEAD_DIM:(h + 1) * HEAD_DIM] = o.astype(o_ref.dtype)


def _nsa(proj3, bias_c, toep, posk, w1k, w2k, posv, w1v, w2v):
    b, s, _ = proj3.shape
    nr = s // CMP_STRIDE
    gq = NSA_GROUP * HEAD_DIM

    def kvspec(idx):
        return pl.BlockSpec((1, s, HEAD_DIM), lambda bi, g, qi, idx=idx: (bi, 0, P_KV // HEAD_DIM + 2 * idx + g))

    def whole(a):
        return pl.BlockSpec(a.shape, lambda bi, g, qi, nd=a.ndim: (0,) * nd)

    in_specs = [
        pl.BlockSpec((1, TQ, gq), lambda bi, g, qi: (bi, qi, P_Q // gq + g)),
        kvspec(0), kvspec(1), kvspec(2), kvspec(3), kvspec(4), kvspec(5),
        pl.BlockSpec((1, TQ, LANE), lambda bi, g, qi: (bi, qi, P_GATE // LANE)),
        pl.BlockSpec((1, TQ, gq), lambda bi, g, qi: (bi, qi, P_ZA // gq + g)),
        pl.BlockSpec((NSA_GROUP, TQ, nr), lambda bi, g, qi: (g, qi, 0)),
        pl.BlockSpec((NSA_GROUP, NT_ALL, TB, TB), lambda bi, g, qi: (g, 0, 0, 0)),
        whole(posk), whole(w1k), whole(w2k), whole(posv), whole(w1v), whole(w2v),
    ]
    return pl.pallas_call(
        _nsa_kernel,
        grid=(b, NSA_KV_HEADS, s // TQ),
        in_specs=in_specs,
        out_specs=pl.BlockSpec((1, TQ, gq), lambda bi, g, qi: (bi, qi, g)),
        out_shape=jax.ShapeDtypeStruct((b, s, NSA_WIDTH), BF16),
        scratch_shapes=[pltpu.VMEM((nr, HEAD_DIM), BF16), pltpu.VMEM((nr, HEAD_DIM), BF16),
                        pltpu.VMEM((s // TQ, NSA_GROUP * TQ, TQ), F32),
                        pltpu.VMEM((NSA_GROUP * TQ, LANE), F32),
                        pltpu.VMEM((NSA_GROUP * TQ, 2 * HEAD_DIM), F32)],
        compiler_params=pltpu.CompilerParams(
            dimension_semantics=("parallel", "parallel", "arbitrary"),
            vmem_limit_bytes=56 * 1024 * 1024),
        name="nsa",
    )(proj3, proj3, proj3, proj3, proj3, proj3, proj3, proj3, proj3,
      bias_c, toep, posk, w1k, w2k, posv, w1v, w2v)


def _shift_mix(ref, prev_ref, mu, sl):
    x = ref[0, :, sl]
    prev = jnp.where(_iota(x.shape, 0) == 0, prev_ref[:, sl], pltpu.roll(x, 1, 0))
    prev_ref[:, sl] = x[x.shape[0] - 1:]
    return x + mu * (prev - x)


def _rwkv_kernel(r_ref, k_ref, v_ref, wa_ref, zb_ref, vec_ref, muwa_ref, w2_ref, a2_ref,
                 o_ref, st_ref, pr_ref, pk_ref, pv_ref, pwa_ref):
    first = pl.program_id(2) == 0
    tb = r_ref.shape[1]
    n_groups = r_ref.shape[2] // GW
    n_chunks = tb // CH

    @pl.when(first)
    def _():
        for ref in (st_ref, pr_ref, pk_ref, pv_ref, pwa_ref):
            ref[...] = jnp.zeros_like(ref)

    wa = _shift_mix(wa_ref, pwa_ref, muwa_ref[...], slice(0, LANE))
    wd_act = jnp.tanh(wa[:, :LORA]).astype(BF16)
    ad = wa[:, LORA:].astype(BF16)

    seg = (_div_pow2(_iota((GW, GW), 0), RWKV_HEAD_DIM) == _div_pow2(_iota((GW, GW), 1), RWKV_HEAD_DIM))
    segf = seg.astype(F32)
    segb = seg.astype(BF16)
    assert CH == RWKV_HEAD_DIM
    lane_s = _mod_pow2(_iota((CH, GW), 1), CH)
    row_t = _iota((CH, GW), 0)
    strict = lane_s < row_t
    incl = lane_s <= row_t
    eye = (lane_s == row_t).astype(F32)
    ti_r, ti_c = _iota((tb, tb), 0), _iota((tb, tb), 1)
    trib = ((ti_c <= ti_r) & (_div_pow2(ti_c, CH) == _div_pow2(ti_r, CH))).astype(BF16)

    def bd(x):
        xb = x.astype(BF16)
        return jnp.concatenate([xb] * HPG, axis=0) * segb

    def prep(gi):
        lanes = slice(gi * GW, (gi + 1) * GW)
        vec = vec_ref[:, lanes]
        mu_r, mu_k, mu_v = vec[0:1], vec[1:2], vec[2:3]
        w0, a0, k_k, k_a = vec[3:4], vec[4:5], vec[5:6], vec[6:7]
        r = _shift_mix(r_ref, pr_ref, mu_r, lanes)
        k = _shift_mix(k_ref, pk_ref, mu_k, lanes)
        v = _shift_mix(v_ref, pv_ref, mu_v, lanes)
        w_lora = _mm(wd_act, w2_ref[:, lanes])
        a_lora = _mm(ad, a2_ref[:, lanes])
        kk = k * k_k
        kk_ss = _mm((kk * kk).astype(BF16), segb)
        yield
        lw = jax.nn.sigmoid(w0 + w_lora) * (-math.exp(-0.5) * LOG2E)
        cum = _mm_split_rhs(trib, lw)
        yield
        a_sig = jax.nn.sigmoid(a0 + a_lora)
        kk = kk * lax.rsqrt(jnp.maximum(kk_ss, 1e-24))
        k = k * (1.0 + (a_sig - 1.0) * k_a)
        return dict(r=r, k=k, v=v, a=-kk, b=kk * a_sig, lw=lw, cum=cum, vec=vec, lanes=lanes)

    groups = _interleave([prep(gi) for gi in range(n_groups)])

    def chunk_local(g, c):
        ts = slice(c * CH, (c + 1) * CH)
        rc, kc, vc, ac, bc, lwc, cum = (g[n][ts] for n in ("r", "k", "v", "a", "b", "lw", "cum"))
        tot = cum[CH - 1:CH]
        e_out = jnp.exp2(-cum)
        e_end = jnp.exp2(tot - cum)
        r_t = rc * jnp.exp2(cum)
        a_t = ac * jnp.exp2(cum - lwc)
        lhs = jnp.concatenate([a_t, r_t], axis=0).astype(BF16)
        aa = _mm_nt(lhs, jnp.concatenate([bd(bc * e_out), bd(kc * e_out)], axis=0))
        yield
        a_ab = jnp.where(strict, aa[:CH, :GW], 0.0)
        a_ak = jnp.where(strict, aa[:CH, GW:], 0.0)
        a_rb = jnp.where(incl, aa[CH:, :GW], 0.0)
        a_rk = jnp.where(incl, aa[CH:, GW:], 0.0)
        t_inv = eye + a_ab
        mpow = _mm(a_ab.astype(BF16), bd(a_ab))
        av = _mm(a_ak.astype(BF16), bd(vc))
        yield
        for _ in range(int(math.log2(CH)) - 1):
            res = _mm(jnp.concatenate([t_inv, mpow], axis=0).astype(BF16), bd(mpow))
            yield
            t_inv = t_inv + res[:CH]
            mpow = res[CH:]
        wu = _mm(t_inv.astype(BF16), jnp.concatenate([bd(a_t), bd(av)], axis=1))
        yield
        return dict(
            lhs=jnp.concatenate([wu[:, :GW], r_t], axis=0).astype(BF16), u_loc=wu[:, GW:],
            a_r=jnp.concatenate([a_rb, a_rk], axis=1).astype(BF16), bdv=bd(vc), vc=vc,
            bk_end=jnp.concatenate([bc * e_end, kc * e_end], axis=0).astype(BF16), dec=jnp.exp2(tot))

    loc = _interleave([chunk_local(g, c) for g in groups for c in range(n_chunks)])

    def chain(gi):
        ys = []
        g_state = st_ref[gi]
        for c in range(n_chunks):
            lc = loc[gi * n_chunks + c]
            x0 = _mm_nt(lc["lhs"], g_state.astype(BF16))
            yield
            u = x0[:CH] + lc["u_loc"]
            y_c = _mm(lc["a_r"], jnp.concatenate([bd(u), lc["bdv"]], axis=0))
            upd = _mm_tn(jnp.concatenate([u, lc["vc"]], axis=0).astype(BF16), lc["bk_end"])
            yield
            ys.append(x0[CH:] + y_c)
            g_state = g_state * lc["dec"] + upd * segf
        st_ref[gi] = g_state
        return jnp.concatenate(ys, axis=0)

    ys = _interleave([chain(gi) for gi in range(n_groups)])

    def finish(g, y):
        vec = g["vec"]
        ln_w, ln_b, r_k = vec[7:8], vec[8:9], vec[9:10]
        inv_n = 1.0 / RWKV_HEAD_DIM
        mean = _mm(y.astype(BF16), segb) * inv_n
        bonus = _mm((g["r"] * g["k"] * r_k).astype(BF16), segb) * g["v"]
        yield
        yc = y - mean
        var = _mm((yc * yc).astype(BF16), segb) * inv_n
        yield
        yn = yc * lax.rsqrt(var + RWKV_GN_EPS) * ln_w + ln_b
        zb = zb_ref[0, :, g["lanes"]]
        o_ref[0, :, g["lanes"]] = ((yn + bonus) * (zb * jax.nn.sigmoid(zb))).astype(o_ref.dtype)

    _interleave([finish(g, y) for g, y in zip(groups, ys)])


def _rwkv(proj3, vecs, mu_wa, w2, a2, tb, gps):
    b, s, _ = proj3.shape
    gw = gps * GW
    ng = RWKV_WIDTH // gw

    def col(off):
        return pl.BlockSpec((1, tb, gw), lambda bi, g, ti, off=off: (bi, ti, off // gw + g))

    in_specs = [
        col(P_RKV), col(P_RKV + RWKV_WIDTH), col(P_RKV + 2 * RWKV_WIDTH),
        pl.BlockSpec((1, tb, LANE), lambda bi, g, ti: (bi, ti, P_WDAD // LANE)),
        col(P_ZB),
        pl.BlockSpec((vecs.shape[0], gw), lambda bi, g, ti: (0, g)),
        pl.BlockSpec((1, LANE), lambda bi, g, ti: (0, 0)),
        pl.BlockSpec((LORA, gw), lambda bi, g, ti: (0, g)),
        pl.BlockSpec((LORA, gw), lambda bi, g, ti: (0, g)),
    ]
    return pl.pallas_call(
        _rwkv_kernel,
        grid=(b, ng, s // tb),
        in_specs=in_specs,
        out_specs=pl.BlockSpec((1, tb, gw), lambda bi, g, ti: (bi, ti, g)),
        out_shape=jax.ShapeDtypeStruct((b, s, RWKV_WIDTH), BF16),
        scratch_shapes=[pltpu.VMEM((gps, GW, GW), F32), pltpu.VMEM((1, gw), F32), pltpu.VMEM((1, gw), F32),
                        pltpu.VMEM((1, gw), F32), pltpu.VMEM((1, LANE), F32)],
        compiler_params=pltpu.CompilerParams(
            dimension_semantics=("parallel", "parallel", "arbitrary")),
        name="rwkv",
    )(proj3, proj3, proj3, proj3, proj3, vecs, mu_wa, w2, a2)


def _outproj_kernel(ma_ref, mb_ref, wa_ref, wb_ref, x_ref, g_ref, o_ref):
    y = _mm(ma_ref[...], wa_ref[...]) + _mm(mb_ref[...], wb_ref[...])
    ms = jnp.mean(y * y, axis=-1, keepdims=True)
    o_ref[...] = x_ref[...] + y * lax.rsqrt(ms + NORM_EPS) * g_ref[...]


def _outproj(mix_a, mix_b, w_a, w_b, x2, g, tm):
    m, d = x2.shape
    ka, kb = mix_a.shape[1], mix_b.shape[1]
    return pl.pallas_call(
        _outproj_kernel,
        grid=(m // tm,),
        in_specs=[
            pl.BlockSpec((tm, ka), lambda i: (i, 0)),
            pl.BlockSpec((tm, kb), lambda i: (i, 0)),
            pl.BlockSpec((ka, d), lambda i: (0, 0)),
            pl.BlockSpec((kb, d), lambda i: (0, 0)),
            pl.BlockSpec((tm, d), lambda i: (i, 0)),
            pl.BlockSpec((1, d), lambda i: (0, 0)),
        ],
        out_specs=pl.BlockSpec((tm, d), lambda i: (i, 0)),
        out_shape=jax.ShapeDtypeStruct((m, d), F32),
        compiler_params=pltpu.CompilerParams(
            dimension_semantics=("parallel",), vmem_limit_bytes=56 * 1024 * 1024),
        name="outproj",
    )(mix_a, mix_b, w_a, w_b, x2, g)


_W_SEGMENTS = (
    (P_Q, R_Q, NSA_WIDTH),
    (P_RKV, R_FEAT, 3 * RWKV_WIDTH),
    (P_ZB, R_ZB, RWKV_WIDTH),
    (P_ZA, R_ZA, NSA_WIDTH),
    (P_KV, R_KV, 6 * NSA_KV_HEADS * HEAD_DIM),
    (P_WDAD, R_FEAT + 3 * RWKV_WIDTH, 2 * LORA),
    (P_GATE, R_GATE, LANE),
)


def _relayout_kernel(w_ref, o_ref):
    for dst, src, width in _W_SEGMENTS:
        o_ref[:, dst:dst + width] = w_ref[:, src:src + width].astype(BF16)
    used = P_GATE + LANE
    o_ref[:, used:] = jnp.zeros((o_ref.shape[0], NP - used), BF16)


def _permute_w_in(w, rows):
    d, n = w.shape
    return pl.pallas_call(
        _relayout_kernel,
        grid=(d // rows,),
        in_specs=[pl.BlockSpec((rows, n), lambda i: (i, 0))],
        out_specs=pl.BlockSpec((rows, NP), lambda i: (i, 0)),
        out_shape=jax.ShapeDtypeStruct((d, NP), BF16),
        compiler_params=pltpu.CompilerParams(dimension_semantics=("parallel",)),
        name="relayout",
    )(w)


def _block(x, pre_norm_g, w_in, rel_bias_table, cmp_pos_k, cmp_pos_v, cmp_k_w1, cmp_k_w2, cmp_v_w1,
           cmp_v_w2, rwkv_mu, rwkv_w0, rwkv_w2, rwkv_a0, rwkv_a2, rwkv_k_k, rwkv_k_a, rwkv_r_k,
           rwkv_ln_w, rwkv_ln_b, w_out, post_norm_g):
    b, s, d = x.shape
    x2 = x.reshape(b * s, d)
    tm = min(1024, b * s)
    proj = _inproj(x2, pre_norm_g.reshape(1, d), _permute_w_in(w_in.astype(BF16), 256), tm, 2048)
    proj3 = proj.reshape(b, s, NP)

    bias_c, toep = _bias(rel_bias_table.reshape(-1), s)
    half = CMP_STRIDE * HEAD_DIM
    mix_a = _nsa(proj3, bias_c, toep,
                 cmp_pos_k.reshape(2, half), cmp_k_w1.astype(BF16), cmp_k_w2.astype(BF16),
                 cmp_pos_v.reshape(2, half), cmp_v_w1.astype(BF16), cmp_v_w2.astype(BF16))

    w3 = 3 * RWKV_WIDTH
    vec_rows = [rwkv_mu[:RWKV_WIDTH], rwkv_mu[RWKV_WIDTH:2 * RWKV_WIDTH], rwkv_mu[2 * RWKV_WIDTH:w3],
                rwkv_w0, rwkv_a0, rwkv_k_k, rwkv_k_a, rwkv_ln_w, rwkv_ln_b, rwkv_r_k.reshape(-1)]
    vecs = jnp.stack(vec_rows + [jnp.zeros_like(rwkv_w0)] * (16 - len(vec_rows)), axis=0)
    mix_b = _rwkv(proj3, vecs, rwkv_mu[w3:].reshape(1, 2 * LORA), rwkv_w2.astype(BF16),
                  rwkv_a2.astype(BF16), min(512, s), RWKV_GROUPS_PER_STEP)

    w_o = w_out.astype(BF16)
    out = _outproj(mix_a.reshape(b * s, NSA_WIDTH), mix_b.reshape(b * s, RWKV_WIDTH),
                   w_o[:NSA_WIDTH], w_o[NSA_WIDTH:], x2, post_norm_g.reshape(1, d), min(512, b * s))
    return out.reshape(b, s, d)


def kernel(x, pre_norm_g, w_in, rel_bias_table, cmp_pos_k, cmp_pos_v, cmp_k_w1, cmp_k_w2, cmp_v_w1,
           cmp_v_w2, rwkv_mu, rwkv_w0, rwkv_w2, rwkv_a0, rwkv_a2, rwkv_k_k, rwkv_k_a, rwkv_r_k,
           rwkv_ln_w, rwkv_ln_b, w_out, post_norm_g):
    h = x
    for l in range(pre_norm_g.shape[0]):
        h = _block(h, pre_norm_g[l], w_in[l], rel_bias_table, cmp_pos_k[l], cmp_pos_v[l], cmp_k_w1[l],
                   cmp_k_w2[l], cmp_v_w1[l], cmp_v_w2[l], rwkv_mu[l], rwkv_w0[l], rwkv_w2[l],
                   rwkv_a0[l], rwkv_a2[l], rwkv_k_k[l], rwkv_k_a[l], rwkv_r_k[l], rwkv_ln_w[l],
                   rwkv_ln_b[l], w_out[l], post_norm_g[l])
    return h
```

```python
import math

import jax
import jax.numpy as jnp
from jax import lax
from jax.experimental import pallas as pl
from jax.experimental.pallas import tpu as pltpu

F32 = jnp.float32
BF16 = jnp.bfloat16

NSA_HEADS = 8
NSA_KV_HEADS = 2
NSA_GROUP = NSA_HEADS // NSA_KV_HEADS
HEAD_DIM = 128
NSA_WIDTH = NSA_HEADS * HEAD_DIM
CMP_BLOCK = 32
CMP_STRIDE = 16
SLC_BLOCK = 64
SLC_TOP_N = 16
WINDOW = 512
RWKV_WIDTH = 1024
RWKV_HEAD_DIM = 64
LORA = 64
NUM_BUCKETS = 32
MAX_DISTANCE = 1024
NORM_EPS = 1e-6
RWKV_GN_EPS = 64e-5

R_Q = 0
R_KV = R_Q + NSA_WIDTH
R_GATE = R_KV + 6 * NSA_KV_HEADS * HEAD_DIM
R_ZA = R_GATE + 3 * NSA_HEADS
R_FEAT = R_ZA + NSA_WIDTH
R_ZB = R_FEAT + 3 * RWKV_WIDTH + 2 * LORA
R_END = R_ZB + RWKV_WIDTH

P_Q = 0
P_RKV = 1024
P_ZB = 4096
P_ZA = 5120
P_KV = 6144
P_WDAD = 7680
P_GATE = 7808
NP = 8192

LANE = 128
TQ = 256
TB = 128
ND = 9
T_DIAG = ND
T_WEND = ND + 1
T_NONE = ND + 2
NT_ALL = ND + 3
LOG2E = math.log2(math.e)
BIG = 2.0 ** 100
CH = 64
HPG = 4
GW = HPG * RWKV_HEAD_DIM
RWKV_GROUPS_PER_STEP = 4
NEG = -1e30


VMEM_LIMIT_BYTES = 56 * 1024 * 1024
INPROJ_TM, INPROJ_TN = 1024, 2048
OUTPROJ_TM = 512
RELAYOUT_ROWS = 256
BIAS_ROWS = 512
RWKV_SLAB = 256


def _bucket_thresholds():
    exact = NUM_BUCKETS // 2
    ratio = MAX_DISTANCE // exact
    out = []
    for k in range(1, NUM_BUCKETS - exact):
        n = exact
        while n ** exact < (exact ** exact) * (ratio ** k):
            n += 1
        out.append(n)
    return out


_THR = _bucket_thresholds()


def _mm(a, b):
    return jnp.dot(a, b, preferred_element_type=F32)


def _mm_nt(a, b):
    return lax.dot_general(a, b, (((1,), (1,)), ((), ())), preferred_element_type=F32)


def _mm_tn(a, b):
    return lax.dot_general(a, b, (((0,), (0,)), ((), ())), preferred_element_type=F32)


def _split3(x):
    x1 = x.astype(BF16)
    r1 = x - x1.astype(F32)
    x2 = r1.astype(BF16)
    x3 = (r1 - x2.astype(F32)).astype(BF16)
    return x1, x2, x3


def _mm_split_rhs(a_exact, b):
    b1 = b.astype(BF16)
    b2 = (b - b1.astype(F32)).astype(BF16)
    return _mm(a_exact, b1) + _mm(a_exact, b2)


def _iota(shape, dim):
    return lax.broadcasted_iota(jnp.int32, shape, dim)


def _interleave(gens):
    results = [None] * len(gens)
    live = list(enumerate(gens))
    while live:
        still = []
        for i, g in live:
            try:
                next(g)
                still.append((i, g))
            except StopIteration as stop:
                results[i] = stop.value
        live = still
    return results


def _div_pow2(x, n):
    assert n & (n - 1) == 0
    return x >> (n.bit_length() - 1)


def _mod_pow2(x, n):
    assert n & (n - 1) == 0
    return x & (n - 1)


def _inproj_kernel(x_ref, g_ref, w_ref, o_ref, hn_ref):
    @pl.when(pl.program_id(1) == 0)
    def _():
        x = x_ref[...]
        ms = jnp.mean(x * x, axis=-1, keepdims=True)
        hn_ref[...] = (x * lax.rsqrt(ms + NORM_EPS) * g_ref[...]).astype(BF16)

    o_ref[...] = _mm(hn_ref[...], w_ref[...])


def _inproj(x2, g, w, tm, tn):
    m, d = x2.shape
    n = w.shape[1]
    return pl.pallas_call(
        _inproj_kernel,
        grid=(m // tm, n // tn),
        in_specs=[
            pl.BlockSpec((tm, d), lambda i, j: (i, 0)),
            pl.BlockSpec((1, d), lambda i, j: (0, 0)),
            pl.BlockSpec((d, tn), lambda i, j: (0, j)),
        ],
        out_specs=pl.BlockSpec((tm, tn), lambda i, j: (i, j)),
        out_shape=jax.ShapeDtypeStruct((m, n), F32),
        scratch_shapes=[pltpu.VMEM((tm, d), BF16)],
        compiler_params=pltpu.CompilerParams(
            dimension_semantics=("parallel", "arbitrary"),
            vmem_limit_bytes=VMEM_LIMIT_BYTES),
        name="inproj",
    )(x2, g, w)


def _bucket(n):
    n = jnp.maximum(n, 0)
    large = jnp.full(n.shape, NUM_BUCKETS // 2, jnp.int32)
    for thr in _THR:
        large = large + (n >= thr).astype(jnp.int32)
    return jnp.where(n < NUM_BUCKETS // 2, n, large)


def _lookup_all_heads(dist, tab_ref):
    bucket = _bucket(dist)
    hits = [bucket == b for b in range(NUM_BUCKETS)]
    outs = []
    for h in range(NSA_HEADS):
        out = jnp.zeros(dist.shape, F32)
        for b in range(NUM_BUCKETS):
            out = jnp.where(hits[b], tab_ref[b * NSA_HEADS + h] * LOG2E, out)
        outs.append(out)
    return outs


def _bias_kernel(tab_ref, bc_ref, tp_ref):
    i = pl.program_id(0)
    rows, nr = bc_ref.shape[1], bc_ref.shape[2]
    dist_c = (i * rows + _iota((rows, nr), 0)) - (_iota((rows, nr), 1) * CMP_STRIDE + (CMP_BLOCK - 1))
    for h, vals in enumerate(_lookup_all_heads(dist_c, tab_ref)):
        bc_ref[h] = vals

    @pl.when(i == 0)
    def _():
        base = _iota((TB, TB), 0) - _iota((TB, TB), 1)
        neg = jnp.full((TB, TB), NEG, F32)
        for h in range(NSA_HEADS):
            tp_ref[h, T_NONE] = neg
        for d in range(ND):
            for h, vals in enumerate(_lookup_all_heads(base + d * TB, tab_ref)):
                tp_ref[h, d] = vals
                if d == 0:
                    tp_ref[h, T_DIAG] = jnp.where(base >= 0, vals, neg)
                if d == WINDOW // TB:
                    tp_ref[h, T_WEND] = jnp.where(base < 0, vals, neg)


def _bias(table_flat, s):
    nr = s // CMP_STRIDE
    rows = min(BIAS_ROWS, s)
    return pl.pallas_call(
        _bias_kernel,
        grid=(s // rows,),
        in_specs=[pl.BlockSpec(memory_space=pltpu.SMEM)],
        out_specs=[
            pl.BlockSpec((NSA_HEADS, rows, nr), lambda i: (0, i, 0)),
            pl.BlockSpec((NSA_HEADS, NT_ALL, TB, TB), lambda i: (0, 0, 0, 0)),
        ],
        out_shape=[
            jax.ShapeDtypeStruct((NSA_HEADS, s, nr), F32),
            jax.ShapeDtypeStruct((NSA_HEADS, NT_ALL, TB, TB), F32),
        ],
        compiler_params=pltpu.CompilerParams(dimension_semantics=("arbitrary",)),
        name="bias",
    )(table_flat)


def _compress(kv_ref, pos_ref, w1_ref, w2_ref, nr):
    half = CMP_STRIDE * HEAD_DIM
    r = jnp.concatenate(
        [kv_ref[0, pl.ds(m, nr, stride=CMP_STRIDE), :] for m in range(CMP_STRIDE)], axis=1)
    a = _mm((r + pos_ref[0:1, :]).astype(BF16), w1_ref[0:half, :])
    b = _mm((r + pos_ref[1:2, :]).astype(BF16), w1_ref[half:2 * half, :])
    pre = a + pltpu.roll(b, nr - 1, 0)
    h1 = pre * jax.nn.sigmoid(pre)
    return _mm(h1.astype(BF16), w2_ref[...])


def _bias_head(toep_ref, h, d_tiles, window, valid=True):
    sub = TQ // TB
    rows = []
    for ri in range(sub):
        cols = []
        for ci in range(sub):
            d = d_tiles + ri - ci
            idx = jnp.where(d == 0, T_DIAG, jnp.minimum(d, ND - 1))
            if window:
                idx = jnp.where(d == WINDOW // TB, T_WEND, jnp.where(d > WINDOW // TB, T_NONE, idx))
            idx = jnp.where((d < 0) | jnp.logical_not(valid), T_NONE, idx)
            cols.append(toep_ref[h, idx])
        rows.append(jnp.concatenate(cols, axis=1))
    return jnp.concatenate(rows, axis=0)


def _bias_tile(toep_ref, d_tiles, window, valid=True):
    return jnp.concatenate([_bias_head(toep_ref, h, d_tiles, window, valid) for h in range(NSA_GROUP)], axis=0)


def _with_ones(v):
    return jnp.concatenate([v, jnp.ones(v.shape, v.dtype)], axis=1)


def _fold_lanes(x, op):
    out = x[:, :LANE]
    for c in range(1, x.shape[1] // LANE):
        out = op(out, x[:, c * LANE:(c + 1) * LANE])
    return out


def _nsa_kernel(q_ref, kc_ref, vc_ref, ks_ref, vs_ref, kw_ref, vw_ref, gate_ref, za_ref,
                bc_ref, toep_ref, posk_ref, w1k_ref, w2k_ref, posv_ref, w1v_ref, w2v_ref,
                o_ref, kcs_ref, vcs_ref, s_ref, macc_ref, acc_ref):
    qi = pl.program_id(2)
    s_len = kc_ref.shape[1]
    nr = s_len // CMP_STRIDE
    nb = s_len // SLC_BLOCK
    n_sel = min(SLC_TOP_N, nb)

    @pl.when(qi == 0)
    def _():
        kcs_ref[...] = _compress(kc_ref, posk_ref, w1k_ref, w2k_ref, nr).astype(BF16)
        vcs_ref[...] = _compress(vc_ref, posv_ref, w1v_ref, w2v_ref, nr).astype(BF16)

    q0 = qi * TQ
    rows = NSA_GROUP * TQ
    q = q_ref[0] * (HEAD_DIM ** -0.5 * LOG2E)
    q4 = jnp.concatenate([q[:, h * HEAD_DIM:(h + 1) * HEAD_DIM] for h in range(NSA_GROUP)],
                         axis=0).astype(BF16)

    sub = TQ // TB
    n_tiles = s_len // TQ

    def compressed_and_selection():
        t_c = q0 + _iota((TQ, nr), 0)
        i_c = _iota((TQ, nr), 1)
        mask_c = (t_c - (i_c * CMP_STRIDE + (CMP_BLOCK - 1)) >= 0) & (i_c < nr - 1)
        qk = _mm_nt(q4, kcs_ref[...])
        mask_add = jnp.where(mask_c, 0.0, NEG)
        yield
        p_heads = []
        for h in range(NSA_GROUP):
            lg = qk[h * TQ:(h + 1) * TQ] + bc_ref[h] + mask_add
            m_c = jnp.max(lg, axis=-1, keepdims=True)
            yield
            e = jnp.where(lg > 0.5 * NEG, jnp.exp2(lg - m_c), 0.0)
            l_c = jnp.sum(e, axis=-1, keepdims=True)
            yield
            p_heads.append(e / jnp.maximum(l_c, 1e-30))
        p = jnp.concatenate(p_heads, axis=0)
        o_c = _mm(p.astype(BF16), vcs_ref[...])
        psum = p_heads[0]
        for h in range(1, NSA_GROUP):
            psum = psum + p_heads[h]
        ov_i = _iota((nb, nr), 1) * CMP_STRIDE
        ov_j = _iota((nb, nr), 0) * SLC_BLOCK
        ov_t = ((ov_i < ov_j + SLC_BLOCK) & (ov_i + CMP_BLOCK > ov_j)).astype(BF16)
        p1, p2, p3 = _split3(psum)
        imp_t = _mm_nt(ov_t, p1) + _mm_nt(ov_t, p2) + _mm_nt(ov_t, p3)
        yield
        jb = _iota((nb, TQ), 0)
        cur = _div_pow2(q0 + _iota((nb, TQ), 1), SLC_BLOCK)
        forced = (jb == 0) | (jb == cur) | (jb == cur - 1)
        causal = jb <= cur
        score = jnp.where(forced, jnp.inf, jnp.where(causal, imp_t, -jnp.inf))
        rank = jnp.zeros((nb, TQ), jnp.int32)
        for jp in range(nb):
            sj = score[jp:jp + 1, :]
            beats = (sj > score) | ((sj == score) & (jb > jp))
            rank = rank + beats.astype(jnp.int32)
            if jp % 8 == 7:
                yield
        sel_t = ((rank < n_sel) & causal).astype(BF16)
        place = (_iota((nb, LANE), 0) == _iota((nb, LANE), 1)).astype(BF16)
        unsel = _mm_tn(sel_t, place) - (_iota((TQ, LANE), 1) < nb).astype(F32)
        yield
        return o_c, unsel.astype(BF16)

    def window():
        n_band = WINDOW // TQ + 1
        qk, tiles, v_w = [], [], []
        for c in range(n_band):
            j = qi - (n_band - 1) + c
            jc = jnp.maximum(j, 0)
            k0 = pl.multiple_of(jc * TQ, TQ)
            tiles.append((j, jc))
            qk.append(_mm_nt(q4, kw_ref[0, pl.ds(k0, TQ), :].astype(BF16)))
            v_w.append(_with_ones(vw_ref[0, pl.ds(k0, TQ), :].astype(BF16)))
        yield
        heads = range(NSA_GROUP)
        s_w = [[None] * NSA_GROUP for _ in range(n_band)]
        m_w = [None] * NSA_GROUP
        for c in range(n_band):
            for h in heads:
                s = qk[c][h * TQ:(h + 1) * TQ] + _bias_head(toep_ref, h, (qi - tiles[c][1]) * sub, True,
                                                             tiles[c][0] >= 0)
                s_w[c][h] = s
                m_w[h] = s if c == 0 else jnp.maximum(m_w[h], s)
                yield
        for h in heads:
            m_w[h] = jnp.max(m_w[h], axis=-1, keepdims=True)
        yield
        v_band = jnp.concatenate(v_w, axis=0)
        acc_w = []
        for h in heads:
            p_h = []
            for c in range(n_band):
                p_h.append(jnp.exp2(s_w[c][h] - m_w[h]).astype(BF16))
                yield
            acc_w.append(_mm(jnp.concatenate(p_h, axis=1), v_band))
            yield
        return jnp.concatenate([a[:, :HEAD_DIM] / jnp.maximum(a[:, HEAD_DIM:], 1e-30) for a in acc_w], axis=0)

    def gates():
        gts = jax.nn.sigmoid(gate_ref[0])
        grp = pl.program_id(1)
        za = za_ref[0]
        out = [[None] * NSA_GROUP for _ in range(3)]
        for h in range(NSA_GROUP):
            z = za[:, h * HEAD_DIM:(h + 1) * HEAD_DIM]
            zs = z * jax.nn.sigmoid(z)
            for branch in range(3):
                lane = branch * NSA_HEADS + h
                col = gts[:, lane:lane + 1]
                for g in range(1, NSA_KV_HEADS):
                    lg = lane + g * NSA_GROUP
                    col = jnp.where(grp == g, gts[:, lg:lg + 1], col)
                out[branch][h] = zs * col
                yield
        return out

    (o_c, unsel), o_w, gz = _interleave([compressed_and_selection(), window(), gates()])
    o_cw = [gz[0][h] * o_c[h * TQ:(h + 1) * TQ] + gz[2][h] * o_w[h * TQ:(h + 1) * TQ] for h in range(NSA_GROUP)]

    q_sel = jnp.concatenate([q4, jnp.concatenate([unsel] * NSA_GROUP, axis=0)], axis=1)

    def key_tile(j):
        k0 = pl.multiple_of(j * TQ, TQ)
        blk = _div_pow2(j * TQ + _iota((TQ, LANE), 0), SLC_BLOCK)
        marks = jnp.where(_iota((TQ, LANE), 1) == blk, BIG, 0.0).astype(BF16)
        return jnp.concatenate([ks_ref[0, pl.ds(k0, TQ), :].astype(BF16), marks], axis=1)

    assert n_tiles % 4 == 0
    n_pairs = _div_pow2(qi + 2, 2)
    passes, first = [], 0
    for tiles_per_iter in (8, 4, 2):
        if tiles_per_iter <= n_tiles:
            trips = _div_pow2(2 * n_pairs - first, tiles_per_iter)
            passes.append((tiles_per_iter, first, trips))
            first = first + tiles_per_iter * trips

    def slc_logits(tpi, base):
        def body(it, carry):
            tiles = [base + tpi * it + c for c in range(tpi)]
            qk = [_mm_nt(q_sel, key_tile(j)) for j in tiles]
            macc = macc_ref[...]
            for j, qk_j in zip(tiles, qk):
                s = qk_j + _bias_tile(toep_ref, (qi - j) * sub, False)
                s_ref[j] = s
                macc = jnp.maximum(macc, _fold_lanes(s, jnp.maximum))
            macc_ref[...] = macc
            return carry
        return body

    macc_ref[...] = jnp.full((rows, LANE), NEG, F32)
    for tpi, base, trips in passes:
        lax.fori_loop(0, trips, slc_logits(tpi, base), 0)
    m_s = jnp.max(macc_ref[...], axis=-1, keepdims=True)

    def slc_values(tpi, base):
        def body(it, carry):
            j0 = base + tpi * it
            k0 = pl.multiple_of(j0 * TQ, 2 * TQ)
            p_it = jnp.concatenate([jnp.exp2(s_ref[j0 + c] - m_s).astype(BF16) for c in range(tpi)], axis=1)
            acc_ref[...] += _mm(p_it, _with_ones(vs_ref[0, pl.ds(k0, tpi * TQ), :].astype(BF16)))
            return carry
        return body

    acc_ref[...] = jnp.zeros((rows, 2 * HEAD_DIM), F32)
    for tpi, base, trips in passes:
        lax.fori_loop(0, trips, slc_values(tpi, base), 0)
    acc = acc_ref[...]
    o_s = acc[:, :HEAD_DIM] / jnp.maximum(acc[:, HEAD_DIM:], 1e-30)

    for h in range(NSA_GROUP):
        o = o_cw[h] + gz[1][h] * o_s[h * TQ:(h + 1) * TQ]
        o_ref[0, :, h * HEAD_DIM:(h + 1) * HEAD_DIM] = o.astype(o_ref.dtype)


def _nsa(proj3, bias_c, toep, posk, w1k, w2k, posv, w1v, w2v):
    b, s, _ = proj3.shape
    nr = s // CMP_STRIDE
    gq = NSA_GROUP * HEAD_DIM

    def kvspec(idx):
        return pl.BlockSpec((1, s, HEAD_DIM), lambda bi, g, qi, idx=idx: (bi, 0, P_KV // HEAD_DIM + 2 * idx + g))

    def whole(a):
        return pl.BlockSpec(a.shape, lambda bi, g, qi, nd=a.ndim: (0,) * nd)

    in_specs = [
        pl.BlockSpec((1, TQ, gq), lambda bi, g, qi: (bi, qi, P_Q // gq + g)),
        kvspec(0), kvspec(1), kvspec(2), kvspec(3), kvspec(4), kvspec(5),
        pl.BlockSpec((1, TQ, LANE), lambda bi, g, qi: (bi, qi, P_GATE // LANE)),
        pl.BlockSpec((1, TQ, gq), lambda bi, g, qi: (bi, qi, P_ZA // gq + g)),
        pl.BlockSpec((NSA_GROUP, TQ, nr), lambda bi, g, qi: (g, qi, 0)),
        pl.BlockSpec((NSA_GROUP, NT_ALL, TB, TB), lambda bi, g, qi: (g, 0, 0, 0)),
        whole(posk), whole(w1k), whole(w2k), whole(posv), whole(w1v), whole(w2v),
    ]
    return pl.pallas_call(
        _nsa_kernel,
        grid=(b, NSA_KV_HEADS, s // TQ),
        in_specs=in_specs,
        out_specs=pl.BlockSpec((1, TQ, gq), lambda bi, g, qi: (bi, qi, g)),
        out_shape=jax.ShapeDtypeStruct((b, s, NSA_WIDTH), BF16),
        scratch_shapes=[pltpu.VMEM((nr, HEAD_DIM), BF16), pltpu.VMEM((nr, HEAD_DIM), BF16),
                        pltpu.VMEM((s // TQ, NSA_GROUP * TQ, TQ), F32),
                        pltpu.VMEM((NSA_GROUP * TQ, LANE), F32),
                        pltpu.VMEM((NSA_GROUP * TQ, 2 * HEAD_DIM), F32)],
        compiler_params=pltpu.CompilerParams(
            dimension_semantics=("parallel", "parallel", "arbitrary"),
            vmem_limit_bytes=VMEM_LIMIT_BYTES),
        name="nsa",
    )(proj3, proj3, proj3, proj3, proj3, proj3, proj3, proj3, proj3,
      bias_c, toep, posk, w1k, w2k, posv, w1v, w2v)


def _shift_mix(ref, prev_ref, mu, sl):
    x = ref[0, :, sl]
    prev = jnp.where(_iota(x.shape, 0) == 0, prev_ref[:, sl], pltpu.roll(x, 1, 0))
    prev_ref[:, sl] = x[x.shape[0] - 1:]
    return x + mu * (prev - x)


def _rwkv_kernel(r_ref, k_ref, v_ref, wa_ref, zb_ref, vec_ref, muwa_ref, w2_ref, a2_ref,
                 o_ref, st_ref, pr_ref, pk_ref, pv_ref, pwa_ref):
    first = pl.program_id(2) == 0
    tb = r_ref.shape[1]
    n_groups = r_ref.shape[2] // GW
    n_chunks = tb // CH

    @pl.when(first)
    def _():
        for ref in (st_ref, pr_ref, pk_ref, pv_ref, pwa_ref):
            ref[...] = jnp.zeros_like(ref)

    wa = _shift_mix(wa_ref, pwa_ref, muwa_ref[...], slice(0, LANE))
    wd_act = jnp.tanh(wa[:, :LORA]).astype(BF16)
    ad = wa[:, LORA:].astype(BF16)

    seg = (_div_pow2(_iota((GW, GW), 0), RWKV_HEAD_DIM) == _div_pow2(_iota((GW, GW), 1), RWKV_HEAD_DIM))
    segf = seg.astype(F32)
    segb = seg.astype(BF16)
    assert CH == RWKV_HEAD_DIM
    lane_s = _mod_pow2(_iota((CH, GW), 1), CH)
    row_t = _iota((CH, GW), 0)
    strict = lane_s < row_t
    incl = lane_s <= row_t
    eye = (lane_s == row_t).astype(F32)
    ti_r, ti_c = _iota((tb, tb), 0), _iota((tb, tb), 1)
    trib = ((ti_c <= ti_r) & (_div_pow2(ti_c, CH) == _div_pow2(ti_r, CH))).astype(BF16)

    def bd(x):
        xb = x.astype(BF16)
        return jnp.concatenate([xb] * HPG, axis=0) * segb

    def prep(gi):
        lanes = slice(gi * GW, (gi + 1) * GW)
        vec = vec_ref[:, lanes]
        mu_r, mu_k, mu_v = vec[0:1], vec[1:2], vec[2:3]
        w0, a0, k_k, k_a = vec[3:4], vec[4:5], vec[5:6], vec[6:7]
        r = _shift_mix(r_ref, pr_ref, mu_r, lanes)
        k = _shift_mix(k_ref, pk_ref, mu_k, lanes)
        v = _shift_mix(v_ref, pv_ref, mu_v, lanes)
        w_lora = _mm(wd_act, w2_ref[:, lanes])
        a_lora = _mm(ad, a2_ref[:, lanes])
        kk = k * k_k
        kk_ss = _mm((kk * kk).astype(BF16), segb)
        yield
        lw = jax.nn.sigmoid(w0 + w_lora) * (-math.exp(-0.5) * LOG2E)
        cum = _mm_split_rhs(trib, lw)
        yield
        a_sig = jax.nn.sigmoid(a0 + a_lora)
        kk = kk * lax.rsqrt(jnp.maximum(kk_ss, 1e-24))
        k = k * (1.0 + (a_sig - 1.0) * k_a)
        return dict(r=r, k=k, v=v, a=-kk, b=kk * a_sig, lw=lw, cum=cum, vec=vec, lanes=lanes)

    groups = _interleave([prep(gi) for gi in range(n_groups)])

    def chunk_local(g, c):
        ts = slice(c * CH, (c + 1) * CH)
        rc, kc, vc, ac, bc, lwc, cum = (g[n][ts] for n in ("r", "k", "v", "a", "b", "lw", "cum"))
        tot = cum[CH - 1:CH]
        e_out = jnp.exp2(-cum)
        e_end = jnp.exp2(tot - cum)
        r_t = rc * jnp.exp2(cum)
        a_t = ac * jnp.exp2(cum - lwc)
        lhs = jnp.concatenate([a_t, r_t], axis=0).astype(BF16)
        aa = _mm_nt(lhs, jnp.concatenate([bd(bc * e_out), bd(kc * e_out)], axis=0))
        yield
        a_ab = jnp.where(strict, aa[:CH, :GW], 0.0)
        a_ak = jnp.where(strict, aa[:CH, GW:], 0.0)
        a_rb = jnp.where(incl, aa[CH:, :GW], 0.0)
        a_rk = jnp.where(incl, aa[CH:, GW:], 0.0)
        t_inv = eye + a_ab
        mpow = _mm(a_ab.astype(BF16), bd(a_ab))
        av = _mm(a_ak.astype(BF16), bd(vc))
        yield
        for _ in range(int(math.log2(CH)) - 1):
            res = _mm(jnp.concatenate([t_inv, mpow], axis=0).astype(BF16), bd(mpow))
            yield
            t_inv = t_inv + res[:CH]
            mpow = res[CH:]
        wu = _mm(t_inv.astype(BF16), jnp.concatenate([bd(a_t), bd(av)], axis=1))
        yield
        return dict(
            lhs=jnp.concatenate([wu[:, :GW], r_t], axis=0).astype(BF16), u_loc=wu[:, GW:],
            a_r=jnp.concatenate([a_rb, a_rk], axis=1).astype(BF16), bdv=bd(vc), vc=vc,
            bk_end=jnp.concatenate([bc * e_end, kc * e_end], axis=0).astype(BF16), dec=jnp.exp2(tot))

    loc = _interleave([chunk_local(g, c) for g in groups for c in range(n_chunks)])

    def chain(gi):
        ys = []
        g_state = st_ref[gi]
        for c in range(n_chunks):
            lc = loc[gi * n_chunks + c]
            x0 = _mm_nt(lc["lhs"], g_state.astype(BF16))
            yield
            u = x0[:CH] + lc["u_loc"]
            y_c = _mm(lc["a_r"], jnp.concatenate([bd(u), lc["bdv"]], axis=0))
            upd = _mm_tn(jnp.concatenate([u, lc["vc"]], axis=0).astype(BF16), lc["bk_end"])
            yield
            ys.append(x0[CH:] + y_c)
            g_state = g_state * lc["dec"] + upd * segf
        st_ref[gi] = g_state
        return jnp.concatenate(ys, axis=0)

    ys = _interleave([chain(gi) for gi in range(n_groups)])

    def finish(g, y):
        vec = g["vec"]
        ln_w, ln_b, r_k = vec[7:8], vec[8:9], vec[9:10]
        inv_n = 1.0 / RWKV_HEAD_DIM
        mean = _mm(y.astype(BF16), segb) * inv_n
        bonus = _mm((g["r"] * g["k"] * r_k).astype(BF16), segb) * g["v"]
        yield
        yc = y - mean
        var = _mm((yc * yc).astype(BF16), segb) * inv_n
        yield
        yn = yc * lax.rsqrt(var + RWKV_GN_EPS) * ln_w + ln_b
        zb = zb_ref[0, :, g["lanes"]]
        o_ref[0, :, g["lanes"]] = ((yn + bonus) * (zb * jax.nn.sigmoid(zb))).astype(o_ref.dtype)

    _interleave([finish(g, y) for g, y in zip(groups, ys)])


def _rwkv(proj3, vecs, mu_wa, w2, a2, tb, gps):
    b, s, _ = proj3.shape
    gw = gps * GW
    ng = RWKV_WIDTH // gw

    def col(off):
        return pl.BlockSpec((1, tb, gw), lambda bi, g, ti, off=off: (bi, ti, off // gw + g))

    in_specs = [
        col(P_RKV), col(P_RKV + RWKV_WIDTH), col(P_RKV + 2 * RWKV_WIDTH),
        pl.BlockSpec((1, tb, LANE), lambda bi, g, ti: (bi, ti, P_WDAD // LANE)),
        col(P_ZB),
        pl.BlockSpec((vecs.shape[0], gw), lambda bi, g, ti: (0, g)),
        pl.BlockSpec((1, LANE), lambda bi, g, ti: (0, 0)),
        pl.BlockSpec((LORA, gw), lambda bi, g, ti: (0, g)),
        pl.BlockSpec((LORA, gw), lambda bi, g, ti: (0, g)),
    ]
    return pl.pallas_call(
        _rwkv_kernel,
        grid=(b, ng, s // tb),
        in_specs=in_specs,
        out_specs=pl.BlockSpec((1, tb, gw), lambda bi, g, ti: (bi, ti, g)),
        out_shape=jax.ShapeDtypeStruct((b, s, RWKV_WIDTH), BF16),
        scratch_shapes=[pltpu.VMEM((gps, GW, GW), F32), pltpu.VMEM((1, gw), F32), pltpu.VMEM((1, gw), F32),
                        pltpu.VMEM((1, gw), F32), pltpu.VMEM((1, LANE), F32)],
        compiler_params=pltpu.CompilerParams(
            dimension_semantics=("parallel", "parallel", "arbitrary")),
        name="rwkv",
    )(proj3, proj3, proj3, proj3, proj3, vecs, mu_wa, w2, a2)


def _outproj_kernel(ma_ref, mb_ref, wa_ref, wb_ref, x_ref, g_ref, o_ref):
    y = _mm(ma_ref[...], wa_ref[...]) + _mm(mb_ref[...], wb_ref[...])
    ms = jnp.mean(y * y, axis=-1, keepdims=True)
    o_ref[...] = x_ref[...] + y * lax.rsqrt(ms + NORM_EPS) * g_ref[...]


def _outproj(mix_a, mix_b, w_a, w_b, x2, g, tm):
    m, d = x2.shape
    ka, kb = mix_a.shape[1], mix_b.shape[1]
    return pl.pallas_call(
        _outproj_kernel,
        grid=(m // tm,),
        in_specs=[
            pl.BlockSpec((tm, ka), lambda i: (i, 0)),
            pl.BlockSpec((tm, kb), lambda i: (i, 0)),
            pl.BlockSpec((ka, d), lambda i: (0, 0)),
            pl.BlockSpec((kb, d), lambda i: (0, 0)),
            pl.BlockSpec((tm, d), lambda i: (i, 0)),
            pl.BlockSpec((1, d), lambda i: (0, 0)),
        ],
        out_specs=pl.BlockSpec((tm, d), lambda i: (i, 0)),
        out_shape=jax.ShapeDtypeStruct((m, d), F32),
        compiler_params=pltpu.CompilerParams(
            dimension_semantics=("parallel",), vmem_limit_bytes=VMEM_LIMIT_BYTES),
        name="outproj",
    )(mix_a, mix_b, w_a, w_b, x2, g)


_W_SEGMENTS = (
    (P_Q, R_Q, NSA_WIDTH),
    (P_RKV, R_FEAT, 3 * RWKV_WIDTH),
    (P_ZB, R_ZB, RWKV_WIDTH),
    (P_ZA, R_ZA, NSA_WIDTH),
    (P_KV, R_KV, 6 * NSA_KV_HEADS * HEAD_DIM),
    (P_WDAD, R_FEAT + 3 * RWKV_WIDTH, 2 * LORA),
    (P_GATE, R_GATE, LANE),
)


def _relayout_kernel(w_ref, o_ref):
    for dst, src, width in _W_SEGMENTS:
        o_ref[:, dst:dst + width] = w_ref[:, src:src + width].astype(BF16)
    used = P_GATE + LANE
    o_ref[:, used:] = jnp.zeros((o_ref.shape[0], NP - used), BF16)


def _permute_w_in(w, rows):
    d, n = w.shape
    return pl.pallas_call(
        _relayout_kernel,
        grid=(d // rows,),
        in_specs=[pl.BlockSpec((rows, n), lambda i: (i, 0))],
        out_specs=pl.BlockSpec((rows, NP), lambda i: (i, 0)),
        out_shape=jax.ShapeDtypeStruct((d, NP), BF16),
        compiler_params=pltpu.CompilerParams(dimension_semantics=("parallel",)),
        name="relayout",
    )(w)


def _block(x, pre_norm_g, w_in, rel_bias_table, cmp_pos_k, cmp_pos_v, cmp_k_w1, cmp_k_w2, cmp_v_w1,
           cmp_v_w2, rwkv_mu, rwkv_w0, rwkv_w2, rwkv_a0, rwkv_a2, rwkv_k_k, rwkv_k_a, rwkv_r_k,
           rwkv_ln_w, rwkv_ln_b, w_out, post_norm_g):
    b, s, d = x.shape
    assert w_in.shape == (d, R_END) and w_out.shape == (NSA_WIDTH + RWKV_WIDTH, d), (w_in.shape, w_out.shape)
    assert s % (4 * TQ) == 0 and s >= WINDOW + TQ, s
    x2 = x.reshape(b * s, d)
    proj = _inproj(x2, pre_norm_g.reshape(1, d), _permute_w_in(w_in.astype(BF16), RELAYOUT_ROWS),
                   min(INPROJ_TM, b * s), INPROJ_TN)
    proj3 = proj.reshape(b, s, NP)

    bias_c, toep = _bias(rel_bias_table.reshape(-1), s)
    half = CMP_STRIDE * HEAD_DIM
    mix_a = _nsa(proj3, bias_c, toep,
                 cmp_pos_k.reshape(2, half), cmp_k_w1.astype(BF16), cmp_k_w2.astype(BF16),
                 cmp_pos_v.reshape(2, half), cmp_v_w1.astype(BF16), cmp_v_w2.astype(BF16))

    w3 = 3 * RWKV_WIDTH
    vec_rows = [rwkv_mu[:RWKV_WIDTH], rwkv_mu[RWKV_WIDTH:2 * RWKV_WIDTH], rwkv_mu[2 * RWKV_WIDTH:w3],
                rwkv_w0, rwkv_a0, rwkv_k_k, rwkv_k_a, rwkv_ln_w, rwkv_ln_b, rwkv_r_k.reshape(-1)]
    vecs = jnp.stack(vec_rows + [jnp.zeros_like(rwkv_w0)] * (16 - len(vec_rows)), axis=0)
    mix_b = _rwkv(proj3, vecs, rwkv_mu[w3:].reshape(1, 2 * LORA), rwkv_w2.astype(BF16),
                  rwkv_a2.astype(BF16), min(RWKV_SLAB, s), RWKV_GROUPS_PER_STEP)

    w_o = w_out.astype(BF16)
    out = _outproj(mix_a.reshape(b * s, NSA_WIDTH), mix_b.reshape(b * s, RWKV_WIDTH),
                   w_o[:NSA_WIDTH], w_o[NSA_WIDTH:], x2, post_norm_g.reshape(1, d), min(OUTPROJ_TM, b * s))
    return out.reshape(b, s, d)


def kernel(x, pre_norm_g, w_in, rel_bias_table, cmp_pos_k, cmp_pos_v, cmp_k_w1, cmp_k_w2, cmp_v_w1,
           cmp_v_w2, rwkv_mu, rwkv_w0, rwkv_w2, rwkv_a0, rwkv_a2, rwkv_k_k, rwkv_k_a, rwkv_r_k,
           rwkv_ln_w, rwkv_ln_b, w_out, post_norm_g):
    h = x
    for l in range(pre_norm_g.shape[0]):
        h = _block(h, pre_norm_g[l], w_in[l], rel_bias_table, cmp_pos_k[l], cmp_pos_v[l], cmp_k_w1[l],
                   cmp_k_w2[l], cmp_v_w1[l], cmp_v_w2[l], rwkv_mu[l], rwkv_w0[l], rwkv_w2[l],
                   rwkv_a0[l], rwkv_a2[l], rwkv_k_k[l], rwkv_k_a[l], rwkv_r_k[l], rwkv_ln_w[l],
                   rwkv_ln_b[l], w_out[l], post_norm_g[l])
    return h
```

```python
import math

import jax
import jax.numpy as jnp
from jax import lax
from jax.experimental import pallas as pl
from jax.experimental.pallas import tpu as pltpu

F32 = jnp.float32
BF16 = jnp.bfloat16

NSA_HEADS = 8
NSA_KV_HEADS = 2
NSA_GROUP = NSA_HEADS // NSA_KV_HEADS
HEAD_DIM = 128
NSA_WIDTH = NSA_HEADS * HEAD_DIM
CMP_BLOCK = 32
CMP_STRIDE = 16
SLC_BLOCK = 64
SLC_TOP_N = 16
WINDOW = 512
RWKV_WIDTH = 1024
RWKV_HEAD_DIM = 64
LORA = 64
NUM_BUCKETS = 32
MAX_DISTANCE = 1024
NORM_EPS = 1e-6
RWKV_GN_EPS = 64e-5

R_Q = 0
R_KV = R_Q + NSA_WIDTH
R_GATE = R_KV + 6 * NSA_KV_HEADS * HEAD_DIM
R_ZA = R_GATE + 3 * NSA_HEADS
R_FEAT = R_ZA + NSA_WIDTH
R_ZB = R_FEAT + 3 * RWKV_WIDTH + 2 * LORA
R_END = R_ZB + RWKV_WIDTH

P_Q = 0
P_RKV = 1024
P_ZB = 4096
P_ZA = 5120
P_KV = 6144
P_WDAD = 7680
P_GATE = 7808
NP = 8192

LANE = 128
TQ = 256
TB = 128
ND = 9
T_DIAG = ND
T_WEND = ND + 1
T_NONE = ND + 2
NT_ALL = ND + 3
LOG2E = math.log2(math.e)
BIG = 2.0 ** 100
CH = 64
HPG = 4
GW = HPG * RWKV_HEAD_DIM
RWKV_GROUPS_PER_STEP = 4
RWKV_BATCH_PER_STEP = 2
NEG = -1e30


VMEM_LIMIT_BYTES = 56 * 1024 * 1024
INPROJ_TM, INPROJ_TN = 1024, 2048
OUTPROJ_TM = 512
RELAYOUT_ROWS = 256
BIAS_ROWS = 512
RWKV_SLAB = 256


def _bucket_thresholds():
    exact = NUM_BUCKETS // 2
    ratio = MAX_DISTANCE // exact
    out = []
    for k in range(1, NUM_BUCKETS - exact):
        n = exact
        while n ** exact < (exact ** exact) * (ratio ** k):
            n += 1
        out.append(n)
    return out


_THR = _bucket_thresholds()


def _mm(a, b):
    return jnp.dot(a, b, preferred_element_type=F32)


def _mm_nt(a, b):
    return lax.dot_general(a, b, (((1,), (1,)), ((), ())), preferred_element_type=F32)


def _mm_tn(a, b):
    return lax.dot_general(a, b, (((0,), (0,)), ((), ())), preferred_element_type=F32)


def _split3(x):
    x1 = x.astype(BF16)
    r1 = x - x1.astype(F32)
    x2 = r1.astype(BF16)
    x3 = (r1 - x2.astype(F32)).astype(BF16)
    return x1, x2, x3


def _mm_split_rhs(a_exact, b):
    b1 = b.astype(BF16)
    b2 = (b - b1.astype(F32)).astype(BF16)
    return _mm(a_exact, b1) + _mm(a_exact, b2)


def _iota(shape, dim):
    return lax.broadcasted_iota(jnp.int32, shape, dim)


def _interleave(gens):
    results = [None] * len(gens)
    live = list(enumerate(gens))
    while live:
        still = []
        for i, g in live:
            try:
                next(g)
                still.append((i, g))
            except StopIteration as stop:
                results[i] = stop.value
        live = still
    return results


def _div_pow2(x, n):
    assert n & (n - 1) == 0
    return x >> (n.bit_length() - 1)


def _mod_pow2(x, n):
    assert n & (n - 1) == 0
    return x & (n - 1)


def _inproj_kernel(x_ref, g_ref, w_ref, o_ref, hn_ref):
    @pl.when(pl.program_id(1) == 0)
    def _():
        x = x_ref[...]
        ms = jnp.mean(x * x, axis=-1, keepdims=True)
        hn_ref[...] = (x * lax.rsqrt(ms + NORM_EPS) * g_ref[...]).astype(BF16)

    o_ref[...] = _mm(hn_ref[...], w_ref[...])


def _inproj(x2, g, w, tm, tn):
    m, d = x2.shape
    n = w.shape[1]
    return pl.pallas_call(
        _inproj_kernel,
        grid=(m // tm, n // tn),
        in_specs=[
            pl.BlockSpec((tm, d), lambda i, j: (i, 0)),
            pl.BlockSpec((1, d), lambda i, j: (0, 0)),
            pl.BlockSpec((d, tn), lambda i, j: (0, j)),
        ],
        out_specs=pl.BlockSpec((tm, tn), lambda i, j: (i, j)),
        out_shape=jax.ShapeDtypeStruct((m, n), F32),
        scratch_shapes=[pltpu.VMEM((tm, d), BF16)],
        compiler_params=pltpu.CompilerParams(
            dimension_semantics=("parallel", "arbitrary"),
            vmem_limit_bytes=VMEM_LIMIT_BYTES),
        name="inproj",
    )(x2, g, w)


def _bucket(n):
    n = jnp.maximum(n, 0)
    large = jnp.full(n.shape, NUM_BUCKETS // 2, jnp.int32)
    for thr in _THR:
        large = large + (n >= thr).astype(jnp.int32)
    return jnp.where(n < NUM_BUCKETS // 2, n, large)


def _lookup_all_heads(dist, tab_ref):
    bucket = _bucket(dist)
    hits = [bucket == b for b in range(NUM_BUCKETS)]
    outs = []
    for h in range(NSA_HEADS):
        out = jnp.zeros(dist.shape, F32)
        for b in range(NUM_BUCKETS):
            out = jnp.where(hits[b], tab_ref[b * NSA_HEADS + h] * LOG2E, out)
        outs.append(out)
    return outs


def _bias_kernel(tab_ref, bc_ref, tp_ref):
    i = pl.program_id(0)
    rows, nr = bc_ref.shape[1], bc_ref.shape[2]
    dist_c = (i * rows + _iota((rows, nr), 0)) - (_iota((rows, nr), 1) * CMP_STRIDE + (CMP_BLOCK - 1))
    for h, vals in enumerate(_lookup_all_heads(dist_c, tab_ref)):
        bc_ref[h] = vals

    @pl.when(i == 0)
    def _():
        base = _iota((TB, TB), 0) - _iota((TB, TB), 1)
        neg = jnp.full((TB, TB), NEG, F32)
        for h in range(NSA_HEADS):
            tp_ref[h, T_NONE] = neg
        for d in range(ND):
            for h, vals in enumerate(_lookup_all_heads(base + d * TB, tab_ref)):
                tp_ref[h, d] = vals
                if d == 0:
                    tp_ref[h, T_DIAG] = jnp.where(base >= 0, vals, neg)
                if d == WINDOW // TB:
                    tp_ref[h, T_WEND] = jnp.where(base < 0, vals, neg)


def _bias(table_flat, s):
    nr = s // CMP_STRIDE
    rows = min(BIAS_ROWS, s)
    return pl.pallas_call(
        _bias_kernel,
        grid=(s // rows,),
        in_specs=[pl.BlockSpec(memory_space=pltpu.SMEM)],
        out_specs=[
            pl.BlockSpec((NSA_HEADS, rows, nr), lambda i: (0, i, 0)),
            pl.BlockSpec((NSA_HEADS, NT_ALL, TB, TB), lambda i: (0, 0, 0, 0)),
        ],
        out_shape=[
            jax.ShapeDtypeStruct((NSA_HEADS, s, nr), F32),
            jax.ShapeDtypeStruct((NSA_HEADS, NT_ALL, TB, TB), F32),
        ],
        compiler_params=pltpu.CompilerParams(dimension_semantics=("arbitrary",)),
        name="bias",
    )(table_flat)


def _compress(kv_ref, pos_ref, w1_ref, w2_ref, nr):
    half = CMP_STRIDE * HEAD_DIM
    r = jnp.concatenate(
        [kv_ref[0, pl.ds(m, nr, stride=CMP_STRIDE), :] for m in range(CMP_STRIDE)], axis=1)
    a = _mm((r + pos_ref[0:1, :]).astype(BF16), w1_ref[0:half, :])
    b = _mm((r + pos_ref[1:2, :]).astype(BF16), w1_ref[half:2 * half, :])
    pre = a + pltpu.roll(b, nr - 1, 0)
    h1 = pre * jax.nn.sigmoid(pre)
    return _mm(h1.astype(BF16), w2_ref[...])


def _bias_head(toep_ref, h, d_tiles, window, valid=True):
    sub = TQ // TB
    rows = []
    for ri in range(sub):
        cols = []
        for ci in range(sub):
            d = d_tiles + ri - ci
            idx = jnp.where(d == 0, T_DIAG, jnp.minimum(d, ND - 1))
            if window:
                idx = jnp.where(d == WINDOW // TB, T_WEND, jnp.where(d > WINDOW // TB, T_NONE, idx))
            idx = jnp.where((d < 0) | jnp.logical_not(valid), T_NONE, idx)
            cols.append(toep_ref[h, idx])
        rows.append(jnp.concatenate(cols, axis=1))
    return jnp.concatenate(rows, axis=0)


def _bias_tile(toep_ref, d_tiles, window, valid=True):
    return jnp.concatenate([_bias_head(toep_ref, h, d_tiles, window, valid) for h in range(NSA_GROUP)], axis=0)


def _with_ones(v):
    return jnp.concatenate([v, jnp.ones(v.shape, v.dtype)], axis=1)


def _fold_lanes(x, op):
    out = x[:, :LANE]
    for c in range(1, x.shape[1] // LANE):
        out = op(out, x[:, c * LANE:(c + 1) * LANE])
    return out


def _nsa_kernel(q_ref, kc_ref, vc_ref, ks_ref, vs_ref, kw_ref, vw_ref, gate_ref, za_ref,
                bc_ref, toep_ref, posk_ref, w1k_ref, w2k_ref, posv_ref, w1v_ref, w2v_ref,
                o_ref, kcs_ref, vcs_ref, s_ref, macc_ref, acc_ref):
    qi = pl.program_id(2)
    s_len = kc_ref.shape[1]
    nr = s_len // CMP_STRIDE
    nb = s_len // SLC_BLOCK
    n_sel = min(SLC_TOP_N, nb)

    @pl.when(qi == 0)
    def _():
        kcs_ref[...] = _compress(kc_ref, posk_ref, w1k_ref, w2k_ref, nr).astype(BF16)
        vcs_ref[...] = _compress(vc_ref, posv_ref, w1v_ref, w2v_ref, nr).astype(BF16)

    q0 = qi * TQ
    rows = NSA_GROUP * TQ
    q = q_ref[0] * (HEAD_DIM ** -0.5 * LOG2E)
    q4 = jnp.concatenate([q[:, h * HEAD_DIM:(h + 1) * HEAD_DIM] for h in range(NSA_GROUP)],
                         axis=0).astype(BF16)

    sub = TQ // TB
    n_tiles = s_len // TQ

    def compressed_and_selection():
        t_c = q0 + _iota((TQ, nr), 0)
        i_c = _iota((TQ, nr), 1)
        mask_c = (t_c - (i_c * CMP_STRIDE + (CMP_BLOCK - 1)) >= 0) & (i_c < nr - 1)
        qk = _mm_nt(q4, kcs_ref[...])
        mask_add = jnp.where(mask_c, 0.0, NEG)
        yield
        p_heads = []
        for h in range(NSA_GROUP):
            lg = qk[h * TQ:(h + 1) * TQ] + bc_ref[h] + mask_add
            m_c = jnp.max(lg, axis=-1, keepdims=True)
            yield
            e = jnp.where(lg > 0.5 * NEG, jnp.exp2(lg - m_c), 0.0)
            l_c = jnp.sum(e, axis=-1, keepdims=True)
            yield
            p_heads.append(e / jnp.maximum(l_c, 1e-30))
        p = jnp.concatenate(p_heads, axis=0)
        o_c = _mm(p.astype(BF16), vcs_ref[...])
        psum = p_heads[0]
        for h in range(1, NSA_GROUP):
            psum = psum + p_heads[h]
        ov_i = _iota((nb, nr), 1) * CMP_STRIDE
        ov_j = _iota((nb, nr), 0) * SLC_BLOCK
        ov_t = ((ov_i < ov_j + SLC_BLOCK) & (ov_i + CMP_BLOCK > ov_j)).astype(BF16)
        p1, p2, p3 = _split3(psum)
        imp_t = _mm_nt(ov_t, p1) + _mm_nt(ov_t, p2) + _mm_nt(ov_t, p3)
        yield
        jb = _iota((nb, TQ), 0)
        cur = _div_pow2(q0 + _iota((nb, TQ), 1), SLC_BLOCK)
        forced = (jb == 0) | (jb == cur) | (jb == cur - 1)
        causal = jb <= cur
        score = jnp.where(forced, jnp.inf, jnp.where(causal, imp_t, -jnp.inf))
        rank = jnp.zeros((nb, TQ), jnp.int32)
        for jp in range(nb):
            sj = score[jp:jp + 1, :]
            beats = (sj > score) | ((sj == score) & (jb > jp))
            rank = rank + beats.astype(jnp.int32)
            if jp % 8 == 7:
                yield
        sel_t = ((rank < n_sel) & causal).astype(BF16)
        place = (_iota((nb, LANE), 0) == _iota((nb, LANE), 1)).astype(BF16)
        unsel = _mm_tn(sel_t, place) - (_iota((TQ, LANE), 1) < nb).astype(F32)
        yield
        return o_c, unsel.astype(BF16)

    def window():
        n_band = WINDOW // TQ + 1
        qk, tiles, v_w = [], [], []
        for c in range(n_band):
            j = qi - (n_band - 1) + c
            jc = jnp.maximum(j, 0)
            k0 = pl.multiple_of(jc * TQ, TQ)
            tiles.append((j, jc))
            qk.append(_mm_nt(q4, kw_ref[0, pl.ds(k0, TQ), :].astype(BF16)))
            v_w.append(_with_ones(vw_ref[0, pl.ds(k0, TQ), :].astype(BF16)))
        yield
        heads = range(NSA_GROUP)
        s_w = [[None] * NSA_GROUP for _ in range(n_band)]
        m_w = [None] * NSA_GROUP
        for c in range(n_band):
            for h in heads:
                s = qk[c][h * TQ:(h + 1) * TQ] + _bias_head(toep_ref, h, (qi - tiles[c][1]) * sub, True,
                                                             tiles[c][0] >= 0)
                s_w[c][h] = s
                m_w[h] = s if c == 0 else jnp.maximum(m_w[h], s)
                yield
        for h in heads:
            m_w[h] = jnp.max(m_w[h], axis=-1, keepdims=True)
        yield
        v_band = jnp.concatenate(v_w, axis=0)
        acc_w = []
        for h in heads:
            p_h = []
            for c in range(n_band):
                p_h.append(jnp.exp2(s_w[c][h] - m_w[h]).astype(BF16))
                yield
            acc_w.append(_mm(jnp.concatenate(p_h, axis=1), v_band))
            yield
        return jnp.concatenate([a[:, :HEAD_DIM] / jnp.maximum(a[:, HEAD_DIM:], 1e-30) for a in acc_w], axis=0)

    def gates():
        gts = jax.nn.sigmoid(gate_ref[0])
        grp = pl.program_id(1)
        za = za_ref[0]
        out = [[None] * NSA_GROUP for _ in range(3)]
        for h in range(NSA_GROUP):
            z = za[:, h * HEAD_DIM:(h + 1) * HEAD_DIM]
            zs = z * jax.nn.sigmoid(z)
            for branch in range(3):
                lane = branch * NSA_HEADS + h
                col = gts[:, lane:lane + 1]
                for g in range(1, NSA_KV_HEADS):
                    lg = lane + g * NSA_GROUP
                    col = jnp.where(grp == g, gts[:, lg:lg + 1], col)
                out[branch][h] = zs * col
                yield
        return out

    (o_c, unsel), o_w, gz = _interleave([compressed_and_selection(), window(), gates()])
    o_cw = [gz[0][h] * o_c[h * TQ:(h + 1) * TQ] + gz[2][h] * o_w[h * TQ:(h + 1) * TQ] for h in range(NSA_GROUP)]

    q_sel = jnp.concatenate([q4, jnp.concatenate([unsel] * NSA_GROUP, axis=0)], axis=1)

    def key_tile(j):
        k0 = pl.multiple_of(j * TQ, TQ)
        blk = _div_pow2(j * TQ + _iota((TQ, LANE), 0), SLC_BLOCK)
        marks = jnp.where(_iota((TQ, LANE), 1) == blk, BIG, 0.0).astype(BF16)
        return jnp.concatenate([ks_ref[0, pl.ds(k0, TQ), :].astype(BF16), marks], axis=1)

    assert n_tiles % 4 == 0
    n_pairs = _div_pow2(qi + 2, 2)
    passes, first = [], 0
    for tiles_per_iter in (8, 4, 2):
        if tiles_per_iter <= n_tiles:
            trips = _div_pow2(2 * n_pairs - first, tiles_per_iter)
            passes.append((tiles_per_iter, first, trips))
            first = first + tiles_per_iter * trips

    def slc_logits(tpi, base):
        def body(it, carry):
            tiles = [base + tpi * it + c for c in range(tpi)]
            qk = [_mm_nt(q_sel, key_tile(j)) for j in tiles]
            macc = macc_ref[...]
            for j, qk_j in zip(tiles, qk):
                s = qk_j + _bias_tile(toep_ref, (qi - j) * sub, False)
                s_ref[j] = s
                macc = jnp.maximum(macc, _fold_lanes(s, jnp.maximum))
            macc_ref[...] = macc
            return carry
        return body

    macc_ref[...] = jnp.full((rows, LANE), NEG, F32)
    for tpi, base, trips in passes:
        lax.fori_loop(0, trips, slc_logits(tpi, base), 0)
    m_s = jnp.max(macc_ref[...], axis=-1, keepdims=True)

    def slc_values(tpi, base):
        def body(it, carry):
            j0 = base + tpi * it
            k0 = pl.multiple_of(j0 * TQ, 2 * TQ)
            p_it = jnp.concatenate([jnp.exp2(s_ref[j0 + c] - m_s).astype(BF16) for c in range(tpi)], axis=1)
            acc_ref[...] += _mm(p_it, _with_ones(vs_ref[0, pl.ds(k0, tpi * TQ), :].astype(BF16)))
            return carry
        return body

    acc_ref[...] = jnp.zeros((rows, 2 * HEAD_DIM), F32)
    for tpi, base, trips in passes:
        lax.fori_loop(0, trips, slc_values(tpi, base), 0)
    acc = acc_ref[...]
    o_s = acc[:, :HEAD_DIM] / jnp.maximum(acc[:, HEAD_DIM:], 1e-30)

    for h in range(NSA_GROUP):
        o = o_cw[h] + gz[1][h] * o_s[h * TQ:(h + 1) * TQ]
        o_ref[0, :, h * HEAD_DIM:(h + 1) * HEAD_DIM] = o.astype(o_ref.dtype)


def _nsa(proj3, bias_c, toep, posk, w1k, w2k, posv, w1v, w2v):
    b, s, _ = proj3.shape
    nr = s // CMP_STRIDE
    gq = NSA_GROUP * HEAD_DIM

    def kvspec(idx):
        return pl.BlockSpec((1, s, HEAD_DIM), lambda bi, g, qi, idx=idx: (bi, 0, P_KV // HEAD_DIM + 2 * idx + g))

    def whole(a):
        return pl.BlockSpec(a.shape, lambda bi, g, qi, nd=a.ndim: (0,) * nd)

    in_specs = [
        pl.BlockSpec((1, TQ, gq), lambda bi, g, qi: (bi, qi, P_Q // gq + g)),
        kvspec(0), kvspec(1), kvspec(2), kvspec(3), kvspec(4), kvspec(5),
        pl.BlockSpec((1, TQ, LANE), lambda bi, g, qi: (bi, qi, P_GATE // LANE)),
        pl.BlockSpec((1, TQ, gq), lambda bi, g, qi: (bi, qi, P_ZA // gq + g)),
        pl.BlockSpec((NSA_GROUP, TQ, nr), lambda bi, g, qi: (g, qi, 0)),
        pl.BlockSpec((NSA_GROUP, NT_ALL, TB, TB), lambda bi, g, qi: (g, 0, 0, 0)),
        whole(posk), whole(w1k), whole(w2k), whole(posv), whole(w1v), whole(w2v),
    ]
    return pl.pallas_call(
        _nsa_kernel,
        grid=(b, NSA_KV_HEADS, s // TQ),
        in_specs=in_specs,
        out_specs=pl.BlockSpec((1, TQ, gq), lambda bi, g, qi: (bi, qi, g)),
        out_shape=jax.ShapeDtypeStruct((b, s, NSA_WIDTH), BF16),
        scratch_shapes=[pltpu.VMEM((nr, HEAD_DIM), BF16), pltpu.VMEM((nr, HEAD_DIM), BF16),
                        pltpu.VMEM((s // TQ, NSA_GROUP * TQ, TQ), F32),
                        pltpu.VMEM((NSA_GROUP * TQ, LANE), F32),
                        pltpu.VMEM((NSA_GROUP * TQ, 2 * HEAD_DIM), F32)],
        compiler_params=pltpu.CompilerParams(
            dimension_semantics=("parallel", "parallel", "arbitrary"),
            vmem_limit_bytes=VMEM_LIMIT_BYTES),
        name="nsa",
    )(proj3, proj3, proj3, proj3, proj3, proj3, proj3, proj3, proj3,
      bias_c, toep, posk, w1k, w2k, posv, w1v, w2v)


def _shift_mix(ref, prev_ref, mu, bb, sl):
    x = ref[bb, :, sl]
    prev = jnp.where(_iota(x.shape, 0) == 0, prev_ref[bb:bb + 1, sl], pltpu.roll(x, 1, 0))
    prev_ref[bb:bb + 1, sl] = x[x.shape[0] - 1:]
    return x + mu * (prev - x)


def _rwkv_kernel(r_ref, k_ref, v_ref, wa_ref, zb_ref, vec_ref, muwa_ref, w2_ref, a2_ref,
                 o_ref, st_ref, pr_ref, pk_ref, pv_ref, pwa_ref):
    first = pl.program_id(2) == 0
    n_batch, tb = r_ref.shape[0], r_ref.shape[1]
    n_groups = r_ref.shape[2] // GW
    n_chunks = tb // CH
    seqs = [(bb, gi) for bb in range(n_batch) for gi in range(n_groups)]

    @pl.when(first)
    def _():
        for ref in (st_ref, pr_ref, pk_ref, pv_ref, pwa_ref):
            ref[...] = jnp.zeros_like(ref)

    wd_act, ad = [], []
    for bb in range(n_batch):
        wa = _shift_mix(wa_ref, pwa_ref, muwa_ref[...], bb, slice(0, LANE))
        wd_act.append(jnp.tanh(wa[:, :LORA]).astype(BF16))
        ad.append(wa[:, LORA:].astype(BF16))

    seg = (_div_pow2(_iota((GW, GW), 0), RWKV_HEAD_DIM) == _div_pow2(_iota((GW, GW), 1), RWKV_HEAD_DIM))
    segf = seg.astype(F32)
    segb = seg.astype(BF16)
    assert CH == RWKV_HEAD_DIM
    lane_s = _mod_pow2(_iota((CH, GW), 1), CH)
    row_t = _iota((CH, GW), 0)
    strict = lane_s < row_t
    incl = lane_s <= row_t
    eye = (lane_s == row_t).astype(F32)
    ti_r, ti_c = _iota((tb, tb), 0), _iota((tb, tb), 1)
    trib = ((ti_c <= ti_r) & (_div_pow2(ti_c, CH) == _div_pow2(ti_r, CH))).astype(BF16)

    def bd(x):
        xb = x.astype(BF16)
        return jnp.concatenate([xb] * HPG, axis=0) * segb

    def prep(bb, gi):
        lanes = slice(gi * GW, (gi + 1) * GW)
        vec = vec_ref[:, lanes]
        mu_r, mu_k, mu_v = vec[0:1], vec[1:2], vec[2:3]
        w0, a0, k_k, k_a = vec[3:4], vec[4:5], vec[5:6], vec[6:7]
        r = _shift_mix(r_ref, pr_ref, mu_r, bb, lanes)
        k = _shift_mix(k_ref, pk_ref, mu_k, bb, lanes)
        v = _shift_mix(v_ref, pv_ref, mu_v, bb, lanes)
        w_lora = _mm(wd_act[bb], w2_ref[:, lanes])
        a_lora = _mm(ad[bb], a2_ref[:, lanes])
        kk = k * k_k
        kk_ss = _mm((kk * kk).astype(BF16), segb)
        yield
        lw = jax.nn.sigmoid(w0 + w_lora) * (-math.exp(-0.5) * LOG2E)
        cum = _mm_split_rhs(trib, lw)
        yield
        a_sig = jax.nn.sigmoid(a0 + a_lora)
        kk = kk * lax.rsqrt(jnp.maximum(kk_ss, 1e-24))
        k = k * (1.0 + (a_sig - 1.0) * k_a)
        return dict(r=r, k=k, v=v, a=-kk, b=kk * a_sig, lw=lw, cum=cum, vec=vec, lanes=lanes, bb=bb)

    groups = _interleave([prep(bb, gi) for bb, gi in seqs])

    def chunk_local(g, c):
        ts = slice(c * CH, (c + 1) * CH)
        rc, kc, vc, ac, bc, lwc, cum = (g[n][ts] for n in ("r", "k", "v", "a", "b", "lw", "cum"))
        tot = cum[CH - 1:CH]
        e_out = jnp.exp2(-cum)
        e_end = jnp.exp2(tot - cum)
        r_t = rc * jnp.exp2(cum)
        a_t = ac * jnp.exp2(cum - lwc)
        lhs = jnp.concatenate([a_t, r_t], axis=0).astype(BF16)
        aa = _mm_nt(lhs, jnp.concatenate([bd(bc * e_out), bd(kc * e_out)], axis=0))
        yield
        a_ab = jnp.where(strict, aa[:CH, :GW], 0.0)
        a_ak = jnp.where(strict, aa[:CH, GW:], 0.0)
        a_rb = jnp.where(incl, aa[CH:, :GW], 0.0)
        a_rk = jnp.where(incl, aa[CH:, GW:], 0.0)
        t_inv = eye + a_ab
        mpow = _mm(a_ab.astype(BF16), bd(a_ab))
        av = _mm(a_ak.astype(BF16), bd(vc))
        yield
        for _ in range(int(math.log2(CH)) - 1):
            res = _mm(jnp.concatenate([t_inv, mpow], axis=0).astype(BF16), bd(mpow))
            yield
            t_inv = t_inv + res[:CH]
            mpow = res[CH:]
        wu = _mm(t_inv.astype(BF16), jnp.concatenate([bd(a_t), bd(av)], axis=1))
        yield
        return dict(
            lhs=jnp.concatenate([wu[:, :GW], r_t], axis=0).astype(BF16), u_loc=wu[:, GW:],
            a_r=jnp.concatenate([a_rb, a_rk], axis=1).astype(BF16), bdv=bd(vc), vc=vc,
            bk_end=jnp.concatenate([bc * e_end, kc * e_end], axis=0).astype(BF16), dec=jnp.exp2(tot))

    loc = _interleave([chunk_local(g, c) for g in groups for c in range(n_chunks)])

    def chain(q):
        ys = []
        g_state = st_ref[q]
        for c in range(n_chunks):
            lc = loc[q * n_chunks + c]
            x0 = _mm_nt(lc["lhs"], g_state.astype(BF16))
            yield
            u = x0[:CH] + lc["u_loc"]
            y_c = _mm(lc["a_r"], jnp.concatenate([bd(u), lc["bdv"]], axis=0))
            upd = _mm_tn(jnp.concatenate([u, lc["vc"]], axis=0).astype(BF16), lc["bk_end"])
            yield
            ys.append(x0[CH:] + y_c)
            g_state = g_state * lc["dec"] + upd * segf
        st_ref[q] = g_state
        return jnp.concatenate(ys, axis=0)

    ys = _interleave([chain(q) for q in range(len(seqs))])

    def finish(g, y):
        vec = g["vec"]
        ln_w, ln_b, r_k = vec[7:8], vec[8:9], vec[9:10]
        inv_n = 1.0 / RWKV_HEAD_DIM
        mean = _mm(y.astype(BF16), segb) * inv_n
        bonus = _mm((g["r"] * g["k"] * r_k).astype(BF16), segb) * g["v"]
        yield
        yc = y - mean
        var = _mm((yc * yc).astype(BF16), segb) * inv_n
        yield
        yn = yc * lax.rsqrt(var + RWKV_GN_EPS) * ln_w + ln_b
        zb = zb_ref[g["bb"], :, g["lanes"]]
        o_ref[g["bb"], :, g["lanes"]] = ((yn + bonus) * (zb * jax.nn.sigmoid(zb))).astype(o_ref.dtype)

    _interleave([finish(g, y) for g, y in zip(groups, ys)])


def _rwkv(proj3, vecs, mu_wa, w2, a2, tb, gps, bps):
    b, s, _ = proj3.shape
    assert b % bps == 0, (b, bps)
    gw = gps * GW
    ng = RWKV_WIDTH // gw

    def col(off):
        return pl.BlockSpec((bps, tb, gw), lambda bi, g, ti, off=off: (bi, ti, off // gw + g))

    in_specs = [
        col(P_RKV), col(P_RKV + RWKV_WIDTH), col(P_RKV + 2 * RWKV_WIDTH),
        pl.BlockSpec((bps, tb, LANE), lambda bi, g, ti: (bi, ti, P_WDAD // LANE)),
        col(P_ZB),
        pl.BlockSpec((vecs.shape[0], gw), lambda bi, g, ti: (0, g)),
        pl.BlockSpec((1, LANE), lambda bi, g, ti: (0, 0)),
        pl.BlockSpec((LORA, gw), lambda bi, g, ti: (0, g)),
        pl.BlockSpec((LORA, gw), lambda bi, g, ti: (0, g)),
    ]
    return pl.pallas_call(
        _rwkv_kernel,
        grid=(b // bps, ng, s // tb),
        in_specs=in_specs,
        out_specs=pl.BlockSpec((bps, tb, gw), lambda bi, g, ti: (bi, ti, g)),
        out_shape=jax.ShapeDtypeStruct((b, s, RWKV_WIDTH), BF16),
        scratch_shapes=[pltpu.VMEM((bps * gps, GW, GW), F32), pltpu.VMEM((bps, gw), F32),
                        pltpu.VMEM((bps, gw), F32), pltpu.VMEM((bps, gw), F32), pltpu.VMEM((bps, LANE), F32)],
        compiler_params=pltpu.CompilerParams(
            dimension_semantics=("parallel", "parallel", "arbitrary")),
        name="rwkv",
    )(proj3, proj3, proj3, proj3, proj3, vecs, mu_wa, w2, a2)


def _outproj_kernel(ma_ref, mb_ref, wa_ref, wb_ref, x_ref, g_ref, o_ref):
    y = _mm(ma_ref[...], wa_ref[...]) + _mm(mb_ref[...], wb_ref[...])
    ms = jnp.mean(y * y, axis=-1, keepdims=True)
    o_ref[...] = x_ref[...] + y * lax.rsqrt(ms + NORM_EPS) * g_ref[...]


def _outproj(mix_a, mix_b, w_a, w_b, x2, g, tm):
    m, d = x2.shape
    ka, kb = mix_a.shape[1], mix_b.shape[1]
    return pl.pallas_call(
        _outproj_kernel,
        grid=(m // tm,),
        in_specs=[
            pl.BlockSpec((tm, ka), lambda i: (i, 0)),
            pl.BlockSpec((tm, kb), lambda i: (i, 0)),
            pl.BlockSpec((ka, d), lambda i: (0, 0)),
            pl.BlockSpec((kb, d), lambda i: (0, 0)),
            pl.BlockSpec((tm, d), lambda i: (i, 0)),
            pl.BlockSpec((1, d), lambda i: (0, 0)),
        ],
        out_specs=pl.BlockSpec((tm, d), lambda i: (i, 0)),
        out_shape=jax.ShapeDtypeStruct((m, d), F32),
        compiler_params=pltpu.CompilerParams(
            dimension_semantics=("parallel",), vmem_limit_bytes=VMEM_LIMIT_BYTES),
        name="outproj",
    )(mix_a, mix_b, w_a, w_b, x2, g)


_W_SEGMENTS = (
    (P_Q, R_Q, NSA_WIDTH),
    (P_RKV, R_FEAT, 3 * RWKV_WIDTH),
    (P_ZB, R_ZB, RWKV_WIDTH),
    (P_ZA, R_ZA, NSA_WIDTH),
    (P_KV, R_KV, 6 * NSA_KV_HEADS * HEAD_DIM),
    (P_WDAD, R_FEAT + 3 * RWKV_WIDTH, 2 * LORA),
    (P_GATE, R_GATE, LANE),
)


def _relayout_kernel(w_ref, o_ref):
    for dst, src, width in _W_SEGMENTS:
        o_ref[:, dst:dst + width] = w_ref[:, src:src + width].astype(BF16)
    used = P_GATE + LANE
    o_ref[:, used:] = jnp.zeros((o_ref.shape[0], NP - used), BF16)


def _permute_w_in(w, rows):
    d, n = w.shape
    return pl.pallas_call(
        _relayout_kernel,
        grid=(d // rows,),
        in_specs=[pl.BlockSpec((rows, n), lambda i: (i, 0))],
        out_specs=pl.BlockSpec((rows, NP), lambda i: (i, 0)),
        out_shape=jax.ShapeDtypeStruct((d, NP), BF16),
        compiler_params=pltpu.CompilerParams(dimension_semantics=("parallel",)),
        name="relayout",
    )(w)


def _block(x, pre_norm_g, w_in, rel_bias_table, cmp_pos_k, cmp_pos_v, cmp_k_w1, cmp_k_w2, cmp_v_w1,
           cmp_v_w2, rwkv_mu, rwkv_w0, rwkv_w2, rwkv_a0, rwkv_a2, rwkv_k_k, rwkv_k_a, rwkv_r_k,
           rwkv_ln_w, rwkv_ln_b, w_out, post_norm_g):
    b, s, d = x.shape
    assert w_in.shape == (d, R_END) and w_out.shape == (NSA_WIDTH + RWKV_WIDTH, d), (w_in.shape, w_out.shape)
    assert s % (4 * TQ) == 0 and s >= WINDOW + TQ, s
    x2 = x.reshape(b * s, d)
    proj = _inproj(x2, pre_norm_g.reshape(1, d), _permute_w_in(w_in.astype(BF16), RELAYOUT_ROWS),
                   min(INPROJ_TM, b * s), INPROJ_TN)
    proj3 = proj.reshape(b, s, NP)

    bias_c, toep = _bias(rel_bias_table.reshape(-1), s)
    half = CMP_STRIDE * HEAD_DIM
    mix_a = _nsa(proj3, bias_c, toep,
                 cmp_pos_k.reshape(2, half), cmp_k_w1.astype(BF16), cmp_k_w2.astype(BF16),
                 cmp_pos_v.reshape(2, half), cmp_v_w1.astype(BF16), cmp_v_w2.astype(BF16))

    w3 = 3 * RWKV_WIDTH
    vec_rows = [rwkv_mu[:RWKV_WIDTH], rwkv_mu[RWKV_WIDTH:2 * RWKV_WIDTH], rwkv_mu[2 * RWKV_WIDTH:w3],
                rwkv_w0, rwkv_a0, rwkv_k_k, rwkv_k_a, rwkv_ln_w, rwkv_ln_b, rwkv_r_k.reshape(-1)]
    vecs = jnp.stack(vec_rows + [jnp.zeros_like(rwkv_w0)] * (16 - len(vec_rows)), axis=0)
    mix_b = _rwkv(proj3, vecs, rwkv_mu[w3:].reshape(1, 2 * LORA), rwkv_w2.astype(BF16),
                  rwkv_a2.astype(BF16), min(RWKV_SLAB, s), RWKV_GROUPS_PER_STEP, math.gcd(RWKV_BATCH_PER_STEP, b))

    w_o = w_out.astype(BF16)
    out = _outproj(mix_a.reshape(b * s, NSA_WIDTH), mix_b.reshape(b * s, RWKV_WIDTH),
                   w_o[:NSA_WIDTH], w_o[NSA_WIDTH:], x2, post_norm_g.reshape(1, d), min(OUTPROJ_TM, b * s))
    return out.reshape(b, s, d)


def kernel(x, pre_norm_g, w_in, rel_bias_table, cmp_pos_k, cmp_pos_v, cmp_k_w1, cmp_k_w2, cmp_v_w1,
           cmp_v_w2, rwkv_mu, rwkv_w0, rwkv_w2, rwkv_a0, rwkv_a2, rwkv_k_k, rwkv_k_a, rwkv_r_k,
           rwkv_ln_w, rwkv_ln_b, w_out, post_norm_g):
    h = x
    for l in range(pre_norm_g.shape[0]):
        h = _block(h, pre_norm_g[l], w_in[l], rel_bias_table, cmp_pos_k[l], cmp_pos_v[l], cmp_k_w1[l],
                   cmp_k_w2[l], cmp_v_w1[l], cmp_v_w2[l], rwkv_mu[l], rwkv_w0[l], rwkv_w2[l],
                   rwkv_a0[l], rwkv_a2[l], rwkv_k_k[l], rwkv_k_a[l], rwkv_r_k[l], rwkv_ln_w[l],
                   rwkv_ln_b[l], w_out[l], post_norm_g[l])
    return h
```

```python
import math

import jax
import jax.numpy as jnp
from jax import lax
from jax.experimental import pallas as pl
from jax.experimental.pallas import tpu as pltpu

F32 = jnp.float32
BF16 = jnp.bfloat16

NSA_HEADS = 8
NSA_KV_HEADS = 2
NSA_GROUP = NSA_HEADS // NSA_KV_HEADS
HEAD_DIM = 128
NSA_WIDTH = NSA_HEADS * HEAD_DIM
CMP_BLOCK = 32
CMP_STRIDE = 16
SLC_BLOCK = 64
SLC_TOP_N = 16
WINDOW = 512
RWKV_WIDTH = 1024
RWKV_HEAD_DIM = 64
LORA = 64
NUM_BUCKETS = 32
MAX_DISTANCE = 1024
NORM_EPS = 1e-6
RWKV_GN_EPS = 64e-5

R_Q = 0
R_KV = R_Q + NSA_WIDTH
R_GATE = R_KV + 6 * NSA_KV_HEADS * HEAD_DIM
R_ZA = R_GATE + 3 * NSA_HEADS
R_FEAT = R_ZA + NSA_WIDTH
R_ZB = R_FEAT + 3 * RWKV_WIDTH + 2 * LORA
R_END = R_ZB + RWKV_WIDTH

P_Q = 0
P_RKV = 1024
P_ZB = 4096
P_ZA = 5120
P_KV = 6144
P_WDAD = 7680
P_GATE = 7808
NP = 8192

LANE = 128
TQ = 256
TB = 128
ND = 9
T_DIAG = ND
T_WEND = ND + 1
T_NONE = ND + 2
NT_ALL = ND + 3
LOG2E = math.log2(math.e)
BIG = 2.0 ** 100
CH = 64
HPG = 4
GW = HPG * RWKV_HEAD_DIM
RWKV_GROUPS_PER_STEP = 4
RWKV_BATCH_PER_STEP = 4
NEG = -1e30


VMEM_LIMIT_BYTES = 56 * 1024 * 1024
INPROJ_TM, INPROJ_TN = 1024, 2048
OUTPROJ_TM = 512
RELAYOUT_ROWS = 256
BIAS_ROWS = 512
RWKV_SLAB = 128


def _bucket_thresholds():
    exact = NUM_BUCKETS // 2
    ratio = MAX_DISTANCE // exact
    out = []
    for k in range(1, NUM_BUCKETS - exact):
        n = exact
        while n ** exact < (exact ** exact) * (ratio ** k):
            n += 1
        out.append(n)
    return out


_THR = _bucket_thresholds()


def _mm(a, b):
    return jnp.dot(a, b, preferred_element_type=F32)


def _mm_nt(a, b):
    return lax.dot_general(a, b, (((1,), (1,)), ((), ())), preferred_element_type=F32)


def _mm_tn(a, b):
    return lax.dot_general(a, b, (((0,), (0,)), ((), ())), preferred_element_type=F32)


def _split3(x):
    x1 = x.astype(BF16)
    r1 = x - x1.astype(F32)
    x2 = r1.astype(BF16)
    x3 = (r1 - x2.astype(F32)).astype(BF16)
    return x1, x2, x3


def _mm_split_rhs(a_exact, b):
    b1 = b.astype(BF16)
    b2 = (b - b1.astype(F32)).astype(BF16)
    return _mm(a_exact, b1) + _mm(a_exact, b2)


def _iota(shape, dim):
    return lax.broadcasted_iota(jnp.int32, shape, dim)


def _interleave(gens):
    results = [None] * len(gens)
    live = list(enumerate(gens))
    while live:
        still = []
        for i, g in live:
            try:
                next(g)
                still.append((i, g))
            except StopIteration as stop:
                results[i] = stop.value
        live = still
    return results


def _div_pow2(x, n):
    assert n & (n - 1) == 0
    return x >> (n.bit_length() - 1)


def _mod_pow2(x, n):
    assert n & (n - 1) == 0
    return x & (n - 1)


def _inproj_kernel(x_ref, g_ref, w_ref, o_ref, hn_ref):
    @pl.when(pl.program_id(1) == 0)
    def _():
        x = x_ref[...]
        ms = jnp.mean(x * x, axis=-1, keepdims=True)
        hn_ref[...] = (x * lax.rsqrt(ms + NORM_EPS) * g_ref[...]).astype(BF16)

    o_ref[...] = _mm(hn_ref[...], w_ref[...])


def _inproj(x2, g, w, tm, tn):
    m, d = x2.shape
    n = w.shape[1]
    return pl.pallas_call(
        _inproj_kernel,
        grid=(m // tm, n // tn),
        in_specs=[
            pl.BlockSpec((tm, d), lambda i, j: (i, 0)),
            pl.BlockSpec((1, d), lambda i, j: (0, 0)),
            pl.BlockSpec((d, tn), lambda i, j: (0, j)),
        ],
        out_specs=pl.BlockSpec((tm, tn), lambda i, j: (i, j)),
        out_shape=jax.ShapeDtypeStruct((m, n), F32),
        scratch_shapes=[pltpu.VMEM((tm, d), BF16)],
        compiler_params=pltpu.CompilerParams(
            dimension_semantics=("parallel", "arbitrary"),
            vmem_limit_bytes=VMEM_LIMIT_BYTES),
        name="inproj",
    )(x2, g, w)


def _bucket(n):
    n = jnp.maximum(n, 0)
    large = jnp.full(n.shape, NUM_BUCKETS // 2, jnp.int32)
    for thr in _THR:
        large = large + (n >= thr).astype(jnp.int32)
    return jnp.where(n < NUM_BUCKETS // 2, n, large)


def _lookup_all_heads(dist, tab_ref):
    bucket = _bucket(dist)
    hits = [bucket == b for b in range(NUM_BUCKETS)]
    outs = []
    for h in range(NSA_HEADS):
        out = jnp.zeros(dist.shape, F32)
        for b in range(NUM_BUCKETS):
            out = jnp.where(hits[b], tab_ref[b * NSA_HEADS + h] * LOG2E, out)
        outs.append(out)
    return outs


def _bias_kernel(tab_ref, bc_ref, tp_ref):
    i = pl.program_id(0)
    rows, nr = bc_ref.shape[1], bc_ref.shape[2]
    dist_c = (i * rows + _iota((rows, nr), 0)) - (_iota((rows, nr), 1) * CMP_STRIDE + (CMP_BLOCK - 1))
    for h, vals in enumerate(_lookup_all_heads(dist_c, tab_ref)):
        bc_ref[h] = vals

    @pl.when(i == 0)
    def _():
        base = _iota((TB, TB), 0) - _iota((TB, TB), 1)
        neg = jnp.full((TB, TB), NEG, F32)
        for h in range(NSA_HEADS):
            tp_ref[h, T_NONE] = neg
        for d in range(ND):
            for h, vals in enumerate(_lookup_all_heads(base + d * TB, tab_ref)):
                tp_ref[h, d] = vals
                if d == 0:
                    tp_ref[h, T_DIAG] = jnp.where(base >= 0, vals, neg)
                if d == WINDOW // TB:
                    tp_ref[h, T_WEND] = jnp.where(base < 0, vals, neg)


def _bias(table_flat, s):
    nr = s // CMP_STRIDE
    rows = min(BIAS_ROWS, s)
    return pl.pallas_call(
        _bias_kernel,
        grid=(s // rows,),
        in_specs=[pl.BlockSpec(memory_space=pltpu.SMEM)],
        out_specs=[
            pl.BlockSpec((NSA_HEADS, rows, nr), lambda i: (0, i, 0)),
            pl.BlockSpec((NSA_HEADS, NT_ALL, TB, TB), lambda i: (0, 0, 0, 0)),
        ],
        out_shape=[
            jax.ShapeDtypeStruct((NSA_HEADS, s, nr), F32),
            jax.ShapeDtypeStruct((NSA_HEADS, NT_ALL, TB, TB), F32),
        ],
        compiler_params=pltpu.CompilerParams(dimension_semantics=("arbitrary",)),
        name="bias",
    )(table_flat)


def _compress(kv_ref, pos_ref, w1_ref, w2_ref, nr):
    half = CMP_STRIDE * HEAD_DIM
    r = jnp.concatenate(
        [kv_ref[0, pl.ds(m, nr, stride=CMP_STRIDE), :] for m in range(CMP_STRIDE)], axis=1)
    a = _mm((r + pos_ref[0:1, :]).astype(BF16), w1_ref[0:half, :])
    b = _mm((r + pos_ref[1:2, :]).astype(BF16), w1_ref[half:2 * half, :])
    pre = a + pltpu.roll(b, nr - 1, 0)
    h1 = pre * jax.nn.sigmoid(pre)
    return _mm(h1.astype(BF16), w2_ref[...])


def _bias_head(toep_ref, h, d_tiles, window, valid=True):
    sub = TQ // TB
    rows = []
    for ri in range(sub):
        cols = []
        for ci in range(sub):
            d = d_tiles + ri - ci
            idx = jnp.where(d == 0, T_DIAG, jnp.minimum(d, ND - 1))
            if window:
                idx = jnp.where(d == WINDOW // TB, T_WEND, jnp.where(d > WINDOW // TB, T_NONE, idx))
            idx = jnp.where((d < 0) | jnp.logical_not(valid), T_NONE, idx)
            cols.append(toep_ref[h, idx])
        rows.append(jnp.concatenate(cols, axis=1))
    return jnp.concatenate(rows, axis=0)


def _bias_tile(toep_ref, d_tiles, window, valid=True):
    return jnp.concatenate([_bias_head(toep_ref, h, d_tiles, window, valid) for h in range(NSA_GROUP)], axis=0)


def _with_ones(v):
    return jnp.concatenate([v, jnp.ones(v.shape, v.dtype)], axis=1)


def _fold_lanes(x, op):
    out = x[:, :LANE]
    for c in range(1, x.shape[1] // LANE):
        out = op(out, x[:, c * LANE:(c + 1) * LANE])
    return out


def _nsa_kernel(q_ref, kc_ref, vc_ref, ks_ref, vs_ref, kw_ref, vw_ref, gate_ref, za_ref,
                bc_ref, toep_ref, posk_ref, w1k_ref, w2k_ref, posv_ref, w1v_ref, w2v_ref,
                o_ref, kcs_ref, vcs_ref, s_ref, macc_ref, acc_ref):
    qi = pl.program_id(2)
    s_len = kc_ref.shape[1]
    nr = s_len // CMP_STRIDE
    nb = s_len // SLC_BLOCK
    n_sel = min(SLC_TOP_N, nb)

    @pl.when(qi == 0)
    def _():
        kcs_ref[...] = _compress(kc_ref, posk_ref, w1k_ref, w2k_ref, nr).astype(BF16)
        vcs_ref[...] = _compress(vc_ref, posv_ref, w1v_ref, w2v_ref, nr).astype(BF16)

    q0 = qi * TQ
    rows = NSA_GROUP * TQ
    q = q_ref[0] * (HEAD_DIM ** -0.5 * LOG2E)
    q4 = jnp.concatenate([q[:, h * HEAD_DIM:(h + 1) * HEAD_DIM] for h in range(NSA_GROUP)],
                         axis=0).astype(BF16)

    sub = TQ // TB
    n_tiles = s_len // TQ

    def compressed_and_selection():
        t_c = q0 + _iota((TQ, nr), 0)
        i_c = _iota((TQ, nr), 1)
        mask_c = (t_c - (i_c * CMP_STRIDE + (CMP_BLOCK - 1)) >= 0) & (i_c < nr - 1)
        qk = _mm_nt(q4, kcs_ref[...])
        mask_add = jnp.where(mask_c, 0.0, NEG)
        yield
        p_heads = []
        for h in range(NSA_GROUP):
            lg = qk[h * TQ:(h + 1) * TQ] + bc_ref[h] + mask_add
            m_c = jnp.max(lg, axis=-1, keepdims=True)
            yield
            e = jnp.where(lg > 0.5 * NEG, jnp.exp2(lg - m_c), 0.0)
            l_c = jnp.sum(e, axis=-1, keepdims=True)
            yield
            p_heads.append(e / jnp.maximum(l_c, 1e-30))
        p = jnp.concatenate(p_heads, axis=0)
        o_c = _mm(p.astype(BF16), vcs_ref[...])
        psum = p_heads[0]
        for h in range(1, NSA_GROUP):
            psum = psum + p_heads[h]
        ov_i = _iota((nb, nr), 1) * CMP_STRIDE
        ov_j = _iota((nb, nr), 0) * SLC_BLOCK
        ov_t = ((ov_i < ov_j + SLC_BLOCK) & (ov_i + CMP_BLOCK > ov_j)).astype(BF16)
        p1, p2, p3 = _split3(psum)
        imp_t = _mm_nt(ov_t, p1) + _mm_nt(ov_t, p2) + _mm_nt(ov_t, p3)
        yield
        jb = _iota((nb, TQ), 0)
        cur = _div_pow2(q0 + _iota((nb, TQ), 1), SLC_BLOCK)
        forced = (jb == 0) | (jb == cur) | (jb == cur - 1)
        causal = jb <= cur
        score = jnp.where(forced, jnp.inf, jnp.where(causal, imp_t, -jnp.inf))
        rank = jnp.zeros((nb, TQ), jnp.int32)
        for jp in range(nb):
            sj = score[jp:jp + 1, :]
            beats = (sj > score) | ((sj == score) & (jb > jp))
            rank = rank + beats.astype(jnp.int32)
            if jp % 8 == 7:
                yield
        sel_t = ((rank < n_sel) & causal).astype(BF16)
        place = (_iota((nb, LANE), 0) == _iota((nb, LANE), 1)).astype(BF16)
        unsel = _mm_tn(sel_t, place) - (_iota((TQ, LANE), 1) < nb).astype(F32)
        yield
        return o_c, unsel.astype(BF16)

    def window():
        n_band = WINDOW // TQ + 1
        qk, tiles, v_w = [], [], []
        for c in range(n_band):
            j = qi - (n_band - 1) + c
            jc = jnp.maximum(j, 0)
            k0 = pl.multiple_of(jc * TQ, TQ)
            tiles.append((j, jc))
            qk.append(_mm_nt(q4, kw_ref[0, pl.ds(k0, TQ), :].astype(BF16)))
            v_w.append(_with_ones(vw_ref[0, pl.ds(k0, TQ), :].astype(BF16)))
        yield
        heads = range(NSA_GROUP)
        s_w = [[None] * NSA_GROUP for _ in range(n_band)]
        m_w = [None] * NSA_GROUP
        for c in range(n_band):
            for h in heads:
                s = qk[c][h * TQ:(h + 1) * TQ] + _bias_head(toep_ref, h, (qi - tiles[c][1]) * sub, True,
                                                             tiles[c][0] >= 0)
                s_w[c][h] = s
                m_w[h] = s if c == 0 else jnp.maximum(m_w[h], s)
                yield
        for h in heads:
            m_w[h] = jnp.max(m_w[h], axis=-1, keepdims=True)
        yield
        v_band = jnp.concatenate(v_w, axis=0)
        acc_w = []
        for h in heads:
            p_h = []
            for c in range(n_band):
                p_h.append(jnp.exp2(s_w[c][h] - m_w[h]).astype(BF16))
                yield
            acc_w.append(_mm(jnp.concatenate(p_h, axis=1), v_band))
            yield
        return jnp.concatenate([a[:, :HEAD_DIM] / jnp.maximum(a[:, HEAD_DIM:], 1e-30) for a in acc_w], axis=0)

    def gates():
        gts = jax.nn.sigmoid(gate_ref[0])
        grp = pl.program_id(1)
        za = za_ref[0]
        out = [[None] * NSA_GROUP for _ in range(3)]
        for h in range(NSA_GROUP):
            z = za[:, h * HEAD_DIM:(h + 1) * HEAD_DIM]
            zs = z * jax.nn.sigmoid(z)
            for branch in range(3):
                lane = branch * NSA_HEADS + h
                col = gts[:, lane:lane + 1]
                for g in range(1, NSA_KV_HEADS):
                    lg = lane + g * NSA_GROUP
                    col = jnp.where(grp == g, gts[:, lg:lg + 1], col)
                out[branch][h] = zs * col
                yield
        return out

    (o_c, unsel), o_w, gz = _interleave([compressed_and_selection(), window(), gates()])
    o_cw = [gz[0][h] * o_c[h * TQ:(h + 1) * TQ] + gz[2][h] * o_w[h * TQ:(h + 1) * TQ] for h in range(NSA_GROUP)]

    q_sel = jnp.concatenate([q4, jnp.concatenate([unsel] * NSA_GROUP, axis=0)], axis=1)

    def key_tile(j):
        k0 = pl.multiple_of(j * TQ, TQ)
        blk = _div_pow2(j * TQ + _iota((TQ, LANE), 0), SLC_BLOCK)
        marks = jnp.where(_iota((TQ, LANE), 1) == blk, BIG, 0.0).astype(BF16)
        return jnp.concatenate([ks_ref[0, pl.ds(k0, TQ), :].astype(BF16), marks], axis=1)

    assert n_tiles % 4 == 0
    n_pairs = _div_pow2(qi + 2, 2)
    passes, first = [], 0
    for tiles_per_iter in (8, 4, 2):
        if tiles_per_iter <= n_tiles:
            trips = _div_pow2(2 * n_pairs - first, tiles_per_iter)
            passes.append((tiles_per_iter, first, trips))
            first = first + tiles_per_iter * trips

    def slc_logits(tpi, base):
        def body(it, carry):
            tiles = [base + tpi * it + c for c in range(tpi)]
            qk = [_mm_nt(q_sel, key_tile(j)) for j in tiles]
            macc = macc_ref[...]
            for j, qk_j in zip(tiles, qk):
                s = qk_j + _bias_tile(toep_ref, (qi - j) * sub, False)
                s_ref[j] = s
                macc = jnp.maximum(macc, _fold_lanes(s, jnp.maximum))
            macc_ref[...] = macc
            return carry
        return body

    macc_ref[...] = jnp.full((rows, LANE), NEG, F32)
    for tpi, base, trips in passes:
        lax.fori_loop(0, trips, slc_logits(tpi, base), 0)
    m_rep = jnp.broadcast_to(jnp.max(macc_ref[...], axis=-1, keepdims=True), (rows, LANE))
    m_s = jnp.concatenate([m_rep] * (TQ // LANE), axis=1)

    def slc_values(tpi, base):
        def body(it, carry):
            j0 = base + tpi * it
            k0 = pl.multiple_of(j0 * TQ, 2 * TQ)
            p_it = jnp.concatenate([jnp.exp2(s_ref[j0 + c] - m_s).astype(BF16) for c in range(tpi)], axis=1)
            acc_ref[...] += _mm(p_it, _with_ones(vs_ref[0, pl.ds(k0, tpi * TQ), :].astype(BF16)))
            return carry
        return body

    acc_ref[...] = jnp.zeros((rows, 2 * HEAD_DIM), F32)
    for tpi, base, trips in passes:
        lax.fori_loop(0, trips, slc_values(tpi, base), 0)
    acc = acc_ref[...]
    o_s = acc[:, :HEAD_DIM] / jnp.maximum(acc[:, HEAD_DIM:], 1e-30)

    for h in range(NSA_GROUP):
        o = o_cw[h] + gz[1][h] * o_s[h * TQ:(h + 1) * TQ]
        o_ref[0, :, h * HEAD_DIM:(h + 1) * HEAD_DIM] = o.astype(o_ref.dtype)


def _nsa(proj3, bias_c, toep, posk, w1k, w2k, posv, w1v, w2v):
    b, s, _ = proj3.shape
    nr = s // CMP_STRIDE
    gq = NSA_GROUP * HEAD_DIM

    def kvspec(idx):
        return pl.BlockSpec((1, s, HEAD_DIM), lambda bi, g, qi, idx=idx: (bi, 0, P_KV // HEAD_DIM + 2 * idx + g))

    def whole(a):
        return pl.BlockSpec(a.shape, lambda bi, g, qi, nd=a.ndim: (0,) * nd)

    in_specs = [
        pl.BlockSpec((1, TQ, gq), lambda bi, g, qi: (bi, qi, P_Q // gq + g)),
        kvspec(0), kvspec(1), kvspec(2), kvspec(3), kvspec(4), kvspec(5),
        pl.BlockSpec((1, TQ, LANE), lambda bi, g, qi: (bi, qi, P_GATE // LANE)),
        pl.BlockSpec((1, TQ, gq), lambda bi, g, qi: (bi, qi, P_ZA // gq + g)),
        pl.BlockSpec((NSA_GROUP, TQ, nr), lambda bi, g, qi: (g, qi, 0)),
        pl.BlockSpec((NSA_GROUP, NT_ALL, TB, TB), lambda bi, g, qi: (g, 0, 0, 0)),
        whole(posk), whole(w1k), whole(w2k), whole(posv), whole(w1v), whole(w2v),
    ]
    return pl.pallas_call(
        _nsa_kernel,
        grid=(b, NSA_KV_HEADS, s // TQ),
        in_specs=in_specs,
        out_specs=pl.BlockSpec((1, TQ, gq), lambda bi, g, qi: (bi, qi, g)),
        out_shape=jax.ShapeDtypeStruct((b, s, NSA_WIDTH), BF16),
        scratch_shapes=[pltpu.VMEM((nr, HEAD_DIM), BF16), pltpu.VMEM((nr, HEAD_DIM), BF16),
                        pltpu.VMEM((s // TQ, NSA_GROUP * TQ, TQ), F32),
                        pltpu.VMEM((NSA_GROUP * TQ, LANE), F32),
                        pltpu.VMEM((NSA_GROUP * TQ, 2 * HEAD_DIM), F32)],
        compiler_params=pltpu.CompilerParams(
            dimension_semantics=("parallel", "parallel", "arbitrary"),
            vmem_limit_bytes=VMEM_LIMIT_BYTES),
        name="nsa",
    )(proj3, proj3, proj3, proj3, proj3, proj3, proj3, proj3, proj3,
      bias_c, toep, posk, w1k, w2k, posv, w1v, w2v)


def _shift_mix(ref, prev_ref, mu, bb, sl):
    x = ref[bb, :, sl]
    prev = jnp.where(_iota(x.shape, 0) == 0, prev_ref[bb:bb + 1, sl], pltpu.roll(x, 1, 0))
    prev_ref[bb:bb + 1, sl] = x[x.shape[0] - 1:]
    return x + mu * (prev - x)


def _rwkv_kernel(r_ref, k_ref, v_ref, wa_ref, zb_ref, vec_ref, muwa_ref, w2_ref, a2_ref,
                 o_ref, st_ref, pr_ref, pk_ref, pv_ref, pwa_ref):
    first = pl.program_id(2) == 0
    n_batch, tb = r_ref.shape[0], r_ref.shape[1]
    n_groups = r_ref.shape[2] // GW
    n_chunks = tb // CH
    seqs = [(bb, gi) for bb in range(n_batch) for gi in range(n_groups)]

    @pl.when(first)
    def _():
        for ref in (st_ref, pr_ref, pk_ref, pv_ref, pwa_ref):
            ref[...] = jnp.zeros_like(ref)

    wd_act, ad = [], []
    for bb in range(n_batch):
        wa = _shift_mix(wa_ref, pwa_ref, muwa_ref[...], bb, slice(0, LANE))
        wd_act.append(jnp.tanh(wa[:, :LORA]).astype(BF16))
        ad.append(wa[:, LORA:].astype(BF16))

    seg = (_div_pow2(_iota((GW, GW), 0), RWKV_HEAD_DIM) == _div_pow2(_iota((GW, GW), 1), RWKV_HEAD_DIM))
    segf = seg.astype(F32)
    segb = seg.astype(BF16)
    assert CH == RWKV_HEAD_DIM
    lane_s = _mod_pow2(_iota((CH, GW), 1), CH)
    row_t = _iota((CH, GW), 0)
    strict = lane_s < row_t
    incl = lane_s <= row_t
    eye = (lane_s == row_t).astype(F32)
    ti_r, ti_c = _iota((tb, tb), 0), _iota((tb, tb), 1)
    trib = ((ti_c <= ti_r) & (_div_pow2(ti_c, CH) == _div_pow2(ti_r, CH))).astype(BF16)

    def bd(x):
        xb = x.astype(BF16)
        return jnp.concatenate([xb] * HPG, axis=0) * segb

    def prep(bb, gi):
        lanes = slice(gi * GW, (gi + 1) * GW)
        vec = vec_ref[:, lanes]
        mu_r, mu_k, mu_v = vec[0:1], vec[1:2], vec[2:3]
        w0, a0, k_k, k_a = vec[3:4], vec[4:5], vec[5:6], vec[6:7]
        r = _shift_mix(r_ref, pr_ref, mu_r, bb, lanes)
        k = _shift_mix(k_ref, pk_ref, mu_k, bb, lanes)
        v = _shift_mix(v_ref, pv_ref, mu_v, bb, lanes)
        w_lora = _mm(wd_act[bb], w2_ref[:, lanes])
        a_lora = _mm(ad[bb], a2_ref[:, lanes])
        kk = k * k_k
        kk_ss = _mm((kk * kk).astype(BF16), segb)
        yield
        lw = jax.nn.sigmoid(w0 + w_lora) * (-math.exp(-0.5) * LOG2E)
        cum = _mm_split_rhs(trib, lw)
        yield
        a_sig = jax.nn.sigmoid(a0 + a_lora)
        kk = kk * lax.rsqrt(jnp.maximum(kk_ss, 1e-24))
        k = k * (1.0 + (a_sig - 1.0) * k_a)
        return dict(r=r, k=k, v=v, a=-kk, b=kk * a_sig, lw=lw, cum=cum, vec=vec, lanes=lanes, bb=bb)

    groups = _interleave([prep(bb, gi) for bb, gi in seqs])

    def chunk_local(g, c):
        ts = slice(c * CH, (c + 1) * CH)
        rc, kc, vc, ac, bc, lwc, cum = (g[n][ts] for n in ("r", "k", "v", "a", "b", "lw", "cum"))
        tot = cum[CH - 1:CH]
        e_out = jnp.exp2(-cum)
        e_end = jnp.exp2(tot - cum)
        r_t = rc * jnp.exp2(cum)
        a_t = ac * jnp.exp2(cum - lwc)
        lhs = jnp.concatenate([a_t, r_t], axis=0).astype(BF16)
        aa = _mm_nt(lhs, jnp.concatenate([bd(bc * e_out), bd(kc * e_out)], axis=0))
        yield
        a_ab = jnp.where(strict, aa[:CH, :GW], 0.0)
        a_ak = jnp.where(strict, aa[:CH, GW:], 0.0)
        a_rb = jnp.where(incl, aa[CH:, :GW], 0.0)
        a_rk = jnp.where(incl, aa[CH:, GW:], 0.0)
        t_inv = eye + a_ab
        mpow = _mm(a_ab.astype(BF16), bd(a_ab))
        av = _mm(a_ak.astype(BF16), bd(vc))
        yield
        for _ in range(int(math.log2(CH)) - 1):
            res = _mm(jnp.concatenate([t_inv, mpow], axis=0).astype(BF16), bd(mpow))
            yield
            t_inv = t_inv + res[:CH]
            mpow = res[CH:]
        wu = _mm(t_inv.astype(BF16), jnp.concatenate([bd(a_t), bd(av)], axis=1))
        yield
        return dict(
            lhs=jnp.concatenate([wu[:, :GW], r_t], axis=0).astype(BF16), u_loc=wu[:, GW:],
            a_r=jnp.concatenate([a_rb, a_rk], axis=1).astype(BF16), bdv=bd(vc), vc=vc,
            bk_end=jnp.concatenate([bc * e_end, kc * e_end], axis=0).astype(BF16), dec=jnp.exp2(tot))

    loc = _interleave([chunk_local(g, c) for g in groups for c in range(n_chunks)])

    def chain(q):
        ys = []
        g_state = st_ref[q]
        for c in range(n_chunks):
            lc = loc[q * n_chunks + c]
            x0 = _mm_nt(lc["lhs"], g_state.astype(BF16))
            yield
            u = x0[:CH] + lc["u_loc"]
            y_c = _mm(lc["a_r"], jnp.concatenate([bd(u), lc["bdv"]], axis=0))
            upd = _mm_tn(jnp.concatenate([u, lc["vc"]], axis=0).astype(BF16), lc["bk_end"])
            yield
            ys.append(x0[CH:] + y_c)
            g_state = g_state * lc["dec"] + upd * segf
        st_ref[q] = g_state
        return jnp.concatenate(ys, axis=0)

    ys = _interleave([chain(q) for q in range(len(seqs))])

    def finish(g, y):
        vec = g["vec"]
        ln_w, ln_b, r_k = vec[7:8], vec[8:9], vec[9:10]
        inv_n = 1.0 / RWKV_HEAD_DIM
        mean = _mm(y.astype(BF16), segb) * inv_n
        bonus = _mm((g["r"] * g["k"] * r_k).astype(BF16), segb) * g["v"]
        yield
        yc = y - mean
        var = _mm((yc * yc).astype(BF16), segb) * inv_n
        yield
        yn = yc * lax.rsqrt(var + RWKV_GN_EPS) * ln_w + ln_b
        zb = zb_ref[g["bb"], :, g["lanes"]]
        o_ref[g["bb"], :, g["lanes"]] = ((yn + bonus) * (zb * jax.nn.sigmoid(zb))).astype(o_ref.dtype)

    _interleave([finish(g, y) for g, y in zip(groups, ys)])


def _rwkv(proj3, vecs, mu_wa, w2, a2, tb, gps, bps):
    b, s, _ = proj3.shape
    assert b % bps == 0, (b, bps)
    gw = gps * GW
    ng = RWKV_WIDTH // gw

    def col(off):
        return pl.BlockSpec((bps, tb, gw), lambda bi, g, ti, off=off: (bi, ti, off // gw + g))

    in_specs = [
        col(P_RKV), col(P_RKV + RWKV_WIDTH), col(P_RKV + 2 * RWKV_WIDTH),
        pl.BlockSpec((bps, tb, LANE), lambda bi, g, ti: (bi, ti, P_WDAD // LANE)),
        col(P_ZB),
        pl.BlockSpec((vecs.shape[0], gw), lambda bi, g, ti: (0, g)),
        pl.BlockSpec((1, LANE), lambda bi, g, ti: (0, 0)),
        pl.BlockSpec((LORA, gw), lambda bi, g, ti: (0, g)),
        pl.BlockSpec((LORA, gw), lambda bi, g, ti: (0, g)),
    ]
    return pl.pallas_call(
        _rwkv_kernel,
        grid=(b // bps, ng, s // tb),
        in_specs=in_specs,
        out_specs=pl.BlockSpec((bps, tb, gw), lambda bi, g, ti: (bi, ti, g)),
        out_shape=jax.ShapeDtypeStruct((b, s, RWKV_WIDTH), BF16),
        scratch_shapes=[pltpu.VMEM((bps * gps, GW, GW), F32), pltpu.VMEM((bps, gw), F32),
                        pltpu.VMEM((bps, gw), F32), pltpu.VMEM((bps, gw), F32), pltpu.VMEM((bps, LANE), F32)],
        compiler_params=pltpu.CompilerParams(
            dimension_semantics=("parallel", "parallel", "arbitrary")),
        name="rwkv",
    )(proj3, proj3, proj3, proj3, proj3, vecs, mu_wa, w2, a2)


def _outproj_kernel(ma_ref, mb_ref, wa_ref, wb_ref, x_ref, g_ref, o_ref):
    y = _mm(ma_ref[...], wa_ref[...]) + _mm(mb_ref[...], wb_ref[...])
    ms = jnp.mean(y * y, axis=-1, keepdims=True)
    o_ref[...] = x_ref[...] + y * lax.rsqrt(ms + NORM_EPS) * g_ref[...]


def _outproj(mix_a, mix_b, w_a, w_b, x2, g, tm):
    m, d = x2.shape
    ka, kb = mix_a.shape[1], mix_b.shape[1]
    return pl.pallas_call(
        _outproj_kernel,
        grid=(m // tm,),
        in_specs=[
            pl.BlockSpec((tm, ka), lambda i: (i, 0)),
            pl.BlockSpec((tm, kb), lambda i: (i, 0)),
            pl.BlockSpec((ka, d), lambda i: (0, 0)),
            pl.BlockSpec((kb, d), lambda i: (0, 0)),
            pl.BlockSpec((tm, d), lambda i: (i, 0)),
            pl.BlockSpec((1, d), lambda i: (0, 0)),
        ],
        out_specs=pl.BlockSpec((tm, d), lambda i: (i, 0)),
        out_shape=jax.ShapeDtypeStruct((m, d), F32),
        compiler_params=pltpu.CompilerParams(
            dimension_semantics=("parallel",), vmem_limit_bytes=VMEM_LIMIT_BYTES),
        name="outproj",
    )(mix_a, mix_b, w_a, w_b, x2, g)


_W_SEGMENTS = (
    (P_Q, R_Q, NSA_WIDTH),
    (P_RKV, R_FEAT, 3 * RWKV_WIDTH),
    (P_ZB, R_ZB, RWKV_WIDTH),
    (P_ZA, R_ZA, NSA_WIDTH),
    (P_KV, R_KV, 6 * NSA_KV_HEADS * HEAD_DIM),
    (P_WDAD, R_FEAT + 3 * RWKV_WIDTH, 2 * LORA),
    (P_GATE, R_GATE, LANE),
)


def _relayout_kernel(w_ref, o_ref):
    for dst, src, width in _W_SEGMENTS:
        o_ref[:, dst:dst + width] = w_ref[:, src:src + width].astype(BF16)
    used = P_GATE + LANE
    o_ref[:, used:] = jnp.zeros((o_ref.shape[0], NP - used), BF16)


def _permute_w_in(w, rows):
    d, n = w.shape
    return pl.pallas_call(
        _relayout_kernel,
        grid=(d // rows,),
        in_specs=[pl.BlockSpec((rows, n), lambda i: (i, 0))],
        out_specs=pl.BlockSpec((rows, NP), lambda i: (i, 0)),
        out_shape=jax.ShapeDtypeStruct((d, NP), BF16),
        compiler_params=pltpu.CompilerParams(dimension_semantics=("parallel",)),
        name="relayout",
    )(w)


def _block(x, pre_norm_g, w_in, rel_bias_table, cmp_pos_k, cmp_pos_v, cmp_k_w1, cmp_k_w2, cmp_v_w1,
           cmp_v_w2, rwkv_mu, rwkv_w0, rwkv_w2, rwkv_a0, rwkv_a2, rwkv_k_k, rwkv_k_a, rwkv_r_k,
           rwkv_ln_w, rwkv_ln_b, w_out, post_norm_g):
    b, s, d = x.shape
    assert w_in.shape == (d, R_END) and w_out.shape == (NSA_WIDTH + RWKV_WIDTH, d), (w_in.shape, w_out.shape)
    assert s % (4 * TQ) == 0 and s >= WINDOW + TQ, s
    x2 = x.reshape(b * s, d)
    proj = _inproj(x2, pre_norm_g.reshape(1, d), _permute_w_in(w_in.astype(BF16), RELAYOUT_ROWS),
                   min(INPROJ_TM, b * s), INPROJ_TN)
    proj3 = proj.reshape(b, s, NP)

    bias_c, toep = _bias(rel_bias_table.reshape(-1), s)
    half = CMP_STRIDE * HEAD_DIM
    mix_a = _nsa(proj3, bias_c, toep,
                 cmp_pos_k.reshape(2, half), cmp_k_w1.astype(BF16), cmp_k_w2.astype(BF16),
                 cmp_pos_v.reshape(2, half), cmp_v_w1.astype(BF16), cmp_v_w2.astype(BF16))

    w3 = 3 * RWKV_WIDTH
    vec_rows = [rwkv_mu[:RWKV_WIDTH], rwkv_mu[RWKV_WIDTH:2 * RWKV_WIDTH], rwkv_mu[2 * RWKV_WIDTH:w3],
                rwkv_w0, rwkv_a0, rwkv_k_k, rwkv_k_a, rwkv_ln_w, rwkv_ln_b, rwkv_r_k.reshape(-1)]
    vecs = jnp.stack(vec_rows + [jnp.zeros_like(rwkv_w0)] * (16 - len(vec_rows)), axis=0)
    mix_b = _rwkv(proj3, vecs, rwkv_mu[w3:].reshape(1, 2 * LORA), rwkv_w2.astype(BF16),
                  rwkv_a2.astype(BF16), min(RWKV_SLAB, s), RWKV_GROUPS_PER_STEP, math.gcd(RWKV_BATCH_PER_STEP, b))

    w_o = w_out.astype(BF16)
    out = _outproj(mix_a.reshape(b * s, NSA_WIDTH), mix_b.reshape(b * s, RWKV_WIDTH),
                   w_o[:NSA_WIDTH], w_o[NSA_WIDTH:], x2, post_norm_g.reshape(1, d), min(OUTPROJ_TM, b * s))
    return out.reshape(b, s, d)


def kernel(x, pre_norm_g, w_in, rel_bias_table, cmp_pos_k, cmp_pos_v, cmp_k_w1, cmp_k_w2, cmp_v_w1,
           cmp_v_w2, rwkv_mu, rwkv_w0, rwkv_w2, rwkv_a0, rwkv_a2, rwkv_k_k, rwkv_k_a, rwkv_r_k,
           rwkv_ln_w, rwkv_ln_b, w_out, post_norm_g):
    h = x
    for l in range(pre_norm_g.shape[0]):
        h = _block(h, pre_norm_g[l], w_in[l], rel_bias_table, cmp_pos_k[l], cmp_pos_v[l], cmp_k_w1[l],
                   cmp_k_w2[l], cmp_v_w1[l], cmp_v_w2[l], rwkv_mu[l], rwkv_w0[l], rwkv_w2[l],
                   rwkv_a0[l], rwkv_a2[l], rwkv_k_k[l], rwkv_k_a[l], rwkv_r_k[l], rwkv_ln_w[l],
                   rwkv_ln_b[l], w_out[l], post_norm_g[l])
    return h
```

```python
import math

import jax
import jax.numpy as jnp
from jax import lax
from jax.experimental import pallas as pl
from jax.experimental.pallas import tpu as pltpu

F32 = jnp.float32
BF16 = jnp.bfloat16

NSA_HEADS = 8
NSA_KV_HEADS = 2
NSA_GROUP = NSA_HEADS // NSA_KV_HEADS
HEAD_DIM = 128
NSA_WIDTH = NSA_HEADS * HEAD_DIM
CMP_BLOCK = 32
CMP_STRIDE = 16
SLC_BLOCK = 64
SLC_TOP_N = 16
WINDOW = 512
RWKV_WIDTH = 1024
RWKV_HEAD_DIM = 64
LORA = 64
NUM_BUCKETS = 32
MAX_DISTANCE = 1024
NORM_EPS = 1e-6
RWKV_GN_EPS = 64e-5

R_Q = 0
R_KV = R_Q + NSA_WIDTH
R_GATE = R_KV + 6 * NSA_KV_HEADS * HEAD_DIM
R_ZA = R_GATE + 3 * NSA_HEADS
R_FEAT = R_ZA + NSA_WIDTH
R_ZB = R_FEAT + 3 * RWKV_WIDTH + 2 * LORA
R_END = R_ZB + RWKV_WIDTH

P_Q = 0
P_RKV = 1024
P_ZB = 4096
P_ZA = 5120
P_KV = 6144
P_WDAD = 7680
P_GATE = 7808
NP = 8192

LANE = 128
TQ = 256
TB = 128
ND = 9
T_DIAG = ND
T_WEND = ND + 1
T_NONE = ND + 2
NT_ALL = ND + 3
LOG2E = math.log2(math.e)
BIG = 2.0 ** 100
CH = 64
HPG = 4
GW = HPG * RWKV_HEAD_DIM
RWKV_GROUPS_PER_STEP = 4
RWKV_BATCH_PER_STEP = 4
NEG = -1e30


VMEM_LIMIT_BYTES = 56 * 1024 * 1024
INPROJ_TM, INPROJ_TN = 1024, 2048
OUTPROJ_TM = 512
RELAYOUT_ROWS = 256
BIAS_ROWS = 512
RWKV_SLAB = 128


def _bucket_thresholds():
    exact = NUM_BUCKETS // 2
    ratio = MAX_DISTANCE // exact
    out = []
    for k in range(1, NUM_BUCKETS - exact):
        n = exact
        while n ** exact < (exact ** exact) * (ratio ** k):
            n += 1
        out.append(n)
    return out


_THR = _bucket_thresholds()


def _mm(a, b):
    return jnp.dot(a, b, preferred_element_type=F32)


def _mm_nt(a, b):
    return lax.dot_general(a, b, (((1,), (1,)), ((), ())), preferred_element_type=F32)


def _mm_tn(a, b):
    return lax.dot_general(a, b, (((0,), (0,)), ((), ())), preferred_element_type=F32)


def _split3(x):
    x1 = x.astype(BF16)
    r1 = x - x1.astype(F32)
    x2 = r1.astype(BF16)
    x3 = (r1 - x2.astype(F32)).astype(BF16)
    return x1, x2, x3


def _mm_split_rhs(a_exact, b):
    b1 = b.astype(BF16)
    b2 = (b - b1.astype(F32)).astype(BF16)
    return _mm(a_exact, b1) + _mm(a_exact, b2)


def _iota(shape, dim):
    return lax.broadcasted_iota(jnp.int32, shape, dim)


def _interleave(gens):
    results = [None] * len(gens)
    live = list(enumerate(gens))
    while live:
        still = []
        for i, g in live:
            try:
                next(g)
                still.append((i, g))
            except StopIteration as stop:
                results[i] = stop.value
        live = still
    return results


def _div_pow2(x, n):
    assert n & (n - 1) == 0
    return x >> (n.bit_length() - 1)


def _mod_pow2(x, n):
    assert n & (n - 1) == 0
    return x & (n - 1)


def _inproj_kernel(x_ref, g_ref, w_ref, o_ref, hn_ref):
    @pl.when(pl.program_id(1) == 0)
    def _():
        x = x_ref[...]
        ms = jnp.mean(x * x, axis=-1, keepdims=True)
        hn_ref[...] = (x * lax.rsqrt(ms + NORM_EPS) * g_ref[...]).astype(BF16)

    o_ref[...] = _mm(hn_ref[...], w_ref[...])


def _inproj(x2, g, w, tm, tn):
    m, d = x2.shape
    n = w.shape[1]
    return pl.pallas_call(
        _inproj_kernel,
        grid=(m // tm, n // tn),
        in_specs=[
            pl.BlockSpec((tm, d), lambda i, j: (i, 0)),
            pl.BlockSpec((1, d), lambda i, j: (0, 0)),
            pl.BlockSpec((d, tn), lambda i, j: (0, j)),
        ],
        out_specs=pl.BlockSpec((tm, tn), lambda i, j: (i, j)),
        out_shape=jax.ShapeDtypeStruct((m, n), F32),
        scratch_shapes=[pltpu.VMEM((tm, d), BF16)],
        compiler_params=pltpu.CompilerParams(
            dimension_semantics=("parallel", "arbitrary"),
            vmem_limit_bytes=VMEM_LIMIT_BYTES),
        name="inproj",
    )(x2, g, w)


def _bucket(n):
    n = jnp.maximum(n, 0)
    large = jnp.full(n.shape, NUM_BUCKETS // 2, jnp.int32)
    for thr in _THR:
        large = large + (n >= thr).astype(jnp.int32)
    return jnp.where(n < NUM_BUCKETS // 2, n, large)


def _bucket_of(n):
    n = max(n, 0)
    return n if n < NUM_BUCKETS // 2 else NUM_BUCKETS // 2 + sum(n >= thr for thr in _THR)


def _lookup_all_heads(dist, tab_ref, d_min=None, d_max=None):
    lo = 0 if d_min is None else _bucket_of(d_min)
    hi = NUM_BUCKETS - 1 if d_max is None else _bucket_of(d_max)
    bucket = _bucket(dist)
    hits = {b: bucket == b for b in range(lo, hi + 1)}
    outs = []
    for h in range(NSA_HEADS):
        out = jnp.zeros(dist.shape, F32)
        for b in range(lo, hi + 1):
            out = jnp.where(hits[b], tab_ref[b * NSA_HEADS + h] * LOG2E, out)
        outs.append(out)
    return outs


def _bias_kernel(tab_ref, bc_ref, tp_ref):
    i = pl.program_id(0)
    rows, nr = bc_ref.shape[1], bc_ref.shape[2]
    dist_c = (i * rows + _iota((rows, nr), 0)) - (_iota((rows, nr), 1) * CMP_STRIDE + (CMP_BLOCK - 1))
    for h, vals in enumerate(_lookup_all_heads(dist_c, tab_ref)):
        bc_ref[h] = vals

    @pl.when(i == 0)
    def _():
        base = _iota((TB, TB), 0) - _iota((TB, TB), 1)
        neg = jnp.full((TB, TB), NEG, F32)
        for h in range(NSA_HEADS):
            tp_ref[h, T_NONE] = neg
        for d in range(ND):
            for h, vals in enumerate(_lookup_all_heads(base + d * TB, tab_ref, (d - 1) * TB + 1, (d + 1) * TB - 1)):
                tp_ref[h, d] = vals
                if d == 0:
                    tp_ref[h, T_DIAG] = jnp.where(base >= 0, vals, neg)
                if d == WINDOW // TB:
                    tp_ref[h, T_WEND] = jnp.where(base < 0, vals, neg)


def _bias(table_flat, s):
    nr = s // CMP_STRIDE
    rows = min(BIAS_ROWS, s)
    return pl.pallas_call(
        _bias_kernel,
        grid=(s // rows,),
        in_specs=[pl.BlockSpec(memory_space=pltpu.SMEM)],
        out_specs=[
            pl.BlockSpec((NSA_HEADS, rows, nr), lambda i: (0, i, 0)),
            pl.BlockSpec((NSA_HEADS, NT_ALL, TB, TB), lambda i: (0, 0, 0, 0)),
        ],
        out_shape=[
            jax.ShapeDtypeStruct((NSA_HEADS, s, nr), F32),
            jax.ShapeDtypeStruct((NSA_HEADS, NT_ALL, TB, TB), F32),
        ],
        compiler_params=pltpu.CompilerParams(dimension_semantics=("arbitrary",)),
        name="bias",
    )(table_flat)


def _compress(kv_ref, pos_ref, w1_ref, w2_ref, nr):
    half = CMP_STRIDE * HEAD_DIM
    r = jnp.concatenate(
        [kv_ref[0, pl.ds(m, nr, stride=CMP_STRIDE), :] for m in range(CMP_STRIDE)], axis=1)
    a = _mm((r + pos_ref[0:1, :]).astype(BF16), w1_ref[0:half, :])
    b = _mm((r + pos_ref[1:2, :]).astype(BF16), w1_ref[half:2 * half, :])
    pre = a + pltpu.roll(b, nr - 1, 0)
    h1 = pre * jax.nn.sigmoid(pre)
    return _mm(h1.astype(BF16), w2_ref[...])


def _bias_head(toep_ref, h, d_tiles, window, valid=True):
    sub = TQ // TB
    rows = []
    for ri in range(sub):
        cols = []
        for ci in range(sub):
            d = d_tiles + ri - ci
            idx = jnp.where(d == 0, T_DIAG, jnp.minimum(d, ND - 1))
            if window:
                idx = jnp.where(d == WINDOW // TB, T_WEND, jnp.where(d > WINDOW // TB, T_NONE, idx))
            idx = jnp.where((d < 0) | jnp.logical_not(valid), T_NONE, idx)
            cols.append(toep_ref[h, idx])
        rows.append(jnp.concatenate(cols, axis=1))
    return jnp.concatenate(rows, axis=0)


def _bias_tile(toep_ref, d_tiles, window, valid=True):
    return jnp.concatenate([_bias_head(toep_ref, h, d_tiles, window, valid) for h in range(NSA_GROUP)], axis=0)


def _with_ones(v):
    return jnp.concatenate([v, jnp.ones(v.shape, v.dtype)], axis=1)


def _fold_lanes(x, op):
    out = x[:, :LANE]
    for c in range(1, x.shape[1] // LANE):
        out = op(out, x[:, c * LANE:(c + 1) * LANE])
    return out


def _nsa_kernel(q_ref, kc_ref, vc_ref, ks_ref, vs_ref, kw_ref, vw_ref, gate_ref, za_ref,
                bc_ref, toep_ref, posk_ref, w1k_ref, w2k_ref, posv_ref, w1v_ref, w2v_ref,
                o_ref, kcs_ref, vcs_ref, s_ref, macc_ref, acc_ref):
    qi = pl.program_id(2)
    s_len = kc_ref.shape[1]
    nr = s_len // CMP_STRIDE
    nb = s_len // SLC_BLOCK
    n_sel = min(SLC_TOP_N, nb)

    @pl.when(qi == 0)
    def _():
        kcs_ref[...] = _compress(kc_ref, posk_ref, w1k_ref, w2k_ref, nr).astype(BF16)
        vcs_ref[...] = _compress(vc_ref, posv_ref, w1v_ref, w2v_ref, nr).astype(BF16)

    q0 = qi * TQ
    rows = NSA_GROUP * TQ
    q = q_ref[0] * (HEAD_DIM ** -0.5 * LOG2E)
    q4 = jnp.concatenate([q[:, h * HEAD_DIM:(h + 1) * HEAD_DIM] for h in range(NSA_GROUP)],
                         axis=0).astype(BF16)

    sub = TQ // TB
    n_tiles = s_len // TQ

    def compressed_and_selection():
        t_c = q0 + _iota((TQ, nr), 0)
        i_c = _iota((TQ, nr), 1)
        mask_c = (t_c - (i_c * CMP_STRIDE + (CMP_BLOCK - 1)) >= 0) & (i_c < nr - 1)
        qk = _mm_nt(q4, kcs_ref[...])
        mask_add = jnp.where(mask_c, 0.0, NEG)
        yield
        p_heads = []
        for h in range(NSA_GROUP):
            lg = qk[h * TQ:(h + 1) * TQ] + bc_ref[h] + mask_add
            m_c = jnp.max(lg, axis=-1, keepdims=True)
            yield
            e = jnp.where(lg > 0.5 * NEG, jnp.exp2(lg - m_c), 0.0)
            l_c = jnp.sum(e, axis=-1, keepdims=True)
            yield
            p_heads.append(e / jnp.maximum(l_c, 1e-30))
        p = jnp.concatenate(p_heads, axis=0)
        o_c = _mm(p.astype(BF16), vcs_ref[...])
        psum = p_heads[0]
        for h in range(1, NSA_GROUP):
            psum = psum + p_heads[h]
        ov_i = _iota((nb, nr), 1) * CMP_STRIDE
        ov_j = _iota((nb, nr), 0) * SLC_BLOCK
        ov_t = ((ov_i < ov_j + SLC_BLOCK) & (ov_i + CMP_BLOCK > ov_j)).astype(BF16)
        p1, p2, p3 = _split3(psum)
        imp_t = _mm_nt(ov_t, p1) + _mm_nt(ov_t, p2) + _mm_nt(ov_t, p3)
        yield
        jb = _iota((nb, TQ), 0)
        cur = _div_pow2(q0 + _iota((nb, TQ), 1), SLC_BLOCK)
        forced = (jb == 0) | (jb == cur) | (jb == cur - 1)
        causal = jb <= cur
        score = jnp.where(forced, jnp.inf, jnp.where(causal, imp_t, -jnp.inf))
        rank = jnp.zeros((nb, TQ), jnp.int32)
        for jp in range(nb):
            sj = score[jp:jp + 1, :]
            beats = (sj > score) | ((sj == score) & (jb > jp))
            rank = rank + beats.astype(jnp.int32)
            if jp % 8 == 7:
                yield
        sel_t = ((rank < n_sel) & causal).astype(BF16)
        place = (_iota((nb, LANE), 0) == _iota((nb, LANE), 1)).astype(BF16)
        unsel = _mm_tn(sel_t, place) - (_iota((TQ, LANE), 1) < nb).astype(F32)
        yield
        return o_c, unsel.astype(BF16)

    def window():
        n_band = WINDOW // TQ + 1
        qk, tiles, v_w = [], [], []
        for c in range(n_band):
            j = qi - (n_band - 1) + c
            jc = jnp.maximum(j, 0)
            k0 = pl.multiple_of(jc * TQ, TQ)
            tiles.append((j, jc))
            qk.append(_mm_nt(q4, kw_ref[0, pl.ds(k0, TQ), :].astype(BF16)))
            v_w.append(_with_ones(vw_ref[0, pl.ds(k0, TQ), :].astype(BF16)))
        yield
        heads = range(NSA_GROUP)
        s_w = [[None] * NSA_GROUP for _ in range(n_band)]
        m_w = [None] * NSA_GROUP
        for c in range(n_band):
            for h in heads:
                s = qk[c][h * TQ:(h + 1) * TQ] + _bias_head(toep_ref, h, (qi - tiles[c][1]) * sub, True,
                                                             tiles[c][0] >= 0)
                s_w[c][h] = s
                m_w[h] = s if c == 0 else jnp.maximum(m_w[h], s)
                yield
        for h in heads:
            m_w[h] = jnp.max(m_w[h], axis=-1, keepdims=True)
        yield
        v_band = jnp.concatenate(v_w, axis=0)
        acc_w = []
        for h in heads:
            p_h = []
            for c in range(n_band):
                p_h.append(jnp.exp2(s_w[c][h] - m_w[h]).astype(BF16))
                yield
            acc_w.append(_mm(jnp.concatenate(p_h, axis=1), v_band))
            yield
        return jnp.concatenate([a[:, :HEAD_DIM] / jnp.maximum(a[:, HEAD_DIM:], 1e-30) for a in acc_w], axis=0)

    def gates():
        gts = jax.nn.sigmoid(gate_ref[0])
        grp = pl.program_id(1)
        za = za_ref[0]
        out = [[None] * NSA_GROUP for _ in range(3)]
        for h in range(NSA_GROUP):
            z = za[:, h * HEAD_DIM:(h + 1) * HEAD_DIM]
            zs = z * jax.nn.sigmoid(z)
            for branch in range(3):
                lane = branch * NSA_HEADS + h
                col = gts[:, lane:lane + 1]
                for g in range(1, NSA_KV_HEADS):
                    lg = lane + g * NSA_GROUP
                    col = jnp.where(grp == g, gts[:, lg:lg + 1], col)
                out[branch][h] = zs * col
                yield
        return out

    (o_c, unsel), o_w, gz = _interleave([compressed_and_selection(), window(), gates()])
    o_cw = [gz[0][h] * o_c[h * TQ:(h + 1) * TQ] + gz[2][h] * o_w[h * TQ:(h + 1) * TQ] for h in range(NSA_GROUP)]

    q_sel = jnp.concatenate([q4, jnp.concatenate([unsel] * NSA_GROUP, axis=0)], axis=1)

    def key_tile(j):
        k0 = pl.multiple_of(j * TQ, TQ)
        blk = _div_pow2(j * TQ + _iota((TQ, LANE), 0), SLC_BLOCK)
        marks = jnp.where(_iota((TQ, LANE), 1) == blk, BIG, 0.0).astype(BF16)
        return jnp.concatenate([ks_ref[0, pl.ds(k0, TQ), :].astype(BF16), marks], axis=1)

    assert n_tiles % 4 == 0
    n_pairs = _div_pow2(qi + 2, 2)
    passes, first = [], 0
    for tiles_per_iter in (8, 4, 2):
        if tiles_per_iter <= n_tiles:
            trips = _div_pow2(2 * n_pairs - first, tiles_per_iter)
            passes.append((tiles_per_iter, first, trips))
            first = first + tiles_per_iter * trips

    def slc_logits(tpi, base):
        def body(it, carry):
            tiles = [base + tpi * it + c for c in range(tpi)]
            qk = [_mm_nt(q_sel, key_tile(j)) for j in tiles]
            macc = macc_ref[...]
            for j, qk_j in zip(tiles, qk):
                s = qk_j + _bias_tile(toep_ref, (qi - j) * sub, False)
                s_ref[j] = s
                macc = jnp.maximum(macc, _fold_lanes(s, jnp.maximum))
            macc_ref[...] = macc
            return carry
        return body

    macc_ref[...] = jnp.full((rows, LANE), NEG, F32)
    for tpi, base, trips in passes:
        lax.fori_loop(0, trips, slc_logits(tpi, base), 0)
    m_s = jnp.max(macc_ref[...], axis=-1, keepdims=True)

    def slc_values(tpi, base):
        def body(it, carry):
            j0 = base + tpi * it
            k0 = pl.multiple_of(j0 * TQ, 2 * TQ)
            p_it = jnp.concatenate([jnp.exp2(s_ref[j0 + c] - m_s).astype(BF16) for c in range(tpi)], axis=1)
            acc_ref[...] += _mm(p_it, _with_ones(vs_ref[0, pl.ds(k0, tpi * TQ), :].astype(BF16)))
            return carry
        return body

    acc_ref[...] = jnp.zeros((rows, 2 * HEAD_DIM), F32)
    for tpi, base, trips in passes:
        lax.fori_loop(0, trips, slc_values(tpi, base), 0)
    acc = acc_ref[...]
    o_s = acc[:, :HEAD_DIM] / jnp.maximum(acc[:, HEAD_DIM:], 1e-30)

    for h in range(NSA_GROUP):
        o = o_cw[h] + gz[1][h] * o_s[h * TQ:(h + 1) * TQ]
        o_ref[0, :, h * HEAD_DIM:(h + 1) * HEAD_DIM] = o.astype(o_ref.dtype)


def _nsa(proj3, bias_c, toep, posk, w1k, w2k, posv, w1v, w2v):
    b, s, _ = proj3.shape
    nr = s // CMP_STRIDE
    gq = NSA_GROUP * HEAD_DIM

    def kvspec(idx):
        return pl.BlockSpec((1, s, HEAD_DIM), lambda bi, g, qi, idx=idx: (bi, 0, P_KV // HEAD_DIM + 2 * idx + g))

    def whole(a):
        return pl.BlockSpec(a.shape, lambda bi, g, qi, nd=a.ndim: (0,) * nd)

    in_specs = [
        pl.BlockSpec((1, TQ, gq), lambda bi, g, qi: (bi, qi, P_Q // gq + g)),
        kvspec(0), kvspec(1), kvspec(2), kvspec(3), kvspec(4), kvspec(5),
        pl.BlockSpec((1, TQ, LANE), lambda bi, g, qi: (bi, qi, P_GATE // LANE)),
        pl.BlockSpec((1, TQ, gq), lambda bi, g, qi: (bi, qi, P_ZA // gq + g)),
        pl.BlockSpec((NSA_GROUP, TQ, nr), lambda bi, g, qi: (g, qi, 0)),
        pl.BlockSpec((NSA_GROUP, NT_ALL, TB, TB), lambda bi, g, qi: (g, 0, 0, 0)),
        whole(posk), whole(w1k), whole(w2k), whole(posv), whole(w1v), whole(w2v),
    ]
    return pl.pallas_call(
        _nsa_kernel,
        grid=(b, NSA_KV_HEADS, s // TQ),
        in_specs=in_specs,
        out_specs=pl.BlockSpec((1, TQ, gq), lambda bi, g, qi: (bi, qi, g)),
        out_shape=jax.ShapeDtypeStruct((b, s, NSA_WIDTH), BF16),
        scratch_shapes=[pltpu.VMEM((nr, HEAD_DIM), BF16), pltpu.VMEM((nr, HEAD_DIM), BF16),
                        pltpu.VMEM((s // TQ, NSA_GROUP * TQ, TQ), F32),
                        pltpu.VMEM((NSA_GROUP * TQ, LANE), F32),
                        pltpu.VMEM((NSA_GROUP * TQ, 2 * HEAD_DIM), F32)],
        compiler_params=pltpu.CompilerParams(
            dimension_semantics=("parallel", "parallel", "arbitrary"),
            vmem_limit_bytes=VMEM_LIMIT_BYTES),
        name="nsa",
    )(proj3, proj3, proj3, proj3, proj3, proj3, proj3, proj3, proj3,
      bias_c, toep, posk, w1k, w2k, posv, w1v, w2v)


def _shift_mix(ref, prev_ref, mu, bb, sl):
    x = ref[bb, :, sl]
    prev = jnp.where(_iota(x.shape, 0) == 0, prev_ref[bb:bb + 1, sl], pltpu.roll(x, 1, 0))
    prev_ref[bb:bb + 1, sl] = x[x.shape[0] - 1:]
    return x + mu * (prev - x)


def _rwkv_kernel(r_ref, k_ref, v_ref, wa_ref, zb_ref, vec_ref, muwa_ref, w2_ref, a2_ref,
                 o_ref, st_ref, pr_ref, pk_ref, pv_ref, pwa_ref):
    first = pl.program_id(2) == 0
    n_batch, tb = r_ref.shape[0], r_ref.shape[1]
    n_groups = r_ref.shape[2] // GW
    n_chunks = tb // CH
    seqs = [(bb, gi) for bb in range(n_batch) for gi in range(n_groups)]

    @pl.when(first)
    def _():
        for ref in (st_ref, pr_ref, pk_ref, pv_ref, pwa_ref):
            ref[...] = jnp.zeros_like(ref)

    wd_act, ad = [], []
    for bb in range(n_batch):
        wa = _shift_mix(wa_ref, pwa_ref, muwa_ref[...], bb, slice(0, LANE))
        wd_act.append(jnp.tanh(wa[:, :LORA]).astype(BF16))
        ad.append(wa[:, LORA:].astype(BF16))

    seg = (_div_pow2(_iota((GW, GW), 0), RWKV_HEAD_DIM) == _div_pow2(_iota((GW, GW), 1), RWKV_HEAD_DIM))
    segf = seg.astype(F32)
    segb = seg.astype(BF16)
    assert CH == RWKV_HEAD_DIM
    lane_s = _mod_pow2(_iota((CH, GW), 1), CH)
    row_t = _iota((CH, GW), 0)
    strict = lane_s < row_t
    incl = lane_s <= row_t
    eye = (lane_s == row_t).astype(F32)
    ti_r, ti_c = _iota((tb, tb), 0), _iota((tb, tb), 1)
    trib = ((ti_c <= ti_r) & (_div_pow2(ti_c, CH) == _div_pow2(ti_r, CH))).astype(BF16)

    def bd(x):
        xb = x.astype(BF16)
        return jnp.concatenate([xb] * HPG, axis=0) * segb

    def prep(bb, gi):
        lanes = slice(gi * GW, (gi + 1) * GW)
        vec = vec_ref[:, lanes]
        mu_r, mu_k, mu_v = vec[0:1], vec[1:2], vec[2:3]
        w0, a0, k_k, k_a = vec[3:4], vec[4:5], vec[5:6], vec[6:7]
        r = _shift_mix(r_ref, pr_ref, mu_r, bb, lanes)
        k = _shift_mix(k_ref, pk_ref, mu_k, bb, lanes)
        v = _shift_mix(v_ref, pv_ref, mu_v, bb, lanes)
        w_lora = _mm(wd_act[bb], w2_ref[:, lanes])
        a_lora = _mm(ad[bb], a2_ref[:, lanes])
        kk = k * k_k
        kk_ss = _mm((kk * kk).astype(BF16), segb)
        yield
        lw = jax.nn.sigmoid(w0 + w_lora) * (-math.exp(-0.5) * LOG2E)
        cum = _mm_split_rhs(trib, lw)
        yield
        a_sig = jax.nn.sigmoid(a0 + a_lora)
        kk = kk * lax.rsqrt(jnp.maximum(kk_ss, 1e-24))
        k = k * (1.0 + (a_sig - 1.0) * k_a)
        return dict(r=r, k=k, v=v, a=-kk, b=kk * a_sig, lw=lw, cum=cum, vec=vec, lanes=lanes, bb=bb)

    groups = _interleave([prep(bb, gi) for bb, gi in seqs])

    def chunk_local(g, c):
        ts = slice(c * CH, (c + 1) * CH)
        rc, kc, vc, ac, bc, lwc, cum = (g[n][ts] for n in ("r", "k", "v", "a", "b", "lw", "cum"))
        tot = cum[CH - 1:CH]
        e_out = jnp.exp2(-cum)
        e_end = jnp.exp2(tot - cum)
        r_t = rc * jnp.exp2(cum)
        a_t = ac * jnp.exp2(cum - lwc)
        lhs = jnp.concatenate([a_t, r_t], axis=0).astype(BF16)
        aa = _mm_nt(lhs, jnp.concatenate([bd(bc * e_out), bd(kc * e_out)], axis=0))
        yield
        a_ab = jnp.where(strict, aa[:CH, :GW], 0.0)
        a_ak = jnp.where(strict, aa[:CH, GW:], 0.0)
        a_rb = jnp.where(incl, aa[CH:, :GW], 0.0)
        a_rk = jnp.where(incl, aa[CH:, GW:], 0.0)
        t_inv = eye + a_ab
        mpow = _mm(a_ab.astype(BF16), bd(a_ab))
        av = _mm(a_ak.astype(BF16), bd(vc))
        yield
        for _ in range(int(math.log2(CH)) - 1):
            res = _mm(jnp.concatenate([t_inv, mpow], axis=0).astype(BF16), bd(mpow))
            yield
            t_inv = t_inv + res[:CH]
            mpow = res[CH:]
        wu = _mm(t_inv.astype(BF16), jnp.concatenate([bd(a_t), bd(av)], axis=1))
        yield
        return dict(
            lhs=jnp.concatenate([wu[:, :GW], r_t], axis=0).astype(BF16), u_loc=wu[:, GW:],
            a_r=jnp.concatenate([a_rb, a_rk], axis=1).astype(BF16), bdv=bd(vc), vc=vc,
            bk_end=jnp.concatenate([bc * e_end, kc * e_end], axis=0).astype(BF16), dec=jnp.exp2(tot))

    loc = _interleave([chunk_local(g, c) for g in groups for c in range(n_chunks)])

    def chain(q):
        ys = []
        g_state = st_ref[q]
        for c in range(n_chunks):
            lc = loc[q * n_chunks + c]
            x0 = _mm_nt(lc["lhs"], g_state.astype(BF16))
            yield
            u = x0[:CH] + lc["u_loc"]
            y_c = _mm(lc["a_r"], jnp.concatenate([bd(u), lc["bdv"]], axis=0))
            upd = _mm_tn(jnp.concatenate([u, lc["vc"]], axis=0).astype(BF16), lc["bk_end"])
            yield
            ys.append(x0[CH:] + y_c)
            g_state = g_state * lc["dec"] + upd * segf
        st_ref[q] = g_state
        return jnp.concatenate(ys, axis=0)

    ys = _interleave([chain(q) for q in range(len(seqs))])

    def finish(g, y):
        vec = g["vec"]
        ln_w, ln_b, r_k = vec[7:8], vec[8:9], vec[9:10]
        inv_n = 1.0 / RWKV_HEAD_DIM
        mean = _mm(y.astype(BF16), segb) * inv_n
        bonus = _mm((g["r"] * g["k"] * r_k).astype(BF16), segb) * g["v"]
        yield
        yc = y - mean
        var = _mm((yc * yc).astype(BF16), segb) * inv_n
        yield
        yn = yc * lax.rsqrt(var + RWKV_GN_EPS) * ln_w + ln_b
        zb = zb_ref[g["bb"], :, g["lanes"]]
        o_ref[g["bb"], :, g["lanes"]] = ((yn + bonus) * (zb * jax.nn.sigmoid(zb))).astype(o_ref.dtype)

    _interleave([finish(g, y) for g, y in zip(groups, ys)])


def _rwkv(proj3, vecs, mu_wa, w2, a2, tb, gps, bps):
    b, s, _ = proj3.shape
    assert b % bps == 0, (b, bps)
    gw = gps * GW
    ng = RWKV_WIDTH // gw

    def col(off):
        return pl.BlockSpec((bps, tb, gw), lambda bi, g, ti, off=off: (bi, ti, off // gw + g))

    in_specs = [
        col(P_RKV), col(P_RKV + RWKV_WIDTH), col(P_RKV + 2 * RWKV_WIDTH),
        pl.BlockSpec((bps, tb, LANE), lambda bi, g, ti: (bi, ti, P_WDAD // LANE)),
        col(P_ZB),
        pl.BlockSpec((vecs.shape[0], gw), lambda bi, g, ti: (0, g)),
        pl.BlockSpec((1, LANE), lambda bi, g, ti: (0, 0)),
        pl.BlockSpec((LORA, gw), lambda bi, g, ti: (0, g)),
        pl.BlockSpec((LORA, gw), lambda bi, g, ti: (0, g)),
    ]
    return pl.pallas_call(
        _rwkv_kernel,
        grid=(b // bps, ng, s // tb),
        in_specs=in_specs,
        out_specs=pl.BlockSpec((bps, tb, gw), lambda bi, g, ti: (bi, ti, g)),
        out_shape=jax.ShapeDtypeStruct((b, s, RWKV_WIDTH), BF16),
        scratch_shapes=[pltpu.VMEM((bps * gps, GW, GW), F32), pltpu.VMEM((bps, gw), F32),
                        pltpu.VMEM((bps, gw), F32), pltpu.VMEM((bps, gw), F32), pltpu.VMEM((bps, LANE), F32)],
        compiler_params=pltpu.CompilerParams(
            dimension_semantics=("parallel", "parallel", "arbitrary")),
        name="rwkv",
    )(proj3, proj3, proj3, proj3, proj3, vecs, mu_wa, w2, a2)


def _outproj_kernel(ma_ref, mb_ref, wa_ref, wb_ref, x_ref, g_ref, o_ref):
    y = _mm(ma_ref[...], wa_ref[...]) + _mm(mb_ref[...], wb_ref[...])
    ms = jnp.mean(y * y, axis=-1, keepdims=True)
    o_ref[...] = x_ref[...] + y * lax.rsqrt(ms + NORM_EPS) * g_ref[...]


def _outproj(mix_a, mix_b, w_a, w_b, x2, g, tm):
    m, d = x2.shape
    ka, kb = mix_a.shape[1], mix_b.shape[1]
    return pl.pallas_call(
        _outproj_kernel,
        grid=(m // tm,),
        in_specs=[
            pl.BlockSpec((tm, ka), lambda i: (i, 0)),
            pl.BlockSpec((tm, kb), lambda i: (i, 0)),
            pl.BlockSpec((ka, d), lambda i: (0, 0)),
            pl.BlockSpec((kb, d), lambda i: (0, 0)),
            pl.BlockSpec((tm, d), lambda i: (i, 0)),
            pl.BlockSpec((1, d), lambda i: (0, 0)),
        ],
        out_specs=pl.BlockSpec((tm, d), lambda i: (i, 0)),
        out_shape=jax.ShapeDtypeStruct((m, d), F32),
        compiler_params=pltpu.CompilerParams(
            dimension_semantics=("parallel",), vmem_limit_bytes=VMEM_LIMIT_BYTES),
        name="outproj",
    )(mix_a, mix_b, w_a, w_b, x2, g)


_W_SEGMENTS = (
    (P_Q, R_Q, NSA_WIDTH),
    (P_RKV, R_FEAT, 3 * RWKV_WIDTH),
    (P_ZB, R_ZB, RWKV_WIDTH),
    (P_ZA, R_ZA, NSA_WIDTH),
    (P_KV, R_KV, 6 * NSA_KV_HEADS * HEAD_DIM),
    (P_WDAD, R_FEAT + 3 * RWKV_WIDTH, 2 * LORA),
    (P_GATE, R_GATE, LANE),
)


def _relayout_kernel(w_ref, o_ref):
    for dst, src, width in _W_SEGMENTS:
        o_ref[:, dst:dst + width] = w_ref[:, src:src + width].astype(BF16)
    used = P_GATE + LANE
    o_ref[:, used:] = jnp.zeros((o_ref.shape[0], NP - used), BF16)


def _permute_w_in(w, rows):
    d, n = w.shape
    return pl.pallas_call(
        _relayout_kernel,
        grid=(d // rows,),
        in_specs=[pl.BlockSpec((rows, n), lambda i: (i, 0))],
        out_specs=pl.BlockSpec((rows, NP), lambda i: (i, 0)),
        out_shape=jax.ShapeDtypeStruct((d, NP), BF16),
        compiler_params=pltpu.CompilerParams(dimension_semantics=("parallel",)),
        name="relayout",
    )(w)


def _block(x, pre_norm_g, w_in, rel_bias_table, cmp_pos_k, cmp_pos_v, cmp_k_w1, cmp_k_w2, cmp_v_w1,
           cmp_v_w2, rwkv_mu, rwkv_w0, rwkv_w2, rwkv_a0, rwkv_a2, rwkv_k_k, rwkv_k_a, rwkv_r_k,
           rwkv_ln_w, rwkv_ln_b, w_out, post_norm_g):
    b, s, d = x.shape
    assert w_in.shape == (d, R_END) and w_out.shape == (NSA_WIDTH + RWKV_WIDTH, d), (w_in.shape, w_out.shape)
    assert s % (4 * TQ) == 0 and s >= WINDOW + TQ, s
    x2 = x.reshape(b * s, d)
    proj = _inproj(x2, pre_norm_g.reshape(1, d), _permute_w_in(w_in.astype(BF16), RELAYOUT_ROWS),
                   min(INPROJ_TM, b * s), INPROJ_TN)
    proj3 = proj.reshape(b, s, NP)

    bias_c, toep = _bias(rel_bias_table.reshape(-1), s)
    half = CMP_STRIDE * HEAD_DIM
    mix_a = _nsa(proj3, bias_c, toep,
                 cmp_pos_k.reshape(2, half), cmp_k_w1.astype(BF16), cmp_k_w2.astype(BF16),
                 cmp_pos_v.reshape(2, half), cmp_v_w1.astype(BF16), cmp_v_w2.astype(BF16))

    w3 = 3 * RWKV_WIDTH
    vec_rows = [rwkv_mu[:RWKV_WIDTH], rwkv_mu[RWKV_WIDTH:2 * RWKV_WIDTH], rwkv_mu[2 * RWKV_WIDTH:w3],
                rwkv_w0, rwkv_a0, rwkv_k_k, rwkv_k_a, rwkv_ln_w, rwkv_ln_b, rwkv_r_k.reshape(-1)]
    vecs = jnp.stack(vec_rows + [jnp.zeros_like(rwkv_w0)] * (16 - len(vec_rows)), axis=0)
    mix_b = _rwkv(proj3, vecs, rwkv_mu[w3:].reshape(1, 2 * LORA), rwkv_w2.astype(BF16),
                  rwkv_a2.astype(BF16), min(RWKV_SLAB, s), RWKV_GROUPS_PER_STEP, math.gcd(RWKV_BATCH_PER_STEP, b))

    w_o = w_out.astype(BF16)
    out = _outproj(mix_a.reshape(b * s, NSA_WIDTH), mix_b.reshape(b * s, RWKV_WIDTH),
                   w_o[:NSA_WIDTH], w_o[NSA_WIDTH:], x2, post_norm_g.reshape(1, d), min(OUTPROJ_TM, b * s))
    return out.reshape(b, s, d)


def kernel(x, pre_norm_g, w_in, rel_bias_table, cmp_pos_k, cmp_pos_v, cmp_k_w1, cmp_k_w2, cmp_v_w1,
           cmp_v_w2, rwkv_mu, rwkv_w0, rwkv_w2, rwkv_a0, rwkv_a2, rwkv_k_k, rwkv_k_a, rwkv_r_k,
           rwkv_ln_w, rwkv_ln_b, w_out, post_norm_g):
    h = x
    for l in range(pre_norm_g.shape[0]):
        h = _block(h, pre_norm_g[l], w_in[l], rel_bias_table, cmp_pos_k[l], cmp_pos_v[l], cmp_k_w1[l],
                   cmp_k_w2[l], cmp_v_w1[l], cmp_v_w2[l], rwkv_mu[l], rwkv_w0[l], rwkv_w2[l],
                   rwkv_a0[l], rwkv_a2[l], rwkv_k_k[l], rwkv_k_a[l], rwkv_r_k[l], rwkv_ln_w[l],
                   rwkv_ln_b[l], w_out[l], post_norm_g[l])
    return h
```

```python
import math

import jax
import jax.numpy as jnp
from jax import lax
from jax.experimental import pallas as pl
from jax.experimental.pallas import tpu as pltpu

F32 = jnp.float32
BF16 = jnp.bfloat16

NSA_HEADS = 8
NSA_KV_HEADS = 2
NSA_GROUP = NSA_HEADS // NSA_KV_HEADS
HEAD_DIM = 128
NSA_WIDTH = NSA_HEADS * HEAD_DIM
CMP_BLOCK = 32
CMP_STRIDE = 16
SLC_BLOCK = 64
SLC_TOP_N = 16
WINDOW = 512
RWKV_WIDTH = 1024
RWKV_HEAD_DIM = 64
LORA = 64
NUM_BUCKETS = 32
MAX_DISTANCE = 1024
NORM_EPS = 1e-6
RWKV_GN_EPS = 64e-5

R_Q = 0
R_KV = R_Q + NSA_WIDTH
R_GATE = R_KV + 6 * NSA_KV_HEADS * HEAD_DIM
R_ZA = R_GATE + 3 * NSA_HEADS
R_FEAT = R_ZA + NSA_WIDTH
R_ZB = R_FEAT + 3 * RWKV_WIDTH + 2 * LORA
R_END = R_ZB + RWKV_WIDTH

P_Q = 0
P_RKV = 1024
P_ZB = 4096
P_ZA = 5120
P_KV = 6144
P_WDAD = 7680
P_GATE = 7808
NP = 8192

LANE = 128
TQ = 256
TB = 128
ND = 9
T_DIAG = ND
T_WEND = ND + 1
T_NONE = ND + 2
NT_ALL = ND + 3
LOG2E = math.log2(math.e)
BIG = 2.0 ** 100
CH = 64
HPG = 4
GW = HPG * RWKV_HEAD_DIM
RWKV_GROUPS_PER_STEP = 4
RWKV_BATCH_PER_STEP = 4
NEG = -1e30


VMEM_LIMIT_BYTES = 56 * 1024 * 1024
INPROJ_TM, INPROJ_TN = 1024, 2048
OUTPROJ_TM = 512
RELAYOUT_ROWS = 256
BIAS_ROWS = 512
RWKV_SLAB = 128


def _bucket_thresholds():
    exact = NUM_BUCKETS // 2
    ratio = MAX_DISTANCE // exact
    out = []
    for k in range(1, NUM_BUCKETS - exact):
        n = exact
        while n ** exact < (exact ** exact) * (ratio ** k):
            n += 1
        out.append(n)
    return out


_THR = _bucket_thresholds()


def _mm(a, b):
    return jnp.dot(a, b, preferred_element_type=F32)


def _mm_nt(a, b):
    return lax.dot_general(a, b, (((1,), (1,)), ((), ())), preferred_element_type=F32)


def _mm_tn(a, b):
    return lax.dot_general(a, b, (((0,), (0,)), ((), ())), preferred_element_type=F32)


def _split3(x):
    x1 = x.astype(BF16)
    r1 = x - x1.astype(F32)
    x2 = r1.astype(BF16)
    x3 = (r1 - x2.astype(F32)).astype(BF16)
    return x1, x2, x3


def _mm_split_rhs(a_exact, b):
    b1 = b.astype(BF16)
    b2 = (b - b1.astype(F32)).astype(BF16)
    return _mm(a_exact, b1) + _mm(a_exact, b2)


def _iota(shape, dim):
    return lax.broadcasted_iota(jnp.int32, shape, dim)


def _interleave(gens):
    results = [None] * len(gens)
    live = list(enumerate(gens))
    while live:
        still = []
        for i, g in live:
            try:
                next(g)
                still.append((i, g))
            except StopIteration as stop:
                results[i] = stop.value
        live = still
    return results


def _div_pow2(x, n):
    assert n & (n - 1) == 0
    return x >> (n.bit_length() - 1)


def _mod_pow2(x, n):
    assert n & (n - 1) == 0
    return x & (n - 1)


def _inproj_kernel(x_ref, g_ref, w_ref, o_ref, hn_ref):
    @pl.when(pl.program_id(1) == 0)
    def _():
        x = x_ref[...]
        ms = jnp.mean(x * x, axis=-1, keepdims=True)
        hn_ref[...] = (x * lax.rsqrt(ms + NORM_EPS) * g_ref[...]).astype(BF16)

    o_ref[...] = _mm(hn_ref[...], w_ref[...])


def _inproj(x2, g, w, tm, tn):
    m, d = x2.shape
    n = w.shape[1]
    return pl.pallas_call(
        _inproj_kernel,
        grid=(m // tm, n // tn),
        in_specs=[
            pl.BlockSpec((tm, d), lambda i, j: (i, 0)),
            pl.BlockSpec((1, d), lambda i, j: (0, 0)),
            pl.BlockSpec((d, tn), lambda i, j: (0, j)),
        ],
        out_specs=pl.BlockSpec((tm, tn), lambda i, j: (i, j)),
        out_shape=jax.ShapeDtypeStruct((m, n), F32),
        scratch_shapes=[pltpu.VMEM((tm, d), BF16)],
        compiler_params=pltpu.CompilerParams(
            dimension_semantics=("parallel", "arbitrary"),
            vmem_limit_bytes=VMEM_LIMIT_BYTES),
        name="inproj",
    )(x2, g, w)


def _bucket(n):
    n = jnp.maximum(n, 0)
    large = jnp.full(n.shape, NUM_BUCKETS // 2, jnp.int32)
    for thr in _THR:
        large = large + (n >= thr).astype(jnp.int32)
    return jnp.where(n < NUM_BUCKETS // 2, n, large)


def _bucket_of(n):
    n = max(n, 0)
    return n if n < NUM_BUCKETS // 2 else NUM_BUCKETS // 2 + sum(n >= thr for thr in _THR)


def _lookup_all_heads(dist, tab_ref, d_min=None, d_max=None):
    lo = 0 if d_min is None else _bucket_of(d_min)
    hi = NUM_BUCKETS - 1 if d_max is None else _bucket_of(d_max)
    bucket = _bucket(dist)
    hits = {b: bucket == b for b in range(lo, hi + 1)}
    outs = []
    for h in range(NSA_HEADS):
        out = jnp.zeros(dist.shape, F32)
        for b in range(lo, hi + 1):
            out = jnp.where(hits[b], tab_ref[b * NSA_HEADS + h] * LOG2E, out)
        outs.append(out)
    return outs


def _bias_kernel(tab_ref, bc_ref, tp_ref):
    i = pl.program_id(0)
    rows, nr = bc_ref.shape[1], bc_ref.shape[2]
    dist_c = (i * rows + _iota((rows, nr), 0)) - (_iota((rows, nr), 1) * CMP_STRIDE + (CMP_BLOCK - 1))
    for h, vals in enumerate(_lookup_all_heads(dist_c, tab_ref)):
        bc_ref[h] = vals

    @pl.when(i == 0)
    def _():
        base = _iota((TB, TB), 0) - _iota((TB, TB), 1)
        neg = jnp.full((TB, TB), NEG, F32)
        for h in range(NSA_HEADS):
            tp_ref[h, T_NONE] = neg
        for d in range(ND):
            for h, vals in enumerate(_lookup_all_heads(base + d * TB, tab_ref, (d - 1) * TB + 1, (d + 1) * TB - 1)):
                tp_ref[h, d] = vals
                if d == 0:
                    tp_ref[h, T_DIAG] = jnp.where(base >= 0, vals, neg)
                if d == WINDOW // TB:
                    tp_ref[h, T_WEND] = jnp.where(base < 0, vals, neg)


def _bias(table_flat, s):
    nr = s // CMP_STRIDE
    rows = min(BIAS_ROWS, s)
    return pl.pallas_call(
        _bias_kernel,
        grid=(s // rows,),
        in_specs=[pl.BlockSpec(memory_space=pltpu.SMEM)],
        out_specs=[
            pl.BlockSpec((NSA_HEADS, rows, nr), lambda i: (0, i, 0)),
            pl.BlockSpec((NSA_HEADS, NT_ALL, TB, TB), lambda i: (0, 0, 0, 0)),
        ],
        out_shape=[
            jax.ShapeDtypeStruct((NSA_HEADS, s, nr), F32),
            jax.ShapeDtypeStruct((NSA_HEADS, NT_ALL, TB, TB), F32),
        ],
        compiler_params=pltpu.CompilerParams(dimension_semantics=("arbitrary",)),
        name="bias",
    )(table_flat)


def _compress(kv_ref, pos_ref, w1_ref, w2_ref, nr):
    half = CMP_STRIDE * HEAD_DIM
    r = jnp.concatenate(
        [kv_ref[0, pl.ds(m, nr, stride=CMP_STRIDE), :] for m in range(CMP_STRIDE)], axis=1)
    a = _mm((r + pos_ref[0:1, :]).astype(BF16), w1_ref[0:half, :])
    b = _mm((r + pos_ref[1:2, :]).astype(BF16), w1_ref[half:2 * half, :])
    pre = a + pltpu.roll(b, nr - 1, 0)
    h1 = pre * jax.nn.sigmoid(pre)
    return _mm(h1.astype(BF16), w2_ref[...])


def _bias_head(toep_ref, h, d_tiles, window, valid=True):
    sub = TQ // TB
    rows = []
    for ri in range(sub):
        cols = []
        for ci in range(sub):
            d = d_tiles + ri - ci
            idx = jnp.where(d == 0, T_DIAG, jnp.minimum(d, ND - 1))
            if window:
                idx = jnp.where(d == WINDOW // TB, T_WEND, jnp.where(d > WINDOW // TB, T_NONE, idx))
            idx = jnp.where((d < 0) | jnp.logical_not(valid), T_NONE, idx)
            cols.append(toep_ref[h, idx])
        rows.append(jnp.concatenate(cols, axis=1))
    return jnp.concatenate(rows, axis=0)


def _bias_tile(toep_ref, d_tiles, window, valid=True):
    return jnp.concatenate([_bias_head(toep_ref, h, d_tiles, window, valid) for h in range(NSA_GROUP)], axis=0)


def _with_ones(v):
    return jnp.concatenate([v, jnp.ones(v.shape, v.dtype)], axis=1)


def _fold_lanes(x, op):
    out = x[:, :LANE]
    for c in range(1, x.shape[1] // LANE):
        out = op(out, x[:, c * LANE:(c + 1) * LANE])
    return out


def _nsa_kernel(q_ref, kc_ref, vc_ref, ks_ref, vs_ref, kw_ref, vw_ref, gate_ref, za_ref,
                bc_ref, toep_ref, posk_ref, w1k_ref, w2k_ref, posv_ref, w1v_ref, w2v_ref,
                o_ref, kcs_ref, vcs_ref, s_ref, macc_ref, acc_ref):
    qi = pl.program_id(2)
    s_len = kc_ref.shape[1]
    nr = s_len // CMP_STRIDE
    nb = s_len // SLC_BLOCK
    n_sel = min(SLC_TOP_N, nb)

    @pl.when(qi == 0)
    def _():
        kcs_ref[...] = _compress(kc_ref, posk_ref, w1k_ref, w2k_ref, nr).astype(BF16)
        vcs_ref[...] = _compress(vc_ref, posv_ref, w1v_ref, w2v_ref, nr).astype(BF16)

    q0 = qi * TQ
    rows = NSA_GROUP * TQ
    q = q_ref[0] * (HEAD_DIM ** -0.5 * LOG2E)
    q4 = jnp.concatenate([q[:, h * HEAD_DIM:(h + 1) * HEAD_DIM] for h in range(NSA_GROUP)],
                         axis=0).astype(BF16)

    sub = TQ // TB
    n_tiles = s_len // TQ

    def compressed_and_selection():
        t_c = q0 + _iota((TQ, nr), 0)
        i_c = _iota((TQ, nr), 1)
        mask_c = (t_c - (i_c * CMP_STRIDE + (CMP_BLOCK - 1)) >= 0) & (i_c < nr - 1)
        qk = _mm_nt(q4, kcs_ref[...])
        mask_add = jnp.where(mask_c, 0.0, NEG)
        yield
        p_heads = []
        for h in range(NSA_GROUP):
            lg = qk[h * TQ:(h + 1) * TQ] + bc_ref[h] + mask_add
            m_c = jnp.max(lg, axis=-1, keepdims=True)
            yield
            e = jnp.where(lg > 0.5 * NEG, jnp.exp2(lg - m_c), 0.0)
            l_c = jnp.sum(e, axis=-1, keepdims=True)
            yield
            p_heads.append(e / jnp.maximum(l_c, 1e-30))
        p = jnp.concatenate(p_heads, axis=0)
        o_c = _mm(p.astype(BF16), vcs_ref[...])
        psum = p_heads[0]
        for h in range(1, NSA_GROUP):
            psum = psum + p_heads[h]
        ov_i = _iota((nb, nr), 1) * CMP_STRIDE
        ov_j = _iota((nb, nr), 0) * SLC_BLOCK
        ov_t = ((ov_i < ov_j + SLC_BLOCK) & (ov_i + CMP_BLOCK > ov_j)).astype(BF16)
        p1, p2, p3 = _split3(psum)
        imp_t = _mm_nt(ov_t, p1) + _mm_nt(ov_t, p2) + _mm_nt(ov_t, p3)
        yield
        jb = _iota((nb, TQ), 0)
        cur = _div_pow2(q0 + _iota((nb, TQ), 1), SLC_BLOCK)
        forced = (jb == 0) | (jb == cur) | (jb == cur - 1)
        causal = jb <= cur
        score = jnp.where(forced, jnp.inf, jnp.where(causal, imp_t, -jnp.inf))
        rank = jnp.zeros((nb, TQ), jnp.int32)
        for jp in range(nb):
            sj = score[jp:jp + 1, :]
            beats = (sj > score) | ((sj == score) & (jb > jp))
            rank = rank + beats.astype(jnp.int32)
            if jp % 8 == 7:
                yield
        sel_t = ((rank < n_sel) & causal).astype(BF16)
        place = (_iota((nb, LANE), 0) == _iota((nb, LANE), 1)).astype(BF16)
        unsel = _mm_tn(sel_t, place) - (_iota((TQ, LANE), 1) < nb).astype(F32)
        yield
        return o_c, unsel.astype(BF16)

    def window():
        n_band = WINDOW // TQ + 1
        qk, tiles, v_w = [], [], []
        for c in range(n_band):
            j = qi - (n_band - 1) + c
            jc = jnp.maximum(j, 0)
            k0 = pl.multiple_of(jc * TQ, TQ)
            tiles.append((j, jc))
            qk.append(_mm_nt(q4, kw_ref[0, pl.ds(k0, TQ), :].astype(BF16)))
            v_w.append(_with_ones(vw_ref[0, pl.ds(k0, TQ), :].astype(BF16)))
        yield
        heads = range(NSA_GROUP)
        s_w = [[None] * NSA_GROUP for _ in range(n_band)]
        m_w = [None] * NSA_GROUP
        for c in range(n_band):
            for h in heads:
                s = qk[c][h * TQ:(h + 1) * TQ] + _bias_head(toep_ref, h, (qi - tiles[c][1]) * sub, True,
                                                             tiles[c][0] >= 0)
                s_w[c][h] = s
                m_w[h] = s if c == 0 else jnp.maximum(m_w[h], s)
                yield
        for h in heads:
            m_w[h] = jnp.max(m_w[h], axis=-1, keepdims=True)
        yield
        v_band = jnp.concatenate(v_w, axis=0)
        acc_w = []
        for h in heads:
            p_h = []
            for c in range(n_band):
                p_h.append(jnp.exp2(s_w[c][h] - m_w[h]).astype(BF16))
                yield
            acc_w.append(_mm(jnp.concatenate(p_h, axis=1), v_band))
            yield
        return jnp.concatenate([a[:, :HEAD_DIM] / jnp.maximum(a[:, HEAD_DIM:], 1e-30) for a in acc_w], axis=0)

    def gates():
        gts = jax.nn.sigmoid(gate_ref[0])
        grp = pl.program_id(1)
        za = za_ref[0]
        out = [[None] * NSA_GROUP for _ in range(3)]
        for h in range(NSA_GROUP):
            z = za[:, h * HEAD_DIM:(h + 1) * HEAD_DIM]
            zs = z * jax.nn.sigmoid(z)
            for branch in range(3):
                lane = branch * NSA_HEADS + h
                col = gts[:, lane:lane + 1]
                for g in range(1, NSA_KV_HEADS):
                    lg = lane + g * NSA_GROUP
                    col = jnp.where(grp == g, gts[:, lg:lg + 1], col)
                out[branch][h] = zs * col
                yield
        return out

    (o_c, unsel), o_w, gz = _interleave([compressed_and_selection(), window(), gates()])
    o_cw = [gz[0][h] * o_c[h * TQ:(h + 1) * TQ] + gz[2][h] * o_w[h * TQ:(h + 1) * TQ] for h in range(NSA_GROUP)]

    q_sel = jnp.concatenate([q4, jnp.concatenate([unsel] * NSA_GROUP, axis=0)], axis=1)

    def key_tile(j):
        k0 = pl.multiple_of(j * TQ, TQ)
        blk = _div_pow2(j * TQ + _iota((TQ, LANE), 0), SLC_BLOCK)
        marks = jnp.where(_iota((TQ, LANE), 1) == blk, BIG, 0.0).astype(BF16)
        return jnp.concatenate([ks_ref[0, pl.ds(k0, TQ), :].astype(BF16), marks], axis=1)

    assert n_tiles % 4 == 0
    n_pairs = _div_pow2(qi + 2, 2)
    passes, first = [], 0
    for tiles_per_iter in (8, 4, 2):
        if tiles_per_iter <= n_tiles:
            trips = _div_pow2(2 * n_pairs - first, tiles_per_iter)
            passes.append((tiles_per_iter, first, trips))
            first = first + tiles_per_iter * trips

    def slc_logits(tpi, base):
        def body(it, carry):
            tiles = [base + tpi * it + c for c in range(tpi)]
            qk = [_mm_nt(q_sel, key_tile(j)) for j in tiles]
            macc = macc_ref[...]
            for j, qk_j in zip(tiles, qk):
                s = qk_j + _bias_tile(toep_ref, (qi - j) * sub, False)
                s_ref[j] = s
                macc = jnp.maximum(macc, _fold_lanes(s, jnp.maximum))
            macc_ref[...] = macc
            return carry
        return body

    macc_ref[...] = jnp.full((rows, LANE), NEG, F32)
    for tpi, base, trips in passes:
        lax.fori_loop(0, trips, slc_logits(tpi, base), 0)
    m_s = jnp.max(macc_ref[...], axis=-1, keepdims=True)

    def slc_values(tpi, base):
        def body(it, carry):
            j0 = base + tpi * it
            k0 = pl.multiple_of(j0 * TQ, 2 * TQ)
            p_it = jnp.concatenate([jnp.exp2(s_ref[j0 + c] - m_s).astype(BF16) for c in range(tpi)], axis=1)
            acc_ref[...] += _mm(p_it, _with_ones(vs_ref[0, pl.ds(k0, tpi * TQ), :].astype(BF16)))
            return carry
        return body

    acc_ref[...] = jnp.zeros((rows, 2 * HEAD_DIM), F32)
    for tpi, base, trips in passes:
        lax.fori_loop(0, trips, slc_values(tpi, base), 0)
    acc = acc_ref[...]
    o_s = acc[:, :HEAD_DIM] / jnp.maximum(acc[:, HEAD_DIM:], 1e-30)

    for h in range(NSA_GROUP):
        o = o_cw[h] + gz[1][h] * o_s[h * TQ:(h + 1) * TQ]
        o_ref[0, :, h * HEAD_DIM:(h + 1) * HEAD_DIM] = o.astype(o_ref.dtype)


def _nsa(proj3, bias_c, toep, posk, w1k, w2k, posv, w1v, w2v):
    b, s, _ = proj3.shape
    nr = s // CMP_STRIDE
    gq = NSA_GROUP * HEAD_DIM

    def kvspec(idx):
        return pl.BlockSpec((1, s, HEAD_DIM), lambda bi, g, qi, idx=idx: (bi, 0, P_KV // HEAD_DIM + 2 * idx + g))

    def whole(a):
        return pl.BlockSpec(a.shape, lambda bi, g, qi, nd=a.ndim: (0,) * nd)

    in_specs = [
        pl.BlockSpec((1, TQ, gq), lambda bi, g, qi: (bi, qi, P_Q // gq + g)),
        kvspec(0), kvspec(1), kvspec(2), kvspec(3), kvspec(4), kvspec(5),
        pl.BlockSpec((1, TQ, LANE), lambda bi, g, qi: (bi, qi, P_GATE // LANE)),
        pl.BlockSpec((1, TQ, gq), lambda bi, g, qi: (bi, qi, P_ZA // gq + g)),
        pl.BlockSpec((NSA_GROUP, TQ, nr), lambda bi, g, qi: (g, qi, 0)),
        pl.BlockSpec((NSA_GROUP, NT_ALL, TB, TB), lambda bi, g, qi: (g, 0, 0, 0)),
        whole(posk), whole(w1k), whole(w2k), whole(posv), whole(w1v), whole(w2v),
    ]
    return pl.pallas_call(
        _nsa_kernel,
        grid=(b, NSA_KV_HEADS, s // TQ),
        in_specs=in_specs,
        out_specs=pl.BlockSpec((1, TQ, gq), lambda bi, g, qi: (bi, qi, g)),
        out_shape=jax.ShapeDtypeStruct((b, s, NSA_WIDTH), BF16),
        scratch_shapes=[pltpu.VMEM((nr, HEAD_DIM), BF16), pltpu.VMEM((nr, HEAD_DIM), BF16),
                        pltpu.VMEM((s // TQ, NSA_GROUP * TQ, TQ), F32),
                        pltpu.VMEM((NSA_GROUP * TQ, LANE), F32),
                        pltpu.VMEM((NSA_GROUP * TQ, 2 * HEAD_DIM), F32)],
        compiler_params=pltpu.CompilerParams(
            dimension_semantics=("parallel", "parallel", "arbitrary"),
            vmem_limit_bytes=VMEM_LIMIT_BYTES),
        name="nsa",
    )(proj3, proj3, proj3, proj3, proj3, proj3, proj3, proj3, proj3,
      bias_c, toep, posk, w1k, w2k, posv, w1v, w2v)


def _shift_mix(ref, prev_ref, mu, bb, sl):
    x = ref[bb, :, sl]
    prev = jnp.where(_iota(x.shape, 0) == 0, prev_ref[bb:bb + 1, sl], pltpu.roll(x, 1, 0))
    prev_ref[bb:bb + 1, sl] = x[x.shape[0] - 1:]
    return x + mu * (prev - x)


def _rwkv_kernel(r_ref, k_ref, v_ref, wa_ref, zb_ref, vec_ref, muwa_ref, w2_ref, a2_ref,
                 o_ref, st_ref, pr_ref, pk_ref, pv_ref, pwa_ref):
    first = pl.program_id(2) == 0
    n_batch, tb = r_ref.shape[0], r_ref.shape[1]
    n_groups = r_ref.shape[2] // GW
    n_chunks = tb // CH
    seqs = [(bb, gi) for bb in range(n_batch) for gi in range(n_groups)]

    @pl.when(first)
    def _():
        for ref in (st_ref, pr_ref, pk_ref, pv_ref, pwa_ref):
            ref[...] = jnp.zeros_like(ref)

    wd_act, ad = [], []
    for bb in range(n_batch):
        wa = _shift_mix(wa_ref, pwa_ref, muwa_ref[...], bb, slice(0, LANE))
        wd_act.append(jnp.tanh(wa[:, :LORA]).astype(BF16))
        ad.append(wa[:, LORA:].astype(BF16))

    seg = (_div_pow2(_iota((GW, GW), 0), RWKV_HEAD_DIM) == _div_pow2(_iota((GW, GW), 1), RWKV_HEAD_DIM))
    segf = seg.astype(F32)
    segb = seg.astype(BF16)
    assert CH == RWKV_HEAD_DIM
    lane_s = _mod_pow2(_iota((CH, GW), 1), CH)
    row_t = _iota((CH, GW), 0)
    strict = lane_s < row_t
    incl = lane_s <= row_t
    eye = (lane_s == row_t).astype(F32)
    ti_r, ti_c = _iota((tb, tb), 0), _iota((tb, tb), 1)
    trib = ((ti_c <= ti_r) & (_div_pow2(ti_c, CH) == _div_pow2(ti_r, CH))).astype(BF16)

    def bd(x):
        xb = x.astype(BF16)
        return jnp.concatenate([xb] * HPG, axis=0) * segb

    def prep(bb, gi):
        lanes = slice(gi * GW, (gi + 1) * GW)
        vec = vec_ref[:, lanes]
        mu_r, mu_k, mu_v = vec[0:1], vec[1:2], vec[2:3]
        w0, a0, k_k, k_a = vec[3:4], vec[4:5], vec[5:6], vec[6:7]
        r = _shift_mix(r_ref, pr_ref, mu_r, bb, lanes)
        k = _shift_mix(k_ref, pk_ref, mu_k, bb, lanes)
        v = _shift_mix(v_ref, pv_ref, mu_v, bb, lanes)
        w_lora = _mm(wd_act[bb], w2_ref[:, lanes])
        a_lora = _mm(ad[bb], a2_ref[:, lanes])
        kk = k * k_k
        kk_ss = _mm((kk * kk).astype(BF16), segb)
        yield
        lw = jax.nn.sigmoid(w0 + w_lora) * (-math.exp(-0.5) * LOG2E)
        cum = _mm_split_rhs(trib, lw)
        yield
        a_sig = jax.nn.sigmoid(a0 + a_lora)
        kk = kk * lax.rsqrt(jnp.maximum(kk_ss, 1e-24))
        k = k * (1.0 + (a_sig - 1.0) * k_a)
        return dict(r=r, k=k, v=v, a=-kk, b=kk * a_sig, lw=lw, cum=cum, vec=vec, lanes=lanes, bb=bb)

    groups = _interleave([prep(bb, gi) for bb, gi in seqs])

    def chunk_local(g, c):
        ts = slice(c * CH, (c + 1) * CH)
        rc, kc, vc, ac, bc, lwc, cum = (g[n][ts] for n in ("r", "k", "v", "a", "b", "lw", "cum"))
        tot = cum[CH - 1:CH]
        e_out = jnp.exp2(-cum)
        e_end = jnp.exp2(tot - cum)
        r_t = rc * jnp.exp2(cum)
        a_t = ac * jnp.exp2(cum - lwc)
        lhs = jnp.concatenate([a_t, r_t], axis=0).astype(BF16)
        aa = _mm_nt(lhs, jnp.concatenate([bd(bc * e_out), bd(kc * e_out)], axis=0))
        yield
        a_ab = jnp.where(strict, aa[:CH, :GW], 0.0)
        a_ak = jnp.where(strict, aa[:CH, GW:], 0.0)
        a_rb = jnp.where(incl, aa[CH:, :GW], 0.0)
        a_rk = jnp.where(incl, aa[CH:, GW:], 0.0)
        t_inv = eye + a_ab
        mpow = _mm(a_ab.astype(BF16), bd(a_ab))
        av = _mm(a_ak.astype(BF16), bd(vc))
        yield
        for _ in range(int(math.log2(CH)) - 1):
            res = _mm(jnp.concatenate([t_inv, mpow], axis=0).astype(BF16), bd(mpow))
            yield
            t_inv = t_inv + res[:CH]
            mpow = res[CH:]
        wu = _mm(t_inv.astype(BF16), jnp.concatenate([bd(a_t), bd(av)], axis=1))
        yield
        return dict(
            lhs=jnp.concatenate([wu[:, :GW], r_t], axis=0).astype(BF16), u_loc=wu[:, GW:],
            a_r=jnp.concatenate([a_rb, a_rk], axis=1).astype(BF16), bdv=bd(vc), vc=vc,
            bk_end=jnp.concatenate([bc * e_end, kc * e_end], axis=0).astype(BF16), dec=jnp.exp2(tot))

    loc = _interleave([chunk_local(g, c) for g in groups for c in range(n_chunks)])

    def chain(q):
        ys = []
        g_state = st_ref[q]
        for c in range(n_chunks):
            lc = loc[q * n_chunks + c]
            x0 = _mm_nt(lc["lhs"], g_state.astype(BF16))
            yield
            u = x0[:CH] + lc["u_loc"]
            y_c = _mm(lc["a_r"], jnp.concatenate([bd(u), lc["bdv"]], axis=0))
            upd = _mm_tn(jnp.concatenate([u, lc["vc"]], axis=0).astype(BF16), lc["bk_end"])
            yield
            ys.append(x0[CH:] + y_c)
            g_state = g_state * lc["dec"] + upd * segf
        st_ref[q] = g_state
        return jnp.concatenate(ys, axis=0)

    ys = _interleave([chain(q) for q in range(len(seqs))])

    def finish(g, y):
        vec = g["vec"]
        ln_w, ln_b, r_k = vec[7:8], vec[8:9], vec[9:10]
        inv_n = 1.0 / RWKV_HEAD_DIM
        mean = _mm(y.astype(BF16), segb) * inv_n
        bonus = _mm((g["r"] * g["k"] * r_k).astype(BF16), segb) * g["v"]
        yield
        yc = y - mean
        var = _mm((yc * yc).astype(BF16), segb) * inv_n
        yield
        yn = yc * lax.rsqrt(var + RWKV_GN_EPS) * ln_w + ln_b
        zb = zb_ref[g["bb"], :, g["lanes"]]
        o_ref[g["bb"], :, g["lanes"]] = ((yn + bonus) * (zb * jax.nn.sigmoid(zb))).astype(o_ref.dtype)

    _interleave([finish(g, y) for g, y in zip(groups, ys)])


def _rwkv(proj3, vecs, mu_wa, w2, a2, tb, gps, bps):
    b, s, _ = proj3.shape
    assert b % bps == 0, (b, bps)
    gw = gps * GW
    ng = RWKV_WIDTH // gw

    def col(off):
        return pl.BlockSpec((bps, tb, gw), lambda bi, g, ti, off=off: (bi, ti, off // gw + g))

    in_specs = [
        col(P_RKV), col(P_RKV + RWKV_WIDTH), col(P_RKV + 2 * RWKV_WIDTH),
        pl.BlockSpec((bps, tb, LANE), lambda bi, g, ti: (bi, ti, P_WDAD // LANE)),
        col(P_ZB),
        pl.BlockSpec((vecs.shape[0], gw), lambda bi, g, ti: (0, g)),
        pl.BlockSpec((1, LANE), lambda bi, g, ti: (0, 0)),
        pl.BlockSpec((LORA, gw), lambda bi, g, ti: (0, g)),
        pl.BlockSpec((LORA, gw), lambda bi, g, ti: (0, g)),
    ]
    return pl.pallas_call(
        _rwkv_kernel,
        grid=(b // bps, ng, s // tb),
        in_specs=in_specs,
        out_specs=pl.BlockSpec((bps, tb, gw), lambda bi, g, ti: (bi, ti, g)),
        out_shape=jax.ShapeDtypeStruct((b, s, RWKV_WIDTH), BF16),
        scratch_shapes=[pltpu.VMEM((bps * gps, GW, GW), F32), pltpu.VMEM((bps, gw), F32),
                        pltpu.VMEM((bps, gw), F32), pltpu.VMEM((bps, gw), F32), pltpu.VMEM((bps, LANE), F32)],
        compiler_params=pltpu.CompilerParams(
            dimension_semantics=("parallel", "parallel", "arbitrary")),
        name="rwkv",
    )(proj3, proj3, proj3, proj3, proj3, vecs, mu_wa, w2, a2)


def _outproj_kernel(ma_ref, mb_ref, wa_ref, wb_ref, x_ref, g_ref, o_ref):
    y = _mm(ma_ref[...], wa_ref[...]) + _mm(mb_ref[...], wb_ref[...])
    ms = jnp.mean(y * y, axis=-1, keepdims=True)
    o_ref[...] = x_ref[...] + y * lax.rsqrt(ms + NORM_EPS) * g_ref[...]


def _outproj(mix_a, mix_b, w_a, w_b, x2, g, tm):
    m, d = x2.shape
    ka, kb = mix_a.shape[1], mix_b.shape[1]
    return pl.pallas_call(
        _outproj_kernel,
        grid=(m // tm,),
        in_specs=[
            pl.BlockSpec((tm, ka), lambda i: (i, 0)),
            pl.BlockSpec((tm, kb), lambda i: (i, 0)),
            pl.BlockSpec((ka, d), lambda i: (0, 0)),
            pl.BlockSpec((kb, d), lambda i: (0, 0)),
            pl.BlockSpec((tm, d), lambda i: (i, 0)),
            pl.BlockSpec((1, d), lambda i: (0, 0)),
        ],
        out_specs=pl.BlockSpec((tm, d), lambda i: (i, 0)),
        out_shape=jax.ShapeDtypeStruct((m, d), F32),
        compiler_params=pltpu.CompilerParams(
            dimension_semantics=("parallel",), vmem_limit_bytes=VMEM_LIMIT_BYTES),
        name="outproj",
    )(mix_a, mix_b, w_a, w_b, x2, g)


_W_SEGMENTS = (
    (P_Q, R_Q, NSA_WIDTH),
    (P_RKV, R_FEAT, 3 * RWKV_WIDTH),
    (P_ZB, R_ZB, RWKV_WIDTH),
    (P_ZA, R_ZA, NSA_WIDTH),
    (P_KV, R_KV, 6 * NSA_KV_HEADS * HEAD_DIM),
    (P_WDAD, R_FEAT + 3 * RWKV_WIDTH, 2 * LORA),
    (P_GATE, R_GATE, LANE),
)


def _relayout_kernel(w_ref, o_ref):
    for dst, src, width in _W_SEGMENTS:
        o_ref[:, dst:dst + width] = w_ref[:, src:src + width].astype(BF16)
    used = P_GATE + LANE
    o_ref[:, used:] = jnp.zeros((o_ref.shape[0], NP - used), BF16)


def _permute_w_in(w, rows):
    d, n = w.shape
    return pl.pallas_call(
        _relayout_kernel,
        grid=(d // rows,),
        in_specs=[pl.BlockSpec((rows, n), lambda i: (i, 0))],
        out_specs=pl.BlockSpec((rows, NP), lambda i: (i, 0)),
        out_shape=jax.ShapeDtypeStruct((d, NP), BF16),
        compiler_params=pltpu.CompilerParams(dimension_semantics=("parallel",), allow_input_fusion=[True]),
        name="relayout",
    )(w)


def _block(x, pre_norm_g, w_in, rel_bias_table, cmp_pos_k, cmp_pos_v, cmp_k_w1, cmp_k_w2, cmp_v_w1,
           cmp_v_w2, rwkv_mu, rwkv_w0, rwkv_w2, rwkv_a0, rwkv_a2, rwkv_k_k, rwkv_k_a, rwkv_r_k,
           rwkv_ln_w, rwkv_ln_b, w_out, post_norm_g):
    b, s, d = x.shape
    assert w_in.shape == (d, R_END) and w_out.shape == (NSA_WIDTH + RWKV_WIDTH, d), (w_in.shape, w_out.shape)
    assert s % (4 * TQ) == 0 and s >= WINDOW + TQ, s
    x2 = x.reshape(b * s, d)
    proj = _inproj(x2, pre_norm_g.reshape(1, d), _permute_w_in(w_in.astype(BF16), RELAYOUT_ROWS),
                   min(INPROJ_TM, b * s), INPROJ_TN)
    proj3 = proj.reshape(b, s, NP)

    bias_c, toep = _bias(rel_bias_table.reshape(-1), s)
    half = CMP_STRIDE * HEAD_DIM
    mix_a = _nsa(proj3, bias_c, toep,
                 cmp_pos_k.reshape(2, half), cmp_k_w1.astype(BF16), cmp_k_w2.astype(BF16),
                 cmp_pos_v.reshape(2, half), cmp_v_w1.astype(BF16), cmp_v_w2.astype(BF16))

    w3 = 3 * RWKV_WIDTH
    vec_rows = [rwkv_mu[:RWKV_WIDTH], rwkv_mu[RWKV_WIDTH:2 * RWKV_WIDTH], rwkv_mu[2 * RWKV_WIDTH:w3],
                rwkv_w0, rwkv_a0, rwkv_k_k, rwkv_k_a, rwkv_ln_w, rwkv_ln_b, rwkv_r_k.reshape(-1)]
    vecs = jnp.stack(vec_rows + [jnp.zeros_like(rwkv_w0)] * (16 - len(vec_rows)), axis=0)
    mix_b = _rwkv(proj3, vecs, rwkv_mu[w3:].reshape(1, 2 * LORA), rwkv_w2.astype(BF16),
                  rwkv_a2.astype(BF16), min(RWKV_SLAB, s), RWKV_GROUPS_PER_STEP, math.gcd(RWKV_BATCH_PER_STEP, b))

    w_o = w_out.astype(BF16)
    out = _outproj(mix_a.reshape(b * s, NSA_WIDTH), mix_b.reshape(b * s, RWKV_WIDTH),
                   w_o[:NSA_WIDTH], w_o[NSA_WIDTH:], x2, post_norm_g.reshape(1, d), min(OUTPROJ_TM, b * s))
    return out.reshape(b, s, d)


def kernel(x, pre_norm_g, w_in, rel_bias_table, cmp_pos_k, cmp_pos_v, cmp_k_w1, cmp_k_w2, cmp_v_w1,
           cmp_v_w2, rwkv_mu, rwkv_w0, rwkv_w2, rwkv_a0, rwkv_a2, rwkv_k_k, rwkv_k_a, rwkv_r_k,
           rwkv_ln_w, rwkv_ln_b, w_out, post_norm_g):
    h = x
    for l in range(pre_norm_g.shape[0]):
        h = _block(h, pre_norm_g[l], w_in[l], rel_bias_table, cmp_pos_k[l], cmp_pos_v[l], cmp_k_w1[l],
                   cmp_k_w2[l], cmp_v_w1[l], cmp_v_w2[l], rwkv_mu[l], rwkv_w0[l], rwkv_w2[l],
                   rwkv_a0[l], rwkv_a2[l], rwkv_k_k[l], rwkv_k_a[l], rwkv_r_k[l], rwkv_ln_w[l],
                   rwkv_ln_b[l], w_out[l], post_norm_g[l])
    return h
```
